```python
import math
import jax
import jax.numpy as jnp
from jax import lax
import numpy as np

D_MODEL = 1024
BATCH = 32
SEQ = 256
DEPTH = 4
DEC_BATCH = 2
DEC_SEQ = 4096
PAST_LEN = 256

GRID_W = 64
N_MIXERS = 4
MOD_CHUNKS = 6
RMS_EPS = 1e-6

S5_GROUP = 16
S5_GROUPS = D_MODEL // S5_GROUP
S5_STATE = 64
S5_DT_MIN = 1e-3
S5_DT_MAX = 1e-1

HY_ORDER = 2
HY_BANDS = 16
HY_EMB = 1 + 2 * HY_BANDS
HY_FILTER_HIDDEN = 64
HY_TARGET = 1e-2
HY_FAST_PCT = 0.3
HY_SLOW_PCT = 1.5

GLA_HEADS = 4
GLA_DK = D_MODEL // 2 // GLA_HEADS
GLA_DV = D_MODEL // GLA_HEADS
GLA_GATE_RANK = 16
GLA_TAU = 16.0
GLA_CHUNK = 64

RET_HEADS = 4
RET_DK = D_MODEL // RET_HEADS
RET_DV = 2 * D_MODEL // RET_HEADS
RET_CHUNK = 64
ROPE_BASE = 10000.0

FFN_DIM = 2816
N_EXPERTS = 8
TOP_K = 2
EXPERT_DIM = 3584
MOE_BLOCK = 128

kernel_name = 'hybrid_flow_s5_hyena_gla_retnet_step'


def rmsnorm(x):
    xf = x.astype(jnp.float32)
    return (xf * lax.rsqrt(jnp.mean(xf * xf, axis=-1, keepdims=True) + RMS_EPS)).astype(x.dtype)


def modulation(cond, w, b):
    m = jax.nn.silu(cond) @ w + b
    return jnp.split(m[..., None, :], MOD_CHUNKS, axis=-1)


def modulate(x, shift, scale):
    return rmsnorm(x) * (1 + scale) + shift


def head_rmsnorm(o, g):
    return o * lax.rsqrt(jnp.mean(o * o, axis=-1, keepdims=True) + RMS_EPS) * g.astype(jnp.float32)


def head_layernorm(o, g):
    oc = o - jnp.mean(o, axis=-1, keepdims=True)
    return oc * lax.rsqrt(jnp.mean(oc * oc, axis=-1, keepdims=True) + RMS_EPS) * g.astype(jnp.float32)


def _flip(t):
    return jnp.flip(t, axis=1)


def short_conv3(x, w, b):
    xp = jnp.pad(x, ((0, 0), (1, 1), (0, 0)))
    return xp[:, :-2] * w[0] + xp[:, 1:-1] * w[1] + xp[:, 2:] * w[2] + b


def _s5_combine(left, right):
    a_l, b_l = left
    a_r, b_r = right
    return a_r * a_l, a_r * b_l + b_r


def s5_mixer(u, h0_re, h0_im, a_re, a_im, log_dt, b_re, b_im, c_re, c_im, d_skip, glu_w):
    f32 = jnp.float32
    bsz, L, _ = u.shape
    uf = u.astype(f32)
    uc = uf.reshape(bsz, L, S5_GROUPS, S5_GROUP).astype(jnp.complex64)
    y = uf * d_skip.astype(f32)
    finals = []
    for dr in range(2):
        reverse = dr == 1
        a = lax.complex(a_re[dr].astype(f32), a_im[dr].astype(f32))
        adt = a * jnp.exp(log_dt[dr].astype(f32))[:, None]
        a_bar = jnp.exp(adt)
        bmat = lax.complex(b_re[dr].astype(f32), b_im[dr].astype(f32))
        b_bar = ((a_bar - 1.0) / a)[..., None] * bmat
        bu = jnp.einsum('gpq,blgq->blgp', b_bar, uc)
        _, h = lax.associative_scan(_s5_combine, (jnp.broadcast_to(a_bar, bu.shape), bu),
                                    reverse=reverse, axis=1)
        if h0_re is not None:
            steps = jnp.arange(1, L + 1, dtype=f32)
            if reverse:
                steps = steps[::-1]
            h0 = lax.complex(h0_re[:, dr].astype(f32), h0_im[:, dr].astype(f32))
            h = h + jnp.exp(steps[:, None, None] * adt)[None] * h0[:, None]
        cmat = lax.complex(c_re[dr].astype(f32), c_im[dr].astype(f32))
        y = y + jnp.real(jnp.einsum('gqp,blgp->blgq', cmat, h)).reshape(bsz, L, D_MODEL)
        finals.append(h[:, 0] if reverse else h[:, -1])
    g = jax.nn.gelu(y)
    val, gate = jnp.split(g @ glu_w.astype(f32), 2, axis=-1)
    out = val * jax.nn.sigmoid(gate)
    fin = jnp.stack(finals, axis=1)
    return out.astype(u.dtype), jnp.real(fin), jnp.imag(fin)


def hyena_filters(L, f_w1, f_b1, f_w2, f_b2, f_w3, f_freq):
    f32 = jnp.float32
    t = jnp.linspace(0.0, 1.0, L, dtype=f32)[:, None]
    w = 2.0 * math.pi * jnp.arange(L, dtype=f32)[:, None] / L
    f = jnp.linspace(1e-4, HY_BANDS - 1, HY_BANDS, dtype=f32)[None, :]
    feats = jnp.concatenate([t, jnp.cos(f * w), -jnp.sin(f * w)], axis=-1)
    z = jnp.sin(f_freq[0] * (feats @ f_w1 + f_b1))
    z = jnp.sin(f_freq[1] * (z @ f_w2 + f_b2))
    h = (z @ f_w3).astype(f32).reshape(L, HY_ORDER, 2, D_MODEL)
    max_decay = math.log(HY_TARGET) / HY_FAST_PCT
    min_decay = math.log(HY_TARGET) / HY_SLOW_PCT
    deltas = jnp.abs(jnp.linspace(min_decay, max_decay, D_MODEL, dtype=f32))
    h = h * jnp.exp(-t[:, :, None, None] * deltas)
    k = jnp.concatenate([h[:, :, 0], jnp.zeros((1, HY_ORDER, D_MODEL), f32), h[:0:-1, :, 1]], axis=0)
    return k / jnp.sum(jnp.abs(k), axis=0, keepdims=True)


def fft_long_conv(z, k, skip):
    L = z.shape[1]
    zf = jnp.fft.rfft(z.astype(jnp.float32), n=2 * L, axis=1)
    kf = jnp.fft.rfft(k, n=2 * L, axis=0)
    y = jnp.fft.irfft(zf * kf[None], n=2 * L, axis=1)[:, :L]
    return (y + z.astype(jnp.float32) * skip.astype(jnp.float32)).astype(z.dtype)


def hyena_mixer(u, w_in, b_in, short_w, short_b, f_w1, f_b1, f_w2, f_b2, f_w3, f_freq, skip, w_out, b_out):
    L = u.shape[1]
    proj = short_conv3(u @ w_in + b_in, short_w, short_b)
    x1, x2, v = jnp.split(proj, 3, axis=-1)
    k = hyena_filters(L, f_w1, f_b1, f_w2, f_b2, f_w3, f_freq)
    z = v
    for n, gate in enumerate((x1, x2)):
        z = gate * fft_long_conv(z, k[:, n], skip[n])
    return z @ w_out + b_out


def gla_chunk_scan(q, k, v, log_g, s0):
    bsz, L, H, K = q.shape
    V = v.shape[-1]
    C = GLA_CHUNK
    n = L // C
    qc, kc, vc, gc = (t.reshape(bsz, n, C, H, t.shape[-1]) for t in (q, k, v, log_g))
    b = jnp.cumsum(gc, axis=2)
    b_tot = b[:, :, -1]
    q_in = qc * jnp.exp(b)
    k_in = kc * jnp.exp(-b)
    k_out = kc * jnp.exp(b_tot[:, :, None] - b)
    mask = jnp.tril(jnp.ones((C, C), dtype=bool))
    scores = jnp.where(mask, jnp.einsum('bnthk,bnshk->bnhts', q_in, k_in), 0.0)
    o_intra = jnp.einsum('bnhts,bnshv->bnthv', scores, vc)
    kv = jnp.einsum('bnshk,bnshv->bnhkv', k_out, vc)
    if s0 is None:
        s0 = jnp.zeros((bsz, H, K, V), jnp.float32)

    def step(s, inp):
        kv_n, dec_n = inp
        return jnp.exp(dec_n)[..., None] * s + kv_n, s

    s_fin, s_start = lax.scan(step, s0, (jnp.moveaxis(kv, 1, 0), jnp.moveaxis(b_tot, 1, 0)))
    o_inter = jnp.einsum('bnthk,nbhkv->bnthv', q_in, s_start)
    return (o_intra + o_inter).reshape(bsz, L, H, V), s_fin


def gla_mixer(u, s0, w_in, gate_w1, gate_w2, gate_b, norm_g, w_out):
    bsz, L, _ = u.shape
    hk, hv = GLA_HEADS * GLA_DK, GLA_HEADS * GLA_DV
    q, k, v, r = jnp.split(u @ w_in, [hk, 2 * hk, 2 * hk + hv], axis=-1)
    heads = lambda t, d: t.astype(jnp.float32).reshape(bsz, L, GLA_HEADS, d)
    q = heads(q, GLA_DK) * GLA_DK ** -0.5
    k = heads(k, GLA_DK)
    v = heads(v, GLA_DV)
    o_sum, finals = 0.0, []
    for dr in range(2):
        log_g = jax.nn.log_sigmoid(heads((u @ gate_w1[dr]) @ gate_w2[dr] + gate_b[dr], GLA_DK)) / GLA_TAU
        init = None if s0 is None else s0[:, dr].astype(jnp.float32)
        if dr == 0:
            o, s = gla_chunk_scan(q, k, v, log_g, init)
        else:
            o, s = gla_chunk_scan(_flip(q), _flip(k), _flip(v), _flip(log_g), init)
            o = _flip(o)
        o_sum = o_sum + o
        finals.append(s)
    o = head_rmsnorm(o_sum, norm_g).reshape(bsz, L, hv) * jax.nn.silu(r.astype(jnp.float32))
    return o.astype(u.dtype) @ w_out, jnp.stack(finals, axis=1)


def rope_2d(x):
    L = x.shape[1]
    n_rows = L // GRID_W
    rows = jnp.repeat(jnp.arange(n_rows, dtype=jnp.float32), GRID_W)
    cols = jnp.tile(jnp.arange(GRID_W, dtype=jnp.float32), n_rows)
    half = x.shape[-1] // 2
    nf = half // 2
    inv = ROPE_BASE ** (-jnp.arange(nf, dtype=jnp.float32) / nf)

    def rot(xp, pos):
        ang = pos[:, None] * inv[None, :]
        cos, sin = jnp.cos(ang)[None, :, None, :], jnp.sin(ang)[None, :, None, :]
        x1, x2 = xp[..., :nf], xp[..., nf:]
        return jnp.concatenate([x1 * cos - x2 * sin, x1 * sin + x2 * cos], axis=-1)

    return jnp.concatenate([rot(x[..., :half], rows), rot(x[..., half:], cols)], axis=-1)


def retention_chunk_scan(q, k, v, log_gamma, s0):
    bsz, L, H, K = q.shape
    V = v.shape[-1]
    C = RET_CHUNK
    n = L // C
    lg = log_gamma.astype(jnp.float32)[:, None]
    idx = jnp.arange(C, dtype=jnp.float32)
    lag = idx[:, None] - idx[None, :]
    decay = jnp.where(lag >= 0, jnp.exp(jnp.maximum(lag, 0.0)[None] * lg[:, :, None]), 0.0)
    q_dec = jnp.exp((idx + 1.0)[None] * lg)
    k_dec = jnp.exp((C - 1.0 - idx)[None] * lg)
    chunk_dec = jnp.exp(C * lg)[None, :, :, None]
    qc, kc, vc = (t.reshape(bsz, n, C, H, t.shape[-1]) for t in (q, k, v))
    scores = jnp.einsum('bnthk,bnshk->bnhts', qc, kc) * decay
    o_intra = jnp.einsum('bnhts,bnshv->bnthv', scores, vc)
    kv = jnp.einsum('bnshk,bnshv,hs->bnhkv', kc, vc, k_dec)
    if s0 is None:
        s0 = jnp.zeros((bsz, H, K, V), jnp.float32)

    def step(s, kv_n):
        return chunk_dec * s + kv_n, s

    s_fin, s_start = lax.scan(step, s0, jnp.moveaxis(kv, 1, 0))
    o_inter = jnp.einsum('bnthk,nbhkv,ht->bnthv', qc, s_start, q_dec)
    return (o_intra + o_inter).reshape(bsz, L, H, V), s_fin


def retnet_mixer(u, s0, grid, w_in, log_decay, norm_g, w_out):
    bsz, L, _ = u.shape
    hk, hv = RET_HEADS * RET_DK, RET_HEADS * RET_DV
    q, k, v, g = jnp.split(u @ w_in, [hk, 2 * hk, 2 * hk + hv], axis=-1)
    heads = lambda t, d: t.astype(jnp.float32).reshape(bsz, L, RET_HEADS, d)
    q = heads(q, RET_DK)
    k = heads(k, RET_DK) * RET_DK ** -0.5
    v = heads(v, RET_DV)
    if grid:
        q, k = rope_2d(q), rope_2d(k)
    o_sum, finals = 0.0, []
    for dr in range(2):
        init = None if s0 is None else s0[:, dr].astype(jnp.float32)
        if dr == 0:
            o, s = retention_chunk_scan(q, k, v, log_decay[dr], init)
        else:
            o, s = retention_chunk_scan(_flip(q), _flip(k), _flip(v), log_decay[dr], init)
            o = _flip(o)
        o_sum = o_sum + o
        finals.append(s)
    o = head_layernorm(o_sum, norm_g).reshape(bsz, L, hv) * jax.nn.silu(g.astype(jnp.float32))
    return o.astype(u.dtype) @ w_out, jnp.stack(finals, axis=1)


def swiglu(h, w_in, w_out):
    a, b = jnp.split(h @ w_in, 2, axis=-1)
    return (jax.nn.silu(a) * b) @ w_out


def moe_swiglu(h, router_w, w_in, w_out):
    shp = h.shape
    x = h.reshape(-1, D_MODEL)
    T = x.shape[0]
    A = T * TOP_K
    logits = (x @ router_w).astype(jnp.float32)
    top_logit, top_e = lax.top_k(logits, TOP_K)
    gates = jax.nn.softmax(top_logit, axis=-1)
    flat_e = top_e.reshape(-1)
    flat_tok = jnp.repeat(jnp.arange(T, dtype=jnp.int32), TOP_K)
    order = jnp.argsort(flat_e)
    e_sorted = flat_e[order]
    tok_sorted = flat_tok[order]
    gate_sorted = gates.reshape(-1)[order]
    counts = jnp.bincount(flat_e, length=N_EXPERTS)
    padded = (counts + MOE_BLOCK - 1) // MOE_BLOCK * MOE_BLOCK
    pad_end = jnp.cumsum(padded)
    pad_start = pad_end - padded
    start = jnp.cumsum(counts) - counts
    dest = pad_start[e_sorted] + jnp.arange(A) - start[e_sorted]
    n_blocks = -(-(A + N_EXPERTS * (MOE_BLOCK - 1)) // MOE_BLOCK)
    row_tok = jnp.zeros((n_blocks * MOE_BLOCK,), jnp.int32).at[dest].set(tok_sorted)
    block_start = jnp.arange(n_blocks) * MOE_BLOCK
    block_e = jnp.minimum(jnp.searchsorted(pad_end, block_start, side='right'), N_EXPERTS - 1)
    xb = x[row_tok].reshape(n_blocks, MOE_BLOCK, D_MODEL)
    yb = lax.map(lambda args: swiglu(args[0], w_in[args[1]], w_out[args[1]]), (xb, block_e))
    yb = yb.reshape(n_blocks * MOE_BLOCK, D_MODEL)
    y = jnp.zeros_like(x).at[tok_sorted].add(yb[dest] * gate_sorted[:, None].astype(x.dtype))
    return y.reshape(shp)


def setup_inputs(seed: int = 0) -> dict:
    key = jax.random.key(seed)
    keys = jax.random.split(key, 80)
    cnt = [0]

    def nk():
        cnt[0] += 1
        return keys[cnt[0] - 1]

    def nrm(shape, scale):
        return jax.random.normal(nk(), shape, jnp.float32) * scale

    def gain(shape):
        return 1.0 + nrm(shape, 0.01)

    D = D_MODEL
    G, P, Q = S5_GROUPS, S5_STATE, S5_GROUP
    gla_k, gla_v = GLA_HEADS * GLA_DK, GLA_HEADS * GLA_DV
    ret_k, ret_v = RET_HEADS * RET_DK, RET_HEADS * RET_DV
    inp = {}
    inp['x_prompt'] = nrm((BATCH, SEQ, D), 1.0)
    inp['x_sample'] = nrm((DEC_BATCH, DEC_SEQ, D), 1.0)
    inp['c'] = nrm((DEC_BATCH, D), 1.0)
    inp['state_l0_s5_re'] = nrm((DEC_BATCH, 2, G, P), 0.5)
    inp['state_l0_s5_im'] = nrm((DEC_BATCH, 2, G, P), 0.5)
    inp['state_l2_gla'] = nrm((DEC_BATCH, 2, GLA_HEADS, GLA_DK, GLA_DV), 1.0)
    inp['state_l3_ret'] = nrm((DEC_BATCH, 2, RET_HEADS, RET_DK, RET_DV), 1.0)
    inp['c_ctx'] = nrm((D,), 1.0)
    inp['l0_mod_w'] = nrm((D, MOD_CHUNKS * D), D ** -0.5)
    inp['l0_mod_b'] = nrm((MOD_CHUNKS * D,), 0.01)
    inp['l0_s5_a_re'] = -0.5 + nrm((2, G, P), 0.01)
    inp['l0_s5_a_im'] = math.pi * jnp.arange(P, dtype=jnp.float32) + nrm((2, G, P), 0.01)
    inp['l0_s5_log_dt'] = jax.random.uniform(nk(), (2, G), jnp.float32, math.log(S5_DT_MIN), math.log(S5_DT_MAX))
    inp['l0_s5_b_re'] = nrm((2, G, P, Q), (2 * Q) ** -0.5)
    inp['l0_s5_b_im'] = nrm((2, G, P, Q), (2 * Q) ** -0.5)
    inp['l0_s5_c_re'] = nrm((2, G, Q, P), (2 * P) ** -0.5)
    inp['l0_s5_c_im'] = nrm((2, G, Q, P), (2 * P) ** -0.5)
    inp['l0_s5_d'] = nrm((D,), 1.0)
    inp['l0_s5_glu_w'] = nrm((D, 2 * D), D ** -0.5)
    inp['l0_ffn_w_in'] = nrm((D, 2 * FFN_DIM), D ** -0.5)
    inp['l0_ffn_w_out'] = nrm((FFN_DIM, D), FFN_DIM ** -0.5)
    inp['l1_mod_w'] = nrm((D, MOD_CHUNKS * D), D ** -0.5)
    inp['l1_mod_b'] = nrm((MOD_CHUNKS * D,), 0.01)
    inp['l1_hy_w_in'] = nrm((D, 3 * D), D ** -0.5)
    inp['l1_hy_b_in'] = nrm((3 * D,), 0.01)
    inp['l1_hy_short_w'] = nrm((3, 3 * D), 3 ** -0.5)
    inp['l1_hy_short_b'] = nrm((3 * D,), 0.01)
    inp['l1_hy_f_w1'] = nrm((HY_EMB, HY_FILTER_HIDDEN), HY_EMB ** -0.5)
    inp['l1_hy_f_b1'] = nrm((HY_FILTER_HIDDEN,), 0.02)
    inp['l1_hy_f_w2'] = nrm((HY_FILTER_HIDDEN, HY_FILTER_HIDDEN), HY_FILTER_HIDDEN ** -0.5)
    inp['l1_hy_f_b2'] = nrm((HY_FILTER_HIDDEN,), 0.02)
    inp['l1_hy_f_w3'] = nrm((HY_FILTER_HIDDEN, HY_ORDER * 2 * D), HY_FILTER_HIDDEN ** -0.5)
    inp['l1_hy_f_freq'] = gain((2, HY_FILTER_HIDDEN))
    inp['l1_hy_skip'] = nrm((HY_ORDER, D), 1.0)
    inp['l1_hy_w_out'] = nrm((D, D), D ** -0.5)
    inp['l1_hy_b_out'] = nrm((D,), 0.01)
    inp['l1_moe_router'] = nrm((D, N_EXPERTS), D ** -0.5)
    inp['l1_moe_w_in'] = nrm((N_EXPERTS, D, 2 * EXPERT_DIM), D ** -0.5)
    inp['l1_moe_w_out'] = nrm((N_EXPERTS, EXPERT_DIM, D), EXPERT_DIM ** -0.5)
    inp['l2_mod_w'] = nrm((D, MOD_CHUNKS * D), D ** -0.5)
    inp['l2_mod_b'] = nrm((MOD_CHUNKS * D,), 0.01)
    inp['l2_gla_w_in'] = nrm((D, 2 * gla_k + 2 * gla_v), D ** -0.5)
    inp['l2_gla_gate_w1'] = nrm((2, D, GLA_GATE_RANK), D ** -0.5)
    inp['l2_gla_gate_w2'] = nrm((2, GLA_GATE_RANK, gla_k), GLA_GATE_RANK ** -0.5)
    inp['l2_gla_gate_b'] = nrm((2, gla_k), 0.1)
    inp['l2_gla_norm_g'] = gain((GLA_DV,))
    inp['l2_gla_w_out'] = nrm((gla_v, D), gla_v ** -0.5)
    inp['l2_ffn_w_in'] = nrm((D, 2 * FFN_DIM), D ** -0.5)
    inp['l2_ffn_w_out'] = nrm((FFN_DIM, D), FFN_DIM ** -0.5)
    inp['l3_mod_w'] = nrm((D, MOD_CHUNKS * D), D ** -0.5)
    inp['l3_mod_b'] = nrm((MOD_CHUNKS * D,), 0.01)
    inp['l3_ret_w_in'] = nrm((D, 2 * ret_k + 2 * ret_v), D ** -0.5)
    base_decay = jnp.log(1.0 - 2.0 ** (-5.0 - jnp.arange(RET_HEADS, dtype=jnp.float32)))
    inp['l3_ret_log_decay'] = base_decay[None, :] * (1.0 + nrm((2, RET_HEADS), 0.05))
    inp['l3_ret_norm_g'] = gain((RET_DV,))
    inp['l3_ret_w_out'] = nrm((ret_v, D), ret_v ** -0.5)
    inp['l3_moe_router'] = nrm((D, N_EXPERTS), D ** -0.5)
    inp['l3_moe_w_in'] = nrm((N_EXPERTS, D, 2 * EXPERT_DIM), D ** -0.5)
    inp['l3_moe_w_out'] = nrm((N_EXPERTS, EXPERT_DIM, D), EXPERT_DIM ** -0.5)
    inp['final_norm_g'] = gain((D,))
    return inp


def reference(x_prompt, x_sample, c, state_l0_s5_re, state_l0_s5_im, state_l2_gla, state_l3_ret, c_ctx,
              l0_mod_w, l0_mod_b, l0_s5_a_re, l0_s5_a_im, l0_s5_log_dt, l0_s5_b_re, l0_s5_b_im,
              l0_s5_c_re, l0_s5_c_im, l0_s5_d, l0_s5_glu_w, l0_ffn_w_in, l0_ffn_w_out,
              l1_mod_w, l1_mod_b, l1_hy_w_in, l1_hy_b_in, l1_hy_short_w, l1_hy_short_b,
              l1_hy_f_w1, l1_hy_f_b1, l1_hy_f_w2, l1_hy_f_b2, l1_hy_f_w3, l1_hy_f_freq, l1_hy_skip,
              l1_hy_w_out, l1_hy_b_out, l1_moe_router, l1_moe_w_in, l1_moe_w_out,
              l2_mod_w, l2_mod_b, l2_gla_w_in, l2_gla_gate_w1, l2_gla_gate_w2, l2_gla_gate_b,
              l2_gla_norm_g, l2_gla_w_out, l2_ffn_w_in, l2_ffn_w_out,
              l3_mod_w, l3_mod_b, l3_ret_w_in, l3_ret_log_decay, l3_ret_norm_g, l3_ret_w_out,
              l3_moe_router, l3_moe_w_in, l3_moe_w_out,
              final_norm_g):
    mod_w = (l0_mod_w, l1_mod_w, l2_mod_w, l3_mod_w)
    mod_b = (l0_mod_b, l1_mod_b, l2_mod_b, l3_mod_b)
    channel_params = ((l0_ffn_w_in, l0_ffn_w_out),
                      (l1_moe_router, l1_moe_w_in, l1_moe_w_out),
                      (l2_ffn_w_in, l2_ffn_w_out),
                      (l3_moe_router, l3_moe_w_in, l3_moe_w_out))

    def mixer(i, h, latent):
        kind = i % N_MIXERS
        if kind == 0:
            out, s_re, s_im = s5_mixer(h, state_l0_s5_re if latent else None, state_l0_s5_im if latent else None,
                                       l0_s5_a_re, l0_s5_a_im, l0_s5_log_dt, l0_s5_b_re, l0_s5_b_im,
                                       l0_s5_c_re, l0_s5_c_im, l0_s5_d, l0_s5_glu_w)
            return out, [s_re, s_im]
        if kind == 1:
            out = hyena_mixer(h, l1_hy_w_in, l1_hy_b_in, l1_hy_short_w, l1_hy_short_b, l1_hy_f_w1, l1_hy_f_b1,
                              l1_hy_f_w2, l1_hy_f_b2, l1_hy_f_w3, l1_hy_f_freq, l1_hy_skip, l1_hy_w_out, l1_hy_b_out)
            return out, []
        if kind == 2:
            out, s = gla_mixer(h, state_l2_gla if latent else None, l2_gla_w_in, l2_gla_gate_w1, l2_gla_gate_w2,
                               l2_gla_gate_b, l2_gla_norm_g, l2_gla_w_out)
            return out, [s]
        out, s = retnet_mixer(h, state_l3_ret if latent else None, latent, l3_ret_w_in, l3_ret_log_decay,
                              l3_ret_norm_g, l3_ret_w_out)
        return out, [s]

    def channel(i, h):
        if i % 2 == 0:
            return swiglu(h, *channel_params[i])
        return moe_swiglu(h, *channel_params[i])

    def trunk(x, cond, latent):
        states = []
        for i in range(DEPTH):
            sh1, sc1, g1, sh2, sc2, g2 = modulation(cond, mod_w[i], mod_b[i])
            out, st = mixer(i, modulate(x, sh1, sc1), latent)
            x = x + g1 * out
            x = x + g2 * channel(i, modulate(x, sh2, sc2))
            states = states + st
        return rmsnorm(x) * final_norm_g, states

    y_prompt, ctx_states = trunk(x_prompt, c_ctx, False)
    new_l0_s5_re, new_l0_s5_im, new_l2_gla, new_l3_ret = ctx_states
    y_sample, _ = trunk(x_sample, c, True)
    return (y_prompt, y_sample, new_l0_s5_re, new_l0_s5_im, new_l2_gla, new_l3_ret)
```

```python
import functools
import math

import jax
import jax.numpy as jnp
import numpy as np
from jax import lax
from jax.experimental import pallas as pl
from jax.experimental.pallas import tpu as pltpu

F32 = jnp.float32
BF16 = jnp.bfloat16
HIGHEST = lax.Precision.HIGHEST

D = 1024
RMS_EPS = 1e-6
MOD_CHUNKS = 6
GRID_W = 64

S5_Q = 16
S5_G = D // S5_Q
S5_P = 64
S5_T = 16
S5_SCAN_ROWS = 64

HY_BANDS = 16
HY_TARGET = 1e-2
HY_FAST_PCT = 0.3
HY_SLOW_PCT = 1.5

GLA_H, GLA_DK, GLA_DV = 4, 128, 256
GLA_RANK = 16
GLA_TAU = 16.0
RET_H, RET_DK, RET_DV = 4, 256, 512
CHUNK = 64
ROPE_BASE = 10000.0

FFN_DIM = 2816
N_EXPERTS = 8
EXPERT_DIM = 3584

VMEM_LIMIT_V7X = 56 * 1024 * 1024


def _cp(*sem):
    return pltpu.CompilerParams(dimension_semantics=sem, vmem_limit_bytes=VMEM_LIMIT_V7X)


def _silu(x):
    return x * jax.nn.sigmoid(x)


def _modulate(x, shift, scale):
    ms = jnp.mean(x * x, axis=-1, keepdims=True)
    return x * lax.rsqrt(ms + RMS_EPS) * (1.0 + scale) + shift


class Layout:
    def __init__(self, n_prompt, l_prompt, n_sample, l_sample):
        self.B, self.L, self.NS, self.LS = n_prompt, l_prompt, n_sample, l_sample
        self.TP = n_prompt * l_prompt
        self.T = self.TP + n_sample * l_sample

    def tile(self, want):
        t = math.gcd(math.gcd(self.TP, self.LS), want)
        assert t % 8 == 0
        return t

    def group(self, row):
        return jnp.where(row < self.TP, 0, 1 + (row - self.TP) // self.LS)


def _mod_spec(lay, tm, chunk, ngrid):
    def imap(*ids):
        return (lay.group(ids[0] * tm) * MOD_CHUNKS + chunk, 0, 0)
    del ngrid
    return pl.BlockSpec((None, 1, D), imap)


def _mods_kernel(c_ref, w_ref, b_ref, o_ref):
    o_ref[...] = jnp.dot(_silu(c_ref[...]), w_ref[...], precision=HIGHEST,
                         preferred_element_type=F32) + b_ref[...]


def _mods(cond, w, b):
    n = MOD_CHUNKS * D
    tn = 1536
    out = pl.pallas_call(
        _mods_kernel,
        out_shape=jax.ShapeDtypeStruct((8, n), F32),
        grid=(n // tn,),
        in_specs=[pl.BlockSpec((8, D), lambda j: (0, 0)),
                  pl.BlockSpec((D, tn), lambda j: (0, j)),
                  pl.BlockSpec((1, tn), lambda j: (0, j))],
        out_specs=pl.BlockSpec((8, tn), lambda j: (0, j)),
        compiler_params=_cp("arbitrary"),
        name="mods",
    )(cond, w, b.reshape(1, n))
    return out.reshape(8 * MOD_CHUNKS, 1, D)


def _modulate_kernel(x_ref, sh_ref, sc_ref, o_ref):
    o_ref[...] = _modulate(x_ref[...], sh_ref[...], sc_ref[...]).astype(o_ref.dtype)


def _modulate_call(lay, x, mods, c_shift, c_scale, dtype):
    tm = lay.tile(512)
    return pl.pallas_call(
        _modulate_kernel,
        out_shape=jax.ShapeDtypeStruct((lay.T, D), dtype),
        grid=(lay.T // tm,),
        in_specs=[pl.BlockSpec((tm, D), lambda i: (i, 0)),
                  _mod_spec(lay, tm, c_shift, 1), _mod_spec(lay, tm, c_scale, 1)],
        out_specs=pl.BlockSpec((tm, D), lambda i: (i, 0)),
        compiler_params=_cp("arbitrary"),
        name="modulate",
    )(x, mods, mods)


def _proj_kernel(x_ref, sh_ref, sc_ref, w_ref, b_ref, o_ref, u_sc):
    @pl.when(pl.program_id(1) == 0)
    def _():
        u_sc[...] = _modulate(x_ref[...], sh_ref[...], sc_ref[...]).astype(BF16)

    acc = jnp.dot(u_sc[...], w_ref[...], preferred_element_type=F32) + b_ref[...]
    o_ref[...] = acc.astype(o_ref.dtype)


def _proj(lay, x, mods, c_shift, c_scale, w, b, tn, out_dtype=BF16):
    tm = lay.tile(1024)
    n = w.shape[1]
    assert n % tn == 0
    return pl.pallas_call(
        _proj_kernel,
        out_shape=jax.ShapeDtypeStruct((lay.T, n), out_dtype),
        grid=(lay.T // tm, n // tn),
        in_specs=[pl.BlockSpec((tm, D), lambda i, j: (i, 0)),
                  _mod_spec(lay, tm, c_shift, 2), _mod_spec(lay, tm, c_scale, 2),
                  pl.BlockSpec((D, tn), lambda i, j: (0, j)),
                  pl.BlockSpec((1, tn), lambda i, j: (0, j))],
        out_specs=pl.BlockSpec((tm, tn), lambda i, j: (i, j)),
        scratch_shapes=[pltpu.VMEM((tm, D), BF16)],
        compiler_params=_cp("arbitrary", "arbitrary"),
        name="proj",
    )(x, mods, mods, w, b)


def _ffn_kernel(nf, x_ref, sh_ref, sc_ref, g_ref, wa_ref, wb_ref, wo_ref, o_ref, u_sc, acc_sc):
    f = pl.program_id(1)

    @pl.when(f == 0)
    def _():
        u_sc[...] = _modulate(x_ref[...], sh_ref[...], sc_ref[...]).astype(BF16)
        acc_sc[...] = jnp.zeros_like(acc_sc)

    u = u_sc[...]
    a = jnp.dot(u, wa_ref[...], preferred_element_type=F32)
    b = jnp.dot(u, wb_ref[...], preferred_element_type=F32)
    h = (_silu(a) * b).astype(BF16)
    acc_sc[...] += jnp.dot(h, wo_ref[...], preferred_element_type=F32)

    @pl.when(f == nf - 1)
    def _():
        o_ref[...] = x_ref[...] + g_ref[...] * acc_sc[...]


def _ffn(lay, x, mods, w_in, w_out):
    tm = lay.tile(512)
    tf = 1408
    nf = FFN_DIM // tf
    return pl.pallas_call(
        functools.partial(_ffn_kernel, nf),
        out_shape=jax.ShapeDtypeStruct((lay.T, D), F32),
        grid=(lay.T // tm, nf),
        in_specs=[pl.BlockSpec((tm, D), lambda i, f: (i, 0)),
                  _mod_spec(lay, tm, 3, 2), _mod_spec(lay, tm, 4, 2), _mod_spec(lay, tm, 5, 2),
                  pl.BlockSpec((D, tf), lambda i, f: (0, f)),
                  pl.BlockSpec((D, tf), lambda i, f: (0, nf + f)),
                  pl.BlockSpec((tf, D), lambda i, f: (f, 0))],
        out_specs=pl.BlockSpec((tm, D), lambda i, f: (i, 0)),
        scratch_shapes=[pltpu.VMEM((tm, D), BF16), pltpu.VMEM((tm, D), F32)],
        compiler_params=_cp("arbitrary", "arbitrary"),
        name="ffn",
    )(x, mods, mods, mods, w_in, w_in, w_out)


def _s5_tables(a_re, a_im, log_dt, b_re, b_im, c_re, c_im, d_skip):
    T, G, P, Q = S5_T, S5_G, S5_P, S5_Q
    a = lax.complex(a_re, a_im)
    adt = a * jnp.exp(log_dt)[..., None]
    lam = jnp.exp(adt)
    bb = ((lam - 1.0) / a)[..., None] * lax.complex(b_re, b_im)
    cm = lax.complex(c_re, c_im)
    steps = jnp.arange(T + 1, dtype=F32)
    pw = jnp.exp(steps[None, :, None, None] * adt[:, None])
    kern = jnp.real(jnp.einsum('dgqp,djgp,dgpr->djgqr', cm, pw[:, :T], bb))
    s_i = jnp.arange(T)[:, None]
    t_i = jnp.arange(T)[None, :]
    kf = kern[0][jnp.clip(t_i - s_i, 0, T - 1)] * (t_i >= s_i)[..., None, None, None]
    kb = kern[1][jnp.clip(s_i - t_i, 0, T - 1)] * (s_i >= t_i)[..., None, None, None]
    m = jnp.transpose(kf + kb, (2, 0, 4, 1, 3))
    eye = (jnp.eye(T)[:, None, :, None] * jnp.eye(Q)[None, :, None, :])
    m = m + eye[None] * d_skip.reshape(G, 1, 1, 1, Q)
    m = m.reshape(G, T * Q, T * Q)
    e_f = pw[0][T - 1 - jnp.arange(T)]
    e_b = pw[1][jnp.arange(T)]
    n_f = e_f[..., None] * bb[0][None]
    n_b = e_b[..., None] * bb[1][None]
    n_c = jnp.concatenate([n_f, n_b], axis=2)
    n_c = jnp.transpose(n_c, (1, 0, 3, 2)).reshape(G, T * Q, 2 * P)
    lam_t = jnp.concatenate([pw[0][T], pw[1][T]], axis=-1)
    w_f = cm[0][:, None] * jnp.transpose(pw[0][1:T + 1], (1, 0, 2))[:, :, None, :]
    w_b = cm[1][:, None] * jnp.transpose(pw[1][T - jnp.arange(T)], (1, 0, 2))[:, :, None, :]
    w_f = jnp.transpose(w_f, (0, 3, 1, 2)).reshape(G, P, T * Q)
    w_b = jnp.transpose(w_b, (0, 3, 1, 2)).reshape(G, P, T * Q)
    z = jnp.zeros_like(jnp.real(w_f))
    c_mats = dict(c_f_re=jnp.concatenate([jnp.real(w_f), z], axis=1),
                  c_f_im=jnp.concatenate([-jnp.imag(w_f), z], axis=1),
                  c_b_re=jnp.concatenate([z, jnp.real(w_b)], axis=1),
                  c_b_im=jnp.concatenate([z, -jnp.imag(w_b)], axis=1))
    return dict(m=m.astype(BF16), n_re=jnp.real(n_c).astype(BF16), n_im=jnp.imag(n_c).astype(BF16),
                l_re=jnp.real(lam_t), l_im=jnp.imag(lam_t), **{k: v.astype(BF16) for k, v in c_mats.items()})


def _s5_in_kernel(u_ref, m_ref, nre_ref, nim_ref, yi_ref, sre_ref, sim_ref):
    u = u_ref[...]
    yi_ref[...] = jnp.dot(u, m_ref[...], preferred_element_type=F32)
    sre_ref[...] = jnp.dot(u, nre_ref[...], preferred_element_type=F32)
    sim_ref[...] = jnp.dot(u, nim_ref[...], preferred_element_type=F32)


def _s5_in(ug, tabs):
    G, R, W = ug.shape
    P2 = 2 * S5_P
    gspec = lambda n: pl.BlockSpec((None, W, n), lambda g: (g, 0, 0))
    rspec = lambda n: pl.BlockSpec((None, R, n), lambda g: (g, 0, 0))
    return pl.pallas_call(
        _s5_in_kernel,
        out_shape=(jax.ShapeDtypeStruct((G, R, W), F32),
                   jax.ShapeDtypeStruct((G, R, P2), F32),
                   jax.ShapeDtypeStruct((G, R, P2), F32)),
        grid=(G,),
        in_specs=[rspec(W), gspec(W), gspec(P2), gspec(P2)],
        out_specs=(rspec(W), rspec(P2), rspec(P2)),
        compiler_params=_cp("arbitrary"),
        name="s5_in",
    )(ug, tabs['m'], tabs['n_re'], tabs['n_im'])


def _s5_scan_kernel(nsb, ncb, nblk, sref_ref, simf_ref, sreb_ref, simb_ref, lre_ref, lim_ref,
                    h0re_ref, h0im_ref, *rest):
    hfre_ref, hfim_ref, hbre_ref, hbim_ref, fre_ref, fim_ref, cre_sc, cim_sc = rest[4:]
    P = S5_P
    j = pl.program_id(1)
    fwd = lax.broadcasted_iota(jnp.int32, (1, 1, 2 * P), 2) < P
    lre = lre_ref[...][:, None, :]
    lim = lim_ref[...][:, None, :]

    @pl.when(j == 0)
    def _():
        cre_sc[...] = h0re_ref[...]
        cim_sc[...] = h0im_ref[...]

    def at(k):
        return pl.ds(k, 1) if nsb == 1 else pl.ds(k, nsb, stride=ncb)

    def body(k, carry):
        hre, him = carry
        kb = ncb - 1 - k
        hfre_ref[:, at(k), :] = hre
        hfim_ref[:, at(k), :] = him
        hbre_ref[:, at(kb), :] = hre
        hbim_ref[:, at(kb), :] = him
        sre = jnp.where(fwd, sref_ref[:, at(k), :], sreb_ref[:, at(kb), :])
        sim = jnp.where(fwd, simf_ref[:, at(k), :], simb_ref[:, at(kb), :])
        return (lre * hre - lim * him + sre, lre * him + lim * hre + sim)

    hre, him = lax.fori_loop(0, ncb, body, (cre_sc[...], cim_sc[...]))
    cre_sc[...] = hre
    cim_sc[...] = him

    @pl.when(j == nblk - 1)
    def _():
        fre_ref[...] = hre
        fim_ref[...] = him


def _s5_scan(sre, sim, tabs, h0re, h0im, hprev, row0, nseq, nc, nsb, nblk):
    G, R, P2 = sre.shape
    assert nblk == 1 or nsb == 1
    ncb = nc // nblk
    rb = nsb * ncb
    assert row0 % rb == 0 and nseq % nsb == 0 and nc % nblk == 0
    b0 = row0 // rb
    fspec = pl.BlockSpec((G, rb, P2), lambda i, j: (0, b0 + i * nblk + j, 0))
    bspec = pl.BlockSpec((G, rb, P2), lambda i, j: (0, b0 + i * nblk + nblk - 1 - j, 0))
    lspec = pl.BlockSpec((G, P2), lambda i, j: (0, 0))
    qspec = pl.BlockSpec((None, G, nsb, P2), lambda i, j: (i, 0, 0, 0))
    anyspec = pl.BlockSpec(memory_space=pl.ANY)
    fin = jax.ShapeDtypeStruct((nseq // nsb, G, nsb, P2), F32)
    outs = pl.pallas_call(
        functools.partial(_s5_scan_kernel, nsb, ncb, nblk),
        out_shape=tuple(jax.ShapeDtypeStruct(h.shape, h.dtype) for h in hprev) + (fin, fin),
        grid=(nseq // nsb, nblk),
        in_specs=[fspec, fspec, bspec, bspec, lspec, lspec, qspec, qspec] + [anyspec] * 4,
        out_specs=(fspec, fspec, bspec, bspec, qspec, qspec),
        scratch_shapes=[pltpu.VMEM((G, nsb, P2), F32), pltpu.VMEM((G, nsb, P2), F32)],
        input_output_aliases={8: 0, 9: 1, 10: 2, 11: 3},
        compiler_params=_cp("arbitrary", "arbitrary"),
        name="s5_scan",
    )(sre, sim, sre, sim, tabs['l_re'], tabs['l_im'], h0re, h0im, *hprev)
    return outs[:4], outs[4], outs[5]


def _s5_out_kernel(yi_ref, hfre_ref, hfim_ref, hbre_ref, hbim_ref, cfre_ref, cfim_ref, cbre_ref, cbim_ref,
                   y_ref):
    y = yi_ref[...]
    for h_ref, c_ref in ((hfre_ref, cfre_ref), (hfim_ref, cfim_ref), (hbre_ref, cbre_ref), (hbim_ref, cbim_ref)):
        y += jnp.dot(h_ref[...].astype(BF16), c_ref[...], preferred_element_type=F32)
    y_ref[...] = y.astype(y_ref.dtype)


def _s5_out(yi, hprev, tabs):
    G, R, W = yi.shape
    P2 = 2 * S5_P
    gspec = pl.BlockSpec((None, P2, W), lambda g: (g, 0, 0))
    hspec = pl.BlockSpec((None, R, P2), lambda g: (g, 0, 0))
    rspec = pl.BlockSpec((None, R, W), lambda g: (g, 0, 0))
    return pl.pallas_call(
        _s5_out_kernel,
        out_shape=jax.ShapeDtypeStruct((G, R, W), BF16),
        grid=(G,),
        in_specs=[rspec] + [hspec] * 4 + [gspec] * 4,
        out_specs=rspec,
        compiler_params=_cp("arbitrary"),
        name="s5_out",
    )(yi, *hprev, tabs['c_f_re'], tabs['c_f_im'], tabs['c_b_re'], tabs['c_b_im'])


def _s5_glu_kernel(x_ref, y_ref, g_ref, wv_ref, wg_ref, o_ref, a_sc):
    @pl.when(pl.program_id(1) == 0)
    def _():
        a_sc[...] = jax.nn.gelu(y_ref[...].astype(F32)).astype(BF16)

    a = a_sc[...]
    val = jnp.dot(a, wv_ref[...], preferred_element_type=F32)
    gate = jnp.dot(a, wg_ref[...], preferred_element_type=F32)
    o_ref[...] = x_ref[...] + g_ref[...] * (val * jax.nn.sigmoid(gate))


def _s5_glu(lay, x, y, mods, glu_w):
    tm = lay.tile(1024)
    tn = 512
    nn = D // tn
    return pl.pallas_call(
        _s5_glu_kernel,
        out_shape=jax.ShapeDtypeStruct((lay.T, D), F32),
        grid=(lay.T // tm, nn),
        in_specs=[pl.BlockSpec((tm, tn), lambda i, j: (i, j)),
                  pl.BlockSpec((tm, D), lambda i, j: (i, 0)),
                  pl.BlockSpec((None, 1, tn), lambda i, j: (lay.group(i * tm) * MOD_CHUNKS + 2, 0, j)),
                  pl.BlockSpec((D, tn), lambda i, j: (0, j)),
                  pl.BlockSpec((D, tn), lambda i, j: (0, nn + j))],
        out_specs=pl.BlockSpec((tm, tn), lambda i, j: (i, j)),
        scratch_shapes=[pltpu.VMEM((tm, D), BF16)],
        compiler_params=_cp("arbitrary", "arbitrary"),
        name="s5_glu",
    )(x, y, mods, glu_w, glu_w)


def _s5_layer(lay, x, mods, p, h0_re, h0_im):
    T, G, P, Q = S5_T, S5_G, S5_P, S5_Q
    tabs = _s5_tables(p['a_re'], p['a_im'], p['log_dt'], p['b_re'], p['b_im'], p['c_re'], p['c_im'], p['d'])
    u = _modulate_call(lay, x, mods, 0, 1, BF16)
    R = lay.T // T
    ug = jnp.transpose(u.reshape(R, T, G, Q), (2, 0, 1, 3)).reshape(G, R, T * Q)
    yi, sre, sim = _s5_in(ug, tabs)
    hprev = tuple(jnp.zeros((G, R, 2 * P), F32) for _ in range(4))
    ncp, ncs = lay.L // T, lay.LS // T
    nsb = math.gcd(lay.B, max(1, S5_SCAN_ROWS // ncp))
    zero = jnp.zeros((lay.B // nsb, G, nsb, 2 * P), F32)
    hprev, fre, fim = _s5_scan(sre, sim, tabs, zero, zero, hprev, 0, lay.B, ncp, nsb, 1)
    to_lanes = lambda s: jnp.transpose(s, (0, 2, 1, 3)).reshape(lay.NS, G, 1, 2 * P)
    hprev, _, _ = _s5_scan(sre, sim, tabs, to_lanes(h0_re), to_lanes(h0_im), hprev,
                           lay.TP // T, lay.NS, ncs, 1, max(1, ncs // S5_SCAN_ROWS))
    yg = _s5_out(yi, hprev, tabs)
    y = jnp.transpose(yg.reshape(G, R, T, Q), (1, 2, 0, 3)).reshape(lay.T, D)
    x = _s5_glu(lay, x, y, mods, p['glu_w'])
    from_lanes = lambda s: jnp.transpose(s, (0, 2, 1, 3)).reshape(lay.B, G, 2, P).transpose(0, 2, 1, 3)
    return x, from_lanes(fre), from_lanes(fim)


def _hyena_filters(L, p):
    t = jnp.linspace(0.0, 1.0, L, dtype=F32)[:, None]
    w = 2.0 * math.pi * jnp.arange(L, dtype=F32)[:, None] / L
    f = jnp.linspace(1e-4, HY_BANDS - 1, HY_BANDS, dtype=F32)[None, :]
    feats = jnp.concatenate([t, jnp.cos(f * w), -jnp.sin(f * w)], axis=-1)
    mm = functools.partial(jnp.matmul, precision=HIGHEST)
    z = jnp.sin(p['f_freq'][0] * (mm(feats, p['f_w1']) + p['f_b1']))
    z = jnp.sin(p['f_freq'][1] * (mm(z, p['f_w2']) + p['f_b2']))
    h = mm(z, p['f_w3']).reshape(L, 2, 2, D)
    max_decay = math.log(HY_TARGET) / HY_FAST_PCT
    min_decay = math.log(HY_TARGET) / HY_SLOW_PCT
    deltas = jnp.abs(jnp.linspace(min_decay, max_decay, D, dtype=F32))
    h = h * jnp.exp(-t[:, :, None, None] * deltas)
    k_lo = h[:, :, 0]
    k_hi = jnp.concatenate([jnp.zeros((1, 2, D), F32), h[:0:-1, :, 1]], axis=0)
    norm = jnp.sum(jnp.abs(k_lo), axis=0, keepdims=True) + jnp.sum(jnp.abs(k_hi), axis=0, keepdims=True)
    k_lo, k_hi = k_lo / norm, k_hi / norm
    alt = (1.0 - 2.0 * (jnp.arange(L) % 2).astype(F32))[:, None, None]
    k_ny = jnp.sum(alt * (k_lo + k_hi), axis=0) / (2 * L)
    return jnp.transpose(k_lo, (1, 0, 2)), jnp.transpose(k_hi, (1, 0, 2)), k_ny


def _dft_tables(L):
    i = jnp.arange(L, dtype=jnp.int32)
    idx = (i[:, None] * i[None, :]) % (2 * L)
    ang = idx.astype(F32) * (math.pi / L)
    return jnp.cos(ang).astype(BF16), jnp.sin(ang).astype(BF16)


def _hy_spec_kernel(L, tr, c_ref, s_ref, klo_ref, khi_ref, p_ref, q_ref):
    r = pl.program_id(2)
    f = r * tr + lax.broadcasted_iota(jnp.int32, (tr, 1), 0)
    sgn = (1 - 2 * (f % 2)).astype(F32)
    scale = jnp.where(f == 0, 1.0, 2.0) * (1.0 / (2 * L))
    c, s = c_ref[...], s_ref[...]
    lo, hi = klo_ref[...], khi_ref[...]
    dot = functools.partial(jnp.dot, preferred_element_type=F32)
    p_ref[...] = scale * (dot(c, lo) + sgn * dot(c, hi))
    q_ref[...] = scale * (dot(s, lo) + sgn * dot(s, hi))


def _hy_spectrum(L, cos, sin, k_lo, k_hi):
    tr = min(L, 512)
    tc = 512
    kspec = pl.BlockSpec((None, L, tc), lambda o, j, r: (o, 0, j))
    tspec = pl.BlockSpec((tr, L), lambda o, j, r: (r, 0))
    ospec = pl.BlockSpec((None, tr, tc), lambda o, j, r: (o, r, j))
    return pl.pallas_call(
        functools.partial(_hy_spec_kernel, L, tr),
        out_shape=(jax.ShapeDtypeStruct((2, L, D), F32), jax.ShapeDtypeStruct((2, L, D), F32)),
        grid=(2, D // tc, L // tr),
        in_specs=[tspec, tspec, kspec, kspec],
        out_specs=(ospec, ospec),
        compiler_params=_cp("arbitrary", "arbitrary", "arbitrary"),
        name="hy_spectrum",
    )(cos, sin, k_lo.astype(BF16), k_hi.astype(BF16))


def _hy_core_kernel(L, tr, ngrp, tc, x1_ref, x2_ref, v_ref, sw1_ref, sw2_ref, swv_ref, sb1_ref, sb2_ref,
                    sbv_ref, c_ref, s_ref, p_ref, q_ref, kny_ref, skip_ref, *rest):
    o_ref, z0_sc, z1_sc, x1_sc, x2_sc, a_sc, b_sc, ny_sc = rest[-8:]
    ph = pl.program_id(2)
    r = pl.program_id(3)
    nrt = L // tr
    W = ngrp * tc
    row = lax.broadcasted_iota(jnp.int32, (L, 1), 0)
    alt_all = (1 - 2 * (row % 2)).astype(F32)

    def conv3(src_ref, g, w_ref, b_ref):
        x = src_ref[g].astype(F32)
        prev = jnp.where(row == 0, 0.0, pltpu.roll(x, 1, 0))
        nxt = jnp.where(row == L - 1, 0.0, pltpu.roll(x, L - 1, 0))
        return prev * w_ref[0:1, :] + x * w_ref[1:2, :] + nxt * w_ref[2:3, :] + b_ref[...]

    @pl.when(jnp.logical_and(ph == 0, r == 0))
    def _():
        for g in range(ngrp):
            cols = slice(g * tc, (g + 1) * tc)
            x1_sc[:, cols] = conv3(x1_ref, g, sw1_ref, sb1_ref).astype(BF16)
            x2_sc[:, cols] = conv3(x2_ref, g, sw2_ref, sb2_ref).astype(BF16)
            z0_sc[:, cols] = conv3(v_ref, g, swv_ref, sbv_ref).astype(BF16)
        ny_sc[...] = jnp.sum(alt_all * z0_sc[...].astype(F32), axis=0, keepdims=True)

    rows = pl.ds(pl.multiple_of(r * tr, tr), tr)
    tile = lambda ref: jnp.concatenate([ref[...]] * ngrp, axis=1)
    dot = functools.partial(jnp.dot, preferred_element_type=F32)

    def forward(z_sc):
        z = z_sc[...]
        zre = dot(c_ref[...], z)
        zim = dot(s_ref[...], z)
        pw, qw = tile(p_ref), tile(q_ref)
        a_sc[rows, :] = (zre * pw - zim * qw).astype(BF16)
        b_sc[rows, :] = (zim * pw + zre * qw).astype(BF16)

    def inverse(order, z_sc):
        y = dot(c_ref[...], a_sc[...]) + dot(s_ref[...], b_sc[...])
        t = r * tr + lax.broadcasted_iota(jnp.int32, (tr, 1), 0)
        alt = (1 - 2 * (t % 2)).astype(F32)
        kny = jnp.concatenate([kny_ref[order:order + 1, :]] * ngrp, axis=1)
        skip = jnp.concatenate([skip_ref[order:order + 1, :]] * ngrp, axis=1)
        return y + alt * (ny_sc[...] * kny) + skip * z_sc[rows, :].astype(F32)

    @pl.when(ph == 0)
    def _():
        forward(z0_sc)

    @pl.when(ph == 1)
    def _():
        z1_sc[rows, :] = (x1_sc[rows, :].astype(F32) * inverse(0, z0_sc)).astype(BF16)

        @pl.when(r == nrt - 1)
        def _():
            ny_sc[...] = jnp.sum(alt_all * z1_sc[...].astype(F32), axis=0, keepdims=True)

    @pl.when(ph == 2)
    def _():
        forward(z1_sc)

    @pl.when(ph == 3)
    def _():
        out = x2_sc[rows, :].astype(F32) * inverse(1, z1_sc)
        for g in range(ngrp):
            o_ref[g, rows, :] = out[:, g * tc:(g + 1) * tc].astype(o_ref.dtype)


def _hy_core(proj, short_w, short_b, skip, cos, sin, pq, k_ny, o_prev, row0, nseq, L, ngrp, tc):
    T = proj.shape[0]
    tr = min(L, 512)
    nrt = L // tr
    assert row0 % (L * ngrp) == 0 and nseq % ngrp == 0 and T % L == 0
    sb0 = row0 // (L * ngrp)
    nct = D // tc
    p3 = proj.reshape(T // L, L, 3 * D)
    p_arr, q_arr = pq

    def xspec(part):
        return pl.BlockSpec((ngrp, L, tc), lambda i, j, ph, r: (sb0 + i, 0, part * nct + j))

    def wspec(part, rows_):
        return pl.BlockSpec((rows_, tc), lambda i, j, ph, r: (0, part * nct + j))

    def pq_map(i, j, ph, r):
        return (ph // 2, jnp.where(ph % 2 == 0, r, nrt - 1), j)

    tspec = pl.BlockSpec((tr, L), lambda i, j, ph, r: (r, 0))
    in_specs = [xspec(0), xspec(1), xspec(2), wspec(0, 3), wspec(1, 3), wspec(2, 3),
                wspec(0, 1), wspec(1, 1), wspec(2, 1), tspec, tspec,
                pl.BlockSpec((None, tr, tc), pq_map), pl.BlockSpec((None, tr, tc), pq_map),
                pl.BlockSpec((2, tc), lambda i, j, ph, r: (0, j)),
                pl.BlockSpec((2, tc), lambda i, j, ph, r: (0, j))]
    sb = short_b.reshape(1, 3 * D)
    args = [p3, p3, p3, short_w, short_w, short_w, sb, sb, sb, cos, sin, p_arr, q_arr, k_ny, skip]
    aliases = {}
    if o_prev is not None:
        in_specs.append(pl.BlockSpec(memory_space=pl.ANY))
        args.append(o_prev.reshape(T // L, L, D))
        aliases = {len(args) - 1: 0}
    W = ngrp * tc
    out = pl.pallas_call(
        functools.partial(_hy_core_kernel, L, tr, ngrp, tc),
        out_shape=jax.ShapeDtypeStruct((T // L, L, D), BF16),
        grid=(nseq // ngrp, nct, 4, nrt),
        in_specs=in_specs,
        out_specs=pl.BlockSpec((ngrp, L, tc), lambda i, j, ph, r: (sb0 + i, 0, j)),
        scratch_shapes=[pltpu.VMEM((L, W), BF16)] * 6 + [pltpu.VMEM((1, W), F32)],
        input_output_aliases=aliases,
        compiler_params=_cp("arbitrary", "arbitrary", "arbitrary", "arbitrary"),
        name="hy_core",
    )(*args)
    return out.reshape(T, D)


def _plain_out_kernel(x_ref, z_ref, g_ref, w_ref, b_ref, o_ref):
    acc = jnp.dot(z_ref[...], w_ref[...], preferred_element_type=F32) + b_ref[...]
    o_ref[...] = x_ref[...] + g_ref[...] * acc


def _plain_out(lay, x, mods, z, w, b):
    tm = lay.tile(1024)
    tn = 512
    kdim = z.shape[1]
    return pl.pallas_call(
        _plain_out_kernel,
        out_shape=jax.ShapeDtypeStruct((lay.T, D), F32),
        grid=(lay.T // tm, D // tn),
        in_specs=[pl.BlockSpec((tm, tn), lambda i, j: (i, j)),
                  pl.BlockSpec((tm, kdim), lambda i, j: (i, 0)),
                  pl.BlockSpec((None, 1, tn), lambda i, j: (lay.group(i * tm) * MOD_CHUNKS + 2, 0, j)),
                  pl.BlockSpec((kdim, tn), lambda i, j: (0, j)),
                  pl.BlockSpec((1, tn), lambda i, j: (0, j))],
        out_specs=pl.BlockSpec((tm, tn), lambda i, j: (i, j)),
        compiler_params=_cp("arbitrary", "arbitrary"),
        name="plain_out",
    )(x, z, mods, w, b.reshape(1, D))


def _hyena_layer(lay, x, mods, p):
    proj = _proj(lay, x, mods, 0, 1, p['w_in'].astype(BF16), p['b_in'].reshape(1, 3 * D), 768)
    z = jnp.zeros((lay.T, D), BF16)
    for row0, nseq, L, ngrp, tc in ((0, lay.B, lay.L, math.gcd(lay.B, 4), 256),
                                    (lay.TP, lay.NS, lay.LS, lay.NS, 128)):
        k_lo, k_hi, k_ny = _hyena_filters(L, p)
        cos, sin = _dft_tables(L)
        pq = _hy_spectrum(L, cos, sin, k_lo, k_hi)
        z = _hy_core(proj, p['short_w'], p['short_b'], p['skip'], cos, sin, pq, k_ny, z, row0, nseq, L, ngrp, tc)
    return _plain_out(lay, x, mods, z, p['w_out'].astype(BF16), p['b_out'])


_NT = (((1,), (1,)), ((), ()))
_TN = (((0,), (0,)), ((), ()))


def _tri(dr):
    t = lax.broadcasted_iota(jnp.int32, (CHUNK, CHUNK), 0)
    s = lax.broadcasted_iota(jnp.int32, (CHUNK, CHUNK), 1)
    return (s <= t) if dr == 0 else (s >= t)


def _gla_kernel(nc, has_s0, *refs):
    q_ref, k_ref, v_ref, lr_ref, w2f_ref, w2b_ref, gb_ref = refs[:7]
    s0_ref = refs[7] if has_s0 else None
    o_ref, sf_ref, st_sc = refs[-3:]
    C = CHUNK
    w2 = (w2f_ref, w2b_ref)
    for dr in range(2):
        if has_s0:
            st_sc[dr] = jnp.transpose(s0_ref[dr], (1, 0))
        else:
            st_sc[dr] = jnp.zeros(st_sc.shape[1:], F32)
    o_ref[...] = jnp.zeros_like(o_ref)

    def chunk(dr, r0):
        rows = pl.ds(r0, C)
        q = q_ref[rows, :].astype(F32) * (GLA_DK ** -0.5)
        k = k_ref[rows, :].astype(F32)
        v = v_ref[rows, :]
        pre = jnp.dot(lr_ref[rows, :], w2[dr][...], preferred_element_type=F32) + gb_ref[dr:dr + 1, :]
        g = (jnp.minimum(pre, 0.0) - jnp.log(1.0 + jnp.exp(-jnp.abs(pre)))) * (1.0 / GLA_TAU)
        tri = _tri(dr)
        b = jnp.dot(tri.astype(F32), g, precision=HIGHEST, preferred_element_type=F32)
        btot = b[C - 1:C, :] if dr == 0 else b[0:1, :]
        q_in = (q * jnp.exp(b)).astype(BF16)
        k_in = (k * jnp.exp(-b)).astype(BF16)
        k_out = (k * jnp.exp(btot - b)).astype(BF16)
        sc = lax.dot_general(q_in, k_in, _NT, preferred_element_type=F32)
        sc = jnp.where(tri, sc, 0.0).astype(BF16)
        st = st_sc[dr]
        o = jnp.dot(sc, v, preferred_element_type=F32)
        o += lax.dot_general(q_in, st.astype(BF16), _NT, preferred_element_type=F32)
        st_sc[dr] = jnp.exp(btot) * st + lax.dot_general(v, k_out, _TN, preferred_element_type=F32)
        o_ref[rows, :] += o

    def body(i, carry):
        chunk(0, pl.multiple_of(i * C, C))
        chunk(1, pl.multiple_of((nc - 1 - i) * C, C))
        return carry

    lax.fori_loop(0, nc, body, 0)
    for dr in range(2):
        sf_ref[dr] = jnp.transpose(st_sc[dr], (1, 0))


def _gla_core(proj, w2f, w2b, gate_b, s0, o_prev, row0, nseq, seqlen):
    T = proj.shape[0]
    assert row0 % seqlen == 0 and seqlen % CHUNK == 0
    rb = row0 // seqlen
    hk = GLA_H * GLA_DK
    has_s0 = s0 is not None
    in_specs = [pl.BlockSpec((seqlen, GLA_DK), lambda b, h: (rb + b, h)),
                pl.BlockSpec((seqlen, GLA_DK), lambda b, h: (rb + b, GLA_H + h)),
                pl.BlockSpec((seqlen, GLA_DV), lambda b, h: (rb + b, 2 * hk // GLA_DV + h)),
                pl.BlockSpec((seqlen, 128), lambda b, h: (rb + b, (2 * hk + 2 * GLA_H * GLA_DV) // 128)),
                pl.BlockSpec((128, GLA_DK), lambda b, h: (0, h)),
                pl.BlockSpec((128, GLA_DK), lambda b, h: (0, h)),
                pl.BlockSpec((2, GLA_DK), lambda b, h: (0, h))]
    args = [proj, proj, proj, proj, w2f, w2b, gate_b]
    sspec = pl.BlockSpec((None, 2, None, GLA_DK, GLA_DV), lambda b, h: (b, 0, h, 0, 0))
    if has_s0:
        in_specs.append(sspec)
        args.append(s0)
    in_specs.append(pl.BlockSpec(memory_space=pl.ANY))
    args.append(o_prev)
    aliases = {len(args) - 1: 0}
    return pl.pallas_call(
        functools.partial(_gla_kernel, seqlen // CHUNK, has_s0),
        out_shape=(jax.ShapeDtypeStruct((T, GLA_H * GLA_DV), F32),
                   jax.ShapeDtypeStruct((nseq, 2, GLA_H, GLA_DK, GLA_DV), F32)),
        grid=(nseq, GLA_H),
        in_specs=in_specs,
        out_specs=(pl.BlockSpec((seqlen, GLA_DV), lambda b, h: (rb + b, h)), sspec),
        scratch_shapes=[pltpu.VMEM((2, GLA_DV, GLA_DK), F32)],
        input_output_aliases=aliases,
        compiler_params=_cp("arbitrary", "arbitrary"),
        name="gla",
    )(*args)


def _ret_kernel(nc, has_s0, rope, *refs):
    refs = list(refs)
    q_ref, k_ref, v_ref, dm_ref, qd_ref, kd_ref, cd_ref = refs[:7]
    del refs[:7]
    if rope:
        cos_ref, sin_ref = refs[:2]
        del refs[:2]
    s0_ref = refs.pop(0) if has_s0 else None
    o_ref, sf_ref, s_sc = refs[-3:]
    C = CHUNK
    for dr in range(2):
        s_sc[dr] = s0_ref[dr] if has_s0 else jnp.zeros(s_sc.shape[1:], F32)
    o_ref[...] = jnp.zeros_like(o_ref)

    def rot(x, rows):
        if not rope:
            return x
        half = x.shape[1] // 2
        swapped = jnp.concatenate([pltpu.roll(x[:, :half], half // 2, 1),
                                   pltpu.roll(x[:, half:], half // 2, 1)], axis=1)
        return x * cos_ref[rows, :] + swapped * sin_ref[rows, :]

    def chunk(dr, r0):
        rows = pl.ds(r0, C)
        q = rot(q_ref[rows, :].astype(F32), rows)
        k = rot(k_ref[rows, :].astype(F32), rows) * (RET_DK ** -0.5)
        v = v_ref[rows, :]
        sc = lax.dot_general(q.astype(BF16), k.astype(BF16), _NT, preferred_element_type=F32)
        sc = (sc * dm_ref[dr]).astype(BF16)
        s = s_sc[dr]
        o = jnp.dot(sc, v, preferred_element_type=F32)
        o += jnp.dot((q * qd_ref[dr]).astype(BF16), s.astype(BF16), preferred_element_type=F32)
        kv = lax.dot_general((k * kd_ref[dr]).astype(BF16), v, _TN, preferred_element_type=F32)
        s_sc[dr] = cd_ref[dr] * s + kv
        o_ref[rows, :] += o

    def body(i, carry):
        chunk(0, pl.multiple_of(i * C, C))
        chunk(1, pl.multiple_of((nc - 1 - i) * C, C))
        return carry

    lax.fori_loop(0, nc, body, 0)
    for dr in range(2):
        sf_ref[dr] = s_sc[dr]


def _ret_tables(log_decay):
    C = CHUNK
    lg = log_decay.astype(F32)[:, :, None, None]
    t = jnp.arange(C, dtype=F32)[:, None]
    s = jnp.arange(C, dtype=F32)[None, :]
    lag = jnp.stack([t - s, s - t])[:, None]
    dmask = jnp.where(lag >= 0, jnp.exp(jnp.maximum(lag, 0.0) * lg), 0.0)
    tl = jnp.arange(C, dtype=F32)[None, None, :, None]
    qdec = jnp.concatenate([jnp.exp((tl + 1.0) * lg[0:1]), jnp.exp((C - tl) * lg[1:2])], axis=0)
    kdec = jnp.concatenate([jnp.exp((C - 1.0 - tl) * lg[0:1]), jnp.exp(tl * lg[1:2])], axis=0)
    cdec = jnp.exp(C * lg)
    return dmask, qdec, kdec, cdec


def _rope_tables(seqlen, dk):
    half = dk // 2
    nf = half // 2
    pos = jnp.arange(seqlen, dtype=jnp.int32)
    inv = ROPE_BASE ** (-jnp.arange(nf, dtype=F32) / nf)
    ang_r = (pos // GRID_W).astype(F32)[:, None] * inv[None, :]
    ang_c = (pos % GRID_W).astype(F32)[:, None] * inv[None, :]
    cos = jnp.concatenate([jnp.cos(ang_r)] * 2 + [jnp.cos(ang_c)] * 2, axis=1)
    sin = jnp.concatenate([-jnp.sin(ang_r), jnp.sin(ang_r), -jnp.sin(ang_c), jnp.sin(ang_c)], axis=1)
    return cos, sin


def _ret_core(proj, tabs, s0, o_prev, row0, nseq, seqlen, rope):
    T = proj.shape[0]
    assert row0 % seqlen == 0 and seqlen % CHUNK == 0
    rb = row0 // seqlen
    hk = RET_H * RET_DK
    has_s0 = s0 is not None
    C = CHUNK
    tspec = lambda r, c: pl.BlockSpec((2, None, r, c), lambda b, h: (0, h, 0, 0))
    in_specs = [pl.BlockSpec((seqlen, RET_DK), lambda b, h: (rb + b, h)),
                pl.BlockSpec((seqlen, RET_DK), lambda b, h: (rb + b, RET_H + h)),
                pl.BlockSpec((seqlen, RET_DV), lambda b, h: (rb + b, 2 * hk // RET_DV + h)),
                tspec(C, C), tspec(C, 1), tspec(C, 1), tspec(1, 1)]
    args = [proj, proj, proj, *tabs]
    if rope:
        cos, sin = _rope_tables(seqlen, RET_DK)
        in_specs += [pl.BlockSpec((seqlen, RET_DK), lambda b, h: (0, 0), pipeline_mode=pl.Buffered(1))] * 2
        args += [cos, sin]
    sspec = pl.BlockSpec((None, 2, None, RET_DK, RET_DV), lambda b, h: (b, 0, h, 0, 0))
    if has_s0:
        in_specs.append(sspec)
        args.append(s0)
    in_specs.append(pl.BlockSpec(memory_space=pl.ANY))
    args.append(o_prev)
    aliases = {len(args) - 1: 0}
    return pl.pallas_call(
        functools.partial(_ret_kernel, seqlen // C, has_s0, rope),
        out_shape=(jax.ShapeDtypeStruct((T, RET_H * RET_DV), F32),
                   jax.ShapeDtypeStruct((nseq, 2, RET_H, RET_DK, RET_DV), F32)),
        grid=(nseq, RET_H),
        in_specs=in_specs,
        out_specs=(pl.BlockSpec((seqlen, RET_DV), lambda b, h: (rb + b, h)), sspec),
        scratch_shapes=[pltpu.VMEM((2, RET_DK, RET_DV), F32)],
        input_output_aliases=aliases,
        compiler_params=_cp("arbitrary", "arbitrary"),
        name="ret",
    )(*args)


def _headnorm_out_kernel(nh, dv, center, x_ref, o_ref, gt_ref, ng_ref, g_ref, w_ref, out_ref, a_sc):
    @pl.when(pl.program_id(1) == 0)
    def _():
        for h in range(nh):
            cols = slice(h * dv, (h + 1) * dv)
            oh = o_ref[:, cols]
            if center:
                oh = oh - jnp.mean(oh, axis=-1, keepdims=True)
            oh = oh * lax.rsqrt(jnp.mean(oh * oh, axis=-1, keepdims=True) + RMS_EPS) * ng_ref[...]
            a_sc[:, cols] = (oh * _silu(gt_ref[:, cols].astype(F32))).astype(BF16)

    out_ref[...] = x_ref[...] + g_ref[...] * jnp.dot(a_sc[...], w_ref[...], preferred_element_type=F32)


def _headnorm_out(lay, x, mods, o, proj, gate_col, norm_g, w_out, nh, dv, center):
    tm = lay.tile(512)
    tn = 512
    kdim = nh * dv
    assert gate_col % kdim == 0
    return pl.pallas_call(
        functools.partial(_headnorm_out_kernel, nh, dv, center),
        out_shape=jax.ShapeDtypeStruct((lay.T, D), F32),
        grid=(lay.T // tm, D // tn),
        in_specs=[pl.BlockSpec((tm, tn), lambda i, j: (i, j)),
                  pl.BlockSpec((tm, kdim), lambda i, j: (i, 0)),
                  pl.BlockSpec((tm, kdim), lambda i, j: (i, gate_col // kdim)),
                  pl.BlockSpec((1, dv), lambda i, j: (0, 0)),
                  pl.BlockSpec((None, 1, tn), lambda i, j: (lay.group(i * tm) * MOD_CHUNKS + 2, 0, j)),
                  pl.BlockSpec((kdim, tn), lambda i, j: (0, j))],
        out_specs=pl.BlockSpec((tm, tn), lambda i, j: (i, j)),
        scratch_shapes=[pltpu.VMEM((tm, kdim), BF16)],
        compiler_params=_cp("arbitrary", "arbitrary"),
        name="headnorm_out",
    )(x, o, proj, norm_g.reshape(1, dv), mods, w_out)


def _gla_layer(lay, x, mods, p, s0):
    hk, hv = GLA_H * GLA_DK, GLA_H * GLA_DV
    w_all = jnp.concatenate([p['w_in'], p['gate_w1'][0], p['gate_w1'][1],
                             jnp.zeros((D, 128 - 2 * GLA_RANK), F32)], axis=1).astype(BF16)
    proj = _proj(lay, x, mods, 0, 1, w_all, jnp.zeros((1, w_all.shape[1]), F32), 640)
    pad = lambda w, lo: jnp.pad(w, ((lo, 128 - GLA_RANK - lo), (0, 0))).astype(BF16)
    w2f, w2b = pad(p['gate_w2'][0], 0), pad(p['gate_w2'][1], GLA_RANK)
    o = jnp.zeros((lay.T, hv), F32)
    o, s_fin = _gla_core(proj, w2f, w2b, p['gate_b'], None, o, 0, lay.B, lay.L)
    o, _ = _gla_core(proj, w2f, w2b, p['gate_b'], s0, o, lay.TP, lay.NS, lay.LS)
    x = _headnorm_out(lay, x, mods, o, proj, 2 * hk + hv, p['norm_g'], p['w_out'].astype(BF16),
                      GLA_H, GLA_DV, False)
    return x, s_fin


def _ret_layer(lay, x, mods, p, s0):
    hk, hv = RET_H * RET_DK, RET_H * RET_DV
    proj = _proj(lay, x, mods, 0, 1, p['w_in'].astype(BF16), jnp.zeros((1, 2 * hk + 2 * hv), F32), 768)
    tabs = _ret_tables(p['log_decay'])
    o = jnp.zeros((lay.T, hv), F32)
    o, s_fin = _ret_core(proj, tabs, None, o, 0, lay.B, lay.L, False)
    o, _ = _ret_core(proj, tabs, s0, o, lay.TP, lay.NS, lay.LS, True)
    x = _headnorm_out(lay, x, mods, o, proj, 2 * hk + hv, p['norm_g'], p['w_out'].astype(BF16),
                      RET_H, RET_DV, True)
    return x, s_fin


MOE_BM = 512
ROUTER_LANES = 128


def _router_kernel(x_ref, sh_ref, sc_ref, rw_ref, h_ref, idx_ref, gate_ref):
    h = _modulate(x_ref[...], sh_ref[...], sc_ref[...])
    h_ref[...] = h
    logits = jnp.dot(h, rw_ref[...], precision=HIGHEST, preferred_element_type=F32)
    lane = lax.broadcasted_iota(jnp.int32, logits.shape, 1)
    neg = jnp.float32(-jnp.inf)
    logits = jnp.where(lane < N_EXPERTS, logits, neg)
    m1 = jnp.max(logits, axis=-1, keepdims=True)
    i1 = jnp.min(jnp.where(logits == m1, lane, ROUTER_LANES), axis=-1, keepdims=True)
    rest = jnp.where(lane == i1, neg, logits)
    m2 = jnp.max(rest, axis=-1, keepdims=True)
    i2 = jnp.min(jnp.where(rest == m2, lane, ROUTER_LANES), axis=-1, keepdims=True)
    e2 = jnp.exp(m2 - m1)
    g1 = 1.0 / (1.0 + e2)
    idx_ref[:, 0:1] = i1
    idx_ref[:, 1:2] = i2
    gate_ref[:, 0:1] = g1
    gate_ref[:, 1:2] = e2 * g1


def _router(lay, x, mods, router_w):
    tm = lay.tile(512)
    rw = jnp.pad(router_w, ((0, 0), (0, ROUTER_LANES - N_EXPERTS)))
    return pl.pallas_call(
        _router_kernel,
        out_shape=(jax.ShapeDtypeStruct((lay.T, D), F32),
                   jax.ShapeDtypeStruct((lay.T, 2), jnp.int32),
                   jax.ShapeDtypeStruct((lay.T, 2), F32)),
        grid=(lay.T // tm,),
        in_specs=[pl.BlockSpec((tm, D), lambda i: (i, 0)),
                  _mod_spec(lay, tm, 3, 1), _mod_spec(lay, tm, 4, 1),
                  pl.BlockSpec((D, ROUTER_LANES), lambda i: (0, 0))],
        out_specs=(pl.BlockSpec((tm, D), lambda i: (i, 0)),
                   pl.BlockSpec((tm, 2), lambda i: (i, 0)),
                   pl.BlockSpec((tm, 2), lambda i: (i, 0))),
        compiler_params=_cp("arbitrary"),
        name="router",
    )(x, mods, mods, rw)


def _moe_plan(idx, gates, bm):
    a = idx.size
    e = idx.reshape(a)
    onehot = (e[:, None] == jnp.arange(N_EXPERTS, dtype=jnp.int32)[None, :]).astype(jnp.int32)
    csum = jnp.cumsum(onehot, axis=0)
    counts = csum[-1]
    rank = jnp.sum((csum - onehot) * onehot, axis=-1)
    padded = (counts + bm - 1) // bm * bm
    pad_end = jnp.cumsum(padded)
    dest = ((pad_end - padded)[e] + rank).astype(jnp.int32)
    nb = -(-(a + N_EXPERTS * (bm - 1)) // bm)
    tok = jnp.arange(a, dtype=jnp.int32) // 2
    row_tok = jnp.zeros((nb * bm,), jnp.int32).at[dest].set(tok)
    row_gate = jnp.zeros((nb * bm,), F32).at[dest].set(gates.reshape(a)).reshape(nb * bm, 1)
    block_start = jnp.arange(nb, dtype=jnp.int32) * bm
    block_e = jnp.minimum(jnp.searchsorted(pad_end, block_start, side='right'), N_EXPERTS - 1).astype(jnp.int32)
    nvalid = (pad_end[-1] // bm).astype(jnp.int32).reshape(1)
    return dest, row_tok, row_gate, block_e, nvalid, nb


def _experts_kernel(bm, nf, be_ref, nv_ref, rt_ref, h_hbm, gate_ref, wa_ref, wb_ref, wo_ref, o_ref,
                    xs_sc, xb_sc, acc_sc, sem):
    i = pl.program_id(0)
    f = pl.program_id(1)
    valid = i < nv_ref[0]

    def row_copy(r, tok):
        return pltpu.make_async_copy(h_hbm.at[pl.ds(tok, 1)], xs_sc.at[pl.ds(r, 1)], sem)

    @pl.when(jnp.logical_and(valid, f == 0))
    def _():
        def issue(r, carry):
            row_copy(r, rt_ref[i * bm + r]).start()
            return carry

        def drain(r, carry):
            row_copy(r, 0).wait()
            return carry

        lax.fori_loop(0, bm, issue, 0)
        lax.fori_loop(0, bm, drain, 0)
        xb_sc[...] = xs_sc[...].astype(BF16)
        acc_sc[...] = jnp.zeros_like(acc_sc)

    @pl.when(valid)
    def _():
        xb = xb_sc[...]
        a = jnp.dot(xb, wa_ref[...], preferred_element_type=F32)
        b = jnp.dot(xb, wb_ref[...], preferred_element_type=F32)
        h = (_silu(a) * b).astype(BF16)
        acc_sc[...] += jnp.dot(h, wo_ref[...], preferred_element_type=F32)

    @pl.when(f == nf - 1)
    def _():
        o_ref[...] = jnp.where(valid, acc_sc[...] * gate_ref[...], 0.0)


def _experts(h, row_tok, row_gate, block_e, nvalid, nb, bm, w_in, w_out):
    tf = 896
    nf = EXPERT_DIM // tf

    def wmap(off):
        def imap(i, f, be, nv, rt):
            fe = jnp.where(i < nv[0], f, nf - 1)
            return (be[i], 0, off + fe)
        return imap

    def womap(i, f, be, nv, rt):
        fe = jnp.where(i < nv[0], f, nf - 1)
        return (be[i], fe, 0)

    grid_spec = pltpu.PrefetchScalarGridSpec(
        num_scalar_prefetch=3,
        grid=(nb, nf),
        in_specs=[pl.BlockSpec(memory_space=pl.ANY),
                  pl.BlockSpec((bm, 1), lambda i, f, be, nv, rt: (i, 0)),
                  pl.BlockSpec((None, D, tf), wmap(0)),
                  pl.BlockSpec((None, D, tf), wmap(nf)),
                  pl.BlockSpec((None, tf, D), womap)],
        out_specs=pl.BlockSpec((bm, D), lambda i, f, be, nv, rt: (i, 0)),
        scratch_shapes=[pltpu.VMEM((bm, D), F32), pltpu.VMEM((bm, D), BF16), pltpu.VMEM((bm, D), F32),
                        pltpu.SemaphoreType.DMA(())],
    )
    return pl.pallas_call(
        functools.partial(_experts_kernel, bm, nf),
        out_shape=jax.ShapeDtypeStruct((nb * bm, D), F32),
        grid_spec=grid_spec,
        compiler_params=_cp("arbitrary", "arbitrary"),
        name="experts",
    )(block_e, nvalid, row_tok, h, row_gate, w_in, w_in, w_out)


def _combine_kernel(tm, final, dest_ref, x_ref, g_ref, fg_ref, ys_hbm, o_ref, y0_sc, y1_sc, sem):
    i = pl.program_id(0)

    def row_copy(r, src, dst):
        return pltpu.make_async_copy(ys_hbm.at[pl.ds(src, 1)], dst.at[pl.ds(r, 1)], sem)

    def issue(r, carry):
        a = 2 * (i * tm + r)
        row_copy(r, dest_ref[a], y0_sc).start()
        row_copy(r, dest_ref[a + 1], y1_sc).start()
        return carry

    def drain(r, carry):
        row_copy(r, 0, y0_sc).wait()
        row_copy(r, 0, y1_sc).wait()
        return carry

    lax.fori_loop(0, tm, issue, 0)
    lax.fori_loop(0, tm, drain, 0)
    out = x_ref[...] + g_ref[...] * (y0_sc[...] + y1_sc[...])
    if final:
        ms = jnp.mean(out * out, axis=-1, keepdims=True)
        out = out * lax.rsqrt(ms + RMS_EPS) * fg_ref[...]
    o_ref[...] = out


def _combine(lay, x, mods, ys, dest, final_g):
    tm = lay.tile(256)
    final = final_g is not None
    fg = (final_g if final else jnp.ones((D,), F32)).reshape(1, D)
    grid_spec = pltpu.PrefetchScalarGridSpec(
        num_scalar_prefetch=1,
        grid=(lay.T // tm,),
        in_specs=[pl.BlockSpec((tm, D), lambda i, d: (i, 0)),
                  pl.BlockSpec((None, 1, D), lambda i, d: (lay.group(i * tm) * MOD_CHUNKS + 5, 0, 0)),
                  pl.BlockSpec((1, D), lambda i, d: (0, 0)),
                  pl.BlockSpec(memory_space=pl.ANY)],
        out_specs=pl.BlockSpec((tm, D), lambda i, d: (i, 0)),
        scratch_shapes=[pltpu.VMEM((tm, D), F32), pltpu.VMEM((tm, D), F32), pltpu.SemaphoreType.DMA(())],
    )
    return pl.pallas_call(
        functools.partial(_combine_kernel, tm, final),
        out_shape=jax.ShapeDtypeStruct((lay.T, D), F32),
        grid_spec=grid_spec,
        compiler_params=_cp("arbitrary"),
        name="combine",
    )(dest, x, mods, fg, ys)


def _moe_layer(lay, x, mods, router_w, w_in, w_out, final_g=None, bm=MOE_BM):
    h, idx, gates = _router(lay, x, mods, router_w)
    dest, row_tok, row_gate, block_e, nvalid, nb = _moe_plan(idx, gates, bm)
    ys = _experts(h, row_tok, row_gate, block_e, nvalid, nb, bm, w_in, w_out)
    return _combine(lay, x, mods, ys, dest, final_g)


def kernel(x_prompt, x_sample, c, state_l0_s5_re, state_l0_s5_im, state_l2_gla, state_l3_ret, c_ctx, l0_mod_w, l0_mod_b, l0_s5_a_re, l0_s5_a_im, l0_s5_log_dt, l0_s5_b_re, l0_s5_b_im, l0_s5_c_re, l0_s5_c_im, l0_s5_d, l0_s5_glu_w, l0_ffn_w_in, l0_ffn_w_out, l1_mod_w, l1_mod_b, l1_hy_w_in, l1_hy_b_in, l1_hy_short_w, l1_hy_short_b, l1_hy_f_w1, l1_hy_f_b1, l1_hy_f_w2, l1_hy_f_b2, l1_hy_f_w3, l1_hy_f_freq, l1_hy_skip, l1_hy_w_out, l1_hy_b_out, l1_moe_router, l1_moe_w_in, l1_moe_w_out, l2_mod_w, l2_mod_b, l2_gla_w_in, l2_gla_gate_w1, l2_gla_gate_w2, l2_gla_gate_b, l2_gla_norm_g, l2_gla_w_out, l2_ffn_w_in, l2_ffn_w_out, l3_mod_w, l3_mod_b, l3_ret_w_in, l3_ret_log_decay, l3_ret_norm_g, l3_ret_w_out, l3_moe_router, l3_moe_w_in, l3_moe_w_out, final_norm_g):
    B, L, _ = x_prompt.shape
    NS, LS, _ = x_sample.shape
    lay = Layout(B, L, NS, LS)
    x = jnp.concatenate([x_prompt.reshape(B * L, D), x_sample.reshape(NS * LS, D)], axis=0)
    cond = jnp.concatenate([c_ctx[None], c, jnp.zeros((8 - 1 - NS, D), F32)], axis=0)
    mods0 = _mods(cond, l0_mod_w, l0_mod_b)
    p0 = dict(a_re=l0_s5_a_re, a_im=l0_s5_a_im, log_dt=l0_s5_log_dt, b_re=l0_s5_b_re, b_im=l0_s5_b_im,
              c_re=l0_s5_c_re, c_im=l0_s5_c_im, d=l0_s5_d, glu_w=l0_s5_glu_w.astype(BF16))
    x, s5_re, s5_im = _s5_layer(lay, x, mods0, p0, state_l0_s5_re, state_l0_s5_im)
    x = _ffn(lay, x, mods0, l0_ffn_w_in.astype(BF16), l0_ffn_w_out.astype(BF16))

    mods1 = _mods(cond, l1_mod_w, l1_mod_b)
    p1 = dict(w_in=l1_hy_w_in, b_in=l1_hy_b_in, short_w=l1_hy_short_w, short_b=l1_hy_short_b,
              f_w1=l1_hy_f_w1, f_b1=l1_hy_f_b1, f_w2=l1_hy_f_w2, f_b2=l1_hy_f_b2, f_w3=l1_hy_f_w3,
              f_freq=l1_hy_f_freq, skip=l1_hy_skip, w_out=l1_hy_w_out, b_out=l1_hy_b_out)
    x = _hyena_layer(lay, x, mods1, p1)
    x = _moe_layer(lay, x, mods1, l1_moe_router, l1_moe_w_in.astype(BF16), l1_moe_w_out.astype(BF16))

    mods2 = _mods(cond, l2_mod_w, l2_mod_b)
    p2 = dict(w_in=l2_gla_w_in, gate_w1=l2_gla_gate_w1, gate_w2=l2_gla_gate_w2, gate_b=l2_gla_gate_b,
              norm_g=l2_gla_norm_g, w_out=l2_gla_w_out)
    x, gla_state = _gla_layer(lay, x, mods2, p2, state_l2_gla)
    x = _ffn(lay, x, mods2, l2_ffn_w_in.astype(BF16), l2_ffn_w_out.astype(BF16))

    mods3 = _mods(cond, l3_mod_w, l3_mod_b)
    p3 = dict(w_in=l3_ret_w_in, log_decay=l3_ret_log_decay, norm_g=l3_ret_norm_g, w_out=l3_ret_w_out)
    x, ret_state = _ret_layer(lay, x, mods3, p3, state_l3_ret)
    y = _moe_layer(lay, x, mods3, l3_moe_router, l3_moe_w_in.astype(BF16), l3_moe_w_out.astype(BF16),
                   final_g=final_norm_g)
    return (y[:lay.TP].reshape(B, L, D), y[lay.TP:].reshape(NS, LS, D), s5_re, s5_im, gla_state, ret_state)
```

```python
import functools
import math

import jax
import jax.numpy as jnp
import numpy as np
from jax import lax
from jax.experimental import pallas as pl
from jax.experimental.pallas import tpu as pltpu

F32 = jnp.float32
BF16 = jnp.bfloat16
HIGHEST = lax.Precision.HIGHEST

D = 1024
RMS_EPS = 1e-6
MOD_CHUNKS = 6
GRID_W = 64

S5_Q = 16
S5_G = D // S5_Q
S5_P = 64
S5_T = 16
S5_SCAN_ROWS = 64

HY_BANDS = 16
HY_TARGET = 1e-2
HY_FAST_PCT = 0.3
HY_SLOW_PCT = 1.5

GLA_H, GLA_DK, GLA_DV = 4, 128, 256
GLA_RANK = 16
GLA_TAU = 16.0
RET_H, RET_DK, RET_DV = 4, 256, 512
CHUNK = 64
ROPE_BASE = 10000.0

FFN_DIM = 2816
N_EXPERTS = 8
EXPERT_DIM = 3584

VMEM_LIMIT_V7X = 56 * 1024 * 1024


def _cp(*sem):
    return pltpu.CompilerParams(dimension_semantics=sem, vmem_limit_bytes=VMEM_LIMIT_V7X)


def _silu(x):
    return x * jax.nn.sigmoid(x)


def _modulate(x, shift, scale):
    ms = jnp.mean(x * x, axis=-1, keepdims=True)
    return x * lax.rsqrt(ms + RMS_EPS) * (1.0 + scale) + shift


class Layout:
    def __init__(self, n_prompt, l_prompt, n_sample, l_sample):
        self.B, self.L, self.NS, self.LS = n_prompt, l_prompt, n_sample, l_sample
        self.TP = n_prompt * l_prompt
        self.T = self.TP + n_sample * l_sample

    def tile(self, want):
        t = math.gcd(math.gcd(self.TP, self.LS), want)
        assert t % 8 == 0
        return t

    def group(self, row):
        return jnp.where(row < self.TP, 0, 1 + (row - self.TP) // self.LS)


def _mod_spec(lay, tm, chunk, ngrid):
    def imap(*ids):
        return (lay.group(ids[0] * tm) * MOD_CHUNKS + chunk, 0, 0)
    del ngrid
    return pl.BlockSpec((None, 1, D), imap)


def _mods_kernel(c_ref, w_ref, b_ref, o_ref):
    o_ref[...] = jnp.dot(_silu(c_ref[...]), w_ref[...], precision=HIGHEST,
                         preferred_element_type=F32) + b_ref[...]


def _mods(cond, w, b):
    n = MOD_CHUNKS * D
    tn = 1536
    out = pl.pallas_call(
        _mods_kernel,
        out_shape=jax.ShapeDtypeStruct((8, n), F32),
        grid=(n // tn,),
        in_specs=[pl.BlockSpec((8, D), lambda j: (0, 0)),
                  pl.BlockSpec((D, tn), lambda j: (0, j)),
                  pl.BlockSpec((1, tn), lambda j: (0, j))],
        out_specs=pl.BlockSpec((8, tn), lambda j: (0, j)),
        compiler_params=_cp("arbitrary"),
        name="mods",
    )(cond, w, b.reshape(1, n))
    return out.reshape(8 * MOD_CHUNKS, 1, D)


def _modulate_kernel(x_ref, sh_ref, sc_ref, o_ref):
    o_ref[...] = _modulate(x_ref[...], sh_ref[...], sc_ref[...]).astype(o_ref.dtype)


def _modulate_call(lay, x, mods, c_shift, c_scale, dtype):
    tm = lay.tile(512)
    return pl.pallas_call(
        _modulate_kernel,
        out_shape=jax.ShapeDtypeStruct((lay.T, D), dtype),
        grid=(lay.T // tm,),
        in_specs=[pl.BlockSpec((tm, D), lambda i: (i, 0)),
                  _mod_spec(lay, tm, c_shift, 1), _mod_spec(lay, tm, c_scale, 1)],
        out_specs=pl.BlockSpec((tm, D), lambda i: (i, 0)),
        compiler_params=_cp("arbitrary"),
        name="modulate",
    )(x, mods, mods)


def _proj_kernel(x_ref, sh_ref, sc_ref, w_ref, b_ref, o_ref, u_sc):
    @pl.when(pl.program_id(1) == 0)
    def _():
        u_sc[...] = _modulate(x_ref[...], sh_ref[...], sc_ref[...]).astype(BF16)

    acc = jnp.dot(u_sc[...], w_ref[...], preferred_element_type=F32) + b_ref[...]
    o_ref[...] = acc.astype(o_ref.dtype)


def _proj(lay, x, mods, c_shift, c_scale, w, b, tn, out_dtype=BF16):
    tm = lay.tile(1024)
    n = w.shape[1]
    assert n % tn == 0
    return pl.pallas_call(
        _proj_kernel,
        out_shape=jax.ShapeDtypeStruct((lay.T, n), out_dtype),
        grid=(lay.T // tm, n // tn),
        in_specs=[pl.BlockSpec((tm, D), lambda i, j: (i, 0)),
                  _mod_spec(lay, tm, c_shift, 2), _mod_spec(lay, tm, c_scale, 2),
                  pl.BlockSpec((D, tn), lambda i, j: (0, j)),
                  pl.BlockSpec((1, tn), lambda i, j: (0, j))],
        out_specs=pl.BlockSpec((tm, tn), lambda i, j: (i, j)),
        scratch_shapes=[pltpu.VMEM((tm, D), BF16)],
        compiler_params=_cp("arbitrary", "arbitrary"),
        name="proj",
    )(x, mods, mods, w, b)


def _ffn_kernel(nf, x_ref, sh_ref, sc_ref, g_ref, wa_ref, wb_ref, wo_ref, o_ref, u_sc, acc_sc):
    f = pl.program_id(1)

    @pl.when(f == 0)
    def _():
        u_sc[...] = _modulate(x_ref[...], sh_ref[...], sc_ref[...]).astype(BF16)
        acc_sc[...] = jnp.zeros_like(acc_sc)

    u = u_sc[...]
    a = jnp.dot(u, wa_ref[...], preferred_element_type=F32)
    b = jnp.dot(u, wb_ref[...], preferred_element_type=F32)
    h = (_silu(a) * b).astype(BF16)
    acc_sc[...] += jnp.dot(h, wo_ref[...], preferred_element_type=F32)

    @pl.when(f == nf - 1)
    def _():
        o_ref[...] = x_ref[...] + g_ref[...] * acc_sc[...]


def _ffn(lay, x, mods, w_in, w_out):
    tm = lay.tile(512)
    tf = 1408
    nf = FFN_DIM // tf
    return pl.pallas_call(
        functools.partial(_ffn_kernel, nf),
        out_shape=jax.ShapeDtypeStruct((lay.T, D), F32),
        grid=(lay.T // tm, nf),
        in_specs=[pl.BlockSpec((tm, D), lambda i, f: (i, 0)),
                  _mod_spec(lay, tm, 3, 2), _mod_spec(lay, tm, 4, 2), _mod_spec(lay, tm, 5, 2),
                  pl.BlockSpec((D, tf), lambda i, f: (0, f)),
                  pl.BlockSpec((D, tf), lambda i, f: (0, nf + f)),
                  pl.BlockSpec((tf, D), lambda i, f: (f, 0))],
        out_specs=pl.BlockSpec((tm, D), lambda i, f: (i, 0)),
        scratch_shapes=[pltpu.VMEM((tm, D), BF16), pltpu.VMEM((tm, D), F32)],
        compiler_params=_cp("arbitrary", "arbitrary"),
        name="ffn",
    )(x, mods, mods, mods, w_in, w_in, w_out)


def _s5_tables(a_re, a_im, log_dt, b_re, b_im, c_re, c_im, d_skip):
    T, G, P, Q = S5_T, S5_G, S5_P, S5_Q
    a = lax.complex(a_re, a_im)
    adt = a * jnp.exp(log_dt)[..., None]
    lam = jnp.exp(adt)
    bb = ((lam - 1.0) / a)[..., None] * lax.complex(b_re, b_im)
    cm = lax.complex(c_re, c_im)
    steps = jnp.arange(T + 1, dtype=F32)
    pw = jnp.exp(steps[None, :, None, None] * adt[:, None])
    kern = jnp.real(jnp.einsum('dgqp,djgp,dgpr->djgqr', cm, pw[:, :T], bb))
    s_i = jnp.arange(T)[:, None]
    t_i = jnp.arange(T)[None, :]
    kf = kern[0][jnp.clip(t_i - s_i, 0, T - 1)] * (t_i >= s_i)[..., None, None, None]
    kb = kern[1][jnp.clip(s_i - t_i, 0, T - 1)] * (s_i >= t_i)[..., None, None, None]
    m = jnp.transpose(kf + kb, (2, 0, 4, 1, 3))
    eye = (jnp.eye(T)[:, None, :, None] * jnp.eye(Q)[None, :, None, :])
    m = m + eye[None] * d_skip.reshape(G, 1, 1, 1, Q)
    m = m.reshape(G, T * Q, T * Q)
    e_f = pw[0][T - 1 - jnp.arange(T)]
    e_b = pw[1][jnp.arange(T)]
    n_f = e_f[..., None] * bb[0][None]
    n_b = e_b[..., None] * bb[1][None]
    n_c = jnp.concatenate([n_f, n_b], axis=2)
    n_c = jnp.transpose(n_c, (1, 0, 3, 2)).reshape(G, T * Q, 2 * P)
    lam_t = jnp.concatenate([pw[0][T], pw[1][T]], axis=-1)
    w_f = cm[0][:, None] * jnp.transpose(pw[0][1:T + 1], (1, 0, 2))[:, :, None, :]
    w_b = cm[1][:, None] * jnp.transpose(pw[1][T - jnp.arange(T)], (1, 0, 2))[:, :, None, :]
    w_f = jnp.transpose(w_f, (0, 3, 1, 2)).reshape(G, P, T * Q)
    w_b = jnp.transpose(w_b, (0, 3, 1, 2)).reshape(G, P, T * Q)
    z = jnp.zeros_like(jnp.real(w_f))
    c_mats = dict(c_f_re=jnp.concatenate([jnp.real(w_f), z], axis=1),
                  c_f_im=jnp.concatenate([-jnp.imag(w_f), z], axis=1),
                  c_b_re=jnp.concatenate([z, jnp.real(w_b)], axis=1),
                  c_b_im=jnp.concatenate([z, -jnp.imag(w_b)], axis=1))
    return dict(m=m.astype(BF16), n_re=jnp.real(n_c).astype(BF16), n_im=jnp.imag(n_c).astype(BF16),
                l_re=jnp.real(lam_t), l_im=jnp.imag(lam_t), **{k: v.astype(BF16) for k, v in c_mats.items()})


def _s5_in_kernel(u_ref, m_ref, nre_ref, nim_ref, yi_ref, sre_ref, sim_ref):
    u = u_ref[...]
    yi_ref[...] = jnp.dot(u, m_ref[...], preferred_element_type=F32)
    sre_ref[...] = jnp.dot(u, nre_ref[...], preferred_element_type=F32)
    sim_ref[...] = jnp.dot(u, nim_ref[...], preferred_element_type=F32)


def _s5_in(ug, tabs):
    G, R, W = ug.shape
    P2 = 2 * S5_P
    gspec = lambda n: pl.BlockSpec((None, W, n), lambda g: (g, 0, 0))
    rspec = lambda n: pl.BlockSpec((None, R, n), lambda g: (g, 0, 0))
    return pl.pallas_call(
        _s5_in_kernel,
        out_shape=(jax.ShapeDtypeStruct((G, R, W), F32),
                   jax.ShapeDtypeStruct((G, R, P2), F32),
                   jax.ShapeDtypeStruct((G, R, P2), F32)),
        grid=(G,),
        in_specs=[rspec(W), gspec(W), gspec(P2), gspec(P2)],
        out_specs=(rspec(W), rspec(P2), rspec(P2)),
        compiler_params=_cp("arbitrary"),
        name="s5_in",
    )(ug, tabs['m'], tabs['n_re'], tabs['n_im'])


def _s5_scan_kernel(nsb, ncb, nblk, sref_ref, simf_ref, sreb_ref, simb_ref, lre_ref, lim_ref,
                    h0re_ref, h0im_ref, *rest):
    hfre_ref, hfim_ref, hbre_ref, hbim_ref, fre_ref, fim_ref, cre_sc, cim_sc = rest[4:]
    P = S5_P
    j = pl.program_id(1)
    fwd = lax.broadcasted_iota(jnp.int32, (1, 1, 2 * P), 2) < P
    lre = lre_ref[...][:, None, :]
    lim = lim_ref[...][:, None, :]

    @pl.when(j == 0)
    def _():
        cre_sc[...] = h0re_ref[...]
        cim_sc[...] = h0im_ref[...]

    def at(k):
        return pl.ds(k, 1) if nsb == 1 else pl.ds(k, nsb, stride=ncb)

    def body(k, carry):
        hre, him = carry
        kb = ncb - 1 - k
        hfre_ref[:, at(k), :] = hre
        hfim_ref[:, at(k), :] = him
        hbre_ref[:, at(kb), :] = hre
        hbim_ref[:, at(kb), :] = him
        sre = jnp.where(fwd, sref_ref[:, at(k), :], sreb_ref[:, at(kb), :])
        sim = jnp.where(fwd, simf_ref[:, at(k), :], simb_ref[:, at(kb), :])
        return (lre * hre - lim * him + sre, lre * him + lim * hre + sim)

    hre, him = lax.fori_loop(0, ncb, body, (cre_sc[...], cim_sc[...]))
    cre_sc[...] = hre
    cim_sc[...] = him

    @pl.when(j == nblk - 1)
    def _():
        fre_ref[...] = hre
        fim_ref[...] = him


def _s5_scan(sre, sim, tabs, h0re, h0im, hprev, row0, nseq, nc, nsb, nblk):
    G, R, P2 = sre.shape
    assert nblk == 1 or nsb == 1
    ncb = nc // nblk
    rb = nsb * ncb
    assert row0 % rb == 0 and nseq % nsb == 0 and nc % nblk == 0
    b0 = row0 // rb
    fspec = pl.BlockSpec((G, rb, P2), lambda i, j: (0, b0 + i * nblk + j, 0))
    bspec = pl.BlockSpec((G, rb, P2), lambda i, j: (0, b0 + i * nblk + nblk - 1 - j, 0))
    lspec = pl.BlockSpec((G, P2), lambda i, j: (0, 0))
    qspec = pl.BlockSpec((None, G, nsb, P2), lambda i, j: (i, 0, 0, 0))
    anyspec = pl.BlockSpec(memory_space=pl.ANY)
    fin = jax.ShapeDtypeStruct((nseq // nsb, G, nsb, P2), F32)
    outs = pl.pallas_call(
        functools.partial(_s5_scan_kernel, nsb, ncb, nblk),
        out_shape=tuple(jax.ShapeDtypeStruct(h.shape, h.dtype) for h in hprev) + (fin, fin),
        grid=(nseq // nsb, nblk),
        in_specs=[fspec, fspec, bspec, bspec, lspec, lspec, qspec, qspec] + [anyspec] * 4,
        out_specs=(fspec, fspec, bspec, bspec, qspec, qspec),
        scratch_shapes=[pltpu.VMEM((G, nsb, P2), F32), pltpu.VMEM((G, nsb, P2), F32)],
        input_output_aliases={8: 0, 9: 1, 10: 2, 11: 3},
        compiler_params=_cp("arbitrary", "arbitrary"),
        name="s5_scan",
    )(sre, sim, sre, sim, tabs['l_re'], tabs['l_im'], h0re, h0im, *hprev)
    return outs[:4], outs[4], outs[5]


def _s5_out_kernel(yi_ref, hfre_ref, hfim_ref, hbre_ref, hbim_ref, cfre_ref, cfim_ref, cbre_ref, cbim_ref,
                   y_ref):
    y = yi_ref[...]
    for h_ref, c_ref in ((hfre_ref, cfre_ref), (hfim_ref, cfim_ref), (hbre_ref, cbre_ref), (hbim_ref, cbim_ref)):
        y += jnp.dot(h_ref[...].astype(BF16), c_ref[...], preferred_element_type=F32)
    y_ref[...] = y.astype(y_ref.dtype)


def _s5_out(yi, hprev, tabs):
    G, R, W = yi.shape
    P2 = 2 * S5_P
    gspec = pl.BlockSpec((None, P2, W), lambda g: (g, 0, 0))
    hspec = pl.BlockSpec((None, R, P2), lambda g: (g, 0, 0))
    rspec = pl.BlockSpec((None, R, W), lambda g: (g, 0, 0))
    return pl.pallas_call(
        _s5_out_kernel,
        out_shape=jax.ShapeDtypeStruct((G, R, W), BF16),
        grid=(G,),
        in_specs=[rspec] + [hspec] * 4 + [gspec] * 4,
        out_specs=rspec,
        compiler_params=_cp("arbitrary"),
        name="s5_out",
    )(yi, *hprev, tabs['c_f_re'], tabs['c_f_im'], tabs['c_b_re'], tabs['c_b_im'])


def _s5_glu_kernel(x_ref, y_ref, g_ref, wv_ref, wg_ref, o_ref, a_sc):
    @pl.when(pl.program_id(1) == 0)
    def _():
        a_sc[...] = jax.nn.gelu(y_ref[...].astype(F32)).astype(BF16)

    a = a_sc[...]
    val = jnp.dot(a, wv_ref[...], preferred_element_type=F32)
    gate = jnp.dot(a, wg_ref[...], preferred_element_type=F32)
    o_ref[...] = x_ref[...] + g_ref[...] * (val * jax.nn.sigmoid(gate))


def _s5_glu(lay, x, y, mods, glu_w):
    tm = lay.tile(1024)
    tn = 512
    nn = D // tn
    return pl.pallas_call(
        _s5_glu_kernel,
        out_shape=jax.ShapeDtypeStruct((lay.T, D), F32),
        grid=(lay.T // tm, nn),
        in_specs=[pl.BlockSpec((tm, tn), lambda i, j: (i, j)),
                  pl.BlockSpec((tm, D), lambda i, j: (i, 0)),
                  pl.BlockSpec((None, 1, tn), lambda i, j: (lay.group(i * tm) * MOD_CHUNKS + 2, 0, j)),
                  pl.BlockSpec((D, tn), lambda i, j: (0, j)),
                  pl.BlockSpec((D, tn), lambda i, j: (0, nn + j))],
        out_specs=pl.BlockSpec((tm, tn), lambda i, j: (i, j)),
        scratch_shapes=[pltpu.VMEM((tm, D), BF16)],
        compiler_params=_cp("arbitrary", "arbitrary"),
        name="s5_glu",
    )(x, y, mods, glu_w, glu_w)


def _s5_layer(lay, x, mods, p, h0_re, h0_im):
    T, G, P, Q = S5_T, S5_G, S5_P, S5_Q
    tabs = _s5_tables(p['a_re'], p['a_im'], p['log_dt'], p['b_re'], p['b_im'], p['c_re'], p['c_im'], p['d'])
    u = _modulate_call(lay, x, mods, 0, 1, BF16)
    R = lay.T // T
    ug = jnp.transpose(u.reshape(R, T, G, Q), (2, 0, 1, 3)).reshape(G, R, T * Q)
    yi, sre, sim = _s5_in(ug, tabs)
    hprev = tuple(jnp.zeros((G, R, 2 * P), F32) for _ in range(4))
    ncp, ncs = lay.L // T, lay.LS // T
    nsb = math.gcd(lay.B, max(1, S5_SCAN_ROWS // ncp))
    zero = jnp.zeros((lay.B // nsb, G, nsb, 2 * P), F32)
    hprev, fre, fim = _s5_scan(sre, sim, tabs, zero, zero, hprev, 0, lay.B, ncp, nsb, 1)
    to_lanes = lambda s: jnp.transpose(s, (0, 2, 1, 3)).reshape(lay.NS, G, 1, 2 * P)
    hprev, _, _ = _s5_scan(sre, sim, tabs, to_lanes(h0_re), to_lanes(h0_im), hprev,
                           lay.TP // T, lay.NS, ncs, 1, max(1, ncs // S5_SCAN_ROWS))
    yg = _s5_out(yi, hprev, tabs)
    y = jnp.transpose(yg.reshape(G, R, T, Q), (1, 2, 0, 3)).reshape(lay.T, D)
    x = _s5_glu(lay, x, y, mods, p['glu_w'])
    from_lanes = lambda s: jnp.transpose(s, (0, 2, 1, 3)).reshape(lay.B, G, 2, P).transpose(0, 2, 1, 3)
    return x, from_lanes(fre), from_lanes(fim)


def _hyena_filters(L, p):
    t = jnp.linspace(0.0, 1.0, L, dtype=F32)[:, None]
    w = 2.0 * math.pi * jnp.arange(L, dtype=F32)[:, None] / L
    f = jnp.linspace(1e-4, HY_BANDS - 1, HY_BANDS, dtype=F32)[None, :]
    feats = jnp.concatenate([t, jnp.cos(f * w), -jnp.sin(f * w)], axis=-1)
    mm = functools.partial(jnp.matmul, precision=HIGHEST)
    z = jnp.sin(p['f_freq'][0] * (mm(feats, p['f_w1']) + p['f_b1']))
    z = jnp.sin(p['f_freq'][1] * (mm(z, p['f_w2']) + p['f_b2']))
    h = mm(z, p['f_w3']).reshape(L, 2, 2, D)
    max_decay = math.log(HY_TARGET) / HY_FAST_PCT
    min_decay = math.log(HY_TARGET) / HY_SLOW_PCT
    deltas = jnp.abs(jnp.linspace(min_decay, max_decay, D, dtype=F32))
    h = h * jnp.exp(-t[:, :, None, None] * deltas)
    k_lo = h[:, :, 0]
    k_hi = jnp.concatenate([jnp.zeros((1, 2, D), F32), h[:0:-1, :, 1]], axis=0)
    norm = jnp.sum(jnp.abs(k_lo), axis=0, keepdims=True) + jnp.sum(jnp.abs(k_hi), axis=0, keepdims=True)
    k_lo, k_hi = k_lo / norm, k_hi / norm
    alt = (1.0 - 2.0 * (jnp.arange(L) % 2).astype(F32))[:, None, None]
    k_ny = jnp.sum(alt * (k_lo + k_hi), axis=0) / (2 * L)
    return jnp.transpose(k_lo, (1, 0, 2)), jnp.transpose(k_hi, (1, 0, 2)), k_ny


def _dft_tables(L):
    i = jnp.arange(L, dtype=jnp.int32)
    idx = (i[:, None] * i[None, :]) % (2 * L)
    ang = idx.astype(F32) * (math.pi / L)
    return jnp.cos(ang).astype(BF16), jnp.sin(ang).astype(BF16)


def _hy_spec_kernel(L, tr, c_ref, s_ref, klo_ref, khi_ref, p_ref, q_ref):
    r = pl.program_id(2)
    f = r * tr + lax.broadcasted_iota(jnp.int32, (tr, 1), 0)
    sgn = (1 - 2 * (f % 2)).astype(F32)
    scale = jnp.where(f == 0, 1.0, 2.0) * (1.0 / (2 * L))
    c, s = c_ref[...], s_ref[...]
    lo, hi = klo_ref[...], khi_ref[...]
    dot = functools.partial(jnp.dot, preferred_element_type=F32)
    p_ref[...] = scale * (dot(c, lo) + sgn * dot(c, hi))
    q_ref[...] = scale * (dot(s, lo) + sgn * dot(s, hi))


def _hy_spectrum(L, cos, sin, k_lo, k_hi):
    tr = min(L, 512)
    tc = 512
    kspec = pl.BlockSpec((None, L, tc), lambda o, j, r: (o, 0, j))
    tspec = pl.BlockSpec((tr, L), lambda o, j, r: (r, 0))
    ospec = pl.BlockSpec((None, tr, tc), lambda o, j, r: (o, r, j))
    return pl.pallas_call(
        functools.partial(_hy_spec_kernel, L, tr),
        out_shape=(jax.ShapeDtypeStruct((2, L, D), F32), jax.ShapeDtypeStruct((2, L, D), F32)),
        grid=(2, D // tc, L // tr),
        in_specs=[tspec, tspec, kspec, kspec],
        out_specs=(ospec, ospec),
        compiler_params=_cp("arbitrary", "arbitrary", "arbitrary"),
        name="hy_spectrum",
    )(cos, sin, k_lo.astype(BF16), k_hi.astype(BF16))


def _hy_core_kernel(L, tr, ngrp, tc, x1_ref, x2_ref, v_ref, sw1_ref, sw2_ref, swv_ref, sb1_ref, sb2_ref,
                    sbv_ref, c_ref, s_ref, p_ref, q_ref, kny_ref, skip_ref, *rest):
    o_ref, z0_sc, z1_sc, x1_sc, x2_sc, a_sc, b_sc, ny_sc = rest[-8:]
    ph = pl.program_id(2)
    r = pl.program_id(3)
    nrt = L // tr
    W = ngrp * tc
    row = lax.broadcasted_iota(jnp.int32, (L, 1), 0)
    alt_all = (1 - 2 * (row % 2)).astype(F32)

    def conv3(src_ref, g, w_ref, b_ref):
        x = src_ref[g].astype(F32)
        prev = jnp.where(row == 0, 0.0, pltpu.roll(x, 1, 0))
        nxt = jnp.where(row == L - 1, 0.0, pltpu.roll(x, L - 1, 0))
        return prev * w_ref[0:1, :] + x * w_ref[1:2, :] + nxt * w_ref[2:3, :] + b_ref[...]

    @pl.when(jnp.logical_and(ph == 0, r == 0))
    def _():
        for g in range(ngrp):
            cols = slice(g * tc, (g + 1) * tc)
            x1_sc[:, cols] = conv3(x1_ref, g, sw1_ref, sb1_ref).astype(BF16)
            x2_sc[:, cols] = conv3(x2_ref, g, sw2_ref, sb2_ref).astype(BF16)
            z0_sc[:, cols] = conv3(v_ref, g, swv_ref, sbv_ref).astype(BF16)
        ny_sc[...] = jnp.sum(alt_all * z0_sc[...].astype(F32), axis=0, keepdims=True)

    rows = pl.ds(pl.multiple_of(r * tr, tr), tr)
    tile = lambda ref: jnp.concatenate([ref[...]] * ngrp, axis=1)
    dot = functools.partial(jnp.dot, preferred_element_type=F32)

    def forward(z_sc):
        z = z_sc[...]
        zre = dot(c_ref[...], z)
        zim = dot(s_ref[...], z)
        pw, qw = tile(p_ref), tile(q_ref)
        a_sc[rows, :] = (zre * pw - zim * qw).astype(BF16)
        b_sc[rows, :] = (zim * pw + zre * qw).astype(BF16)

    def inverse(order, z_sc):
        y = dot(c_ref[...], a_sc[...]) + dot(s_ref[...], b_sc[...])
        t = r * tr + lax.broadcasted_iota(jnp.int32, (tr, 1), 0)
        alt = (1 - 2 * (t % 2)).astype(F32)
        kny = jnp.concatenate([kny_ref[order:order + 1, :]] * ngrp, axis=1)
        skip = jnp.concatenate([skip_ref[order:order + 1, :]] * ngrp, axis=1)
        return y + alt * (ny_sc[...] * kny) + skip * z_sc[rows, :].astype(F32)

    @pl.when(ph == 0)
    def _():
        forward(z0_sc)

    @pl.when(ph == 1)
    def _():
        z1_sc[rows, :] = (x1_sc[rows, :].astype(F32) * inverse(0, z0_sc)).astype(BF16)

        @pl.when(r == nrt - 1)
        def _():
            ny_sc[...] = jnp.sum(alt_all * z1_sc[...].astype(F32), axis=0, keepdims=True)

    @pl.when(ph == 2)
    def _():
        forward(z1_sc)

    @pl.when(ph == 3)
    def _():
        out = x2_sc[rows, :].astype(F32) * inverse(1, z1_sc)
        for g in range(ngrp):
            o_ref[g, rows, :] = out[:, g * tc:(g + 1) * tc].astype(o_ref.dtype)


def _hy_core(proj, short_w, short_b, skip, cos, sin, pq, k_ny, o_prev, row0, nseq, L, ngrp, tc):
    T = proj.shape[0]
    tr = min(L, 512)
    nrt = L // tr
    assert row0 % (L * ngrp) == 0 and nseq % ngrp == 0 and T % L == 0
    sb0 = row0 // (L * ngrp)
    nct = D // tc
    p3 = proj.reshape(T // L, L, 3 * D)
    p_arr, q_arr = pq

    def xspec(part):
        return pl.BlockSpec((ngrp, L, tc), lambda i, j, ph, r: (sb0 + i, 0, part * nct + j))

    def wspec(part, rows_):
        return pl.BlockSpec((rows_, tc), lambda i, j, ph, r: (0, part * nct + j))

    def pq_map(i, j, ph, r):
        return (ph // 2, jnp.where(ph % 2 == 0, r, nrt - 1), j)

    tspec = pl.BlockSpec((tr, L), lambda i, j, ph, r: (r, 0))
    in_specs = [xspec(0), xspec(1), xspec(2), wspec(0, 3), wspec(1, 3), wspec(2, 3),
                wspec(0, 1), wspec(1, 1), wspec(2, 1), tspec, tspec,
                pl.BlockSpec((None, tr, tc), pq_map), pl.BlockSpec((None, tr, tc), pq_map),
                pl.BlockSpec((2, tc), lambda i, j, ph, r: (0, j)),
                pl.BlockSpec((2, tc), lambda i, j, ph, r: (0, j))]
    sb = short_b.reshape(1, 3 * D)
    args = [p3, p3, p3, short_w, short_w, short_w, sb, sb, sb, cos, sin, p_arr, q_arr, k_ny, skip]
    aliases = {}
    if o_prev is not None:
        in_specs.append(pl.BlockSpec(memory_space=pl.ANY))
        args.append(o_prev.reshape(T // L, L, D))
        aliases = {len(args) - 1: 0}
    W = ngrp * tc
    out = pl.pallas_call(
        functools.partial(_hy_core_kernel, L, tr, ngrp, tc),
        out_shape=jax.ShapeDtypeStruct((T // L, L, D), BF16),
        grid=(nseq // ngrp, nct, 4, nrt),
        in_specs=in_specs,
        out_specs=pl.BlockSpec((ngrp, L, tc), lambda i, j, ph, r: (sb0 + i, 0, j)),
        scratch_shapes=[pltpu.VMEM((L, W), BF16)] * 6 + [pltpu.VMEM((1, W), F32)],
        input_output_aliases=aliases,
        compiler_params=_cp("arbitrary", "arbitrary", "arbitrary", "arbitrary"),
        name="hy_core",
    )(*args)
    return out.reshape(T, D)


def _plain_out_kernel(x_ref, z_ref, g_ref, w_ref, b_ref, o_ref):
    acc = jnp.dot(z_ref[...], w_ref[...], preferred_element_type=F32) + b_ref[...]
    o_ref[...] = x_ref[...] + g_ref[...] * acc


def _plain_out(lay, x, mods, z, w, b):
    tm = lay.tile(1024)
    tn = 512
    kdim = z.shape[1]
    return pl.pallas_call(
        _plain_out_kernel,
        out_shape=jax.ShapeDtypeStruct((lay.T, D), F32),
        grid=(lay.T // tm, D // tn),
        in_specs=[pl.BlockSpec((tm, tn), lambda i, j: (i, j)),
                  pl.BlockSpec((tm, kdim), lambda i, j: (i, 0)),
                  pl.BlockSpec((None, 1, tn), lambda i, j: (lay.group(i * tm) * MOD_CHUNKS + 2, 0, j)),
                  pl.BlockSpec((kdim, tn), lambda i, j: (0, j)),
                  pl.BlockSpec((1, tn), lambda i, j: (0, j))],
        out_specs=pl.BlockSpec((tm, tn), lambda i, j: (i, j)),
        compiler_params=_cp("arbitrary", "arbitrary"),
        name="plain_out",
    )(x, z, mods, w, b.reshape(1, D))


def _hyena_layer(lay, x, mods, p):
    proj = _proj(lay, x, mods, 0, 1, p['w_in'].astype(BF16), p['b_in'].reshape(1, 3 * D), 768)
    z = jnp.zeros((lay.T, D), BF16)
    for row0, nseq, L, ngrp, tc in ((0, lay.B, lay.L, math.gcd(lay.B, 4), 256),
                                    (lay.TP, lay.NS, lay.LS, lay.NS, 128)):
        k_lo, k_hi, k_ny = _hyena_filters(L, p)
        cos, sin = _dft_tables(L)
        pq = _hy_spectrum(L, cos, sin, k_lo, k_hi)
        z = _hy_core(proj, p['short_w'], p['short_b'], p['skip'], cos, sin, pq, k_ny, z, row0, nseq, L, ngrp, tc)
    return _plain_out(lay, x, mods, z, p['w_out'].astype(BF16), p['b_out'])


_NT = (((1,), (1,)), ((), ()))
_TN = (((0,), (0,)), ((), ()))


def _tri(dr):
    t = lax.broadcasted_iota(jnp.int32, (CHUNK, CHUNK), 0)
    s = lax.broadcasted_iota(jnp.int32, (CHUNK, CHUNK), 1)
    return (s <= t) if dr == 0 else (s >= t)


def _gla_kernel(nc, has_s0, *refs):
    q_ref, k_ref, v_ref, lr_ref, w2f_ref, w2b_ref, gb_ref = refs[:7]
    s0_ref = refs[7] if has_s0 else None
    o_ref, sf_ref, st_sc = refs[-3:]
    C = CHUNK
    w2 = (w2f_ref, w2b_ref)
    for dr in range(2):
        if has_s0:
            st_sc[dr] = jnp.transpose(s0_ref[dr], (1, 0))
        else:
            st_sc[dr] = jnp.zeros(st_sc.shape[1:], F32)
    o_ref[...] = jnp.zeros_like(o_ref)

    def chunk(dr, r0):
        rows = pl.ds(r0, C)
        q = q_ref[rows, :].astype(F32) * (GLA_DK ** -0.5)
        k = k_ref[rows, :].astype(F32)
        v = v_ref[rows, :]
        pre = jnp.dot(lr_ref[rows, :], w2[dr][...], preferred_element_type=F32) + gb_ref[dr:dr + 1, :]
        g = (jnp.minimum(pre, 0.0) - jnp.log(1.0 + jnp.exp(-jnp.abs(pre)))) * (1.0 / GLA_TAU)
        tri = _tri(dr)
        b = jnp.dot(tri.astype(F32), g, precision=HIGHEST, preferred_element_type=F32)
        btot = b[C - 1:C, :] if dr == 0 else b[0:1, :]
        q_in = (q * jnp.exp(b)).astype(BF16)
        k_in = (k * jnp.exp(-b)).astype(BF16)
        k_out = (k * jnp.exp(btot - b)).astype(BF16)
        sc = lax.dot_general(q_in, k_in, _NT, preferred_element_type=F32)
        sc = jnp.where(tri, sc, 0.0).astype(BF16)
        st = st_sc[dr]
        o = jnp.dot(sc, v, preferred_element_type=F32)
        o += lax.dot_general(q_in, st.astype(BF16), _NT, preferred_element_type=F32)
        st_sc[dr] = jnp.exp(btot) * st + lax.dot_general(v, k_out, _TN, preferred_element_type=F32)
        o_ref[rows, :] += o

    def body(i, carry):
        chunk(0, pl.multiple_of(i * C, C))
        chunk(1, pl.multiple_of((nc - 1 - i) * C, C))
        return carry

    lax.fori_loop(0, nc, body, 0)
    for dr in range(2):
        sf_ref[dr] = jnp.transpose(st_sc[dr], (1, 0))


def _gla_core(proj, w2f, w2b, gate_b, s0, o_prev, row0, nseq, seqlen):
    T = proj.shape[0]
    assert row0 % seqlen == 0 and seqlen % CHUNK == 0
    rb = row0 // seqlen
    hk = GLA_H * GLA_DK
    has_s0 = s0 is not None
    in_specs = [pl.BlockSpec((seqlen, GLA_DK), lambda b, h: (rb + b, h)),
                pl.BlockSpec((seqlen, GLA_DK), lambda b, h: (rb + b, GLA_H + h)),
                pl.BlockSpec((seqlen, GLA_DV), lambda b, h: (rb + b, 2 * hk // GLA_DV + h)),
                pl.BlockSpec((seqlen, 128), lambda b, h: (rb + b, (2 * hk + 2 * GLA_H * GLA_DV) // 128)),
                pl.BlockSpec((128, GLA_DK), lambda b, h: (0, h)),
                pl.BlockSpec((128, GLA_DK), lambda b, h: (0, h)),
                pl.BlockSpec((2, GLA_DK), lambda b, h: (0, h))]
    args = [proj, proj, proj, proj, w2f, w2b, gate_b]
    sspec = pl.BlockSpec((None, 2, None, GLA_DK, GLA_DV), lambda b, h: (b, 0, h, 0, 0))
    if has_s0:
        in_specs.append(sspec)
        args.append(s0)
    in_specs.append(pl.BlockSpec(memory_space=pl.ANY))
    args.append(o_prev)
    aliases = {len(args) - 1: 0}
    return pl.pallas_call(
        functools.partial(_gla_kernel, seqlen // CHUNK, has_s0),
        out_shape=(jax.ShapeDtypeStruct((T, GLA_H * GLA_DV), F32),
                   jax.ShapeDtypeStruct((nseq, 2, GLA_H, GLA_DK, GLA_DV), F32)),
        grid=(nseq, GLA_H),
        in_specs=in_specs,
        out_specs=(pl.BlockSpec((seqlen, GLA_DV), lambda b, h: (rb + b, h)), sspec),
        scratch_shapes=[pltpu.VMEM((2, GLA_DV, GLA_DK), F32)],
        input_output_aliases=aliases,
        compiler_params=_cp("arbitrary", "arbitrary"),
        name="gla",
    )(*args)


def _ret_kernel(nc, has_s0, rope, *refs):
    refs = list(refs)
    q_ref, k_ref, v_ref, dm_ref, qd_ref, kd_ref, cd_ref = refs[:7]
    del refs[:7]
    if rope:
        cos_ref, sin_ref = refs[:2]
        del refs[:2]
    s0_ref = refs.pop(0) if has_s0 else None
    o_ref, sf_ref, s_sc = refs[-3:]
    C = CHUNK
    for dr in range(2):
        s_sc[dr] = s0_ref[dr] if has_s0 else jnp.zeros(s_sc.shape[1:], F32)
    o_ref[...] = jnp.zeros_like(o_ref)

    def rot(x, rows):
        if not rope:
            return x
        half = x.shape[1] // 2
        swapped = jnp.concatenate([pltpu.roll(x[:, :half], half // 2, 1),
                                   pltpu.roll(x[:, half:], half // 2, 1)], axis=1)
        return x * cos_ref[rows, :] + swapped * sin_ref[rows, :]

    def chunk(dr, r0):
        rows = pl.ds(r0, C)
        q = rot(q_ref[rows, :].astype(F32), rows)
        k = rot(k_ref[rows, :].astype(F32), rows) * (RET_DK ** -0.5)
        v = v_ref[rows, :]
        sc = lax.dot_general(q.astype(BF16), k.astype(BF16), _NT, preferred_element_type=F32)
        sc = (sc * dm_ref[dr]).astype(BF16)
        s = s_sc[dr]
        o = jnp.dot(sc, v, preferred_element_type=F32)
        o += jnp.dot((q * qd_ref[dr]).astype(BF16), s.astype(BF16), preferred_element_type=F32)
        kv = lax.dot_general((k * kd_ref[dr]).astype(BF16), v, _TN, preferred_element_type=F32)
        s_sc[dr] = cd_ref[dr] * s + kv
        o_ref[rows, :] += o

    def body(i, carry):
        chunk(0, pl.multiple_of(i * C, C))
        chunk(1, pl.multiple_of((nc - 1 - i) * C, C))
        return carry

    lax.fori_loop(0, nc, body, 0)
    for dr in range(2):
        sf_ref[dr] = s_sc[dr]


def _ret_tables(log_decay):
    C = CHUNK
    lg = log_decay.astype(F32)[:, :, None, None]
    t = jnp.arange(C, dtype=F32)[:, None]
    s = jnp.arange(C, dtype=F32)[None, :]
    lag = jnp.stack([t - s, s - t])[:, None]
    dmask = jnp.where(lag >= 0, jnp.exp(jnp.maximum(lag, 0.0) * lg), 0.0)
    tl = jnp.arange(C, dtype=F32)[None, None, :, None]
    qdec = jnp.concatenate([jnp.exp((tl + 1.0) * lg[0:1]), jnp.exp((C - tl) * lg[1:2])], axis=0)
    kdec = jnp.concatenate([jnp.exp((C - 1.0 - tl) * lg[0:1]), jnp.exp(tl * lg[1:2])], axis=0)
    cdec = jnp.exp(C * lg)
    return dmask, qdec, kdec, cdec


def _rope_tables(seqlen, dk):
    half = dk // 2
    nf = half // 2
    pos = jnp.arange(seqlen, dtype=jnp.int32)
    inv = ROPE_BASE ** (-jnp.arange(nf, dtype=F32) / nf)
    ang_r = (pos // GRID_W).astype(F32)[:, None] * inv[None, :]
    ang_c = (pos % GRID_W).astype(F32)[:, None] * inv[None, :]
    cos = jnp.concatenate([jnp.cos(ang_r)] * 2 + [jnp.cos(ang_c)] * 2, axis=1)
    sin = jnp.concatenate([-jnp.sin(ang_r), jnp.sin(ang_r), -jnp.sin(ang_c), jnp.sin(ang_c)], axis=1)
    return cos, sin


def _ret_core(proj, tabs, s0, o_prev, row0, nseq, seqlen, rope):
    T = proj.shape[0]
    assert row0 % seqlen == 0 and seqlen % CHUNK == 0
    rb = row0 // seqlen
    hk = RET_H * RET_DK
    has_s0 = s0 is not None
    C = CHUNK
    tspec = lambda r, c: pl.BlockSpec((2, None, r, c), lambda b, h: (0, h, 0, 0))
    in_specs = [pl.BlockSpec((seqlen, RET_DK), lambda b, h: (rb + b, h)),
                pl.BlockSpec((seqlen, RET_DK), lambda b, h: (rb + b, RET_H + h)),
                pl.BlockSpec((seqlen, RET_DV), lambda b, h: (rb + b, 2 * hk // RET_DV + h)),
                tspec(C, C), tspec(C, 1), tspec(C, 1), tspec(1, 1)]
    args = [proj, proj, proj, *tabs]
    if rope:
        cos, sin = _rope_tables(seqlen, RET_DK)
        in_specs += [pl.BlockSpec((seqlen, RET_DK), lambda b, h: (0, 0), pipeline_mode=pl.Buffered(1))] * 2
        args += [cos, sin]
    sspec = pl.BlockSpec((None, 2, None, RET_DK, RET_DV), lambda b, h: (b, 0, h, 0, 0))
    if has_s0:
        in_specs.append(sspec)
        args.append(s0)
    in_specs.append(pl.BlockSpec(memory_space=pl.ANY))
    args.append(o_prev)
    aliases = {len(args) - 1: 0}
    return pl.pallas_call(
        functools.partial(_ret_kernel, seqlen // C, has_s0, rope),
        out_shape=(jax.ShapeDtypeStruct((T, RET_H * RET_DV), F32),
                   jax.ShapeDtypeStruct((nseq, 2, RET_H, RET_DK, RET_DV), F32)),
        grid=(nseq, RET_H),
        in_specs=in_specs,
        out_specs=(pl.BlockSpec((seqlen, RET_DV), lambda b, h: (rb + b, h)), sspec),
        scratch_shapes=[pltpu.VMEM((2, RET_DK, RET_DV), F32)],
        input_output_aliases=aliases,
        compiler_params=_cp("arbitrary", "arbitrary"),
        name="ret",
    )(*args)


def _headnorm_out_kernel(nh, dv, center, x_ref, o_ref, gt_ref, ng_ref, g_ref, w_ref, out_ref, a_sc):
    @pl.when(pl.program_id(1) == 0)
    def _():
        for h in range(nh):
            cols = slice(h * dv, (h + 1) * dv)
            oh = o_ref[:, cols]
            if center:
                oh = oh - jnp.mean(oh, axis=-1, keepdims=True)
            oh = oh * lax.rsqrt(jnp.mean(oh * oh, axis=-1, keepdims=True) + RMS_EPS) * ng_ref[...]
            a_sc[:, cols] = (oh * _silu(gt_ref[:, cols].astype(F32))).astype(BF16)

    out_ref[...] = x_ref[...] + g_ref[...] * jnp.dot(a_sc[...], w_ref[...], preferred_element_type=F32)


def _headnorm_out(lay, x, mods, o, proj, gate_col, norm_g, w_out, nh, dv, center):
    tm = lay.tile(512)
    tn = 512
    kdim = nh * dv
    assert gate_col % kdim == 0
    return pl.pallas_call(
        functools.partial(_headnorm_out_kernel, nh, dv, center),
        out_shape=jax.ShapeDtypeStruct((lay.T, D), F32),
        grid=(lay.T // tm, D // tn),
        in_specs=[pl.BlockSpec((tm, tn), lambda i, j: (i, j)),
                  pl.BlockSpec((tm, kdim), lambda i, j: (i, 0)),
                  pl.BlockSpec((tm, kdim), lambda i, j: (i, gate_col // kdim)),
                  pl.BlockSpec((1, dv), lambda i, j: (0, 0)),
                  pl.BlockSpec((None, 1, tn), lambda i, j: (lay.group(i * tm) * MOD_CHUNKS + 2, 0, j)),
                  pl.BlockSpec((kdim, tn), lambda i, j: (0, j))],
        out_specs=pl.BlockSpec((tm, tn), lambda i, j: (i, j)),
        scratch_shapes=[pltpu.VMEM((tm, kdim), BF16)],
        compiler_params=_cp("arbitrary", "arbitrary"),
        name="headnorm_out",
    )(x, o, proj, norm_g.reshape(1, dv), mods, w_out)


def _gla_layer(lay, x, mods, p, s0):
    hk, hv = GLA_H * GLA_DK, GLA_H * GLA_DV
    w_all = jnp.concatenate([p['w_in'], p['gate_w1'][0], p['gate_w1'][1],
                             jnp.zeros((D, 128 - 2 * GLA_RANK), F32)], axis=1).astype(BF16)
    proj = _proj(lay, x, mods, 0, 1, w_all, jnp.zeros((1, w_all.shape[1]), F32), 640)
    pad = lambda w, lo: jnp.pad(w, ((lo, 128 - GLA_RANK - lo), (0, 0))).astype(BF16)
    w2f, w2b = pad(p['gate_w2'][0], 0), pad(p['gate_w2'][1], GLA_RANK)
    o = jnp.zeros((lay.T, hv), F32)
    o, s_fin = _gla_core(proj, w2f, w2b, p['gate_b'], None, o, 0, lay.B, lay.L)
    o, _ = _gla_core(proj, w2f, w2b, p['gate_b'], s0, o, lay.TP, lay.NS, lay.LS)
    x = _headnorm_out(lay, x, mods, o, proj, 2 * hk + hv, p['norm_g'], p['w_out'].astype(BF16),
                      GLA_H, GLA_DV, False)
    return x, s_fin


def _ret_layer(lay, x, mods, p, s0):
    hk, hv = RET_H * RET_DK, RET_H * RET_DV
    proj = _proj(lay, x, mods, 0, 1, p['w_in'].astype(BF16), jnp.zeros((1, 2 * hk + 2 * hv), F32), 768)
    tabs = _ret_tables(p['log_decay'])
    o = jnp.zeros((lay.T, hv), F32)
    o, s_fin = _ret_core(proj, tabs, None, o, 0, lay.B, lay.L, False)
    o, _ = _ret_core(proj, tabs, s0, o, lay.TP, lay.NS, lay.LS, True)
    x = _headnorm_out(lay, x, mods, o, proj, 2 * hk + hv, p['norm_g'], p['w_out'].astype(BF16),
                      RET_H, RET_DV, True)
    return x, s_fin


MOE_BM = 512
EXPERT_TF = 1792
ROUTER_LANES = 128
DMA_UNROLL = 8


def _router_kernel(x_ref, sh_ref, sc_ref, rw_ref, h_ref, idx_ref, gate_ref):
    h = _modulate(x_ref[...], sh_ref[...], sc_ref[...])
    h_ref[...] = h
    logits = jnp.dot(h, rw_ref[...], precision=HIGHEST, preferred_element_type=F32)
    lane = lax.broadcasted_iota(jnp.int32, logits.shape, 1)
    neg = jnp.float32(-jnp.inf)
    logits = jnp.where(lane < N_EXPERTS, logits, neg)
    m1 = jnp.max(logits, axis=-1, keepdims=True)
    i1 = jnp.min(jnp.where(logits == m1, lane, ROUTER_LANES), axis=-1, keepdims=True)
    rest = jnp.where(lane == i1, neg, logits)
    m2 = jnp.max(rest, axis=-1, keepdims=True)
    i2 = jnp.min(jnp.where(rest == m2, lane, ROUTER_LANES), axis=-1, keepdims=True)
    e2 = jnp.exp(m2 - m1)
    g1 = 1.0 / (1.0 + e2)
    idx_ref[:, 0:1] = i1
    idx_ref[:, 1:2] = i2
    gate_ref[:, 0:1] = g1
    gate_ref[:, 1:2] = e2 * g1


def _router(lay, x, mods, router_w):
    tm = lay.tile(512)
    rw = jnp.pad(router_w, ((0, 0), (0, ROUTER_LANES - N_EXPERTS)))
    return pl.pallas_call(
        _router_kernel,
        out_shape=(jax.ShapeDtypeStruct((lay.T, D), F32),
                   jax.ShapeDtypeStruct((lay.T, 2), jnp.int32),
                   jax.ShapeDtypeStruct((lay.T, 2), F32)),
        grid=(lay.T // tm,),
        in_specs=[pl.BlockSpec((tm, D), lambda i: (i, 0)),
                  _mod_spec(lay, tm, 3, 1), _mod_spec(lay, tm, 4, 1),
                  pl.BlockSpec((D, ROUTER_LANES), lambda i: (0, 0))],
        out_specs=(pl.BlockSpec((tm, D), lambda i: (i, 0)),
                   pl.BlockSpec((tm, 2), lambda i: (i, 0)),
                   pl.BlockSpec((tm, 2), lambda i: (i, 0))),
        compiler_params=_cp("arbitrary"),
        name="router",
    )(x, mods, mods, rw)


def _moe_plan(idx, bm):
    a = idx.size
    e = idx.reshape(a)
    onehot = (e[:, None] == jnp.arange(N_EXPERTS, dtype=jnp.int32)[None, :]).astype(jnp.int32)
    csum = jnp.cumsum(onehot, axis=0)
    counts = csum[-1]
    rank = jnp.sum((csum - onehot) * onehot, axis=-1)
    padded = (counts + bm - 1) // bm * bm
    pad_end = jnp.cumsum(padded)
    dest = ((pad_end - padded)[e] + rank).astype(jnp.int32)
    nb = -(-(a + N_EXPERTS * (bm - 1)) // bm)
    block_start = jnp.arange(nb, dtype=jnp.int32) * bm
    block_e = jnp.minimum(jnp.searchsorted(pad_end, block_start, side='right'), N_EXPERTS - 1).astype(jnp.int32)
    nvalid = (pad_end[-1] // bm).astype(jnp.int32).reshape(1)
    return dest, block_e, nvalid, nb


def _dispatch_kernel(tm, dest_ref, h_ref, xs_in, xs_hbm, sem):
    del xs_in
    i = pl.program_id(0)

    def row_copy(r, dst):
        return pltpu.make_async_copy(h_ref.at[pl.ds(r, 1)], xs_hbm.at[pl.ds(dst, 1)], sem)

    def issue(r, carry):
        a = 2 * (i * tm + r)
        row_copy(r, dest_ref[a]).start()
        row_copy(r, dest_ref[a + 1]).start()
        return carry

    def drain(r, carry):
        row_copy(r, 0).wait()
        row_copy(r, 0).wait()
        return carry

    lax.fori_loop(0, tm, issue, 0, unroll=DMA_UNROLL)
    lax.fori_loop(0, tm, drain, 0, unroll=DMA_UNROLL)


def _dispatch(lay, h, dest, nb, bm):
    tm = lay.tile(512)
    xs = jnp.zeros((nb * bm, D), F32)
    grid_spec = pltpu.PrefetchScalarGridSpec(
        num_scalar_prefetch=1,
        grid=(lay.T // tm,),
        in_specs=[pl.BlockSpec((tm, D), lambda i, d: (i, 0)), pl.BlockSpec(memory_space=pl.ANY)],
        out_specs=pl.BlockSpec(memory_space=pl.ANY),
        scratch_shapes=[pltpu.SemaphoreType.DMA(())],
    )
    return pl.pallas_call(
        functools.partial(_dispatch_kernel, tm),
        out_shape=jax.ShapeDtypeStruct(xs.shape, F32),
        grid_spec=grid_spec,
        input_output_aliases={2: 0},
        compiler_params=_cp("arbitrary"),
        name="dispatch",
    )(dest, h, xs)


def _experts_kernel(nf, be_ref, nv_ref, xs_ref, wa_ref, wb_ref, wo_ref, o_ref, xb_sc, acc_sc):
    i = pl.program_id(0)
    f = pl.program_id(1)
    valid = i < nv_ref[0]

    @pl.when(jnp.logical_and(valid, f == 0))
    def _():
        xb_sc[...] = xs_ref[...].astype(BF16)

    @pl.when(valid)
    def _():
        xb = xb_sc[...]
        a = jnp.dot(xb, wa_ref[...], preferred_element_type=F32)
        b = jnp.dot(xb, wb_ref[...], preferred_element_type=F32)
        h = (_silu(a) * b).astype(BF16)
        y = jnp.dot(h, wo_ref[...], preferred_element_type=F32)

        @pl.when(f == 0)
        def _():
            acc_sc[...] = y

        @pl.when(f > 0)
        def _():
            acc_sc[...] += y

    @pl.when(f == nf - 1)
    def _():
        o_ref[...] = jnp.where(valid, acc_sc[...], 0.0)


def _experts(xs, block_e, nvalid, nb, bm, w_in, w_out):
    tf = EXPERT_TF
    nf = EXPERT_DIM // tf

    def wmap(off):
        def imap(i, f, be, nv):
            fe = jnp.where(i < nv[0], f, nf - 1)
            return (be[i], 0, off + fe)
        return imap

    def womap(i, f, be, nv):
        fe = jnp.where(i < nv[0], f, nf - 1)
        return (be[i], fe, 0)

    grid_spec = pltpu.PrefetchScalarGridSpec(
        num_scalar_prefetch=2,
        grid=(nb, nf),
        in_specs=[pl.BlockSpec((bm, D), lambda i, f, be, nv: (jnp.minimum(i, nv[0] - 1), 0)),
                  pl.BlockSpec((None, D, tf), wmap(0)),
                  pl.BlockSpec((None, D, tf), wmap(nf)),
                  pl.BlockSpec((None, tf, D), womap)],
        out_specs=pl.BlockSpec((bm, D), lambda i, f, be, nv: (i, 0)),
        scratch_shapes=[pltpu.VMEM((bm, D), BF16), pltpu.VMEM((bm, D), F32)],
    )
    return pl.pallas_call(
        functools.partial(_experts_kernel, nf),
        out_shape=jax.ShapeDtypeStruct((nb * bm, D), F32),
        grid_spec=grid_spec,
        compiler_params=_cp("arbitrary", "arbitrary"),
        name="experts",
    )(block_e, nvalid, xs, w_in, w_in, w_out)


def _combine_kernel(tm, nt, final, dest_ref, x_ref, gate_ref, g_ref, fg_ref, ys_hbm, o_ref, y_sc, sem):
    i = pl.program_id(0)
    slot = i % 2

    def row_copy(s, k, r, src):
        return pltpu.make_async_copy(ys_hbm.at[pl.ds(src, 1)], y_sc.at[s, k, pl.ds(r, 1)], sem.at[s])

    def issue_tile(t, s):
        def issue(r, carry):
            a = 2 * (t * tm + r)
            row_copy(s, 0, r, dest_ref[a]).start()
            row_copy(s, 1, r, dest_ref[a + 1]).start()
            return carry
        lax.fori_loop(0, tm, issue, 0, unroll=DMA_UNROLL)

    @pl.when(i == 0)
    def _():
        issue_tile(0, 0)

    @pl.when(i + 1 < nt)
    def _():
        issue_tile(i + 1, 1 - slot)

    def drain(r, carry):
        row_copy(slot, 0, r, 0).wait()
        row_copy(slot, 1, r, 0).wait()
        return carry

    lax.fori_loop(0, tm, drain, 0, unroll=DMA_UNROLL)
    gate = gate_ref[...]
    out = x_ref[...] + g_ref[...] * (gate[:, 0:1] * y_sc[slot, 0] + gate[:, 1:2] * y_sc[slot, 1])
    if final:
        ms = jnp.mean(out * out, axis=-1, keepdims=True)
        out = out * lax.rsqrt(ms + RMS_EPS) * fg_ref[...]
    o_ref[...] = out


def _combine(lay, x, mods, gates, ys, dest, final_g):
    tm = lay.tile(256)
    nt = lay.T // tm
    final = final_g is not None
    fg = (final_g if final else jnp.ones((D,), F32)).reshape(1, D)
    grid_spec = pltpu.PrefetchScalarGridSpec(
        num_scalar_prefetch=1,
        grid=(nt,),
        in_specs=[pl.BlockSpec((tm, D), lambda i, d: (i, 0)),
                  pl.BlockSpec((tm, 2), lambda i, d: (i, 0)),
                  pl.BlockSpec((None, 1, D), lambda i, d: (lay.group(i * tm) * MOD_CHUNKS + 5, 0, 0)),
                  pl.BlockSpec((1, D), lambda i, d: (0, 0)),
                  pl.BlockSpec(memory_space=pl.ANY)],
        out_specs=pl.BlockSpec((tm, D), lambda i, d: (i, 0)),
        scratch_shapes=[pltpu.VMEM((2, 2, tm, D), F32), pltpu.SemaphoreType.DMA((2,))],
    )
    return pl.pallas_call(
        functools.partial(_combine_kernel, tm, nt, final),
        out_shape=jax.ShapeDtypeStruct((lay.T, D), F32),
        grid_spec=grid_spec,
        compiler_params=_cp("arbitrary"),
        name="combine",
    )(dest, x, gates, mods, fg, ys)


def _moe_layer(lay, x, mods, router_w, w_in, w_out, final_g=None, bm=MOE_BM):
    h, idx, gates = _router(lay, x, mods, router_w)
    dest, block_e, nvalid, nb = _moe_plan(idx, bm)
    xs = _dispatch(lay, h, dest, nb, bm)
    ys = _experts(xs, block_e, nvalid, nb, bm, w_in, w_out)
    return _combine(lay, x, mods, gates, ys, dest, final_g)


def kernel(x_prompt, x_sample, c, state_l0_s5_re, state_l0_s5_im, state_l2_gla, state_l3_ret, c_ctx, l0_mod_w, l0_mod_b, l0_s5_a_re, l0_s5_a_im, l0_s5_log_dt, l0_s5_b_re, l0_s5_b_im, l0_s5_c_re, l0_s5_c_im, l0_s5_d, l0_s5_glu_w, l0_ffn_w_in, l0_ffn_w_out, l1_mod_w, l1_mod_b, l1_hy_w_in, l1_hy_b_in, l1_hy_short_w, l1_hy_short_b, l1_hy_f_w1, l1_hy_f_b1, l1_hy_f_w2, l1_hy_f_b2, l1_hy_f_w3, l1_hy_f_freq, l1_hy_skip, l1_hy_w_out, l1_hy_b_out, l1_moe_router, l1_moe_w_in, l1_moe_w_out, l2_mod_w, l2_mod_b, l2_gla_w_in, l2_gla_gate_w1, l2_gla_gate_w2, l2_gla_gate_b, l2_gla_norm_g, l2_gla_w_out, l2_ffn_w_in, l2_ffn_w_out, l3_mod_w, l3_mod_b, l3_ret_w_in, l3_ret_log_decay, l3_ret_norm_g, l3_ret_w_out, l3_moe_router, l3_moe_w_in, l3_moe_w_out, final_norm_g):
    B, L, _ = x_prompt.shape
    NS, LS, _ = x_sample.shape
    lay = Layout(B, L, NS, LS)
    x = jnp.concatenate([x_prompt.reshape(B * L, D), x_sample.reshape(NS * LS, D)], axis=0)
    cond = jnp.concatenate([c_ctx[None], c, jnp.zeros((8 - 1 - NS, D), F32)], axis=0)
    mods0 = _mods(cond, l0_mod_w, l0_mod_b)
    p0 = dict(a_re=l0_s5_a_re, a_im=l0_s5_a_im, log_dt=l0_s5_log_dt, b_re=l0_s5_b_re, b_im=l0_s5_b_im,
              c_re=l0_s5_c_re, c_im=l0_s5_c_im, d=l0_s5_d, glu_w=l0_s5_glu_w.astype(BF16))
    x, s5_re, s5_im = _s5_layer(lay, x, mods0, p0, state_l0_s5_re, state_l0_s5_im)
    x = _ffn(lay, x, mods0, l0_ffn_w_in.astype(BF16), l0_ffn_w_out.astype(BF16))

    mods1 = _mods(cond, l1_mod_w, l1_mod_b)
    p1 = dict(w_in=l1_hy_w_in, b_in=l1_hy_b_in, short_w=l1_hy_short_w, short_b=l1_hy_short_b,
              f_w1=l1_hy_f_w1, f_b1=l1_hy_f_b1, f_w2=l1_hy_f_w2, f_b2=l1_hy_f_b2, f_w3=l1_hy_f_w3,
              f_freq=l1_hy_f_freq, skip=l1_hy_skip, w_out=l1_hy_w_out, b_out=l1_hy_b_out)
    x = _hyena_layer(lay, x, mods1, p1)
    x = _moe_layer(lay, x, mods1, l1_moe_router, l1_moe_w_in.astype(BF16), l1_moe_w_out.astype(BF16))

    mods2 = _mods(cond, l2_mod_w, l2_mod_b)
    p2 = dict(w_in=l2_gla_w_in, gate_w1=l2_gla_gate_w1, gate_w2=l2_gla_gate_w2, gate_b=l2_gla_gate_b,
              norm_g=l2_gla_norm_g, w_out=l2_gla_w_out)
    x, gla_state = _gla_layer(lay, x, mods2, p2, state_l2_gla)
    x = _ffn(lay, x, mods2, l2_ffn_w_in.astype(BF16), l2_ffn_w_out.astype(BF16))

    mods3 = _mods(cond, l3_mod_w, l3_mod_b)
    p3 = dict(w_in=l3_ret_w_in, log_decay=l3_ret_log_decay, norm_g=l3_ret_norm_g, w_out=l3_ret_w_out)
    x, ret_state = _ret_layer(lay, x, mods3, p3, state_l3_ret)
    y = _moe_layer(lay, x, mods3, l3_moe_router, l3_moe_w_in.astype(BF16), l3_moe_w_out.astype(BF16),
                   final_g=final_norm_g)
    return (y[:lay.TP].reshape(B, L, D), y[lay.TP:].reshape(NS, LS, D), s5_re, s5_im, gla_state, ret_state)
```

```python
import functools
import math

import jax
import jax.numpy as jnp
import numpy as np
from jax import lax
from jax.experimental import pallas as pl
from jax.experimental.pallas import tpu as pltpu

F32 = jnp.float32
BF16 = jnp.bfloat16
HIGHEST = lax.Precision.HIGHEST

D = 1024
RMS_EPS = 1e-6
MOD_CHUNKS = 6
GRID_W = 64

S5_Q = 16
S5_G = D // S5_Q
S5_P = 64
S5_T = 16
S5_SCAN_ROWS = 64

HY_BANDS = 16
HY_TARGET = 1e-2
HY_FAST_PCT = 0.3
HY_SLOW_PCT = 1.5

GLA_H, GLA_DK, GLA_DV = 4, 128, 256
GLA_RANK = 16
GLA_TAU = 16.0
RET_H, RET_DK, RET_DV = 4, 256, 512
CHUNK = 64
ROPE_BASE = 10000.0

FFN_DIM = 2816
N_EXPERTS = 8
EXPERT_DIM = 3584

VMEM_LIMIT_V7X = 56 * 1024 * 1024


def _cp(*sem):
    return pltpu.CompilerParams(dimension_semantics=sem, vmem_limit_bytes=VMEM_LIMIT_V7X)


def _silu(x):
    return x * jax.nn.sigmoid(x)


def _modulate(x, shift, scale):
    ms = jnp.mean(x * x, axis=-1, keepdims=True)
    return x * lax.rsqrt(ms + RMS_EPS) * (1.0 + scale) + shift


class Layout:
    def __init__(self, n_prompt, l_prompt, n_sample, l_sample):
        self.B, self.L, self.NS, self.LS = n_prompt, l_prompt, n_sample, l_sample
        self.TP = n_prompt * l_prompt
        self.T = self.TP + n_sample * l_sample

    def tile(self, want):
        t = math.gcd(math.gcd(self.TP, self.LS), want)
        assert t % 8 == 0
        return t

    def group(self, row):
        return jnp.where(row < self.TP, 0, 1 + (row - self.TP) // self.LS)


def _mod_spec(lay, tm, chunk, ngrid):
    def imap(*ids):
        return (lay.group(ids[0] * tm) * MOD_CHUNKS + chunk, 0, 0)
    del ngrid
    return pl.BlockSpec((None, 1, D), imap)


def _mods_kernel(c_ref, w_ref, b_ref, o_ref):
    o_ref[...] = jnp.dot(_silu(c_ref[...]), w_ref[...], precision=HIGHEST,
                         preferred_element_type=F32) + b_ref[...]


def _mods(cond, w, b):
    n = MOD_CHUNKS * D
    tn = 1536
    out = pl.pallas_call(
        _mods_kernel,
        out_shape=jax.ShapeDtypeStruct((8, n), F32),
        grid=(n // tn,),
        in_specs=[pl.BlockSpec((8, D), lambda j: (0, 0)),
                  pl.BlockSpec((D, tn), lambda j: (0, j)),
                  pl.BlockSpec((1, tn), lambda j: (0, j))],
        out_specs=pl.BlockSpec((8, tn), lambda j: (0, j)),
        compiler_params=_cp("arbitrary"),
        name="mods",
    )(cond, w, b.reshape(1, n))
    return out.reshape(8 * MOD_CHUNKS, 1, D)


def _modulate_kernel(x_ref, sh_ref, sc_ref, o_ref):
    o_ref[...] = _modulate(x_ref[...], sh_ref[...], sc_ref[...]).astype(o_ref.dtype)


def _modulate_call(lay, x, mods, c_shift, c_scale, dtype):
    tm = lay.tile(512)
    return pl.pallas_call(
        _modulate_kernel,
        out_shape=jax.ShapeDtypeStruct((lay.T, D), dtype),
        grid=(lay.T // tm,),
        in_specs=[pl.BlockSpec((tm, D), lambda i: (i, 0)),
                  _mod_spec(lay, tm, c_shift, 1), _mod_spec(lay, tm, c_scale, 1)],
        out_specs=pl.BlockSpec((tm, D), lambda i: (i, 0)),
        compiler_params=_cp("arbitrary"),
        name="modulate",
    )(x, mods, mods)


def _proj_kernel(x_ref, sh_ref, sc_ref, w_ref, b_ref, o_ref, u_sc):
    @pl.when(pl.program_id(1) == 0)
    def _():
        u_sc[...] = _modulate(x_ref[...], sh_ref[...], sc_ref[...]).astype(BF16)

    acc = jnp.dot(u_sc[...], w_ref[...], preferred_element_type=F32) + b_ref[...]
    o_ref[...] = acc.astype(o_ref.dtype)


def _proj(lay, x, mods, c_shift, c_scale, w, b, tn, out_dtype=BF16):
    tm = lay.tile(1024)
    n = w.shape[1]
    assert n % tn == 0
    return pl.pallas_call(
        _proj_kernel,
        out_shape=jax.ShapeDtypeStruct((lay.T, n), out_dtype),
        grid=(lay.T // tm, n // tn),
        in_specs=[pl.BlockSpec((tm, D), lambda i, j: (i, 0)),
                  _mod_spec(lay, tm, c_shift, 2), _mod_spec(lay, tm, c_scale, 2),
                  pl.BlockSpec((D, tn), lambda i, j: (0, j)),
                  pl.BlockSpec((1, tn), lambda i, j: (0, j))],
        out_specs=pl.BlockSpec((tm, tn), lambda i, j: (i, j)),
        scratch_shapes=[pltpu.VMEM((tm, D), BF16)],
        compiler_params=_cp("arbitrary", "arbitrary"),
        name="proj",
    )(x, mods, mods, w, b)


def _ffn_kernel(nf, x_ref, sh_ref, sc_ref, g_ref, wa_ref, wb_ref, wo_ref, o_ref, u_sc, acc_sc):
    f = pl.program_id(1)

    @pl.when(f == 0)
    def _():
        u_sc[...] = _modulate(x_ref[...], sh_ref[...], sc_ref[...]).astype(BF16)
        acc_sc[...] = jnp.zeros_like(acc_sc)

    u = u_sc[...]
    a = jnp.dot(u, wa_ref[...], preferred_element_type=F32)
    b = jnp.dot(u, wb_ref[...], preferred_element_type=F32)
    h = (_silu(a) * b).astype(BF16)
    acc_sc[...] += jnp.dot(h, wo_ref[...], preferred_element_type=F32)

    @pl.when(f == nf - 1)
    def _():
        o_ref[...] = x_ref[...] + g_ref[...] * acc_sc[...]


def _ffn(lay, x, mods, w_in, w_out):
    tm = lay.tile(512)
    tf = 1408
    nf = FFN_DIM // tf
    return pl.pallas_call(
        functools.partial(_ffn_kernel, nf),
        out_shape=jax.ShapeDtypeStruct((lay.T, D), F32),
        grid=(lay.T // tm, nf),
        in_specs=[pl.BlockSpec((tm, D), lambda i, f: (i, 0)),
                  _mod_spec(lay, tm, 3, 2), _mod_spec(lay, tm, 4, 2), _mod_spec(lay, tm, 5, 2),
                  pl.BlockSpec((D, tf), lambda i, f: (0, f)),
                  pl.BlockSpec((D, tf), lambda i, f: (0, nf + f)),
                  pl.BlockSpec((tf, D), lambda i, f: (f, 0))],
        out_specs=pl.BlockSpec((tm, D), lambda i, f: (i, 0)),
        scratch_shapes=[pltpu.VMEM((tm, D), BF16), pltpu.VMEM((tm, D), F32)],
        compiler_params=_cp("arbitrary", "arbitrary"),
        name="ffn",
    )(x, mods, mods, mods, w_in, w_in, w_out)


def _s5_tables(a_re, a_im, log_dt, b_re, b_im, c_re, c_im, d_skip):
    T, G, P, Q = S5_T, S5_G, S5_P, S5_Q
    a = lax.complex(a_re, a_im)
    adt = a * jnp.exp(log_dt)[..., None]
    lam = jnp.exp(adt)
    bb = ((lam - 1.0) / a)[..., None] * lax.complex(b_re, b_im)
    cm = lax.complex(c_re, c_im)
    steps = jnp.arange(T + 1, dtype=F32)
    pw = jnp.exp(steps[None, :, None, None] * adt[:, None])
    kern = jnp.real(jnp.einsum('dgqp,djgp,dgpr->djgqr', cm, pw[:, :T], bb))
    s_i = jnp.arange(T)[:, None]
    t_i = jnp.arange(T)[None, :]
    kf = kern[0][jnp.clip(t_i - s_i, 0, T - 1)] * (t_i >= s_i)[..., None, None, None]
    kb = kern[1][jnp.clip(s_i - t_i, 0, T - 1)] * (s_i >= t_i)[..., None, None, None]
    m = jnp.transpose(kf + kb, (2, 0, 4, 1, 3))
    eye = (jnp.eye(T)[:, None, :, None] * jnp.eye(Q)[None, :, None, :])
    m = m + eye[None] * d_skip.reshape(G, 1, 1, 1, Q)
    m = m.reshape(G, T * Q, T * Q)
    e_f = pw[0][T - 1 - jnp.arange(T)]
    e_b = pw[1][jnp.arange(T)]
    n_f = e_f[..., None] * bb[0][None]
    n_b = e_b[..., None] * bb[1][None]
    n_c = jnp.concatenate([n_f, n_b], axis=2)
    n_c = jnp.transpose(n_c, (1, 0, 3, 2)).reshape(G, T * Q, 2 * P)
    lam_t = jnp.concatenate([pw[0][T], pw[1][T]], axis=-1)
    w_f = cm[0][:, None] * jnp.transpose(pw[0][1:T + 1], (1, 0, 2))[:, :, None, :]
    w_b = cm[1][:, None] * jnp.transpose(pw[1][T - jnp.arange(T)], (1, 0, 2))[:, :, None, :]
    w_f = jnp.transpose(w_f, (0, 3, 1, 2)).reshape(G, P, T * Q)
    w_b = jnp.transpose(w_b, (0, 3, 1, 2)).reshape(G, P, T * Q)
    z = jnp.zeros_like(jnp.real(w_f))
    c_mats = dict(c_f_re=jnp.concatenate([jnp.real(w_f), z], axis=1),
                  c_f_im=jnp.concatenate([-jnp.imag(w_f), z], axis=1),
                  c_b_re=jnp.concatenate([z, jnp.real(w_b)], axis=1),
                  c_b_im=jnp.concatenate([z, -jnp.imag(w_b)], axis=1))
    return dict(m=m.astype(BF16), n_re=jnp.real(n_c).astype(BF16), n_im=jnp.imag(n_c).astype(BF16),
                l_re=jnp.real(lam_t), l_im=jnp.imag(lam_t), **{k: v.astype(BF16) for k, v in c_mats.items()})


def _s5_in_kernel(u_ref, m_ref, nre_ref, nim_ref, yi_ref, sre_ref, sim_ref):
    u = u_ref[...]
    yi_ref[...] = jnp.dot(u, m_ref[...], preferred_element_type=F32)
    sre_ref[...] = jnp.dot(u, nre_ref[...], preferred_element_type=F32)
    sim_ref[...] = jnp.dot(u, nim_ref[...], preferred_element_type=F32)


def _s5_in(ug, tabs):
    G, R, W = ug.shape
    P2 = 2 * S5_P
    gspec = lambda n: pl.BlockSpec((None, W, n), lambda g: (g, 0, 0))
    rspec = lambda n: pl.BlockSpec((None, R, n), lambda g: (g, 0, 0))
    return pl.pallas_call(
        _s5_in_kernel,
        out_shape=(jax.ShapeDtypeStruct((G, R, W), F32),
                   jax.ShapeDtypeStruct((G, R, P2), F32),
                   jax.ShapeDtypeStruct((G, R, P2), F32)),
        grid=(G,),
        in_specs=[rspec(W), gspec(W), gspec(P2), gspec(P2)],
        out_specs=(rspec(W), rspec(P2), rspec(P2)),
        compiler_params=_cp("arbitrary"),
        name="s5_in",
    )(ug, tabs['m'], tabs['n_re'], tabs['n_im'])


def _s5_scan_kernel(nsb, ncb, nblk, sref_ref, simf_ref, sreb_ref, simb_ref, lre_ref, lim_ref,
                    h0re_ref, h0im_ref, *rest):
    hfre_ref, hfim_ref, hbre_ref, hbim_ref, fre_ref, fim_ref, cre_sc, cim_sc = rest[4:]
    P = S5_P
    j = pl.program_id(1)
    fwd = lax.broadcasted_iota(jnp.int32, (1, 1, 2 * P), 2) < P
    lre = lre_ref[...][:, None, :]
    lim = lim_ref[...][:, None, :]

    @pl.when(j == 0)
    def _():
        cre_sc[...] = h0re_ref[...]
        cim_sc[...] = h0im_ref[...]

    def at(k):
        return pl.ds(k, 1) if nsb == 1 else pl.ds(k, nsb, stride=ncb)

    def body(k, carry):
        hre, him = carry
        kb = ncb - 1 - k
        hfre_ref[:, at(k), :] = hre
        hfim_ref[:, at(k), :] = him
        hbre_ref[:, at(kb), :] = hre
        hbim_ref[:, at(kb), :] = him
        sre = jnp.where(fwd, sref_ref[:, at(k), :], sreb_ref[:, at(kb), :])
        sim = jnp.where(fwd, simf_ref[:, at(k), :], simb_ref[:, at(kb), :])
        return (lre * hre - lim * him + sre, lre * him + lim * hre + sim)

    hre, him = lax.fori_loop(0, ncb, body, (cre_sc[...], cim_sc[...]))
    cre_sc[...] = hre
    cim_sc[...] = him

    @pl.when(j == nblk - 1)
    def _():
        fre_ref[...] = hre
        fim_ref[...] = him


def _s5_scan(sre, sim, tabs, h0re, h0im, hprev, row0, nseq, nc, nsb, nblk):
    G, R, P2 = sre.shape
    assert nblk == 1 or nsb == 1
    ncb = nc // nblk
    rb = nsb * ncb
    assert row0 % rb == 0 and nseq % nsb == 0 and nc % nblk == 0
    b0 = row0 // rb
    fspec = pl.BlockSpec((G, rb, P2), lambda i, j: (0, b0 + i * nblk + j, 0))
    bspec = pl.BlockSpec((G, rb, P2), lambda i, j: (0, b0 + i * nblk + nblk - 1 - j, 0))
    lspec = pl.BlockSpec((G, P2), lambda i, j: (0, 0))
    qspec = pl.BlockSpec((None, G, nsb, P2), lambda i, j: (i, 0, 0, 0))
    anyspec = pl.BlockSpec(memory_space=pl.ANY)
    fin = jax.ShapeDtypeStruct((nseq // nsb, G, nsb, P2), F32)
    outs = pl.pallas_call(
        functools.partial(_s5_scan_kernel, nsb, ncb, nblk),
        out_shape=tuple(jax.ShapeDtypeStruct(h.shape, h.dtype) for h in hprev) + (fin, fin),
        grid=(nseq // nsb, nblk),
        in_specs=[fspec, fspec, bspec, bspec, lspec, lspec, qspec, qspec] + [anyspec] * 4,
        out_specs=(fspec, fspec, bspec, bspec, qspec, qspec),
        scratch_shapes=[pltpu.VMEM((G, nsb, P2), F32), pltpu.VMEM((G, nsb, P2), F32)],
        input_output_aliases={8: 0, 9: 1, 10: 2, 11: 3},
        compiler_params=_cp("arbitrary", "arbitrary"),
        name="s5_scan",
    )(sre, sim, sre, sim, tabs['l_re'], tabs['l_im'], h0re, h0im, *hprev)
    return outs[:4], outs[4], outs[5]


def _s5_out_kernel(yi_ref, hfre_ref, hfim_ref, hbre_ref, hbim_ref, cfre_ref, cfim_ref, cbre_ref, cbim_ref,
                   y_ref):
    y = yi_ref[...]
    for h_ref, c_ref in ((hfre_ref, cfre_ref), (hfim_ref, cfim_ref), (hbre_ref, cbre_ref), (hbim_ref, cbim_ref)):
        y += jnp.dot(h_ref[...].astype(BF16), c_ref[...], preferred_element_type=F32)
    y_ref[...] = y.astype(y_ref.dtype)


def _s5_out(yi, hprev, tabs):
    G, R, W = yi.shape
    P2 = 2 * S5_P
    gspec = pl.BlockSpec((None, P2, W), lambda g: (g, 0, 0))
    hspec = pl.BlockSpec((None, R, P2), lambda g: (g, 0, 0))
    rspec = pl.BlockSpec((None, R, W), lambda g: (g, 0, 0))
    return pl.pallas_call(
        _s5_out_kernel,
        out_shape=jax.ShapeDtypeStruct((G, R, W), BF16),
        grid=(G,),
        in_specs=[rspec] + [hspec] * 4 + [gspec] * 4,
        out_specs=rspec,
        compiler_params=_cp("arbitrary"),
        name="s5_out",
    )(yi, *hprev, tabs['c_f_re'], tabs['c_f_im'], tabs['c_b_re'], tabs['c_b_im'])


def _s5_glu_kernel(x_ref, y_ref, g_ref, wv_ref, wg_ref, o_ref, a_sc):
    @pl.when(pl.program_id(1) == 0)
    def _():
        a_sc[...] = jax.nn.gelu(y_ref[...].astype(F32)).astype(BF16)

    a = a_sc[...]
    val = jnp.dot(a, wv_ref[...], preferred_element_type=F32)
    gate = jnp.dot(a, wg_ref[...], preferred_element_type=F32)
    o_ref[...] = x_ref[...] + g_ref[...] * (val * jax.nn.sigmoid(gate))


def _s5_glu(lay, x, y, mods, glu_w):
    tm = lay.tile(1024)
    tn = 512
    nn = D // tn
    return pl.pallas_call(
        _s5_glu_kernel,
        out_shape=jax.ShapeDtypeStruct((lay.T, D), F32),
        grid=(lay.T // tm, nn),
        in_specs=[pl.BlockSpec((tm, tn), lambda i, j: (i, j)),
                  pl.BlockSpec((tm, D), lambda i, j: (i, 0)),
                  pl.BlockSpec((None, 1, tn), lambda i, j: (lay.group(i * tm) * MOD_CHUNKS + 2, 0, j)),
                  pl.BlockSpec((D, tn), lambda i, j: (0, j)),
                  pl.BlockSpec((D, tn), lambda i, j: (0, nn + j))],
        out_specs=pl.BlockSpec((tm, tn), lambda i, j: (i, j)),
        scratch_shapes=[pltpu.VMEM((tm, D), BF16)],
        compiler_params=_cp("arbitrary", "arbitrary"),
        name="s5_glu",
    )(x, y, mods, glu_w, glu_w)


def _s5_layer(lay, x, mods, p, h0_re, h0_im):
    T, G, P, Q = S5_T, S5_G, S5_P, S5_Q
    tabs = _s5_tables(p['a_re'], p['a_im'], p['log_dt'], p['b_re'], p['b_im'], p['c_re'], p['c_im'], p['d'])
    u = _modulate_call(lay, x, mods, 0, 1, BF16)
    R = lay.T // T
    ug = jnp.transpose(u.reshape(R, T, G, Q), (2, 0, 1, 3)).reshape(G, R, T * Q)
    yi, sre, sim = _s5_in(ug, tabs)
    hprev = tuple(jnp.zeros((G, R, 2 * P), F32) for _ in range(4))
    ncp, ncs = lay.L // T, lay.LS // T
    nsb = math.gcd(lay.B, max(1, S5_SCAN_ROWS // ncp))
    zero = jnp.zeros((lay.B // nsb, G, nsb, 2 * P), F32)
    hprev, fre, fim = _s5_scan(sre, sim, tabs, zero, zero, hprev, 0, lay.B, ncp, nsb, 1)
    to_lanes = lambda s: jnp.transpose(s, (0, 2, 1, 3)).reshape(lay.NS, G, 1, 2 * P)
    hprev, _, _ = _s5_scan(sre, sim, tabs, to_lanes(h0_re), to_lanes(h0_im), hprev,
                           lay.TP // T, lay.NS, ncs, 1, max(1, ncs // S5_SCAN_ROWS))
    yg = _s5_out(yi, hprev, tabs)
    y = jnp.transpose(yg.reshape(G, R, T, Q), (1, 2, 0, 3)).reshape(lay.T, D)
    x = _s5_glu(lay, x, y, mods, p['glu_w'])
    from_lanes = lambda s: jnp.transpose(s, (0, 2, 1, 3)).reshape(lay.B, G, 2, P).transpose(0, 2, 1, 3)
    return x, from_lanes(fre), from_lanes(fim)


def _hyena_filters(L, p):
    t = jnp.linspace(0.0, 1.0, L, dtype=F32)[:, None]
    w = 2.0 * math.pi * jnp.arange(L, dtype=F32)[:, None] / L
    f = jnp.linspace(1e-4, HY_BANDS - 1, HY_BANDS, dtype=F32)[None, :]
    feats = jnp.concatenate([t, jnp.cos(f * w), -jnp.sin(f * w)], axis=-1)
    mm = functools.partial(jnp.matmul, precision=HIGHEST)
    z = jnp.sin(p['f_freq'][0] * (mm(feats, p['f_w1']) + p['f_b1']))
    z = jnp.sin(p['f_freq'][1] * (mm(z, p['f_w2']) + p['f_b2']))
    h = mm(z, p['f_w3']).reshape(L, 2, 2, D)
    max_decay = math.log(HY_TARGET) / HY_FAST_PCT
    min_decay = math.log(HY_TARGET) / HY_SLOW_PCT
    deltas = jnp.abs(jnp.linspace(min_decay, max_decay, D, dtype=F32))
    h = h * jnp.exp(-t[:, :, None, None] * deltas)
    k_lo = h[:, :, 0]
    k_hi = jnp.concatenate([jnp.zeros((1, 2, D), F32), h[:0:-1, :, 1]], axis=0)
    norm = jnp.sum(jnp.abs(k_lo), axis=0, keepdims=True) + jnp.sum(jnp.abs(k_hi), axis=0, keepdims=True)
    k_lo, k_hi = k_lo / norm, k_hi / norm
    alt = (1.0 - 2.0 * (jnp.arange(L) % 2).astype(F32))[:, None, None]
    k_ny = jnp.sum(alt * (k_lo + k_hi), axis=0) / (2 * L)
    return jnp.transpose(k_lo, (1, 0, 2)), jnp.transpose(k_hi, (1, 0, 2)), k_ny


def _dft_tables(L):
    i = jnp.arange(L, dtype=jnp.int32)
    idx = (i[:, None] * i[None, :]) % (2 * L)
    ang = idx.astype(F32) * (math.pi / L)
    return jnp.cos(ang).astype(BF16), jnp.sin(ang).astype(BF16)


def _hy_spec_kernel(L, tr, c_ref, s_ref, klo_ref, khi_ref, p_ref, q_ref):
    r = pl.program_id(2)
    f = r * tr + lax.broadcasted_iota(jnp.int32, (tr, 1), 0)
    sgn = (1 - 2 * (f % 2)).astype(F32)
    scale = jnp.where(f == 0, 1.0, 2.0) * (1.0 / (2 * L))
    c, s = c_ref[...], s_ref[...]
    lo, hi = klo_ref[...], khi_ref[...]
    dot = functools.partial(jnp.dot, preferred_element_type=F32)
    p_ref[...] = scale * (dot(c, lo) + sgn * dot(c, hi))
    q_ref[...] = scale * (dot(s, lo) + sgn * dot(s, hi))


def _hy_spectrum(L, cos, sin, k_lo, k_hi):
    tr = min(L, 512)
    tc = 512
    kspec = pl.BlockSpec((None, L, tc), lambda o, j, r: (o, 0, j))
    tspec = pl.BlockSpec((tr, L), lambda o, j, r: (r, 0))
    ospec = pl.BlockSpec((None, tr, tc), lambda o, j, r: (o, r, j))
    return pl.pallas_call(
        functools.partial(_hy_spec_kernel, L, tr),
        out_shape=(jax.ShapeDtypeStruct((2, L, D), F32), jax.ShapeDtypeStruct((2, L, D), F32)),
        grid=(2, D // tc, L // tr),
        in_specs=[tspec, tspec, kspec, kspec],
        out_specs=(ospec, ospec),
        compiler_params=_cp("arbitrary", "arbitrary", "arbitrary"),
        name="hy_spectrum",
    )(cos, sin, k_lo.astype(BF16), k_hi.astype(BF16))


def _hy_core_kernel(L, tr, ngrp, tc, x1_ref, x2_ref, v_ref, sw1_ref, sw2_ref, swv_ref, sb1_ref, sb2_ref,
                    sbv_ref, c_ref, s_ref, p_ref, q_ref, kny_ref, skip_ref, *rest):
    o_ref, z0_sc, z1_sc, x1_sc, x2_sc, a_sc, b_sc, ny_sc = rest[-8:]
    ph = pl.program_id(2)
    r = pl.program_id(3)
    nrt = L // tr
    W = ngrp * tc
    row = lax.broadcasted_iota(jnp.int32, (L, 1), 0)
    alt_all = (1 - 2 * (row % 2)).astype(F32)

    def conv3(src_ref, g, w_ref, b_ref):
        x = src_ref[g].astype(F32)
        prev = jnp.where(row == 0, 0.0, pltpu.roll(x, 1, 0))
        nxt = jnp.where(row == L - 1, 0.0, pltpu.roll(x, L - 1, 0))
        return prev * w_ref[0:1, :] + x * w_ref[1:2, :] + nxt * w_ref[2:3, :] + b_ref[...]

    @pl.when(jnp.logical_and(ph == 0, r == 0))
    def _():
        for g in range(ngrp):
            cols = slice(g * tc, (g + 1) * tc)
            x1_sc[:, cols] = conv3(x1_ref, g, sw1_ref, sb1_ref).astype(BF16)
            x2_sc[:, cols] = conv3(x2_ref, g, sw2_ref, sb2_ref).astype(BF16)
            z0_sc[:, cols] = conv3(v_ref, g, swv_ref, sbv_ref).astype(BF16)
        ny_sc[...] = jnp.sum(alt_all * z0_sc[...].astype(F32), axis=0, keepdims=True)

    rows = pl.ds(pl.multiple_of(r * tr, tr), tr)
    tile = lambda ref: jnp.concatenate([ref[...]] * ngrp, axis=1)
    dot = functools.partial(jnp.dot, preferred_element_type=F32)

    def forward(z_sc):
        z = z_sc[...]
        zre = dot(c_ref[...], z)
        zim = dot(s_ref[...], z)
        pw, qw = tile(p_ref), tile(q_ref)
        a_sc[rows, :] = (zre * pw - zim * qw).astype(BF16)
        b_sc[rows, :] = (zim * pw + zre * qw).astype(BF16)

    def inverse(order, z_sc):
        y = dot(c_ref[...], a_sc[...]) + dot(s_ref[...], b_sc[...])
        t = r * tr + lax.broadcasted_iota(jnp.int32, (tr, 1), 0)
        alt = (1 - 2 * (t % 2)).astype(F32)
        kny = jnp.concatenate([kny_ref[order:order + 1, :]] * ngrp, axis=1)
        skip = jnp.concatenate([skip_ref[order:order + 1, :]] * ngrp, axis=1)
        return y + alt * (ny_sc[...] * kny) + skip * z_sc[rows, :].astype(F32)

    @pl.when(ph == 0)
    def _():
        forward(z0_sc)

    @pl.when(ph == 1)
    def _():
        z1_sc[rows, :] = (x1_sc[rows, :].astype(F32) * inverse(0, z0_sc)).astype(BF16)

        @pl.when(r == nrt - 1)
        def _():
            ny_sc[...] = jnp.sum(alt_all * z1_sc[...].astype(F32), axis=0, keepdims=True)

    @pl.when(ph == 2)
    def _():
        forward(z1_sc)

    @pl.when(ph == 3)
    def _():
        out = x2_sc[rows, :].astype(F32) * inverse(1, z1_sc)
        for g in range(ngrp):
            o_ref[g, rows, :] = out[:, g * tc:(g + 1) * tc].astype(o_ref.dtype)


def _hy_core(proj, short_w, short_b, skip, cos, sin, pq, k_ny, o_prev, row0, nseq, L, ngrp, tc):
    T = proj.shape[0]
    tr = min(L, 512)
    nrt = L // tr
    assert row0 % (L * ngrp) == 0 and nseq % ngrp == 0 and T % L == 0
    sb0 = row0 // (L * ngrp)
    nct = D // tc
    p3 = proj.reshape(T // L, L, 3 * D)
    p_arr, q_arr = pq

    def xspec(part):
        return pl.BlockSpec((ngrp, L, tc), lambda i, j, ph, r: (sb0 + i, 0, part * nct + j))

    def wspec(part, rows_):
        return pl.BlockSpec((rows_, tc), lambda i, j, ph, r: (0, part * nct + j))

    def pq_map(i, j, ph, r):
        return (ph // 2, jnp.where(ph % 2 == 0, r, nrt - 1), j)

    tspec = pl.BlockSpec((tr, L), lambda i, j, ph, r: (r, 0))
    in_specs = [xspec(0), xspec(1), xspec(2), wspec(0, 3), wspec(1, 3), wspec(2, 3),
                wspec(0, 1), wspec(1, 1), wspec(2, 1), tspec, tspec,
                pl.BlockSpec((None, tr, tc), pq_map), pl.BlockSpec((None, tr, tc), pq_map),
                pl.BlockSpec((2, tc), lambda i, j, ph, r: (0, j)),
                pl.BlockSpec((2, tc), lambda i, j, ph, r: (0, j))]
    sb = short_b.reshape(1, 3 * D)
    args = [p3, p3, p3, short_w, short_w, short_w, sb, sb, sb, cos, sin, p_arr, q_arr, k_ny, skip]
    aliases = {}
    if o_prev is not None:
        in_specs.append(pl.BlockSpec(memory_space=pl.ANY))
        args.append(o_prev.reshape(T // L, L, D))
        aliases = {len(args) - 1: 0}
    W = ngrp * tc
    out = pl.pallas_call(
        functools.partial(_hy_core_kernel, L, tr, ngrp, tc),
        out_shape=jax.ShapeDtypeStruct((T // L, L, D), BF16),
        grid=(nseq // ngrp, nct, 4, nrt),
        in_specs=in_specs,
        out_specs=pl.BlockSpec((ngrp, L, tc), lambda i, j, ph, r: (sb0 + i, 0, j)),
        scratch_shapes=[pltpu.VMEM((L, W), BF16)] * 6 + [pltpu.VMEM((1, W), F32)],
        input_output_aliases=aliases,
        compiler_params=_cp("arbitrary", "arbitrary", "arbitrary", "arbitrary"),
        name="hy_core",
    )(*args)
    return out.reshape(T, D)


def _plain_out_kernel(x_ref, z_ref, g_ref, w_ref, b_ref, o_ref):
    acc = jnp.dot(z_ref[...], w_ref[...], preferred_element_type=F32) + b_ref[...]
    o_ref[...] = x_ref[...] + g_ref[...] * acc


def _plain_out(lay, x, mods, z, w, b):
    tm = lay.tile(1024)
    tn = 512
    kdim = z.shape[1]
    return pl.pallas_call(
        _plain_out_kernel,
        out_shape=jax.ShapeDtypeStruct((lay.T, D), F32),
        grid=(lay.T // tm, D // tn),
        in_specs=[pl.BlockSpec((tm, tn), lambda i, j: (i, j)),
                  pl.BlockSpec((tm, kdim), lambda i, j: (i, 0)),
                  pl.BlockSpec((None, 1, tn), lambda i, j: (lay.group(i * tm) * MOD_CHUNKS + 2, 0, j)),
                  pl.BlockSpec((kdim, tn), lambda i, j: (0, j)),
                  pl.BlockSpec((1, tn), lambda i, j: (0, j))],
        out_specs=pl.BlockSpec((tm, tn), lambda i, j: (i, j)),
        compiler_params=_cp("arbitrary", "arbitrary"),
        name="plain_out",
    )(x, z, mods, w, b.reshape(1, D))


def _hyena_layer(lay, x, mods, p):
    proj = _proj(lay, x, mods, 0, 1, p['w_in'].astype(BF16), p['b_in'].reshape(1, 3 * D), 768)
    z = jnp.zeros((lay.T, D), BF16)
    for row0, nseq, L, ngrp, tc in ((0, lay.B, lay.L, math.gcd(lay.B, 4), 256),
                                    (lay.TP, lay.NS, lay.LS, lay.NS, 128)):
        k_lo, k_hi, k_ny = _hyena_filters(L, p)
        cos, sin = _dft_tables(L)
        pq = _hy_spectrum(L, cos, sin, k_lo, k_hi)
        z = _hy_core(proj, p['short_w'], p['short_b'], p['skip'], cos, sin, pq, k_ny, z, row0, nseq, L, ngrp, tc)
    return _plain_out(lay, x, mods, z, p['w_out'].astype(BF16), p['b_out'])


_NT = (((1,), (1,)), ((), ()))
_TN = (((0,), (0,)), ((), ()))


def _tri(dr):
    t = lax.broadcasted_iota(jnp.int32, (CHUNK, CHUNK), 0)
    s = lax.broadcasted_iota(jnp.int32, (CHUNK, CHUNK), 1)
    return (s <= t) if dr == 0 else (s >= t)


def _chunk_cumsum(g, dr):
    n = g.shape[0]
    pos = lax.broadcasted_iota(jnp.int32, g.shape, 0) % CHUNK
    sh = 1
    while sh < CHUNK:
        if dr == 0:
            g = g + jnp.where(pos >= sh, pltpu.roll(g, sh, 0), 0.0)
        else:
            g = g + jnp.where(pos < CHUNK - sh, pltpu.roll(g, n - sh, 0), 0.0)
        sh *= 2
    return g


def _gla_kernel(cps, nseg, U, has_s0, want_final, *refs):
    q_ref, k_ref, v_ref, lr_ref, w2f_ref, w2b_ref, gb_ref = refs[:7]
    s0_ref = refs[7] if has_s0 else None
    qin_sc, kin_sc, kout_sc, dec_sc, st_sc, s_sc, s0t_sc = refs[-7:]
    outs = refs[-9:-7] if want_final else refs[-8:-7]
    o_ref = outs[0]
    sf_ref = outs[1] if want_final else None
    C = CHUNK
    nsc = cps // U
    nchunks = nseg * cps
    rows_total = nchunks * C
    w2 = (w2f_ref, w2b_ref)

    for dr in range(2):
        pre = jnp.dot(lr_ref[...], w2[dr][...], preferred_element_type=F32) + gb_ref[dr:dr + 1, :]
        g = (jnp.minimum(pre, 0.0) - jnp.log(1.0 + jnp.exp(-jnp.abs(pre)))) * (1.0 / GLA_TAU)
        b = _chunk_cumsum(g, dr)
        b3 = b.reshape(nchunks, C, GLA_DK)
        tot = b3[:, C - 1:C, :] if dr == 0 else b3[:, 0:1, :]
        dec_sc[...] = jnp.exp(tot).reshape(nchunks, GLA_DK)
        k = k_ref[...].astype(F32)
        qin_sc[...] = (q_ref[...].astype(F32) * (GLA_DK ** -0.5) * jnp.exp(b)).astype(BF16)
        kin_sc[...] = (k * jnp.exp(-b)).astype(BF16)
        kout_sc[...] = (k * jnp.exp(tot - b3).reshape(rows_total, GLA_DK)).astype(BF16)
        if has_s0:
            s0t_sc[...] = jnp.transpose(s0_ref[dr], (1, 0))
        tri = _tri(dr)

        def super_chunk(jj, carry, dr=dr, tri=tri):
            j = jj if dr == 0 else nseg * nsc - 1 - jj
            in_seg = j % nsc
            first = (in_seg == 0) if dr == 0 else (in_seg == nsc - 1)
            last = (in_seg == nsc - 1) if dr == 0 else (in_seg == 0)

            @pl.when(first)
            def _():
                s_sc[...] = s0t_sc[...] if has_s0 else jnp.zeros_like(s_sc)

            base = j * (U * C)
            for u in range(U):
                rows = pl.ds(pl.multiple_of(base + u * C, C), C)
                v = v_ref[rows, :]
                sc = lax.dot_general(qin_sc[rows, :], kin_sc[rows, :], _NT, preferred_element_type=F32)
                o = jnp.dot(jnp.where(tri, sc, 0.0).astype(BF16), v, preferred_element_type=F32)
                st_sc[u] = lax.dot_general(v, kout_sc[rows, :], _TN, preferred_element_type=F32)
                if dr == 0:
                    o_ref[rows, :] = o
                else:
                    o_ref[rows, :] += o
            s = s_sc[...]
            for u in (range(U) if dr == 0 else reversed(range(U))):
                kv = st_sc[u]
                st_sc[u] = s
                s = dec_sc[pl.ds(j * U + u, 1), :] * s + kv
            s_sc[...] = s
            for u in range(U):
                rows = pl.ds(pl.multiple_of(base + u * C, C), C)
                o_ref[rows, :] += lax.dot_general(qin_sc[rows, :], st_sc[u].astype(BF16), _NT,
                                                  preferred_element_type=F32)
            if want_final:
                @pl.when(last)
                def _():
                    sf_ref[j // nsc, dr] = jnp.transpose(s, (1, 0))
            return carry

        lax.fori_loop(0, nseg * nsc, super_chunk, 0)


def _gla_core(proj, w2f, w2b, gate_b, s0, o_prev, row0, nseq, seqlen, nseg, want_final):
    T = proj.shape[0]
    rows = nseg * seqlen
    cps = seqlen // CHUNK
    U = math.gcd(cps, 8)
    assert row0 % rows == 0 and seqlen % CHUNK == 0 and nseq % nseg == 0
    rb = row0 // rows
    hk = GLA_H * GLA_DK
    has_s0 = s0 is not None
    assert not has_s0 or nseg == 1
    in_specs = [pl.BlockSpec((rows, GLA_DK), lambda b, h: (rb + b, h)),
                pl.BlockSpec((rows, GLA_DK), lambda b, h: (rb + b, GLA_H + h)),
                pl.BlockSpec((rows, GLA_DV), lambda b, h: (rb + b, 2 * hk // GLA_DV + h)),
                pl.BlockSpec((rows, 128), lambda b, h: (rb + b, (2 * hk + 2 * GLA_H * GLA_DV) // 128)),
                pl.BlockSpec((128, GLA_DK), lambda b, h: (0, h)),
                pl.BlockSpec((128, GLA_DK), lambda b, h: (0, h)),
                pl.BlockSpec((2, GLA_DK), lambda b, h: (0, h))]
    args = [proj, proj, proj, proj, w2f, w2b, gate_b]
    if has_s0:
        in_specs.append(pl.BlockSpec((None, 2, None, GLA_DK, GLA_DV), lambda b, h: (b, 0, h, 0, 0)))
        args.append(s0)
    in_specs.append(pl.BlockSpec(memory_space=pl.ANY))
    args.append(o_prev)
    aliases = {len(args) - 1: 0}
    out_shape = [jax.ShapeDtypeStruct((T, GLA_H * GLA_DV), F32)]
    out_specs = [pl.BlockSpec((rows, GLA_DV), lambda b, h: (rb + b, h))]
    if want_final:
        out_shape.append(jax.ShapeDtypeStruct((nseq, 2, GLA_H, GLA_DK, GLA_DV), F32))
        out_specs.append(pl.BlockSpec((nseg, 2, None, GLA_DK, GLA_DV), lambda b, h: (b, 0, h, 0, 0)))
    outs = pl.pallas_call(
        functools.partial(_gla_kernel, cps, nseg, U, has_s0, want_final),
        out_shape=tuple(out_shape),
        grid=(nseq // nseg, GLA_H),
        in_specs=in_specs,
        out_specs=tuple(out_specs),
        scratch_shapes=[pltpu.VMEM((rows, GLA_DK), BF16)] * 3
        + [pltpu.VMEM((nseg * cps, GLA_DK), F32), pltpu.VMEM((U, GLA_DV, GLA_DK), F32),
           pltpu.VMEM((GLA_DV, GLA_DK), F32), pltpu.VMEM((GLA_DV, GLA_DK), F32)],
        input_output_aliases=aliases,
        compiler_params=_cp("arbitrary", "arbitrary"),
        name="gla",
    )(*args)
    return (outs[0], outs[1]) if want_final else (outs[0], None)


def _ret_kernel(cps, nseg, U, has_s0, want_final, rope, *refs):
    q_ref, k_ref, v_ref, dm_ref, qd_ref, kd_ref, cd_ref = refs[:7]
    nxt = 7
    if rope:
        cos_ref, sin_ref = refs[7:9]
        nxt = 9
    s0_ref = refs[nxt] if has_s0 else None
    qr_sc, kr_sc, qd_sc, kd_sc, st_sc, s_sc = refs[-6:]
    outs = refs[-8:-6] if want_final else refs[-7:-6]
    o_ref = outs[0]
    sf_ref = outs[1] if want_final else None
    C = CHUNK
    nsc = cps // U
    R = U * C
    SB = 64

    def rot(x, rows):
        if not rope:
            return x
        half = x.shape[1] // 2
        swapped = jnp.concatenate([pltpu.roll(x[:, :half], half // 2, 1),
                                   pltpu.roll(x[:, half:], half // 2, 1)], axis=1)
        return x * cos_ref[rows, :] + swapped * sin_ref[rows, :]

    for dr in range(2):
        def super_chunk(jj, carry, dr=dr):
            j = jj if dr == 0 else nseg * nsc - 1 - jj
            in_seg = j % nsc
            first = (in_seg == 0) if dr == 0 else (in_seg == nsc - 1)
            last = (in_seg == nsc - 1) if dr == 0 else (in_seg == 0)

            @pl.when(first)
            def _():
                s_sc[...] = s0_ref[dr] if has_s0 else jnp.zeros_like(s_sc)

            base = pl.multiple_of(j * R, R)
            rows_r = pl.ds(base, R)
            q = rot(q_ref[rows_r, :].astype(F32), rows_r)
            k = rot(k_ref[rows_r, :].astype(F32), rows_r) * (RET_DK ** -0.5)
            qr_sc[...] = q.astype(BF16)
            kr_sc[...] = k.astype(BF16)
            qd_sc[...] = (q.reshape(U, C, RET_DK) * qd_ref[dr][None]).reshape(R, RET_DK).astype(BF16)
            kd_sc[...] = (k.reshape(U, C, RET_DK) * kd_ref[dr][None]).reshape(R, RET_DK).astype(BF16)
            for u in range(U):
                loc = pl.ds(u * C, C)
                rows = pl.ds(pl.multiple_of(base + u * C, C), C)
                v = v_ref[rows, :]
                sc = lax.dot_general(qr_sc[loc, :], kr_sc[loc, :], _NT, preferred_element_type=F32)
                o = jnp.dot((sc * dm_ref[dr]).astype(BF16), v, preferred_element_type=F32)
                st_sc[u] = lax.dot_general(kd_sc[loc, :], v, _TN, preferred_element_type=F32)
                if dr == 0:
                    o_ref[rows, :] = o
                else:
                    o_ref[rows, :] += o
            cd = cd_ref[dr]
            for r0 in range(0, RET_DK, SB):
                srows = pl.ds(r0, SB)
                s = s_sc[srows, :]
                for u in (range(U) if dr == 0 else reversed(range(U))):
                    kv = st_sc[u, srows, :]
                    st_sc[u, srows, :] = s
                    s = cd * s + kv
                s_sc[srows, :] = s
            for u in range(U):
                rows = pl.ds(pl.multiple_of(base + u * C, C), C)
                o_ref[rows, :] += jnp.dot(qd_sc[pl.ds(u * C, C), :], st_sc[u].astype(BF16),
                                          preferred_element_type=F32)
            if want_final:
                @pl.when(last)
                def _():
                    sf_ref[j // nsc, dr] = s_sc[...]
            return carry

        lax.fori_loop(0, nseg * nsc, super_chunk, 0)


def _ret_tables(log_decay):
    C = CHUNK
    lg = log_decay.astype(F32)[:, :, None, None]
    t = jnp.arange(C, dtype=F32)[:, None]
    s = jnp.arange(C, dtype=F32)[None, :]
    lag = jnp.stack([t - s, s - t])[:, None]
    dmask = jnp.where(lag >= 0, jnp.exp(jnp.maximum(lag, 0.0) * lg), 0.0)
    tl = jnp.arange(C, dtype=F32)[None, None, :, None]
    qdec = jnp.concatenate([jnp.exp((tl + 1.0) * lg[0:1]), jnp.exp((C - tl) * lg[1:2])], axis=0)
    kdec = jnp.concatenate([jnp.exp((C - 1.0 - tl) * lg[0:1]), jnp.exp(tl * lg[1:2])], axis=0)
    cdec = jnp.exp(C * lg)
    return dmask, qdec, kdec, cdec


def _rope_tables(seqlen, dk):
    half = dk // 2
    nf = half // 2
    pos = jnp.arange(seqlen, dtype=jnp.int32)
    inv = ROPE_BASE ** (-jnp.arange(nf, dtype=F32) / nf)
    ang_r = (pos // GRID_W).astype(F32)[:, None] * inv[None, :]
    ang_c = (pos % GRID_W).astype(F32)[:, None] * inv[None, :]
    cos = jnp.concatenate([jnp.cos(ang_r)] * 2 + [jnp.cos(ang_c)] * 2, axis=1)
    sin = jnp.concatenate([-jnp.sin(ang_r), jnp.sin(ang_r), -jnp.sin(ang_c), jnp.sin(ang_c)], axis=1)
    return cos, sin


def _ret_core(proj, tabs, s0, o_prev, row0, nseq, seqlen, nseg, want_final, rope):
    T = proj.shape[0]
    rows = nseg * seqlen
    cps = seqlen // CHUNK
    U = math.gcd(cps, 8)
    assert row0 % rows == 0 and seqlen % CHUNK == 0 and nseq % nseg == 0
    rb = row0 // rows
    hk = RET_H * RET_DK
    has_s0 = s0 is not None
    assert not (has_s0 or rope) or nseg == 1
    C = CHUNK
    tspec = lambda r, c: pl.BlockSpec((2, None, r, c), lambda b, h: (0, h, 0, 0))
    in_specs = [pl.BlockSpec((rows, RET_DK), lambda b, h: (rb + b, h)),
                pl.BlockSpec((rows, RET_DK), lambda b, h: (rb + b, RET_H + h)),
                pl.BlockSpec((rows, RET_DV), lambda b, h: (rb + b, 2 * hk // RET_DV + h)),
                tspec(C, C), tspec(C, 1), tspec(C, 1), tspec(1, 1)]
    args = [proj, proj, proj, *tabs]
    if rope:
        cos, sin = _rope_tables(seqlen, RET_DK)
        in_specs += [pl.BlockSpec((seqlen, RET_DK), lambda b, h: (0, 0), pipeline_mode=pl.Buffered(1))] * 2
        args += [cos, sin]
    if has_s0:
        in_specs.append(pl.BlockSpec((None, 2, None, RET_DK, RET_DV), lambda b, h: (b, 0, h, 0, 0)))
        args.append(s0)
    in_specs.append(pl.BlockSpec(memory_space=pl.ANY))
    args.append(o_prev)
    aliases = {len(args) - 1: 0}
    out_shape = [jax.ShapeDtypeStruct((T, RET_H * RET_DV), F32)]
    out_specs = [pl.BlockSpec((rows, RET_DV), lambda b, h: (rb + b, h))]
    if want_final:
        out_shape.append(jax.ShapeDtypeStruct((nseq, 2, RET_H, RET_DK, RET_DV), F32))
        out_specs.append(pl.BlockSpec((nseg, 2, None, RET_DK, RET_DV), lambda b, h: (b, 0, h, 0, 0)))
    outs = pl.pallas_call(
        functools.partial(_ret_kernel, cps, nseg, U, has_s0, want_final, rope),
        out_shape=tuple(out_shape),
        grid=(nseq // nseg, RET_H),
        in_specs=in_specs,
        out_specs=tuple(out_specs),
        scratch_shapes=[pltpu.VMEM((U * C, RET_DK), BF16)] * 4
        + [pltpu.VMEM((U, RET_DK, RET_DV), F32), pltpu.VMEM((RET_DK, RET_DV), F32)],
        input_output_aliases=aliases,
        compiler_params=_cp("arbitrary", "arbitrary"),
        name="ret",
    )(*args)
    return (outs[0], outs[1]) if want_final else (outs[0], None)


def _headnorm_out_kernel(nh, dv, center, x_ref, o_ref, gt_ref, ng_ref, g_ref, w_ref, out_ref, a_sc):
    @pl.when(pl.program_id(1) == 0)
    def _():
        for h in range(nh):
            cols = slice(h * dv, (h + 1) * dv)
            oh = o_ref[:, cols]
            if center:
                oh = oh - jnp.mean(oh, axis=-1, keepdims=True)
            oh = oh * lax.rsqrt(jnp.mean(oh * oh, axis=-1, keepdims=True) + RMS_EPS) * ng_ref[...]
            a_sc[:, cols] = (oh * _silu(gt_ref[:, cols].astype(F32))).astype(BF16)

    out_ref[...] = x_ref[...] + g_ref[...] * jnp.dot(a_sc[...], w_ref[...], preferred_element_type=F32)


def _headnorm_out(lay, x, mods, o, proj, gate_col, norm_g, w_out, nh, dv, center):
    tm = lay.tile(512)
    tn = 512
    kdim = nh * dv
    assert gate_col % kdim == 0
    return pl.pallas_call(
        functools.partial(_headnorm_out_kernel, nh, dv, center),
        out_shape=jax.ShapeDtypeStruct((lay.T, D), F32),
        grid=(lay.T // tm, D // tn),
        in_specs=[pl.BlockSpec((tm, tn), lambda i, j: (i, j)),
                  pl.BlockSpec((tm, kdim), lambda i, j: (i, 0)),
                  pl.BlockSpec((tm, kdim), lambda i, j: (i, gate_col // kdim)),
                  pl.BlockSpec((1, dv), lambda i, j: (0, 0)),
                  pl.BlockSpec((None, 1, tn), lambda i, j: (lay.group(i * tm) * MOD_CHUNKS + 2, 0, j)),
                  pl.BlockSpec((kdim, tn), lambda i, j: (0, j))],
        out_specs=pl.BlockSpec((tm, tn), lambda i, j: (i, j)),
        scratch_shapes=[pltpu.VMEM((tm, kdim), BF16)],
        compiler_params=_cp("arbitrary", "arbitrary"),
        name="headnorm_out",
    )(x, o, proj, norm_g.reshape(1, dv), mods, w_out)


def _gla_layer(lay, x, mods, p, s0):
    hk, hv = GLA_H * GLA_DK, GLA_H * GLA_DV
    w_all = jnp.concatenate([p['w_in'], p['gate_w1'][0], p['gate_w1'][1],
                             jnp.zeros((D, 128 - 2 * GLA_RANK), F32)], axis=1).astype(BF16)
    proj = _proj(lay, x, mods, 0, 1, w_all, jnp.zeros((1, w_all.shape[1]), F32), 640)
    pad = lambda w, lo: jnp.pad(w, ((lo, 128 - GLA_RANK - lo), (0, 0))).astype(BF16)
    w2f, w2b = pad(p['gate_w2'][0], 0), pad(p['gate_w2'][1], GLA_RANK)
    o = jnp.zeros((lay.T, hv), F32)
    o, s_fin = _gla_core(proj, w2f, w2b, p['gate_b'], None, o, 0, lay.B, lay.L, math.gcd(lay.B, 8), True)
    o, _ = _gla_core(proj, w2f, w2b, p['gate_b'], s0, o, lay.TP, lay.NS, lay.LS, 1, False)
    x = _headnorm_out(lay, x, mods, o, proj, 2 * hk + hv, p['norm_g'], p['w_out'].astype(BF16),
                      GLA_H, GLA_DV, False)
    return x, s_fin


def _ret_layer(lay, x, mods, p, s0):
    hk, hv = RET_H * RET_DK, RET_H * RET_DV
    proj = _proj(lay, x, mods, 0, 1, p['w_in'].astype(BF16), jnp.zeros((1, 2 * hk + 2 * hv), F32), 768)
    tabs = _ret_tables(p['log_decay'])
    o = jnp.zeros((lay.T, hv), F32)
    o, s_fin = _ret_core(proj, tabs, None, o, 0, lay.B, lay.L, math.gcd(lay.B, 8), True, False)
    o, _ = _ret_core(proj, tabs, s0, o, lay.TP, lay.NS, lay.LS, 1, False, True)
    x = _headnorm_out(lay, x, mods, o, proj, 2 * hk + hv, p['norm_g'], p['w_out'].astype(BF16),
                      RET_H, RET_DV, True)
    return x, s_fin


MOE_BM = 512
EXPERT_TF = 1792
ROUTER_LANES = 128
DMA_UNROLL = 8


def _router_kernel(x_ref, sh_ref, sc_ref, rw_ref, h_ref, idx_ref, gate_ref):
    h = _modulate(x_ref[...], sh_ref[...], sc_ref[...])
    h_ref[...] = h
    logits = jnp.dot(h, rw_ref[...], precision=HIGHEST, preferred_element_type=F32)
    lane = lax.broadcasted_iota(jnp.int32, logits.shape, 1)
    neg = jnp.float32(-jnp.inf)
    logits = jnp.where(lane < N_EXPERTS, logits, neg)
    m1 = jnp.max(logits, axis=-1, keepdims=True)
    i1 = jnp.min(jnp.where(logits == m1, lane, ROUTER_LANES), axis=-1, keepdims=True)
    rest = jnp.where(lane == i1, neg, logits)
    m2 = jnp.max(rest, axis=-1, keepdims=True)
    i2 = jnp.min(jnp.where(rest == m2, lane, ROUTER_LANES), axis=-1, keepdims=True)
    e2 = jnp.exp(m2 - m1)
    g1 = 1.0 / (1.0 + e2)
    idx_ref[:, 0:1] = i1
    idx_ref[:, 1:2] = i2
    gate_ref[:, 0:1] = g1
    gate_ref[:, 1:2] = e2 * g1


def _router(lay, x, mods, router_w):
    tm = lay.tile(512)
    rw = jnp.pad(router_w, ((0, 0), (0, ROUTER_LANES - N_EXPERTS)))
    return pl.pallas_call(
        _router_kernel,
        out_shape=(jax.ShapeDtypeStruct((lay.T, D), F32),
                   jax.ShapeDtypeStruct((lay.T, 2), jnp.int32),
                   jax.ShapeDtypeStruct((lay.T, 2), F32)),
        grid=(lay.T // tm,),
        in_specs=[pl.BlockSpec((tm, D), lambda i: (i, 0)),
                  _mod_spec(lay, tm, 3, 1), _mod_spec(lay, tm, 4, 1),
                  pl.BlockSpec((D, ROUTER_LANES), lambda i: (0, 0))],
        out_specs=(pl.BlockSpec((tm, D), lambda i: (i, 0)),
                   pl.BlockSpec((tm, 2), lambda i: (i, 0)),
                   pl.BlockSpec((tm, 2), lambda i: (i, 0))),
        compiler_params=_cp("arbitrary"),
        name="router",
    )(x, mods, mods, rw)


def _moe_plan(idx, bm):
    a = idx.size
    e = idx.reshape(a)
    onehot = (e[:, None] == jnp.arange(N_EXPERTS, dtype=jnp.int32)[None, :]).astype(jnp.int32)
    csum = jnp.cumsum(onehot, axis=0)
    counts = csum[-1]
    rank = jnp.sum((csum - onehot) * onehot, axis=-1)
    padded = (counts + bm - 1) // bm * bm
    pad_end = jnp.cumsum(padded)
    dest = ((pad_end - padded)[e] + rank).astype(jnp.int32)
    nb = -(-(a + N_EXPERTS * (bm - 1)) // bm)
    block_start = jnp.arange(nb, dtype=jnp.int32) * bm
    block_e = jnp.minimum(jnp.searchsorted(pad_end, block_start, side='right'), N_EXPERTS - 1).astype(jnp.int32)
    nvalid = (pad_end[-1] // bm).astype(jnp.int32).reshape(1)
    return dest, block_e, nvalid, nb


def _dispatch_kernel(tm, dest_ref, h_ref, xs_in, xs_hbm, sem):
    del xs_in
    i = pl.program_id(0)

    def row_copy(r, dst):
        return pltpu.make_async_copy(h_ref.at[pl.ds(r, 1)], xs_hbm.at[pl.ds(dst, 1)], sem)

    def issue(r, carry):
        a = 2 * (i * tm + r)
        row_copy(r, dest_ref[a]).start()
        row_copy(r, dest_ref[a + 1]).start()
        return carry

    def drain(r, carry):
        row_copy(r, 0).wait()
        row_copy(r, 0).wait()
        return carry

    lax.fori_loop(0, tm, issue, 0, unroll=DMA_UNROLL)
    lax.fori_loop(0, tm, drain, 0, unroll=DMA_UNROLL)


def _dispatch(lay, h, dest, nb, bm):
    tm = lay.tile(512)
    xs = jnp.zeros((nb * bm, D), F32)
    grid_spec = pltpu.PrefetchScalarGridSpec(
        num_scalar_prefetch=1,
        grid=(lay.T // tm,),
        in_specs=[pl.BlockSpec((tm, D), lambda i, d: (i, 0)), pl.BlockSpec(memory_space=pl.ANY)],
        out_specs=pl.BlockSpec(memory_space=pl.ANY),
        scratch_shapes=[pltpu.SemaphoreType.DMA(())],
    )
    return pl.pallas_call(
        functools.partial(_dispatch_kernel, tm),
        out_shape=jax.ShapeDtypeStruct(xs.shape, F32),
        grid_spec=grid_spec,
        input_output_aliases={2: 0},
        compiler_params=_cp("arbitrary"),
        name="dispatch",
    )(dest, h, xs)


def _experts_kernel(nf, be_ref, nv_ref, xs_ref, wa_ref, wb_ref, wo_ref, o_ref, xb_sc, acc_sc):
    i = pl.program_id(0)
    f = pl.program_id(1)
    valid = i < nv_ref[0]

    @pl.when(jnp.logical_and(valid, f == 0))
    def _():
        xb_sc[...] = xs_ref[...].astype(BF16)

    @pl.when(valid)
    def _():
        xb = xb_sc[...]
        a = jnp.dot(xb, wa_ref[...], preferred_element_type=F32)
        b = jnp.dot(xb, wb_ref[...], preferred_element_type=F32)
        h = (_silu(a) * b).astype(BF16)
        y = jnp.dot(h, wo_ref[...], preferred_element_type=F32)

        @pl.when(f == 0)
        def _():
            acc_sc[...] = y

        @pl.when(f > 0)
        def _():
            acc_sc[...] += y

    @pl.when(f == nf - 1)
    def _():
        o_ref[...] = jnp.where(valid, acc_sc[...], 0.0)


def _experts(xs, block_e, nvalid, nb, bm, w_in, w_out):
    tf = EXPERT_TF
    nf = EXPERT_DIM // tf

    def wmap(off):
        def imap(i, f, be, nv):
            fe = jnp.where(i < nv[0], f, nf - 1)
            return (be[i], 0, off + fe)
        return imap

    def womap(i, f, be, nv):
        fe = jnp.where(i < nv[0], f, nf - 1)
        return (be[i], fe, 0)

    grid_spec = pltpu.PrefetchScalarGridSpec(
        num_scalar_prefetch=2,
        grid=(nb, nf),
        in_specs=[pl.BlockSpec((bm, D), lambda i, f, be, nv: (jnp.minimum(i, nv[0] - 1), 0)),
                  pl.BlockSpec((None, D, tf), wmap(0)),
                  pl.BlockSpec((None, D, tf), wmap(nf)),
                  pl.BlockSpec((None, tf, D), womap)],
        out_specs=pl.BlockSpec((bm, D), lambda i, f, be, nv: (i, 0)),
        scratch_shapes=[pltpu.VMEM((bm, D), BF16), pltpu.VMEM((bm, D), F32)],
    )
    return pl.pallas_call(
        functools.partial(_experts_kernel, nf),
        out_shape=jax.ShapeDtypeStruct((nb * bm, D), F32),
        grid_spec=grid_spec,
        compiler_params=_cp("arbitrary", "arbitrary"),
        name="experts",
    )(block_e, nvalid, xs, w_in, w_in, w_out)


def _combine_kernel(tm, nt, final, dest_ref, x_ref, gate_ref, g_ref, fg_ref, ys_hbm, o_ref, y_sc, sem):
    i = pl.program_id(0)
    slot = i % 2

    def row_copy(s, k, r, src):
        return pltpu.make_async_copy(ys_hbm.at[pl.ds(src, 1)], y_sc.at[s, k, pl.ds(r, 1)], sem.at[s])

    def issue_tile(t, s):
        def issue(r, carry):
            a = 2 * (t * tm + r)
            row_copy(s, 0, r, dest_ref[a]).start()
            row_copy(s, 1, r, dest_ref[a + 1]).start()
            return carry
        lax.fori_loop(0, tm, issue, 0, unroll=DMA_UNROLL)

    @pl.when(i == 0)
    def _():
        issue_tile(0, 0)

    @pl.when(i + 1 < nt)
    def _():
        issue_tile(i + 1, 1 - slot)

    def drain(r, carry):
        row_copy(slot, 0, r, 0).wait()
        row_copy(slot, 1, r, 0).wait()
        return carry

    lax.fori_loop(0, tm, drain, 0, unroll=DMA_UNROLL)
    gate = gate_ref[...]
    out = x_ref[...] + g_ref[...] * (gate[:, 0:1] * y_sc[slot, 0] + gate[:, 1:2] * y_sc[slot, 1])
    if final:
        ms = jnp.mean(out * out, axis=-1, keepdims=True)
        out = out * lax.rsqrt(ms + RMS_EPS) * fg_ref[...]
    o_ref[...] = out


def _combine(lay, x, mods, gates, ys, dest, final_g):
    tm = lay.tile(256)
    nt = lay.T // tm
    final = final_g is not None
    fg = (final_g if final else jnp.ones((D,), F32)).reshape(1, D)
    grid_spec = pltpu.PrefetchScalarGridSpec(
        num_scalar_prefetch=1,
        grid=(nt,),
        in_specs=[pl.BlockSpec((tm, D), lambda i, d: (i, 0)),
                  pl.BlockSpec((tm, 2), lambda i, d: (i, 0)),
                  pl.BlockSpec((None, 1, D), lambda i, d: (lay.group(i * tm) * MOD_CHUNKS + 5, 0, 0)),
                  pl.BlockSpec((1, D), lambda i, d: (0, 0)),
                  pl.BlockSpec(memory_space=pl.ANY)],
        out_specs=pl.BlockSpec((tm, D), lambda i, d: (i, 0)),
        scratch_shapes=[pltpu.VMEM((2, 2, tm, D), F32), pltpu.SemaphoreType.DMA((2,))],
    )
    return pl.pallas_call(
        functools.partial(_combine_kernel, tm, nt, final),
        out_shape=jax.ShapeDtypeStruct((lay.T, D), F32),
        grid_spec=grid_spec,
        compiler_params=_cp("arbitrary"),
        name="combine",
    )(dest, x, gates, mods, fg, ys)


def _moe_layer(lay, x, mods, router_w, w_in, w_out, final_g=None, bm=MOE_BM):
    h, idx, gates = _router(lay, x, mods, router_w)
    dest, block_e, nvalid, nb = _moe_plan(idx, bm)
    xs = _dispatch(lay, h, dest, nb, bm)
    ys = _experts(xs, block_e, nvalid, nb, bm, w_in, w_out)
    return _combine(lay, x, mods, gates, ys, dest, final_g)


def kernel(x_prompt, x_sample, c, state_l0_s5_re, state_l0_s5_im, state_l2_gla, state_l3_ret, c_ctx, l0_mod_w, l0_mod_b, l0_s5_a_re, l0_s5_a_im, l0_s5_log_dt, l0_s5_b_re, l0_s5_b_im, l0_s5_c_re, l0_s5_c_im, l0_s5_d, l0_s5_glu_w, l0_ffn_w_in, l0_ffn_w_out, l1_mod_w, l1_mod_b, l1_hy_w_in, l1_hy_b_in, l1_hy_short_w, l1_hy_short_b, l1_hy_f_w1, l1_hy_f_b1, l1_hy_f_w2, l1_hy_f_b2, l1_hy_f_w3, l1_hy_f_freq, l1_hy_skip, l1_hy_w_out, l1_hy_b_out, l1_moe_router, l1_moe_w_in, l1_moe_w_out, l2_mod_w, l2_mod_b, l2_gla_w_in, l2_gla_gate_w1, l2_gla_gate_w2, l2_gla_gate_b, l2_gla_norm_g, l2_gla_w_out, l2_ffn_w_in, l2_ffn_w_out, l3_mod_w, l3_mod_b, l3_ret_w_in, l3_ret_log_decay, l3_ret_norm_g, l3_ret_w_out, l3_moe_router, l3_moe_w_in, l3_moe_w_out, final_norm_g):
    B, L, _ = x_prompt.shape
    NS, LS, _ = x_sample.shape
    lay = Layout(B, L, NS, LS)
    x = jnp.concatenate([x_prompt.reshape(B * L, D), x_sample.reshape(NS * LS, D)], axis=0)
    cond = jnp.concatenate([c_ctx[None], c, jnp.zeros((8 - 1 - NS, D), F32)], axis=0)
    mods0 = _mods(cond, l0_mod_w, l0_mod_b)
    p0 = dict(a_re=l0_s5_a_re, a_im=l0_s5_a_im, log_dt=l0_s5_log_dt, b_re=l0_s5_b_re, b_im=l0_s5_b_im,
              c_re=l0_s5_c_re, c_im=l0_s5_c_im, d=l0_s5_d, glu_w=l0_s5_glu_w.astype(BF16))
    x, s5_re, s5_im = _s5_layer(lay, x, mods0, p0, state_l0_s5_re, state_l0_s5_im)
    x = _ffn(lay, x, mods0, l0_ffn_w_in.astype(BF16), l0_ffn_w_out.astype(BF16))

    mods1 = _mods(cond, l1_mod_w, l1_mod_b)
    p1 = dict(w_in=l1_hy_w_in, b_in=l1_hy_b_in, short_w=l1_hy_short_w, short_b=l1_hy_short_b,
              f_w1=l1_hy_f_w1, f_b1=l1_hy_f_b1, f_w2=l1_hy_f_w2, f_b2=l1_hy_f_b2, f_w3=l1_hy_f_w3,
              f_freq=l1_hy_f_freq, skip=l1_hy_skip, w_out=l1_hy_w_out, b_out=l1_hy_b_out)
    x = _hyena_layer(lay, x, mods1, p1)
    x = _moe_layer(lay, x, mods1, l1_moe_router, l1_moe_w_in.astype(BF16), l1_moe_w_out.astype(BF16))

    mods2 = _mods(cond, l2_mod_w, l2_mod_b)
    p2 = dict(w_in=l2_gla_w_in, gate_w1=l2_gla_gate_w1, gate_w2=l2_gla_gate_w2, gate_b=l2_gla_gate_b,
              norm_g=l2_gla_norm_g, w_out=l2_gla_w_out)
    x, gla_state = _gla_layer(lay, x, mods2, p2, state_l2_gla)
    x = _ffn(lay, x, mods2, l2_ffn_w_in.astype(BF16), l2_ffn_w_out.astype(BF16))

    mods3 = _mods(cond, l3_mod_w, l3_mod_b)
    p3 = dict(w_in=l3_ret_w_in, log_decay=l3_ret_log_decay, norm_g=l3_ret_norm_g, w_out=l3_ret_w_out)
    x, ret_state = _ret_layer(lay, x, mods3, p3, state_l3_ret)
    y = _moe_layer(lay, x, mods3, l3_moe_router, l3_moe_w_in.astype(BF16), l3_moe_w_out.astype(BF16),
                   final_g=final_norm_g)
    return (y[:lay.TP].reshape(B, L, D), y[lay.TP:].reshape(NS, LS, D), s5_re, s5_im, gla_state, ret_state)
```

```python
import functools
import math

import jax
import jax.numpy as jnp
import numpy as np
from jax import lax
from jax.experimental import pallas as pl
from jax.experimental.pallas import tpu as pltpu

F32 = jnp.float32
BF16 = jnp.bfloat16
HIGHEST = lax.Precision.HIGHEST

D = 1024
RMS_EPS = 1e-6
MOD_CHUNKS = 6
GRID_W = 64

S5_Q = 16
S5_G = D // S5_Q
S5_P = 64
S5_T = 16
S5_SCAN_ROWS = 64

HY_BANDS = 16
HY_TARGET = 1e-2
HY_FAST_PCT = 0.3
HY_SLOW_PCT = 1.5

GLA_H, GLA_DK, GLA_DV = 4, 128, 256
GLA_RANK = 16
GLA_TAU = 16.0
RET_H, RET_DK, RET_DV = 4, 256, 512
CHUNK = 64
ROPE_BASE = 10000.0

FFN_DIM = 2816
N_EXPERTS = 8
EXPERT_DIM = 3584

VMEM_LIMIT_V7X = 56 * 1024 * 1024


def _cp(*sem):
    return pltpu.CompilerParams(dimension_semantics=sem, vmem_limit_bytes=VMEM_LIMIT_V7X)


def _silu(x):
    return x * jax.nn.sigmoid(x)


def _modulate(x, shift, scale):
    ms = jnp.mean(x * x, axis=-1, keepdims=True)
    return x * lax.rsqrt(ms + RMS_EPS) * (1.0 + scale) + shift


class Layout:
    def __init__(self, n_prompt, l_prompt, n_sample, l_sample):
        self.B, self.L, self.NS, self.LS = n_prompt, l_prompt, n_sample, l_sample
        self.TP = n_prompt * l_prompt
        self.T = self.TP + n_sample * l_sample

    def tile(self, want):
        t = math.gcd(math.gcd(self.TP, self.LS), want)
        assert t % 8 == 0
        return t

    def group(self, row):
        return jnp.where(row < self.TP, 0, 1 + (row - self.TP) // self.LS)


def _mod_spec(lay, tm, chunk, ngrid):
    def imap(*ids):
        return (lay.group(ids[0] * tm) * MOD_CHUNKS + chunk, 0, 0)
    del ngrid
    return pl.BlockSpec((None, 1, D), imap)


def _mods_kernel(c_ref, w_ref, b_ref, o_ref):
    o_ref[...] = jnp.dot(_silu(c_ref[...]), w_ref[...], precision=HIGHEST,
                         preferred_element_type=F32) + b_ref[...]


def _mods(cond, w, b):
    n = MOD_CHUNKS * D
    tn = 1536
    out = pl.pallas_call(
        _mods_kernel,
        out_shape=jax.ShapeDtypeStruct((8, n), F32),
        grid=(n // tn,),
        in_specs=[pl.BlockSpec((8, D), lambda j: (0, 0)),
                  pl.BlockSpec((D, tn), lambda j: (0, j)),
                  pl.BlockSpec((1, tn), lambda j: (0, j))],
        out_specs=pl.BlockSpec((8, tn), lambda j: (0, j)),
        compiler_params=_cp("arbitrary"),
        name="mods",
    )(cond, w, b.reshape(1, n))
    return out.reshape(8 * MOD_CHUNKS, 1, D)


def _modulate_kernel(x_ref, sh_ref, sc_ref, o_ref):
    o_ref[...] = _modulate(x_ref[...], sh_ref[...], sc_ref[...]).astype(o_ref.dtype)


def _modulate_call(lay, x, mods, c_shift, c_scale, dtype):
    tm = lay.tile(512)
    return pl.pallas_call(
        _modulate_kernel,
        out_shape=jax.ShapeDtypeStruct((lay.T, D), dtype),
        grid=(lay.T // tm,),
        in_specs=[pl.BlockSpec((tm, D), lambda i: (i, 0)),
                  _mod_spec(lay, tm, c_shift, 1), _mod_spec(lay, tm, c_scale, 1)],
        out_specs=pl.BlockSpec((tm, D), lambda i: (i, 0)),
        compiler_params=_cp("arbitrary"),
        name="modulate",
    )(x, mods, mods)


def _proj_kernel(x_ref, sh_ref, sc_ref, w_ref, b_ref, o_ref, u_sc):
    @pl.when(pl.program_id(1) == 0)
    def _():
        u_sc[...] = _modulate(x_ref[...], sh_ref[...], sc_ref[...]).astype(BF16)

    acc = jnp.dot(u_sc[...], w_ref[...], preferred_element_type=F32) + b_ref[...]
    o_ref[...] = acc.astype(o_ref.dtype)


def _proj(lay, x, mods, c_shift, c_scale, w, b, tn, out_dtype=BF16):
    tm = lay.tile(1024)
    n = w.shape[1]
    assert n % tn == 0
    return pl.pallas_call(
        _proj_kernel,
        out_shape=jax.ShapeDtypeStruct((lay.T, n), out_dtype),
        grid=(lay.T // tm, n // tn),
        in_specs=[pl.BlockSpec((tm, D), lambda i, j: (i, 0)),
                  _mod_spec(lay, tm, c_shift, 2), _mod_spec(lay, tm, c_scale, 2),
                  pl.BlockSpec((D, tn), lambda i, j: (0, j)),
                  pl.BlockSpec((1, tn), lambda i, j: (0, j))],
        out_specs=pl.BlockSpec((tm, tn), lambda i, j: (i, j)),
        scratch_shapes=[pltpu.VMEM((tm, D), BF16)],
        compiler_params=_cp("arbitrary", "arbitrary"),
        name="proj",
    )(x, mods, mods, w, b)


def _ffn_kernel(nf, x_ref, sh_ref, sc_ref, g_ref, wa_ref, wb_ref, wo_ref, o_ref, u_sc, acc_sc):
    f = pl.program_id(1)

    @pl.when(f == 0)
    def _():
        u_sc[...] = _modulate(x_ref[...], sh_ref[...], sc_ref[...]).astype(BF16)
        acc_sc[...] = jnp.zeros_like(acc_sc)

    u = u_sc[...]
    a = jnp.dot(u, wa_ref[...], preferred_element_type=F32)
    b = jnp.dot(u, wb_ref[...], preferred_element_type=F32)
    h = (_silu(a) * b).astype(BF16)
    acc_sc[...] += jnp.dot(h, wo_ref[...], preferred_element_type=F32)

    @pl.when(f == nf - 1)
    def _():
        o_ref[...] = x_ref[...] + g_ref[...] * acc_sc[...]


def _ffn(lay, x, mods, w_in, w_out):
    tm = lay.tile(512)
    tf = 1408
    nf = FFN_DIM // tf
    return pl.pallas_call(
        functools.partial(_ffn_kernel, nf),
        out_shape=jax.ShapeDtypeStruct((lay.T, D), F32),
        grid=(lay.T // tm, nf),
        in_specs=[pl.BlockSpec((tm, D), lambda i, f: (i, 0)),
                  _mod_spec(lay, tm, 3, 2), _mod_spec(lay, tm, 4, 2), _mod_spec(lay, tm, 5, 2),
                  pl.BlockSpec((D, tf), lambda i, f: (0, f)),
                  pl.BlockSpec((D, tf), lambda i, f: (0, nf + f)),
                  pl.BlockSpec((tf, D), lambda i, f: (f, 0))],
        out_specs=pl.BlockSpec((tm, D), lambda i, f: (i, 0)),
        scratch_shapes=[pltpu.VMEM((tm, D), BF16), pltpu.VMEM((tm, D), F32)],
        compiler_params=_cp("arbitrary", "arbitrary"),
        name="ffn",
    )(x, mods, mods, mods, w_in, w_in, w_out)


def _s5_tables(a_re, a_im, log_dt, b_re, b_im, c_re, c_im, d_skip):
    T, G, P, Q = S5_T, S5_G, S5_P, S5_Q
    a = lax.complex(a_re, a_im)
    adt = a * jnp.exp(log_dt)[..., None]
    lam = jnp.exp(adt)
    bb = ((lam - 1.0) / a)[..., None] * lax.complex(b_re, b_im)
    cm = lax.complex(c_re, c_im)
    steps = jnp.arange(T + 1, dtype=F32)
    pw = jnp.exp(steps[None, :, None, None] * adt[:, None])
    kern = jnp.real(jnp.einsum('dgqp,djgp,dgpr->djgqr', cm, pw[:, :T], bb))
    s_i = jnp.arange(T)[:, None]
    t_i = jnp.arange(T)[None, :]
    kf = kern[0][jnp.clip(t_i - s_i, 0, T - 1)] * (t_i >= s_i)[..., None, None, None]
    kb = kern[1][jnp.clip(s_i - t_i, 0, T - 1)] * (s_i >= t_i)[..., None, None, None]
    m = jnp.transpose(kf + kb, (2, 0, 4, 1, 3))
    eye = (jnp.eye(T)[:, None, :, None] * jnp.eye(Q)[None, :, None, :])
    m = m + eye[None] * d_skip.reshape(G, 1, 1, 1, Q)
    m = m.reshape(G, T * Q, T * Q)
    e_f = pw[0][T - 1 - jnp.arange(T)]
    e_b = pw[1][jnp.arange(T)]
    n_f = e_f[..., None] * bb[0][None]
    n_b = e_b[..., None] * bb[1][None]
    n_c = jnp.concatenate([n_f, n_b], axis=2)
    n_c = jnp.transpose(n_c, (1, 0, 3, 2)).reshape(G, T * Q, 2 * P)
    lam_t = jnp.concatenate([pw[0][T], pw[1][T]], axis=-1)
    w_f = cm[0][:, None] * jnp.transpose(pw[0][1:T + 1], (1, 0, 2))[:, :, None, :]
    w_b = cm[1][:, None] * jnp.transpose(pw[1][T - jnp.arange(T)], (1, 0, 2))[:, :, None, :]
    w_f = jnp.transpose(w_f, (0, 3, 1, 2)).reshape(G, P, T * Q)
    w_b = jnp.transpose(w_b, (0, 3, 1, 2)).reshape(G, P, T * Q)
    z = jnp.zeros_like(jnp.real(w_f))
    c_mats = dict(c_f_re=jnp.concatenate([jnp.real(w_f), z], axis=1),
                  c_f_im=jnp.concatenate([-jnp.imag(w_f), z], axis=1),
                  c_b_re=jnp.concatenate([z, jnp.real(w_b)], axis=1),
                  c_b_im=jnp.concatenate([z, -jnp.imag(w_b)], axis=1))
    return dict(m=m.astype(BF16), n_re=jnp.real(n_c).astype(BF16), n_im=jnp.imag(n_c).astype(BF16),
                l_re=jnp.real(lam_t), l_im=jnp.imag(lam_t), **{k: v.astype(BF16) for k, v in c_mats.items()})


def _s5_in_kernel(u_ref, m_ref, nre_ref, nim_ref, yi_ref, sre_ref, sim_ref):
    u = u_ref[...]
    yi_ref[...] = jnp.dot(u, m_ref[...], preferred_element_type=F32)
    sre_ref[...] = jnp.dot(u, nre_ref[...], preferred_element_type=F32)
    sim_ref[...] = jnp.dot(u, nim_ref[...], preferred_element_type=F32)


def _s5_in(ug, tabs):
    G, R, W = ug.shape
    P2 = 2 * S5_P
    gspec = lambda n: pl.BlockSpec((None, W, n), lambda g: (g, 0, 0))
    rspec = lambda n: pl.BlockSpec((None, R, n), lambda g: (g, 0, 0))
    return pl.pallas_call(
        _s5_in_kernel,
        out_shape=(jax.ShapeDtypeStruct((G, R, W), F32),
                   jax.ShapeDtypeStruct((G, R, P2), F32),
                   jax.ShapeDtypeStruct((G, R, P2), F32)),
        grid=(G,),
        in_specs=[rspec(W), gspec(W), gspec(P2), gspec(P2)],
        out_specs=(rspec(W), rspec(P2), rspec(P2)),
        compiler_params=_cp("arbitrary"),
        name="s5_in",
    )(ug, tabs['m'], tabs['n_re'], tabs['n_im'])


def _s5_scan_kernel(nsb, ncb, nblk, sref_ref, simf_ref, sreb_ref, simb_ref, lre_ref, lim_ref,
                    h0re_ref, h0im_ref, *rest):
    hfre_ref, hfim_ref, hbre_ref, hbim_ref, fre_ref, fim_ref, cre_sc, cim_sc = rest[4:]
    P = S5_P
    j = pl.program_id(1)
    fwd = lax.broadcasted_iota(jnp.int32, (1, 1, 2 * P), 2) < P
    lre = lre_ref[...][:, None, :]
    lim = lim_ref[...][:, None, :]

    @pl.when(j == 0)
    def _():
        cre_sc[...] = h0re_ref[...]
        cim_sc[...] = h0im_ref[...]

    def at(k):
        return pl.ds(k, 1) if nsb == 1 else pl.ds(k, nsb, stride=ncb)

    def body(k, carry):
        hre, him = carry
        kb = ncb - 1 - k
        hfre_ref[:, at(k), :] = hre
        hfim_ref[:, at(k), :] = him
        hbre_ref[:, at(kb), :] = hre
        hbim_ref[:, at(kb), :] = him
        sre = jnp.where(fwd, sref_ref[:, at(k), :], sreb_ref[:, at(kb), :])
        sim = jnp.where(fwd, simf_ref[:, at(k), :], simb_ref[:, at(kb), :])
        return (lre * hre - lim * him + sre, lre * him + lim * hre + sim)

    hre, him = lax.fori_loop(0, ncb, body, (cre_sc[...], cim_sc[...]))
    cre_sc[...] = hre
    cim_sc[...] = him

    @pl.when(j == nblk - 1)
    def _():
        fre_ref[...] = hre
        fim_ref[...] = him


def _s5_scan(sre, sim, tabs, h0re, h0im, hprev, row0, nseq, nc, nsb, nblk):
    G, R, P2 = sre.shape
    assert nblk == 1 or nsb == 1
    ncb = nc // nblk
    rb = nsb * ncb
    assert row0 % rb == 0 and nseq % nsb == 0 and nc % nblk == 0
    b0 = row0 // rb
    fspec = pl.BlockSpec((G, rb, P2), lambda i, j: (0, b0 + i * nblk + j, 0))
    bspec = pl.BlockSpec((G, rb, P2), lambda i, j: (0, b0 + i * nblk + nblk - 1 - j, 0))
    lspec = pl.BlockSpec((G, P2), lambda i, j: (0, 0))
    qspec = pl.BlockSpec((None, G, nsb, P2), lambda i, j: (i, 0, 0, 0))
    anyspec = pl.BlockSpec(memory_space=pl.ANY)
    fin = jax.ShapeDtypeStruct((nseq // nsb, G, nsb, P2), F32)
    outs = pl.pallas_call(
        functools.partial(_s5_scan_kernel, nsb, ncb, nblk),
        out_shape=tuple(jax.ShapeDtypeStruct(h.shape, h.dtype) for h in hprev) + (fin, fin),
        grid=(nseq // nsb, nblk),
        in_specs=[fspec, fspec, bspec, bspec, lspec, lspec, qspec, qspec] + [anyspec] * 4,
        out_specs=(fspec, fspec, bspec, bspec, qspec, qspec),
        scratch_shapes=[pltpu.VMEM((G, nsb, P2), F32), pltpu.VMEM((G, nsb, P2), F32)],
        input_output_aliases={8: 0, 9: 1, 10: 2, 11: 3},
        compiler_params=_cp("arbitrary", "arbitrary"),
        name="s5_scan",
    )(sre, sim, sre, sim, tabs['l_re'], tabs['l_im'], h0re, h0im, *hprev)
    return outs[:4], outs[4], outs[5]


def _s5_out_kernel(yi_ref, hfre_ref, hfim_ref, hbre_ref, hbim_ref, cfre_ref, cfim_ref, cbre_ref, cbim_ref,
                   y_ref):
    y = yi_ref[...]
    for h_ref, c_ref in ((hfre_ref, cfre_ref), (hfim_ref, cfim_ref), (hbre_ref, cbre_ref), (hbim_ref, cbim_ref)):
        y += jnp.dot(h_ref[...].astype(BF16), c_ref[...], preferred_element_type=F32)
    y_ref[...] = y.astype(y_ref.dtype)


def _s5_out(yi, hprev, tabs):
    G, R, W = yi.shape
    P2 = 2 * S5_P
    gspec = pl.BlockSpec((None, P2, W), lambda g: (g, 0, 0))
    hspec = pl.BlockSpec((None, R, P2), lambda g: (g, 0, 0))
    rspec = pl.BlockSpec((None, R, W), lambda g: (g, 0, 0))
    return pl.pallas_call(
        _s5_out_kernel,
        out_shape=jax.ShapeDtypeStruct((G, R, W), F32),
        grid=(G,),
        in_specs=[rspec] + [hspec] * 4 + [gspec] * 4,
        out_specs=rspec,
        compiler_params=_cp("arbitrary"),
        name="s5_out",
    )(yi, *hprev, tabs['c_f_re'], tabs['c_f_im'], tabs['c_b_re'], tabs['c_b_im'])


LANES = 128
S5_GB = LANES // S5_Q


def _block_transpose(regs):
    blk = lax.broadcasted_iota(jnp.int32, regs[0].shape, 1) // S5_Q
    regs = list(regs)
    d = S5_GB // 2
    while d:
        keep = (blk & d) == 0
        for i in range(S5_GB):
            if i & d:
                continue
            a, b = regs[i], regs[i + d]
            regs[i] = jnp.where(keep, a, pltpu.roll(b, d * S5_Q, 1))
            regs[i + d] = jnp.where(keep, pltpu.roll(a, LANES - d * S5_Q, 1), b)
        d //= 2
    return regs


def _s5_pre_kernel(tm, x_ref, sh_ref, sc_ref, ug_ref, u_sc):
    u = _modulate(x_ref[...], sh_ref[...], sc_ref[...])
    for j in range(D // LANES):
        u_sc[j] = u[:, j * LANES:(j + 1) * LANES]
    rows16 = 16
    for c in range(tm // (S5_T * rows16)):
        base = c * S5_T * rows16
        for j in range(D // LANES):
            for h in range(S5_T // S5_GB):
                regs = [u_sc[j, pl.ds(base + h * S5_GB + s, rows16, stride=S5_T), :] for s in range(S5_GB)]
                for gl, t in enumerate(_block_transpose(regs)):
                    ug_ref[j * S5_GB + gl, c * rows16:(c + 1) * rows16, h * LANES:(h + 1) * LANES] = t.astype(BF16)


def _s5_pre(lay, x, mods):
    tm = lay.tile(512)
    assert tm % (S5_T * 16) == 0
    return pl.pallas_call(
        functools.partial(_s5_pre_kernel, tm),
        out_shape=jax.ShapeDtypeStruct((S5_G, lay.T // S5_T, S5_T * S5_Q), BF16),
        grid=(lay.T // tm,),
        in_specs=[pl.BlockSpec((tm, D), lambda i: (i, 0)), _mod_spec(lay, tm, 0, 1), _mod_spec(lay, tm, 1, 1)],
        out_specs=pl.BlockSpec((S5_G, tm // S5_T, S5_T * S5_Q), lambda i: (0, i, 0)),
        scratch_shapes=[pltpu.VMEM((D // LANES, tm, LANES), F32)],
        compiler_params=_cp("arbitrary"),
        name="s5_pre",
    )(x, mods, mods)


def _s5_glu_kernel(tm, x_ref, yg_ref, g_ref, wv_ref, wg_ref, o_ref, a_sc, y_sc):
    @pl.when(pl.program_id(1) == 0)
    def _():
        def sub_tile(c, carry):
            crow = pl.ds(pl.multiple_of(c * 8, 8), 8)
            base = c * (S5_T * 8)
            for j in range(D // LANES):
                for h in range(S5_T // S5_GB):
                    regs = [yg_ref[j * S5_GB + gl, crow, h * LANES:(h + 1) * LANES] for gl in range(S5_GB)]
                    for s, t in enumerate(_block_transpose(regs)):
                        y_sc[j, pl.ds(base + h * S5_GB + s, 8, stride=S5_T), :] = t
            return carry

        lax.fori_loop(0, tm // (S5_T * 8), sub_tile, 0)
        for j in range(D // LANES):
            a_sc[:, j * LANES:(j + 1) * LANES] = jax.nn.gelu(y_sc[j]).astype(BF16)

    a = a_sc[...]
    val = jnp.dot(a, wv_ref[...], preferred_element_type=F32)
    gate = jnp.dot(a, wg_ref[...], preferred_element_type=F32)
    o_ref[...] = x_ref[...] + g_ref[...] * (val * jax.nn.sigmoid(gate))


def _s5_glu(lay, x, yg, mods, glu_w):
    tm = lay.tile(1024)
    assert tm % (S5_T * 8) == 0
    tn = 512
    nn = D // tn
    return pl.pallas_call(
        functools.partial(_s5_glu_kernel, tm),
        out_shape=jax.ShapeDtypeStruct((lay.T, D), F32),
        grid=(lay.T // tm, nn),
        in_specs=[pl.BlockSpec((tm, tn), lambda i, j: (i, j)),
                  pl.BlockSpec((S5_G, tm // S5_T, S5_T * S5_Q), lambda i, j: (0, i, 0)),
                  pl.BlockSpec((None, 1, tn), lambda i, j: (lay.group(i * tm) * MOD_CHUNKS + 2, 0, j)),
                  pl.BlockSpec((D, tn), lambda i, j: (0, j)),
                  pl.BlockSpec((D, tn), lambda i, j: (0, nn + j))],
        out_specs=pl.BlockSpec((tm, tn), lambda i, j: (i, j)),
        scratch_shapes=[pltpu.VMEM((tm, D), BF16), pltpu.VMEM((D // LANES, tm, LANES), F32)],
        compiler_params=_cp("arbitrary", "arbitrary"),
        name="s5_glu",
    )(x, yg, mods, glu_w, glu_w)


def _s5_layer(lay, x, mods, p, h0_re, h0_im):
    T, G, P, Q = S5_T, S5_G, S5_P, S5_Q
    tabs = _s5_tables(p['a_re'], p['a_im'], p['log_dt'], p['b_re'], p['b_im'], p['c_re'], p['c_im'], p['d'])
    R = lay.T // T
    ug = _s5_pre(lay, x, mods)
    yi, sre, sim = _s5_in(ug, tabs)
    hprev = tuple(jnp.zeros((G, R, 2 * P), F32) for _ in range(4))
    ncp, ncs = lay.L // T, lay.LS // T
    nsb = math.gcd(lay.B, max(1, S5_SCAN_ROWS // ncp))
    zero = jnp.zeros((lay.B // nsb, G, nsb, 2 * P), F32)
    hprev, fre, fim = _s5_scan(sre, sim, tabs, zero, zero, hprev, 0, lay.B, ncp, nsb, 1)
    to_lanes = lambda s: jnp.transpose(s, (0, 2, 1, 3)).reshape(lay.NS, G, 1, 2 * P)
    hprev, _, _ = _s5_scan(sre, sim, tabs, to_lanes(h0_re), to_lanes(h0_im), hprev,
                           lay.TP // T, lay.NS, ncs, 1, max(1, ncs // S5_SCAN_ROWS))
    yg = _s5_out(yi, hprev, tabs)
    x = _s5_glu(lay, x, yg, mods, p['glu_w'])
    from_lanes = lambda s: jnp.transpose(s, (0, 2, 1, 3)).reshape(lay.B, G, 2, P).transpose(0, 2, 1, 3)
    return x, from_lanes(fre), from_lanes(fim)


def _hyena_filters(L, p):
    mm = functools.partial(jnp.matmul, precision=HIGHEST)
    f = jnp.linspace(1e-4, HY_BANDS - 1, HY_BANDS, dtype=F32)[None, :]
    max_decay = math.log(HY_TARGET) / HY_FAST_PCT
    min_decay = math.log(HY_TARGET) / HY_SLOW_PCT
    deltas = jnp.abs(jnp.linspace(min_decay, max_decay, D, dtype=F32))
    w3 = p['f_w3'].reshape(-1, 2, 2, D)

    def side(pos, s):
        t = (pos.astype(F32) / (L - 1))[:, None]
        w = 2.0 * math.pi * pos.astype(F32)[:, None] / L
        feats = jnp.concatenate([t, jnp.cos(f * w), -jnp.sin(f * w)], axis=-1)
        z = jnp.sin(p['f_freq'][0] * (mm(feats, p['f_w1']) + p['f_b1']))
        z = jnp.sin(p['f_freq'][1] * (mm(z, p['f_w2']) + p['f_b2']))
        win = jnp.exp(-t * deltas)
        return jnp.stack([mm(z, w3[:, o, s]) * win for o in range(2)])

    j = jnp.arange(L, dtype=jnp.int32)
    k_lo = side(j, 0)
    k_hi = side((L - j) % L, 1) * (j > 0).astype(F32)[None, :, None]
    norm = jnp.sum(jnp.abs(k_lo), axis=1, keepdims=True) + jnp.sum(jnp.abs(k_hi), axis=1, keepdims=True)
    k_lo, k_hi = k_lo / norm, k_hi / norm
    alt = (1.0 - 2.0 * (j % 2).astype(F32))[None, :, None]
    k_ny = jnp.sum(alt * (k_lo + k_hi), axis=1) / (2 * L)
    return k_lo, k_hi, k_ny


def _dft_tables(L):
    r = math.isqrt(L)
    assert r * r == L
    t = jnp.arange(L, dtype=jnp.int32)[None, :]
    a = jnp.arange(r, dtype=jnp.int32)[:, None]

    def unit(idx):
        ang = (idx % (2 * L)).astype(F32) * (math.pi / L)
        return jnp.cos(ang), jnp.sin(ang)

    c1, s1 = unit(a * r * t)
    c2, s2 = unit(a * t)
    c1, s1, c2, s2 = c1[:, None], s1[:, None], c2[None], s2[None]
    cos = (c1 * c2 - s1 * s2).reshape(L, L)
    sin = (s1 * c2 + c1 * s2).reshape(L, L)
    return cos.astype(BF16), sin.astype(BF16)


def _hy_spec_kernel(L, tr, c_ref, s_ref, klo_ref, khi_ref, p_ref, q_ref):
    r = pl.program_id(2)
    f = r * tr + lax.broadcasted_iota(jnp.int32, (tr, 1), 0)
    sgn = (1 - 2 * (f % 2)).astype(F32)
    scale = jnp.where(f == 0, 1.0, 2.0) * (1.0 / (2 * L))
    c, s = c_ref[...], s_ref[...]
    lo, hi = klo_ref[...], khi_ref[...]
    dot = functools.partial(jnp.dot, preferred_element_type=F32)
    p_ref[...] = scale * (dot(c, lo) + sgn * dot(c, hi))
    q_ref[...] = scale * (dot(s, lo) + sgn * dot(s, hi))


def _hy_spectrum(L, cos, sin, k_lo, k_hi):
    tr = min(L, 512)
    tc = 512
    kspec = pl.BlockSpec((None, L, tc), lambda o, j, r: (o, 0, j))
    tspec = pl.BlockSpec((tr, L), lambda o, j, r: (r, 0))
    ospec = pl.BlockSpec((None, tr, tc), lambda o, j, r: (o, r, j))
    return pl.pallas_call(
        functools.partial(_hy_spec_kernel, L, tr),
        out_shape=(jax.ShapeDtypeStruct((2, L, D), F32), jax.ShapeDtypeStruct((2, L, D), F32)),
        grid=(2, D // tc, L // tr),
        in_specs=[tspec, tspec, kspec, kspec],
        out_specs=(ospec, ospec),
        compiler_params=_cp("arbitrary", "arbitrary", "arbitrary"),
        name="hy_spectrum",
    )(cos, sin, k_lo.astype(BF16), k_hi.astype(BF16))


def _hy_core_kernel(L, tr, ngrp, tc, x1_ref, x2_ref, v_ref, sw1_ref, sw2_ref, swv_ref, sb1_ref, sb2_ref,
                    sbv_ref, c_ref, s_ref, p_ref, q_ref, kny_ref, skip_ref, *rest):
    o_ref, z0_sc, z1_sc, x1_sc, x2_sc, a_sc, b_sc, ny_sc = rest[-8:]
    ph = pl.program_id(2)
    r = pl.program_id(3)
    nrt = L // tr
    W = ngrp * tc
    row = lax.broadcasted_iota(jnp.int32, (L, 1), 0)
    alt_all = (1 - 2 * (row % 2)).astype(F32)

    def conv3(src_ref, g, w_ref, b_ref):
        x = src_ref[g].astype(F32)
        prev = jnp.where(row == 0, 0.0, pltpu.roll(x, 1, 0))
        nxt = jnp.where(row == L - 1, 0.0, pltpu.roll(x, L - 1, 0))
        return prev * w_ref[0:1, :] + x * w_ref[1:2, :] + nxt * w_ref[2:3, :] + b_ref[...]

    @pl.when(jnp.logical_and(ph == 0, r == 0))
    def _():
        for g in range(ngrp):
            cols = slice(g * tc, (g + 1) * tc)
            x1_sc[:, cols] = conv3(x1_ref, g, sw1_ref, sb1_ref).astype(BF16)
            x2_sc[:, cols] = conv3(x2_ref, g, sw2_ref, sb2_ref).astype(BF16)
            z0_sc[:, cols] = conv3(v_ref, g, swv_ref, sbv_ref).astype(BF16)
        ny_sc[...] = jnp.sum(alt_all * z0_sc[...].astype(F32), axis=0, keepdims=True)

    rows = pl.ds(pl.multiple_of(r * tr, tr), tr)
    tile = lambda ref: jnp.concatenate([ref[...]] * ngrp, axis=1)
    dot = functools.partial(jnp.dot, preferred_element_type=F32)

    def forward(z_sc):
        z = z_sc[...]
        zre = dot(c_ref[...], z)
        zim = dot(s_ref[...], z)
        pw, qw = tile(p_ref), tile(q_ref)
        a_sc[rows, :] = (zre * pw - zim * qw).astype(BF16)
        b_sc[rows, :] = (zim * pw + zre * qw).astype(BF16)

    def inverse(order, z_sc):
        y = dot(c_ref[...], a_sc[...]) + dot(s_ref[...], b_sc[...])
        t = r * tr + lax.broadcasted_iota(jnp.int32, (tr, 1), 0)
        alt = (1 - 2 * (t % 2)).astype(F32)
        kny = jnp.concatenate([kny_ref[order:order + 1, :]] * ngrp, axis=1)
        skip = jnp.concatenate([skip_ref[order:order + 1, :]] * ngrp, axis=1)
        return y + alt * (ny_sc[...] * kny) + skip * z_sc[rows, :].astype(F32)

    @pl.when(ph == 0)
    def _():
        forward(z0_sc)

    @pl.when(ph == 1)
    def _():
        z1_sc[rows, :] = (x1_sc[rows, :].astype(F32) * inverse(0, z0_sc)).astype(BF16)

        @pl.when(r == nrt - 1)
        def _():
            ny_sc[...] = jnp.sum(alt_all * z1_sc[...].astype(F32), axis=0, keepdims=True)

    @pl.when(ph == 2)
    def _():
        forward(z1_sc)

    @pl.when(ph == 3)
    def _():
        out = x2_sc[rows, :].astype(F32) * inverse(1, z1_sc)
        for g in range(ngrp):
            o_ref[g, rows, :] = out[:, g * tc:(g + 1) * tc].astype(o_ref.dtype)


def _hy_core(proj, short_w, short_b, skip, cos, sin, pq, k_ny, o_prev, row0, nseq, L, ngrp, tc):
    T = proj.shape[0]
    tr = min(L, 512)
    nrt = L // tr
    assert row0 % (L * ngrp) == 0 and nseq % ngrp == 0 and T % L == 0
    sb0 = row0 // (L * ngrp)
    nct = D // tc
    p3 = proj.reshape(T // L, L, 3 * D)
    p_arr, q_arr = pq

    def xspec(part):
        return pl.BlockSpec((ngrp, L, tc), lambda i, j, ph, r: (sb0 + i, 0, part * nct + j))

    def wspec(part, rows_):
        return pl.BlockSpec((rows_, tc), lambda i, j, ph, r: (0, part * nct + j))

    def pq_map(i, j, ph, r):
        return (ph // 2, jnp.where(ph % 2 == 0, r, nrt - 1), j)

    tspec = pl.BlockSpec((tr, L), lambda i, j, ph, r: (r, 0))
    in_specs = [xspec(0), xspec(1), xspec(2), wspec(0, 3), wspec(1, 3), wspec(2, 3),
                wspec(0, 1), wspec(1, 1), wspec(2, 1), tspec, tspec,
                pl.BlockSpec((None, tr, tc), pq_map), pl.BlockSpec((None, tr, tc), pq_map),
                pl.BlockSpec((2, tc), lambda i, j, ph, r: (0, j)),
                pl.BlockSpec((2, tc), lambda i, j, ph, r: (0, j))]
    sb = short_b.reshape(1, 3 * D)
    args = [p3, p3, p3, short_w, short_w, short_w, sb, sb, sb, cos, sin, p_arr, q_arr, k_ny, skip]
    aliases = {}
    if o_prev is not None:
        in_specs.append(pl.BlockSpec(memory_space=pl.ANY))
        args.append(o_prev.reshape(T // L, L, D))
        aliases = {len(args) - 1: 0}
    W = ngrp * tc
    out = pl.pallas_call(
        functools.partial(_hy_core_kernel, L, tr, ngrp, tc),
        out_shape=jax.ShapeDtypeStruct((T // L, L, D), BF16),
        grid=(nseq // ngrp, nct, 4, nrt),
        in_specs=in_specs,
        out_specs=pl.BlockSpec((ngrp, L, tc), lambda i, j, ph, r: (sb0 + i, 0, j)),
        scratch_shapes=[pltpu.VMEM((L, W), BF16)] * 6 + [pltpu.VMEM((1, W), F32)],
        input_output_aliases=aliases,
        compiler_params=_cp("arbitrary", "arbitrary", "arbitrary", "arbitrary"),
        name="hy_core",
    )(*args)
    return out.reshape(T, D)


def _plain_out_kernel(x_ref, z_ref, g_ref, w_ref, b_ref, o_ref):
    acc = jnp.dot(z_ref[...], w_ref[...], preferred_element_type=F32) + b_ref[...]
    o_ref[...] = x_ref[...] + g_ref[...] * acc


def _plain_out(lay, x, mods, z, w, b):
    tm = lay.tile(1024)
    tn = 512
    kdim = z.shape[1]
    return pl.pallas_call(
        _plain_out_kernel,
        out_shape=jax.ShapeDtypeStruct((lay.T, D), F32),
        grid=(lay.T // tm, D // tn),
        in_specs=[pl.BlockSpec((tm, tn), lambda i, j: (i, j)),
                  pl.BlockSpec((tm, kdim), lambda i, j: (i, 0)),
                  pl.BlockSpec((None, 1, tn), lambda i, j: (lay.group(i * tm) * MOD_CHUNKS + 2, 0, j)),
                  pl.BlockSpec((kdim, tn), lambda i, j: (0, j)),
                  pl.BlockSpec((1, tn), lambda i, j: (0, j))],
        out_specs=pl.BlockSpec((tm, tn), lambda i, j: (i, j)),
        compiler_params=_cp("arbitrary", "arbitrary"),
        name="plain_out",
    )(x, z, mods, w, b.reshape(1, D))


def _hyena_layer(lay, x, mods, p):
    proj = _proj(lay, x, mods, 0, 1, p['w_in'].astype(BF16), p['b_in'].reshape(1, 3 * D), 768)
    z = jnp.zeros((lay.T, D), BF16)
    for row0, nseq, L, ngrp, tc in ((0, lay.B, lay.L, math.gcd(lay.B, 4), 256),
                                    (lay.TP, lay.NS, lay.LS, lay.NS, 128)):
        k_lo, k_hi, k_ny = _hyena_filters(L, p)
        cos, sin = _dft_tables(L)
        pq = _hy_spectrum(L, cos, sin, k_lo, k_hi)
        z = _hy_core(proj, p['short_w'], p['short_b'], p['skip'], cos, sin, pq, k_ny, z, row0, nseq, L, ngrp, tc)
    return _plain_out(lay, x, mods, z, p['w_out'].astype(BF16), p['b_out'])


_NT = (((1,), (1,)), ((), ()))
_TN = (((0,), (0,)), ((), ()))


def _tri(dr):
    t = lax.broadcasted_iota(jnp.int32, (CHUNK, CHUNK), 0)
    s = lax.broadcasted_iota(jnp.int32, (CHUNK, CHUNK), 1)
    return (s <= t) if dr == 0 else (s >= t)


def _chunk_cumsum(g, dr):
    n = g.shape[0]
    pos = lax.broadcasted_iota(jnp.int32, g.shape, 0) % CHUNK
    sh = 1
    while sh < CHUNK:
        if dr == 0:
            g = g + jnp.where(pos >= sh, pltpu.roll(g, sh, 0), 0.0)
        else:
            g = g + jnp.where(pos < CHUNK - sh, pltpu.roll(g, n - sh, 0), 0.0)
        sh *= 2
    return g


def _gla_kernel(cps, nseg, U, has_s0, want_final, *refs):
    q_ref, k_ref, v_ref, lr_ref, w2f_ref, w2b_ref, gb_ref = refs[:7]
    s0_ref = refs[7] if has_s0 else None
    qin_sc, kin_sc, kout_sc, dec_sc, st_sc, s_sc, s0t_sc = refs[-7:]
    outs = refs[-9:-7] if want_final else refs[-8:-7]
    o_ref = outs[0]
    sf_ref = outs[1] if want_final else None
    C = CHUNK
    nsc = cps // U
    nchunks = nseg * cps
    rows_total = nchunks * C
    w2 = (w2f_ref, w2b_ref)

    for dr in range(2):
        pre = jnp.dot(lr_ref[...], w2[dr][...], preferred_element_type=F32) + gb_ref[dr:dr + 1, :]
        g = (jnp.minimum(pre, 0.0) - jnp.log(1.0 + jnp.exp(-jnp.abs(pre)))) * (1.0 / GLA_TAU)
        b = _chunk_cumsum(g, dr)
        b3 = b.reshape(nchunks, C, GLA_DK)
        tot = b3[:, C - 1:C, :] if dr == 0 else b3[:, 0:1, :]
        dec_sc[...] = jnp.exp(tot).reshape(nchunks, GLA_DK)
        k = k_ref[...].astype(F32)
        qin_sc[...] = (q_ref[...].astype(F32) * (GLA_DK ** -0.5) * jnp.exp(b)).astype(BF16)
        kin_sc[...] = (k * jnp.exp(-b)).astype(BF16)
        kout_sc[...] = (k * jnp.exp(tot - b3).reshape(rows_total, GLA_DK)).astype(BF16)
        if has_s0:
            s0t_sc[...] = jnp.transpose(s0_ref[dr], (1, 0))
        tri = _tri(dr)

        def super_chunk(jj, carry, dr=dr, tri=tri):
            j = jj if dr == 0 else nseg * nsc - 1 - jj
            in_seg = j % nsc
            first = (in_seg == 0) if dr == 0 else (in_seg == nsc - 1)
            last = (in_seg == nsc - 1) if dr == 0 else (in_seg == 0)

            @pl.when(first)
            def _():
                s_sc[...] = s0t_sc[...] if has_s0 else jnp.zeros_like(s_sc)

            base = j * (U * C)
            for u in range(U):
                rows = pl.ds(pl.multiple_of(base + u * C, C), C)
                v = v_ref[rows, :]
                sc = lax.dot_general(qin_sc[rows, :], kin_sc[rows, :], _NT, preferred_element_type=F32)
                o = jnp.dot(jnp.where(tri, sc, 0.0).astype(BF16), v, preferred_element_type=F32)
                st_sc[u] = lax.dot_general(v, kout_sc[rows, :], _TN, preferred_element_type=F32)
                if dr == 0:
                    o_ref[rows, :] = o
                else:
                    o_ref[rows, :] += o
            s = s_sc[...]
            for u in (range(U) if dr == 0 else reversed(range(U))):
                kv = st_sc[u]
                st_sc[u] = s
                s = dec_sc[pl.ds(j * U + u, 1), :] * s + kv
            s_sc[...] = s
            for u in range(U):
                rows = pl.ds(pl.multiple_of(base + u * C, C), C)
                o_ref[rows, :] += lax.dot_general(qin_sc[rows, :], st_sc[u].astype(BF16), _NT,
                                                  preferred_element_type=F32)
            if want_final:
                @pl.when(last)
                def _():
                    sf_ref[j // nsc, dr] = jnp.transpose(s, (1, 0))
            return carry

        lax.fori_loop(0, nseg * nsc, super_chunk, 0)


def _gla_core(proj, w2f, w2b, gate_b, s0, o_prev, row0, nseq, seqlen, nseg, want_final):
    T = proj.shape[0]
    rows = nseg * seqlen
    cps = seqlen // CHUNK
    U = math.gcd(cps, 8)
    assert row0 % rows == 0 and seqlen % CHUNK == 0 and nseq % nseg == 0
    rb = row0 // rows
    hk = GLA_H * GLA_DK
    has_s0 = s0 is not None
    assert not has_s0 or nseg == 1
    in_specs = [pl.BlockSpec((rows, GLA_DK), lambda b, h: (rb + b, h)),
                pl.BlockSpec((rows, GLA_DK), lambda b, h: (rb + b, GLA_H + h)),
                pl.BlockSpec((rows, GLA_DV), lambda b, h: (rb + b, 2 * hk // GLA_DV + h)),
                pl.BlockSpec((rows, 128), lambda b, h: (rb + b, (2 * hk + 2 * GLA_H * GLA_DV) // 128)),
                pl.BlockSpec((128, GLA_DK), lambda b, h: (0, h)),
                pl.BlockSpec((128, GLA_DK), lambda b, h: (0, h)),
                pl.BlockSpec((2, GLA_DK), lambda b, h: (0, h))]
    args = [proj, proj, proj, proj, w2f, w2b, gate_b]
    if has_s0:
        in_specs.append(pl.BlockSpec((None, 2, None, GLA_DK, GLA_DV), lambda b, h: (b, 0, h, 0, 0)))
        args.append(s0)
    in_specs.append(pl.BlockSpec(memory_space=pl.ANY))
    args.append(o_prev)
    aliases = {len(args) - 1: 0}
    out_shape = [jax.ShapeDtypeStruct((T, GLA_H * GLA_DV), F32)]
    out_specs = [pl.BlockSpec((rows, GLA_DV), lambda b, h: (rb + b, h))]
    if want_final:
        out_shape.append(jax.ShapeDtypeStruct((nseq, 2, GLA_H, GLA_DK, GLA_DV), F32))
        out_specs.append(pl.BlockSpec((nseg, 2, None, GLA_DK, GLA_DV), lambda b, h: (b, 0, h, 0, 0)))
    outs = pl.pallas_call(
        functools.partial(_gla_kernel, cps, nseg, U, has_s0, want_final),
        out_shape=tuple(out_shape),
        grid=(nseq // nseg, GLA_H),
        in_specs=in_specs,
        out_specs=tuple(out_specs),
        scratch_shapes=[pltpu.VMEM((rows, GLA_DK), BF16)] * 3
        + [pltpu.VMEM((nseg * cps, GLA_DK), F32), pltpu.VMEM((U, GLA_DV, GLA_DK), F32),
           pltpu.VMEM((GLA_DV, GLA_DK), F32), pltpu.VMEM((GLA_DV, GLA_DK), F32)],
        input_output_aliases=aliases,
        compiler_params=_cp("arbitrary", "arbitrary"),
        name="gla",
    )(*args)
    return (outs[0], outs[1]) if want_final else (outs[0], None)


def _ret_kernel(cps, nseg, U, has_s0, want_final, rope, *refs):
    q_ref, k_ref, v_ref, dm_ref, qd_ref, kd_ref, cd_ref = refs[:7]
    nxt = 7
    if rope:
        cos_ref, sin_ref = refs[7:9]
        nxt = 9
    s0_ref = refs[nxt] if has_s0 else None
    qr_sc, kr_sc, qd_sc, kd_sc, st_sc, s_sc = refs[-6:]
    outs = refs[-8:-6] if want_final else refs[-7:-6]
    o_ref = outs[0]
    sf_ref = outs[1] if want_final else None
    C = CHUNK
    nsc = cps // U
    R = U * C
    SB = 64

    def rot(x, rows):
        if not rope:
            return x
        half = x.shape[1] // 2
        swapped = jnp.concatenate([pltpu.roll(x[:, :half], half // 2, 1),
                                   pltpu.roll(x[:, half:], half // 2, 1)], axis=1)
        return x * cos_ref[rows, :] + swapped * sin_ref[rows, :]

    for dr in range(2):
        def super_chunk(jj, carry, dr=dr):
            j = jj if dr == 0 else nseg * nsc - 1 - jj
            in_seg = j % nsc
            first = (in_seg == 0) if dr == 0 else (in_seg == nsc - 1)
            last = (in_seg == nsc - 1) if dr == 0 else (in_seg == 0)

            @pl.when(first)
            def _():
                s_sc[...] = s0_ref[dr] if has_s0 else jnp.zeros_like(s_sc)

            base = pl.multiple_of(j * R, R)
            rows_r = pl.ds(base, R)
            q = rot(q_ref[rows_r, :].astype(F32), rows_r)
            k = rot(k_ref[rows_r, :].astype(F32), rows_r) * (RET_DK ** -0.5)
            qr_sc[...] = q.astype(BF16)
            kr_sc[...] = k.astype(BF16)
            qd_sc[...] = (q.reshape(U, C, RET_DK) * qd_ref[dr][None]).reshape(R, RET_DK).astype(BF16)
            kd_sc[...] = (k.reshape(U, C, RET_DK) * kd_ref[dr][None]).reshape(R, RET_DK).astype(BF16)
            for u in range(U):
                loc = pl.ds(u * C, C)
                rows = pl.ds(pl.multiple_of(base + u * C, C), C)
                v = v_ref[rows, :]
                sc = lax.dot_general(qr_sc[loc, :], kr_sc[loc, :], _NT, preferred_element_type=F32)
                o = jnp.dot((sc * dm_ref[dr]).astype(BF16), v, preferred_element_type=F32)
                st_sc[u] = lax.dot_general(kd_sc[loc, :], v, _TN, preferred_element_type=F32)
                if dr == 0:
                    o_ref[rows, :] = o
                else:
                    o_ref[rows, :] += o
            cd = cd_ref[dr]
            for r0 in range(0, RET_DK, SB):
                srows = pl.ds(r0, SB)
                s = s_sc[srows, :]
                for u in (range(U) if dr == 0 else reversed(range(U))):
                    kv = st_sc[u, srows, :]
                    st_sc[u, srows, :] = s
                    s = cd * s + kv
                s_sc[srows, :] = s
            for u in range(U):
                rows = pl.ds(pl.multiple_of(base + u * C, C), C)
                o_ref[rows, :] += jnp.dot(qd_sc[pl.ds(u * C, C), :], st_sc[u].astype(BF16),
                                          preferred_element_type=F32)
            if want_final:
                @pl.when(last)
                def _():
                    sf_ref[j // nsc, dr] = s_sc[...]
            return carry

        lax.fori_loop(0, nseg * nsc, super_chunk, 0)


def _ret_tables(log_decay):
    C = CHUNK
    lg = log_decay.astype(F32)[:, :, None, None]
    t = jnp.arange(C, dtype=F32)[:, None]
    s = jnp.arange(C, dtype=F32)[None, :]
    lag = jnp.stack([t - s, s - t])[:, None]
    dmask = jnp.where(lag >= 0, jnp.exp(jnp.maximum(lag, 0.0) * lg), 0.0)
    tl = jnp.arange(C, dtype=F32)[None, None, :, None]
    qdec = jnp.concatenate([jnp.exp((tl + 1.0) * lg[0:1]), jnp.exp((C - tl) * lg[1:2])], axis=0)
    kdec = jnp.concatenate([jnp.exp((C - 1.0 - tl) * lg[0:1]), jnp.exp(tl * lg[1:2])], axis=0)
    cdec = jnp.exp(C * lg)
    return dmask, qdec, kdec, cdec


def _rope_tables(seqlen, dk):
    half = dk // 2
    nf = half // 2
    pos = jnp.arange(seqlen, dtype=jnp.int32)
    inv = ROPE_BASE ** (-jnp.arange(nf, dtype=F32) / nf)
    ang_r = (pos // GRID_W).astype(F32)[:, None] * inv[None, :]
    ang_c = (pos % GRID_W).astype(F32)[:, None] * inv[None, :]
    cos = jnp.concatenate([jnp.cos(ang_r)] * 2 + [jnp.cos(ang_c)] * 2, axis=1)
    sin = jnp.concatenate([-jnp.sin(ang_r), jnp.sin(ang_r), -jnp.sin(ang_c), jnp.sin(ang_c)], axis=1)
    return cos, sin


def _ret_core(proj, tabs, s0, o_prev, row0, nseq, seqlen, nseg, want_final, rope):
    T = proj.shape[0]
    rows = nseg * seqlen
    cps = seqlen // CHUNK
    U = math.gcd(cps, 8)
    assert row0 % rows == 0 and seqlen % CHUNK == 0 and nseq % nseg == 0
    rb = row0 // rows
    hk = RET_H * RET_DK
    has_s0 = s0 is not None
    assert not (has_s0 or rope) or nseg == 1
    C = CHUNK
    tspec = lambda r, c: pl.BlockSpec((2, None, r, c), lambda b, h: (0, h, 0, 0))
    in_specs = [pl.BlockSpec((rows, RET_DK), lambda b, h: (rb + b, h)),
                pl.BlockSpec((rows, RET_DK), lambda b, h: (rb + b, RET_H + h)),
                pl.BlockSpec((rows, RET_DV), lambda b, h: (rb + b, 2 * hk // RET_DV + h)),
                tspec(C, C), tspec(C, 1), tspec(C, 1), tspec(1, 1)]
    args = [proj, proj, proj, *tabs]
    if rope:
        cos, sin = _rope_tables(seqlen, RET_DK)
        in_specs += [pl.BlockSpec((seqlen, RET_DK), lambda b, h: (0, 0), pipeline_mode=pl.Buffered(1))] * 2
        args += [cos, sin]
    if has_s0:
        in_specs.append(pl.BlockSpec((None, 2, None, RET_DK, RET_DV), lambda b, h: (b, 0, h, 0, 0)))
        args.append(s0)
    in_specs.append(pl.BlockSpec(memory_space=pl.ANY))
    args.append(o_prev)
    aliases = {len(args) - 1: 0}
    out_shape = [jax.ShapeDtypeStruct((T, RET_H * RET_DV), F32)]
    out_specs = [pl.BlockSpec((rows, RET_DV), lambda b, h: (rb + b, h))]
    if want_final:
        out_shape.append(jax.ShapeDtypeStruct((nseq, 2, RET_H, RET_DK, RET_DV), F32))
        out_specs.append(pl.BlockSpec((nseg, 2, None, RET_DK, RET_DV), lambda b, h: (b, 0, h, 0, 0)))
    outs = pl.pallas_call(
        functools.partial(_ret_kernel, cps, nseg, U, has_s0, want_final, rope),
        out_shape=tuple(out_shape),
        grid=(nseq // nseg, RET_H),
        in_specs=in_specs,
        out_specs=tuple(out_specs),
        scratch_shapes=[pltpu.VMEM((U * C, RET_DK), BF16)] * 4
        + [pltpu.VMEM((U, RET_DK, RET_DV), F32), pltpu.VMEM((RET_DK, RET_DV), F32)],
        input_output_aliases=aliases,
        compiler_params=_cp("arbitrary", "arbitrary"),
        name="ret",
    )(*args)
    return (outs[0], outs[1]) if want_final else (outs[0], None)


def _headnorm_out_kernel(nh, dv, center, x_ref, o_ref, gt_ref, ng_ref, g_ref, w_ref, out_ref, a_sc):
    @pl.when(pl.program_id(1) == 0)
    def _():
        for h in range(nh):
            cols = slice(h * dv, (h + 1) * dv)
            oh = o_ref[:, cols]
            if center:
                oh = oh - jnp.mean(oh, axis=-1, keepdims=True)
            oh = oh * lax.rsqrt(jnp.mean(oh * oh, axis=-1, keepdims=True) + RMS_EPS) * ng_ref[...]
            a_sc[:, cols] = (oh * _silu(gt_ref[:, cols].astype(F32))).astype(BF16)

    out_ref[...] = x_ref[...] + g_ref[...] * jnp.dot(a_sc[...], w_ref[...], preferred_element_type=F32)


def _headnorm_out(lay, x, mods, o, proj, gate_col, norm_g, w_out, nh, dv, center):
    tm = lay.tile(512)
    tn = 512
    kdim = nh * dv
    assert gate_col % kdim == 0
    return pl.pallas_call(
        functools.partial(_headnorm_out_kernel, nh, dv, center),
        out_shape=jax.ShapeDtypeStruct((lay.T, D), F32),
        grid=(lay.T // tm, D // tn),
        in_specs=[pl.BlockSpec((tm, tn), lambda i, j: (i, j)),
                  pl.BlockSpec((tm, kdim), lambda i, j: (i, 0)),
                  pl.BlockSpec((tm, kdim), lambda i, j: (i, gate_col // kdim)),
                  pl.BlockSpec((1, dv), lambda i, j: (0, 0)),
                  pl.BlockSpec((None, 1, tn), lambda i, j: (lay.group(i * tm) * MOD_CHUNKS + 2, 0, j)),
                  pl.BlockSpec((kdim, tn), lambda i, j: (0, j))],
        out_specs=pl.BlockSpec((tm, tn), lambda i, j: (i, j)),
        scratch_shapes=[pltpu.VMEM((tm, kdim), BF16)],
        compiler_params=_cp("arbitrary", "arbitrary"),
        name="headnorm_out",
    )(x, o, proj, norm_g.reshape(1, dv), mods, w_out)


def _gla_layer(lay, x, mods, p, s0):
    hk, hv = GLA_H * GLA_DK, GLA_H * GLA_DV
    w_all = jnp.concatenate([p['w_in'], p['gate_w1'][0], p['gate_w1'][1],
                             jnp.zeros((D, 128 - 2 * GLA_RANK), F32)], axis=1).astype(BF16)
    proj = _proj(lay, x, mods, 0, 1, w_all, jnp.zeros((1, w_all.shape[1]), F32), 640)
    pad = lambda w, lo: jnp.pad(w, ((lo, 128 - GLA_RANK - lo), (0, 0))).astype(BF16)
    w2f, w2b = pad(p['gate_w2'][0], 0), pad(p['gate_w2'][1], GLA_RANK)
    o = jnp.zeros((lay.T, hv), F32)
    o, s_fin = _gla_core(proj, w2f, w2b, p['gate_b'], None, o, 0, lay.B, lay.L, math.gcd(lay.B, 8), True)
    o, _ = _gla_core(proj, w2f, w2b, p['gate_b'], s0, o, lay.TP, lay.NS, lay.LS, 1, False)
    x = _headnorm_out(lay, x, mods, o, proj, 2 * hk + hv, p['norm_g'], p['w_out'].astype(BF16),
                      GLA_H, GLA_DV, False)
    return x, s_fin


def _ret_layer(lay, x, mods, p, s0):
    hk, hv = RET_H * RET_DK, RET_H * RET_DV
    proj = _proj(lay, x, mods, 0, 1, p['w_in'].astype(BF16), jnp.zeros((1, 2 * hk + 2 * hv), F32), 768)
    tabs = _ret_tables(p['log_decay'])
    o = jnp.zeros((lay.T, hv), F32)
    o, s_fin = _ret_core(proj, tabs, None, o, 0, lay.B, lay.L, math.gcd(lay.B, 8), True, False)
    o, _ = _ret_core(proj, tabs, s0, o, lay.TP, lay.NS, lay.LS, 1, False, True)
    x = _headnorm_out(lay, x, mods, o, proj, 2 * hk + hv, p['norm_g'], p['w_out'].astype(BF16),
                      RET_H, RET_DV, True)
    return x, s_fin


MOE_BM = 512
EXPERT_TF = 1792
ROUTER_LANES = 128
DMA_UNROLL = 8


def _router_kernel(x_ref, sh_ref, sc_ref, rw_ref, h_ref, idx_ref, gate_ref):
    h = _modulate(x_ref[...], sh_ref[...], sc_ref[...])
    h_ref[...] = h
    logits = jnp.dot(h, rw_ref[...], precision=HIGHEST, preferred_element_type=F32)
    lane = lax.broadcasted_iota(jnp.int32, logits.shape, 1)
    neg = jnp.float32(-jnp.inf)
    logits = jnp.where(lane < N_EXPERTS, logits, neg)
    m1 = jnp.max(logits, axis=-1, keepdims=True)
    i1 = jnp.min(jnp.where(logits == m1, lane, ROUTER_LANES), axis=-1, keepdims=True)
    rest = jnp.where(lane == i1, neg, logits)
    m2 = jnp.max(rest, axis=-1, keepdims=True)
    i2 = jnp.min(jnp.where(rest == m2, lane, ROUTER_LANES), axis=-1, keepdims=True)
    e2 = jnp.exp(m2 - m1)
    g1 = 1.0 / (1.0 + e2)
    idx_ref[:, 0:1] = i1
    idx_ref[:, 1:2] = i2
    gate_ref[:, 0:1] = g1
    gate_ref[:, 1:2] = e2 * g1


def _router(lay, x, mods, router_w):
    tm = lay.tile(512)
    rw = jnp.pad(router_w, ((0, 0), (0, ROUTER_LANES - N_EXPERTS)))
    return pl.pallas_call(
        _router_kernel,
        out_shape=(jax.ShapeDtypeStruct((lay.T, D), F32),
                   jax.ShapeDtypeStruct((lay.T, 2), jnp.int32),
                   jax.ShapeDtypeStruct((lay.T, 2), F32)),
        grid=(lay.T // tm,),
        in_specs=[pl.BlockSpec((tm, D), lambda i: (i, 0)),
                  _mod_spec(lay, tm, 3, 1), _mod_spec(lay, tm, 4, 1),
                  pl.BlockSpec((D, ROUTER_LANES), lambda i: (0, 0))],
        out_specs=(pl.BlockSpec((tm, D), lambda i: (i, 0)),
                   pl.BlockSpec((tm, 2), lambda i: (i, 0)),
                   pl.BlockSpec((tm, 2), lambda i: (i, 0))),
        compiler_params=_cp("arbitrary"),
        name="router",
    )(x, mods, mods, rw)


def _moe_plan(idx, bm):
    a = idx.size
    e = idx.reshape(a)
    onehot = (e[:, None] == jnp.arange(N_EXPERTS, dtype=jnp.int32)[None, :]).astype(jnp.int32)
    csum = jnp.cumsum(onehot, axis=0)
    counts = csum[-1]
    rank = jnp.sum((csum - onehot) * onehot, axis=-1)
    padded = (counts + bm - 1) // bm * bm
    pad_end = jnp.cumsum(padded)
    dest = ((pad_end - padded)[e] + rank).astype(jnp.int32)
    nb = -(-(a + N_EXPERTS * (bm - 1)) // bm)
    block_start = jnp.arange(nb, dtype=jnp.int32) * bm
    block_e = jnp.minimum(jnp.searchsorted(pad_end, block_start, side='right'), N_EXPERTS - 1).astype(jnp.int32)
    nvalid = (pad_end[-1] // bm).astype(jnp.int32).reshape(1)
    return dest, block_e, nvalid, nb


def _dispatch_kernel(tm, dest_ref, h_ref, xs_in, xs_hbm, sem):
    del xs_in
    i = pl.program_id(0)

    def row_copy(r, dst):
        return pltpu.make_async_copy(h_ref.at[pl.ds(r, 1)], xs_hbm.at[pl.ds(dst, 1)], sem)

    def issue(r, carry):
        a = 2 * (i * tm + r)
        row_copy(r, dest_ref[a]).start()
        row_copy(r, dest_ref[a + 1]).start()
        return carry

    def drain(r, carry):
        row_copy(r, 0).wait()
        row_copy(r, 0).wait()
        return carry

    lax.fori_loop(0, tm, issue, 0, unroll=DMA_UNROLL)
    lax.fori_loop(0, tm, drain, 0, unroll=DMA_UNROLL)


def _dispatch(lay, h, dest, nb, bm):
    tm = lay.tile(512)
    xs = jnp.zeros((nb * bm, D), F32)
    grid_spec = pltpu.PrefetchScalarGridSpec(
        num_scalar_prefetch=1,
        grid=(lay.T // tm,),
        in_specs=[pl.BlockSpec((tm, D), lambda i, d: (i, 0)), pl.BlockSpec(memory_space=pl.ANY)],
        out_specs=pl.BlockSpec(memory_space=pl.ANY),
        scratch_shapes=[pltpu.SemaphoreType.DMA(())],
    )
    return pl.pallas_call(
        functools.partial(_dispatch_kernel, tm),
        out_shape=jax.ShapeDtypeStruct(xs.shape, F32),
        grid_spec=grid_spec,
        input_output_aliases={2: 0},
        compiler_params=_cp("arbitrary"),
        name="dispatch",
    )(dest, h, xs)


def _experts_kernel(nf, be_ref, nv_ref, xs_ref, wa_ref, wb_ref, wo_ref, o_ref, xb_sc, acc_sc):
    i = pl.program_id(0)
    f = pl.program_id(1)
    valid = i < nv_ref[0]

    @pl.when(jnp.logical_and(valid, f == 0))
    def _():
        xb_sc[...] = xs_ref[...].astype(BF16)

    @pl.when(valid)
    def _():
        xb = xb_sc[...]
        a = jnp.dot(xb, wa_ref[...], preferred_element_type=F32)
        b = jnp.dot(xb, wb_ref[...], preferred_element_type=F32)
        h = (_silu(a) * b).astype(BF16)
        y = jnp.dot(h, wo_ref[...], preferred_element_type=F32)

        @pl.when(f == 0)
        def _():
            acc_sc[...] = y

        @pl.when(f > 0)
        def _():
            acc_sc[...] += y

    @pl.when(f == nf - 1)
    def _():
        o_ref[...] = jnp.where(valid, acc_sc[...], 0.0)


def _experts(xs, block_e, nvalid, nb, bm, w_in, w_out):
    tf = EXPERT_TF
    nf = EXPERT_DIM // tf

    def wmap(off):
        def imap(i, f, be, nv):
            fe = jnp.where(i < nv[0], f, nf - 1)
            return (be[i], 0, off + fe)
        return imap

    def womap(i, f, be, nv):
        fe = jnp.where(i < nv[0], f, nf - 1)
        return (be[i], fe, 0)

    grid_spec = pltpu.PrefetchScalarGridSpec(
        num_scalar_prefetch=2,
        grid=(nb, nf),
        in_specs=[pl.BlockSpec((bm, D), lambda i, f, be, nv: (jnp.minimum(i, nv[0] - 1), 0)),
                  pl.BlockSpec((None, D, tf), wmap(0)),
                  pl.BlockSpec((None, D, tf), wmap(nf)),
                  pl.BlockSpec((None, tf, D), womap)],
        out_specs=pl.BlockSpec((bm, D), lambda i, f, be, nv: (i, 0)),
        scratch_shapes=[pltpu.VMEM((bm, D), BF16), pltpu.VMEM((bm, D), F32)],
    )
    return pl.pallas_call(
        functools.partial(_experts_kernel, nf),
        out_shape=jax.ShapeDtypeStruct((nb * bm, D), F32),
        grid_spec=grid_spec,
        compiler_params=_cp("arbitrary", "arbitrary"),
        name="experts",
    )(block_e, nvalid, xs, w_in, w_in, w_out)


def _combine_kernel(tm, nt, final, dest_ref, x_ref, gate_ref, g_ref, fg_ref, ys_hbm, o_ref, y_sc, sem):
    i = pl.program_id(0)
    slot = i % 2

    def row_copy(s, k, r, src):
        return pltpu.make_async_copy(ys_hbm.at[pl.ds(src, 1)], y_sc.at[s, k, pl.ds(r, 1)], sem.at[s])

    def issue_tile(t, s):
        def issue(r, carry):
            a = 2 * (t * tm + r)
            row_copy(s, 0, r, dest_ref[a]).start()
            row_copy(s, 1, r, dest_ref[a + 1]).start()
            return carry
        lax.fori_loop(0, tm, issue, 0, unroll=DMA_UNROLL)

    @pl.when(i == 0)
    def _():
        issue_tile(0, 0)

    @pl.when(i + 1 < nt)
    def _():
        issue_tile(i + 1, 1 - slot)

    def drain(r, carry):
        row_copy(slot, 0, r, 0).wait()
        row_copy(slot, 1, r, 0).wait()
        return carry

    lax.fori_loop(0, tm, drain, 0, unroll=DMA_UNROLL)
    gate = gate_ref[...]
    out = x_ref[...] + g_ref[...] * (gate[:, 0:1] * y_sc[slot, 0] + gate[:, 1:2] * y_sc[slot, 1])
    if final:
        ms = jnp.mean(out * out, axis=-1, keepdims=True)
        out = out * lax.rsqrt(ms + RMS_EPS) * fg_ref[...]
    o_ref[...] = out


def _combine(lay, x, mods, gates, ys, dest, final_g):
    tm = lay.tile(256)
    nt = lay.T // tm
    final = final_g is not None
    fg = (final_g if final else jnp.ones((D,), F32)).reshape(1, D)
    grid_spec = pltpu.PrefetchScalarGridSpec(
        num_scalar_prefetch=1,
        grid=(nt,),
        in_specs=[pl.BlockSpec((tm, D), lambda i, d: (i, 0)),
                  pl.BlockSpec((tm, 2), lambda i, d: (i, 0)),
                  pl.BlockSpec((None, 1, D), lambda i, d: (lay.group(i * tm) * MOD_CHUNKS + 5, 0, 0)),
                  pl.BlockSpec((1, D), lambda i, d: (0, 0)),
                  pl.BlockSpec(memory_space=pl.ANY)],
        out_specs=pl.BlockSpec((tm, D), lambda i, d: (i, 0)),
        scratch_shapes=[pltpu.VMEM((2, 2, tm, D), F32), pltpu.SemaphoreType.DMA((2,))],
    )
    return pl.pallas_call(
        functools.partial(_combine_kernel, tm, nt, final),
        out_shape=jax.ShapeDtypeStruct((lay.T, D), F32),
        grid_spec=grid_spec,
        compiler_params=_cp("arbitrary"),
        name="combine",
    )(dest, x, gates, mods, fg, ys)


def _moe_layer(lay, x, mods, router_w, w_in, w_out, final_g=None, bm=MOE_BM):
    h, idx, gates = _router(lay, x, mods, router_w)
    dest, block_e, nvalid, nb = _moe_plan(idx, bm)
    xs = _dispatch(lay, h, dest, nb, bm)
    ys = _experts(xs, block_e, nvalid, nb, bm, w_in, w_out)
    return _combine(lay, x, mods, gates, ys, dest, final_g)


def kernel(x_prompt, x_sample, c, state_l0_s5_re, state_l0_s5_im, state_l2_gla, state_l3_ret, c_ctx, l0_mod_w, l0_mod_b, l0_s5_a_re, l0_s5_a_im, l0_s5_log_dt, l0_s5_b_re, l0_s5_b_im, l0_s5_c_re, l0_s5_c_im, l0_s5_d, l0_s5_glu_w, l0_ffn_w_in, l0_ffn_w_out, l1_mod_w, l1_mod_b, l1_hy_w_in, l1_hy_b_in, l1_hy_short_w, l1_hy_short_b, l1_hy_f_w1, l1_hy_f_b1, l1_hy_f_w2, l1_hy_f_b2, l1_hy_f_w3, l1_hy_f_freq, l1_hy_skip, l1_hy_w_out, l1_hy_b_out, l1_moe_router, l1_moe_w_in, l1_moe_w_out, l2_mod_w, l2_mod_b, l2_gla_w_in, l2_gla_gate_w1, l2_gla_gate_w2, l2_gla_gate_b, l2_gla_norm_g, l2_gla_w_out, l2_ffn_w_in, l2_ffn_w_out, l3_mod_w, l3_mod_b, l3_ret_w_in, l3_ret_log_decay, l3_ret_norm_g, l3_ret_w_out, l3_moe_router, l3_moe_w_in, l3_moe_w_out, final_norm_g):
    B, L, _ = x_prompt.shape
    NS, LS, _ = x_sample.shape
    lay = Layout(B, L, NS, LS)
    x = jnp.concatenate([x_prompt.reshape(B * L, D), x_sample.reshape(NS * LS, D)], axis=0)
    cond = jnp.concatenate([c_ctx[None], c, jnp.zeros((8 - 1 - NS, D), F32)], axis=0)
    mods0 = _mods(cond, l0_mod_w, l0_mod_b)
    p0 = dict(a_re=l0_s5_a_re, a_im=l0_s5_a_im, log_dt=l0_s5_log_dt, b_re=l0_s5_b_re, b_im=l0_s5_b_im,
              c_re=l0_s5_c_re, c_im=l0_s5_c_im, d=l0_s5_d, glu_w=l0_s5_glu_w.astype(BF16))
    x, s5_re, s5_im = _s5_layer(lay, x, mods0, p0, state_l0_s5_re, state_l0_s5_im)
    x = _ffn(lay, x, mods0, l0_ffn_w_in.astype(BF16), l0_ffn_w_out.astype(BF16))

    mods1 = _mods(cond, l1_mod_w, l1_mod_b)
    p1 = dict(w_in=l1_hy_w_in, b_in=l1_hy_b_in, short_w=l1_hy_short_w, short_b=l1_hy_short_b,
              f_w1=l1_hy_f_w1, f_b1=l1_hy_f_b1, f_w2=l1_hy_f_w2, f_b2=l1_hy_f_b2, f_w3=l1_hy_f_w3,
              f_freq=l1_hy_f_freq, skip=l1_hy_skip, w_out=l1_hy_w_out, b_out=l1_hy_b_out)
    x = _hyena_layer(lay, x, mods1, p1)
    x = _moe_layer(lay, x, mods1, l1_moe_router, l1_moe_w_in.astype(BF16), l1_moe_w_out.astype(BF16))

    mods2 = _mods(cond, l2_mod_w, l2_mod_b)
    p2 = dict(w_in=l2_gla_w_in, gate_w1=l2_gla_gate_w1, gate_w2=l2_gla_gate_w2, gate_b=l2_gla_gate_b,
              norm_g=l2_gla_norm_g, w_out=l2_gla_w_out)
    x, gla_state = _gla_layer(lay, x, mods2, p2, state_l2_gla)
    x = _ffn(lay, x, mods2, l2_ffn_w_in.astype(BF16), l2_ffn_w_out.astype(BF16))

    mods3 = _mods(cond, l3_mod_w, l3_mod_b)
    p3 = dict(w_in=l3_ret_w_in, log_decay=l3_ret_log_decay, norm_g=l3_ret_norm_g, w_out=l3_ret_w_out)
    x, ret_state = _ret_layer(lay, x, mods3, p3, state_l3_ret)
    y = _moe_layer(lay, x, mods3, l3_moe_router, l3_moe_w_in.astype(BF16), l3_moe_w_out.astype(BF16),
                   final_g=final_norm_g)
    return (y[:lay.TP].reshape(B, L, D), y[lay.TP:].reshape(NS, LS, D), s5_re, s5_im, gla_state, ret_state)
```

```python
import functools
import math

import jax
import jax.numpy as jnp
import numpy as np
from jax import lax
from jax.experimental import pallas as pl
from jax.experimental.pallas import tpu as pltpu

F32 = jnp.float32
BF16 = jnp.bfloat16
HIGHEST = lax.Precision.HIGHEST

D = 1024
RMS_EPS = 1e-6
MOD_CHUNKS = 6
GRID_W = 64

S5_Q = 16
S5_G = D // S5_Q
S5_P = 64
S5_T = 16
S5_SCAN_ROWS = 64

HY_BANDS = 16
HY_TARGET = 1e-2
HY_FAST_PCT = 0.3
HY_SLOW_PCT = 1.5

GLA_H, GLA_DK, GLA_DV = 4, 128, 256
GLA_RANK = 16
GLA_TAU = 16.0
RET_H, RET_DK, RET_DV = 4, 256, 512
CHUNK = 64
RET_CHUNK = 256
ROPE_BASE = 10000.0

FFN_DIM = 2816
N_EXPERTS = 8
EXPERT_DIM = 3584

VMEM_LIMIT_V7X = 56 * 1024 * 1024


def _cp(*sem):
    return pltpu.CompilerParams(dimension_semantics=sem, vmem_limit_bytes=VMEM_LIMIT_V7X)


def _silu(x):
    return x * jax.nn.sigmoid(x)


def _modulate(x, shift, scale):
    ms = jnp.mean(x * x, axis=-1, keepdims=True)
    return x * lax.rsqrt(ms + RMS_EPS) * (1.0 + scale) + shift


class Layout:
    def __init__(self, n_prompt, l_prompt, n_sample, l_sample):
        self.B, self.L, self.NS, self.LS = n_prompt, l_prompt, n_sample, l_sample
        self.TP = n_prompt * l_prompt
        self.T = self.TP + n_sample * l_sample

    def tile(self, want):
        t = math.gcd(math.gcd(self.TP, self.LS), want)
        assert t % 8 == 0
        return t

    def group(self, row):
        return jnp.where(row < self.TP, 0, 1 + (row - self.TP) // self.LS)


def _mod_spec(lay, tm, chunk, ngrid):
    def imap(*ids):
        return (lay.group(ids[0] * tm) * MOD_CHUNKS + chunk, 0, 0)
    del ngrid
    return pl.BlockSpec((None, 1, D), imap)


def _mods_kernel(c_ref, w_ref, b_ref, o_ref):
    o_ref[...] = jnp.dot(_silu(c_ref[...]), w_ref[...], precision=HIGHEST,
                         preferred_element_type=F32) + b_ref[...]


def _mods(cond, w, b):
    n = MOD_CHUNKS * D
    tn = 1536
    out = pl.pallas_call(
        _mods_kernel,
        out_shape=jax.ShapeDtypeStruct((8, n), F32),
        grid=(n // tn,),
        in_specs=[pl.BlockSpec((8, D), lambda j: (0, 0)),
                  pl.BlockSpec((D, tn), lambda j: (0, j)),
                  pl.BlockSpec((1, tn), lambda j: (0, j))],
        out_specs=pl.BlockSpec((8, tn), lambda j: (0, j)),
        compiler_params=_cp("arbitrary"),
        name="mods",
    )(cond, w, b.reshape(1, n))
    return out.reshape(8 * MOD_CHUNKS, 1, D)


def _modulate_kernel(x_ref, sh_ref, sc_ref, o_ref):
    o_ref[...] = _modulate(x_ref[...], sh_ref[...], sc_ref[...]).astype(o_ref.dtype)


def _modulate_call(lay, x, mods, c_shift, c_scale, dtype):
    tm = lay.tile(512)
    return pl.pallas_call(
        _modulate_kernel,
        out_shape=jax.ShapeDtypeStruct((lay.T, D), dtype),
        grid=(lay.T // tm,),
        in_specs=[pl.BlockSpec((tm, D), lambda i: (i, 0)),
                  _mod_spec(lay, tm, c_shift, 1), _mod_spec(lay, tm, c_scale, 1)],
        out_specs=pl.BlockSpec((tm, D), lambda i: (i, 0)),
        compiler_params=_cp("arbitrary"),
        name="modulate",
    )(x, mods, mods)


def _proj_kernel(x_ref, sh_ref, sc_ref, w_ref, b_ref, o_ref, u_sc):
    @pl.when(pl.program_id(1) == 0)
    def _():
        u_sc[...] = _modulate(x_ref[...], sh_ref[...], sc_ref[...]).astype(BF16)

    acc = jnp.dot(u_sc[...], w_ref[...], preferred_element_type=F32) + b_ref[...]
    o_ref[...] = acc.astype(o_ref.dtype)


def _proj(lay, x, mods, c_shift, c_scale, w, b, tn, out_dtype=BF16):
    tm = lay.tile(1024)
    n = w.shape[1]
    assert n % tn == 0
    return pl.pallas_call(
        _proj_kernel,
        out_shape=jax.ShapeDtypeStruct((lay.T, n), out_dtype),
        grid=(lay.T // tm, n // tn),
        in_specs=[pl.BlockSpec((tm, D), lambda i, j: (i, 0)),
                  _mod_spec(lay, tm, c_shift, 2), _mod_spec(lay, tm, c_scale, 2),
                  pl.BlockSpec((D, tn), lambda i, j: (0, j)),
                  pl.BlockSpec((1, tn), lambda i, j: (0, j))],
        out_specs=pl.BlockSpec((tm, tn), lambda i, j: (i, j)),
        scratch_shapes=[pltpu.VMEM((tm, D), BF16)],
        compiler_params=_cp("arbitrary", "arbitrary"),
        name="proj",
    )(x, mods, mods, w, b)


def _ffn_kernel(nf, x_ref, sh_ref, sc_ref, g_ref, wa_ref, wb_ref, wo_ref, o_ref, u_sc, acc_sc):
    f = pl.program_id(1)

    @pl.when(f == 0)
    def _():
        u_sc[...] = _modulate(x_ref[...], sh_ref[...], sc_ref[...]).astype(BF16)
        acc_sc[...] = jnp.zeros_like(acc_sc)

    u = u_sc[...]
    a = jnp.dot(u, wa_ref[...], preferred_element_type=F32)
    b = jnp.dot(u, wb_ref[...], preferred_element_type=F32)
    h = (_silu(a) * b).astype(BF16)
    acc_sc[...] += jnp.dot(h, wo_ref[...], preferred_element_type=F32)

    @pl.when(f == nf - 1)
    def _():
        o_ref[...] = x_ref[...] + g_ref[...] * acc_sc[...]


def _ffn(lay, x, mods, w_in, w_out):
    tm = lay.tile(512)
    tf = 1408
    nf = FFN_DIM // tf
    return pl.pallas_call(
        functools.partial(_ffn_kernel, nf),
        out_shape=jax.ShapeDtypeStruct((lay.T, D), F32),
        grid=(lay.T // tm, nf),
        in_specs=[pl.BlockSpec((tm, D), lambda i, f: (i, 0)),
                  _mod_spec(lay, tm, 3, 2), _mod_spec(lay, tm, 4, 2), _mod_spec(lay, tm, 5, 2),
                  pl.BlockSpec((D, tf), lambda i, f: (0, f)),
                  pl.BlockSpec((D, tf), lambda i, f: (0, nf + f)),
                  pl.BlockSpec((tf, D), lambda i, f: (f, 0))],
        out_specs=pl.BlockSpec((tm, D), lambda i, f: (i, 0)),
        scratch_shapes=[pltpu.VMEM((tm, D), BF16), pltpu.VMEM((tm, D), F32)],
        compiler_params=_cp("arbitrary", "arbitrary"),
        name="ffn",
    )(x, mods, mods, mods, w_in, w_in, w_out)


def _s5_tables(a_re, a_im, log_dt, b_re, b_im, c_re, c_im, d_skip):
    T, G, P, Q = S5_T, S5_G, S5_P, S5_Q
    a = lax.complex(a_re, a_im)
    adt = a * jnp.exp(log_dt)[..., None]
    lam = jnp.exp(adt)
    bb = ((lam - 1.0) / a)[..., None] * lax.complex(b_re, b_im)
    cm = lax.complex(c_re, c_im)
    steps = jnp.arange(T + 1, dtype=F32)
    pw = jnp.exp(steps[None, :, None, None] * adt[:, None])
    kern = jnp.real(jnp.einsum('dgqp,djgp,dgpr->djgqr', cm, pw[:, :T], bb))
    s_i = jnp.arange(T)[:, None]
    t_i = jnp.arange(T)[None, :]
    kf = kern[0][jnp.clip(t_i - s_i, 0, T - 1)] * (t_i >= s_i)[..., None, None, None]
    kb = kern[1][jnp.clip(s_i - t_i, 0, T - 1)] * (s_i >= t_i)[..., None, None, None]
    m = jnp.transpose(kf + kb, (2, 0, 4, 1, 3))
    eye = (jnp.eye(T)[:, None, :, None] * jnp.eye(Q)[None, :, None, :])
    m = m + eye[None] * d_skip.reshape(G, 1, 1, 1, Q)
    m = m.reshape(G, T * Q, T * Q)
    e_f = pw[0][T - 1 - jnp.arange(T)]
    e_b = pw[1][jnp.arange(T)]
    n_f = e_f[..., None] * bb[0][None]
    n_b = e_b[..., None] * bb[1][None]
    n_c = jnp.concatenate([n_f, n_b], axis=2)
    n_c = jnp.transpose(n_c, (1, 0, 3, 2)).reshape(G, T * Q, 2 * P)
    lam_t = jnp.concatenate([pw[0][T], pw[1][T]], axis=-1)
    w_f = cm[0][:, None] * jnp.transpose(pw[0][1:T + 1], (1, 0, 2))[:, :, None, :]
    w_b = cm[1][:, None] * jnp.transpose(pw[1][T - jnp.arange(T)], (1, 0, 2))[:, :, None, :]
    w_f = jnp.transpose(w_f, (0, 3, 1, 2)).reshape(G, P, T * Q)
    w_b = jnp.transpose(w_b, (0, 3, 1, 2)).reshape(G, P, T * Q)
    z = jnp.zeros_like(jnp.real(w_f))
    c_mats = dict(c_f_re=jnp.concatenate([jnp.real(w_f), z], axis=1),
                  c_f_im=jnp.concatenate([-jnp.imag(w_f), z], axis=1),
                  c_b_re=jnp.concatenate([z, jnp.real(w_b)], axis=1),
                  c_b_im=jnp.concatenate([z, -jnp.imag(w_b)], axis=1))
    return dict(m=m.astype(BF16), n_re=jnp.real(n_c).astype(BF16), n_im=jnp.imag(n_c).astype(BF16),
                l_re=jnp.real(lam_t), l_im=jnp.imag(lam_t), **{k: v.astype(BF16) for k, v in c_mats.items()})


def _s5_in_kernel(u_ref, m_ref, nre_ref, nim_ref, yi_ref, sre_ref, sim_ref):
    u = u_ref[...]
    yi_ref[...] = jnp.dot(u, m_ref[...], preferred_element_type=F32)
    sre_ref[...] = jnp.dot(u, nre_ref[...], preferred_element_type=F32)
    sim_ref[...] = jnp.dot(u, nim_ref[...], preferred_element_type=F32)


def _s5_in(ug, tabs):
    G, R, W = ug.shape
    P2 = 2 * S5_P
    gspec = lambda n: pl.BlockSpec((None, W, n), lambda g: (g, 0, 0))
    rspec = lambda n: pl.BlockSpec((None, R, n), lambda g: (g, 0, 0))
    return pl.pallas_call(
        _s5_in_kernel,
        out_shape=(jax.ShapeDtypeStruct((G, R, W), F32),
                   jax.ShapeDtypeStruct((G, R, P2), F32),
                   jax.ShapeDtypeStruct((G, R, P2), F32)),
        grid=(G,),
        in_specs=[rspec(W), gspec(W), gspec(P2), gspec(P2)],
        out_specs=(rspec(W), rspec(P2), rspec(P2)),
        compiler_params=_cp("arbitrary"),
        name="s5_in",
    )(ug, tabs['m'], tabs['n_re'], tabs['n_im'])


def _s5_scan_kernel(nsb, ncb, nblk, sref_ref, simf_ref, sreb_ref, simb_ref, lre_ref, lim_ref,
                    h0re_ref, h0im_ref, *rest):
    hfre_ref, hfim_ref, hbre_ref, hbim_ref, fre_ref, fim_ref, cre_sc, cim_sc = rest[4:]
    P = S5_P
    j = pl.program_id(1)
    fwd = lax.broadcasted_iota(jnp.int32, (1, 1, 2 * P), 2) < P
    lre = lre_ref[...][:, None, :]
    lim = lim_ref[...][:, None, :]

    @pl.when(j == 0)
    def _():
        cre_sc[...] = h0re_ref[...]
        cim_sc[...] = h0im_ref[...]

    def at(k):
        return pl.ds(k, 1) if nsb == 1 else pl.ds(k, nsb, stride=ncb)

    def body(k, carry):
        hre, him = carry
        kb = ncb - 1 - k
        hfre_ref[:, at(k), :] = hre
        hfim_ref[:, at(k), :] = him
        hbre_ref[:, at(kb), :] = hre
        hbim_ref[:, at(kb), :] = him
        sre = jnp.where(fwd, sref_ref[:, at(k), :], sreb_ref[:, at(kb), :])
        sim = jnp.where(fwd, simf_ref[:, at(k), :], simb_ref[:, at(kb), :])
        return (lre * hre - lim * him + sre, lre * him + lim * hre + sim)

    hre, him = lax.fori_loop(0, ncb, body, (cre_sc[...], cim_sc[...]))
    cre_sc[...] = hre
    cim_sc[...] = him

    @pl.when(j == nblk - 1)
    def _():
        fre_ref[...] = hre
        fim_ref[...] = him


def _s5_scan(sre, sim, tabs, h0re, h0im, hprev, row0, nseq, nc, nsb, nblk):
    G, R, P2 = sre.shape
    assert nblk == 1 or nsb == 1
    ncb = nc // nblk
    rb = nsb * ncb
    assert row0 % rb == 0 and nseq % nsb == 0 and nc % nblk == 0
    b0 = row0 // rb
    fspec = pl.BlockSpec((G, rb, P2), lambda i, j: (0, b0 + i * nblk + j, 0))
    bspec = pl.BlockSpec((G, rb, P2), lambda i, j: (0, b0 + i * nblk + nblk - 1 - j, 0))
    lspec = pl.BlockSpec((G, P2), lambda i, j: (0, 0))
    qspec = pl.BlockSpec((None, G, nsb, P2), lambda i, j: (i, 0, 0, 0))
    anyspec = pl.BlockSpec(memory_space=pl.ANY)
    fin = jax.ShapeDtypeStruct((nseq // nsb, G, nsb, P2), F32)
    outs = pl.pallas_call(
        functools.partial(_s5_scan_kernel, nsb, ncb, nblk),
        out_shape=tuple(jax.ShapeDtypeStruct(h.shape, h.dtype) for h in hprev) + (fin, fin),
        grid=(nseq // nsb, nblk),
        in_specs=[fspec, fspec, bspec, bspec, lspec, lspec, qspec, qspec] + [anyspec] * 4,
        out_specs=(fspec, fspec, bspec, bspec, qspec, qspec),
        scratch_shapes=[pltpu.VMEM((G, nsb, P2), F32), pltpu.VMEM((G, nsb, P2), F32)],
        input_output_aliases={8: 0, 9: 1, 10: 2, 11: 3},
        compiler_params=_cp("arbitrary", "arbitrary"),
        name="s5_scan",
    )(sre, sim, sre, sim, tabs['l_re'], tabs['l_im'], h0re, h0im, *hprev)
    return outs[:4], outs[4], outs[5]


def _s5_out_kernel(yi_ref, hfre_ref, hfim_ref, hbre_ref, hbim_ref, cfre_ref, cfim_ref, cbre_ref, cbim_ref,
                   y_ref):
    y = yi_ref[...]
    for h_ref, c_ref in ((hfre_ref, cfre_ref), (hfim_ref, cfim_ref), (hbre_ref, cbre_ref), (hbim_ref, cbim_ref)):
        y += jnp.dot(h_ref[...].astype(BF16), c_ref[...], preferred_element_type=F32)
    y_ref[...] = y.astype(y_ref.dtype)


def _s5_out(yi, hprev, tabs):
    G, R, W = yi.shape
    P2 = 2 * S5_P
    gspec = pl.BlockSpec((None, P2, W), lambda g: (g, 0, 0))
    hspec = pl.BlockSpec((None, R, P2), lambda g: (g, 0, 0))
    rspec = pl.BlockSpec((None, R, W), lambda g: (g, 0, 0))
    return pl.pallas_call(
        _s5_out_kernel,
        out_shape=jax.ShapeDtypeStruct((G, R, W), F32),
        grid=(G,),
        in_specs=[rspec] + [hspec] * 4 + [gspec] * 4,
        out_specs=rspec,
        compiler_params=_cp("arbitrary"),
        name="s5_out",
    )(yi, *hprev, tabs['c_f_re'], tabs['c_f_im'], tabs['c_b_re'], tabs['c_b_im'])


LANES = 128
S5_GB = LANES // S5_Q


def _block_transpose(regs):
    blk = lax.broadcasted_iota(jnp.int32, regs[0].shape, 1) // S5_Q
    regs = list(regs)
    d = S5_GB // 2
    while d:
        keep = (blk & d) == 0
        for i in range(S5_GB):
            if i & d:
                continue
            a, b = regs[i], regs[i + d]
            regs[i] = jnp.where(keep, a, pltpu.roll(b, d * S5_Q, 1))
            regs[i + d] = jnp.where(keep, pltpu.roll(a, LANES - d * S5_Q, 1), b)
        d //= 2
    return regs


def _s5_pre_kernel(tm, x_ref, sh_ref, sc_ref, ug_ref, u_sc):
    u = _modulate(x_ref[...], sh_ref[...], sc_ref[...])
    for j in range(D // LANES):
        u_sc[j] = u[:, j * LANES:(j + 1) * LANES]
    rows16 = 16
    for c in range(tm // (S5_T * rows16)):
        base = c * S5_T * rows16
        for j in range(D // LANES):
            for h in range(S5_T // S5_GB):
                regs = [u_sc[j, pl.ds(base + h * S5_GB + s, rows16, stride=S5_T), :] for s in range(S5_GB)]
                for gl, t in enumerate(_block_transpose(regs)):
                    ug_ref[j * S5_GB + gl, c * rows16:(c + 1) * rows16, h * LANES:(h + 1) * LANES] = t.astype(BF16)


def _s5_pre(lay, x, mods):
    tm = lay.tile(512)
    assert tm % (S5_T * 16) == 0
    return pl.pallas_call(
        functools.partial(_s5_pre_kernel, tm),
        out_shape=jax.ShapeDtypeStruct((S5_G, lay.T // S5_T, S5_T * S5_Q), BF16),
        grid=(lay.T // tm,),
        in_specs=[pl.BlockSpec((tm, D), lambda i: (i, 0)), _mod_spec(lay, tm, 0, 1), _mod_spec(lay, tm, 1, 1)],
        out_specs=pl.BlockSpec((S5_G, tm // S5_T, S5_T * S5_Q), lambda i: (0, i, 0)),
        scratch_shapes=[pltpu.VMEM((D // LANES, tm, LANES), F32)],
        compiler_params=_cp("arbitrary"),
        name="s5_pre",
    )(x, mods, mods)


def _s5_glu_kernel(tm, x_ref, yg_ref, g_ref, wv_ref, wg_ref, o_ref, a_sc, y_sc):
    @pl.when(pl.program_id(1) == 0)
    def _():
        def sub_tile(c, carry):
            crow = pl.ds(pl.multiple_of(c * 8, 8), 8)
            base = c * (S5_T * 8)
            for j in range(D // LANES):
                for h in range(S5_T // S5_GB):
                    regs = [yg_ref[j * S5_GB + gl, crow, h * LANES:(h + 1) * LANES] for gl in range(S5_GB)]
                    for s, t in enumerate(_block_transpose(regs)):
                        y_sc[j, pl.ds(base + h * S5_GB + s, 8, stride=S5_T), :] = t
            return carry

        lax.fori_loop(0, tm // (S5_T * 8), sub_tile, 0)
        for j in range(D // LANES):
            a_sc[:, j * LANES:(j + 1) * LANES] = jax.nn.gelu(y_sc[j]).astype(BF16)

    a = a_sc[...]
    val = jnp.dot(a, wv_ref[...], preferred_element_type=F32)
    gate = jnp.dot(a, wg_ref[...], preferred_element_type=F32)
    o_ref[...] = x_ref[...] + g_ref[...] * (val * jax.nn.sigmoid(gate))


def _s5_glu(lay, x, yg, mods, glu_w):
    tm = lay.tile(1024)
    assert tm % (S5_T * 8) == 0
    tn = 512
    nn = D // tn
    return pl.pallas_call(
        functools.partial(_s5_glu_kernel, tm),
        out_shape=jax.ShapeDtypeStruct((lay.T, D), F32),
        grid=(lay.T // tm, nn),
        in_specs=[pl.BlockSpec((tm, tn), lambda i, j: (i, j)),
                  pl.BlockSpec((S5_G, tm // S5_T, S5_T * S5_Q), lambda i, j: (0, i, 0)),
                  pl.BlockSpec((None, 1, tn), lambda i, j: (lay.group(i * tm) * MOD_CHUNKS + 2, 0, j)),
                  pl.BlockSpec((D, tn), lambda i, j: (0, j)),
                  pl.BlockSpec((D, tn), lambda i, j: (0, nn + j))],
        out_specs=pl.BlockSpec((tm, tn), lambda i, j: (i, j)),
        scratch_shapes=[pltpu.VMEM((tm, D), BF16), pltpu.VMEM((D // LANES, tm, LANES), F32)],
        compiler_params=_cp("arbitrary", "arbitrary"),
        name="s5_glu",
    )(x, yg, mods, glu_w, glu_w)


def _s5_layer(lay, x, mods, p, h0_re, h0_im):
    T, G, P, Q = S5_T, S5_G, S5_P, S5_Q
    tabs = _s5_tables(p['a_re'], p['a_im'], p['log_dt'], p['b_re'], p['b_im'], p['c_re'], p['c_im'], p['d'])
    R = lay.T // T
    ug = _s5_pre(lay, x, mods)
    yi, sre, sim = _s5_in(ug, tabs)
    hprev = tuple(jnp.zeros((G, R, 2 * P), F32) for _ in range(4))
    ncp, ncs = lay.L // T, lay.LS // T
    nsb = math.gcd(lay.B, max(1, S5_SCAN_ROWS // ncp))
    zero = jnp.zeros((lay.B // nsb, G, nsb, 2 * P), F32)
    hprev, fre, fim = _s5_scan(sre, sim, tabs, zero, zero, hprev, 0, lay.B, ncp, nsb, 1)
    to_lanes = lambda s: jnp.transpose(s, (0, 2, 1, 3)).reshape(lay.NS, G, 1, 2 * P)
    hprev, _, _ = _s5_scan(sre, sim, tabs, to_lanes(h0_re), to_lanes(h0_im), hprev,
                           lay.TP // T, lay.NS, ncs, 1, max(1, ncs // S5_SCAN_ROWS))
    yg = _s5_out(yi, hprev, tabs)
    x = _s5_glu(lay, x, yg, mods, p['glu_w'])
    from_lanes = lambda s: jnp.transpose(s, (0, 2, 1, 3)).reshape(lay.B, G, 2, P).transpose(0, 2, 1, 3)
    return x, from_lanes(fre), from_lanes(fim)


def _hyena_filters(L, p):
    mm = functools.partial(jnp.matmul, precision=HIGHEST)
    f = jnp.linspace(1e-4, HY_BANDS - 1, HY_BANDS, dtype=F32)[None, :]
    max_decay = math.log(HY_TARGET) / HY_FAST_PCT
    min_decay = math.log(HY_TARGET) / HY_SLOW_PCT
    deltas = jnp.abs(jnp.linspace(min_decay, max_decay, D, dtype=F32))
    w3 = p['f_w3'].reshape(-1, 2, 2, D)

    def side(pos, s):
        t = (pos.astype(F32) / (L - 1))[:, None]
        w = 2.0 * math.pi * pos.astype(F32)[:, None] / L
        feats = jnp.concatenate([t, jnp.cos(f * w), -jnp.sin(f * w)], axis=-1)
        z = jnp.sin(p['f_freq'][0] * (mm(feats, p['f_w1']) + p['f_b1']))
        z = jnp.sin(p['f_freq'][1] * (mm(z, p['f_w2']) + p['f_b2']))
        win = jnp.exp(-t * deltas)
        return jnp.stack([mm(z, w3[:, o, s]) * win for o in range(2)])

    j = jnp.arange(L, dtype=jnp.int32)
    k_lo = side(j, 0)
    k_hi = side((L - j) % L, 1) * (j > 0).astype(F32)[None, :, None]
    norm = jnp.sum(jnp.abs(k_lo), axis=1, keepdims=True) + jnp.sum(jnp.abs(k_hi), axis=1, keepdims=True)
    k_lo, k_hi = k_lo / norm, k_hi / norm
    alt = (1.0 - 2.0 * (j % 2).astype(F32))[None, :, None]
    k_ny = jnp.sum(alt * (k_lo + k_hi), axis=1) / (2 * L)
    return k_lo, k_hi, k_ny


def _dft_tables(L):
    r = math.isqrt(L)
    assert r * r == L
    t = jnp.arange(L, dtype=jnp.int32)[None, :]
    a = jnp.arange(r, dtype=jnp.int32)[:, None]

    def unit(idx):
        ang = (idx % (2 * L)).astype(F32) * (math.pi / L)
        return jnp.cos(ang), jnp.sin(ang)

    c1, s1 = unit(a * r * t)
    c2, s2 = unit(a * t)
    c1, s1, c2, s2 = c1[:, None], s1[:, None], c2[None], s2[None]
    cos = (c1 * c2 - s1 * s2).reshape(L, L)
    sin = (s1 * c2 + c1 * s2).reshape(L, L)
    return cos.astype(BF16), sin.astype(BF16)


def _hy_spec_kernel(L, tr, c_ref, s_ref, klo_ref, khi_ref, p_ref, q_ref):
    r = pl.program_id(2)
    f = r * tr + lax.broadcasted_iota(jnp.int32, (tr, 1), 0)
    sgn = (1 - 2 * (f % 2)).astype(F32)
    scale = jnp.where(f == 0, 1.0, 2.0) * (1.0 / (2 * L))
    c, s = c_ref[...], s_ref[...]
    lo, hi = klo_ref[...], khi_ref[...]
    dot = functools.partial(jnp.dot, preferred_element_type=F32)
    p_ref[...] = scale * (dot(c, lo) + sgn * dot(c, hi))
    q_ref[...] = scale * (dot(s, lo) + sgn * dot(s, hi))


def _hy_spectrum(L, cos, sin, k_lo, k_hi):
    tr = min(L, 512)
    tc = 512
    kspec = pl.BlockSpec((None, L, tc), lambda o, j, r: (o, 0, j))
    tspec = pl.BlockSpec((tr, L), lambda o, j, r: (r, 0))
    ospec = pl.BlockSpec((None, tr, tc), lambda o, j, r: (o, r, j))
    return pl.pallas_call(
        functools.partial(_hy_spec_kernel, L, tr),
        out_shape=(jax.ShapeDtypeStruct((2, L, D), F32), jax.ShapeDtypeStruct((2, L, D), F32)),
        grid=(2, D // tc, L // tr),
        in_specs=[tspec, tspec, kspec, kspec],
        out_specs=(ospec, ospec),
        compiler_params=_cp("arbitrary", "arbitrary", "arbitrary"),
        name="hy_spectrum",
    )(cos, sin, k_lo.astype(BF16), k_hi.astype(BF16))


def _hy_core_kernel(L, tr, ngrp, tc, x1_ref, x2_ref, v_ref, sw1_ref, sw2_ref, swv_ref, sb1_ref, sb2_ref,
                    sbv_ref, c_ref, s_ref, p_ref, q_ref, kny_ref, skip_ref, *rest):
    o_ref, z0_sc, z1_sc, x2_sc, a_sc, b_sc, ny0_sc, ny1_sc = rest[-8:]
    ph = pl.program_id(2)
    r = pl.program_id(3)
    W = ngrp * tc
    tcv = min(L, 512)
    halo = 16

    def alt_sign(start, n):
        t = start + lax.broadcasted_iota(jnp.int32, (n, 1), 0)
        return (1 - 2 * (t % 2)).astype(F32)

    def conv3(src_ref, g, a, w_ref, b_ref):
        x = src_ref[g, pl.ds(a, tcv), :].astype(F32)
        row = lax.broadcasted_iota(jnp.int32, (tcv, 1), 0)
        up_at = pl.multiple_of(jnp.maximum(a - halo, 0), halo)
        dn_at = pl.multiple_of(jnp.minimum(a + tcv, L - halo), halo)
        up = src_ref[g, pl.ds(up_at, halo), :][halo - 1:halo, :].astype(F32)
        dn = src_ref[g, pl.ds(dn_at, halo), :][0:1, :].astype(F32)
        up = jnp.where(a > 0, up, 0.0)
        dn = jnp.where(a + tcv < L, dn, 0.0)
        prev = jnp.where(row == 0, up, pltpu.roll(x, 1, 0))
        nxt = jnp.where(row == tcv - 1, dn, pltpu.roll(x, tcv - 1, 0))
        return prev * w_ref[0:1, :] + x * w_ref[1:2, :] + nxt * w_ref[2:3, :] + b_ref[...]

    @pl.when(jnp.logical_and(ph == 0, r == 0))
    def _():
        ny0_sc[...] = jnp.zeros_like(ny0_sc)

        def conv_tile(ti, carry):
            a = pl.multiple_of(ti * tcv, tcv)
            rows_a = pl.ds(a, tcv)
            for g in range(ngrp):
                cols = slice(g * tc, (g + 1) * tc)
                z1_sc[rows_a, cols] = conv3(x1_ref, g, a, sw1_ref, sb1_ref).astype(BF16)
                x2_sc[rows_a, cols] = conv3(x2_ref, g, a, sw2_ref, sb2_ref).astype(BF16)
                z0_sc[rows_a, cols] = conv3(v_ref, g, a, swv_ref, sbv_ref).astype(BF16)
            ny0_sc[...] += jnp.sum(alt_sign(a, tcv) * z0_sc[rows_a, :].astype(F32), axis=0, keepdims=True)
            return carry

        lax.fori_loop(0, L // tcv, conv_tile, 0)

    start = pl.multiple_of(r * tr, tr)
    rows = pl.ds(start, tr)
    tile = lambda ref: jnp.concatenate([ref[...]] * ngrp, axis=1)
    dot = functools.partial(jnp.dot, preferred_element_type=F32)

    def forward(z_sc):
        z = z_sc[...]
        zre = dot(c_ref[...], z)
        zim = dot(s_ref[...], z)
        pw, qw = tile(p_ref), tile(q_ref)
        a_sc[rows, :] = (zre * pw - zim * qw).astype(BF16)
        b_sc[rows, :] = (zim * pw + zre * qw).astype(BF16)

    def inverse(order, z_sc, ny_sc):
        y = dot(c_ref[...], a_sc[...]) + dot(s_ref[...], b_sc[...])
        kny = jnp.concatenate([kny_ref[order:order + 1, :]] * ngrp, axis=1)
        skip = jnp.concatenate([skip_ref[order:order + 1, :]] * ngrp, axis=1)
        return y + alt_sign(start, tr) * (ny_sc[...] * kny) + skip * z_sc[rows, :].astype(F32)

    @pl.when(ph == 0)
    def _():
        forward(z0_sc)

    @pl.when(ph == 1)
    def _():
        z1 = (z1_sc[rows, :].astype(F32) * inverse(0, z0_sc, ny0_sc)).astype(BF16)
        z1_sc[rows, :] = z1

        @pl.when(r == 0)
        def _():
            ny1_sc[...] = jnp.zeros_like(ny1_sc)

        ny1_sc[...] += jnp.sum(alt_sign(start, tr) * z1.astype(F32), axis=0, keepdims=True)

    @pl.when(ph == 2)
    def _():
        forward(z1_sc)

    @pl.when(ph == 3)
    def _():
        out = x2_sc[rows, :].astype(F32) * inverse(1, z1_sc, ny1_sc)
        for g in range(ngrp):
            o_ref[g, rows, :] = out[:, g * tc:(g + 1) * tc].astype(o_ref.dtype)


def _hy_core(proj, short_w, short_b, skip, cos, sin, pq, k_ny, o_prev, row0, nseq, L, ngrp, tc):
    T = proj.shape[0]
    tr = min(L, 256)
    nrt = L // tr
    assert row0 % (L * ngrp) == 0 and nseq % ngrp == 0 and T % L == 0
    sb0 = row0 // (L * ngrp)
    nct = D // tc
    p3 = proj.reshape(T // L, L, 3 * D)
    p_arr, q_arr = pq

    def xspec(part):
        return pl.BlockSpec((ngrp, L, tc), lambda i, j, ph, r: (sb0 + i, 0, part * nct + j),
                            pipeline_mode=pl.Buffered(1))

    def wspec(part, rows_):
        return pl.BlockSpec((rows_, tc), lambda i, j, ph, r: (0, part * nct + j))

    def pq_map(i, j, ph, r):
        return (ph // 2, jnp.where(ph % 2 == 0, r, nrt - 1), j)

    tspec = pl.BlockSpec((tr, L), lambda i, j, ph, r: (r, 0))
    in_specs = [xspec(0), xspec(1), xspec(2), wspec(0, 3), wspec(1, 3), wspec(2, 3),
                wspec(0, 1), wspec(1, 1), wspec(2, 1), tspec, tspec,
                pl.BlockSpec((None, tr, tc), pq_map), pl.BlockSpec((None, tr, tc), pq_map),
                pl.BlockSpec((2, tc), lambda i, j, ph, r: (0, j)),
                pl.BlockSpec((2, tc), lambda i, j, ph, r: (0, j))]
    sb = short_b.reshape(1, 3 * D)
    args = [p3, p3, p3, short_w, short_w, short_w, sb, sb, sb, cos, sin, p_arr, q_arr, k_ny, skip]
    aliases = {}
    if o_prev is not None:
        in_specs.append(pl.BlockSpec(memory_space=pl.ANY))
        args.append(o_prev.reshape(T // L, L, D))
        aliases = {len(args) - 1: 0}
    W = ngrp * tc
    out = pl.pallas_call(
        functools.partial(_hy_core_kernel, L, tr, ngrp, tc),
        out_shape=jax.ShapeDtypeStruct((T // L, L, D), BF16),
        grid=(nseq // ngrp, nct, 4, nrt),
        in_specs=in_specs,
        out_specs=pl.BlockSpec((ngrp, L, tc), lambda i, j, ph, r: (sb0 + i, 0, j)),
        scratch_shapes=[pltpu.VMEM((L, W), BF16)] * 5 + [pltpu.VMEM((1, W), F32)] * 2,
        input_output_aliases=aliases,
        compiler_params=_cp("arbitrary", "arbitrary", "arbitrary", "arbitrary"),
        name="hy_core",
    )(*args)
    return out.reshape(T, D)


def _plain_out_kernel(x_ref, z_ref, g_ref, w_ref, b_ref, o_ref):
    acc = jnp.dot(z_ref[...], w_ref[...], preferred_element_type=F32) + b_ref[...]
    o_ref[...] = x_ref[...] + g_ref[...] * acc


def _plain_out(lay, x, mods, z, w, b):
    tm = lay.tile(1024)
    tn = 512
    kdim = z.shape[1]
    return pl.pallas_call(
        _plain_out_kernel,
        out_shape=jax.ShapeDtypeStruct((lay.T, D), F32),
        grid=(lay.T // tm, D // tn),
        in_specs=[pl.BlockSpec((tm, tn), lambda i, j: (i, j)),
                  pl.BlockSpec((tm, kdim), lambda i, j: (i, 0)),
                  pl.BlockSpec((None, 1, tn), lambda i, j: (lay.group(i * tm) * MOD_CHUNKS + 2, 0, j)),
                  pl.BlockSpec((kdim, tn), lambda i, j: (0, j)),
                  pl.BlockSpec((1, tn), lambda i, j: (0, j))],
        out_specs=pl.BlockSpec((tm, tn), lambda i, j: (i, j)),
        compiler_params=_cp("arbitrary", "arbitrary"),
        name="plain_out",
    )(x, z, mods, w, b.reshape(1, D))


def _hyena_layer(lay, x, mods, p):
    proj = _proj(lay, x, mods, 0, 1, p['w_in'].astype(BF16), p['b_in'].reshape(1, 3 * D), 768)
    z = jnp.zeros((lay.T, D), BF16)
    for row0, nseq, L, ngrp, tc in ((0, lay.B, lay.L, math.gcd(lay.B, 4), 256),
                                    (lay.TP, lay.NS, lay.LS, lay.NS, 256)):
        k_lo, k_hi, k_ny = _hyena_filters(L, p)
        cos, sin = _dft_tables(L)
        pq = _hy_spectrum(L, cos, sin, k_lo, k_hi)
        z = _hy_core(proj, p['short_w'], p['short_b'], p['skip'], cos, sin, pq, k_ny, z, row0, nseq, L, ngrp, tc)
    return _plain_out(lay, x, mods, z, p['w_out'].astype(BF16), p['b_out'])


_NT = (((1,), (1,)), ((), ()))
_TN = (((0,), (0,)), ((), ()))


def _tri(dr):
    t = lax.broadcasted_iota(jnp.int32, (CHUNK, CHUNK), 0)
    s = lax.broadcasted_iota(jnp.int32, (CHUNK, CHUNK), 1)
    return (s <= t) if dr == 0 else (s >= t)


def _chunk_cumsum(g, dr):
    n = g.shape[0]
    pos = lax.broadcasted_iota(jnp.int32, g.shape, 0) % CHUNK
    sh = 1
    while sh < CHUNK:
        if dr == 0:
            g = g + jnp.where(pos >= sh, pltpu.roll(g, sh, 0), 0.0)
        else:
            g = g + jnp.where(pos < CHUNK - sh, pltpu.roll(g, n - sh, 0), 0.0)
        sh *= 2
    return g


def _gla_kernel(cps, nseg, U, has_s0, want_final, *refs):
    q_ref, k_ref, v_ref, lr_ref, w2f_ref, w2b_ref, gb_ref = refs[:7]
    s0_ref = refs[7] if has_s0 else None
    qin_sc, kin_sc, kout_sc, dec_sc, st_sc, s_sc, s0t_sc = refs[-7:]
    outs = refs[-9:-7] if want_final else refs[-8:-7]
    o_ref = outs[0]
    sf_ref = outs[1] if want_final else None
    C = CHUNK
    nsc = cps // U
    nchunks = nseg * cps
    rows_total = nchunks * C
    w2 = (w2f_ref, w2b_ref)

    for dr in range(2):
        pre = jnp.dot(lr_ref[...], w2[dr][...], preferred_element_type=F32) + gb_ref[dr:dr + 1, :]
        g = (jnp.minimum(pre, 0.0) - jnp.log(1.0 + jnp.exp(-jnp.abs(pre)))) * (1.0 / GLA_TAU)
        b = _chunk_cumsum(g, dr)
        b3 = b.reshape(nchunks, C, GLA_DK)
        tot = b3[:, C - 1:C, :] if dr == 0 else b3[:, 0:1, :]
        dec_sc[...] = jnp.exp(tot).reshape(nchunks, GLA_DK)
        k = k_ref[...].astype(F32)
        qin_sc[...] = (q_ref[...].astype(F32) * (GLA_DK ** -0.5) * jnp.exp(b)).astype(BF16)
        kin_sc[...] = (k * jnp.exp(-b)).astype(BF16)
        kout_sc[...] = (k * jnp.exp(tot - b3).reshape(rows_total, GLA_DK)).astype(BF16)
        if has_s0:
            s0t_sc[...] = jnp.transpose(s0_ref[dr], (1, 0))
        tri = _tri(dr)

        def super_chunk(jj, carry, dr=dr, tri=tri):
            j = jj if dr == 0 else nseg * nsc - 1 - jj
            in_seg = j % nsc
            first = (in_seg == 0) if dr == 0 else (in_seg == nsc - 1)
            last = (in_seg == nsc - 1) if dr == 0 else (in_seg == 0)

            @pl.when(first)
            def _():
                s_sc[...] = s0t_sc[...] if has_s0 else jnp.zeros_like(s_sc)

            base = j * (U * C)
            for u in range(U):
                rows = pl.ds(pl.multiple_of(base + u * C, C), C)
                v = v_ref[rows, :]
                sc = lax.dot_general(qin_sc[rows, :], kin_sc[rows, :], _NT, preferred_element_type=F32)
                o = jnp.dot(jnp.where(tri, sc, 0.0).astype(BF16), v, preferred_element_type=F32)
                st_sc[u] = lax.dot_general(v, kout_sc[rows, :], _TN, preferred_element_type=F32)
                if dr == 0:
                    o_ref[rows, :] = o
                else:
                    o_ref[rows, :] += o
            s = s_sc[...]
            for u in (range(U) if dr == 0 else reversed(range(U))):
                kv = st_sc[u]
                st_sc[u] = s
                s = dec_sc[pl.ds(j * U + u, 1), :] * s + kv
            s_sc[...] = s
            for u in range(U):
                rows = pl.ds(pl.multiple_of(base + u * C, C), C)
                o_ref[rows, :] += lax.dot_general(qin_sc[rows, :], st_sc[u].astype(BF16), _NT,
                                                  preferred_element_type=F32)
            if want_final:
                @pl.when(last)
                def _():
                    sf_ref[j // nsc, dr] = jnp.transpose(s, (1, 0))
            return carry

        lax.fori_loop(0, nseg * nsc, super_chunk, 0)


def _gla_core(proj, w2f, w2b, gate_b, s0, o_prev, row0, nseq, seqlen, nseg, want_final):
    T = proj.shape[0]
    rows = nseg * seqlen
    cps = seqlen // CHUNK
    U = math.gcd(cps, 8)
    assert row0 % rows == 0 and seqlen % CHUNK == 0 and nseq % nseg == 0
    rb = row0 // rows
    hk = GLA_H * GLA_DK
    has_s0 = s0 is not None
    assert not has_s0 or nseg == 1
    in_specs = [pl.BlockSpec((rows, GLA_DK), lambda b, h: (rb + b, h)),
                pl.BlockSpec((rows, GLA_DK), lambda b, h: (rb + b, GLA_H + h)),
                pl.BlockSpec((rows, GLA_DV), lambda b, h: (rb + b, 2 * hk // GLA_DV + h)),
                pl.BlockSpec((rows, 128), lambda b, h: (rb + b, (2 * hk + 2 * GLA_H * GLA_DV) // 128)),
                pl.BlockSpec((128, GLA_DK), lambda b, h: (0, h)),
                pl.BlockSpec((128, GLA_DK), lambda b, h: (0, h)),
                pl.BlockSpec((2, GLA_DK), lambda b, h: (0, h))]
    args = [proj, proj, proj, proj, w2f, w2b, gate_b]
    if has_s0:
        in_specs.append(pl.BlockSpec((None, 2, None, GLA_DK, GLA_DV), lambda b, h: (b, 0, h, 0, 0)))
        args.append(s0)
    in_specs.append(pl.BlockSpec(memory_space=pl.ANY))
    args.append(o_prev)
    aliases = {len(args) - 1: 0}
    out_shape = [jax.ShapeDtypeStruct((T, GLA_H * GLA_DV), F32)]
    out_specs = [pl.BlockSpec((rows, GLA_DV), lambda b, h: (rb + b, h))]
    if want_final:
        out_shape.append(jax.ShapeDtypeStruct((nseq, 2, GLA_H, GLA_DK, GLA_DV), F32))
        out_specs.append(pl.BlockSpec((nseg, 2, None, GLA_DK, GLA_DV), lambda b, h: (b, 0, h, 0, 0)))
    outs = pl.pallas_call(
        functools.partial(_gla_kernel, cps, nseg, U, has_s0, want_final),
        out_shape=tuple(out_shape),
        grid=(nseq // nseg, GLA_H),
        in_specs=in_specs,
        out_specs=tuple(out_specs),
        scratch_shapes=[pltpu.VMEM((rows, GLA_DK), BF16)] * 3
        + [pltpu.VMEM((nseg * cps, GLA_DK), F32), pltpu.VMEM((U, GLA_DV, GLA_DK), F32),
           pltpu.VMEM((GLA_DV, GLA_DK), F32), pltpu.VMEM((GLA_DV, GLA_DK), F32)],
        input_output_aliases=aliases,
        compiler_params=_cp("arbitrary", "arbitrary"),
        name="gla",
    )(*args)
    return (outs[0], outs[1]) if want_final else (outs[0], None)


def _ret_kernel(cps, nseg, U, has_s0, want_final, rope, *refs):
    q_ref, k_ref, v_ref, dm_ref, qd_ref, kd_ref, cd_ref = refs[:7]
    nxt = 7
    if rope:
        cos_ref, sin_ref = refs[7:9]
        nxt = 9
    s0_ref = refs[nxt] if has_s0 else None
    qr_sc, kr_sc, qd_sc, kd_sc, st_sc, s_sc = refs[-6:]
    outs = refs[-8:-6] if want_final else refs[-7:-6]
    o_ref = outs[0]
    sf_ref = outs[1] if want_final else None
    C = RET_CHUNK
    nsc = cps // U
    R = U * C
    SB = 64

    def rot(x, rows):
        if not rope:
            return x
        half = x.shape[1] // 2
        swapped = jnp.concatenate([pltpu.roll(x[:, :half], half // 2, 1),
                                   pltpu.roll(x[:, half:], half // 2, 1)], axis=1)
        return x * cos_ref[rows, :] + swapped * sin_ref[rows, :]

    for dr in range(2):
        def super_chunk(jj, carry, dr=dr):
            j = jj if dr == 0 else nseg * nsc - 1 - jj
            in_seg = j % nsc
            first = (in_seg == 0) if dr == 0 else (in_seg == nsc - 1)
            last = (in_seg == nsc - 1) if dr == 0 else (in_seg == 0)

            @pl.when(first)
            def _():
                s_sc[...] = s0_ref[dr] if has_s0 else jnp.zeros_like(s_sc)

            base = pl.multiple_of(j * R, R)
            rows_r = pl.ds(base, R)
            q = rot(q_ref[rows_r, :].astype(F32), rows_r)
            k = rot(k_ref[rows_r, :].astype(F32), rows_r) * (RET_DK ** -0.5)
            qr_sc[...] = q.astype(BF16)
            kr_sc[...] = k.astype(BF16)
            qd_sc[...] = (q.reshape(U, C, RET_DK) * qd_ref[dr][None]).reshape(R, RET_DK).astype(BF16)
            kd_sc[...] = (k.reshape(U, C, RET_DK) * kd_ref[dr][None]).reshape(R, RET_DK).astype(BF16)
            for u in range(U):
                loc = pl.ds(u * C, C)
                rows = pl.ds(pl.multiple_of(base + u * C, C), C)
                v = v_ref[rows, :]
                sc = lax.dot_general(qr_sc[loc, :], kr_sc[loc, :], _NT, preferred_element_type=F32)
                o = jnp.dot((sc * dm_ref[dr]).astype(BF16), v, preferred_element_type=F32)
                st_sc[u] = lax.dot_general(kd_sc[loc, :], v, _TN, preferred_element_type=F32)
                if dr == 0:
                    o_ref[rows, :] = o
                else:
                    o_ref[rows, :] += o
            cd = cd_ref[dr]
            for r0 in range(0, RET_DK, SB):
                srows = pl.ds(r0, SB)
                s = s_sc[srows, :]
                for u in (range(U) if dr == 0 else reversed(range(U))):
                    kv = st_sc[u, srows, :]
                    st_sc[u, srows, :] = s
                    s = cd * s + kv
                s_sc[srows, :] = s
            for u in range(U):
                rows = pl.ds(pl.multiple_of(base + u * C, C), C)
                o_ref[rows, :] += jnp.dot(qd_sc[pl.ds(u * C, C), :], st_sc[u].astype(BF16),
                                          preferred_element_type=F32)
            if want_final:
                @pl.when(last)
                def _():
                    sf_ref[j // nsc, dr] = s_sc[...]
            return carry

        lax.fori_loop(0, nseg * nsc, super_chunk, 0)


def _ret_tables(log_decay):
    C = RET_CHUNK
    lg = log_decay.astype(F32)[:, :, None, None]
    t = jnp.arange(C, dtype=F32)[:, None]
    s = jnp.arange(C, dtype=F32)[None, :]
    lag = jnp.stack([t - s, s - t])[:, None]
    dmask = jnp.where(lag >= 0, jnp.exp(jnp.maximum(lag, 0.0) * lg), 0.0)
    tl = jnp.arange(C, dtype=F32)[None, None, :, None]
    qdec = jnp.concatenate([jnp.exp((tl + 1.0) * lg[0:1]), jnp.exp((C - tl) * lg[1:2])], axis=0)
    kdec = jnp.concatenate([jnp.exp((C - 1.0 - tl) * lg[0:1]), jnp.exp(tl * lg[1:2])], axis=0)
    cdec = jnp.exp(C * lg)
    return dmask, qdec, kdec, cdec


def _rope_tables(seqlen, dk):
    half = dk // 2
    nf = half // 2
    pos = jnp.arange(seqlen, dtype=jnp.int32)
    inv = ROPE_BASE ** (-jnp.arange(nf, dtype=F32) / nf)
    ang_r = (pos // GRID_W).astype(F32)[:, None] * inv[None, :]
    ang_c = (pos % GRID_W).astype(F32)[:, None] * inv[None, :]
    cos = jnp.concatenate([jnp.cos(ang_r)] * 2 + [jnp.cos(ang_c)] * 2, axis=1)
    sin = jnp.concatenate([-jnp.sin(ang_r), jnp.sin(ang_r), -jnp.sin(ang_c), jnp.sin(ang_c)], axis=1)
    return cos, sin


def _ret_core(proj, tabs, s0, o_prev, row0, nseq, seqlen, nseg, want_final, rope):
    T = proj.shape[0]
    rows = nseg * seqlen
    C = RET_CHUNK
    cps = seqlen // C
    U = math.gcd(cps, 4)
    assert row0 % rows == 0 and seqlen % C == 0 and nseq % nseg == 0
    rb = row0 // rows
    hk = RET_H * RET_DK
    has_s0 = s0 is not None
    assert not (has_s0 or rope) or nseg == 1
    tspec = lambda r, c: pl.BlockSpec((2, None, r, c), lambda b, h: (0, h, 0, 0))
    in_specs = [pl.BlockSpec((rows, RET_DK), lambda b, h: (rb + b, h)),
                pl.BlockSpec((rows, RET_DK), lambda b, h: (rb + b, RET_H + h)),
                pl.BlockSpec((rows, RET_DV), lambda b, h: (rb + b, 2 * hk // RET_DV + h)),
                tspec(C, C), tspec(C, 1), tspec(C, 1), tspec(1, 1)]
    args = [proj, proj, proj, *tabs]
    if rope:
        cos, sin = _rope_tables(seqlen, RET_DK)
        in_specs += [pl.BlockSpec((seqlen, RET_DK), lambda b, h: (0, 0), pipeline_mode=pl.Buffered(1))] * 2
        args += [cos, sin]
    if has_s0:
        in_specs.append(pl.BlockSpec((None, 2, None, RET_DK, RET_DV), lambda b, h: (b, 0, h, 0, 0)))
        args.append(s0)
    in_specs.append(pl.BlockSpec(memory_space=pl.ANY))
    args.append(o_prev)
    aliases = {len(args) - 1: 0}
    out_shape = [jax.ShapeDtypeStruct((T, RET_H * RET_DV), F32)]
    out_specs = [pl.BlockSpec((rows, RET_DV), lambda b, h: (rb + b, h))]
    if want_final:
        out_shape.append(jax.ShapeDtypeStruct((nseq, 2, RET_H, RET_DK, RET_DV), F32))
        out_specs.append(pl.BlockSpec((nseg, 2, None, RET_DK, RET_DV), lambda b, h: (b, 0, h, 0, 0)))
    outs = pl.pallas_call(
        functools.partial(_ret_kernel, cps, nseg, U, has_s0, want_final, rope),
        out_shape=tuple(out_shape),
        grid=(nseq // nseg, RET_H),
        in_specs=in_specs,
        out_specs=tuple(out_specs),
        scratch_shapes=[pltpu.VMEM((U * C, RET_DK), BF16)] * 4
        + [pltpu.VMEM((U, RET_DK, RET_DV), F32), pltpu.VMEM((RET_DK, RET_DV), F32)],
        input_output_aliases=aliases,
        compiler_params=_cp("arbitrary", "arbitrary"),
        name="ret",
    )(*args)
    return (outs[0], outs[1]) if want_final else (outs[0], None)


def _headnorm_out_kernel(nh, dv, center, x_ref, o_ref, gt_ref, ng_ref, g_ref, w_ref, out_ref, a_sc):
    @pl.when(pl.program_id(1) == 0)
    def _():
        for h in range(nh):
            cols = slice(h * dv, (h + 1) * dv)
            oh = o_ref[:, cols]
            if center:
                oh = oh - jnp.mean(oh, axis=-1, keepdims=True)
            oh = oh * lax.rsqrt(jnp.mean(oh * oh, axis=-1, keepdims=True) + RMS_EPS) * ng_ref[...]
            a_sc[:, cols] = (oh * _silu(gt_ref[:, cols].astype(F32))).astype(BF16)

    out_ref[...] = x_ref[...] + g_ref[...] * jnp.dot(a_sc[...], w_ref[...], preferred_element_type=F32)


def _headnorm_out(lay, x, mods, o, proj, gate_col, norm_g, w_out, nh, dv, center):
    tm = lay.tile(512)
    tn = 512
    kdim = nh * dv
    assert gate_col % kdim == 0
    return pl.pallas_call(
        functools.partial(_headnorm_out_kernel, nh, dv, center),
        out_shape=jax.ShapeDtypeStruct((lay.T, D), F32),
        grid=(lay.T // tm, D // tn),
        in_specs=[pl.BlockSpec((tm, tn), lambda i, j: (i, j)),
                  pl.BlockSpec((tm, kdim), lambda i, j: (i, 0)),
                  pl.BlockSpec((tm, kdim), lambda i, j: (i, gate_col // kdim)),
                  pl.BlockSpec((1, dv), lambda i, j: (0, 0)),
                  pl.BlockSpec((None, 1, tn), lambda i, j: (lay.group(i * tm) * MOD_CHUNKS + 2, 0, j)),
                  pl.BlockSpec((kdim, tn), lambda i, j: (0, j))],
        out_specs=pl.BlockSpec((tm, tn), lambda i, j: (i, j)),
        scratch_shapes=[pltpu.VMEM((tm, kdim), BF16)],
        compiler_params=_cp("arbitrary", "arbitrary"),
        name="headnorm_out",
    )(x, o, proj, norm_g.reshape(1, dv), mods, w_out)


def _gla_layer(lay, x, mods, p, s0):
    hk, hv = GLA_H * GLA_DK, GLA_H * GLA_DV
    w_all = jnp.concatenate([p['w_in'], p['gate_w1'][0], p['gate_w1'][1],
                             jnp.zeros((D, 128 - 2 * GLA_RANK), F32)], axis=1).astype(BF16)
    proj = _proj(lay, x, mods, 0, 1, w_all, jnp.zeros((1, w_all.shape[1]), F32), 640)
    pad = lambda w, lo: jnp.pad(w, ((lo, 128 - GLA_RANK - lo), (0, 0))).astype(BF16)
    w2f, w2b = pad(p['gate_w2'][0], 0), pad(p['gate_w2'][1], GLA_RANK)
    o = jnp.zeros((lay.T, hv), F32)
    o, s_fin = _gla_core(proj, w2f, w2b, p['gate_b'], None, o, 0, lay.B, lay.L, math.gcd(lay.B, 8), True)
    o, _ = _gla_core(proj, w2f, w2b, p['gate_b'], s0, o, lay.TP, lay.NS, lay.LS, 1, False)
    x = _headnorm_out(lay, x, mods, o, proj, 2 * hk + hv, p['norm_g'], p['w_out'].astype(BF16),
                      GLA_H, GLA_DV, False)
    return x, s_fin


def _ret_layer(lay, x, mods, p, s0):
    hk, hv = RET_H * RET_DK, RET_H * RET_DV
    proj = _proj(lay, x, mods, 0, 1, p['w_in'].astype(BF16), jnp.zeros((1, 2 * hk + 2 * hv), F32), 768)
    tabs = _ret_tables(p['log_decay'])
    o = jnp.zeros((lay.T, hv), F32)
    o, s_fin = _ret_core(proj, tabs, None, o, 0, lay.B, lay.L, math.gcd(lay.B, 8), True, False)
    o, _ = _ret_core(proj, tabs, s0, o, lay.TP, lay.NS, lay.LS, 1, False, True)
    x = _headnorm_out(lay, x, mods, o, proj, 2 * hk + hv, p['norm_g'], p['w_out'].astype(BF16),
                      RET_H, RET_DV, True)
    return x, s_fin


MOE_BM = 512
EXPERT_TF = 1792
ROUTER_LANES = 128
DMA_UNROLL = 8


def _router_kernel(x_ref, sh_ref, sc_ref, rw_ref, h_ref, idx_ref, gate_ref):
    h = _modulate(x_ref[...], sh_ref[...], sc_ref[...])
    h_ref[...] = h
    logits = jnp.dot(h, rw_ref[...], precision=HIGHEST, preferred_element_type=F32)
    lane = lax.broadcasted_iota(jnp.int32, logits.shape, 1)
    neg = jnp.float32(-jnp.inf)
    logits = jnp.where(lane < N_EXPERTS, logits, neg)
    m1 = jnp.max(logits, axis=-1, keepdims=True)
    i1 = jnp.min(jnp.where(logits == m1, lane, ROUTER_LANES), axis=-1, keepdims=True)
    rest = jnp.where(lane == i1, neg, logits)
    m2 = jnp.max(rest, axis=-1, keepdims=True)
    i2 = jnp.min(jnp.where(rest == m2, lane, ROUTER_LANES), axis=-1, keepdims=True)
    e2 = jnp.exp(m2 - m1)
    g1 = 1.0 / (1.0 + e2)
    idx_ref[:, 0:1] = i1
    idx_ref[:, 1:2] = i2
    gate_ref[:, 0:1] = g1
    gate_ref[:, 1:2] = e2 * g1


def _router(lay, x, mods, router_w):
    tm = lay.tile(512)
    rw = jnp.pad(router_w, ((0, 0), (0, ROUTER_LANES - N_EXPERTS)))
    return pl.pallas_call(
        _router_kernel,
        out_shape=(jax.ShapeDtypeStruct((lay.T, D), F32),
                   jax.ShapeDtypeStruct((lay.T, 2), jnp.int32),
                   jax.ShapeDtypeStruct((lay.T, 2), F32)),
        grid=(lay.T // tm,),
        in_specs=[pl.BlockSpec((tm, D), lambda i: (i, 0)),
                  _mod_spec(lay, tm, 3, 1), _mod_spec(lay, tm, 4, 1),
                  pl.BlockSpec((D, ROUTER_LANES), lambda i: (0, 0))],
        out_specs=(pl.BlockSpec((tm, D), lambda i: (i, 0)),
                   pl.BlockSpec((tm, 2), lambda i: (i, 0)),
                   pl.BlockSpec((tm, 2), lambda i: (i, 0))),
        compiler_params=_cp("arbitrary"),
        name="router",
    )(x, mods, mods, rw)


def _moe_plan(idx, bm):
    a = idx.size
    e = idx.reshape(a)
    onehot = (e[:, None] == jnp.arange(N_EXPERTS, dtype=jnp.int32)[None, :]).astype(jnp.int32)
    csum = jnp.cumsum(onehot, axis=0)
    counts = csum[-1]
    rank = jnp.sum((csum - onehot) * onehot, axis=-1)
    padded = (counts + bm - 1) // bm * bm
    pad_end = jnp.cumsum(padded)
    dest = ((pad_end - padded)[e] + rank).astype(jnp.int32)
    nb = -(-(a + N_EXPERTS * (bm - 1)) // bm)
    block_start = jnp.arange(nb, dtype=jnp.int32) * bm
    block_e = jnp.minimum(jnp.searchsorted(pad_end, block_start, side='right'), N_EXPERTS - 1).astype(jnp.int32)
    nvalid = (pad_end[-1] // bm).astype(jnp.int32).reshape(1)
    return dest, block_e, nvalid, nb


def _dispatch_kernel(tm, dest_ref, h_ref, xs_in, xs_hbm, sem):
    del xs_in
    i = pl.program_id(0)

    def row_copy(r, dst):
        return pltpu.make_async_copy(h_ref.at[pl.ds(r, 1)], xs_hbm.at[pl.ds(dst, 1)], sem)

    def issue(r, carry):
        a = 2 * (i * tm + r)
        row_copy(r, dest_ref[a]).start()
        row_copy(r, dest_ref[a + 1]).start()
        return carry

    def drain(r, carry):
        row_copy(r, 0).wait()
        row_copy(r, 0).wait()
        return carry

    lax.fori_loop(0, tm, issue, 0, unroll=DMA_UNROLL)
    lax.fori_loop(0, tm, drain, 0, unroll=DMA_UNROLL)


def _dispatch(lay, h, dest, nb, bm):
    tm = lay.tile(512)
    xs = jnp.zeros((nb * bm, D), F32)
    grid_spec = pltpu.PrefetchScalarGridSpec(
        num_scalar_prefetch=1,
        grid=(lay.T // tm,),
        in_specs=[pl.BlockSpec((tm, D), lambda i, d: (i, 0)), pl.BlockSpec(memory_space=pl.ANY)],
        out_specs=pl.BlockSpec(memory_space=pl.ANY),
        scratch_shapes=[pltpu.SemaphoreType.DMA(())],
    )
    return pl.pallas_call(
        functools.partial(_dispatch_kernel, tm),
        out_shape=jax.ShapeDtypeStruct(xs.shape, F32),
        grid_spec=grid_spec,
        input_output_aliases={2: 0},
        compiler_params=_cp("arbitrary"),
        name="dispatch",
    )(dest, h, xs)


def _experts_kernel(nf, be_ref, nv_ref, xs_ref, wa_ref, wb_ref, wo_ref, o_ref, xb_sc, acc_sc):
    i = pl.program_id(0)
    f = pl.program_id(1)
    valid = i < nv_ref[0]

    @pl.when(jnp.logical_and(valid, f == 0))
    def _():
        xb_sc[...] = xs_ref[...].astype(BF16)

    @pl.when(valid)
    def _():
        xb = xb_sc[...]
        a = jnp.dot(xb, wa_ref[...], preferred_element_type=F32)
        b = jnp.dot(xb, wb_ref[...], preferred_element_type=F32)
        h = (_silu(a) * b).astype(BF16)
        y = jnp.dot(h, wo_ref[...], preferred_element_type=F32)

        @pl.when(f == 0)
        def _():
            acc_sc[...] = y

        @pl.when(f > 0)
        def _():
            acc_sc[...] += y

    @pl.when(f == nf - 1)
    def _():
        o_ref[...] = jnp.where(valid, acc_sc[...], 0.0)


def _experts(xs, block_e, nvalid, nb, bm, w_in, w_out):
    tf = EXPERT_TF
    nf = EXPERT_DIM // tf

    def wmap(off):
        def imap(i, f, be, nv):
            fe = jnp.where(i < nv[0], f, nf - 1)
            return (be[i], 0, off + fe)
        return imap

    def womap(i, f, be, nv):
        fe = jnp.where(i < nv[0], f, nf - 1)
        return (be[i], fe, 0)

    grid_spec = pltpu.PrefetchScalarGridSpec(
        num_scalar_prefetch=2,
        grid=(nb, nf),
        in_specs=[pl.BlockSpec((bm, D), lambda i, f, be, nv: (jnp.minimum(i, nv[0] - 1), 0)),
                  pl.BlockSpec((None, D, tf), wmap(0)),
                  pl.BlockSpec((None, D, tf), wmap(nf)),
                  pl.BlockSpec((None, tf, D), womap)],
        out_specs=pl.BlockSpec((bm, D), lambda i, f, be, nv: (i, 0)),
        scratch_shapes=[pltpu.VMEM((bm, D), BF16), pltpu.VMEM((bm, D), F32)],
    )
    return pl.pallas_call(
        functools.partial(_experts_kernel, nf),
        out_shape=jax.ShapeDtypeStruct((nb * bm, D), F32),
        grid_spec=grid_spec,
        compiler_params=_cp("arbitrary", "arbitrary"),
        name="experts",
    )(block_e, nvalid, xs, w_in, w_in, w_out)


def _combine_kernel(tm, nt, final, dest_ref, x_ref, gate_ref, g_ref, fg_ref, ys_hbm, o_ref, y_sc, sem):
    i = pl.program_id(0)
    slot = i % 2

    def row_copy(s, k, r, src):
        return pltpu.make_async_copy(ys_hbm.at[pl.ds(src, 1)], y_sc.at[s, k, pl.ds(r, 1)], sem.at[s])

    def issue_tile(t, s):
        def issue(r, carry):
            a = 2 * (t * tm + r)
            row_copy(s, 0, r, dest_ref[a]).start()
            row_copy(s, 1, r, dest_ref[a + 1]).start()
            return carry
        lax.fori_loop(0, tm, issue, 0, unroll=DMA_UNROLL)

    @pl.when(i == 0)
    def _():
        issue_tile(0, 0)

    @pl.when(i + 1 < nt)
    def _():
        issue_tile(i + 1, 1 - slot)

    def drain(r, carry):
        row_copy(slot, 0, r, 0).wait()
        row_copy(slot, 1, r, 0).wait()
        return carry

    lax.fori_loop(0, tm, drain, 0, unroll=DMA_UNROLL)
    gate = gate_ref[...]
    out = x_ref[...] + g_ref[...] * (gate[:, 0:1] * y_sc[slot, 0] + gate[:, 1:2] * y_sc[slot, 1])
    if final:
        ms = jnp.mean(out * out, axis=-1, keepdims=True)
        out = out * lax.rsqrt(ms + RMS_EPS) * fg_ref[...]
    o_ref[...] = out


def _combine(lay, x, mods, gates, ys, dest, final_g):
    tm = lay.tile(256)
    nt = lay.T // tm
    final = final_g is not None
    fg = (final_g if final else jnp.ones((D,), F32)).reshape(1, D)
    grid_spec = pltpu.PrefetchScalarGridSpec(
        num_scalar_prefetch=1,
        grid=(nt,),
        in_specs=[pl.BlockSpec((tm, D), lambda i, d: (i, 0)),
                  pl.BlockSpec((tm, 2), lambda i, d: (i, 0)),
                  pl.BlockSpec((None, 1, D), lambda i, d: (lay.group(i * tm) * MOD_CHUNKS + 5, 0, 0)),
                  pl.BlockSpec((1, D), lambda i, d: (0, 0)),
                  pl.BlockSpec(memory_space=pl.ANY)],
        out_specs=pl.BlockSpec((tm, D), lambda i, d: (i, 0)),
        scratch_shapes=[pltpu.VMEM((2, 2, tm, D), F32), pltpu.SemaphoreType.DMA((2,))],
    )
    return pl.pallas_call(
        functools.partial(_combine_kernel, tm, nt, final),
        out_shape=jax.ShapeDtypeStruct((lay.T, D), F32),
        grid_spec=grid_spec,
        compiler_params=_cp("arbitrary"),
        name="combine",
    )(dest, x, gates, mods, fg, ys)


def _moe_layer(lay, x, mods, router_w, w_in, w_out, final_g=None, bm=MOE_BM):
    h, idx, gates = _router(lay, x, mods, router_w)
    dest, block_e, nvalid, nb = _moe_plan(idx, bm)
    xs = _dispatch(lay, h, dest, nb, bm)
    ys = _experts(xs, block_e, nvalid, nb, bm, w_in, w_out)
    return _combine(lay, x, mods, gates, ys, dest, final_g)


def kernel(x_prompt, x_sample, c, state_l0_s5_re, state_l0_s5_im, state_l2_gla, state_l3_ret, c_ctx, l0_mod_w, l0_mod_b, l0_s5_a_re, l0_s5_a_im, l0_s5_log_dt, l0_s5_b_re, l0_s5_b_im, l0_s5_c_re, l0_s5_c_im, l0_s5_d, l0_s5_glu_w, l0_ffn_w_in, l0_ffn_w_out, l1_mod_w, l1_mod_b, l1_hy_w_in, l1_hy_b_in, l1_hy_short_w, l1_hy_short_b, l1_hy_f_w1, l1_hy_f_b1, l1_hy_f_w2, l1_hy_f_b2, l1_hy_f_w3, l1_hy_f_freq, l1_hy_skip, l1_hy_w_out, l1_hy_b_out, l1_moe_router, l1_moe_w_in, l1_moe_w_out, l2_mod_w, l2_mod_b, l2_gla_w_in, l2_gla_gate_w1, l2_gla_gate_w2, l2_gla_gate_b, l2_gla_norm_g, l2_gla_w_out, l2_ffn_w_in, l2_ffn_w_out, l3_mod_w, l3_mod_b, l3_ret_w_in, l3_ret_log_decay, l3_ret_norm_g, l3_ret_w_out, l3_moe_router, l3_moe_w_in, l3_moe_w_out, final_norm_g):
    B, L, _ = x_prompt.shape
    NS, LS, _ = x_sample.shape
    lay = Layout(B, L, NS, LS)
    x = jnp.concatenate([x_prompt.reshape(B * L, D), x_sample.reshape(NS * LS, D)], axis=0)
    cond = jnp.concatenate([c_ctx[None], c, jnp.zeros((8 - 1 - NS, D), F32)], axis=0)
    mods0 = _mods(cond, l0_mod_w, l0_mod_b)
    p0 = dict(a_re=l0_s5_a_re, a_im=l0_s5_a_im, log_dt=l0_s5_log_dt, b_re=l0_s5_b_re, b_im=l0_s5_b_im,
              c_re=l0_s5_c_re, c_im=l0_s5_c_im, d=l0_s5_d, glu_w=l0_s5_glu_w.astype(BF16))
    x, s5_re, s5_im = _s5_layer(lay, x, mods0, p0, state_l0_s5_re, state_l0_s5_im)
    x = _ffn(lay, x, mods0, l0_ffn_w_in.astype(BF16), l0_ffn_w_out.astype(BF16))

    mods1 = _mods(cond, l1_mod_w, l1_mod_b)
    p1 = dict(w_in=l1_hy_w_in, b_in=l1_hy_b_in, short_w=l1_hy_short_w, short_b=l1_hy_short_b,
              f_w1=l1_hy_f_w1, f_b1=l1_hy_f_b1, f_w2=l1_hy_f_w2, f_b2=l1_hy_f_b2, f_w3=l1_hy_f_w3,
              f_freq=l1_hy_f_freq, skip=l1_hy_skip, w_out=l1_hy_w_out, b_out=l1_hy_b_out)
    x = _hyena_layer(lay, x, mods1, p1)
    x = _moe_layer(lay, x, mods1, l1_moe_router, l1_moe_w_in.astype(BF16), l1_moe_w_out.astype(BF16))

    mods2 = _mods(cond, l2_mod_w, l2_mod_b)
    p2 = dict(w_in=l2_gla_w_in, gate_w1=l2_gla_gate_w1, gate_w2=l2_gla_gate_w2, gate_b=l2_gla_gate_b,
              norm_g=l2_gla_norm_g, w_out=l2_gla_w_out)
    x, gla_state = _gla_layer(lay, x, mods2, p2, state_l2_gla)
    x = _ffn(lay, x, mods2, l2_ffn_w_in.astype(BF16), l2_ffn_w_out.astype(BF16))

    mods3 = _mods(cond, l3_mod_w, l3_mod_b)
    p3 = dict(w_in=l3_ret_w_in, log_decay=l3_ret_log_decay, norm_g=l3_ret_norm_g, w_out=l3_ret_w_out)
    x, ret_state = _ret_layer(lay, x, mods3, p3, state_l3_ret)
    y = _moe_layer(lay, x, mods3, l3_moe_router, l3_moe_w_in.astype(BF16), l3_moe_w_out.astype(BF16),
                   final_g=final_norm_g)
    return (y[:lay.TP].reshape(B, L, D), y[lay.TP:].reshape(NS, LS, D), s5_re, s5_im, gla_state, ret_state)
```

```python
import functools
import math

import jax
import jax.numpy as jnp
import numpy as np
from jax import lax
from jax.experimental import pallas as pl
from jax.experimental.pallas import tpu as pltpu

F32 = jnp.float32
BF16 = jnp.bfloat16
HIGHEST = lax.Precision.HIGHEST

D = 1024
RMS_EPS = 1e-6
MOD_CHUNKS = 6
GRID_W = 64

S5_Q = 16
S5_G = D // S5_Q
S5_P = 64
S5_T = 16
S5_SCAN_ROWS = 64

HY_BANDS = 16
HY_TARGET = 1e-2
HY_FAST_PCT = 0.3
HY_SLOW_PCT = 1.5

GLA_H, GLA_DK, GLA_DV = 4, 128, 256
GLA_RANK = 16
GLA_TAU = 16.0
RET_H, RET_DK, RET_DV = 4, 256, 512
CHUNK = 64
RET_CHUNK = 256
ROPE_BASE = 10000.0

FFN_DIM = 2816
N_EXPERTS = 8
EXPERT_DIM = 3584

VMEM_LIMIT_V7X = 56 * 1024 * 1024


def _cp(*sem):
    return pltpu.CompilerParams(dimension_semantics=sem, vmem_limit_bytes=VMEM_LIMIT_V7X)


def _silu(x):
    return x * jax.nn.sigmoid(x)


def _modulate(x, shift, scale):
    ms = jnp.mean(x * x, axis=-1, keepdims=True)
    return x * lax.rsqrt(ms + RMS_EPS) * (1.0 + scale) + shift


class Layout:
    def __init__(self, n_prompt, l_prompt, n_sample, l_sample):
        self.B, self.L, self.NS, self.LS = n_prompt, l_prompt, n_sample, l_sample
        self.TP = n_prompt * l_prompt
        self.T = self.TP + n_sample * l_sample

    def tile(self, want):
        t = math.gcd(math.gcd(self.TP, self.LS), want)
        assert t % 8 == 0
        return t

    def group(self, row):
        return jnp.where(row < self.TP, 0, 1 + (row - self.TP) // self.LS)


def _mod_spec(lay, tm, chunk, ngrid):
    def imap(*ids):
        return (lay.group(ids[0] * tm) * MOD_CHUNKS + chunk, 0, 0)
    del ngrid
    return pl.BlockSpec((None, 1, D), imap)


def _mods_kernel(c_ref, w_ref, b_ref, o_ref):
    o_ref[...] = jnp.dot(_silu(c_ref[...]), w_ref[...], precision=HIGHEST,
                         preferred_element_type=F32) + b_ref[...]


def _mods(cond, w, b):
    n = MOD_CHUNKS * D
    tn = 1536
    out = pl.pallas_call(
        _mods_kernel,
        out_shape=jax.ShapeDtypeStruct((8, n), F32),
        grid=(n // tn,),
        in_specs=[pl.BlockSpec((8, D), lambda j: (0, 0)),
                  pl.BlockSpec((D, tn), lambda j: (0, j)),
                  pl.BlockSpec((1, tn), lambda j: (0, j))],
        out_specs=pl.BlockSpec((8, tn), lambda j: (0, j)),
        compiler_params=_cp("arbitrary"),
        name="mods",
    )(cond, w, b.reshape(1, n))
    return out.reshape(8 * MOD_CHUNKS, 1, D)


def _modulate_kernel(x_ref, sh_ref, sc_ref, o_ref):
    o_ref[...] = _modulate(x_ref[...], sh_ref[...], sc_ref[...]).astype(o_ref.dtype)


def _modulate_call(lay, x, mods, c_shift, c_scale, dtype):
    tm = lay.tile(512)
    return pl.pallas_call(
        _modulate_kernel,
        out_shape=jax.ShapeDtypeStruct((lay.T, D), dtype),
        grid=(lay.T // tm,),
        in_specs=[pl.BlockSpec((tm, D), lambda i: (i, 0)),
                  _mod_spec(lay, tm, c_shift, 1), _mod_spec(lay, tm, c_scale, 1)],
        out_specs=pl.BlockSpec((tm, D), lambda i: (i, 0)),
        compiler_params=_cp("arbitrary"),
        name="modulate",
    )(x, mods, mods)


def _proj_kernel(x_ref, sh_ref, sc_ref, w_ref, b_ref, o_ref, u_sc):
    @pl.when(pl.program_id(1) == 0)
    def _():
        u_sc[...] = _modulate(x_ref[...], sh_ref[...], sc_ref[...]).astype(BF16)

    acc = jnp.dot(u_sc[...], w_ref[...], preferred_element_type=F32) + b_ref[...]
    o_ref[...] = acc.astype(o_ref.dtype)


def _proj(lay, x, mods, c_shift, c_scale, w, b, tn, out_dtype=BF16):
    tm = lay.tile(1024)
    n = w.shape[1]
    assert n % tn == 0
    return pl.pallas_call(
        _proj_kernel,
        out_shape=jax.ShapeDtypeStruct((lay.T, n), out_dtype),
        grid=(lay.T // tm, n // tn),
        in_specs=[pl.BlockSpec((tm, D), lambda i, j: (i, 0)),
                  _mod_spec(lay, tm, c_shift, 2), _mod_spec(lay, tm, c_scale, 2),
                  pl.BlockSpec((D, tn), lambda i, j: (0, j)),
                  pl.BlockSpec((1, tn), lambda i, j: (0, j))],
        out_specs=pl.BlockSpec((tm, tn), lambda i, j: (i, j)),
        scratch_shapes=[pltpu.VMEM((tm, D), BF16)],
        compiler_params=_cp("arbitrary", "arbitrary"),
        name="proj",
    )(x, mods, mods, w, b)


def _ffn_kernel(nf, x_ref, sh_ref, sc_ref, g_ref, wa_ref, wb_ref, wo_ref, o_ref, u_sc, acc_sc):
    f = pl.program_id(1)

    @pl.when(f == 0)
    def _():
        u_sc[...] = _modulate(x_ref[...], sh_ref[...], sc_ref[...]).astype(BF16)
        acc_sc[...] = jnp.zeros_like(acc_sc)

    u = u_sc[...]
    a = jnp.dot(u, wa_ref[...], preferred_element_type=F32)
    b = jnp.dot(u, wb_ref[...], preferred_element_type=F32)
    h = (_silu(a) * b).astype(BF16)
    acc_sc[...] += jnp.dot(h, wo_ref[...], preferred_element_type=F32)

    @pl.when(f == nf - 1)
    def _():
        o_ref[...] = x_ref[...] + g_ref[...] * acc_sc[...]


def _ffn(lay, x, mods, w_in, w_out):
    tm = lay.tile(512)
    tf = 1408
    nf = FFN_DIM // tf
    return pl.pallas_call(
        functools.partial(_ffn_kernel, nf),
        out_shape=jax.ShapeDtypeStruct((lay.T, D), F32),
        grid=(lay.T // tm, nf),
        in_specs=[pl.BlockSpec((tm, D), lambda i, f: (i, 0)),
                  _mod_spec(lay, tm, 3, 2), _mod_spec(lay, tm, 4, 2), _mod_spec(lay, tm, 5, 2),
                  pl.BlockSpec((D, tf), lambda i, f: (0, f)),
                  pl.BlockSpec((D, tf), lambda i, f: (0, nf + f)),
                  pl.BlockSpec((tf, D), lambda i, f: (f, 0))],
        out_specs=pl.BlockSpec((tm, D), lambda i, f: (i, 0)),
        scratch_shapes=[pltpu.VMEM((tm, D), BF16), pltpu.VMEM((tm, D), F32)],
        compiler_params=_cp("arbitrary", "arbitrary"),
        name="ffn",
    )(x, mods, mods, mods, w_in, w_in, w_out)


def _s5_tables(a_re, a_im, log_dt, b_re, b_im, c_re, c_im, d_skip):
    T, G, P, Q = S5_T, S5_G, S5_P, S5_Q
    a = lax.complex(a_re, a_im)
    adt = a * jnp.exp(log_dt)[..., None]
    lam = jnp.exp(adt)
    bb = ((lam - 1.0) / a)[..., None] * lax.complex(b_re, b_im)
    cm = lax.complex(c_re, c_im)
    steps = jnp.arange(T + 1, dtype=F32)
    pw = jnp.exp(steps[None, :, None, None] * adt[:, None])
    kern = jnp.real(jnp.einsum('dgqp,djgp,dgpr->djgqr', cm, pw[:, :T], bb))
    s_i = jnp.arange(T)[:, None]
    t_i = jnp.arange(T)[None, :]
    kf = kern[0][jnp.clip(t_i - s_i, 0, T - 1)] * (t_i >= s_i)[..., None, None, None]
    kb = kern[1][jnp.clip(s_i - t_i, 0, T - 1)] * (s_i >= t_i)[..., None, None, None]
    m = jnp.transpose(kf + kb, (2, 0, 4, 1, 3))
    eye = (jnp.eye(T)[:, None, :, None] * jnp.eye(Q)[None, :, None, :])
    m = m + eye[None] * d_skip.reshape(G, 1, 1, 1, Q)
    m = m.reshape(G, T * Q, T * Q)
    e_f = pw[0][T - 1 - jnp.arange(T)]
    e_b = pw[1][jnp.arange(T)]
    n_f = e_f[..., None] * bb[0][None]
    n_b = e_b[..., None] * bb[1][None]
    n_c = jnp.concatenate([n_f, n_b], axis=2)
    n_c = jnp.transpose(n_c, (1, 0, 3, 2)).reshape(G, T * Q, 2 * P)
    lam_t = jnp.concatenate([pw[0][T], pw[1][T]], axis=-1)
    w_f = cm[0][:, None] * jnp.transpose(pw[0][1:T + 1], (1, 0, 2))[:, :, None, :]
    w_b = cm[1][:, None] * jnp.transpose(pw[1][T - jnp.arange(T)], (1, 0, 2))[:, :, None, :]
    w_f = jnp.transpose(w_f, (0, 3, 1, 2)).reshape(G, P, T * Q)
    w_b = jnp.transpose(w_b, (0, 3, 1, 2)).reshape(G, P, T * Q)
    z = jnp.zeros_like(jnp.real(w_f))
    c_mats = dict(c_f_re=jnp.concatenate([jnp.real(w_f), z], axis=1),
                  c_f_im=jnp.concatenate([-jnp.imag(w_f), z], axis=1),
                  c_b_re=jnp.concatenate([z, jnp.real(w_b)], axis=1),
                  c_b_im=jnp.concatenate([z, -jnp.imag(w_b)], axis=1))
    return dict(m=m.astype(BF16), n_re=jnp.real(n_c).astype(BF16), n_im=jnp.imag(n_c).astype(BF16),
                l_re=jnp.real(lam_t), l_im=jnp.imag(lam_t), **{k: v.astype(BF16) for k, v in c_mats.items()})


def _s5_in_kernel(u_ref, m_ref, nre_ref, nim_ref, yi_ref, sre_ref, sim_ref):
    u = u_ref[...]
    yi_ref[...] = jnp.dot(u, m_ref[...], preferred_element_type=F32)
    sre_ref[...] = jnp.dot(u, nre_ref[...], preferred_element_type=F32).reshape(sre_ref.shape)
    sim_ref[...] = jnp.dot(u, nim_ref[...], preferred_element_type=F32).reshape(sim_ref.shape)


def _s5_in(ug, tabs):
    G, R, W = ug.shape
    P2 = 2 * S5_P
    RB = S5_SCAN_ROWS
    assert R % RB == 0
    gspec = lambda n: pl.BlockSpec((None, W, n), lambda g: (g, 0, 0))
    rspec = pl.BlockSpec((None, R, W), lambda g: (g, 0, 0))
    sspec = pl.BlockSpec((R // RB, RB, P2), lambda g: (0, g, 0))
    sshape = jax.ShapeDtypeStruct((R // RB, G * RB, P2), F32)
    return pl.pallas_call(
        _s5_in_kernel,
        out_shape=(jax.ShapeDtypeStruct((G, R, W), F32), sshape, sshape),
        grid=(G,),
        in_specs=[rspec, gspec(W), gspec(P2), gspec(P2)],
        out_specs=(rspec, sspec, sspec),
        compiler_params=_cp("arbitrary"),
        name="s5_in",
    )(ug, tabs['m'], tabs['n_re'], tabs['n_im'])


def _s5_scan_kernel(nsb, ncb, nblk, sref_ref, simf_ref, sreb_ref, simb_ref, lre_ref, lim_ref,
                    h0re_ref, h0im_ref, *rest):
    hfre_ref, hfim_ref, hbre_ref, hbim_ref, fre_ref, fim_ref, cre_sc, cim_sc = rest[4:]
    P = S5_P
    rows = sref_ref.shape[0] // ncb
    j = pl.program_id(1)
    fwd = lax.broadcasted_iota(jnp.int32, (1, 2 * P), 1) < P
    lre = lre_ref[...]
    lim = lim_ref[...]

    @pl.when(j == 0)
    def _():
        cre_sc[...] = h0re_ref[...]
        cim_sc[...] = h0im_ref[...]

    def at(k):
        return pl.ds(k, rows, stride=ncb)

    def body(k, carry):
        hre, him = carry
        kb = ncb - 1 - k
        hfre_ref[at(k), :] = hre
        hfim_ref[at(k), :] = him
        hbre_ref[at(kb), :] = hre
        hbim_ref[at(kb), :] = him
        sre = jnp.where(fwd, sref_ref[at(k), :], sreb_ref[at(kb), :])
        sim = jnp.where(fwd, simf_ref[at(k), :], simb_ref[at(kb), :])
        return (lre * hre - lim * him + sre, lre * him + lim * hre + sim)

    hre, him = lax.fori_loop(0, ncb, body, (cre_sc[...], cim_sc[...]), unroll=4)
    cre_sc[...] = hre
    cim_sc[...] = him

    @pl.when(j == nblk - 1)
    def _():
        fre_ref[...] = hre
        fim_ref[...] = him


def _s5_scan(sre, sim, tabs, h0re, h0im, hprev, row0, nseq, nc, nsb, nblk):
    _, grb, P2 = sre.shape
    rb = S5_SCAN_ROWS
    G = grb // rb
    assert nblk == 1 or nsb == 1
    ncb = nc // nblk
    assert nsb * ncb == rb and row0 % rb == 0 and nseq % nsb == 0 and nc % nblk == 0
    b0 = row0 // rb
    fspec = pl.BlockSpec((None, grb, P2), lambda i, j: (b0 + i * nblk + j, 0, 0))
    bspec = pl.BlockSpec((None, grb, P2), lambda i, j: (b0 + i * nblk + nblk - 1 - j, 0, 0))
    lspec = pl.BlockSpec((G * nsb, P2), lambda i, j: (0, 0))
    qspec = pl.BlockSpec((None, G * nsb, P2), lambda i, j: (i, 0, 0))
    anyspec = pl.BlockSpec(memory_space=pl.ANY)
    fin = jax.ShapeDtypeStruct((nseq // nsb, G * nsb, P2), F32)
    rep = lambda a: jnp.repeat(a, nsb, axis=0)
    flat = lambda a: a.reshape(nseq // nsb, G * nsb, P2)
    outs = pl.pallas_call(
        functools.partial(_s5_scan_kernel, nsb, ncb, nblk),
        out_shape=tuple(jax.ShapeDtypeStruct(h.shape, h.dtype) for h in hprev) + (fin, fin),
        grid=(nseq // nsb, nblk),
        in_specs=[fspec, fspec, bspec, bspec, lspec, lspec, qspec, qspec] + [anyspec] * 4,
        out_specs=(fspec, fspec, bspec, bspec, qspec, qspec),
        scratch_shapes=[pltpu.VMEM((G * nsb, P2), F32), pltpu.VMEM((G * nsb, P2), F32)],
        input_output_aliases={8: 0, 9: 1, 10: 2, 11: 3},
        compiler_params=_cp("arbitrary", "arbitrary"),
        name="s5_scan",
    )(sre, sim, sre, sim, rep(tabs['l_re']), rep(tabs['l_im']), flat(h0re), flat(h0im), *hprev)
    return outs[:4], outs[4].reshape(nseq // nsb, G, nsb, P2), outs[5].reshape(nseq // nsb, G, nsb, P2)


def _s5_out_kernel(yi_ref, hfre_ref, hfim_ref, hbre_ref, hbim_ref, cfre_ref, cfim_ref, cbre_ref, cbim_ref,
                   y_ref):
    y = yi_ref[...]
    for h_ref, c_ref in ((hfre_ref, cfre_ref), (hfim_ref, cfim_ref), (hbre_ref, cbre_ref), (hbim_ref, cbim_ref)):
        h = h_ref[...].reshape(y.shape[0], h_ref.shape[-1])
        y += jnp.dot(h.astype(BF16), c_ref[...], preferred_element_type=F32)
    y_ref[...] = y.astype(y_ref.dtype)


def _s5_out(yi, hprev, tabs):
    G, R, W = yi.shape
    P2 = 2 * S5_P
    RB = S5_SCAN_ROWS
    gspec = pl.BlockSpec((None, P2, W), lambda g: (g, 0, 0))
    hspec = pl.BlockSpec((R // RB, RB, P2), lambda g: (0, g, 0))
    rspec = pl.BlockSpec((None, R, W), lambda g: (g, 0, 0))
    return pl.pallas_call(
        _s5_out_kernel,
        out_shape=jax.ShapeDtypeStruct((G, R, W), F32),
        grid=(G,),
        in_specs=[rspec] + [hspec] * 4 + [gspec] * 4,
        out_specs=rspec,
        compiler_params=_cp("arbitrary"),
        name="s5_out",
    )(yi, *hprev, tabs['c_f_re'], tabs['c_f_im'], tabs['c_b_re'], tabs['c_b_im'])


LANES = 128
S5_GB = LANES // S5_Q


def _block_transpose(sets):
    blk = lax.broadcasted_iota(jnp.int32, sets[0][0].shape, 1) // S5_Q
    sets = [list(regs) for regs in sets]
    d = S5_GB // 2
    while d:
        keep = (blk & d) == 0
        for regs in sets:
            for i in range(S5_GB):
                if i & d:
                    continue
                a, b = regs[i], regs[i + d]
                regs[i] = jnp.where(keep, a, pltpu.roll(b, d * S5_Q, 1))
                regs[i + d] = jnp.where(keep, pltpu.roll(a, LANES - d * S5_Q, 1), b)
        d //= 2
    return sets


def _s5_pre_kernel(tm, x_ref, sh_ref, sc_ref, ug_ref, u_sc):
    u = _modulate(x_ref[...], sh_ref[...], sc_ref[...])
    for j in range(D // LANES):
        u_sc[j] = u[:, j * LANES:(j + 1) * LANES]
    rows16 = 16
    nh = S5_T // S5_GB
    for c in range(tm // (S5_T * rows16)):
        base = c * S5_T * rows16
        for j in range(D // LANES):
            sets = [[u_sc[j, pl.ds(base + h * S5_GB + s, rows16, stride=S5_T), :] for s in range(S5_GB)]
                    for h in range(nh)]
            for h, regs in enumerate(_block_transpose(sets)):
                for gl, t in enumerate(regs):
                    ug_ref[j * S5_GB + gl, c * rows16:(c + 1) * rows16, h * LANES:(h + 1) * LANES] = t.astype(BF16)


def _s5_pre(lay, x, mods):
    tm = lay.tile(512)
    assert tm % (S5_T * 16) == 0
    return pl.pallas_call(
        functools.partial(_s5_pre_kernel, tm),
        out_shape=jax.ShapeDtypeStruct((S5_G, lay.T // S5_T, S5_T * S5_Q), BF16),
        grid=(lay.T // tm,),
        in_specs=[pl.BlockSpec((tm, D), lambda i: (i, 0)), _mod_spec(lay, tm, 0, 1), _mod_spec(lay, tm, 1, 1)],
        out_specs=pl.BlockSpec((S5_G, tm // S5_T, S5_T * S5_Q), lambda i: (0, i, 0)),
        scratch_shapes=[pltpu.VMEM((D // LANES, tm, LANES), F32)],
        compiler_params=_cp("arbitrary"),
        name="s5_pre",
    )(x, mods, mods)


def _s5_glu_kernel(tm, x_ref, yg_ref, g_ref, wv_ref, wg_ref, o_ref, a_sc, y_sc):
    @pl.when(pl.program_id(1) == 0)
    def _():
        def sub_tile(c, carry):
            crow = pl.ds(pl.multiple_of(c * 8, 8), 8)
            base = c * (S5_T * 8)
            nh = S5_T // S5_GB
            for j2 in range(0, D // LANES, 2):
                keys = [(j, h) for j in (j2, j2 + 1) for h in range(nh)]
                sets = [[yg_ref[j * S5_GB + gl, crow, h * LANES:(h + 1) * LANES] for gl in range(S5_GB)]
                        for j, h in keys]
                for (j, h), regs in zip(keys, _block_transpose(sets)):
                    for s, t in enumerate(regs):
                        y_sc[j, pl.ds(base + h * S5_GB + s, 8, stride=S5_T), :] = t
            return carry

        lax.fori_loop(0, tm // (S5_T * 8), sub_tile, 0)
        for j in range(D // LANES):
            a_sc[:, j * LANES:(j + 1) * LANES] = jax.nn.gelu(y_sc[j]).astype(BF16)

    a = a_sc[...]
    val = jnp.dot(a, wv_ref[...], preferred_element_type=F32)
    gate = jnp.dot(a, wg_ref[...], preferred_element_type=F32)
    o_ref[...] = x_ref[...] + g_ref[...] * (val * jax.nn.sigmoid(gate))


def _s5_glu(lay, x, yg, mods, glu_w):
    tm = lay.tile(1024)
    assert tm % (S5_T * 8) == 0
    tn = 512
    nn = D // tn
    return pl.pallas_call(
        functools.partial(_s5_glu_kernel, tm),
        out_shape=jax.ShapeDtypeStruct((lay.T, D), F32),
        grid=(lay.T // tm, nn),
        in_specs=[pl.BlockSpec((tm, tn), lambda i, j: (i, j)),
                  pl.BlockSpec((S5_G, tm // S5_T, S5_T * S5_Q), lambda i, j: (0, i, 0)),
                  pl.BlockSpec((None, 1, tn), lambda i, j: (lay.group(i * tm) * MOD_CHUNKS + 2, 0, j)),
                  pl.BlockSpec((D, tn), lambda i, j: (0, j)),
                  pl.BlockSpec((D, tn), lambda i, j: (0, nn + j))],
        out_specs=pl.BlockSpec((tm, tn), lambda i, j: (i, j)),
        scratch_shapes=[pltpu.VMEM((tm, D), BF16), pltpu.VMEM((D // LANES, tm, LANES), F32)],
        compiler_params=_cp("arbitrary", "arbitrary"),
        name="s5_glu",
    )(x, yg, mods, glu_w, glu_w)


def _s5_layer(lay, x, mods, p, h0_re, h0_im):
    T, G, P, Q = S5_T, S5_G, S5_P, S5_Q
    tabs = _s5_tables(p['a_re'], p['a_im'], p['log_dt'], p['b_re'], p['b_im'], p['c_re'], p['c_im'], p['d'])
    R = lay.T // T
    ug = _s5_pre(lay, x, mods)
    yi, sre, sim = _s5_in(ug, tabs)
    hprev = tuple(jnp.zeros(sre.shape, F32) for _ in range(4))
    ncp, ncs = lay.L // T, lay.LS // T
    nsb = S5_SCAN_ROWS // ncp
    zero = jnp.zeros((lay.B // nsb, G, nsb, 2 * P), F32)
    hprev, fre, fim = _s5_scan(sre, sim, tabs, zero, zero, hprev, 0, lay.B, ncp, nsb, 1)
    to_lanes = lambda s: jnp.transpose(s, (0, 2, 1, 3)).reshape(lay.NS, G, 1, 2 * P)
    hprev, _, _ = _s5_scan(sre, sim, tabs, to_lanes(h0_re), to_lanes(h0_im), hprev,
                           lay.TP // T, lay.NS, ncs, 1, max(1, ncs // S5_SCAN_ROWS))
    yg = _s5_out(yi, hprev, tabs)
    x = _s5_glu(lay, x, yg, mods, p['glu_w'])
    from_lanes = lambda s: jnp.transpose(s, (0, 2, 1, 3)).reshape(lay.B, G, 2, P).transpose(0, 2, 1, 3)
    return x, from_lanes(fre), from_lanes(fim)


def _hyena_filters(L, p):
    mm = functools.partial(jnp.matmul, precision=HIGHEST)
    f = jnp.linspace(1e-4, HY_BANDS - 1, HY_BANDS, dtype=F32)[None, :]
    max_decay = math.log(HY_TARGET) / HY_FAST_PCT
    min_decay = math.log(HY_TARGET) / HY_SLOW_PCT
    deltas = jnp.abs(jnp.linspace(min_decay, max_decay, D, dtype=F32))
    w3 = p['f_w3'].reshape(-1, 2, 2, D)

    def side(pos, s):
        t = (pos.astype(F32) / (L - 1))[:, None]
        w = 2.0 * math.pi * pos.astype(F32)[:, None] / L
        feats = jnp.concatenate([t, jnp.cos(f * w), -jnp.sin(f * w)], axis=-1)
        z = jnp.sin(p['f_freq'][0] * (mm(feats, p['f_w1']) + p['f_b1']))
        z = jnp.sin(p['f_freq'][1] * (mm(z, p['f_w2']) + p['f_b2']))
        win = jnp.exp(-t * deltas)
        return jnp.stack([mm(z, w3[:, o, s]) * win for o in range(2)])

    j = jnp.arange(L, dtype=jnp.int32)
    k_lo = side(j, 0)
    k_hi = side((L - j) % L, 1) * (j > 0).astype(F32)[None, :, None]
    norm = jnp.sum(jnp.abs(k_lo), axis=1, keepdims=True) + jnp.sum(jnp.abs(k_hi), axis=1, keepdims=True)
    k_lo, k_hi = k_lo / norm, k_hi / norm
    alt = (1.0 - 2.0 * (j % 2).astype(F32))[None, :, None]
    k_ny = jnp.sum(alt * (k_lo + k_hi), axis=1) / (2 * L)
    return k_lo, k_hi, k_ny


def _dft_tables(L):
    r = math.isqrt(L)
    assert r * r == L
    t = jnp.arange(L, dtype=jnp.int32)[None, :]
    a = jnp.arange(r, dtype=jnp.int32)[:, None]

    def unit(idx):
        ang = (idx % (2 * L)).astype(F32) * (math.pi / L)
        return jnp.cos(ang), jnp.sin(ang)

    c1, s1 = unit(a * r * t)
    c2, s2 = unit(a * t)
    c1, s1, c2, s2 = c1[:, None], s1[:, None], c2[None], s2[None]
    cos = (c1 * c2 - s1 * s2).reshape(L, L)
    sin = (s1 * c2 + c1 * s2).reshape(L, L)
    return cos.astype(BF16), sin.astype(BF16)


def _hy_spec_kernel(L, tr, c_ref, s_ref, klo_ref, khi_ref, p_ref, q_ref):
    r = pl.program_id(2)
    f = r * tr + lax.broadcasted_iota(jnp.int32, (tr, 1), 0)
    sgn = (1 - 2 * (f % 2)).astype(F32)
    scale = jnp.where(f == 0, 1.0, 2.0) * (1.0 / (2 * L))
    c, s = c_ref[...], s_ref[...]
    lo, hi = klo_ref[...], khi_ref[...]
    dot = functools.partial(jnp.dot, preferred_element_type=F32)
    p_ref[...] = scale * (dot(c, lo) + sgn * dot(c, hi))
    q_ref[...] = scale * (dot(s, lo) + sgn * dot(s, hi))


def _hy_spectrum(L, cos, sin, k_lo, k_hi):
    tr = min(L, 512)
    tc = 512
    kspec = pl.BlockSpec((None, L, tc), lambda o, j, r: (o, 0, j))
    tspec = pl.BlockSpec((tr, L), lambda o, j, r: (r, 0))
    ospec = pl.BlockSpec((None, tr, tc), lambda o, j, r: (o, r, j))
    return pl.pallas_call(
        functools.partial(_hy_spec_kernel, L, tr),
        out_shape=(jax.ShapeDtypeStruct((2, L, D), F32), jax.ShapeDtypeStruct((2, L, D), F32)),
        grid=(2, D // tc, L // tr),
        in_specs=[tspec, tspec, kspec, kspec],
        out_specs=(ospec, ospec),
        compiler_params=_cp("arbitrary", "arbitrary", "arbitrary"),
        name="hy_spectrum",
    )(cos, sin, k_lo.astype(BF16), k_hi.astype(BF16))


def _hy_core_kernel(L, tr, ngrp, tc, x1_ref, x2_ref, v_ref, sw1_ref, sw2_ref, swv_ref, sb1_ref, sb2_ref,
                    sbv_ref, c_ref, s_ref, p_ref, q_ref, kny_ref, skip_ref, *rest):
    o_ref, z0_sc, z1_sc, x2_sc, a_sc, b_sc, ny0_sc, ny1_sc = rest[-8:]
    ph = pl.program_id(2)
    r = pl.program_id(3)
    W = ngrp * tc
    tcv = min(L, 512)
    halo = 16

    def alt_sign(start, n):
        t = start + lax.broadcasted_iota(jnp.int32, (n, 1), 0)
        return (1 - 2 * (t % 2)).astype(F32)

    def conv3(src_ref, g, a, w_ref, b_ref):
        x = src_ref[g, pl.ds(a, tcv), :].astype(F32)
        row = lax.broadcasted_iota(jnp.int32, (tcv, 1), 0)
        up_at = pl.multiple_of(jnp.maximum(a - halo, 0), halo)
        dn_at = pl.multiple_of(jnp.minimum(a + tcv, L - halo), halo)
        up = src_ref[g, pl.ds(up_at, halo), :][halo - 1:halo, :].astype(F32)
        dn = src_ref[g, pl.ds(dn_at, halo), :][0:1, :].astype(F32)
        up = jnp.where(a > 0, up, 0.0)
        dn = jnp.where(a + tcv < L, dn, 0.0)
        prev = jnp.where(row == 0, up, pltpu.roll(x, 1, 0))
        nxt = jnp.where(row == tcv - 1, dn, pltpu.roll(x, tcv - 1, 0))
        return prev * w_ref[0:1, :] + x * w_ref[1:2, :] + nxt * w_ref[2:3, :] + b_ref[...]

    @pl.when(jnp.logical_and(ph == 0, r == 0))
    def _():
        ny0_sc[...] = jnp.zeros_like(ny0_sc)

        def conv_tile(ti, carry):
            a = pl.multiple_of(ti * tcv, tcv)
            rows_a = pl.ds(a, tcv)
            for g in range(ngrp):
                cols = slice(g * tc, (g + 1) * tc)
                z1_sc[rows_a, cols] = conv3(x1_ref, g, a, sw1_ref, sb1_ref).astype(BF16)
                x2_sc[rows_a, cols] = conv3(x2_ref, g, a, sw2_ref, sb2_ref).astype(BF16)
                z0_sc[rows_a, cols] = conv3(v_ref, g, a, swv_ref, sbv_ref).astype(BF16)
            ny0_sc[...] += jnp.sum(alt_sign(a, tcv) * z0_sc[rows_a, :].astype(F32), axis=0, keepdims=True)
            return carry

        lax.fori_loop(0, L // tcv, conv_tile, 0)

    start = pl.multiple_of(r * tr, tr)
    rows = pl.ds(start, tr)
    tile = lambda ref: jnp.concatenate([ref[...]] * ngrp, axis=1)
    dot = functools.partial(jnp.dot, preferred_element_type=F32)

    def forward(z_sc):
        z = z_sc[...]
        zre = dot(c_ref[...], z)
        zim = dot(s_ref[...], z)
        pw, qw = tile(p_ref), tile(q_ref)
        a_sc[rows, :] = (zre * pw - zim * qw).astype(BF16)
        b_sc[rows, :] = (zim * pw + zre * qw).astype(BF16)

    def inverse(order, z_sc, ny_sc):
        y = dot(c_ref[...], a_sc[...]) + dot(s_ref[...], b_sc[...])
        kny = jnp.concatenate([kny_ref[order:order + 1, :]] * ngrp, axis=1)
        skip = jnp.concatenate([skip_ref[order:order + 1, :]] * ngrp, axis=1)
        return y + alt_sign(start, tr) * (ny_sc[...] * kny) + skip * z_sc[rows, :].astype(F32)

    @pl.when(ph == 0)
    def _():
        forward(z0_sc)

    @pl.when(ph == 1)
    def _():
        z1 = (z1_sc[rows, :].astype(F32) * inverse(0, z0_sc, ny0_sc)).astype(BF16)
        z1_sc[rows, :] = z1

        @pl.when(r == 0)
        def _():
            ny1_sc[...] = jnp.zeros_like(ny1_sc)

        ny1_sc[...] += jnp.sum(alt_sign(start, tr) * z1.astype(F32), axis=0, keepdims=True)

    @pl.when(ph == 2)
    def _():
        forward(z1_sc)

    @pl.when(ph == 3)
    def _():
        out = x2_sc[rows, :].astype(F32) * inverse(1, z1_sc, ny1_sc)
        for g in range(ngrp):
            o_ref[g, rows, :] = out[:, g * tc:(g + 1) * tc].astype(o_ref.dtype)


def _hy_core(proj, short_w, short_b, skip, cos, sin, pq, k_ny, o_prev, row0, nseq, L, ngrp, tc):
    T = proj.shape[0]
    tr = min(L, 256)
    nrt = L // tr
    assert row0 % (L * ngrp) == 0 and nseq % ngrp == 0 and T % L == 0
    sb0 = row0 // (L * ngrp)
    nct = D // tc
    p3 = proj.reshape(T // L, L, 3 * D)
    p_arr, q_arr = pq

    def xspec(part):
        return pl.BlockSpec((ngrp, L, tc), lambda i, j, ph, r: (sb0 + i, 0, part * nct + j),
                            pipeline_mode=pl.Buffered(1))

    def wspec(part, rows_):
        return pl.BlockSpec((rows_, tc), lambda i, j, ph, r: (0, part * nct + j))

    def pq_map(i, j, ph, r):
        return (ph // 2, jnp.where(ph % 2 == 0, r, nrt - 1), j)

    tspec = pl.BlockSpec((tr, L), lambda i, j, ph, r: (r, 0))
    in_specs = [xspec(0), xspec(1), xspec(2), wspec(0, 3), wspec(1, 3), wspec(2, 3),
                wspec(0, 1), wspec(1, 1), wspec(2, 1), tspec, tspec,
                pl.BlockSpec((None, tr, tc), pq_map), pl.BlockSpec((None, tr, tc), pq_map),
                pl.BlockSpec((2, tc), lambda i, j, ph, r: (0, j)),
                pl.BlockSpec((2, tc), lambda i, j, ph, r: (0, j))]
    sb = short_b.reshape(1, 3 * D)
    args = [p3, p3, p3, short_w, short_w, short_w, sb, sb, sb, cos, sin, p_arr, q_arr, k_ny, skip]
    aliases = {}
    if o_prev is not None:
        in_specs.append(pl.BlockSpec(memory_space=pl.ANY))
        args.append(o_prev.reshape(T // L, L, D))
        aliases = {len(args) - 1: 0}
    W = ngrp * tc
    out = pl.pallas_call(
        functools.partial(_hy_core_kernel, L, tr, ngrp, tc),
        out_shape=jax.ShapeDtypeStruct((T // L, L, D), BF16),
        grid=(nseq // ngrp, nct, 4, nrt),
        in_specs=in_specs,
        out_specs=pl.BlockSpec((ngrp, L, tc), lambda i, j, ph, r: (sb0 + i, 0, j)),
        scratch_shapes=[pltpu.VMEM((L, W), BF16)] * 5 + [pltpu.VMEM((1, W), F32)] * 2,
        input_output_aliases=aliases,
        compiler_params=_cp("arbitrary", "arbitrary", "arbitrary", "arbitrary"),
        name="hy_core",
    )(*args)
    return out.reshape(T, D)


def _plain_out_kernel(x_ref, z_ref, g_ref, w_ref, b_ref, o_ref):
    acc = jnp.dot(z_ref[...], w_ref[...], preferred_element_type=F32) + b_ref[...]
    o_ref[...] = x_ref[...] + g_ref[...] * acc


def _plain_out(lay, x, mods, z, w, b):
    tm = lay.tile(1024)
    tn = 512
    kdim = z.shape[1]
    return pl.pallas_call(
        _plain_out_kernel,
        out_shape=jax.ShapeDtypeStruct((lay.T, D), F32),
        grid=(lay.T // tm, D // tn),
        in_specs=[pl.BlockSpec((tm, tn), lambda i, j: (i, j)),
                  pl.BlockSpec((tm, kdim), lambda i, j: (i, 0)),
                  pl.BlockSpec((None, 1, tn), lambda i, j: (lay.group(i * tm) * MOD_CHUNKS + 2, 0, j)),
                  pl.BlockSpec((kdim, tn), lambda i, j: (0, j)),
                  pl.BlockSpec((1, tn), lambda i, j: (0, j))],
        out_specs=pl.BlockSpec((tm, tn), lambda i, j: (i, j)),
        compiler_params=_cp("arbitrary", "arbitrary"),
        name="plain_out",
    )(x, z, mods, w, b.reshape(1, D))


def _hyena_layer(lay, x, mods, p):
    proj = _proj(lay, x, mods, 0, 1, p['w_in'].astype(BF16), p['b_in'].reshape(1, 3 * D), 768)
    z = jnp.zeros((lay.T, D), BF16)
    for row0, nseq, L, ngrp, tc in ((0, lay.B, lay.L, math.gcd(lay.B, 4), 256),
                                    (lay.TP, lay.NS, lay.LS, lay.NS, 256)):
        k_lo, k_hi, k_ny = _hyena_filters(L, p)
        cos, sin = _dft_tables(L)
        pq = _hy_spectrum(L, cos, sin, k_lo, k_hi)
        z = _hy_core(proj, p['short_w'], p['short_b'], p['skip'], cos, sin, pq, k_ny, z, row0, nseq, L, ngrp, tc)
    return _plain_out(lay, x, mods, z, p['w_out'].astype(BF16), p['b_out'])


_NT = (((1,), (1,)), ((), ()))
_TN = (((0,), (0,)), ((), ()))


def _tri(dr):
    t = lax.broadcasted_iota(jnp.int32, (CHUNK, CHUNK), 0)
    s = lax.broadcasted_iota(jnp.int32, (CHUNK, CHUNK), 1)
    return (s <= t) if dr == 0 else (s >= t)


def _chunk_cumsum(g, dr):
    n = g.shape[0]
    pos = lax.broadcasted_iota(jnp.int32, g.shape, 0) % CHUNK
    sh = 1
    while sh < CHUNK:
        if dr == 0:
            g = g + jnp.where(pos >= sh, pltpu.roll(g, sh, 0), 0.0)
        else:
            g = g + jnp.where(pos < CHUNK - sh, pltpu.roll(g, n - sh, 0), 0.0)
        sh *= 2
    return g


def _gla_kernel(cps, nseg, U, has_s0, want_final, *refs):
    q_ref, k_ref, v_ref, lr_ref, w2f_ref, w2b_ref, gb_ref = refs[:7]
    s0_ref = refs[7] if has_s0 else None
    qin_sc, kin_sc, kout_sc, dec_sc, st_sc, s_sc, s0t_sc = refs[-7:]
    outs = refs[-9:-7] if want_final else refs[-8:-7]
    o_ref = outs[0]
    sf_ref = outs[1] if want_final else None
    C = CHUNK
    nsc = cps // U
    nchunks = nseg * cps
    rows_total = nchunks * C
    w2 = (w2f_ref, w2b_ref)

    for dr in range(2):
        pre = jnp.dot(lr_ref[...], w2[dr][...], preferred_element_type=F32) + gb_ref[dr:dr + 1, :]
        g = (jnp.minimum(pre, 0.0) - jnp.log(1.0 + jnp.exp(-jnp.abs(pre)))) * (1.0 / GLA_TAU)
        b = _chunk_cumsum(g, dr)
        b3 = b.reshape(nchunks, C, GLA_DK)
        tot = b3[:, C - 1:C, :] if dr == 0 else b3[:, 0:1, :]
        dec_sc[...] = jnp.exp(tot).reshape(nchunks, GLA_DK)
        k = k_ref[...].astype(F32)
        qin_sc[...] = (q_ref[...].astype(F32) * (GLA_DK ** -0.5) * jnp.exp(b)).astype(BF16)
        kin_sc[...] = (k * jnp.exp(-b)).astype(BF16)
        kout_sc[...] = (k * jnp.exp(tot - b3).reshape(rows_total, GLA_DK)).astype(BF16)
        if has_s0:
            s0t_sc[...] = jnp.transpose(s0_ref[dr], (1, 0))
        tri = _tri(dr)

        def super_chunk(jj, carry, dr=dr, tri=tri):
            j = jj if dr == 0 else nseg * nsc - 1 - jj
            in_seg = j % nsc
            first = (in_seg == 0) if dr == 0 else (in_seg == nsc - 1)
            last = (in_seg == nsc - 1) if dr == 0 else (in_seg == 0)

            @pl.when(first)
            def _():
                s_sc[...] = s0t_sc[...] if has_s0 else jnp.zeros_like(s_sc)

            base = j * (U * C)
            for u in range(U):
                rows = pl.ds(pl.multiple_of(base + u * C, C), C)
                v = v_ref[rows, :]
                sc = lax.dot_general(qin_sc[rows, :], kin_sc[rows, :], _NT, preferred_element_type=F32)
                o = jnp.dot(jnp.where(tri, sc, 0.0).astype(BF16), v, preferred_element_type=F32)
                st_sc[u] = lax.dot_general(v, kout_sc[rows, :], _TN, preferred_element_type=F32)
                if dr == 0:
                    o_ref[rows, :] = o
                else:
                    o_ref[rows, :] += o
            s = s_sc[...]
            for u in (range(U) if dr == 0 else reversed(range(U))):
                kv = st_sc[u]
                st_sc[u] = s
                s = dec_sc[pl.ds(j * U + u, 1), :] * s + kv
            s_sc[...] = s
            for u in range(U):
                rows = pl.ds(pl.multiple_of(base + u * C, C), C)
                o_ref[rows, :] += lax.dot_general(qin_sc[rows, :], st_sc[u].astype(BF16), _NT,
                                                  preferred_element_type=F32)
            if want_final:
                @pl.when(last)
                def _():
                    sf_ref[j // nsc, dr] = jnp.transpose(s, (1, 0))
            return carry

        lax.fori_loop(0, nseg * nsc, super_chunk, 0)


def _gla_core(proj, w2f, w2b, gate_b, s0, o_prev, row0, nseq, seqlen, nseg, want_final):
    T = proj.shape[0]
    rows = nseg * seqlen
    cps = seqlen // CHUNK
    U = math.gcd(cps, 8)
    assert row0 % rows == 0 and seqlen % CHUNK == 0 and nseq % nseg == 0
    rb = row0 // rows
    hk = GLA_H * GLA_DK
    has_s0 = s0 is not None
    assert not has_s0 or nseg == 1
    in_specs = [pl.BlockSpec((rows, GLA_DK), lambda b, h: (rb + b, h)),
                pl.BlockSpec((rows, GLA_DK), lambda b, h: (rb + b, GLA_H + h)),
                pl.BlockSpec((rows, GLA_DV), lambda b, h: (rb + b, 2 * hk // GLA_DV + h)),
                pl.BlockSpec((rows, 128), lambda b, h: (rb + b, (2 * hk + 2 * GLA_H * GLA_DV) // 128)),
                pl.BlockSpec((128, GLA_DK), lambda b, h: (0, h)),
                pl.BlockSpec((128, GLA_DK), lambda b, h: (0, h)),
                pl.BlockSpec((2, GLA_DK), lambda b, h: (0, h))]
    args = [proj, proj, proj, proj, w2f, w2b, gate_b]
    if has_s0:
        in_specs.append(pl.BlockSpec((None, 2, None, GLA_DK, GLA_DV), lambda b, h: (b, 0, h, 0, 0)))
        args.append(s0)
    in_specs.append(pl.BlockSpec(memory_space=pl.ANY))
    args.append(o_prev)
    aliases = {len(args) - 1: 0}
    out_shape = [jax.ShapeDtypeStruct((T, GLA_H * GLA_DV), F32)]
    out_specs = [pl.BlockSpec((rows, GLA_DV), lambda b, h: (rb + b, h))]
    if want_final:
        out_shape.append(jax.ShapeDtypeStruct((nseq, 2, GLA_H, GLA_DK, GLA_DV), F32))
        out_specs.append(pl.BlockSpec((nseg, 2, None, GLA_DK, GLA_DV), lambda b, h: (b, 0, h, 0, 0)))
    outs = pl.pallas_call(
        functools.partial(_gla_kernel, cps, nseg, U, has_s0, want_final),
        out_shape=tuple(out_shape),
        grid=(nseq // nseg, GLA_H),
        in_specs=in_specs,
        out_specs=tuple(out_specs),
        scratch_shapes=[pltpu.VMEM((rows, GLA_DK), BF16)] * 3
        + [pltpu.VMEM((nseg * cps, GLA_DK), F32), pltpu.VMEM((U, GLA_DV, GLA_DK), F32),
           pltpu.VMEM((GLA_DV, GLA_DK), F32), pltpu.VMEM((GLA_DV, GLA_DK), F32)],
        input_output_aliases=aliases,
        compiler_params=_cp("arbitrary", "arbitrary"),
        name="gla",
    )(*args)
    return (outs[0], outs[1]) if want_final else (outs[0], None)


def _ret_kernel(cps, nseg, U, has_s0, want_final, rope, *refs):
    q_ref, k_ref, v_ref, dm_ref, qd_ref, kd_ref, cd_ref = refs[:7]
    nxt = 7
    if rope:
        cos_ref, sin_ref = refs[7:9]
        nxt = 9
    s0_ref = refs[nxt] if has_s0 else None
    qr_sc, kr_sc, qd_sc, kd_sc, st_sc, s_sc = refs[-6:]
    outs = refs[-8:-6] if want_final else refs[-7:-6]
    o_ref = outs[0]
    sf_ref = outs[1] if want_final else None
    C = RET_CHUNK
    nsc = cps // U
    R = U * C
    SB = 64

    def rot(x, rows):
        if not rope:
            return x
        half = x.shape[1] // 2
        swapped = jnp.concatenate([pltpu.roll(x[:, :half], half // 2, 1),
                                   pltpu.roll(x[:, half:], half // 2, 1)], axis=1)
        return x * cos_ref[rows, :] + swapped * sin_ref[rows, :]

    for dr in range(2):
        def super_chunk(jj, carry, dr=dr):
            j = jj if dr == 0 else nseg * nsc - 1 - jj
            in_seg = j % nsc
            first = (in_seg == 0) if dr == 0 else (in_seg == nsc - 1)
            last = (in_seg == nsc - 1) if dr == 0 else (in_seg == 0)

            @pl.when(first)
            def _():
                s_sc[...] = s0_ref[dr] if has_s0 else jnp.zeros_like(s_sc)

            base = pl.multiple_of(j * R, R)
            rows_r = pl.ds(base, R)
            q = rot(q_ref[rows_r, :].astype(F32), rows_r)
            k = rot(k_ref[rows_r, :].astype(F32), rows_r) * (RET_DK ** -0.5)
            qr_sc[...] = q.astype(BF16)
            kr_sc[...] = k.astype(BF16)
            qd_sc[...] = (q.reshape(U, C, RET_DK) * qd_ref[dr][None]).reshape(R, RET_DK).astype(BF16)
            kd_sc[...] = (k.reshape(U, C, RET_DK) * kd_ref[dr][None]).reshape(R, RET_DK).astype(BF16)
            for u in range(U):
                loc = pl.ds(u * C, C)
                rows = pl.ds(pl.multiple_of(base + u * C, C), C)
                v = v_ref[rows, :]
                sc = lax.dot_general(qr_sc[loc, :], kr_sc[loc, :], _NT, preferred_element_type=F32)
                o = jnp.dot((sc * dm_ref[dr]).astype(BF16), v, preferred_element_type=F32)
                st_sc[u] = lax.dot_general(kd_sc[loc, :], v, _TN, preferred_element_type=F32)
                if dr == 0:
                    o_ref[rows, :] = o
                else:
                    o_ref[rows, :] += o
            cd = cd_ref[dr]
            for r0 in range(0, RET_DK, SB):
                srows = pl.ds(r0, SB)
                s = s_sc[srows, :]
                for u in (range(U) if dr == 0 else reversed(range(U))):
                    kv = st_sc[u, srows, :]
                    st_sc[u, srows, :] = s
                    s = cd * s + kv
                s_sc[srows, :] = s
            for u in range(U):
                rows = pl.ds(pl.multiple_of(base + u * C, C), C)
                o_ref[rows, :] += jnp.dot(qd_sc[pl.ds(u * C, C), :], st_sc[u].astype(BF16),
                                          preferred_element_type=F32)
            if want_final:
                @pl.when(last)
                def _():
                    sf_ref[j // nsc, dr] = s_sc[...]
            return carry

        lax.fori_loop(0, nseg * nsc, super_chunk, 0)


def _ret_tables(log_decay):
    C = RET_CHUNK
    lg = log_decay.astype(F32)[:, :, None, None]
    t = jnp.arange(C, dtype=F32)[:, None]
    s = jnp.arange(C, dtype=F32)[None, :]
    lag = jnp.stack([t - s, s - t])[:, None]
    dmask = jnp.where(lag >= 0, jnp.exp(jnp.maximum(lag, 0.0) * lg), 0.0)
    tl = jnp.arange(C, dtype=F32)[None, None, :, None]
    qdec = jnp.concatenate([jnp.exp((tl + 1.0) * lg[0:1]), jnp.exp((C - tl) * lg[1:2])], axis=0)
    kdec = jnp.concatenate([jnp.exp((C - 1.0 - tl) * lg[0:1]), jnp.exp(tl * lg[1:2])], axis=0)
    cdec = jnp.exp(C * lg)
    return dmask, qdec, kdec, cdec


def _rope_tables(seqlen, dk):
    half = dk // 2
    nf = half // 2
    pos = jnp.arange(seqlen, dtype=jnp.int32)
    inv = ROPE_BASE ** (-jnp.arange(nf, dtype=F32) / nf)
    ang_r = (pos // GRID_W).astype(F32)[:, None] * inv[None, :]
    ang_c = (pos % GRID_W).astype(F32)[:, None] * inv[None, :]
    cos = jnp.concatenate([jnp.cos(ang_r)] * 2 + [jnp.cos(ang_c)] * 2, axis=1)
    sin = jnp.concatenate([-jnp.sin(ang_r), jnp.sin(ang_r), -jnp.sin(ang_c), jnp.sin(ang_c)], axis=1)
    return cos, sin


def _ret_core(proj, tabs, s0, o_prev, row0, nseq, seqlen, nseg, want_final, rope):
    T = proj.shape[0]
    rows = nseg * seqlen
    C = RET_CHUNK
    cps = seqlen // C
    U = math.gcd(cps, 4)
    assert row0 % rows == 0 and seqlen % C == 0 and nseq % nseg == 0
    rb = row0 // rows
    hk = RET_H * RET_DK
    has_s0 = s0 is not None
    assert not (has_s0 or rope) or nseg == 1
    tspec = lambda r, c: pl.BlockSpec((2, None, r, c), lambda b, h: (0, h, 0, 0))
    in_specs = [pl.BlockSpec((rows, RET_DK), lambda b, h: (rb + b, h)),
                pl.BlockSpec((rows, RET_DK), lambda b, h: (rb + b, RET_H + h)),
                pl.BlockSpec((rows, RET_DV), lambda b, h: (rb + b, 2 * hk // RET_DV + h)),
                tspec(C, C), tspec(C, 1), tspec(C, 1), tspec(1, 1)]
    args = [proj, proj, proj, *tabs]
    if rope:
        cos, sin = _rope_tables(seqlen, RET_DK)
        in_specs += [pl.BlockSpec((seqlen, RET_DK), lambda b, h: (0, 0), pipeline_mode=pl.Buffered(1))] * 2
        args += [cos, sin]
    if has_s0:
        in_specs.append(pl.BlockSpec((None, 2, None, RET_DK, RET_DV), lambda b, h: (b, 0, h, 0, 0)))
        args.append(s0)
    in_specs.append(pl.BlockSpec(memory_space=pl.ANY))
    args.append(o_prev)
    aliases = {len(args) - 1: 0}
    out_shape = [jax.ShapeDtypeStruct((T, RET_H * RET_DV), F32)]
    out_specs = [pl.BlockSpec((rows, RET_DV), lambda b, h: (rb + b, h))]
    if want_final:
        out_shape.append(jax.ShapeDtypeStruct((nseq, 2, RET_H, RET_DK, RET_DV), F32))
        out_specs.append(pl.BlockSpec((nseg, 2, None, RET_DK, RET_DV), lambda b, h: (b, 0, h, 0, 0)))
    outs = pl.pallas_call(
        functools.partial(_ret_kernel, cps, nseg, U, has_s0, want_final, rope),
        out_shape=tuple(out_shape),
        grid=(nseq // nseg, RET_H),
        in_specs=in_specs,
        out_specs=tuple(out_specs),
        scratch_shapes=[pltpu.VMEM((U * C, RET_DK), BF16)] * 4
        + [pltpu.VMEM((U, RET_DK, RET_DV), F32), pltpu.VMEM((RET_DK, RET_DV), F32)],
        input_output_aliases=aliases,
        compiler_params=_cp("arbitrary", "arbitrary"),
        name="ret",
    )(*args)
    return (outs[0], outs[1]) if want_final else (outs[0], None)


def _headnorm_out_kernel(nh, dv, center, x_ref, o_ref, gt_ref, ng_ref, g_ref, w_ref, out_ref, a_sc):
    @pl.when(pl.program_id(1) == 0)
    def _():
        for h in range(nh):
            cols = slice(h * dv, (h + 1) * dv)
            oh = o_ref[:, cols]
            if center:
                oh = oh - jnp.mean(oh, axis=-1, keepdims=True)
            oh = oh * lax.rsqrt(jnp.mean(oh * oh, axis=-1, keepdims=True) + RMS_EPS) * ng_ref[...]
            a_sc[:, cols] = (oh * _silu(gt_ref[:, cols].astype(F32))).astype(BF16)

    out_ref[...] = x_ref[...] + g_ref[...] * jnp.dot(a_sc[...], w_ref[...], preferred_element_type=F32)


def _headnorm_out(lay, x, mods, o, proj, gate_col, norm_g, w_out, nh, dv, center):
    tm = lay.tile(512)
    tn = 512
    kdim = nh * dv
    assert gate_col % kdim == 0
    return pl.pallas_call(
        functools.partial(_headnorm_out_kernel, nh, dv, center),
        out_shape=jax.ShapeDtypeStruct((lay.T, D), F32),
        grid=(lay.T // tm, D // tn),
        in_specs=[pl.BlockSpec((tm, tn), lambda i, j: (i, j)),
                  pl.BlockSpec((tm, kdim), lambda i, j: (i, 0)),
                  pl.BlockSpec((tm, kdim), lambda i, j: (i, gate_col // kdim)),
                  pl.BlockSpec((1, dv), lambda i, j: (0, 0)),
                  pl.BlockSpec((None, 1, tn), lambda i, j: (lay.group(i * tm) * MOD_CHUNKS + 2, 0, j)),
                  pl.BlockSpec((kdim, tn), lambda i, j: (0, j))],
        out_specs=pl.BlockSpec((tm, tn), lambda i, j: (i, j)),
        scratch_shapes=[pltpu.VMEM((tm, kdim), BF16)],
        compiler_params=_cp("arbitrary", "arbitrary"),
        name="headnorm_out",
    )(x, o, proj, norm_g.reshape(1, dv), mods, w_out)


def _gla_layer(lay, x, mods, p, s0):
    hk, hv = GLA_H * GLA_DK, GLA_H * GLA_DV
    w_all = jnp.concatenate([p['w_in'], p['gate_w1'][0], p['gate_w1'][1],
                             jnp.zeros((D, 128 - 2 * GLA_RANK), F32)], axis=1).astype(BF16)
    proj = _proj(lay, x, mods, 0, 1, w_all, jnp.zeros((1, w_all.shape[1]), F32), 640)
    pad = lambda w, lo: jnp.pad(w, ((lo, 128 - GLA_RANK - lo), (0, 0))).astype(BF16)
    w2f, w2b = pad(p['gate_w2'][0], 0), pad(p['gate_w2'][1], GLA_RANK)
    o = jnp.zeros((lay.T, hv), F32)
    o, s_fin = _gla_core(proj, w2f, w2b, p['gate_b'], None, o, 0, lay.B, lay.L, math.gcd(lay.B, 8), True)
    o, _ = _gla_core(proj, w2f, w2b, p['gate_b'], s0, o, lay.TP, lay.NS, lay.LS, 1, False)
    x = _headnorm_out(lay, x, mods, o, proj, 2 * hk + hv, p['norm_g'], p['w_out'].astype(BF16),
                      GLA_H, GLA_DV, False)
    return x, s_fin


def _ret_layer(lay, x, mods, p, s0):
    hk, hv = RET_H * RET_DK, RET_H * RET_DV
    proj = _proj(lay, x, mods, 0, 1, p['w_in'].astype(BF16), jnp.zeros((1, 2 * hk + 2 * hv), F32), 768)
    tabs = _ret_tables(p['log_decay'])
    o = jnp.zeros((lay.T, hv), F32)
    o, s_fin = _ret_core(proj, tabs, None, o, 0, lay.B, lay.L, math.gcd(lay.B, 8), True, False)
    o, _ = _ret_core(proj, tabs, s0, o, lay.TP, lay.NS, lay.LS, 1, False, True)
    x = _headnorm_out(lay, x, mods, o, proj, 2 * hk + hv, p['norm_g'], p['w_out'].astype(BF16),
                      RET_H, RET_DV, True)
    return x, s_fin


MOE_BM = 512
EXPERT_TF = 1792
ROUTER_LANES = 128
DMA_UNROLL = 8


def _router_kernel(x_ref, sh_ref, sc_ref, rw_ref, h_ref, idx_ref, gate_ref):
    h = _modulate(x_ref[...], sh_ref[...], sc_ref[...])
    h_ref[...] = h
    logits = jnp.dot(h, rw_ref[...], precision=HIGHEST, preferred_element_type=F32)
    lane = lax.broadcasted_iota(jnp.int32, logits.shape, 1)
    neg = jnp.float32(-jnp.inf)
    logits = jnp.where(lane < N_EXPERTS, logits, neg)
    m1 = jnp.max(logits, axis=-1, keepdims=True)
    i1 = jnp.min(jnp.where(logits == m1, lane, ROUTER_LANES), axis=-1, keepdims=True)
    rest = jnp.where(lane == i1, neg, logits)
    m2 = jnp.max(rest, axis=-1, keepdims=True)
    i2 = jnp.min(jnp.where(rest == m2, lane, ROUTER_LANES), axis=-1, keepdims=True)
    e2 = jnp.exp(m2 - m1)
    g1 = 1.0 / (1.0 + e2)
    idx_ref[:, 0:1] = i1
    idx_ref[:, 1:2] = i2
    gate_ref[:, 0:1] = g1
    gate_ref[:, 1:2] = e2 * g1


def _router(lay, x, mods, router_w):
    tm = lay.tile(512)
    rw = jnp.pad(router_w, ((0, 0), (0, ROUTER_LANES - N_EXPERTS)))
    return pl.pallas_call(
        _router_kernel,
        out_shape=(jax.ShapeDtypeStruct((lay.T, D), F32),
                   jax.ShapeDtypeStruct((lay.T, 2), jnp.int32),
                   jax.ShapeDtypeStruct((lay.T, 2), F32)),
        grid=(lay.T // tm,),
        in_specs=[pl.BlockSpec((tm, D), lambda i: (i, 0)),
                  _mod_spec(lay, tm, 3, 1), _mod_spec(lay, tm, 4, 1),
                  pl.BlockSpec((D, ROUTER_LANES), lambda i: (0, 0))],
        out_specs=(pl.BlockSpec((tm, D), lambda i: (i, 0)),
                   pl.BlockSpec((tm, 2), lambda i: (i, 0)),
                   pl.BlockSpec((tm, 2), lambda i: (i, 0))),
        compiler_params=_cp("arbitrary"),
        name="router",
    )(x, mods, mods, rw)


def _moe_plan(idx, bm):
    a = idx.size
    e = idx.reshape(a)
    onehot = (e[:, None] == jnp.arange(N_EXPERTS, dtype=jnp.int32)[None, :]).astype(jnp.int32)
    csum = jnp.cumsum(onehot, axis=0)
    counts = csum[-1]
    rank = jnp.sum((csum - onehot) * onehot, axis=-1)
    padded = (counts + bm - 1) // bm * bm
    pad_end = jnp.cumsum(padded)
    dest = ((pad_end - padded)[e] + rank).astype(jnp.int32)
    nb = -(-(a + N_EXPERTS * (bm - 1)) // bm)
    block_start = jnp.arange(nb, dtype=jnp.int32) * bm
    block_e = jnp.minimum(jnp.searchsorted(pad_end, block_start, side='right'), N_EXPERTS - 1).astype(jnp.int32)
    nvalid = (pad_end[-1] // bm).astype(jnp.int32).reshape(1)
    return dest, block_e, nvalid, nb


def _dispatch_kernel(tm, dest_ref, h_ref, xs_in, xs_hbm, sem):
    del xs_in
    i = pl.program_id(0)

    def row_copy(r, dst):
        return pltpu.make_async_copy(h_ref.at[pl.ds(r, 1)], xs_hbm.at[pl.ds(dst, 1)], sem)

    def issue(r, carry):
        a = 2 * (i * tm + r)
        row_copy(r, dest_ref[a]).start()
        row_copy(r, dest_ref[a + 1]).start()
        return carry

    def drain(r, carry):
        row_copy(r, 0).wait()
        row_copy(r, 0).wait()
        return carry

    lax.fori_loop(0, tm, issue, 0, unroll=DMA_UNROLL)
    lax.fori_loop(0, tm, drain, 0, unroll=DMA_UNROLL)


def _dispatch(lay, h, dest, nb, bm):
    tm = lay.tile(512)
    xs = jnp.zeros((nb * bm, D), F32)
    grid_spec = pltpu.PrefetchScalarGridSpec(
        num_scalar_prefetch=1,
        grid=(lay.T // tm,),
        in_specs=[pl.BlockSpec((tm, D), lambda i, d: (i, 0)), pl.BlockSpec(memory_space=pl.ANY)],
        out_specs=pl.BlockSpec(memory_space=pl.ANY),
        scratch_shapes=[pltpu.SemaphoreType.DMA(())],
    )
    return pl.pallas_call(
        functools.partial(_dispatch_kernel, tm),
        out_shape=jax.ShapeDtypeStruct(xs.shape, F32),
        grid_spec=grid_spec,
        input_output_aliases={2: 0},
        compiler_params=_cp("arbitrary"),
        name="dispatch",
    )(dest, h, xs)


def _experts_kernel(nf, be_ref, nv_ref, xs_ref, wa_ref, wb_ref, wo_ref, o_ref, xb_sc, acc_sc):
    i = pl.program_id(0)
    f = pl.program_id(1)
    valid = i < nv_ref[0]

    @pl.when(jnp.logical_and(valid, f == 0))
    def _():
        xb_sc[...] = xs_ref[...].astype(BF16)

    @pl.when(valid)
    def _():
        xb = xb_sc[...]
        a = jnp.dot(xb, wa_ref[...], preferred_element_type=F32)
        b = jnp.dot(xb, wb_ref[...], preferred_element_type=F32)
        h = (_silu(a) * b).astype(BF16)
        y = jnp.dot(h, wo_ref[...], preferred_element_type=F32)

        @pl.when(f == 0)
        def _():
            acc_sc[...] = y

        @pl.when(f > 0)
        def _():
            acc_sc[...] += y

    @pl.when(f == nf - 1)
    def _():
        o_ref[...] = jnp.where(valid, acc_sc[...], 0.0)


def _experts(xs, block_e, nvalid, nb, bm, w_in, w_out):
    tf = EXPERT_TF
    nf = EXPERT_DIM // tf

    def wmap(off):
        def imap(i, f, be, nv):
            fe = jnp.where(i < nv[0], f, nf - 1)
            return (be[i], 0, off + fe)
        return imap

    def womap(i, f, be, nv):
        fe = jnp.where(i < nv[0], f, nf - 1)
        return (be[i], fe, 0)

    grid_spec = pltpu.PrefetchScalarGridSpec(
        num_scalar_prefetch=2,
        grid=(nb, nf),
        in_specs=[pl.BlockSpec((bm, D), lambda i, f, be, nv: (jnp.minimum(i, nv[0] - 1), 0)),
                  pl.BlockSpec((None, D, tf), wmap(0)),
                  pl.BlockSpec((None, D, tf), wmap(nf)),
                  pl.BlockSpec((None, tf, D), womap)],
        out_specs=pl.BlockSpec((bm, D), lambda i, f, be, nv: (i, 0)),
        scratch_shapes=[pltpu.VMEM((bm, D), BF16), pltpu.VMEM((bm, D), F32)],
    )
    return pl.pallas_call(
        functools.partial(_experts_kernel, nf),
        out_shape=jax.ShapeDtypeStruct((nb * bm, D), F32),
        grid_spec=grid_spec,
        compiler_params=_cp("arbitrary", "arbitrary"),
        name="experts",
    )(block_e, nvalid, xs, w_in, w_in, w_out)


def _combine_kernel(tm, nt, final, dest_ref, x_ref, gate_ref, g_ref, fg_ref, ys_hbm, o_ref, y_sc, sem):
    i = pl.program_id(0)
    slot = i % 2

    def row_copy(s, k, r, src):
        return pltpu.make_async_copy(ys_hbm.at[pl.ds(src, 1)], y_sc.at[s, k, pl.ds(r, 1)], sem.at[s])

    def issue_tile(t, s):
        def issue(r, carry):
            a = 2 * (t * tm + r)
            row_copy(s, 0, r, dest_ref[a]).start()
            row_copy(s, 1, r, dest_ref[a + 1]).start()
            return carry
        lax.fori_loop(0, tm, issue, 0, unroll=DMA_UNROLL)

    @pl.when(i == 0)
    def _():
        issue_tile(0, 0)

    @pl.when(i + 1 < nt)
    def _():
        issue_tile(i + 1, 1 - slot)

    def drain(r, carry):
        row_copy(slot, 0, r, 0).wait()
        row_copy(slot, 1, r, 0).wait()
        return carry

    lax.fori_loop(0, tm, drain, 0, unroll=DMA_UNROLL)
    gate = gate_ref[...]
    out = x_ref[...] + g_ref[...] * (gate[:, 0:1] * y_sc[slot, 0] + gate[:, 1:2] * y_sc[slot, 1])
    if final:
        ms = jnp.mean(out * out, axis=-1, keepdims=True)
        out = out * lax.rsqrt(ms + RMS_EPS) * fg_ref[...]
    o_ref[...] = out


def _combine(lay, x, mods, gates, ys, dest, final_g):
    tm = lay.tile(256)
    nt = lay.T // tm
    final = final_g is not None
    fg = (final_g if final else jnp.ones((D,), F32)).reshape(1, D)
    grid_spec = pltpu.PrefetchScalarGridSpec(
        num_scalar_prefetch=1,
        grid=(nt,),
        in_specs=[pl.BlockSpec((tm, D), lambda i, d: (i, 0)),
                  pl.BlockSpec((tm, 2), lambda i, d: (i, 0)),
                  pl.BlockSpec((None, 1, D), lambda i, d: (lay.group(i * tm) * MOD_CHUNKS + 5, 0, 0)),
                  pl.BlockSpec((1, D), lambda i, d: (0, 0)),
                  pl.BlockSpec(memory_space=pl.ANY)],
        out_specs=pl.BlockSpec((tm, D), lambda i, d: (i, 0)),
        scratch_shapes=[pltpu.VMEM((2, 2, tm, D), F32), pltpu.SemaphoreType.DMA((2,))],
    )
    return pl.pallas_call(
        functools.partial(_combine_kernel, tm, nt, final),
        out_shape=jax.ShapeDtypeStruct((lay.T, D), F32),
        grid_spec=grid_spec,
        compiler_params=_cp("arbitrary"),
        name="combine",
    )(dest, x, gates, mods, fg, ys)


def _moe_layer(lay, x, mods, router_w, w_in, w_out, final_g=None, bm=MOE_BM):
    h, idx, gates = _router(lay, x, mods, router_w)
    dest, block_e, nvalid, nb = _moe_plan(idx, bm)
    xs = _dispatch(lay, h, dest, nb, bm)
    ys = _experts(xs, block_e, nvalid, nb, bm, w_in, w_out)
    return _combine(lay, x, mods, gates, ys, dest, final_g)


def kernel(x_prompt, x_sample, c, state_l0_s5_re, state_l0_s5_im, state_l2_gla, state_l3_ret, c_ctx, l0_mod_w, l0_mod_b, l0_s5_a_re, l0_s5_a_im, l0_s5_log_dt, l0_s5_b_re, l0_s5_b_im, l0_s5_c_re, l0_s5_c_im, l0_s5_d, l0_s5_glu_w, l0_ffn_w_in, l0_ffn_w_out, l1_mod_w, l1_mod_b, l1_hy_w_in, l1_hy_b_in, l1_hy_short_w, l1_hy_short_b, l1_hy_f_w1, l1_hy_f_b1, l1_hy_f_w2, l1_hy_f_b2, l1_hy_f_w3, l1_hy_f_freq, l1_hy_skip, l1_hy_w_out, l1_hy_b_out, l1_moe_router, l1_moe_w_in, l1_moe_w_out, l2_mod_w, l2_mod_b, l2_gla_w_in, l2_gla_gate_w1, l2_gla_gate_w2, l2_gla_gate_b, l2_gla_norm_g, l2_gla_w_out, l2_ffn_w_in, l2_ffn_w_out, l3_mod_w, l3_mod_b, l3_ret_w_in, l3_ret_log_decay, l3_ret_norm_g, l3_ret_w_out, l3_moe_router, l3_moe_w_in, l3_moe_w_out, final_norm_g):
    B, L, _ = x_prompt.shape
    NS, LS, _ = x_sample.shape
    lay = Layout(B, L, NS, LS)
    x = jnp.concatenate([x_prompt.reshape(B * L, D), x_sample.reshape(NS * LS, D)], axis=0)
    cond = jnp.concatenate([c_ctx[None], c, jnp.zeros((8 - 1 - NS, D), F32)], axis=0)
    mods0 = _mods(cond, l0_mod_w, l0_mod_b)
    p0 = dict(a_re=l0_s5_a_re, a_im=l0_s5_a_im, log_dt=l0_s5_log_dt, b_re=l0_s5_b_re, b_im=l0_s5_b_im,
              c_re=l0_s5_c_re, c_im=l0_s5_c_im, d=l0_s5_d, glu_w=l0_s5_glu_w.astype(BF16))
    x, s5_re, s5_im = _s5_layer(lay, x, mods0, p0, state_l0_s5_re, state_l0_s5_im)
    x = _ffn(lay, x, mods0, l0_ffn_w_in.astype(BF16), l0_ffn_w_out.astype(BF16))

    mods1 = _mods(cond, l1_mod_w, l1_mod_b)
    p1 = dict(w_in=l1_hy_w_in, b_in=l1_hy_b_in, short_w=l1_hy_short_w, short_b=l1_hy_short_b,
              f_w1=l1_hy_f_w1, f_b1=l1_hy_f_b1, f_w2=l1_hy_f_w2, f_b2=l1_hy_f_b2, f_w3=l1_hy_f_w3,
              f_freq=l1_hy_f_freq, skip=l1_hy_skip, w_out=l1_hy_w_out, b_out=l1_hy_b_out)
    x = _hyena_layer(lay, x, mods1, p1)
    x = _moe_layer(lay, x, mods1, l1_moe_router, l1_moe_w_in.astype(BF16), l1_moe_w_out.astype(BF16))

    mods2 = _mods(cond, l2_mod_w, l2_mod_b)
    p2 = dict(w_in=l2_gla_w_in, gate_w1=l2_gla_gate_w1, gate_w2=l2_gla_gate_w2, gate_b=l2_gla_gate_b,
              norm_g=l2_gla_norm_g, w_out=l2_gla_w_out)
    x, gla_state = _gla_layer(lay, x, mods2, p2, state_l2_gla)
    x = _ffn(lay, x, mods2, l2_ffn_w_in.astype(BF16), l2_ffn_w_out.astype(BF16))

    mods3 = _mods(cond, l3_mod_w, l3_mod_b)
    p3 = dict(w_in=l3_ret_w_in, log_decay=l3_ret_log_decay, norm_g=l3_ret_norm_g, w_out=l3_ret_w_out)
    x, ret_state = _ret_layer(lay, x, mods3, p3, state_l3_ret)
    y = _moe_layer(lay, x, mods3, l3_moe_router, l3_moe_w_in.astype(BF16), l3_moe_w_out.astype(BF16),
                   final_g=final_norm_g)
    return (y[:lay.TP].reshape(B, L, D), y[lay.TP:].reshape(NS, LS, D), s5_re, s5_im, gla_state, ret_state)
```

```python
import functools
import math

import jax
import jax.numpy as jnp
import numpy as np
from jax import lax
from jax.experimental import pallas as pl
from jax.experimental.pallas import tpu as pltpu

F32 = jnp.float32
BF16 = jnp.bfloat16
HIGHEST = lax.Precision.HIGHEST

D = 1024
RMS_EPS = 1e-6
MOD_CHUNKS = 6
GRID_W = 64

S5_Q = 16
S5_G = D // S5_Q
S5_P = 64
S5_T = 16
S5_SCAN_ROWS = 64

HY_BANDS = 16
HY_TARGET = 1e-2
HY_FAST_PCT = 0.3
HY_SLOW_PCT = 1.5

GLA_H, GLA_DK, GLA_DV = 4, 128, 256
GLA_RANK = 16
GLA_TAU = 16.0
RET_H, RET_DK, RET_DV = 4, 256, 512
CHUNK = 64
RET_CHUNK = 256
ROPE_BASE = 10000.0

FFN_DIM = 2816
N_EXPERTS = 8
EXPERT_DIM = 3584

VMEM_LIMIT_V7X = 56 * 1024 * 1024


def _cp(*sem):
    return pltpu.CompilerParams(dimension_semantics=sem, vmem_limit_bytes=VMEM_LIMIT_V7X)


def _silu(x):
    return x * jax.nn.sigmoid(x)


def _modulate(x, shift, scale):
    ms = jnp.mean(x * x, axis=-1, keepdims=True)
    return x * lax.rsqrt(ms + RMS_EPS) * (1.0 + scale) + shift


class Layout:
    def __init__(self, n_prompt, l_prompt, n_sample, l_sample):
        self.B, self.L, self.NS, self.LS = n_prompt, l_prompt, n_sample, l_sample
        self.TP = n_prompt * l_prompt
        self.T = self.TP + n_sample * l_sample

    def tile(self, want):
        t = math.gcd(math.gcd(self.TP, self.LS), want)
        assert t % 8 == 0
        return t

    def group(self, row):
        return jnp.where(row < self.TP, 0, 1 + (row - self.TP) // self.LS)


def _mod_spec(lay, tm, chunk, ngrid):
    def imap(*ids):
        return (lay.group(ids[0] * tm) * MOD_CHUNKS + chunk, 0, 0)
    del ngrid
    return pl.BlockSpec((None, 1, D), imap)


def _mods_kernel(c_ref, w_ref, b_ref, o_ref):
    o_ref[...] = jnp.dot(_silu(c_ref[...]), w_ref[...], precision=HIGHEST,
                         preferred_element_type=F32) + b_ref[...]


def _mods(cond, w, b):
    n = MOD_CHUNKS * D
    tn = 1536
    out = pl.pallas_call(
        _mods_kernel,
        out_shape=jax.ShapeDtypeStruct((8, n), F32),
        grid=(n // tn,),
        in_specs=[pl.BlockSpec((8, D), lambda j: (0, 0)),
                  pl.BlockSpec((D, tn), lambda j: (0, j)),
                  pl.BlockSpec((1, tn), lambda j: (0, j))],
        out_specs=pl.BlockSpec((8, tn), lambda j: (0, j)),
        compiler_params=_cp("arbitrary"),
        name="mods",
    )(cond, w, b.reshape(1, n))
    return out.reshape(8 * MOD_CHUNKS, 1, D)


def _modulate_kernel(x_ref, sh_ref, sc_ref, o_ref):
    o_ref[...] = _modulate(x_ref[...], sh_ref[...], sc_ref[...]).astype(o_ref.dtype)


def _modulate_call(lay, x, mods, c_shift, c_scale, dtype):
    tm = lay.tile(512)
    return pl.pallas_call(
        _modulate_kernel,
        out_shape=jax.ShapeDtypeStruct((lay.T, D), dtype),
        grid=(lay.T // tm,),
        in_specs=[pl.BlockSpec((tm, D), lambda i: (i, 0)),
                  _mod_spec(lay, tm, c_shift, 1), _mod_spec(lay, tm, c_scale, 1)],
        out_specs=pl.BlockSpec((tm, D), lambda i: (i, 0)),
        compiler_params=_cp("arbitrary"),
        name="modulate",
    )(x, mods, mods)


def _proj_kernel(x_ref, sh_ref, sc_ref, w_ref, b_ref, o_ref, u_sc):
    @pl.when(pl.program_id(1) == 0)
    def _():
        u_sc[...] = _modulate(x_ref[...], sh_ref[...], sc_ref[...]).astype(BF16)

    acc = jnp.dot(u_sc[...], w_ref[...], preferred_element_type=F32) + b_ref[...]
    o_ref[...] = acc.astype(o_ref.dtype)


def _proj(lay, x, mods, c_shift, c_scale, w, b, tn, out_dtype=BF16):
    tm = lay.tile(1024)
    n = w.shape[1]
    assert n % tn == 0
    return pl.pallas_call(
        _proj_kernel,
        out_shape=jax.ShapeDtypeStruct((lay.T, n), out_dtype),
        grid=(lay.T // tm, n // tn),
        in_specs=[pl.BlockSpec((tm, D), lambda i, j: (i, 0)),
                  _mod_spec(lay, tm, c_shift, 2), _mod_spec(lay, tm, c_scale, 2),
                  pl.BlockSpec((D, tn), lambda i, j: (0, j)),
                  pl.BlockSpec((1, tn), lambda i, j: (0, j))],
        out_specs=pl.BlockSpec((tm, tn), lambda i, j: (i, j)),
        scratch_shapes=[pltpu.VMEM((tm, D), BF16)],
        compiler_params=_cp("arbitrary", "arbitrary"),
        name="proj",
    )(x, mods, mods, w, b)


def _ffn_kernel(nf, x_ref, sh_ref, sc_ref, g_ref, wa_ref, wb_ref, wo_ref, o_ref, u_sc, acc_sc):
    f = pl.program_id(1)

    @pl.when(f == 0)
    def _():
        u_sc[...] = _modulate(x_ref[...], sh_ref[...], sc_ref[...]).astype(BF16)
        acc_sc[...] = jnp.zeros_like(acc_sc)

    u = u_sc[...]
    a = jnp.dot(u, wa_ref[...], preferred_element_type=F32)
    b = jnp.dot(u, wb_ref[...], preferred_element_type=F32)
    h = (_silu(a) * b).astype(BF16)
    acc_sc[...] += jnp.dot(h, wo_ref[...], preferred_element_type=F32)

    @pl.when(f == nf - 1)
    def _():
        o_ref[...] = x_ref[...] + g_ref[...] * acc_sc[...]


def _ffn(lay, x, mods, w_in, w_out):
    tm = lay.tile(512)
    tf = 1408
    nf = FFN_DIM // tf
    return pl.pallas_call(
        functools.partial(_ffn_kernel, nf),
        out_shape=jax.ShapeDtypeStruct((lay.T, D), F32),
        grid=(lay.T // tm, nf),
        in_specs=[pl.BlockSpec((tm, D), lambda i, f: (i, 0)),
                  _mod_spec(lay, tm, 3, 2), _mod_spec(lay, tm, 4, 2), _mod_spec(lay, tm, 5, 2),
                  pl.BlockSpec((D, tf), lambda i, f: (0, f)),
                  pl.BlockSpec((D, tf), lambda i, f: (0, nf + f)),
                  pl.BlockSpec((tf, D), lambda i, f: (f, 0))],
        out_specs=pl.BlockSpec((tm, D), lambda i, f: (i, 0)),
        scratch_shapes=[pltpu.VMEM((tm, D), BF16), pltpu.VMEM((tm, D), F32)],
        compiler_params=_cp("arbitrary", "arbitrary"),
        name="ffn",
    )(x, mods, mods, mods, w_in, w_in, w_out)


def _s5_tables(a_re, a_im, log_dt, b_re, b_im, c_re, c_im, d_skip):
    T, G, P, Q = S5_T, S5_G, S5_P, S5_Q
    a = lax.complex(a_re, a_im)
    adt = a * jnp.exp(log_dt)[..., None]
    lam = jnp.exp(adt)
    bb = ((lam - 1.0) / a)[..., None] * lax.complex(b_re, b_im)
    cm = lax.complex(c_re, c_im)
    steps = jnp.arange(T + 1, dtype=F32)
    pw = jnp.exp(steps[None, :, None, None] * adt[:, None])
    kern = jnp.real(jnp.einsum('dgqp,djgp,dgpr->djgqr', cm, pw[:, :T], bb))
    s_i = jnp.arange(T)[:, None]
    t_i = jnp.arange(T)[None, :]
    kf = kern[0][jnp.clip(t_i - s_i, 0, T - 1)] * (t_i >= s_i)[..., None, None, None]
    kb = kern[1][jnp.clip(s_i - t_i, 0, T - 1)] * (s_i >= t_i)[..., None, None, None]
    m = jnp.transpose(kf + kb, (2, 0, 4, 1, 3))
    eye = (jnp.eye(T)[:, None, :, None] * jnp.eye(Q)[None, :, None, :])
    m = m + eye[None] * d_skip.reshape(G, 1, 1, 1, Q)
    m = m.reshape(G, T * Q, T * Q)
    e_f = pw[0][T - 1 - jnp.arange(T)]
    e_b = pw[1][jnp.arange(T)]
    n_f = e_f[..., None] * bb[0][None]
    n_b = e_b[..., None] * bb[1][None]
    n_c = jnp.concatenate([n_f, n_b], axis=2)
    n_c = jnp.transpose(n_c, (1, 0, 3, 2)).reshape(G, T * Q, 2 * P)
    lam_t = jnp.concatenate([pw[0][T], pw[1][T]], axis=-1)
    w_f = cm[0][:, None] * jnp.transpose(pw[0][1:T + 1], (1, 0, 2))[:, :, None, :]
    w_b = cm[1][:, None] * jnp.transpose(pw[1][T - jnp.arange(T)], (1, 0, 2))[:, :, None, :]
    w_f = jnp.transpose(w_f, (0, 3, 1, 2)).reshape(G, P, T * Q)
    w_b = jnp.transpose(w_b, (0, 3, 1, 2)).reshape(G, P, T * Q)
    z = jnp.zeros_like(jnp.real(w_f))
    c_mats = dict(c_f_re=jnp.concatenate([jnp.real(w_f), z], axis=1),
                  c_f_im=jnp.concatenate([-jnp.imag(w_f), z], axis=1),
                  c_b_re=jnp.concatenate([z, jnp.real(w_b)], axis=1),
                  c_b_im=jnp.concatenate([z, -jnp.imag(w_b)], axis=1))
    return dict(m=m.astype(BF16), n_re=jnp.real(n_c).astype(BF16), n_im=jnp.imag(n_c).astype(BF16),
                l_re=jnp.real(lam_t), l_im=jnp.imag(lam_t), **{k: v.astype(BF16) for k, v in c_mats.items()})


def _s5_in_kernel(u_ref, m_ref, nre_ref, nim_ref, yi_ref, sre_ref, sim_ref):
    u = u_ref[...]
    yi_ref[...] = jnp.dot(u, m_ref[...], preferred_element_type=F32)
    sre_ref[...] = jnp.dot(u, nre_ref[...], preferred_element_type=F32).reshape(sre_ref.shape)
    sim_ref[...] = jnp.dot(u, nim_ref[...], preferred_element_type=F32).reshape(sim_ref.shape)


def _s5_in(ug, tabs):
    G, R, W = ug.shape
    P2 = 2 * S5_P
    RB = S5_SCAN_ROWS
    assert R % RB == 0
    gspec = lambda n: pl.BlockSpec((None, W, n), lambda g: (g, 0, 0))
    rspec = pl.BlockSpec((None, R, W), lambda g: (g, 0, 0))
    sspec = pl.BlockSpec((R // RB, RB, P2), lambda g: (0, g, 0))
    sshape = jax.ShapeDtypeStruct((R // RB, G * RB, P2), F32)
    return pl.pallas_call(
        _s5_in_kernel,
        out_shape=(jax.ShapeDtypeStruct((G, R, W), F32), sshape, sshape),
        grid=(G,),
        in_specs=[rspec, gspec(W), gspec(P2), gspec(P2)],
        out_specs=(rspec, sspec, sspec),
        compiler_params=_cp("arbitrary"),
        name="s5_in",
    )(ug, tabs['m'], tabs['n_re'], tabs['n_im'])


def _s5_scan_kernel(nsb, ncb, nblk, sref_ref, simf_ref, sreb_ref, simb_ref, lre_ref, lim_ref,
                    h0re_ref, h0im_ref, *rest):
    hfre_ref, hfim_ref, hbre_ref, hbim_ref, fre_ref, fim_ref, cre_sc, cim_sc = rest[4:]
    P = S5_P
    rows = sref_ref.shape[0] // ncb
    j = pl.program_id(1)
    fwd = lax.broadcasted_iota(jnp.int32, (1, 2 * P), 1) < P
    lre = lre_ref[...]
    lim = lim_ref[...]

    @pl.when(j == 0)
    def _():
        cre_sc[...] = h0re_ref[...]
        cim_sc[...] = h0im_ref[...]

    def at(k):
        return pl.ds(k, rows, stride=ncb)

    def body(k, carry):
        hre, him = carry
        kb = ncb - 1 - k
        hfre_ref[at(k), :] = hre
        hfim_ref[at(k), :] = him
        hbre_ref[at(kb), :] = hre
        hbim_ref[at(kb), :] = him
        sre = jnp.where(fwd, sref_ref[at(k), :], sreb_ref[at(kb), :])
        sim = jnp.where(fwd, simf_ref[at(k), :], simb_ref[at(kb), :])
        return (lre * hre - lim * him + sre, lre * him + lim * hre + sim)

    hre, him = lax.fori_loop(0, ncb, body, (cre_sc[...], cim_sc[...]), unroll=4)
    cre_sc[...] = hre
    cim_sc[...] = him

    @pl.when(j == nblk - 1)
    def _():
        fre_ref[...] = hre
        fim_ref[...] = him


def _s5_scan(sre, sim, tabs, h0re, h0im, hprev, row0, nseq, nc, nsb, nblk):
    _, grb, P2 = sre.shape
    rb = S5_SCAN_ROWS
    G = grb // rb
    assert nblk == 1 or nsb == 1
    ncb = nc // nblk
    assert nsb * ncb == rb and row0 % rb == 0 and nseq % nsb == 0 and nc % nblk == 0
    b0 = row0 // rb
    fspec = pl.BlockSpec((None, grb, P2), lambda i, j: (b0 + i * nblk + j, 0, 0))
    bspec = pl.BlockSpec((None, grb, P2), lambda i, j: (b0 + i * nblk + nblk - 1 - j, 0, 0))
    lspec = pl.BlockSpec((G * nsb, P2), lambda i, j: (0, 0))
    qspec = pl.BlockSpec((None, G * nsb, P2), lambda i, j: (i, 0, 0))
    anyspec = pl.BlockSpec(memory_space=pl.ANY)
    fin = jax.ShapeDtypeStruct((nseq // nsb, G * nsb, P2), F32)
    rep = lambda a: jnp.repeat(a, nsb, axis=0)
    flat = lambda a: a.reshape(nseq // nsb, G * nsb, P2)
    outs = pl.pallas_call(
        functools.partial(_s5_scan_kernel, nsb, ncb, nblk),
        out_shape=tuple(jax.ShapeDtypeStruct(h.shape, h.dtype) for h in hprev) + (fin, fin),
        grid=(nseq // nsb, nblk),
        in_specs=[fspec, fspec, bspec, bspec, lspec, lspec, qspec, qspec] + [anyspec] * 4,
        out_specs=(fspec, fspec, bspec, bspec, qspec, qspec),
        scratch_shapes=[pltpu.VMEM((G * nsb, P2), F32), pltpu.VMEM((G * nsb, P2), F32)],
        input_output_aliases={8: 0, 9: 1, 10: 2, 11: 3},
        compiler_params=_cp("arbitrary", "arbitrary"),
        name="s5_scan",
    )(sre, sim, sre, sim, rep(tabs['l_re']), rep(tabs['l_im']), flat(h0re), flat(h0im), *hprev)
    return outs[:4], outs[4].reshape(nseq // nsb, G, nsb, P2), outs[5].reshape(nseq // nsb, G, nsb, P2)


def _s5_out_kernel(yi_ref, hfre_ref, hfim_ref, hbre_ref, hbim_ref, cfre_ref, cfim_ref, cbre_ref, cbim_ref,
                   y_ref):
    y = yi_ref[...]
    for h_ref, c_ref in ((hfre_ref, cfre_ref), (hfim_ref, cfim_ref), (hbre_ref, cbre_ref), (hbim_ref, cbim_ref)):
        h = h_ref[...].reshape(y.shape[0], h_ref.shape[-1])
        y += jnp.dot(h.astype(BF16), c_ref[...], preferred_element_type=F32)
    y_ref[...] = y.astype(y_ref.dtype)


def _s5_out(yi, hprev, tabs):
    G, R, W = yi.shape
    P2 = 2 * S5_P
    RB = S5_SCAN_ROWS
    gspec = pl.BlockSpec((None, P2, W), lambda g: (g, 0, 0))
    hspec = pl.BlockSpec((R // RB, RB, P2), lambda g: (0, g, 0))
    rspec = pl.BlockSpec((None, R, W), lambda g: (g, 0, 0))
    return pl.pallas_call(
        _s5_out_kernel,
        out_shape=jax.ShapeDtypeStruct((G, R, W), F32),
        grid=(G,),
        in_specs=[rspec] + [hspec] * 4 + [gspec] * 4,
        out_specs=rspec,
        compiler_params=_cp("arbitrary"),
        name="s5_out",
    )(yi, *hprev, tabs['c_f_re'], tabs['c_f_im'], tabs['c_b_re'], tabs['c_b_im'])


LANES = 128
S5_GB = LANES // S5_Q


def _block_transpose(sets):
    blk = lax.broadcasted_iota(jnp.int32, sets[0][0].shape, 1) // S5_Q
    sets = [list(regs) for regs in sets]
    d = S5_GB // 2
    while d:
        keep = (blk & d) == 0
        for regs in sets:
            for i in range(S5_GB):
                if i & d:
                    continue
                a, b = regs[i], regs[i + d]
                regs[i] = jnp.where(keep, a, pltpu.roll(b, d * S5_Q, 1))
                regs[i + d] = jnp.where(keep, pltpu.roll(a, LANES - d * S5_Q, 1), b)
        d //= 2
    return sets


def _s5_pre_kernel(tm, x_ref, sh_ref, sc_ref, ug_ref, u_sc):
    u = _modulate(x_ref[...], sh_ref[...], sc_ref[...])
    for j in range(D // LANES):
        u_sc[j] = u[:, j * LANES:(j + 1) * LANES]
    rows16 = 16
    nh = S5_T // S5_GB
    for c in range(tm // (S5_T * rows16)):
        base = c * S5_T * rows16
        for j in range(D // LANES):
            sets = [[u_sc[j, pl.ds(base + h * S5_GB + s, rows16, stride=S5_T), :] for s in range(S5_GB)]
                    for h in range(nh)]
            for h, regs in enumerate(_block_transpose(sets)):
                for gl, t in enumerate(regs):
                    ug_ref[j * S5_GB + gl, c * rows16:(c + 1) * rows16, h * LANES:(h + 1) * LANES] = t.astype(BF16)


def _s5_pre(lay, x, mods):
    tm = lay.tile(512)
    assert tm % (S5_T * 16) == 0
    return pl.pallas_call(
        functools.partial(_s5_pre_kernel, tm),
        out_shape=jax.ShapeDtypeStruct((S5_G, lay.T // S5_T, S5_T * S5_Q), BF16),
        grid=(lay.T // tm,),
        in_specs=[pl.BlockSpec((tm, D), lambda i: (i, 0)), _mod_spec(lay, tm, 0, 1), _mod_spec(lay, tm, 1, 1)],
        out_specs=pl.BlockSpec((S5_G, tm // S5_T, S5_T * S5_Q), lambda i: (0, i, 0)),
        scratch_shapes=[pltpu.VMEM((D // LANES, tm, LANES), F32)],
        compiler_params=_cp("arbitrary"),
        name="s5_pre",
    )(x, mods, mods)


def _s5_glu_kernel(tm, x_ref, yg_ref, g_ref, wv_ref, wg_ref, o_ref, a_sc, y_sc):
    @pl.when(pl.program_id(1) == 0)
    def _():
        def sub_tile(c, carry):
            crow = pl.ds(pl.multiple_of(c * 8, 8), 8)
            base = c * (S5_T * 8)
            nh = S5_T // S5_GB
            for j2 in range(0, D // LANES, 2):
                keys = [(j, h) for j in (j2, j2 + 1) for h in range(nh)]
                sets = [[yg_ref[j * S5_GB + gl, crow, h * LANES:(h + 1) * LANES] for gl in range(S5_GB)]
                        for j, h in keys]
                for (j, h), regs in zip(keys, _block_transpose(sets)):
                    for s, t in enumerate(regs):
                        y_sc[j, pl.ds(base + h * S5_GB + s, 8, stride=S5_T), :] = t
            return carry

        lax.fori_loop(0, tm // (S5_T * 8), sub_tile, 0)
        for j in range(D // LANES):
            a_sc[:, j * LANES:(j + 1) * LANES] = jax.nn.gelu(y_sc[j]).astype(BF16)

    a = a_sc[...]
    val = jnp.dot(a, wv_ref[...], preferred_element_type=F32)
    gate = jnp.dot(a, wg_ref[...], preferred_element_type=F32)
    o_ref[...] = x_ref[...] + g_ref[...] * (val * jax.nn.sigmoid(gate))


def _s5_glu(lay, x, yg, mods, glu_w):
    tm = lay.tile(1024)
    assert tm % (S5_T * 8) == 0
    tn = 512
    nn = D // tn
    return pl.pallas_call(
        functools.partial(_s5_glu_kernel, tm),
        out_shape=jax.ShapeDtypeStruct((lay.T, D), F32),
        grid=(lay.T // tm, nn),
        in_specs=[pl.BlockSpec((tm, tn), lambda i, j: (i, j)),
                  pl.BlockSpec((S5_G, tm // S5_T, S5_T * S5_Q), lambda i, j: (0, i, 0)),
                  pl.BlockSpec((None, 1, tn), lambda i, j: (lay.group(i * tm) * MOD_CHUNKS + 2, 0, j)),
                  pl.BlockSpec((D, tn), lambda i, j: (0, j)),
                  pl.BlockSpec((D, tn), lambda i, j: (0, nn + j))],
        out_specs=pl.BlockSpec((tm, tn), lambda i, j: (i, j)),
        scratch_shapes=[pltpu.VMEM((tm, D), BF16), pltpu.VMEM((D // LANES, tm, LANES), F32)],
        compiler_params=_cp("arbitrary", "arbitrary"),
        name="s5_glu",
    )(x, yg, mods, glu_w, glu_w)


def _s5_layer(lay, x, mods, p, h0_re, h0_im):
    T, G, P, Q = S5_T, S5_G, S5_P, S5_Q
    tabs = _s5_tables(p['a_re'], p['a_im'], p['log_dt'], p['b_re'], p['b_im'], p['c_re'], p['c_im'], p['d'])
    R = lay.T // T
    ug = _s5_pre(lay, x, mods)
    yi, sre, sim = _s5_in(ug, tabs)
    hprev = tuple(jnp.zeros(sre.shape, F32) for _ in range(4))
    ncp, ncs = lay.L // T, lay.LS // T
    nsb = S5_SCAN_ROWS // ncp
    zero = jnp.zeros((lay.B // nsb, G, nsb, 2 * P), F32)
    hprev, fre, fim = _s5_scan(sre, sim, tabs, zero, zero, hprev, 0, lay.B, ncp, nsb, 1)
    to_lanes = lambda s: jnp.transpose(s, (0, 2, 1, 3)).reshape(lay.NS, G, 1, 2 * P)
    hprev, _, _ = _s5_scan(sre, sim, tabs, to_lanes(h0_re), to_lanes(h0_im), hprev,
                           lay.TP // T, lay.NS, ncs, 1, max(1, ncs // S5_SCAN_ROWS))
    yg = _s5_out(yi, hprev, tabs)
    x = _s5_glu(lay, x, yg, mods, p['glu_w'])
    from_lanes = lambda s: jnp.transpose(s, (0, 2, 1, 3)).reshape(lay.B, G, 2, P).transpose(0, 2, 1, 3)
    return x, from_lanes(fre), from_lanes(fim)


def _hyena_filters(L, p):
    mm = functools.partial(jnp.matmul, precision=HIGHEST)
    f = jnp.linspace(1e-4, HY_BANDS - 1, HY_BANDS, dtype=F32)[None, :]
    max_decay = math.log(HY_TARGET) / HY_FAST_PCT
    min_decay = math.log(HY_TARGET) / HY_SLOW_PCT
    deltas = jnp.abs(jnp.linspace(min_decay, max_decay, D, dtype=F32))
    w3 = p['f_w3'].reshape(-1, 2, 2, D)

    def side(pos, s):
        t = (pos.astype(F32) / (L - 1))[:, None]
        w = 2.0 * math.pi * pos.astype(F32)[:, None] / L
        feats = jnp.concatenate([t, jnp.cos(f * w), -jnp.sin(f * w)], axis=-1)
        z = jnp.sin(p['f_freq'][0] * (mm(feats, p['f_w1']) + p['f_b1']))
        z = jnp.sin(p['f_freq'][1] * (mm(z, p['f_w2']) + p['f_b2']))
        win = jnp.exp(-t * deltas)
        return jnp.stack([mm(z, w3[:, o, s]) * win for o in range(2)])

    j = jnp.arange(L, dtype=jnp.int32)
    k_lo = side(j, 0)
    k_hi = side((L - j) % L, 1) * (j > 0).astype(F32)[None, :, None]
    norm = jnp.sum(jnp.abs(k_lo), axis=1, keepdims=True) + jnp.sum(jnp.abs(k_hi), axis=1, keepdims=True)
    k_lo, k_hi = k_lo / norm, k_hi / norm
    alt = (1.0 - 2.0 * (j % 2).astype(F32))[None, :, None]
    k_ny = jnp.sum(alt * (k_lo + k_hi), axis=1) / (2 * L)
    return k_lo, k_hi, k_ny


def _dft_tables(L):
    r = math.isqrt(L)
    assert r * r == L
    t = jnp.arange(L, dtype=jnp.int32)[None, :]
    a = jnp.arange(r, dtype=jnp.int32)[:, None]

    def unit(idx):
        ang = (idx % (2 * L)).astype(F32) * (math.pi / L)
        return jnp.cos(ang), jnp.sin(ang)

    c1, s1 = unit(a * r * t)
    c2, s2 = unit(a * t)
    c1, s1, c2, s2 = c1[:, None], s1[:, None], c2[None], s2[None]
    cos = (c1 * c2 - s1 * s2).reshape(L, L)
    sin = (s1 * c2 + c1 * s2).reshape(L, L)
    return cos.astype(BF16), sin.astype(BF16)


def _hy_spec_kernel(L, tr, c_ref, s_ref, klo_ref, khi_ref, p_ref, q_ref):
    r = pl.program_id(2)
    f = r * tr + lax.broadcasted_iota(jnp.int32, (tr, 1), 0)
    sgn = (1 - 2 * (f % 2)).astype(F32)
    scale = jnp.where(f == 0, 1.0, 2.0) * (1.0 / (2 * L))
    c, s = c_ref[...], s_ref[...]
    lo, hi = klo_ref[...], khi_ref[...]
    dot = functools.partial(jnp.dot, preferred_element_type=F32)
    p_ref[...] = scale * (dot(c, lo) + sgn * dot(c, hi))
    q_ref[...] = scale * (dot(s, lo) + sgn * dot(s, hi))


def _hy_spectrum(L, cos, sin, k_lo, k_hi):
    tr = min(L, 512)
    tc = 512
    kspec = pl.BlockSpec((None, L, tc), lambda o, j, r: (o, 0, j))
    tspec = pl.BlockSpec((tr, L), lambda o, j, r: (r, 0))
    ospec = pl.BlockSpec((None, tr, tc), lambda o, j, r: (o, r, j))
    return pl.pallas_call(
        functools.partial(_hy_spec_kernel, L, tr),
        out_shape=(jax.ShapeDtypeStruct((2, L, D), F32), jax.ShapeDtypeStruct((2, L, D), F32)),
        grid=(2, D // tc, L // tr),
        in_specs=[tspec, tspec, kspec, kspec],
        out_specs=(ospec, ospec),
        compiler_params=_cp("arbitrary", "arbitrary", "arbitrary"),
        name="hy_spectrum",
    )(cos, sin, k_lo.astype(BF16), k_hi.astype(BF16))


def _hy_core_kernel(L, tr, ngrp, tc, x1_ref, x2_ref, v_ref, sw1_ref, sw2_ref, swv_ref, sb1_ref, sb2_ref,
                    sbv_ref, c_ref, s_ref, p_ref, q_ref, kny_ref, skip_ref, *rest):
    o_ref, z0_sc, z1_sc, x2_sc, a_sc, b_sc, ny0_sc, ny1_sc = rest[-8:]
    ph = pl.program_id(2)
    r = pl.program_id(3)
    W = ngrp * tc
    tcv = min(L, 512)
    halo = 16

    def alt_sign(start, n):
        t = start + lax.broadcasted_iota(jnp.int32, (n, 1), 0)
        return (1 - 2 * (t % 2)).astype(F32)

    def conv3(src_ref, g, a, w_ref, b_ref):
        x = src_ref[g, pl.ds(a, tcv), :].astype(F32)
        row = lax.broadcasted_iota(jnp.int32, (tcv, 1), 0)
        up_at = pl.multiple_of(jnp.maximum(a - halo, 0), halo)
        dn_at = pl.multiple_of(jnp.minimum(a + tcv, L - halo), halo)
        up = src_ref[g, pl.ds(up_at, halo), :][halo - 1:halo, :].astype(F32)
        dn = src_ref[g, pl.ds(dn_at, halo), :][0:1, :].astype(F32)
        up = jnp.where(a > 0, up, 0.0)
        dn = jnp.where(a + tcv < L, dn, 0.0)
        prev = jnp.where(row == 0, up, pltpu.roll(x, 1, 0))
        nxt = jnp.where(row == tcv - 1, dn, pltpu.roll(x, tcv - 1, 0))
        return prev * w_ref[0:1, :] + x * w_ref[1:2, :] + nxt * w_ref[2:3, :] + b_ref[...]

    @pl.when(jnp.logical_and(ph == 0, r == 0))
    def _():
        ny0_sc[...] = jnp.zeros_like(ny0_sc)

        def conv_tile(ti, carry):
            a = pl.multiple_of(ti * tcv, tcv)
            rows_a = pl.ds(a, tcv)
            for g in range(ngrp):
                cols = slice(g * tc, (g + 1) * tc)
                z1_sc[rows_a, cols] = conv3(x1_ref, g, a, sw1_ref, sb1_ref).astype(BF16)
                x2_sc[rows_a, cols] = conv3(x2_ref, g, a, sw2_ref, sb2_ref).astype(BF16)
                z0_sc[rows_a, cols] = conv3(v_ref, g, a, swv_ref, sbv_ref).astype(BF16)
            ny0_sc[...] += jnp.sum(alt_sign(a, tcv) * z0_sc[rows_a, :].astype(F32), axis=0, keepdims=True)
            return carry

        lax.fori_loop(0, L // tcv, conv_tile, 0)

    start = pl.multiple_of(r * tr, tr)
    rows = pl.ds(start, tr)
    tile = lambda ref: jnp.concatenate([ref[...]] * ngrp, axis=1)
    dot = functools.partial(jnp.dot, preferred_element_type=F32)

    def forward(z_sc):
        z = z_sc[...]
        zre = dot(c_ref[...], z)
        zim = dot(s_ref[...], z)
        pw, qw = tile(p_ref), tile(q_ref)
        a_sc[rows, :] = (zre * pw - zim * qw).astype(BF16)
        b_sc[rows, :] = (zim * pw + zre * qw).astype(BF16)

    def inverse(order, z_sc, ny_sc):
        y = dot(c_ref[...], a_sc[...]) + dot(s_ref[...], b_sc[...])
        kny = jnp.concatenate([kny_ref[order:order + 1, :]] * ngrp, axis=1)
        skip = jnp.concatenate([skip_ref[order:order + 1, :]] * ngrp, axis=1)
        return y + alt_sign(start, tr) * (ny_sc[...] * kny) + skip * z_sc[rows, :].astype(F32)

    @pl.when(ph == 0)
    def _():
        forward(z0_sc)

    @pl.when(ph == 1)
    def _():
        z1 = (z1_sc[rows, :].astype(F32) * inverse(0, z0_sc, ny0_sc)).astype(BF16)
        z1_sc[rows, :] = z1

        @pl.when(r == 0)
        def _():
            ny1_sc[...] = jnp.zeros_like(ny1_sc)

        ny1_sc[...] += jnp.sum(alt_sign(start, tr) * z1.astype(F32), axis=0, keepdims=True)

    @pl.when(ph == 2)
    def _():
        forward(z1_sc)

    @pl.when(ph == 3)
    def _():
        out = x2_sc[rows, :].astype(F32) * inverse(1, z1_sc, ny1_sc)
        for g in range(ngrp):
            o_ref[g, rows, :] = out[:, g * tc:(g + 1) * tc].astype(o_ref.dtype)


def _hy_core(proj, short_w, short_b, skip, cos, sin, pq, k_ny, o_prev, row0, nseq, L, ngrp, tc):
    T = proj.shape[0]
    tr = min(L, 256)
    nrt = L // tr
    assert row0 % (L * ngrp) == 0 and nseq % ngrp == 0 and T % L == 0
    sb0 = row0 // (L * ngrp)
    nct = D // tc
    p3 = proj.reshape(T // L, L, 3 * D)
    p_arr, q_arr = pq

    def xspec(part):
        return pl.BlockSpec((ngrp, L, tc), lambda i, j, ph, r: (sb0 + i, 0, part * nct + j),
                            pipeline_mode=pl.Buffered(1))

    def wspec(part, rows_):
        return pl.BlockSpec((rows_, tc), lambda i, j, ph, r: (0, part * nct + j))

    def pq_map(i, j, ph, r):
        return (ph // 2, jnp.where(ph % 2 == 0, r, nrt - 1), j)

    tspec = pl.BlockSpec((tr, L), lambda i, j, ph, r: (r, 0))
    in_specs = [xspec(0), xspec(1), xspec(2), wspec(0, 3), wspec(1, 3), wspec(2, 3),
                wspec(0, 1), wspec(1, 1), wspec(2, 1), tspec, tspec,
                pl.BlockSpec((None, tr, tc), pq_map), pl.BlockSpec((None, tr, tc), pq_map),
                pl.BlockSpec((2, tc), lambda i, j, ph, r: (0, j)),
                pl.BlockSpec((2, tc), lambda i, j, ph, r: (0, j))]
    sb = short_b.reshape(1, 3 * D)
    args = [p3, p3, p3, short_w, short_w, short_w, sb, sb, sb, cos, sin, p_arr, q_arr, k_ny, skip]
    aliases = {}
    if o_prev is not None:
        in_specs.append(pl.BlockSpec(memory_space=pl.ANY))
        args.append(o_prev.reshape(T // L, L, D))
        aliases = {len(args) - 1: 0}
    W = ngrp * tc
    out = pl.pallas_call(
        functools.partial(_hy_core_kernel, L, tr, ngrp, tc),
        out_shape=jax.ShapeDtypeStruct((T // L, L, D), BF16),
        grid=(nseq // ngrp, nct, 4, nrt),
        in_specs=in_specs,
        out_specs=pl.BlockSpec((ngrp, L, tc), lambda i, j, ph, r: (sb0 + i, 0, j)),
        scratch_shapes=[pltpu.VMEM((L, W), BF16)] * 5 + [pltpu.VMEM((1, W), F32)] * 2,
        input_output_aliases=aliases,
        compiler_params=_cp("arbitrary", "arbitrary", "arbitrary", "arbitrary"),
        name="hy_core",
    )(*args)
    return out.reshape(T, D)


def _plain_out_kernel(x_ref, z_ref, g_ref, w_ref, b_ref, o_ref):
    acc = jnp.dot(z_ref[...], w_ref[...], preferred_element_type=F32) + b_ref[...]
    o_ref[...] = x_ref[...] + g_ref[...] * acc


def _plain_out(lay, x, mods, z, w, b):
    tm = lay.tile(1024)
    tn = 512
    kdim = z.shape[1]
    return pl.pallas_call(
        _plain_out_kernel,
        out_shape=jax.ShapeDtypeStruct((lay.T, D), F32),
        grid=(lay.T // tm, D // tn),
        in_specs=[pl.BlockSpec((tm, tn), lambda i, j: (i, j)),
                  pl.BlockSpec((tm, kdim), lambda i, j: (i, 0)),
                  pl.BlockSpec((None, 1, tn), lambda i, j: (lay.group(i * tm) * MOD_CHUNKS + 2, 0, j)),
                  pl.BlockSpec((kdim, tn), lambda i, j: (0, j)),
                  pl.BlockSpec((1, tn), lambda i, j: (0, j))],
        out_specs=pl.BlockSpec((tm, tn), lambda i, j: (i, j)),
        compiler_params=_cp("arbitrary", "arbitrary"),
        name="plain_out",
    )(x, z, mods, w, b.reshape(1, D))


def _hyena_layer(lay, x, mods, p):
    proj = _proj(lay, x, mods, 0, 1, p['w_in'].astype(BF16), p['b_in'].reshape(1, 3 * D), 768)
    z = jnp.zeros((lay.T, D), BF16)
    for row0, nseq, L, ngrp, tc in ((0, lay.B, lay.L, math.gcd(lay.B, 4), 256),
                                    (lay.TP, lay.NS, lay.LS, lay.NS, 256)):
        k_lo, k_hi, k_ny = _hyena_filters(L, p)
        cos, sin = _dft_tables(L)
        pq = _hy_spectrum(L, cos, sin, k_lo, k_hi)
        z = _hy_core(proj, p['short_w'], p['short_b'], p['skip'], cos, sin, pq, k_ny, z, row0, nseq, L, ngrp, tc)
    return _plain_out(lay, x, mods, z, p['w_out'].astype(BF16), p['b_out'])


_NT = (((1,), (1,)), ((), ()))
_TN = (((0,), (0,)), ((), ()))


def _tri(dr):
    t = lax.broadcasted_iota(jnp.int32, (CHUNK, CHUNK), 0)
    s = lax.broadcasted_iota(jnp.int32, (CHUNK, CHUNK), 1)
    return (s <= t) if dr == 0 else (s >= t)


def _chunk_cumsum(g, dr):
    n = g.shape[0]
    pos = lax.broadcasted_iota(jnp.int32, g.shape, 0) % CHUNK
    sh = 1
    while sh < CHUNK:
        if dr == 0:
            g = g + jnp.where(pos >= sh, pltpu.roll(g, sh, 0), 0.0)
        else:
            g = g + jnp.where(pos < CHUNK - sh, pltpu.roll(g, n - sh, 0), 0.0)
        sh *= 2
    return g


def _head_epilogue(o_sc, gate_ref, ng_ref, a_ref, center):
    rows = o_sc.shape[0]
    tr = math.gcd(rows, 256)

    def tile(i, carry):
        r = pl.ds(pl.multiple_of(i * tr, tr), tr)
        o = o_sc[r, :]
        if center:
            o = o - jnp.mean(o, axis=-1, keepdims=True)
        o = o * lax.rsqrt(jnp.mean(o * o, axis=-1, keepdims=True) + RMS_EPS) * ng_ref[...]
        a_ref[r, :] = (o * _silu(gate_ref[r, :].astype(F32))).astype(a_ref.dtype)
        return carry

    lax.fori_loop(0, rows // tr, tile, 0)


def _gla_kernel(cps, nseg, U, has_s0, want_final, *refs):
    q_ref, k_ref, v_ref, lr_ref, w2f_ref, w2b_ref, gb_ref, gate_ref, ng_ref = refs[:9]
    s0_ref = refs[9] if has_s0 else None
    qin_sc, kin_sc, kout_sc, dec_sc, st_sc, s_sc, s0t_sc, o_ref = refs[-8:]
    outs = refs[-10:-8] if want_final else refs[-9:-8]
    a_ref = outs[0]
    sf_ref = outs[1] if want_final else None
    C = CHUNK
    nsc = cps // U
    nchunks = nseg * cps
    rows_total = nchunks * C
    w2 = (w2f_ref, w2b_ref)

    for dr in range(2):
        pre = jnp.dot(lr_ref[...], w2[dr][...], preferred_element_type=F32) + gb_ref[dr:dr + 1, :]
        g = (jnp.minimum(pre, 0.0) - jnp.log(1.0 + jnp.exp(-jnp.abs(pre)))) * (1.0 / GLA_TAU)
        b = _chunk_cumsum(g, dr)
        b3 = b.reshape(nchunks, C, GLA_DK)
        tot = b3[:, C - 1:C, :] if dr == 0 else b3[:, 0:1, :]
        dec_sc[...] = jnp.exp(tot).reshape(nchunks, GLA_DK)
        k = k_ref[...].astype(F32)
        qin_sc[...] = (q_ref[...].astype(F32) * (GLA_DK ** -0.5) * jnp.exp(b)).astype(BF16)
        kin_sc[...] = (k * jnp.exp(-b)).astype(BF16)
        kout_sc[...] = (k * jnp.exp(tot - b3).reshape(rows_total, GLA_DK)).astype(BF16)
        if has_s0:
            s0t_sc[...] = jnp.transpose(s0_ref[dr], (1, 0))
        tri = _tri(dr)

        def super_chunk(jj, carry, dr=dr, tri=tri):
            j = jj if dr == 0 else nseg * nsc - 1 - jj
            in_seg = j % nsc
            first = (in_seg == 0) if dr == 0 else (in_seg == nsc - 1)
            last = (in_seg == nsc - 1) if dr == 0 else (in_seg == 0)

            @pl.when(first)
            def _():
                s_sc[...] = s0t_sc[...] if has_s0 else jnp.zeros_like(s_sc)

            base = j * (U * C)
            for u in range(U):
                rows = pl.ds(pl.multiple_of(base + u * C, C), C)
                v = v_ref[rows, :]
                sc = lax.dot_general(qin_sc[rows, :], kin_sc[rows, :], _NT, preferred_element_type=F32)
                o = jnp.dot(jnp.where(tri, sc, 0.0).astype(BF16), v, preferred_element_type=F32)
                st_sc[u] = lax.dot_general(v, kout_sc[rows, :], _TN, preferred_element_type=F32)
                if dr == 0:
                    o_ref[rows, :] = o
                else:
                    o_ref[rows, :] += o
            s = s_sc[...]
            for u in (range(U) if dr == 0 else reversed(range(U))):
                kv = st_sc[u]
                st_sc[u] = s
                s = dec_sc[pl.ds(j * U + u, 1), :] * s + kv
            s_sc[...] = s
            for u in range(U):
                rows = pl.ds(pl.multiple_of(base + u * C, C), C)
                o_ref[rows, :] += lax.dot_general(qin_sc[rows, :], st_sc[u].astype(BF16), _NT,
                                                  preferred_element_type=F32)
            if want_final:
                @pl.when(last)
                def _():
                    sf_ref[j // nsc, dr] = jnp.transpose(s, (1, 0))
            return carry

        lax.fori_loop(0, nseg * nsc, super_chunk, 0)

    _head_epilogue(o_ref, gate_ref, ng_ref, a_ref, center=False)


def _gla_core(proj, w2f, w2b, gate_b, norm_g, s0, o_prev, row0, nseq, seqlen, nseg, want_final):
    T = proj.shape[0]
    rows = nseg * seqlen
    cps = seqlen // CHUNK
    U = math.gcd(cps, 8)
    assert row0 % rows == 0 and seqlen % CHUNK == 0 and nseq % nseg == 0
    rb = row0 // rows
    hk = GLA_H * GLA_DK
    has_s0 = s0 is not None
    assert not has_s0 or nseg == 1
    in_specs = [pl.BlockSpec((rows, GLA_DK), lambda b, h: (rb + b, h)),
                pl.BlockSpec((rows, GLA_DK), lambda b, h: (rb + b, GLA_H + h)),
                pl.BlockSpec((rows, GLA_DV), lambda b, h: (rb + b, 2 * hk // GLA_DV + h)),
                pl.BlockSpec((rows, 128), lambda b, h: (rb + b, (2 * hk + 2 * GLA_H * GLA_DV) // 128)),
                pl.BlockSpec((128, GLA_DK), lambda b, h: (0, h)),
                pl.BlockSpec((128, GLA_DK), lambda b, h: (0, h)),
                pl.BlockSpec((2, GLA_DK), lambda b, h: (0, h)),
                pl.BlockSpec((rows, GLA_DV), lambda b, h: (rb + b, (2 * hk) // GLA_DV + GLA_H + h)),
                pl.BlockSpec((1, GLA_DV), lambda b, h: (0, 0))]
    args = [proj, proj, proj, proj, w2f, w2b, gate_b, proj, norm_g.reshape(1, GLA_DV)]
    if has_s0:
        in_specs.append(pl.BlockSpec((None, 2, None, GLA_DK, GLA_DV), lambda b, h: (b, 0, h, 0, 0)))
        args.append(s0)
    in_specs.append(pl.BlockSpec(memory_space=pl.ANY))
    args.append(o_prev)
    aliases = {len(args) - 1: 0}
    out_shape = [jax.ShapeDtypeStruct((T, GLA_H * GLA_DV), BF16)]
    out_specs = [pl.BlockSpec((rows, GLA_DV), lambda b, h: (rb + b, h))]
    if want_final:
        out_shape.append(jax.ShapeDtypeStruct((nseq, 2, GLA_H, GLA_DK, GLA_DV), F32))
        out_specs.append(pl.BlockSpec((nseg, 2, None, GLA_DK, GLA_DV), lambda b, h: (b, 0, h, 0, 0)))
    outs = pl.pallas_call(
        functools.partial(_gla_kernel, cps, nseg, U, has_s0, want_final),
        out_shape=tuple(out_shape),
        grid=(nseq // nseg, GLA_H),
        in_specs=in_specs,
        out_specs=tuple(out_specs),
        scratch_shapes=[pltpu.VMEM((rows, GLA_DK), BF16)] * 3
        + [pltpu.VMEM((nseg * cps, GLA_DK), F32), pltpu.VMEM((U, GLA_DV, GLA_DK), F32),
           pltpu.VMEM((GLA_DV, GLA_DK), F32), pltpu.VMEM((GLA_DV, GLA_DK), F32),
           pltpu.VMEM((rows, GLA_DV), F32)],
        input_output_aliases=aliases,
        compiler_params=_cp("arbitrary", "arbitrary"),
        name="gla",
    )(*args)
    return (outs[0], outs[1]) if want_final else (outs[0], None)


def _ret_kernel(cps, nseg, U, has_s0, want_final, rope, *refs):
    q_ref, k_ref, v_ref, dm_ref, qd_ref, kd_ref, cd_ref, gate_ref, ng_ref = refs[:9]
    nxt = 9
    if rope:
        cos_ref, sin_ref = refs[9:11]
        nxt = 11
    s0_ref = refs[nxt] if has_s0 else None
    qr_sc, kr_sc, qd_sc, kd_sc, st_sc, s_sc, o_ref = refs[-7:]
    outs = refs[-9:-7] if want_final else refs[-8:-7]
    a_ref = outs[0]
    sf_ref = outs[1] if want_final else None
    C = RET_CHUNK
    nsc = cps // U
    R = U * C
    SB = 64

    def rot(x, rows):
        if not rope:
            return x
        half = x.shape[1] // 2
        swapped = jnp.concatenate([pltpu.roll(x[:, :half], half // 2, 1),
                                   pltpu.roll(x[:, half:], half // 2, 1)], axis=1)
        return x * cos_ref[rows, :] + swapped * sin_ref[rows, :]

    for dr in range(2):
        def super_chunk(jj, carry, dr=dr):
            j = jj if dr == 0 else nseg * nsc - 1 - jj
            in_seg = j % nsc
            first = (in_seg == 0) if dr == 0 else (in_seg == nsc - 1)
            last = (in_seg == nsc - 1) if dr == 0 else (in_seg == 0)

            @pl.when(first)
            def _():
                s_sc[...] = s0_ref[dr] if has_s0 else jnp.zeros_like(s_sc)

            base = pl.multiple_of(j * R, R)
            rows_r = pl.ds(base, R)
            q = rot(q_ref[rows_r, :].astype(F32), rows_r)
            k = rot(k_ref[rows_r, :].astype(F32), rows_r) * (RET_DK ** -0.5)
            qr_sc[...] = q.astype(BF16)
            kr_sc[...] = k.astype(BF16)
            qd_sc[...] = (q.reshape(U, C, RET_DK) * qd_ref[dr][None]).reshape(R, RET_DK).astype(BF16)
            kd_sc[...] = (k.reshape(U, C, RET_DK) * kd_ref[dr][None]).reshape(R, RET_DK).astype(BF16)
            for u in range(U):
                loc = pl.ds(u * C, C)
                rows = pl.ds(pl.multiple_of(base + u * C, C), C)
                v = v_ref[rows, :]
                sc = lax.dot_general(qr_sc[loc, :], kr_sc[loc, :], _NT, preferred_element_type=F32)
                o = jnp.dot((sc * dm_ref[dr]).astype(BF16), v, preferred_element_type=F32)
                st_sc[u] = lax.dot_general(kd_sc[loc, :], v, _TN, preferred_element_type=F32)
                if dr == 0:
                    o_ref[rows, :] = o
                else:
                    o_ref[rows, :] += o
            cd = cd_ref[dr]
            for r0 in range(0, RET_DK, SB):
                srows = pl.ds(r0, SB)
                s = s_sc[srows, :]
                for u in (range(U) if dr == 0 else reversed(range(U))):
                    kv = st_sc[u, srows, :]
                    st_sc[u, srows, :] = s
                    s = cd * s + kv
                s_sc[srows, :] = s
            for u in range(U):
                rows = pl.ds(pl.multiple_of(base + u * C, C), C)
                o_ref[rows, :] += jnp.dot(qd_sc[pl.ds(u * C, C), :], st_sc[u].astype(BF16),
                                          preferred_element_type=F32)
            if want_final:
                @pl.when(last)
                def _():
                    sf_ref[j // nsc, dr] = s_sc[...]
            return carry

        lax.fori_loop(0, nseg * nsc, super_chunk, 0)

    _head_epilogue(o_ref, gate_ref, ng_ref, a_ref, center=True)


def _ret_tables(log_decay):
    C = RET_CHUNK
    lg = log_decay.astype(F32)[:, :, None, None]
    t = jnp.arange(C, dtype=F32)[:, None]
    s = jnp.arange(C, dtype=F32)[None, :]
    lag = jnp.stack([t - s, s - t])[:, None]
    dmask = jnp.where(lag >= 0, jnp.exp(jnp.maximum(lag, 0.0) * lg), 0.0)
    tl = jnp.arange(C, dtype=F32)[None, None, :, None]
    qdec = jnp.concatenate([jnp.exp((tl + 1.0) * lg[0:1]), jnp.exp((C - tl) * lg[1:2])], axis=0)
    kdec = jnp.concatenate([jnp.exp((C - 1.0 - tl) * lg[0:1]), jnp.exp(tl * lg[1:2])], axis=0)
    cdec = jnp.exp(C * lg)
    return dmask, qdec, kdec, cdec


def _rope_tables(seqlen, dk):
    half = dk // 2
    nf = half // 2
    pos = jnp.arange(seqlen, dtype=jnp.int32)
    inv = ROPE_BASE ** (-jnp.arange(nf, dtype=F32) / nf)
    ang_r = (pos // GRID_W).astype(F32)[:, None] * inv[None, :]
    ang_c = (pos % GRID_W).astype(F32)[:, None] * inv[None, :]
    cos = jnp.concatenate([jnp.cos(ang_r)] * 2 + [jnp.cos(ang_c)] * 2, axis=1)
    sin = jnp.concatenate([-jnp.sin(ang_r), jnp.sin(ang_r), -jnp.sin(ang_c), jnp.sin(ang_c)], axis=1)
    return cos, sin


def _ret_core(proj, tabs, norm_g, s0, o_prev, row0, nseq, seqlen, nseg, want_final, rope):
    T = proj.shape[0]
    rows = nseg * seqlen
    C = RET_CHUNK
    cps = seqlen // C
    U = math.gcd(cps, 4)
    assert row0 % rows == 0 and seqlen % C == 0 and nseq % nseg == 0
    rb = row0 // rows
    hk, hv = RET_H * RET_DK, RET_H * RET_DV
    has_s0 = s0 is not None
    assert not (has_s0 or rope) or nseg == 1
    tspec = lambda r, c: pl.BlockSpec((2, None, r, c), lambda b, h: (0, h, 0, 0))
    mode = dict(pipeline_mode=pl.Buffered(1)) if rows * RET_DV * 2 >= (4 << 20) else {}
    in_specs = [pl.BlockSpec((rows, RET_DK), lambda b, h: (rb + b, h), **mode),
                pl.BlockSpec((rows, RET_DK), lambda b, h: (rb + b, RET_H + h), **mode),
                pl.BlockSpec((rows, RET_DV), lambda b, h: (rb + b, 2 * hk // RET_DV + h), **mode),
                tspec(C, C), tspec(C, 1), tspec(C, 1), tspec(1, 1),
                pl.BlockSpec((rows, RET_DV), lambda b, h: (rb + b, (2 * hk + hv) // RET_DV + h), **mode),
                pl.BlockSpec((1, RET_DV), lambda b, h: (0, 0))]
    args = [proj, proj, proj, *tabs, proj, norm_g.reshape(1, RET_DV)]
    if rope:
        cos, sin = _rope_tables(seqlen, RET_DK)
        in_specs += [pl.BlockSpec((seqlen, RET_DK), lambda b, h: (0, 0), pipeline_mode=pl.Buffered(1))] * 2
        args += [cos, sin]
    if has_s0:
        in_specs.append(pl.BlockSpec((None, 2, None, RET_DK, RET_DV), lambda b, h: (b, 0, h, 0, 0)))
        args.append(s0)
    in_specs.append(pl.BlockSpec(memory_space=pl.ANY))
    args.append(o_prev)
    aliases = {len(args) - 1: 0}
    out_shape = [jax.ShapeDtypeStruct((T, hv), BF16)]
    out_specs = [pl.BlockSpec((rows, RET_DV), lambda b, h: (rb + b, h))]
    if want_final:
        out_shape.append(jax.ShapeDtypeStruct((nseq, 2, RET_H, RET_DK, RET_DV), F32))
        out_specs.append(pl.BlockSpec((nseg, 2, None, RET_DK, RET_DV), lambda b, h: (b, 0, h, 0, 0)))
    outs = pl.pallas_call(
        functools.partial(_ret_kernel, cps, nseg, U, has_s0, want_final, rope),
        out_shape=tuple(out_shape),
        grid=(nseq // nseg, RET_H),
        in_specs=in_specs,
        out_specs=tuple(out_specs),
        scratch_shapes=[pltpu.VMEM((U * C, RET_DK), BF16)] * 4
        + [pltpu.VMEM((U, RET_DK, RET_DV), F32), pltpu.VMEM((RET_DK, RET_DV), F32),
           pltpu.VMEM((rows, RET_DV), F32)],
        input_output_aliases=aliases,
        compiler_params=_cp("arbitrary", "arbitrary"),
        name="ret",
    )(*args)
    return (outs[0], outs[1]) if want_final else (outs[0], None)


def _gla_layer(lay, x, mods, p, s0):
    hk, hv = GLA_H * GLA_DK, GLA_H * GLA_DV
    w_all = jnp.concatenate([p['w_in'], p['gate_w1'][0], p['gate_w1'][1],
                             jnp.zeros((D, 128 - 2 * GLA_RANK), F32)], axis=1).astype(BF16)
    proj = _proj(lay, x, mods, 0, 1, w_all, jnp.zeros((1, w_all.shape[1]), F32), 640)
    pad = lambda w, lo: jnp.pad(w, ((lo, 128 - GLA_RANK - lo), (0, 0))).astype(BF16)
    w2f, w2b = pad(p['gate_w2'][0], 0), pad(p['gate_w2'][1], GLA_RANK)
    a = jnp.zeros((lay.T, hv), BF16)
    a, s_fin = _gla_core(proj, w2f, w2b, p['gate_b'], p['norm_g'], None, a, 0, lay.B, lay.L,
                         math.gcd(lay.B, 8), True)
    a, _ = _gla_core(proj, w2f, w2b, p['gate_b'], p['norm_g'], s0, a, lay.TP, lay.NS, lay.LS, 1, False)
    x = _plain_out(lay, x, mods, a, p['w_out'].astype(BF16), jnp.zeros((D,), F32))
    return x, s_fin


def _ret_layer(lay, x, mods, p, s0):
    hk, hv = RET_H * RET_DK, RET_H * RET_DV
    proj = _proj(lay, x, mods, 0, 1, p['w_in'].astype(BF16), jnp.zeros((1, 2 * hk + 2 * hv), F32), 768)
    tabs = _ret_tables(p['log_decay'])
    a = jnp.zeros((lay.T, hv), BF16)
    a, s_fin = _ret_core(proj, tabs, p['norm_g'], None, a, 0, lay.B, lay.L, math.gcd(lay.B, 8), True, False)
    a, _ = _ret_core(proj, tabs, p['norm_g'], s0, a, lay.TP, lay.NS, lay.LS, 1, False, True)
    x = _plain_out(lay, x, mods, a, p['w_out'].astype(BF16), jnp.zeros((D,), F32))
    return x, s_fin


MOE_BM = 512
EXPERT_TF = 1792
ROUTER_LANES = 128
DMA_UNROLL = 8


def _router_kernel(x_ref, sh_ref, sc_ref, rw_ref, h_ref, idx_ref, gate_ref):
    h = _modulate(x_ref[...], sh_ref[...], sc_ref[...])
    h_ref[...] = h
    logits = jnp.dot(h, rw_ref[...], precision=HIGHEST, preferred_element_type=F32)
    lane = lax.broadcasted_iota(jnp.int32, logits.shape, 1)
    neg = jnp.float32(-jnp.inf)
    logits = jnp.where(lane < N_EXPERTS, logits, neg)
    m1 = jnp.max(logits, axis=-1, keepdims=True)
    i1 = jnp.min(jnp.where(logits == m1, lane, ROUTER_LANES), axis=-1, keepdims=True)
    rest = jnp.where(lane == i1, neg, logits)
    m2 = jnp.max(rest, axis=-1, keepdims=True)
    i2 = jnp.min(jnp.where(rest == m2, lane, ROUTER_LANES), axis=-1, keepdims=True)
    e2 = jnp.exp(m2 - m1)
    g1 = 1.0 / (1.0 + e2)
    idx_ref[:, 0:1] = i1
    idx_ref[:, 1:2] = i2
    gate_ref[:, 0:1] = g1
    gate_ref[:, 1:2] = e2 * g1


def _router(lay, x, mods, router_w):
    tm = lay.tile(512)
    rw = jnp.pad(router_w, ((0, 0), (0, ROUTER_LANES - N_EXPERTS)))
    return pl.pallas_call(
        _router_kernel,
        out_shape=(jax.ShapeDtypeStruct((lay.T, D), F32),
                   jax.ShapeDtypeStruct((lay.T, 2), jnp.int32),
                   jax.ShapeDtypeStruct((lay.T, 2), F32)),
        grid=(lay.T // tm,),
        in_specs=[pl.BlockSpec((tm, D), lambda i: (i, 0)),
                  _mod_spec(lay, tm, 3, 1), _mod_spec(lay, tm, 4, 1),
                  pl.BlockSpec((D, ROUTER_LANES), lambda i: (0, 0))],
        out_specs=(pl.BlockSpec((tm, D), lambda i: (i, 0)),
                   pl.BlockSpec((tm, 2), lambda i: (i, 0)),
                   pl.BlockSpec((tm, 2), lambda i: (i, 0))),
        compiler_params=_cp("arbitrary"),
        name="router",
    )(x, mods, mods, rw)


def _moe_plan(idx, bm):
    a = idx.size
    e = idx.reshape(a)
    onehot = (e[:, None] == jnp.arange(N_EXPERTS, dtype=jnp.int32)[None, :]).astype(jnp.int32)
    csum = jnp.cumsum(onehot, axis=0)
    counts = csum[-1]
    rank = jnp.sum((csum - onehot) * onehot, axis=-1)
    padded = (counts + bm - 1) // bm * bm
    pad_end = jnp.cumsum(padded)
    dest = ((pad_end - padded)[e] + rank).astype(jnp.int32)
    nb = -(-(a + N_EXPERTS * (bm - 1)) // bm)
    block_start = jnp.arange(nb, dtype=jnp.int32) * bm
    block_e = jnp.minimum(jnp.searchsorted(pad_end, block_start, side='right'), N_EXPERTS - 1).astype(jnp.int32)
    nvalid = (pad_end[-1] // bm).astype(jnp.int32).reshape(1)
    return dest, block_e, nvalid, nb


def _dispatch_kernel(tm, dest_ref, h_ref, xs_in, xs_hbm, sem):
    del xs_in
    i = pl.program_id(0)

    def row_copy(r, dst):
        return pltpu.make_async_copy(h_ref.at[pl.ds(r, 1)], xs_hbm.at[pl.ds(dst, 1)], sem)

    def issue(r, carry):
        a = 2 * (i * tm + r)
        row_copy(r, dest_ref[a]).start()
        row_copy(r, dest_ref[a + 1]).start()
        return carry

    def drain(r, carry):
        row_copy(r, 0).wait()
        row_copy(r, 0).wait()
        return carry

    lax.fori_loop(0, tm, issue, 0, unroll=DMA_UNROLL)
    lax.fori_loop(0, tm, drain, 0, unroll=DMA_UNROLL)


def _dispatch(lay, h, dest, nb, bm):
    tm = lay.tile(512)
    xs = jnp.zeros((nb * bm, D), F32)
    grid_spec = pltpu.PrefetchScalarGridSpec(
        num_scalar_prefetch=1,
        grid=(lay.T // tm,),
        in_specs=[pl.BlockSpec((tm, D), lambda i, d: (i, 0)), pl.BlockSpec(memory_space=pl.ANY)],
        out_specs=pl.BlockSpec(memory_space=pl.ANY),
        scratch_shapes=[pltpu.SemaphoreType.DMA(())],
    )
    return pl.pallas_call(
        functools.partial(_dispatch_kernel, tm),
        out_shape=jax.ShapeDtypeStruct(xs.shape, F32),
        grid_spec=grid_spec,
        input_output_aliases={2: 0},
        compiler_params=_cp("arbitrary"),
        name="dispatch",
    )(dest, h, xs)


def _experts_kernel(nf, be_ref, nv_ref, xs_ref, wa_ref, wb_ref, wo_ref, o_ref, xb_sc, acc_sc):
    i = pl.program_id(0)
    f = pl.program_id(1)
    valid = i < nv_ref[0]

    @pl.when(jnp.logical_and(valid, f == 0))
    def _():
        xb_sc[...] = xs_ref[...].astype(BF16)

    @pl.when(valid)
    def _():
        xb = xb_sc[...]
        a = jnp.dot(xb, wa_ref[...], preferred_element_type=F32)
        b = jnp.dot(xb, wb_ref[...], preferred_element_type=F32)
        h = (_silu(a) * b).astype(BF16)
        y = jnp.dot(h, wo_ref[...], preferred_element_type=F32)

        @pl.when(f == 0)
        def _():
            acc_sc[...] = y

        @pl.when(f > 0)
        def _():
            acc_sc[...] += y

    @pl.when(f == nf - 1)
    def _():
        o_ref[...] = jnp.where(valid, acc_sc[...], 0.0)


def _experts(xs, block_e, nvalid, nb, bm, w_in, w_out):
    tf = EXPERT_TF
    nf = EXPERT_DIM // tf

    def wmap(off):
        def imap(i, f, be, nv):
            fe = jnp.where(i < nv[0], f, nf - 1)
            return (be[i], 0, off + fe)
        return imap

    def womap(i, f, be, nv):
        fe = jnp.where(i < nv[0], f, nf - 1)
        return (be[i], fe, 0)

    grid_spec = pltpu.PrefetchScalarGridSpec(
        num_scalar_prefetch=2,
        grid=(nb, nf),
        in_specs=[pl.BlockSpec((bm, D), lambda i, f, be, nv: (jnp.minimum(i, nv[0] - 1), 0)),
                  pl.BlockSpec((None, D, tf), wmap(0)),
                  pl.BlockSpec((None, D, tf), wmap(nf)),
                  pl.BlockSpec((None, tf, D), womap)],
        out_specs=pl.BlockSpec((bm, D), lambda i, f, be, nv: (i, 0)),
        scratch_shapes=[pltpu.VMEM((bm, D), BF16), pltpu.VMEM((bm, D), F32)],
    )
    return pl.pallas_call(
        functools.partial(_experts_kernel, nf),
        out_shape=jax.ShapeDtypeStruct((nb * bm, D), F32),
        grid_spec=grid_spec,
        compiler_params=_cp("arbitrary", "arbitrary"),
        name="experts",
    )(block_e, nvalid, xs, w_in, w_in, w_out)


def _combine_kernel(tm, nt, final, dest_ref, x_ref, gate_ref, g_ref, fg_ref, ys_hbm, o_ref, y_sc, sem):
    i = pl.program_id(0)
    slot = i % 2

    def row_copy(s, k, r, src):
        return pltpu.make_async_copy(ys_hbm.at[pl.ds(src, 1)], y_sc.at[s, k, pl.ds(r, 1)], sem.at[s])

    def issue_tile(t, s):
        def issue(r, carry):
            a = 2 * (t * tm + r)
            row_copy(s, 0, r, dest_ref[a]).start()
            row_copy(s, 1, r, dest_ref[a + 1]).start()
            return carry
        lax.fori_loop(0, tm, issue, 0, unroll=DMA_UNROLL)

    @pl.when(i == 0)
    def _():
        issue_tile(0, 0)

    @pl.when(i + 1 < nt)
    def _():
        issue_tile(i + 1, 1 - slot)

    def drain(r, carry):
        row_copy(slot, 0, r, 0).wait()
        row_copy(slot, 1, r, 0).wait()
        return carry

    lax.fori_loop(0, tm, drain, 0, unroll=DMA_UNROLL)
    gate = gate_ref[...]
    out = x_ref[...] + g_ref[...] * (gate[:, 0:1] * y_sc[slot, 0] + gate[:, 1:2] * y_sc[slot, 1])
    if final:
        ms = jnp.mean(out * out, axis=-1, keepdims=True)
        out = out * lax.rsqrt(ms + RMS_EPS) * fg_ref[...]
    o_ref[...] = out


def _combine(lay, x, mods, gates, ys, dest, final_g):
    tm = lay.tile(256)
    nt = lay.T // tm
    final = final_g is not None
    fg = (final_g if final else jnp.ones((D,), F32)).reshape(1, D)
    grid_spec = pltpu.PrefetchScalarGridSpec(
        num_scalar_prefetch=1,
        grid=(nt,),
        in_specs=[pl.BlockSpec((tm, D), lambda i, d: (i, 0)),
                  pl.BlockSpec((tm, 2), lambda i, d: (i, 0)),
                  pl.BlockSpec((None, 1, D), lambda i, d: (lay.group(i * tm) * MOD_CHUNKS + 5, 0, 0)),
                  pl.BlockSpec((1, D), lambda i, d: (0, 0)),
                  pl.BlockSpec(memory_space=pl.ANY)],
        out_specs=pl.BlockSpec((tm, D), lambda i, d: (i, 0)),
        scratch_shapes=[pltpu.VMEM((2, 2, tm, D), F32), pltpu.SemaphoreType.DMA((2,))],
    )
    return pl.pallas_call(
        functools.partial(_combine_kernel, tm, nt, final),
        out_shape=jax.ShapeDtypeStruct((lay.T, D), F32),
        grid_spec=grid_spec,
        compiler_params=_cp("arbitrary"),
        name="combine",
    )(dest, x, gates, mods, fg, ys)


def _moe_layer(lay, x, mods, router_w, w_in, w_out, final_g=None, bm=MOE_BM):
    h, idx, gates = _router(lay, x, mods, router_w)
    dest, block_e, nvalid, nb = _moe_plan(idx, bm)
    xs = _dispatch(lay, h, dest, nb, bm)
    ys = _experts(xs, block_e, nvalid, nb, bm, w_in, w_out)
    return _combine(lay, x, mods, gates, ys, dest, final_g)


def kernel(x_prompt, x_sample, c, state_l0_s5_re, state_l0_s5_im, state_l2_gla, state_l3_ret, c_ctx, l0_mod_w, l0_mod_b, l0_s5_a_re, l0_s5_a_im, l0_s5_log_dt, l0_s5_b_re, l0_s5_b_im, l0_s5_c_re, l0_s5_c_im, l0_s5_d, l0_s5_glu_w, l0_ffn_w_in, l0_ffn_w_out, l1_mod_w, l1_mod_b, l1_hy_w_in, l1_hy_b_in, l1_hy_short_w, l1_hy_short_b, l1_hy_f_w1, l1_hy_f_b1, l1_hy_f_w2, l1_hy_f_b2, l1_hy_f_w3, l1_hy_f_freq, l1_hy_skip, l1_hy_w_out, l1_hy_b_out, l1_moe_router, l1_moe_w_in, l1_moe_w_out, l2_mod_w, l2_mod_b, l2_gla_w_in, l2_gla_gate_w1, l2_gla_gate_w2, l2_gla_gate_b, l2_gla_norm_g, l2_gla_w_out, l2_ffn_w_in, l2_ffn_w_out, l3_mod_w, l3_mod_b, l3_ret_w_in, l3_ret_log_decay, l3_ret_norm_g, l3_ret_w_out, l3_moe_router, l3_moe_w_in, l3_moe_w_out, final_norm_g):
    B, L, _ = x_prompt.shape
    NS, LS, _ = x_sample.shape
    lay = Layout(B, L, NS, LS)
    x = jnp.concatenate([x_prompt.reshape(B * L, D), x_sample.reshape(NS * LS, D)], axis=0)
    cond = jnp.concatenate([c_ctx[None], c, jnp.zeros((8 - 1 - NS, D), F32)], axis=0)
    mods0 = _mods(cond, l0_mod_w, l0_mod_b)
    p0 = dict(a_re=l0_s5_a_re, a_im=l0_s5_a_im, log_dt=l0_s5_log_dt, b_re=l0_s5_b_re, b_im=l0_s5_b_im,
              c_re=l0_s5_c_re, c_im=l0_s5_c_im, d=l0_s5_d, glu_w=l0_s5_glu_w.astype(BF16))
    x, s5_re, s5_im = _s5_layer(lay, x, mods0, p0, state_l0_s5_re, state_l0_s5_im)
    x = _ffn(lay, x, mods0, l0_ffn_w_in.astype(BF16), l0_ffn_w_out.astype(BF16))

    mods1 = _mods(cond, l1_mod_w, l1_mod_b)
    p1 = dict(w_in=l1_hy_w_in, b_in=l1_hy_b_in, short_w=l1_hy_short_w, short_b=l1_hy_short_b,
              f_w1=l1_hy_f_w1, f_b1=l1_hy_f_b1, f_w2=l1_hy_f_w2, f_b2=l1_hy_f_b2, f_w3=l1_hy_f_w3,
              f_freq=l1_hy_f_freq, skip=l1_hy_skip, w_out=l1_hy_w_out, b_out=l1_hy_b_out)
    x = _hyena_layer(lay, x, mods1, p1)
    x = _moe_layer(lay, x, mods1, l1_moe_router, l1_moe_w_in.astype(BF16), l1_moe_w_out.astype(BF16))

    mods2 = _mods(cond, l2_mod_w, l2_mod_b)
    p2 = dict(w_in=l2_gla_w_in, gate_w1=l2_gla_gate_w1, gate_w2=l2_gla_gate_w2, gate_b=l2_gla_gate_b,
              norm_g=l2_gla_norm_g, w_out=l2_gla_w_out)
    x, gla_state = _gla_layer(lay, x, mods2, p2, state_l2_gla)
    x = _ffn(lay, x, mods2, l2_ffn_w_in.astype(BF16), l2_ffn_w_out.astype(BF16))

    mods3 = _mods(cond, l3_mod_w, l3_mod_b)
    p3 = dict(w_in=l3_ret_w_in, log_decay=l3_ret_log_decay, norm_g=l3_ret_norm_g, w_out=l3_ret_w_out)
    x, ret_state = _ret_layer(lay, x, mods3, p3, state_l3_ret)
    y = _moe_layer(lay, x, mods3, l3_moe_router, l3_moe_w_in.astype(BF16), l3_moe_w_out.astype(BF16),
                   final_g=final_norm_g)
    return (y[:lay.TP].reshape(B, L, D), y[lay.TP:].reshape(NS, LS, D), s5_re, s5_im, gla_state, ret_state)
```

```python
import functools
import math

import jax
import jax.numpy as jnp
import numpy as np
from jax import lax
from jax.experimental import pallas as pl
from jax.experimental.pallas import tpu as pltpu

F32 = jnp.float32
BF16 = jnp.bfloat16
HIGHEST = lax.Precision.HIGHEST

D = 1024
RMS_EPS = 1e-6
MOD_CHUNKS = 6
GRID_W = 64

S5_Q = 16
S5_G = D // S5_Q
S5_P = 64
S5_T = 16
S5_SCAN_ROWS = 64

HY_BANDS = 16
HY_TARGET = 1e-2
HY_FAST_PCT = 0.3
HY_SLOW_PCT = 1.5

GLA_H, GLA_DK, GLA_DV = 4, 128, 256
GLA_RANK = 16
GLA_TAU = 16.0
RET_H, RET_DK, RET_DV = 4, 256, 512
CHUNK = 64
RET_CHUNK = 256
ROPE_BASE = 10000.0

FFN_DIM = 2816
N_EXPERTS = 8
EXPERT_DIM = 3584

VMEM_LIMIT_V7X = 56 * 1024 * 1024


def _cp(*sem):
    return pltpu.CompilerParams(dimension_semantics=sem, vmem_limit_bytes=VMEM_LIMIT_V7X)


def _silu(x):
    return x * jax.nn.sigmoid(x)


def _modulate(x, shift, scale):
    ms = jnp.mean(x * x, axis=-1, keepdims=True)
    return x * lax.rsqrt(ms + RMS_EPS) * (1.0 + scale) + shift


class Layout:
    def __init__(self, n_prompt, l_prompt, n_sample, l_sample):
        self.B, self.L, self.NS, self.LS = n_prompt, l_prompt, n_sample, l_sample
        self.TP = n_prompt * l_prompt
        self.T = self.TP + n_sample * l_sample

    def tile(self, want):
        t = math.gcd(math.gcd(self.TP, self.LS), want)
        assert t % 8 == 0
        return t

    def group(self, row):
        return jnp.where(row < self.TP, 0, 1 + (row - self.TP) // self.LS)


def _mod_spec(lay, tm, chunk, ngrid):
    def imap(*ids):
        return (lay.group(ids[0] * tm) * MOD_CHUNKS + chunk, 0, 0)
    del ngrid
    return pl.BlockSpec((None, 1, D), imap)


def _mods_kernel(c_ref, w_ref, b_ref, o_ref):
    o_ref[...] = jnp.dot(_silu(c_ref[...]), w_ref[...], precision=HIGHEST,
                         preferred_element_type=F32) + b_ref[...]


def _mods(cond, w, b):
    n = MOD_CHUNKS * D
    tn = 1536
    out = pl.pallas_call(
        _mods_kernel,
        out_shape=jax.ShapeDtypeStruct((8, n), F32),
        grid=(n // tn,),
        in_specs=[pl.BlockSpec((8, D), lambda j: (0, 0)),
                  pl.BlockSpec((D, tn), lambda j: (0, j)),
                  pl.BlockSpec((1, tn), lambda j: (0, j))],
        out_specs=pl.BlockSpec((8, tn), lambda j: (0, j)),
        compiler_params=_cp("arbitrary"),
        name="mods",
    )(cond, w, b.reshape(1, n))
    return out.reshape(8 * MOD_CHUNKS, 1, D)


def _modulate_kernel(x_ref, sh_ref, sc_ref, o_ref):
    o_ref[...] = _modulate(x_ref[...], sh_ref[...], sc_ref[...]).astype(o_ref.dtype)


def _modulate_call(lay, x, mods, c_shift, c_scale, dtype):
    tm = lay.tile(512)
    return pl.pallas_call(
        _modulate_kernel,
        out_shape=jax.ShapeDtypeStruct((lay.T, D), dtype),
        grid=(lay.T // tm,),
        in_specs=[pl.BlockSpec((tm, D), lambda i: (i, 0)),
                  _mod_spec(lay, tm, c_shift, 1), _mod_spec(lay, tm, c_scale, 1)],
        out_specs=pl.BlockSpec((tm, D), lambda i: (i, 0)),
        compiler_params=_cp("arbitrary"),
        name="modulate",
    )(x, mods, mods)


def _proj_kernel(x_ref, sh_ref, sc_ref, w_ref, b_ref, o_ref, u_sc):
    @pl.when(pl.program_id(1) == 0)
    def _():
        u_sc[...] = _modulate(x_ref[...], sh_ref[...], sc_ref[...]).astype(BF16)

    acc = jnp.dot(u_sc[...], w_ref[...], preferred_element_type=F32) + b_ref[...]
    o_ref[...] = acc.astype(o_ref.dtype)


def _proj(lay, x, mods, c_shift, c_scale, w, b, tn, out_dtype=BF16):
    tm = lay.tile(1024)
    n = w.shape[1]
    assert n % tn == 0
    return pl.pallas_call(
        _proj_kernel,
        out_shape=jax.ShapeDtypeStruct((lay.T, n), out_dtype),
        grid=(lay.T // tm, n // tn),
        in_specs=[pl.BlockSpec((tm, D), lambda i, j: (i, 0)),
                  _mod_spec(lay, tm, c_shift, 2), _mod_spec(lay, tm, c_scale, 2),
                  pl.BlockSpec((D, tn), lambda i, j: (0, j)),
                  pl.BlockSpec((1, tn), lambda i, j: (0, j))],
        out_specs=pl.BlockSpec((tm, tn), lambda i, j: (i, j)),
        scratch_shapes=[pltpu.VMEM((tm, D), BF16)],
        compiler_params=_cp("arbitrary", "arbitrary"),
        name="proj",
    )(x, mods, mods, w, b)


def _ffn_kernel(x_ref, sh_ref, sc_ref, g_ref, wa_ref, wb_ref, wo_ref, o_ref):
    x = x_ref[...]
    u = _modulate(x, sh_ref[...], sc_ref[...]).astype(BF16)
    a = jnp.dot(u, wa_ref[...], preferred_element_type=F32)
    b = jnp.dot(u, wb_ref[...], preferred_element_type=F32)
    h = (_silu(a) * b).astype(BF16)
    o_ref[...] = x + g_ref[...] * jnp.dot(h, wo_ref[...], preferred_element_type=F32)


def _ffn(lay, x, mods, w_in, w_out):
    tm = lay.tile(512)
    once = dict(pipeline_mode=pl.Buffered(1))
    return pl.pallas_call(
        _ffn_kernel,
        out_shape=jax.ShapeDtypeStruct((lay.T, D), F32),
        grid=(lay.T // tm,),
        in_specs=[pl.BlockSpec((tm, D), lambda i: (i, 0)),
                  _mod_spec(lay, tm, 3, 1), _mod_spec(lay, tm, 4, 1), _mod_spec(lay, tm, 5, 1),
                  pl.BlockSpec((D, FFN_DIM), lambda i: (0, 0), **once),
                  pl.BlockSpec((D, FFN_DIM), lambda i: (0, 1), **once),
                  pl.BlockSpec((FFN_DIM, D), lambda i: (0, 0), **once)],
        out_specs=pl.BlockSpec((tm, D), lambda i: (i, 0)),
        compiler_params=_cp("arbitrary"),
        name="ffn",
    )(x, mods, mods, mods, w_in, w_in, w_out)


def _s5_tables(a_re, a_im, log_dt, b_re, b_im, c_re, c_im, d_skip):
    T, G, P, Q = S5_T, S5_G, S5_P, S5_Q
    a = lax.complex(a_re, a_im)
    adt = a * jnp.exp(log_dt)[..., None]
    lam = jnp.exp(adt)
    bb = ((lam - 1.0) / a)[..., None] * lax.complex(b_re, b_im)
    cm = lax.complex(c_re, c_im)
    steps = jnp.arange(T + 1, dtype=F32)
    pw = jnp.exp(steps[None, :, None, None] * adt[:, None])
    kern = jnp.real(jnp.einsum('dgqp,djgp,dgpr->djgqr', cm, pw[:, :T], bb))
    s_i = jnp.arange(T)[:, None]
    t_i = jnp.arange(T)[None, :]
    kf = kern[0][jnp.clip(t_i - s_i, 0, T - 1)] * (t_i >= s_i)[..., None, None, None]
    kb = kern[1][jnp.clip(s_i - t_i, 0, T - 1)] * (s_i >= t_i)[..., None, None, None]
    m = jnp.transpose(kf + kb, (2, 0, 4, 1, 3))
    eye = (jnp.eye(T)[:, None, :, None] * jnp.eye(Q)[None, :, None, :])
    m = m + eye[None] * d_skip.reshape(G, 1, 1, 1, Q)
    m = m.reshape(G, T * Q, T * Q)
    e_f = pw[0][T - 1 - jnp.arange(T)]
    e_b = pw[1][jnp.arange(T)]
    n_f = e_f[..., None] * bb[0][None]
    n_b = e_b[..., None] * bb[1][None]
    n_c = jnp.concatenate([n_f, n_b], axis=2)
    n_c = jnp.transpose(n_c, (1, 0, 3, 2)).reshape(G, T * Q, 2 * P)
    lam_t = jnp.concatenate([pw[0][T], pw[1][T]], axis=-1)
    w_f = cm[0][:, None] * jnp.transpose(pw[0][1:T + 1], (1, 0, 2))[:, :, None, :]
    w_b = cm[1][:, None] * jnp.transpose(pw[1][T - jnp.arange(T)], (1, 0, 2))[:, :, None, :]
    w_f = jnp.transpose(w_f, (0, 3, 1, 2)).reshape(G, P, T * Q)
    w_b = jnp.transpose(w_b, (0, 3, 1, 2)).reshape(G, P, T * Q)
    z = jnp.zeros_like(jnp.real(w_f))
    c_mats = dict(c_f_re=jnp.concatenate([jnp.real(w_f), z], axis=1),
                  c_f_im=jnp.concatenate([-jnp.imag(w_f), z], axis=1),
                  c_b_re=jnp.concatenate([z, jnp.real(w_b)], axis=1),
                  c_b_im=jnp.concatenate([z, -jnp.imag(w_b)], axis=1))
    return dict(m=m.astype(BF16), n_re=jnp.real(n_c).astype(BF16), n_im=jnp.imag(n_c).astype(BF16),
                l_re=jnp.real(lam_t), l_im=jnp.imag(lam_t), **{k: v.astype(BF16) for k, v in c_mats.items()})


def _s5_in_kernel(u_ref, m_ref, nre_ref, nim_ref, yi_ref, sre_ref, sim_ref):
    u = u_ref[...]
    yi_ref[...] = jnp.dot(u, m_ref[...], preferred_element_type=F32)
    sre_ref[...] = jnp.dot(u, nre_ref[...], preferred_element_type=F32).reshape(sre_ref.shape)
    sim_ref[...] = jnp.dot(u, nim_ref[...], preferred_element_type=F32).reshape(sim_ref.shape)


def _s5_in(ug, tabs):
    G, R, W = ug.shape
    P2 = 2 * S5_P
    RB = S5_SCAN_ROWS
    assert R % RB == 0
    gspec = lambda n: pl.BlockSpec((None, W, n), lambda g: (g, 0, 0))
    rspec = pl.BlockSpec((None, R, W), lambda g: (g, 0, 0))
    sspec = pl.BlockSpec((R // RB, RB, P2), lambda g: (0, g, 0))
    sshape = jax.ShapeDtypeStruct((R // RB, G * RB, P2), F32)
    return pl.pallas_call(
        _s5_in_kernel,
        out_shape=(jax.ShapeDtypeStruct((G, R, W), F32), sshape, sshape),
        grid=(G,),
        in_specs=[rspec, gspec(W), gspec(P2), gspec(P2)],
        out_specs=(rspec, sspec, sspec),
        compiler_params=_cp("arbitrary"),
        name="s5_in",
    )(ug, tabs['m'], tabs['n_re'], tabs['n_im'])


def _s5_scan_kernel(nsb, ncb, nblk, sref_ref, simf_ref, sreb_ref, simb_ref, lre_ref, lim_ref,
                    h0re_ref, h0im_ref, *rest):
    hfre_ref, hfim_ref, hbre_ref, hbim_ref, fre_ref, fim_ref, cre_sc, cim_sc = rest[4:]
    P = S5_P
    rows = sref_ref.shape[0] // ncb
    j = pl.program_id(1)
    fwd = lax.broadcasted_iota(jnp.int32, (1, 2 * P), 1) < P
    lre = lre_ref[...]
    lim = lim_ref[...]

    @pl.when(j == 0)
    def _():
        cre_sc[...] = h0re_ref[...]
        cim_sc[...] = h0im_ref[...]

    def at(k):
        return pl.ds(k, rows, stride=ncb)

    def body(k, carry):
        hre, him = carry
        kb = ncb - 1 - k
        hfre_ref[at(k), :] = hre
        hfim_ref[at(k), :] = him
        hbre_ref[at(kb), :] = hre
        hbim_ref[at(kb), :] = him
        sre = jnp.where(fwd, sref_ref[at(k), :], sreb_ref[at(kb), :])
        sim = jnp.where(fwd, simf_ref[at(k), :], simb_ref[at(kb), :])
        return (lre * hre - lim * him + sre, lre * him + lim * hre + sim)

    hre, him = lax.fori_loop(0, ncb, body, (cre_sc[...], cim_sc[...]), unroll=4)
    cre_sc[...] = hre
    cim_sc[...] = him

    @pl.when(j == nblk - 1)
    def _():
        fre_ref[...] = hre
        fim_ref[...] = him


def _s5_scan(sre, sim, tabs, h0re, h0im, hprev, row0, nseq, nc, nsb, nblk):
    _, grb, P2 = sre.shape
    rb = S5_SCAN_ROWS
    G = grb // rb
    assert nblk == 1 or nsb == 1
    ncb = nc // nblk
    assert nsb * ncb == rb and row0 % rb == 0 and nseq % nsb == 0 and nc % nblk == 0
    b0 = row0 // rb
    fspec = pl.BlockSpec((None, grb, P2), lambda i, j: (b0 + i * nblk + j, 0, 0))
    bspec = pl.BlockSpec((None, grb, P2), lambda i, j: (b0 + i * nblk + nblk - 1 - j, 0, 0))
    lspec = pl.BlockSpec((G * nsb, P2), lambda i, j: (0, 0))
    qspec = pl.BlockSpec((None, G * nsb, P2), lambda i, j: (i, 0, 0))
    anyspec = pl.BlockSpec(memory_space=pl.ANY)
    fin = jax.ShapeDtypeStruct((nseq // nsb, G * nsb, P2), F32)
    rep = lambda a: jnp.repeat(a, nsb, axis=0)
    flat = lambda a: a.reshape(nseq // nsb, G * nsb, P2)
    outs = pl.pallas_call(
        functools.partial(_s5_scan_kernel, nsb, ncb, nblk),
        out_shape=tuple(jax.ShapeDtypeStruct(h.shape, h.dtype) for h in hprev) + (fin, fin),
        grid=(nseq // nsb, nblk),
        in_specs=[fspec, fspec, bspec, bspec, lspec, lspec, qspec, qspec] + [anyspec] * 4,
        out_specs=(fspec, fspec, bspec, bspec, qspec, qspec),
        scratch_shapes=[pltpu.VMEM((G * nsb, P2), F32), pltpu.VMEM((G * nsb, P2), F32)],
        input_output_aliases={8: 0, 9: 1, 10: 2, 11: 3},
        compiler_params=_cp("arbitrary", "arbitrary"),
        name="s5_scan",
    )(sre, sim, sre, sim, rep(tabs['l_re']), rep(tabs['l_im']), flat(h0re), flat(h0im), *hprev)
    return outs[:4], outs[4].reshape(nseq // nsb, G, nsb, P2), outs[5].reshape(nseq // nsb, G, nsb, P2)


def _s5_out_kernel(yi_ref, hfre_ref, hfim_ref, hbre_ref, hbim_ref, cfre_ref, cfim_ref, cbre_ref, cbim_ref,
                   y_ref):
    y = yi_ref[...]
    for h_ref, c_ref in ((hfre_ref, cfre_ref), (hfim_ref, cfim_ref), (hbre_ref, cbre_ref), (hbim_ref, cbim_ref)):
        h = h_ref[...].reshape(y.shape[0], h_ref.shape[-1])
        y += jnp.dot(h.astype(BF16), c_ref[...], preferred_element_type=F32)
    y_ref[...] = y.astype(y_ref.dtype)


def _s5_out(yi, hprev, tabs):
    G, R, W = yi.shape
    P2 = 2 * S5_P
    RB = S5_SCAN_ROWS
    gspec = pl.BlockSpec((None, P2, W), lambda g: (g, 0, 0))
    hspec = pl.BlockSpec((R // RB, RB, P2), lambda g: (0, g, 0))
    rspec = pl.BlockSpec((None, R, W), lambda g: (g, 0, 0))
    return pl.pallas_call(
        _s5_out_kernel,
        out_shape=jax.ShapeDtypeStruct((G, R, W), F32),
        grid=(G,),
        in_specs=[rspec] + [hspec] * 4 + [gspec] * 4,
        out_specs=rspec,
        compiler_params=_cp("arbitrary"),
        name="s5_out",
    )(yi, *hprev, tabs['c_f_re'], tabs['c_f_im'], tabs['c_b_re'], tabs['c_b_im'])


LANES = 128
S5_GB = LANES // S5_Q


def _block_transpose(sets):
    blk = lax.broadcasted_iota(jnp.int32, sets[0][0].shape, 1) // S5_Q
    sets = [list(regs) for regs in sets]
    d = S5_GB // 2
    while d:
        keep = (blk & d) == 0
        for regs in sets:
            for i in range(S5_GB):
                if i & d:
                    continue
                a, b = regs[i], regs[i + d]
                regs[i] = jnp.where(keep, a, pltpu.roll(b, d * S5_Q, 1))
                regs[i + d] = jnp.where(keep, pltpu.roll(a, LANES - d * S5_Q, 1), b)
        d //= 2
    return sets


def _s5_pre_kernel(tm, x_ref, sh_ref, sc_ref, ug_ref, u_sc):
    u = _modulate(x_ref[...], sh_ref[...], sc_ref[...])
    for j in range(D // LANES):
        u_sc[j] = u[:, j * LANES:(j + 1) * LANES]
    rows16 = 16
    nh = S5_T // S5_GB
    for c in range(tm // (S5_T * rows16)):
        base = c * S5_T * rows16
        for j in range(D // LANES):
            sets = [[u_sc[j, pl.ds(base + h * S5_GB + s, rows16, stride=S5_T), :] for s in range(S5_GB)]
                    for h in range(nh)]
            for h, regs in enumerate(_block_transpose(sets)):
                for gl, t in enumerate(regs):
                    ug_ref[j * S5_GB + gl, c * rows16:(c + 1) * rows16, h * LANES:(h + 1) * LANES] = t.astype(BF16)


def _s5_pre(lay, x, mods):
    tm = lay.tile(512)
    assert tm % (S5_T * 16) == 0
    return pl.pallas_call(
        functools.partial(_s5_pre_kernel, tm),
        out_shape=jax.ShapeDtypeStruct((S5_G, lay.T // S5_T, S5_T * S5_Q), BF16),
        grid=(lay.T // tm,),
        in_specs=[pl.BlockSpec((tm, D), lambda i: (i, 0)), _mod_spec(lay, tm, 0, 1), _mod_spec(lay, tm, 1, 1)],
        out_specs=pl.BlockSpec((S5_G, tm // S5_T, S5_T * S5_Q), lambda i: (0, i, 0)),
        scratch_shapes=[pltpu.VMEM((D // LANES, tm, LANES), F32)],
        compiler_params=_cp("arbitrary"),
        name="s5_pre",
    )(x, mods, mods)


def _s5_glu_kernel(tm, x_ref, yg_ref, g_ref, wv_ref, wg_ref, o_ref, a_sc, y_sc):
    @pl.when(pl.program_id(1) == 0)
    def _():
        def sub_tile(c, carry):
            crow = pl.ds(pl.multiple_of(c * 8, 8), 8)
            base = c * (S5_T * 8)
            nh = S5_T // S5_GB
            for j2 in range(0, D // LANES, 2):
                keys = [(j, h) for j in (j2, j2 + 1) for h in range(nh)]
                sets = [[yg_ref[j * S5_GB + gl, crow, h * LANES:(h + 1) * LANES] for gl in range(S5_GB)]
                        for j, h in keys]
                for (j, h), regs in zip(keys, _block_transpose(sets)):
                    for s, t in enumerate(regs):
                        y_sc[j, pl.ds(base + h * S5_GB + s, 8, stride=S5_T), :] = t
            return carry

        lax.fori_loop(0, tm // (S5_T * 8), sub_tile, 0)
        for j in range(D // LANES):
            a_sc[:, j * LANES:(j + 1) * LANES] = jax.nn.gelu(y_sc[j]).astype(BF16)

    a = a_sc[...]
    val = jnp.dot(a, wv_ref[...], preferred_element_type=F32)
    gate = jnp.dot(a, wg_ref[...], preferred_element_type=F32)
    o_ref[...] = x_ref[...] + g_ref[...] * (val * jax.nn.sigmoid(gate))


def _s5_glu(lay, x, yg, mods, glu_w):
    tm = lay.tile(1024)
    assert tm % (S5_T * 8) == 0
    tn = 512
    nn = D // tn
    return pl.pallas_call(
        functools.partial(_s5_glu_kernel, tm),
        out_shape=jax.ShapeDtypeStruct((lay.T, D), F32),
        grid=(lay.T // tm, nn),
        in_specs=[pl.BlockSpec((tm, tn), lambda i, j: (i, j)),
                  pl.BlockSpec((S5_G, tm // S5_T, S5_T * S5_Q), lambda i, j: (0, i, 0)),
                  pl.BlockSpec((None, 1, tn), lambda i, j: (lay.group(i * tm) * MOD_CHUNKS + 2, 0, j)),
                  pl.BlockSpec((D, tn), lambda i, j: (0, j)),
                  pl.BlockSpec((D, tn), lambda i, j: (0, nn + j))],
        out_specs=pl.BlockSpec((tm, tn), lambda i, j: (i, j)),
        scratch_shapes=[pltpu.VMEM((tm, D), BF16), pltpu.VMEM((D // LANES, tm, LANES), F32)],
        compiler_params=_cp("arbitrary", "arbitrary"),
        name="s5_glu",
    )(x, yg, mods, glu_w, glu_w)


def _s5_layer(lay, x, mods, p, h0_re, h0_im):
    T, G, P, Q = S5_T, S5_G, S5_P, S5_Q
    tabs = _s5_tables(p['a_re'], p['a_im'], p['log_dt'], p['b_re'], p['b_im'], p['c_re'], p['c_im'], p['d'])
    R = lay.T // T
    ug = _s5_pre(lay, x, mods)
    yi, sre, sim = _s5_in(ug, tabs)
    hprev = tuple(jnp.zeros(sre.shape, F32) for _ in range(4))
    ncp, ncs = lay.L // T, lay.LS // T
    nsb = S5_SCAN_ROWS // ncp
    zero = jnp.zeros((lay.B // nsb, G, nsb, 2 * P), F32)
    hprev, fre, fim = _s5_scan(sre, sim, tabs, zero, zero, hprev, 0, lay.B, ncp, nsb, 1)
    to_lanes = lambda s: jnp.transpose(s, (0, 2, 1, 3)).reshape(lay.NS, G, 1, 2 * P)
    hprev, _, _ = _s5_scan(sre, sim, tabs, to_lanes(h0_re), to_lanes(h0_im), hprev,
                           lay.TP // T, lay.NS, ncs, 1, max(1, ncs // S5_SCAN_ROWS))
    yg = _s5_out(yi, hprev, tabs)
    x = _s5_glu(lay, x, yg, mods, p['glu_w'])
    from_lanes = lambda s: jnp.transpose(s, (0, 2, 1, 3)).reshape(lay.B, G, 2, P).transpose(0, 2, 1, 3)
    return x, from_lanes(fre), from_lanes(fim)


def _hyena_filters(L, p):
    mm = functools.partial(jnp.matmul, precision=HIGHEST)
    f = jnp.linspace(1e-4, HY_BANDS - 1, HY_BANDS, dtype=F32)[None, :]
    max_decay = math.log(HY_TARGET) / HY_FAST_PCT
    min_decay = math.log(HY_TARGET) / HY_SLOW_PCT
    deltas = jnp.abs(jnp.linspace(min_decay, max_decay, D, dtype=F32))
    w3 = p['f_w3'].reshape(-1, 2, 2, D)

    def side(pos, s):
        t = (pos.astype(F32) / (L - 1))[:, None]
        w = 2.0 * math.pi * pos.astype(F32)[:, None] / L
        feats = jnp.concatenate([t, jnp.cos(f * w), -jnp.sin(f * w)], axis=-1)
        z = jnp.sin(p['f_freq'][0] * (mm(feats, p['f_w1']) + p['f_b1']))
        z = jnp.sin(p['f_freq'][1] * (mm(z, p['f_w2']) + p['f_b2']))
        win = jnp.exp(-t * deltas)
        return jnp.stack([mm(z, w3[:, o, s]) * win for o in range(2)])

    j = jnp.arange(L, dtype=jnp.int32)
    k_lo = side(j, 0)
    k_hi = side((L - j) % L, 1) * (j > 0).astype(F32)[None, :, None]
    norm = jnp.sum(jnp.abs(k_lo), axis=1, keepdims=True) + jnp.sum(jnp.abs(k_hi), axis=1, keepdims=True)
    k_lo, k_hi = k_lo / norm, k_hi / norm
    alt = (1.0 - 2.0 * (j % 2).astype(F32))[None, :, None]
    k_ny = jnp.sum(alt * (k_lo + k_hi), axis=1) / (2 * L)
    return k_lo, k_hi, k_ny


def _dft_tables(L):
    r = math.isqrt(L)
    assert r * r == L
    t = jnp.arange(L, dtype=jnp.int32)[None, :]
    a = jnp.arange(r, dtype=jnp.int32)[:, None]

    def unit(idx):
        ang = (idx % (2 * L)).astype(F32) * (math.pi / L)
        return jnp.cos(ang), jnp.sin(ang)

    c1, s1 = unit(a * r * t)
    c2, s2 = unit(a * t)
    c1, s1, c2, s2 = c1[:, None], s1[:, None], c2[None], s2[None]
    cos = (c1 * c2 - s1 * s2).reshape(L, L)
    sin = (s1 * c2 + c1 * s2).reshape(L, L)
    return cos.astype(BF16), sin.astype(BF16)


def _hy_spec_kernel(L, tr, c_ref, s_ref, klo_ref, khi_ref, p_ref, q_ref):
    r = pl.program_id(2)
    f = r * tr + lax.broadcasted_iota(jnp.int32, (tr, 1), 0)
    sgn = (1 - 2 * (f % 2)).astype(F32)
    scale = jnp.where(f == 0, 1.0, 2.0) * (1.0 / (2 * L))
    c, s = c_ref[...], s_ref[...]
    lo, hi = klo_ref[...], khi_ref[...]
    dot = functools.partial(jnp.dot, preferred_element_type=F32)
    p_ref[...] = scale * (dot(c, lo) + sgn * dot(c, hi))
    q_ref[...] = scale * (dot(s, lo) + sgn * dot(s, hi))


def _hy_spectrum(L, cos, sin, k_lo, k_hi):
    tr = min(L, 512)
    tc = 512
    kspec = pl.BlockSpec((None, L, tc), lambda o, j, r: (o, 0, j))
    tspec = pl.BlockSpec((tr, L), lambda o, j, r: (r, 0))
    ospec = pl.BlockSpec((None, tr, tc), lambda o, j, r: (o, r, j))
    return pl.pallas_call(
        functools.partial(_hy_spec_kernel, L, tr),
        out_shape=(jax.ShapeDtypeStruct((2, L, D), F32), jax.ShapeDtypeStruct((2, L, D), F32)),
        grid=(2, D // tc, L // tr),
        in_specs=[tspec, tspec, kspec, kspec],
        out_specs=(ospec, ospec),
        compiler_params=_cp("arbitrary", "arbitrary", "arbitrary"),
        name="hy_spectrum",
    )(cos, sin, k_lo.astype(BF16), k_hi.astype(BF16))


def _hy_core_kernel(L, tr, ngrp, tc, x1_ref, x2_ref, v_ref, sw1_ref, sw2_ref, swv_ref, sb1_ref, sb2_ref,
                    sbv_ref, c_ref, s_ref, p_ref, q_ref, kny_ref, skip_ref, *rest):
    o_ref, z0_sc, z1_sc, x2_sc, a_sc, b_sc, ny0_sc, ny1_sc = rest[-8:]
    fused = tr == L
    ph = pl.program_id(2)
    r = pl.program_id(3)
    W = ngrp * tc

    def phase(k):
        return (lambda f: f()) if fused else pl.when(ph == k)
    tcv = min(L, 512)
    halo = 16

    def alt_sign(start, n):
        t = start + lax.broadcasted_iota(jnp.int32, (n, 1), 0)
        return (1 - 2 * (t % 2)).astype(F32)

    def conv3(src_ref, g, a, w_ref, b_ref):
        x = src_ref[g, pl.ds(a, tcv), :].astype(F32)
        row = lax.broadcasted_iota(jnp.int32, (tcv, 1), 0)
        up_at = pl.multiple_of(jnp.maximum(a - halo, 0), halo)
        dn_at = pl.multiple_of(jnp.minimum(a + tcv, L - halo), halo)
        up = src_ref[g, pl.ds(up_at, halo), :][halo - 1:halo, :].astype(F32)
        dn = src_ref[g, pl.ds(dn_at, halo), :][0:1, :].astype(F32)
        up = jnp.where(a > 0, up, 0.0)
        dn = jnp.where(a + tcv < L, dn, 0.0)
        prev = jnp.where(row == 0, up, pltpu.roll(x, 1, 0))
        nxt = jnp.where(row == tcv - 1, dn, pltpu.roll(x, tcv - 1, 0))
        return prev * w_ref[0:1, :] + x * w_ref[1:2, :] + nxt * w_ref[2:3, :] + b_ref[...]

    @(phase(0) if fused else pl.when(jnp.logical_and(ph == 0, r == 0)))
    def _():
        ny0_sc[...] = jnp.zeros_like(ny0_sc)

        def conv_tile(ti, carry):
            a = pl.multiple_of(ti * tcv, tcv)
            rows_a = pl.ds(a, tcv)
            for g in range(ngrp):
                cols = slice(g * tc, (g + 1) * tc)
                z1_sc[rows_a, cols] = conv3(x1_ref, g, a, sw1_ref, sb1_ref).astype(BF16)
                x2_sc[rows_a, cols] = conv3(x2_ref, g, a, sw2_ref, sb2_ref).astype(BF16)
                z0_sc[rows_a, cols] = conv3(v_ref, g, a, swv_ref, sbv_ref).astype(BF16)
            ny0_sc[...] += jnp.sum(alt_sign(a, tcv) * z0_sc[rows_a, :].astype(F32), axis=0, keepdims=True)
            return carry

        lax.fori_loop(0, L // tcv, conv_tile, 0)

    start = pl.multiple_of(r * tr, tr)
    rows = pl.ds(start, tr)
    dot = functools.partial(jnp.dot, preferred_element_type=F32)

    def forward(order, z_sc):
        z = z_sc[...]
        zre = dot(c_ref[...], z)
        zim = dot(s_ref[...], z)
        pw = jnp.concatenate([p_ref[order] if fused else p_ref[...]] * ngrp, axis=1)
        qw = jnp.concatenate([q_ref[order] if fused else q_ref[...]] * ngrp, axis=1)
        a_sc[rows, :] = (zre * pw - zim * qw).astype(BF16)
        b_sc[rows, :] = (zim * pw + zre * qw).astype(BF16)

    def inverse(order, z_sc, ny_sc):
        y = dot(c_ref[...], a_sc[...]) + dot(s_ref[...], b_sc[...])
        kny = jnp.concatenate([kny_ref[order:order + 1, :]] * ngrp, axis=1)
        skip = jnp.concatenate([skip_ref[order:order + 1, :]] * ngrp, axis=1)
        return y + alt_sign(start, tr) * (ny_sc[...] * kny) + skip * z_sc[rows, :].astype(F32)

    @phase(0)
    def _():
        forward(0, z0_sc)

    @phase(1)
    def _():
        z1 = (z1_sc[rows, :].astype(F32) * inverse(0, z0_sc, ny0_sc)).astype(BF16)
        z1_sc[rows, :] = z1

        @pl.when(r == 0)
        def _():
            ny1_sc[...] = jnp.zeros_like(ny1_sc)

        ny1_sc[...] += jnp.sum(alt_sign(start, tr) * z1.astype(F32), axis=0, keepdims=True)

    @phase(2)
    def _():
        forward(1, z1_sc)

    @phase(3)
    def _():
        out = x2_sc[rows, :].astype(F32) * inverse(1, z1_sc, ny1_sc)
        for g in range(ngrp):
            o_ref[g, rows, :] = out[:, g * tc:(g + 1) * tc].astype(o_ref.dtype)


def _hy_core(proj, short_w, short_b, skip, cos, sin, pq, k_ny, o_prev, row0, nseq, L, ngrp, tc):
    T = proj.shape[0]
    tr = min(L, 256)
    nrt = L // tr
    assert row0 % (L * ngrp) == 0 and nseq % ngrp == 0 and T % L == 0
    sb0 = row0 // (L * ngrp)
    nct = D // tc
    p3 = proj.reshape(T // L, L, 3 * D)
    p_arr, q_arr = pq

    def xspec(part):
        mode = {} if nrt == 1 else dict(pipeline_mode=pl.Buffered(1))
        return pl.BlockSpec((ngrp, L, tc), lambda i, j, ph, r: (sb0 + i, 0, part * nct + j), **mode)

    def wspec(part, rows_):
        return pl.BlockSpec((rows_, tc), lambda i, j, ph, r: (0, part * nct + j))

    fused = nrt == 1
    if fused:
        pq_spec = pl.BlockSpec((2, tr, tc), lambda i, j, ph, r: (0, 0, j))
    else:
        pq_spec = pl.BlockSpec((None, tr, tc),
                               lambda i, j, ph, r: (ph // 2, jnp.where(ph % 2 == 0, r, nrt - 1), j))

    tspec = pl.BlockSpec((tr, L), lambda i, j, ph, r: (r, 0))
    in_specs = [xspec(0), xspec(1), xspec(2), wspec(0, 3), wspec(1, 3), wspec(2, 3),
                wspec(0, 1), wspec(1, 1), wspec(2, 1), tspec, tspec, pq_spec, pq_spec,
                pl.BlockSpec((2, tc), lambda i, j, ph, r: (0, j)),
                pl.BlockSpec((2, tc), lambda i, j, ph, r: (0, j))]
    sb = short_b.reshape(1, 3 * D)
    args = [p3, p3, p3, short_w, short_w, short_w, sb, sb, sb, cos, sin, p_arr, q_arr, k_ny, skip]
    aliases = {}
    if o_prev is not None:
        in_specs.append(pl.BlockSpec(memory_space=pl.ANY))
        args.append(o_prev.reshape(T // L, L, D))
        aliases = {len(args) - 1: 0}
    W = ngrp * tc
    out = pl.pallas_call(
        functools.partial(_hy_core_kernel, L, tr, ngrp, tc),
        out_shape=jax.ShapeDtypeStruct((T // L, L, D), BF16),
        grid=(nseq // ngrp, nct, 1 if fused else 4, nrt),
        in_specs=in_specs,
        out_specs=pl.BlockSpec((ngrp, L, tc), lambda i, j, ph, r: (sb0 + i, 0, j)),
        scratch_shapes=[pltpu.VMEM((L, W), BF16)] * 5 + [pltpu.VMEM((1, W), F32)] * 2,
        input_output_aliases=aliases,
        compiler_params=_cp("arbitrary", "arbitrary", "arbitrary", "arbitrary"),
        name="hy_core",
    )(*args)
    return out.reshape(T, D)


def _plain_out_kernel(x_ref, z_ref, g_ref, w_ref, b_ref, o_ref):
    acc = jnp.dot(z_ref[...], w_ref[...], preferred_element_type=F32) + b_ref[...]
    o_ref[...] = x_ref[...] + g_ref[...] * acc


def _plain_out(lay, x, mods, z, w, b):
    tm = lay.tile(1024)
    tn = 512
    kdim = z.shape[1]
    return pl.pallas_call(
        _plain_out_kernel,
        out_shape=jax.ShapeDtypeStruct((lay.T, D), F32),
        grid=(lay.T // tm, D // tn),
        in_specs=[pl.BlockSpec((tm, tn), lambda i, j: (i, j)),
                  pl.BlockSpec((tm, kdim), lambda i, j: (i, 0)),
                  pl.BlockSpec((None, 1, tn), lambda i, j: (lay.group(i * tm) * MOD_CHUNKS + 2, 0, j)),
                  pl.BlockSpec((kdim, tn), lambda i, j: (0, j)),
                  pl.BlockSpec((1, tn), lambda i, j: (0, j))],
        out_specs=pl.BlockSpec((tm, tn), lambda i, j: (i, j)),
        compiler_params=_cp("arbitrary", "arbitrary"),
        name="plain_out",
    )(x, z, mods, w, b.reshape(1, D))


def _hyena_layer(lay, x, mods, p):
    proj = _proj(lay, x, mods, 0, 1, p['w_in'].astype(BF16), p['b_in'].reshape(1, 3 * D), 768)
    z = jnp.zeros((lay.T, D), BF16)
    for row0, nseq, L, ngrp, tc in ((0, lay.B, lay.L, math.gcd(lay.B, 4), 256),
                                    (lay.TP, lay.NS, lay.LS, lay.NS, 256)):
        k_lo, k_hi, k_ny = _hyena_filters(L, p)
        cos, sin = _dft_tables(L)
        pq = _hy_spectrum(L, cos, sin, k_lo, k_hi)
        z = _hy_core(proj, p['short_w'], p['short_b'], p['skip'], cos, sin, pq, k_ny, z, row0, nseq, L, ngrp, tc)
    return _plain_out(lay, x, mods, z, p['w_out'].astype(BF16), p['b_out'])


_NT = (((1,), (1,)), ((), ()))
_TN = (((0,), (0,)), ((), ()))


def _tri(dr):
    t = lax.broadcasted_iota(jnp.int32, (CHUNK, CHUNK), 0)
    s = lax.broadcasted_iota(jnp.int32, (CHUNK, CHUNK), 1)
    return (s <= t) if dr == 0 else (s >= t)


def _chunk_cumsum(g, dr):
    n = g.shape[0]
    pos = lax.broadcasted_iota(jnp.int32, g.shape, 0) % CHUNK
    sh = 1
    while sh < CHUNK:
        if dr == 0:
            g = g + jnp.where(pos >= sh, pltpu.roll(g, sh, 0), 0.0)
        else:
            g = g + jnp.where(pos < CHUNK - sh, pltpu.roll(g, n - sh, 0), 0.0)
        sh *= 2
    return g


def _head_epilogue(o_sc, gate_ref, ng_ref, a_ref, center):
    rows = o_sc.shape[0]
    tr = math.gcd(rows, 256)

    def tile(i, carry):
        r = pl.ds(pl.multiple_of(i * tr, tr), tr)
        o = o_sc[r, :]
        if center:
            o = o - jnp.mean(o, axis=-1, keepdims=True)
        o = o * lax.rsqrt(jnp.mean(o * o, axis=-1, keepdims=True) + RMS_EPS) * ng_ref[...]
        a_ref[r, :] = (o * _silu(gate_ref[r, :].astype(F32))).astype(a_ref.dtype)
        return carry

    lax.fori_loop(0, rows // tr, tile, 0)


def _gla_kernel(cps, nseg, U, has_s0, want_final, *refs):
    q_ref, k_ref, v_ref, lr_ref, w2f_ref, w2b_ref, gb_ref, gate_ref, ng_ref = refs[:9]
    s0_ref = refs[9] if has_s0 else None
    qin_sc, kin_sc, kout_sc, dec_sc, st_sc, s_sc, s0t_sc, o_ref = refs[-8:]
    outs = refs[-10:-8] if want_final else refs[-9:-8]
    a_ref = outs[0]
    sf_ref = outs[1] if want_final else None
    C = CHUNK
    nsc = cps // U
    nchunks = nseg * cps
    rows_total = nchunks * C
    w2 = (w2f_ref, w2b_ref)

    for dr in range(2):
        pre = jnp.dot(lr_ref[...], w2[dr][...], preferred_element_type=F32) + gb_ref[dr:dr + 1, :]
        g = (jnp.minimum(pre, 0.0) - jnp.log(1.0 + jnp.exp(-jnp.abs(pre)))) * (1.0 / GLA_TAU)
        b = _chunk_cumsum(g, dr)
        b3 = b.reshape(nchunks, C, GLA_DK)
        tot = b3[:, C - 1:C, :] if dr == 0 else b3[:, 0:1, :]
        dec_sc[...] = jnp.exp(tot).reshape(nchunks, GLA_DK)
        k = k_ref[...].astype(F32)
        qin_sc[...] = (q_ref[...].astype(F32) * (GLA_DK ** -0.5) * jnp.exp(b)).astype(BF16)
        kin_sc[...] = (k * jnp.exp(-b)).astype(BF16)
        kout_sc[...] = (k * jnp.exp(tot - b3).reshape(rows_total, GLA_DK)).astype(BF16)
        if has_s0:
            s0t_sc[...] = jnp.transpose(s0_ref[dr], (1, 0))
        tri = _tri(dr)

        def super_chunk(jj, carry, dr=dr, tri=tri):
            j = jj if dr == 0 else nseg * nsc - 1 - jj
            in_seg = j % nsc
            first = (in_seg == 0) if dr == 0 else (in_seg == nsc - 1)
            last = (in_seg == nsc - 1) if dr == 0 else (in_seg == 0)

            @pl.when(first)
            def _():
                s_sc[...] = s0t_sc[...] if has_s0 else jnp.zeros_like(s_sc)

            base = j * (U * C)
            for u in range(U):
                rows = pl.ds(pl.multiple_of(base + u * C, C), C)
                v = v_ref[rows, :]
                sc = lax.dot_general(qin_sc[rows, :], kin_sc[rows, :], _NT, preferred_element_type=F32)
                o = jnp.dot(jnp.where(tri, sc, 0.0).astype(BF16), v, preferred_element_type=F32)
                st_sc[u] = lax.dot_general(v, kout_sc[rows, :], _TN, preferred_element_type=F32)
                if dr == 0:
                    o_ref[rows, :] = o
                else:
                    o_ref[rows, :] += o
            s = s_sc[...]
            for u in (range(U) if dr == 0 else reversed(range(U))):
                kv = st_sc[u]
                st_sc[u] = s
                s = dec_sc[pl.ds(j * U + u, 1), :] * s + kv
            s_sc[...] = s
            for u in range(U):
                rows = pl.ds(pl.multiple_of(base + u * C, C), C)
                o_ref[rows, :] += lax.dot_general(qin_sc[rows, :], st_sc[u].astype(BF16), _NT,
                                                  preferred_element_type=F32)
            if want_final:
                @pl.when(last)
                def _():
                    sf_ref[j // nsc, dr] = jnp.transpose(s, (1, 0))
            return carry

        lax.fori_loop(0, nseg * nsc, super_chunk, 0)

    _head_epilogue(o_ref, gate_ref, ng_ref, a_ref, center=False)


def _gla_core(proj, w2f, w2b, gate_b, norm_g, s0, o_prev, row0, nseq, seqlen, nseg, want_final):
    T = proj.shape[0]
    rows = nseg * seqlen
    cps = seqlen // CHUNK
    U = math.gcd(cps, 8)
    assert row0 % rows == 0 and seqlen % CHUNK == 0 and nseq % nseg == 0
    rb = row0 // rows
    hk = GLA_H * GLA_DK
    has_s0 = s0 is not None
    assert not has_s0 or nseg == 1
    in_specs = [pl.BlockSpec((rows, GLA_DK), lambda b, h: (rb + b, h)),
                pl.BlockSpec((rows, GLA_DK), lambda b, h: (rb + b, GLA_H + h)),
                pl.BlockSpec((rows, GLA_DV), lambda b, h: (rb + b, 2 * hk // GLA_DV + h)),
                pl.BlockSpec((rows, 128), lambda b, h: (rb + b, (2 * hk + 2 * GLA_H * GLA_DV) // 128)),
                pl.BlockSpec((128, GLA_DK), lambda b, h: (0, h)),
                pl.BlockSpec((128, GLA_DK), lambda b, h: (0, h)),
                pl.BlockSpec((2, GLA_DK), lambda b, h: (0, h)),
                pl.BlockSpec((rows, GLA_DV), lambda b, h: (rb + b, (2 * hk) // GLA_DV + GLA_H + h)),
                pl.BlockSpec((1, GLA_DV), lambda b, h: (0, 0))]
    args = [proj, proj, proj, proj, w2f, w2b, gate_b, proj, norm_g.reshape(1, GLA_DV)]
    if has_s0:
        in_specs.append(pl.BlockSpec((None, 2, None, GLA_DK, GLA_DV), lambda b, h: (b, 0, h, 0, 0)))
        args.append(s0)
    in_specs.append(pl.BlockSpec(memory_space=pl.ANY))
    args.append(o_prev)
    aliases = {len(args) - 1: 0}
    out_shape = [jax.ShapeDtypeStruct((T, GLA_H * GLA_DV), BF16)]
    out_specs = [pl.BlockSpec((rows, GLA_DV), lambda b, h: (rb + b, h))]
    if want_final:
        out_shape.append(jax.ShapeDtypeStruct((nseq, 2, GLA_H, GLA_DK, GLA_DV), F32))
        out_specs.append(pl.BlockSpec((nseg, 2, None, GLA_DK, GLA_DV), lambda b, h: (b, 0, h, 0, 0)))
    outs = pl.pallas_call(
        functools.partial(_gla_kernel, cps, nseg, U, has_s0, want_final),
        out_shape=tuple(out_shape),
        grid=(nseq // nseg, GLA_H),
        in_specs=in_specs,
        out_specs=tuple(out_specs),
        scratch_shapes=[pltpu.VMEM((rows, GLA_DK), BF16)] * 3
        + [pltpu.VMEM((nseg * cps, GLA_DK), F32), pltpu.VMEM((U, GLA_DV, GLA_DK), F32),
           pltpu.VMEM((GLA_DV, GLA_DK), F32), pltpu.VMEM((GLA_DV, GLA_DK), F32),
           pltpu.VMEM((rows, GLA_DV), F32)],
        input_output_aliases=aliases,
        compiler_params=_cp("arbitrary", "arbitrary"),
        name="gla",
    )(*args)
    return (outs[0], outs[1]) if want_final else (outs[0], None)


def _ret_kernel(cps, nseg, U, has_s0, want_final, rope, *refs):
    q_ref, k_ref, v_ref, dm_ref, qd_ref, kd_ref, cd_ref, gate_ref, ng_ref = refs[:9]
    nxt = 9
    if rope:
        cos_ref, sin_ref = refs[9:11]
        nxt = 11
    s0_ref = refs[nxt] if has_s0 else None
    qr_sc, kr_sc, qd_sc, kd_sc, st_sc, s_sc, o_ref = refs[-7:]
    outs = refs[-9:-7] if want_final else refs[-8:-7]
    a_ref = outs[0]
    sf_ref = outs[1] if want_final else None
    C = RET_CHUNK
    nsc = cps // U
    R = U * C
    SB = 64

    def rot(x, rows):
        if not rope:
            return x
        half = x.shape[1] // 2
        swapped = jnp.concatenate([pltpu.roll(x[:, :half], half // 2, 1),
                                   pltpu.roll(x[:, half:], half // 2, 1)], axis=1)
        return x * cos_ref[rows, :] + swapped * sin_ref[rows, :]

    for dr in range(2):
        def super_chunk(jj, carry, dr=dr):
            j = jj if dr == 0 else nseg * nsc - 1 - jj
            in_seg = j % nsc
            first = (in_seg == 0) if dr == 0 else (in_seg == nsc - 1)
            last = (in_seg == nsc - 1) if dr == 0 else (in_seg == 0)

            @pl.when(first)
            def _():
                s_sc[...] = s0_ref[dr] if has_s0 else jnp.zeros_like(s_sc)

            base = pl.multiple_of(j * R, R)
            rows_r = pl.ds(base, R)
            q = rot(q_ref[rows_r, :].astype(F32), rows_r)
            k = rot(k_ref[rows_r, :].astype(F32), rows_r) * (RET_DK ** -0.5)
            qr_sc[...] = q.astype(BF16)
            kr_sc[...] = k.astype(BF16)
            qd_sc[...] = (q.reshape(U, C, RET_DK) * qd_ref[dr][None]).reshape(R, RET_DK).astype(BF16)
            kd_sc[...] = (k.reshape(U, C, RET_DK) * kd_ref[dr][None]).reshape(R, RET_DK).astype(BF16)
            for u in range(U):
                loc = pl.ds(u * C, C)
                rows = pl.ds(pl.multiple_of(base + u * C, C), C)
                v = v_ref[rows, :]
                sc = lax.dot_general(qr_sc[loc, :], kr_sc[loc, :], _NT, preferred_element_type=F32)
                o = jnp.dot((sc * dm_ref[dr]).astype(BF16), v, preferred_element_type=F32)
                st_sc[u] = lax.dot_general(kd_sc[loc, :], v, _TN, preferred_element_type=F32)
                if dr == 0:
                    o_ref[rows, :] = o
                else:
                    o_ref[rows, :] += o
            cd = cd_ref[dr]
            for r0 in range(0, RET_DK, SB):
                srows = pl.ds(r0, SB)
                s = s_sc[srows, :]
                for u in (range(U) if dr == 0 else reversed(range(U))):
                    kv = st_sc[u, srows, :]
                    st_sc[u, srows, :] = s
                    s = cd * s + kv
                s_sc[srows, :] = s
            for u in range(U):
                rows = pl.ds(pl.multiple_of(base + u * C, C), C)
                o_ref[rows, :] += jnp.dot(qd_sc[pl.ds(u * C, C), :], st_sc[u].astype(BF16),
                                          preferred_element_type=F32)
            if want_final:
                @pl.when(last)
                def _():
                    sf_ref[j // nsc, dr] = s_sc[...]
            return carry

        lax.fori_loop(0, nseg * nsc, super_chunk, 0)

    _head_epilogue(o_ref, gate_ref, ng_ref, a_ref, center=True)


def _ret_tables(log_decay):
    C = RET_CHUNK
    lg = log_decay.astype(F32)[:, :, None, None]
    t = jnp.arange(C, dtype=F32)[:, None]
    s = jnp.arange(C, dtype=F32)[None, :]
    lag = jnp.stack([t - s, s - t])[:, None]
    dmask = jnp.where(lag >= 0, jnp.exp(jnp.maximum(lag, 0.0) * lg), 0.0)
    tl = jnp.arange(C, dtype=F32)[None, None, :, None]
    qdec = jnp.concatenate([jnp.exp((tl + 1.0) * lg[0:1]), jnp.exp((C - tl) * lg[1:2])], axis=0)
    kdec = jnp.concatenate([jnp.exp((C - 1.0 - tl) * lg[0:1]), jnp.exp(tl * lg[1:2])], axis=0)
    cdec = jnp.exp(C * lg)
    return dmask, qdec, kdec, cdec


def _rope_tables(seqlen, dk):
    half = dk // 2
    nf = half // 2
    pos = jnp.arange(seqlen, dtype=jnp.int32)
    inv = ROPE_BASE ** (-jnp.arange(nf, dtype=F32) / nf)
    ang_r = (pos // GRID_W).astype(F32)[:, None] * inv[None, :]
    ang_c = (pos % GRID_W).astype(F32)[:, None] * inv[None, :]
    cos = jnp.concatenate([jnp.cos(ang_r)] * 2 + [jnp.cos(ang_c)] * 2, axis=1)
    sin = jnp.concatenate([-jnp.sin(ang_r), jnp.sin(ang_r), -jnp.sin(ang_c), jnp.sin(ang_c)], axis=1)
    return cos, sin


def _ret_core(proj, tabs, norm_g, s0, o_prev, row0, nseq, seqlen, nseg, want_final, rope):
    T = proj.shape[0]
    rows = nseg * seqlen
    C = RET_CHUNK
    cps = seqlen // C
    U = math.gcd(cps, 4)
    assert row0 % rows == 0 and seqlen % C == 0 and nseq % nseg == 0
    rb = row0 // rows
    hk, hv = RET_H * RET_DK, RET_H * RET_DV
    has_s0 = s0 is not None
    assert not (has_s0 or rope) or nseg == 1
    tspec = lambda r, c: pl.BlockSpec((2, None, r, c), lambda b, h: (0, h, 0, 0))
    mode = dict(pipeline_mode=pl.Buffered(1)) if rows * RET_DV * 2 >= (4 << 20) else {}
    in_specs = [pl.BlockSpec((rows, RET_DK), lambda b, h: (rb + b, h), **mode),
                pl.BlockSpec((rows, RET_DK), lambda b, h: (rb + b, RET_H + h), **mode),
                pl.BlockSpec((rows, RET_DV), lambda b, h: (rb + b, 2 * hk // RET_DV + h), **mode),
                tspec(C, C), tspec(C, 1), tspec(C, 1), tspec(1, 1),
                pl.BlockSpec((rows, RET_DV), lambda b, h: (rb + b, (2 * hk + hv) // RET_DV + h), **mode),
                pl.BlockSpec((1, RET_DV), lambda b, h: (0, 0))]
    args = [proj, proj, proj, *tabs, proj, norm_g.reshape(1, RET_DV)]
    if rope:
        cos, sin = _rope_tables(seqlen, RET_DK)
        in_specs += [pl.BlockSpec((seqlen, RET_DK), lambda b, h: (0, 0), pipeline_mode=pl.Buffered(1))] * 2
        args += [cos, sin]
    if has_s0:
        in_specs.append(pl.BlockSpec((None, 2, None, RET_DK, RET_DV), lambda b, h: (b, 0, h, 0, 0)))
        args.append(s0)
    in_specs.append(pl.BlockSpec(memory_space=pl.ANY))
    args.append(o_prev)
    aliases = {len(args) - 1: 0}
    out_shape = [jax.ShapeDtypeStruct((T, hv), BF16)]
    out_specs = [pl.BlockSpec((rows, RET_DV), lambda b, h: (rb + b, h))]
    if want_final:
        out_shape.append(jax.ShapeDtypeStruct((nseq, 2, RET_H, RET_DK, RET_DV), F32))
        out_specs.append(pl.BlockSpec((nseg, 2, None, RET_DK, RET_DV), lambda b, h: (b, 0, h, 0, 0)))
    outs = pl.pallas_call(
        functools.partial(_ret_kernel, cps, nseg, U, has_s0, want_final, rope),
        out_shape=tuple(out_shape),
        grid=(nseq // nseg, RET_H),
        in_specs=in_specs,
        out_specs=tuple(out_specs),
        scratch_shapes=[pltpu.VMEM((U * C, RET_DK), BF16)] * 4
        + [pltpu.VMEM((U, RET_DK, RET_DV), F32), pltpu.VMEM((RET_DK, RET_DV), F32),
           pltpu.VMEM((rows, RET_DV), F32)],
        input_output_aliases=aliases,
        compiler_params=_cp("arbitrary", "arbitrary"),
        name="ret",
    )(*args)
    return (outs[0], outs[1]) if want_final else (outs[0], None)


def _gla_layer(lay, x, mods, p, s0):
    hk, hv = GLA_H * GLA_DK, GLA_H * GLA_DV
    w_all = jnp.concatenate([p['w_in'], p['gate_w1'][0], p['gate_w1'][1],
                             jnp.zeros((D, 128 - 2 * GLA_RANK), F32)], axis=1).astype(BF16)
    proj = _proj(lay, x, mods, 0, 1, w_all, jnp.zeros((1, w_all.shape[1]), F32), 640)
    pad = lambda w, lo: jnp.pad(w, ((lo, 128 - GLA_RANK - lo), (0, 0))).astype(BF16)
    w2f, w2b = pad(p['gate_w2'][0], 0), pad(p['gate_w2'][1], GLA_RANK)
    a = jnp.zeros((lay.T, hv), BF16)
    a, s_fin = _gla_core(proj, w2f, w2b, p['gate_b'], p['norm_g'], None, a, 0, lay.B, lay.L,
                         math.gcd(lay.B, 8), True)
    a, _ = _gla_core(proj, w2f, w2b, p['gate_b'], p['norm_g'], s0, a, lay.TP, lay.NS, lay.LS, 1, False)
    x = _plain_out(lay, x, mods, a, p['w_out'].astype(BF16), jnp.zeros((D,), F32))
    return x, s_fin


def _ret_layer(lay, x, mods, p, s0):
    hk, hv = RET_H * RET_DK, RET_H * RET_DV
    proj = _proj(lay, x, mods, 0, 1, p['w_in'].astype(BF16), jnp.zeros((1, 2 * hk + 2 * hv), F32), 1536)
    tabs = _ret_tables(p['log_decay'])
    a = jnp.zeros((lay.T, hv), BF16)
    a, s_fin = _ret_core(proj, tabs, p['norm_g'], None, a, 0, lay.B, lay.L, math.gcd(lay.B, 8), True, False)
    a, _ = _ret_core(proj, tabs, p['norm_g'], s0, a, lay.TP, lay.NS, lay.LS, 1, False, True)
    x = _plain_out(lay, x, mods, a, p['w_out'].astype(BF16), jnp.zeros((D,), F32))
    return x, s_fin


MOE_BM = 512
EXPERT_TF = 1792
ROUTER_LANES = 128
DMA_UNROLL = 8


def _router_kernel(x_ref, sh_ref, sc_ref, rw_ref, h_ref, idx_ref, gate_ref):
    h = _modulate(x_ref[...], sh_ref[...], sc_ref[...])
    h_ref[...] = h
    logits = jnp.dot(h, rw_ref[...], precision=HIGHEST, preferred_element_type=F32)
    lane = lax.broadcasted_iota(jnp.int32, logits.shape, 1)
    neg = jnp.float32(-jnp.inf)
    logits = jnp.where(lane < N_EXPERTS, logits, neg)
    m1 = jnp.max(logits, axis=-1, keepdims=True)
    i1 = jnp.min(jnp.where(logits == m1, lane, ROUTER_LANES), axis=-1, keepdims=True)
    rest = jnp.where(lane == i1, neg, logits)
    m2 = jnp.max(rest, axis=-1, keepdims=True)
    i2 = jnp.min(jnp.where(rest == m2, lane, ROUTER_LANES), axis=-1, keepdims=True)
    e2 = jnp.exp(m2 - m1)
    g1 = 1.0 / (1.0 + e2)
    idx_ref[:, 0:1] = i1
    idx_ref[:, 1:2] = i2
    gate_ref[:, 0:1] = g1
    gate_ref[:, 1:2] = e2 * g1


def _router(lay, x, mods, router_w):
    tm = lay.tile(512)
    rw = jnp.pad(router_w, ((0, 0), (0, ROUTER_LANES - N_EXPERTS)))
    return pl.pallas_call(
        _router_kernel,
        out_shape=(jax.ShapeDtypeStruct((lay.T, D), F32),
                   jax.ShapeDtypeStruct((lay.T, 2), jnp.int32),
                   jax.ShapeDtypeStruct((lay.T, 2), F32)),
        grid=(lay.T // tm,),
        in_specs=[pl.BlockSpec((tm, D), lambda i: (i, 0)),
                  _mod_spec(lay, tm, 3, 1), _mod_spec(lay, tm, 4, 1),
                  pl.BlockSpec((D, ROUTER_LANES), lambda i: (0, 0))],
        out_specs=(pl.BlockSpec((tm, D), lambda i: (i, 0)),
                   pl.BlockSpec((tm, 2), lambda i: (i, 0)),
                   pl.BlockSpec((tm, 2), lambda i: (i, 0))),
        compiler_params=_cp("arbitrary"),
        name="router",
    )(x, mods, mods, rw)


def _moe_plan(idx, bm):
    a = idx.size
    e = idx.reshape(a)
    onehot = (e[:, None] == jnp.arange(N_EXPERTS, dtype=jnp.int32)[None, :]).astype(jnp.int32)
    csum = jnp.cumsum(onehot, axis=0)
    counts = csum[-1]
    rank = jnp.sum((csum - onehot) * onehot, axis=-1)
    padded = (counts + bm - 1) // bm * bm
    pad_end = jnp.cumsum(padded)
    dest = ((pad_end - padded)[e] + rank).astype(jnp.int32)
    nb = -(-(a + N_EXPERTS * (bm - 1)) // bm)
    block_start = jnp.arange(nb, dtype=jnp.int32) * bm
    block_e = jnp.minimum(jnp.searchsorted(pad_end, block_start, side='right'), N_EXPERTS - 1).astype(jnp.int32)
    nvalid = (pad_end[-1] // bm).astype(jnp.int32).reshape(1)
    return dest, block_e, nvalid, nb


def _dispatch_kernel(tm, dest_ref, h_ref, xs_in, xs_hbm, sem):
    del xs_in
    i = pl.program_id(0)

    def row_copy(r, dst):
        return pltpu.make_async_copy(h_ref.at[pl.ds(r, 1)], xs_hbm.at[pl.ds(dst, 1)], sem)

    def issue(r, carry):
        a = 2 * (i * tm + r)
        row_copy(r, dest_ref[a]).start()
        row_copy(r, dest_ref[a + 1]).start()
        return carry

    def drain(r, carry):
        row_copy(r, 0).wait()
        row_copy(r, 0).wait()
        return carry

    lax.fori_loop(0, tm, issue, 0, unroll=DMA_UNROLL)
    lax.fori_loop(0, tm, drain, 0, unroll=DMA_UNROLL)


def _dispatch(lay, h, dest, nb, bm):
    tm = lay.tile(512)
    xs = jnp.zeros((nb * bm, D), F32)
    grid_spec = pltpu.PrefetchScalarGridSpec(
        num_scalar_prefetch=1,
        grid=(lay.T // tm,),
        in_specs=[pl.BlockSpec((tm, D), lambda i, d: (i, 0)), pl.BlockSpec(memory_space=pl.ANY)],
        out_specs=pl.BlockSpec(memory_space=pl.ANY),
        scratch_shapes=[pltpu.SemaphoreType.DMA(())],
    )
    return pl.pallas_call(
        functools.partial(_dispatch_kernel, tm),
        out_shape=jax.ShapeDtypeStruct(xs.shape, F32),
        grid_spec=grid_spec,
        input_output_aliases={2: 0},
        compiler_params=_cp("arbitrary"),
        name="dispatch",
    )(dest, h, xs)


def _experts_kernel(nf, be_ref, nv_ref, xs_ref, wa_ref, wb_ref, wo_ref, o_ref, xb_sc, acc_sc):
    i = pl.program_id(0)
    f = pl.program_id(1)
    valid = i < nv_ref[0]

    @pl.when(jnp.logical_and(valid, f == 0))
    def _():
        xb_sc[...] = xs_ref[...].astype(BF16)

    @pl.when(valid)
    def _():
        xb = xb_sc[...]
        a = jnp.dot(xb, wa_ref[...], preferred_element_type=F32)
        b = jnp.dot(xb, wb_ref[...], preferred_element_type=F32)
        h = (_silu(a) * b).astype(BF16)
        y = jnp.dot(h, wo_ref[...], preferred_element_type=F32)

        @pl.when(f == 0)
        def _():
            acc_sc[...] = y

        @pl.when(f > 0)
        def _():
            acc_sc[...] += y

    @pl.when(f == nf - 1)
    def _():
        o_ref[...] = jnp.where(valid, acc_sc[...], 0.0)


def _experts(xs, block_e, nvalid, nb, bm, w_in, w_out):
    tf = EXPERT_TF
    nf = EXPERT_DIM // tf

    def wmap(off):
        def imap(i, f, be, nv):
            fe = jnp.where(i < nv[0], f, nf - 1)
            return (be[i], 0, off + fe)
        return imap

    def womap(i, f, be, nv):
        fe = jnp.where(i < nv[0], f, nf - 1)
        return (be[i], fe, 0)

    grid_spec = pltpu.PrefetchScalarGridSpec(
        num_scalar_prefetch=2,
        grid=(nb, nf),
        in_specs=[pl.BlockSpec((bm, D), lambda i, f, be, nv: (jnp.minimum(i, nv[0] - 1), 0)),
                  pl.BlockSpec((None, D, tf), wmap(0)),
                  pl.BlockSpec((None, D, tf), wmap(nf)),
                  pl.BlockSpec((None, tf, D), womap)],
        out_specs=pl.BlockSpec((bm, D), lambda i, f, be, nv: (i, 0)),
        scratch_shapes=[pltpu.VMEM((bm, D), BF16), pltpu.VMEM((bm, D), F32)],
    )
    return pl.pallas_call(
        functools.partial(_experts_kernel, nf),
        out_shape=jax.ShapeDtypeStruct((nb * bm, D), F32),
        grid_spec=grid_spec,
        compiler_params=_cp("arbitrary", "arbitrary"),
        name="experts",
    )(block_e, nvalid, xs, w_in, w_in, w_out)


def _combine_kernel(tm, nt, final, dest_ref, x_ref, gate_ref, g_ref, fg_ref, ys_hbm, o_ref, y_sc, sem):
    i = pl.program_id(0)
    slot = i % 2

    def row_copy(s, k, r, src):
        return pltpu.make_async_copy(ys_hbm.at[pl.ds(src, 1)], y_sc.at[s, k, pl.ds(r, 1)], sem.at[s])

    def issue_tile(t, s):
        def issue(r, carry):
            a = 2 * (t * tm + r)
            row_copy(s, 0, r, dest_ref[a]).start()
            row_copy(s, 1, r, dest_ref[a + 1]).start()
            return carry
        lax.fori_loop(0, tm, issue, 0, unroll=DMA_UNROLL)

    @pl.when(i == 0)
    def _():
        issue_tile(0, 0)

    @pl.when(i + 1 < nt)
    def _():
        issue_tile(i + 1, 1 - slot)

    def drain(r, carry):
        row_copy(slot, 0, r, 0).wait()
        row_copy(slot, 1, r, 0).wait()
        return carry

    lax.fori_loop(0, tm, drain, 0, unroll=DMA_UNROLL)
    gate = gate_ref[...]
    out = x_ref[...] + g_ref[...] * (gate[:, 0:1] * y_sc[slot, 0] + gate[:, 1:2] * y_sc[slot, 1])
    if final:
        ms = jnp.mean(out * out, axis=-1, keepdims=True)
        out = out * lax.rsqrt(ms + RMS_EPS) * fg_ref[...]
    o_ref[...] = out


def _combine(lay, x, mods, gates, ys, dest, final_g):
    tm = lay.tile(256)
    nt = lay.T // tm
    final = final_g is not None
    fg = (final_g if final else jnp.ones((D,), F32)).reshape(1, D)
    grid_spec = pltpu.PrefetchScalarGridSpec(
        num_scalar_prefetch=1,
        grid=(nt,),
        in_specs=[pl.BlockSpec((tm, D), lambda i, d: (i, 0)),
                  pl.BlockSpec((tm, 2), lambda i, d: (i, 0)),
                  pl.BlockSpec((None, 1, D), lambda i, d: (lay.group(i * tm) * MOD_CHUNKS + 5, 0, 0)),
                  pl.BlockSpec((1, D), lambda i, d: (0, 0)),
                  pl.BlockSpec(memory_space=pl.ANY)],
        out_specs=pl.BlockSpec((tm, D), lambda i, d: (i, 0)),
        scratch_shapes=[pltpu.VMEM((2, 2, tm, D), F32), pltpu.SemaphoreType.DMA((2,))],
    )
    return pl.pallas_call(
        functools.partial(_combine_kernel, tm, nt, final),
        out_shape=jax.ShapeDtypeStruct((lay.T, D), F32),
        grid_spec=grid_spec,
        compiler_params=_cp("arbitrary"),
        name="combine",
    )(dest, x, gates, mods, fg, ys)


def _moe_layer(lay, x, mods, router_w, w_in, w_out, final_g=None, bm=MOE_BM):
    h, idx, gates = _router(lay, x, mods, router_w)
    dest, block_e, nvalid, nb = _moe_plan(idx, bm)
    xs = _dispatch(lay, h, dest, nb, bm)
    ys = _experts(xs, block_e, nvalid, nb, bm, w_in, w_out)
    return _combine(lay, x, mods, gates, ys, dest, final_g)


def kernel(x_prompt, x_sample, c, state_l0_s5_re, state_l0_s5_im, state_l2_gla, state_l3_ret, c_ctx, l0_mod_w, l0_mod_b, l0_s5_a_re, l0_s5_a_im, l0_s5_log_dt, l0_s5_b_re, l0_s5_b_im, l0_s5_c_re, l0_s5_c_im, l0_s5_d, l0_s5_glu_w, l0_ffn_w_in, l0_ffn_w_out, l1_mod_w, l1_mod_b, l1_hy_w_in, l1_hy_b_in, l1_hy_short_w, l1_hy_short_b, l1_hy_f_w1, l1_hy_f_b1, l1_hy_f_w2, l1_hy_f_b2, l1_hy_f_w3, l1_hy_f_freq, l1_hy_skip, l1_hy_w_out, l1_hy_b_out, l1_moe_router, l1_moe_w_in, l1_moe_w_out, l2_mod_w, l2_mod_b, l2_gla_w_in, l2_gla_gate_w1, l2_gla_gate_w2, l2_gla_gate_b, l2_gla_norm_g, l2_gla_w_out, l2_ffn_w_in, l2_ffn_w_out, l3_mod_w, l3_mod_b, l3_ret_w_in, l3_ret_log_decay, l3_ret_norm_g, l3_ret_w_out, l3_moe_router, l3_moe_w_in, l3_moe_w_out, final_norm_g):
    B, L, _ = x_prompt.shape
    NS, LS, _ = x_sample.shape
    lay = Layout(B, L, NS, LS)
    x = jnp.concatenate([x_prompt.reshape(B * L, D), x_sample.reshape(NS * LS, D)], axis=0)
    cond = jnp.concatenate([c_ctx[None], c, jnp.zeros((8 - 1 - NS, D), F32)], axis=0)
    mods0 = _mods(cond, l0_mod_w, l0_mod_b)
    p0 = dict(a_re=l0_s5_a_re, a_im=l0_s5_a_im, log_dt=l0_s5_log_dt, b_re=l0_s5_b_re, b_im=l0_s5_b_im,
              c_re=l0_s5_c_re, c_im=l0_s5_c_im, d=l0_s5_d, glu_w=l0_s5_glu_w.astype(BF16))
    x, s5_re, s5_im = _s5_layer(lay, x, mods0, p0, state_l0_s5_re, state_l0_s5_im)
    x = _ffn(lay, x, mods0, l0_ffn_w_in.astype(BF16), l0_ffn_w_out.astype(BF16))

    mods1 = _mods(cond, l1_mod_w, l1_mod_b)
    p1 = dict(w_in=l1_hy_w_in, b_in=l1_hy_b_in, short_w=l1_hy_short_w, short_b=l1_hy_short_b,
              f_w1=l1_hy_f_w1, f_b1=l1_hy_f_b1, f_w2=l1_hy_f_w2, f_b2=l1_hy_f_b2, f_w3=l1_hy_f_w3,
              f_freq=l1_hy_f_freq, skip=l1_hy_skip, w_out=l1_hy_w_out, b_out=l1_hy_b_out)
    x = _hyena_layer(lay, x, mods1, p1)
    x = _moe_layer(lay, x, mods1, l1_moe_router, l1_moe_w_in.astype(BF16), l1_moe_w_out.astype(BF16))

    mods2 = _mods(cond, l2_mod_w, l2_mod_b)
    p2 = dict(w_in=l2_gla_w_in, gate_w1=l2_gla_gate_w1, gate_w2=l2_gla_gate_w2, gate_b=l2_gla_gate_b,
              norm_g=l2_gla_norm_g, w_out=l2_gla_w_out)
    x, gla_state = _gla_layer(lay, x, mods2, p2, state_l2_gla)
    x = _ffn(lay, x, mods2, l2_ffn_w_in.astype(BF16), l2_ffn_w_out.astype(BF16))

    mods3 = _mods(cond, l3_mod_w, l3_mod_b)
    p3 = dict(w_in=l3_ret_w_in, log_decay=l3_ret_log_decay, norm_g=l3_ret_norm_g, w_out=l3_ret_w_out)
    x, ret_state = _ret_layer(lay, x, mods3, p3, state_l3_ret)
    y = _moe_layer(lay, x, mods3, l3_moe_router, l3_moe_w_in.astype(BF16), l3_moe_w_out.astype(BF16),
                   final_g=final_norm_g)
    return (y[:lay.TP].reshape(B, L, D), y[lay.TP:].reshape(NS, LS, D), s5_re, s5_im, gla_state, ret_state)
```

```python
import functools
import math

import jax
import jax.numpy as jnp
import numpy as np
from jax import lax
from jax.experimental import pallas as pl
from jax.experimental.pallas import tpu as pltpu

F32 = jnp.float32
BF16 = jnp.bfloat16
HIGHEST = lax.Precision.HIGHEST

D = 1024
RMS_EPS = 1e-6
MOD_CHUNKS = 6
GRID_W = 64

S5_Q = 16
S5_G = D // S5_Q
S5_P = 64
S5_T = 16
S5_SCAN_ROWS = 64

HY_BANDS = 16
HY_TARGET = 1e-2
HY_FAST_PCT = 0.3
HY_SLOW_PCT = 1.5

GLA_H, GLA_DK, GLA_DV = 4, 128, 256
GLA_RANK = 16
GLA_TAU = 16.0
RET_H, RET_DK, RET_DV = 4, 256, 512
CHUNK = 64
RET_CHUNK = 256
ROPE_BASE = 10000.0

FFN_DIM = 2816
N_EXPERTS = 8
EXPERT_DIM = 3584

VMEM_LIMIT_V7X = 56 * 1024 * 1024


def _cp(*sem):
    return pltpu.CompilerParams(dimension_semantics=sem, vmem_limit_bytes=VMEM_LIMIT_V7X)


def _silu(x):
    return x * jax.nn.sigmoid(x)


def _modulate(x, shift, scale):
    ms = jnp.mean(x * x, axis=-1, keepdims=True)
    return x * lax.rsqrt(ms + RMS_EPS) * (1.0 + scale) + shift


class Layout:
    def __init__(self, n_prompt, l_prompt, n_sample, l_sample):
        self.B, self.L, self.NS, self.LS = n_prompt, l_prompt, n_sample, l_sample
        self.TP = n_prompt * l_prompt
        self.T = self.TP + n_sample * l_sample

    def tile(self, want):
        t = math.gcd(math.gcd(self.TP, self.LS), want)
        assert t % 8 == 0
        return t

    def group(self, row):
        return jnp.where(row < self.TP, 0, 1 + (row - self.TP) // self.LS)


def _mod_spec(lay, tm, chunk, ngrid):
    def imap(*ids):
        return (lay.group(ids[0] * tm) * MOD_CHUNKS + chunk, 0, 0)
    del ngrid
    return pl.BlockSpec((None, 1, D), imap)


def _mods_kernel(c_ref, w_ref, b_ref, o_ref):
    o_ref[...] = jnp.dot(_silu(c_ref[...]), w_ref[...], precision=HIGHEST,
                         preferred_element_type=F32) + b_ref[...]


def _mods(cond, w, b):
    n = MOD_CHUNKS * D
    tn = 1536
    out = pl.pallas_call(
        _mods_kernel,
        out_shape=jax.ShapeDtypeStruct((8, n), F32),
        grid=(n // tn,),
        in_specs=[pl.BlockSpec((8, D), lambda j: (0, 0)),
                  pl.BlockSpec((D, tn), lambda j: (0, j)),
                  pl.BlockSpec((1, tn), lambda j: (0, j))],
        out_specs=pl.BlockSpec((8, tn), lambda j: (0, j)),
        compiler_params=_cp("arbitrary"),
        name="mods",
    )(cond, w, b.reshape(1, n))
    return out.reshape(8 * MOD_CHUNKS, 1, D)


def _modulate_kernel(x_ref, sh_ref, sc_ref, o_ref):
    o_ref[...] = _modulate(x_ref[...], sh_ref[...], sc_ref[...]).astype(o_ref.dtype)


def _modulate_call(lay, x, mods, c_shift, c_scale, dtype):
    tm = lay.tile(512)
    return pl.pallas_call(
        _modulate_kernel,
        out_shape=jax.ShapeDtypeStruct((lay.T, D), dtype),
        grid=(lay.T // tm,),
        in_specs=[pl.BlockSpec((tm, D), lambda i: (i, 0)),
                  _mod_spec(lay, tm, c_shift, 1), _mod_spec(lay, tm, c_scale, 1)],
        out_specs=pl.BlockSpec((tm, D), lambda i: (i, 0)),
        compiler_params=_cp("arbitrary"),
        name="modulate",
    )(x, mods, mods)


def _proj_kernel(x_ref, sh_ref, sc_ref, w_ref, b_ref, o_ref, u_sc):
    @pl.when(pl.program_id(1) == 0)
    def _():
        u_sc[...] = _modulate(x_ref[...], sh_ref[...], sc_ref[...]).astype(BF16)

    acc = jnp.dot(u_sc[...], w_ref[...], preferred_element_type=F32) + b_ref[...]
    o_ref[...] = acc.astype(o_ref.dtype)


def _proj(lay, x, mods, c_shift, c_scale, w, b, tn, out_dtype=BF16):
    tm = lay.tile(1024)
    n = w.shape[1]
    assert n % tn == 0
    return pl.pallas_call(
        _proj_kernel,
        out_shape=jax.ShapeDtypeStruct((lay.T, n), out_dtype),
        grid=(lay.T // tm, n // tn),
        in_specs=[pl.BlockSpec((tm, D), lambda i, j: (i, 0)),
                  _mod_spec(lay, tm, c_shift, 2), _mod_spec(lay, tm, c_scale, 2),
                  pl.BlockSpec((D, tn), lambda i, j: (0, j)),
                  pl.BlockSpec((1, tn), lambda i, j: (0, j))],
        out_specs=pl.BlockSpec((tm, tn), lambda i, j: (i, j)),
        scratch_shapes=[pltpu.VMEM((tm, D), BF16)],
        compiler_params=_cp("arbitrary", "arbitrary"),
        name="proj",
    )(x, mods, mods, w, b)


def _ffn_kernel(x_ref, sh_ref, sc_ref, g_ref, wa_ref, wb_ref, wo_ref, o_ref):
    x = x_ref[...]
    u = _modulate(x, sh_ref[...], sc_ref[...]).astype(BF16)
    a = jnp.dot(u, wa_ref[...], preferred_element_type=F32)
    b = jnp.dot(u, wb_ref[...], preferred_element_type=F32)
    h = (_silu(a) * b).astype(BF16)
    o_ref[...] = x + g_ref[...] * jnp.dot(h, wo_ref[...], preferred_element_type=F32)


def _ffn(lay, x, mods, w_in, w_out):
    tm = lay.tile(512)
    once = dict(pipeline_mode=pl.Buffered(1))
    return pl.pallas_call(
        _ffn_kernel,
        out_shape=jax.ShapeDtypeStruct((lay.T, D), F32),
        grid=(lay.T // tm,),
        in_specs=[pl.BlockSpec((tm, D), lambda i: (i, 0)),
                  _mod_spec(lay, tm, 3, 1), _mod_spec(lay, tm, 4, 1), _mod_spec(lay, tm, 5, 1),
                  pl.BlockSpec((D, FFN_DIM), lambda i: (0, 0), **once),
                  pl.BlockSpec((D, FFN_DIM), lambda i: (0, 1), **once),
                  pl.BlockSpec((FFN_DIM, D), lambda i: (0, 0), **once)],
        out_specs=pl.BlockSpec((tm, D), lambda i: (i, 0)),
        compiler_params=_cp("arbitrary"),
        name="ffn",
    )(x, mods, mods, mods, w_in, w_in, w_out)


def _s5_tables(a_re, a_im, log_dt, b_re, b_im, c_re, c_im, d_skip):
    T, G, P, Q = S5_T, S5_G, S5_P, S5_Q
    a = lax.complex(a_re, a_im)
    adt = a * jnp.exp(log_dt)[..., None]
    lam = jnp.exp(adt)
    bb = ((lam - 1.0) / a)[..., None] * lax.complex(b_re, b_im)
    cm = lax.complex(c_re, c_im)
    steps = jnp.arange(T + 1, dtype=F32)
    pw = jnp.exp(steps[None, :, None, None] * adt[:, None])
    kern = jnp.real(jnp.einsum('dgqp,djgp,dgpr->djgqr', cm, pw[:, :T], bb))
    s_i = jnp.arange(T)[:, None]
    t_i = jnp.arange(T)[None, :]
    kf = kern[0][jnp.clip(t_i - s_i, 0, T - 1)] * (t_i >= s_i)[..., None, None, None]
    kb = kern[1][jnp.clip(s_i - t_i, 0, T - 1)] * (s_i >= t_i)[..., None, None, None]
    m = jnp.transpose(kf + kb, (2, 0, 4, 1, 3))
    eye = (jnp.eye(T)[:, None, :, None] * jnp.eye(Q)[None, :, None, :])
    m = m + eye[None] * d_skip.reshape(G, 1, 1, 1, Q)
    m = m.reshape(G, T * Q, T * Q)
    e_f = pw[0][T - 1 - jnp.arange(T)]
    e_b = pw[1][jnp.arange(T)]
    n_f = e_f[..., None] * bb[0][None]
    n_b = e_b[..., None] * bb[1][None]
    n_c = jnp.concatenate([n_f, n_b], axis=2)
    n_c = jnp.transpose(n_c, (1, 0, 3, 2)).reshape(G, T * Q, 2 * P)
    lam_t = jnp.concatenate([pw[0][T], pw[1][T]], axis=-1)
    w_f = cm[0][:, None] * jnp.transpose(pw[0][1:T + 1], (1, 0, 2))[:, :, None, :]
    w_b = cm[1][:, None] * jnp.transpose(pw[1][T - jnp.arange(T)], (1, 0, 2))[:, :, None, :]
    w_f = jnp.transpose(w_f, (0, 3, 1, 2)).reshape(G, P, T * Q)
    w_b = jnp.transpose(w_b, (0, 3, 1, 2)).reshape(G, P, T * Q)
    z = jnp.zeros_like(jnp.real(w_f))
    c_mats = dict(c_f_re=jnp.concatenate([jnp.real(w_f), z], axis=1),
                  c_f_im=jnp.concatenate([-jnp.imag(w_f), z], axis=1),
                  c_b_re=jnp.concatenate([z, jnp.real(w_b)], axis=1),
                  c_b_im=jnp.concatenate([z, -jnp.imag(w_b)], axis=1))
    return dict(m=m.astype(BF16), n_re=jnp.real(n_c).astype(BF16), n_im=jnp.imag(n_c).astype(BF16),
                l_re=jnp.real(lam_t), l_im=jnp.imag(lam_t), **{k: v.astype(BF16) for k, v in c_mats.items()})


def _s5_in_kernel(u_ref, m_ref, nre_ref, nim_ref, yi_ref, sre_ref, sim_ref):
    u = u_ref[...]
    yi_ref[...] = jnp.dot(u, m_ref[...], preferred_element_type=F32)
    sre_ref[...] = jnp.dot(u, nre_ref[...], preferred_element_type=F32).reshape(sre_ref.shape)
    sim_ref[...] = jnp.dot(u, nim_ref[...], preferred_element_type=F32).reshape(sim_ref.shape)


def _s5_in(ug, tabs):
    G, R, W = ug.shape
    P2 = 2 * S5_P
    RB = S5_SCAN_ROWS
    assert R % RB == 0
    gspec = lambda n: pl.BlockSpec((None, W, n), lambda g: (g, 0, 0))
    rspec = pl.BlockSpec((None, R, W), lambda g: (g, 0, 0))
    sspec = pl.BlockSpec((R // RB, RB, P2), lambda g: (0, g, 0))
    sshape = jax.ShapeDtypeStruct((R // RB, G * RB, P2), F32)
    return pl.pallas_call(
        _s5_in_kernel,
        out_shape=(jax.ShapeDtypeStruct((G, R, W), F32), sshape, sshape),
        grid=(G,),
        in_specs=[rspec, gspec(W), gspec(P2), gspec(P2)],
        out_specs=(rspec, sspec, sspec),
        compiler_params=_cp("arbitrary"),
        name="s5_in",
    )(ug, tabs['m'], tabs['n_re'], tabs['n_im'])


def _s5_scan_kernel(nsb, ncb, nblk, sref_ref, simf_ref, sreb_ref, simb_ref, lre_ref, lim_ref,
                    h0re_ref, h0im_ref, *rest):
    hfre_ref, hfim_ref, hbre_ref, hbim_ref, fre_ref, fim_ref, cre_sc, cim_sc = rest[4:]
    P = S5_P
    rows = sref_ref.shape[0] // ncb
    j = pl.program_id(1)
    fwd = lax.broadcasted_iota(jnp.int32, (1, 2 * P), 1) < P
    lre = lre_ref[...]
    lim = lim_ref[...]

    @pl.when(j == 0)
    def _():
        cre_sc[...] = h0re_ref[...]
        cim_sc[...] = h0im_ref[...]

    def at(k):
        return pl.ds(k, rows, stride=ncb)

    def body(k, carry):
        hre, him = carry
        kb = ncb - 1 - k
        hfre_ref[at(k), :] = hre
        hfim_ref[at(k), :] = him
        hbre_ref[at(kb), :] = hre
        hbim_ref[at(kb), :] = him
        sre = jnp.where(fwd, sref_ref[at(k), :], sreb_ref[at(kb), :])
        sim = jnp.where(fwd, simf_ref[at(k), :], simb_ref[at(kb), :])
        return (lre * hre - lim * him + sre, lre * him + lim * hre + sim)

    hre, him = lax.fori_loop(0, ncb, body, (cre_sc[...], cim_sc[...]), unroll=4)
    cre_sc[...] = hre
    cim_sc[...] = him

    @pl.when(j == nblk - 1)
    def _():
        fre_ref[...] = hre
        fim_ref[...] = him


def _s5_scan(sre, sim, tabs, h0re, h0im, hprev, row0, nseq, nc, nsb, nblk):
    _, grb, P2 = sre.shape
    rb = S5_SCAN_ROWS
    G = grb // rb
    assert nblk == 1 or nsb == 1
    ncb = nc // nblk
    assert nsb * ncb == rb and row0 % rb == 0 and nseq % nsb == 0 and nc % nblk == 0
    b0 = row0 // rb
    fspec = pl.BlockSpec((None, grb, P2), lambda i, j: (b0 + i * nblk + j, 0, 0))
    bspec = pl.BlockSpec((None, grb, P2), lambda i, j: (b0 + i * nblk + nblk - 1 - j, 0, 0))
    lspec = pl.BlockSpec((G * nsb, P2), lambda i, j: (0, 0))
    qspec = pl.BlockSpec((None, G * nsb, P2), lambda i, j: (i, 0, 0))
    anyspec = pl.BlockSpec(memory_space=pl.ANY)
    fin = jax.ShapeDtypeStruct((nseq // nsb, G * nsb, P2), F32)
    rep = lambda a: jnp.repeat(a, nsb, axis=0)
    flat = lambda a: a.reshape(nseq // nsb, G * nsb, P2)
    outs = pl.pallas_call(
        functools.partial(_s5_scan_kernel, nsb, ncb, nblk),
        out_shape=tuple(jax.ShapeDtypeStruct(h.shape, h.dtype) for h in hprev) + (fin, fin),
        grid=(nseq // nsb, nblk),
        in_specs=[fspec, fspec, bspec, bspec, lspec, lspec, qspec, qspec] + [anyspec] * 4,
        out_specs=(fspec, fspec, bspec, bspec, qspec, qspec),
        scratch_shapes=[pltpu.VMEM((G * nsb, P2), F32), pltpu.VMEM((G * nsb, P2), F32)],
        input_output_aliases={8: 0, 9: 1, 10: 2, 11: 3},
        compiler_params=_cp("arbitrary", "arbitrary"),
        name="s5_scan",
    )(sre, sim, sre, sim, rep(tabs['l_re']), rep(tabs['l_im']), flat(h0re), flat(h0im), *hprev)
    return outs[:4], outs[4].reshape(nseq // nsb, G, nsb, P2), outs[5].reshape(nseq // nsb, G, nsb, P2)


def _s5_out_kernel(yi_ref, hfre_ref, hfim_ref, hbre_ref, hbim_ref, cfre_ref, cfim_ref, cbre_ref, cbim_ref,
                   y_ref):
    y = yi_ref[...]
    for h_ref, c_ref in ((hfre_ref, cfre_ref), (hfim_ref, cfim_ref), (hbre_ref, cbre_ref), (hbim_ref, cbim_ref)):
        h = h_ref[...].reshape(y.shape[0], h_ref.shape[-1])
        y += jnp.dot(h.astype(BF16), c_ref[...], preferred_element_type=F32)
    y_ref[...] = y.astype(y_ref.dtype)


def _s5_out(yi, hprev, tabs):
    G, R, W = yi.shape
    P2 = 2 * S5_P
    RB = S5_SCAN_ROWS
    gspec = pl.BlockSpec((None, P2, W), lambda g: (g, 0, 0))
    hspec = pl.BlockSpec((R // RB, RB, P2), lambda g: (0, g, 0))
    rspec = pl.BlockSpec((None, R, W), lambda g: (g, 0, 0))
    return pl.pallas_call(
        _s5_out_kernel,
        out_shape=jax.ShapeDtypeStruct((G, R, W), F32),
        grid=(G,),
        in_specs=[rspec] + [hspec] * 4 + [gspec] * 4,
        out_specs=rspec,
        compiler_params=_cp("arbitrary"),
        name="s5_out",
    )(yi, *hprev, tabs['c_f_re'], tabs['c_f_im'], tabs['c_b_re'], tabs['c_b_im'])


LANES = 128
S5_GB = LANES // S5_Q


def _block_transpose(sets):
    blk = lax.broadcasted_iota(jnp.int32, sets[0][0].shape, 1) // S5_Q
    sets = [list(regs) for regs in sets]
    d = S5_GB // 2
    while d:
        keep = (blk & d) == 0
        for regs in sets:
            for i in range(S5_GB):
                if i & d:
                    continue
                a, b = regs[i], regs[i + d]
                regs[i] = jnp.where(keep, a, pltpu.roll(b, d * S5_Q, 1))
                regs[i + d] = jnp.where(keep, pltpu.roll(a, LANES - d * S5_Q, 1), b)
        d //= 2
    return sets


def _s5_pre_kernel(tm, x_ref, sh_ref, sc_ref, ug_ref, u_sc):
    u = _modulate(x_ref[...], sh_ref[...], sc_ref[...])
    for j in range(D // LANES):
        u_sc[j] = u[:, j * LANES:(j + 1) * LANES]
    rows16 = 16
    nh = S5_T // S5_GB
    for c in range(tm // (S5_T * rows16)):
        base = c * S5_T * rows16
        for j in range(D // LANES):
            sets = [[u_sc[j, pl.ds(base + h * S5_GB + s, rows16, stride=S5_T), :] for s in range(S5_GB)]
                    for h in range(nh)]
            for h, regs in enumerate(_block_transpose(sets)):
                for gl, t in enumerate(regs):
                    ug_ref[j * S5_GB + gl, c * rows16:(c + 1) * rows16, h * LANES:(h + 1) * LANES] = t.astype(BF16)


def _s5_pre(lay, x, mods):
    tm = lay.tile(512)
    assert tm % (S5_T * 16) == 0
    return pl.pallas_call(
        functools.partial(_s5_pre_kernel, tm),
        out_shape=jax.ShapeDtypeStruct((S5_G, lay.T // S5_T, S5_T * S5_Q), BF16),
        grid=(lay.T // tm,),
        in_specs=[pl.BlockSpec((tm, D), lambda i: (i, 0)), _mod_spec(lay, tm, 0, 1), _mod_spec(lay, tm, 1, 1)],
        out_specs=pl.BlockSpec((S5_G, tm // S5_T, S5_T * S5_Q), lambda i: (0, i, 0)),
        scratch_shapes=[pltpu.VMEM((D // LANES, tm, LANES), F32)],
        compiler_params=_cp("arbitrary"),
        name="s5_pre",
    )(x, mods, mods)


def _s5_glu_kernel(tm, x_ref, yg_ref, g_ref, wv_ref, wg_ref, o_ref, a_sc, y_sc):
    @pl.when(pl.program_id(1) == 0)
    def _():
        def sub_tile(c, carry):
            crow = pl.ds(pl.multiple_of(c * 8, 8), 8)
            base = c * (S5_T * 8)
            nh = S5_T // S5_GB
            for j2 in range(0, D // LANES, 2):
                keys = [(j, h) for j in (j2, j2 + 1) for h in range(nh)]
                sets = [[yg_ref[j * S5_GB + gl, crow, h * LANES:(h + 1) * LANES] for gl in range(S5_GB)]
                        for j, h in keys]
                for (j, h), regs in zip(keys, _block_transpose(sets)):
                    for s, t in enumerate(regs):
                        y_sc[j, pl.ds(base + h * S5_GB + s, 8, stride=S5_T), :] = t
            return carry

        lax.fori_loop(0, tm // (S5_T * 8), sub_tile, 0)
        for j in range(D // LANES):
            a_sc[:, j * LANES:(j + 1) * LANES] = jax.nn.gelu(y_sc[j]).astype(BF16)

    a = a_sc[...]
    val = jnp.dot(a, wv_ref[...], preferred_element_type=F32)
    gate = jnp.dot(a, wg_ref[...], preferred_element_type=F32)
    o_ref[...] = x_ref[...] + g_ref[...] * (val * jax.nn.sigmoid(gate))


def _s5_glu(lay, x, yg, mods, glu_w):
    tm = lay.tile(1024)
    assert tm % (S5_T * 8) == 0
    tn = 512
    nn = D // tn
    return pl.pallas_call(
        functools.partial(_s5_glu_kernel, tm),
        out_shape=jax.ShapeDtypeStruct((lay.T, D), F32),
        grid=(lay.T // tm, nn),
        in_specs=[pl.BlockSpec((tm, tn), lambda i, j: (i, j)),
                  pl.BlockSpec((S5_G, tm // S5_T, S5_T * S5_Q), lambda i, j: (0, i, 0)),
                  pl.BlockSpec((None, 1, tn), lambda i, j: (lay.group(i * tm) * MOD_CHUNKS + 2, 0, j)),
                  pl.BlockSpec((D, tn), lambda i, j: (0, j)),
                  pl.BlockSpec((D, tn), lambda i, j: (0, nn + j))],
        out_specs=pl.BlockSpec((tm, tn), lambda i, j: (i, j)),
        scratch_shapes=[pltpu.VMEM((tm, D), BF16), pltpu.VMEM((D // LANES, tm, LANES), F32)],
        compiler_params=_cp("arbitrary", "arbitrary"),
        name="s5_glu",
    )(x, yg, mods, glu_w, glu_w)


def _s5_layer(lay, x, mods, p, h0_re, h0_im):
    T, G, P, Q = S5_T, S5_G, S5_P, S5_Q
    tabs = _s5_tables(p['a_re'], p['a_im'], p['log_dt'], p['b_re'], p['b_im'], p['c_re'], p['c_im'], p['d'])
    R = lay.T // T
    ug = _s5_pre(lay, x, mods)
    yi, sre, sim = _s5_in(ug, tabs)
    hprev = tuple(jnp.zeros(sre.shape, F32) for _ in range(4))
    ncp, ncs = lay.L // T, lay.LS // T
    nsb = S5_SCAN_ROWS // ncp
    zero = jnp.zeros((lay.B // nsb, G, nsb, 2 * P), F32)
    hprev, fre, fim = _s5_scan(sre, sim, tabs, zero, zero, hprev, 0, lay.B, ncp, nsb, 1)
    to_lanes = lambda s: jnp.transpose(s, (0, 2, 1, 3)).reshape(lay.NS, G, 1, 2 * P)
    hprev, _, _ = _s5_scan(sre, sim, tabs, to_lanes(h0_re), to_lanes(h0_im), hprev,
                           lay.TP // T, lay.NS, ncs, 1, max(1, ncs // S5_SCAN_ROWS))
    yg = _s5_out(yi, hprev, tabs)
    x = _s5_glu(lay, x, yg, mods, p['glu_w'])
    from_lanes = lambda s: jnp.transpose(s, (0, 2, 1, 3)).reshape(lay.B, G, 2, P).transpose(0, 2, 1, 3)
    return x, from_lanes(fre), from_lanes(fim)


def _hyena_filters(L, p):
    mm = functools.partial(jnp.matmul, precision=HIGHEST)
    f = jnp.linspace(1e-4, HY_BANDS - 1, HY_BANDS, dtype=F32)[None, :]
    max_decay = math.log(HY_TARGET) / HY_FAST_PCT
    min_decay = math.log(HY_TARGET) / HY_SLOW_PCT
    deltas = jnp.abs(jnp.linspace(min_decay, max_decay, D, dtype=F32))
    w3 = p['f_w3'].reshape(-1, 2, 2, D)

    def side(pos, s):
        t = (pos.astype(F32) / (L - 1))[:, None]
        w = 2.0 * math.pi * pos.astype(F32)[:, None] / L
        feats = jnp.concatenate([t, jnp.cos(f * w), -jnp.sin(f * w)], axis=-1)
        z = jnp.sin(p['f_freq'][0] * (mm(feats, p['f_w1']) + p['f_b1']))
        z = jnp.sin(p['f_freq'][1] * (mm(z, p['f_w2']) + p['f_b2']))
        win = jnp.exp(-t * deltas)
        return jnp.stack([mm(z, w3[:, o, s]) * win for o in range(2)])

    j = jnp.arange(L, dtype=jnp.int32)
    k_lo = side(j, 0)
    k_hi = side((L - j) % L, 1) * (j > 0).astype(F32)[None, :, None]
    norm = jnp.sum(jnp.abs(k_lo), axis=1, keepdims=True) + jnp.sum(jnp.abs(k_hi), axis=1, keepdims=True)
    k_lo, k_hi = k_lo / norm, k_hi / norm
    alt = (1.0 - 2.0 * (j % 2).astype(F32))[None, :, None]
    k_ny = jnp.sum(alt * (k_lo + k_hi), axis=1) / (2 * L)
    return k_lo, k_hi, k_ny


def _dft_tables(L):
    r = math.isqrt(L)
    assert r * r == L
    t = jnp.arange(L, dtype=jnp.int32)[None, :]
    a = jnp.arange(r, dtype=jnp.int32)[:, None]

    def unit(idx):
        ang = (idx % (2 * L)).astype(F32) * (math.pi / L)
        return jnp.cos(ang), jnp.sin(ang)

    c1, s1 = unit(a * r * t)
    c2, s2 = unit(a * t)
    c1, s1, c2, s2 = c1[:, None], s1[:, None], c2[None], s2[None]
    cos = (c1 * c2 - s1 * s2).reshape(L, L)
    sin = (s1 * c2 + c1 * s2).reshape(L, L)
    return cos.astype(BF16), sin.astype(BF16)


def _hy_spec_kernel(L, tr, c_ref, s_ref, klo_ref, khi_ref, p_ref, q_ref):
    r = pl.program_id(2)
    f = r * tr + lax.broadcasted_iota(jnp.int32, (tr, 1), 0)
    sgn = (1 - 2 * (f % 2)).astype(F32)
    scale = jnp.where(f == 0, 1.0, 2.0) * (1.0 / (2 * L))
    c, s = c_ref[...], s_ref[...]
    lo, hi = klo_ref[...], khi_ref[...]
    dot = functools.partial(jnp.dot, preferred_element_type=F32)
    p_ref[...] = scale * (dot(c, lo) + sgn * dot(c, hi))
    q_ref[...] = scale * (dot(s, lo) + sgn * dot(s, hi))


def _hy_spectrum(L, cos, sin, k_lo, k_hi):
    tr = min(L, 512)
    tc = 512
    kspec = pl.BlockSpec((None, L, tc), lambda o, j, r: (o, 0, j))
    tspec = pl.BlockSpec((tr, L), lambda o, j, r: (r, 0))
    ospec = pl.BlockSpec((None, tr, tc), lambda o, j, r: (o, r, j))
    return pl.pallas_call(
        functools.partial(_hy_spec_kernel, L, tr),
        out_shape=(jax.ShapeDtypeStruct((2, L, D), F32), jax.ShapeDtypeStruct((2, L, D), F32)),
        grid=(2, D // tc, L // tr),
        in_specs=[tspec, tspec, kspec, kspec],
        out_specs=(ospec, ospec),
        compiler_params=_cp("arbitrary", "arbitrary", "arbitrary"),
        name="hy_spectrum",
    )(cos, sin, k_lo.astype(BF16), k_hi.astype(BF16))


def _hy_core_kernel(L, tr, ngrp, tc, x1_ref, x2_ref, v_ref, sw1_ref, sw2_ref, swv_ref, sb1_ref, sb2_ref,
                    sbv_ref, c_ref, s_ref, p_ref, q_ref, kny_ref, skip_ref, *rest):
    o_ref, z0_sc, z1_sc, x2_sc, a_sc, b_sc, ny0_sc, ny1_sc = rest[-8:]
    fused = tr == L
    ph = pl.program_id(2)
    r = pl.program_id(3)
    W = ngrp * tc

    def phase(k):
        return (lambda f: f()) if fused else pl.when(ph == k)
    tcv = min(L, 512)
    halo = 16

    def alt_sign(start, n):
        t = start + lax.broadcasted_iota(jnp.int32, (n, 1), 0)
        return (1 - 2 * (t % 2)).astype(F32)

    def conv3(src_ref, g, a, w_ref, b_ref):
        x = src_ref[g, pl.ds(a, tcv), :].astype(F32)
        row = lax.broadcasted_iota(jnp.int32, (tcv, 1), 0)
        up_at = pl.multiple_of(jnp.maximum(a - halo, 0), halo)
        dn_at = pl.multiple_of(jnp.minimum(a + tcv, L - halo), halo)
        up = src_ref[g, pl.ds(up_at, halo), :][halo - 1:halo, :].astype(F32)
        dn = src_ref[g, pl.ds(dn_at, halo), :][0:1, :].astype(F32)
        up = jnp.where(a > 0, up, 0.0)
        dn = jnp.where(a + tcv < L, dn, 0.0)
        prev = jnp.where(row == 0, up, pltpu.roll(x, 1, 0))
        nxt = jnp.where(row == tcv - 1, dn, pltpu.roll(x, tcv - 1, 0))
        return prev * w_ref[0:1, :] + x * w_ref[1:2, :] + nxt * w_ref[2:3, :] + b_ref[...]

    @(phase(0) if fused else pl.when(jnp.logical_and(ph == 0, r == 0)))
    def _():
        ny0_sc[...] = jnp.zeros_like(ny0_sc)

        def conv_tile(ti, carry):
            a = pl.multiple_of(ti * tcv, tcv)
            rows_a = pl.ds(a, tcv)
            for g in range(ngrp):
                cols = slice(g * tc, (g + 1) * tc)
                z1_sc[rows_a, cols] = conv3(x1_ref, g, a, sw1_ref, sb1_ref).astype(BF16)
                x2_sc[rows_a, cols] = conv3(x2_ref, g, a, sw2_ref, sb2_ref).astype(BF16)
                z0_sc[rows_a, cols] = conv3(v_ref, g, a, swv_ref, sbv_ref).astype(BF16)
            ny0_sc[...] += jnp.sum(alt_sign(a, tcv) * z0_sc[rows_a, :].astype(F32), axis=0, keepdims=True)
            return carry

        lax.fori_loop(0, L // tcv, conv_tile, 0)

    start = pl.multiple_of(r * tr, tr)
    rows = pl.ds(start, tr)
    dot = functools.partial(jnp.dot, preferred_element_type=F32)

    def forward(order, z_sc):
        z = z_sc[...]
        zre = dot(c_ref[...], z)
        zim = dot(s_ref[...], z)
        pw = jnp.concatenate([p_ref[order] if fused else p_ref[...]] * ngrp, axis=1)
        qw = jnp.concatenate([q_ref[order] if fused else q_ref[...]] * ngrp, axis=1)
        a_sc[rows, :] = (zre * pw - zim * qw).astype(BF16)
        b_sc[rows, :] = (zim * pw + zre * qw).astype(BF16)

    def inverse(order, z_sc, ny_sc):
        y = dot(c_ref[...], a_sc[...]) + dot(s_ref[...], b_sc[...])
        kny = jnp.concatenate([kny_ref[order:order + 1, :]] * ngrp, axis=1)
        skip = jnp.concatenate([skip_ref[order:order + 1, :]] * ngrp, axis=1)
        return y + alt_sign(start, tr) * (ny_sc[...] * kny) + skip * z_sc[rows, :].astype(F32)

    @phase(0)
    def _():
        forward(0, z0_sc)

    @phase(1)
    def _():
        z1 = (z1_sc[rows, :].astype(F32) * inverse(0, z0_sc, ny0_sc)).astype(BF16)
        z1_sc[rows, :] = z1

        @pl.when(r == 0)
        def _():
            ny1_sc[...] = jnp.zeros_like(ny1_sc)

        ny1_sc[...] += jnp.sum(alt_sign(start, tr) * z1.astype(F32), axis=0, keepdims=True)

    @phase(2)
    def _():
        forward(1, z1_sc)

    @phase(3)
    def _():
        out = x2_sc[rows, :].astype(F32) * inverse(1, z1_sc, ny1_sc)
        for g in range(ngrp):
            o_ref[g, rows, :] = out[:, g * tc:(g + 1) * tc].astype(o_ref.dtype)


def _hy_core(proj, short_w, short_b, skip, cos, sin, pq, k_ny, o_prev, row0, nseq, L, ngrp, tc):
    T = proj.shape[0]
    tr = min(L, 256)
    nrt = L // tr
    assert row0 % (L * ngrp) == 0 and nseq % ngrp == 0 and T % L == 0
    sb0 = row0 // (L * ngrp)
    nct = D // tc
    p3 = proj.reshape(T // L, L, 3 * D)
    p_arr, q_arr = pq

    def xspec(part):
        mode = {} if nrt == 1 else dict(pipeline_mode=pl.Buffered(1))
        return pl.BlockSpec((ngrp, L, tc), lambda i, j, ph, r: (sb0 + i, 0, part * nct + j), **mode)

    def wspec(part, rows_):
        return pl.BlockSpec((rows_, tc), lambda i, j, ph, r: (0, part * nct + j))

    fused = nrt == 1
    if fused:
        pq_spec = pl.BlockSpec((2, tr, tc), lambda i, j, ph, r: (0, 0, j))
    else:
        pq_spec = pl.BlockSpec((None, tr, tc),
                               lambda i, j, ph, r: (ph // 2, jnp.where(ph % 2 == 0, r, nrt - 1), j))

    tspec = pl.BlockSpec((tr, L), lambda i, j, ph, r: (r, 0))
    in_specs = [xspec(0), xspec(1), xspec(2), wspec(0, 3), wspec(1, 3), wspec(2, 3),
                wspec(0, 1), wspec(1, 1), wspec(2, 1), tspec, tspec, pq_spec, pq_spec,
                pl.BlockSpec((2, tc), lambda i, j, ph, r: (0, j)),
                pl.BlockSpec((2, tc), lambda i, j, ph, r: (0, j))]
    sb = short_b.reshape(1, 3 * D)
    args = [p3, p3, p3, short_w, short_w, short_w, sb, sb, sb, cos, sin, p_arr, q_arr, k_ny, skip]
    aliases = {}
    if o_prev is not None:
        in_specs.append(pl.BlockSpec(memory_space=pl.ANY))
        args.append(o_prev.reshape(T // L, L, D))
        aliases = {len(args) - 1: 0}
    W = ngrp * tc
    out = pl.pallas_call(
        functools.partial(_hy_core_kernel, L, tr, ngrp, tc),
        out_shape=jax.ShapeDtypeStruct((T // L, L, D), BF16),
        grid=(nseq // ngrp, nct, 1 if fused else 4, nrt),
        in_specs=in_specs,
        out_specs=pl.BlockSpec((ngrp, L, tc), lambda i, j, ph, r: (sb0 + i, 0, j)),
        scratch_shapes=[pltpu.VMEM((L, W), BF16)] * 5 + [pltpu.VMEM((1, W), F32)] * 2,
        input_output_aliases=aliases,
        compiler_params=_cp("arbitrary", "arbitrary", "arbitrary", "arbitrary"),
        name="hy_core",
    )(*args)
    return out.reshape(T, D)


def _plain_out_kernel(x_ref, z_ref, g_ref, w_ref, b_ref, o_ref):
    acc = jnp.dot(z_ref[...], w_ref[...], preferred_element_type=F32) + b_ref[...]
    o_ref[...] = x_ref[...] + g_ref[...] * acc


def _plain_out(lay, x, mods, z, w, b):
    tm = lay.tile(1024)
    tn = 512
    kdim = z.shape[1]
    return pl.pallas_call(
        _plain_out_kernel,
        out_shape=jax.ShapeDtypeStruct((lay.T, D), F32),
        grid=(lay.T // tm, D // tn),
        in_specs=[pl.BlockSpec((tm, tn), lambda i, j: (i, j)),
                  pl.BlockSpec((tm, kdim), lambda i, j: (i, 0)),
                  pl.BlockSpec((None, 1, tn), lambda i, j: (lay.group(i * tm) * MOD_CHUNKS + 2, 0, j)),
                  pl.BlockSpec((kdim, tn), lambda i, j: (0, j)),
                  pl.BlockSpec((1, tn), lambda i, j: (0, j))],
        out_specs=pl.BlockSpec((tm, tn), lambda i, j: (i, j)),
        compiler_params=_cp("arbitrary", "arbitrary"),
        name="plain_out",
    )(x, z, mods, w, b.reshape(1, D))


def _hyena_layer(lay, x, mods, p):
    proj = _proj(lay, x, mods, 0, 1, p['w_in'].astype(BF16), p['b_in'].reshape(1, 3 * D), 768)
    z = jnp.zeros((lay.T, D), BF16)
    for row0, nseq, L, ngrp, tc in ((0, lay.B, lay.L, math.gcd(lay.B, 4), 256),
                                    (lay.TP, lay.NS, lay.LS, lay.NS, 256)):
        k_lo, k_hi, k_ny = _hyena_filters(L, p)
        cos, sin = _dft_tables(L)
        pq = _hy_spectrum(L, cos, sin, k_lo, k_hi)
        z = _hy_core(proj, p['short_w'], p['short_b'], p['skip'], cos, sin, pq, k_ny, z, row0, nseq, L, ngrp, tc)
    return _plain_out(lay, x, mods, z, p['w_out'].astype(BF16), p['b_out'])


_NT = (((1,), (1,)), ((), ()))
_TN = (((0,), (0,)), ((), ()))


def _tri(dr):
    t = lax.broadcasted_iota(jnp.int32, (CHUNK, CHUNK), 0)
    s = lax.broadcasted_iota(jnp.int32, (CHUNK, CHUNK), 1)
    return (s <= t) if dr == 0 else (s >= t)


def _chunk_cumsum(g, dr):
    n = g.shape[0]
    pos = lax.broadcasted_iota(jnp.int32, g.shape, 0) % CHUNK
    sh = 1
    while sh < CHUNK:
        if dr == 0:
            g = g + jnp.where(pos >= sh, pltpu.roll(g, sh, 0), 0.0)
        else:
            g = g + jnp.where(pos < CHUNK - sh, pltpu.roll(g, n - sh, 0), 0.0)
        sh *= 2
    return g


def _head_epilogue(o_sc, gate_ref, ng_ref, a_ref, center):
    rows = o_sc.shape[0]
    tr = math.gcd(rows, 256)

    def tile(i, carry):
        r = pl.ds(pl.multiple_of(i * tr, tr), tr)
        o = o_sc[r, :]
        if center:
            o = o - jnp.mean(o, axis=-1, keepdims=True)
        o = o * lax.rsqrt(jnp.mean(o * o, axis=-1, keepdims=True) + RMS_EPS) * ng_ref[...]
        a_ref[r, :] = (o * _silu(gate_ref[r, :].astype(F32))).astype(a_ref.dtype)
        return carry

    lax.fori_loop(0, rows // tr, tile, 0)


def _gla_kernel(cps, nseg, U, has_s0, want_final, *refs):
    q_ref, k_ref, v_ref, lr_ref, w2f_ref, w2b_ref, gb_ref, gate_ref, ng_ref = refs[:9]
    s0_ref = refs[9] if has_s0 else None
    qin_sc, kin_sc, kout_sc, dec_sc, st_sc, s_sc, s0t_sc, o_ref = refs[-8:]
    outs = refs[-10:-8] if want_final else refs[-9:-8]
    a_ref = outs[0]
    sf_ref = outs[1] if want_final else None
    C = CHUNK
    nsc = cps // U
    nchunks = nseg * cps
    rows_total = nchunks * C
    w2 = (w2f_ref, w2b_ref)

    for dr in range(2):
        pre = jnp.dot(lr_ref[...], w2[dr][...], preferred_element_type=F32) + gb_ref[dr:dr + 1, :]
        g = (jnp.minimum(pre, 0.0) - jnp.log(1.0 + jnp.exp(-jnp.abs(pre)))) * (1.0 / GLA_TAU)
        b = _chunk_cumsum(g, dr)
        b3 = b.reshape(nchunks, C, GLA_DK)
        tot = b3[:, C - 1:C, :] if dr == 0 else b3[:, 0:1, :]
        dec_sc[...] = jnp.exp(tot).reshape(nchunks, GLA_DK)
        k = k_ref[...].astype(F32)
        qin_sc[...] = (q_ref[...].astype(F32) * (GLA_DK ** -0.5) * jnp.exp(b)).astype(BF16)
        kin_sc[...] = (k * jnp.exp(-b)).astype(BF16)
        kout_sc[...] = (k * jnp.exp(tot - b3).reshape(rows_total, GLA_DK)).astype(BF16)
        if has_s0:
            s0t_sc[...] = jnp.transpose(s0_ref[dr], (1, 0))
        tri = _tri(dr)

        def super_chunk(jj, carry, dr=dr, tri=tri):
            j = jj if dr == 0 else nseg * nsc - 1 - jj
            in_seg = j % nsc
            first = (in_seg == 0) if dr == 0 else (in_seg == nsc - 1)
            last = (in_seg == nsc - 1) if dr == 0 else (in_seg == 0)

            @pl.when(first)
            def _():
                s_sc[...] = s0t_sc[...] if has_s0 else jnp.zeros_like(s_sc)

            base = j * (U * C)
            for u in range(U):
                rows = pl.ds(pl.multiple_of(base + u * C, C), C)
                v = v_ref[rows, :]
                sc = lax.dot_general(qin_sc[rows, :], kin_sc[rows, :], _NT, preferred_element_type=F32)
                o = jnp.dot(jnp.where(tri, sc, 0.0).astype(BF16), v, preferred_element_type=F32)
                st_sc[u] = lax.dot_general(v, kout_sc[rows, :], _TN, preferred_element_type=F32)
                if dr == 0:
                    o_ref[rows, :] = o
                else:
                    o_ref[rows, :] += o
            s = s_sc[...]
            for u in (range(U) if dr == 0 else reversed(range(U))):
                kv = st_sc[u]
                st_sc[u] = s
                s = dec_sc[pl.ds(j * U + u, 1), :] * s + kv
            s_sc[...] = s
            for u in range(U):
                rows = pl.ds(pl.multiple_of(base + u * C, C), C)
                o_ref[rows, :] += lax.dot_general(qin_sc[rows, :], st_sc[u].astype(BF16), _NT,
                                                  preferred_element_type=F32)
            if want_final:
                @pl.when(last)
                def _():
                    sf_ref[j // nsc, dr] = jnp.transpose(s, (1, 0))
            return carry

        lax.fori_loop(0, nseg * nsc, super_chunk, 0)

    _head_epilogue(o_ref, gate_ref, ng_ref, a_ref, center=False)


def _gla_core(proj, w2f, w2b, gate_b, norm_g, s0, o_prev, row0, nseq, seqlen, nseg, want_final):
    T = proj.shape[0]
    rows = nseg * seqlen
    cps = seqlen // CHUNK
    U = math.gcd(cps, 8)
    assert row0 % rows == 0 and seqlen % CHUNK == 0 and nseq % nseg == 0
    rb = row0 // rows
    hk = GLA_H * GLA_DK
    has_s0 = s0 is not None
    assert not has_s0 or nseg == 1
    in_specs = [pl.BlockSpec((rows, GLA_DK), lambda b, h: (rb + b, h)),
                pl.BlockSpec((rows, GLA_DK), lambda b, h: (rb + b, GLA_H + h)),
                pl.BlockSpec((rows, GLA_DV), lambda b, h: (rb + b, 2 * hk // GLA_DV + h)),
                pl.BlockSpec((rows, 128), lambda b, h: (rb + b, (2 * hk + 2 * GLA_H * GLA_DV) // 128)),
                pl.BlockSpec((128, GLA_DK), lambda b, h: (0, h)),
                pl.BlockSpec((128, GLA_DK), lambda b, h: (0, h)),
                pl.BlockSpec((2, GLA_DK), lambda b, h: (0, h)),
                pl.BlockSpec((rows, GLA_DV), lambda b, h: (rb + b, (2 * hk) // GLA_DV + GLA_H + h)),
                pl.BlockSpec((1, GLA_DV), lambda b, h: (0, 0))]
    args = [proj, proj, proj, proj, w2f, w2b, gate_b, proj, norm_g.reshape(1, GLA_DV)]
    if has_s0:
        in_specs.append(pl.BlockSpec((None, 2, None, GLA_DK, GLA_DV), lambda b, h: (b, 0, h, 0, 0)))
        args.append(s0)
    in_specs.append(pl.BlockSpec(memory_space=pl.ANY))
    args.append(o_prev)
    aliases = {len(args) - 1: 0}
    out_shape = [jax.ShapeDtypeStruct((T, GLA_H * GLA_DV), BF16)]
    out_specs = [pl.BlockSpec((rows, GLA_DV), lambda b, h: (rb + b, h))]
    if want_final:
        out_shape.append(jax.ShapeDtypeStruct((nseq, 2, GLA_H, GLA_DK, GLA_DV), F32))
        out_specs.append(pl.BlockSpec((nseg, 2, None, GLA_DK, GLA_DV), lambda b, h: (b, 0, h, 0, 0)))
    outs = pl.pallas_call(
        functools.partial(_gla_kernel, cps, nseg, U, has_s0, want_final),
        out_shape=tuple(out_shape),
        grid=(nseq // nseg, GLA_H),
        in_specs=in_specs,
        out_specs=tuple(out_specs),
        scratch_shapes=[pltpu.VMEM((rows, GLA_DK), BF16)] * 3
        + [pltpu.VMEM((nseg * cps, GLA_DK), F32), pltpu.VMEM((U, GLA_DV, GLA_DK), F32),
           pltpu.VMEM((GLA_DV, GLA_DK), F32), pltpu.VMEM((GLA_DV, GLA_DK), F32),
           pltpu.VMEM((rows, GLA_DV), F32)],
        input_output_aliases=aliases,
        compiler_params=_cp("arbitrary", "arbitrary"),
        name="gla",
    )(*args)
    return (outs[0], outs[1]) if want_final else (outs[0], None)


def _ret_kernel(cps, nseg, U, has_s0, want_final, rope, *refs):
    q_ref, k_ref, v_ref, dm_ref, qd_ref, kd_ref, cd_ref, gate_ref, ng_ref = refs[:9]
    nxt = 9
    if rope:
        cos_ref, sin_ref = refs[9:11]
        nxt = 11
    s0_ref = refs[nxt] if has_s0 else None
    qr_sc, kr_sc, qd_sc, kd_sc, st_sc, s_sc, o_ref = refs[-7:]
    outs = refs[-9:-7] if want_final else refs[-8:-7]
    a_ref = outs[0]
    sf_ref = outs[1] if want_final else None
    C = RET_CHUNK
    nsc = cps // U
    R = U * C
    SB = 64

    def rot(x, rows):
        if not rope:
            return x
        half = x.shape[1] // 2
        swapped = jnp.concatenate([pltpu.roll(x[:, :half], half // 2, 1),
                                   pltpu.roll(x[:, half:], half // 2, 1)], axis=1)
        return x * cos_ref[rows, :] + swapped * sin_ref[rows, :]

    for dr in range(2):
        def super_chunk(jj, carry, dr=dr):
            j = jj if dr == 0 else nseg * nsc - 1 - jj
            in_seg = j % nsc
            first = (in_seg == 0) if dr == 0 else (in_seg == nsc - 1)
            last = (in_seg == nsc - 1) if dr == 0 else (in_seg == 0)

            @pl.when(first)
            def _():
                s_sc[...] = s0_ref[dr] if has_s0 else jnp.zeros_like(s_sc)

            base = pl.multiple_of(j * R, R)
            rows_r = pl.ds(base, R)
            q = rot(q_ref[rows_r, :].astype(F32), rows_r)
            k = rot(k_ref[rows_r, :].astype(F32), rows_r) * (RET_DK ** -0.5)
            qr_sc[...] = q.astype(BF16)
            kr_sc[...] = k.astype(BF16)
            qd_sc[...] = (q.reshape(U, C, RET_DK) * qd_ref[dr][None]).reshape(R, RET_DK).astype(BF16)
            kd_sc[...] = (k.reshape(U, C, RET_DK) * kd_ref[dr][None]).reshape(R, RET_DK).astype(BF16)
            for u in range(U):
                loc = pl.ds(u * C, C)
                rows = pl.ds(pl.multiple_of(base + u * C, C), C)
                v = v_ref[rows, :]
                sc = lax.dot_general(qr_sc[loc, :], kr_sc[loc, :], _NT, preferred_element_type=F32)
                o = jnp.dot((sc * dm_ref[dr]).astype(BF16), v, preferred_element_type=F32)
                st_sc[u] = lax.dot_general(kd_sc[loc, :], v, _TN, preferred_element_type=F32)
                if dr == 0:
                    o_ref[rows, :] = o
                else:
                    o_ref[rows, :] += o
            cd = cd_ref[dr]
            for r0 in range(0, RET_DK, SB):
                srows = pl.ds(r0, SB)
                s = s_sc[srows, :]
                for u in (range(U) if dr == 0 else reversed(range(U))):
                    kv = st_sc[u, srows, :]
                    st_sc[u, srows, :] = s
                    s = cd * s + kv
                s_sc[srows, :] = s
            for u in range(U):
                rows = pl.ds(pl.multiple_of(base + u * C, C), C)
                o_ref[rows, :] += jnp.dot(qd_sc[pl.ds(u * C, C), :], st_sc[u].astype(BF16),
                                          preferred_element_type=F32)
            if want_final:
                @pl.when(last)
                def _():
                    sf_ref[j // nsc, dr] = s_sc[...]
            return carry

        lax.fori_loop(0, nseg * nsc, super_chunk, 0)

    _head_epilogue(o_ref, gate_ref, ng_ref, a_ref, center=True)


def _ret_tables(log_decay):
    C = RET_CHUNK
    lg = log_decay.astype(F32)[:, :, None, None]
    t = jnp.arange(C, dtype=F32)[:, None]
    s = jnp.arange(C, dtype=F32)[None, :]
    lag = jnp.stack([t - s, s - t])[:, None]
    dmask = jnp.where(lag >= 0, jnp.exp(jnp.maximum(lag, 0.0) * lg), 0.0)
    tl = jnp.arange(C, dtype=F32)[None, None, :, None]
    qdec = jnp.concatenate([jnp.exp((tl + 1.0) * lg[0:1]), jnp.exp((C - tl) * lg[1:2])], axis=0)
    kdec = jnp.concatenate([jnp.exp((C - 1.0 - tl) * lg[0:1]), jnp.exp(tl * lg[1:2])], axis=0)
    cdec = jnp.exp(C * lg)
    return dmask, qdec, kdec, cdec


def _rope_tables(seqlen, dk):
    half = dk // 2
    nf = half // 2
    pos = jnp.arange(seqlen, dtype=jnp.int32)
    inv = ROPE_BASE ** (-jnp.arange(nf, dtype=F32) / nf)
    ang_r = (pos // GRID_W).astype(F32)[:, None] * inv[None, :]
    ang_c = (pos % GRID_W).astype(F32)[:, None] * inv[None, :]
    cos = jnp.concatenate([jnp.cos(ang_r)] * 2 + [jnp.cos(ang_c)] * 2, axis=1)
    sin = jnp.concatenate([-jnp.sin(ang_r), jnp.sin(ang_r), -jnp.sin(ang_c), jnp.sin(ang_c)], axis=1)
    return cos, sin


def _ret_core(proj, tabs, norm_g, s0, o_prev, row0, nseq, seqlen, nseg, want_final, rope):
    T = proj.shape[0]
    rows = nseg * seqlen
    C = RET_CHUNK
    cps = seqlen // C
    U = math.gcd(cps, 4)
    assert row0 % rows == 0 and seqlen % C == 0 and nseq % nseg == 0
    rb = row0 // rows
    hk, hv = RET_H * RET_DK, RET_H * RET_DV
    has_s0 = s0 is not None
    assert not (has_s0 or rope) or nseg == 1
    tspec = lambda r, c: pl.BlockSpec((2, None, r, c), lambda b, h: (0, h, 0, 0))
    mode = dict(pipeline_mode=pl.Buffered(1)) if rows * RET_DV * 2 >= (4 << 20) else {}
    in_specs = [pl.BlockSpec((rows, RET_DK), lambda b, h: (rb + b, h), **mode),
                pl.BlockSpec((rows, RET_DK), lambda b, h: (rb + b, RET_H + h), **mode),
                pl.BlockSpec((rows, RET_DV), lambda b, h: (rb + b, 2 * hk // RET_DV + h), **mode),
                tspec(C, C), tspec(C, 1), tspec(C, 1), tspec(1, 1),
                pl.BlockSpec((rows, RET_DV), lambda b, h: (rb + b, (2 * hk + hv) // RET_DV + h), **mode),
                pl.BlockSpec((1, RET_DV), lambda b, h: (0, 0))]
    args = [proj, proj, proj, *tabs, proj, norm_g.reshape(1, RET_DV)]
    if rope:
        cos, sin = _rope_tables(seqlen, RET_DK)
        in_specs += [pl.BlockSpec((seqlen, RET_DK), lambda b, h: (0, 0), pipeline_mode=pl.Buffered(1))] * 2
        args += [cos, sin]
    if has_s0:
        in_specs.append(pl.BlockSpec((None, 2, None, RET_DK, RET_DV), lambda b, h: (b, 0, h, 0, 0)))
        args.append(s0)
    in_specs.append(pl.BlockSpec(memory_space=pl.ANY))
    args.append(o_prev)
    aliases = {len(args) - 1: 0}
    out_shape = [jax.ShapeDtypeStruct((T, hv), BF16)]
    out_specs = [pl.BlockSpec((rows, RET_DV), lambda b, h: (rb + b, h))]
    if want_final:
        out_shape.append(jax.ShapeDtypeStruct((nseq, 2, RET_H, RET_DK, RET_DV), F32))
        out_specs.append(pl.BlockSpec((nseg, 2, None, RET_DK, RET_DV), lambda b, h: (b, 0, h, 0, 0)))
    outs = pl.pallas_call(
        functools.partial(_ret_kernel, cps, nseg, U, has_s0, want_final, rope),
        out_shape=tuple(out_shape),
        grid=(nseq // nseg, RET_H),
        in_specs=in_specs,
        out_specs=tuple(out_specs),
        scratch_shapes=[pltpu.VMEM((U * C, RET_DK), BF16)] * 4
        + [pltpu.VMEM((U, RET_DK, RET_DV), F32), pltpu.VMEM((RET_DK, RET_DV), F32),
           pltpu.VMEM((rows, RET_DV), F32)],
        input_output_aliases=aliases,
        compiler_params=_cp("arbitrary", "arbitrary"),
        name="ret",
    )(*args)
    return (outs[0], outs[1]) if want_final else (outs[0], None)


def _gla_layer(lay, x, mods, p, s0):
    hk, hv = GLA_H * GLA_DK, GLA_H * GLA_DV
    w_all = jnp.concatenate([p['w_in'], p['gate_w1'][0], p['gate_w1'][1],
                             jnp.zeros((D, 128 - 2 * GLA_RANK), F32)], axis=1).astype(BF16)
    proj = _proj(lay, x, mods, 0, 1, w_all, jnp.zeros((1, w_all.shape[1]), F32), 640)
    pad = lambda w, lo: jnp.pad(w, ((lo, 128 - GLA_RANK - lo), (0, 0))).astype(BF16)
    w2f, w2b = pad(p['gate_w2'][0], 0), pad(p['gate_w2'][1], GLA_RANK)
    a = jnp.zeros((lay.T, hv), BF16)
    a, s_fin = _gla_core(proj, w2f, w2b, p['gate_b'], p['norm_g'], None, a, 0, lay.B, lay.L,
                         math.gcd(lay.B, 8), True)
    a, _ = _gla_core(proj, w2f, w2b, p['gate_b'], p['norm_g'], s0, a, lay.TP, lay.NS, lay.LS, 1, False)
    x = _plain_out(lay, x, mods, a, p['w_out'].astype(BF16), jnp.zeros((D,), F32))
    return x, s_fin


def _ret_layer(lay, x, mods, p, s0):
    hk, hv = RET_H * RET_DK, RET_H * RET_DV
    proj = _proj(lay, x, mods, 0, 1, p['w_in'].astype(BF16), jnp.zeros((1, 2 * hk + 2 * hv), F32), 1536)
    tabs = _ret_tables(p['log_decay'])
    a = jnp.zeros((lay.T, hv), BF16)
    a, s_fin = _ret_core(proj, tabs, p['norm_g'], None, a, 0, lay.B, lay.L, math.gcd(lay.B, 8), True, False)
    a, _ = _ret_core(proj, tabs, p['norm_g'], s0, a, lay.TP, lay.NS, lay.LS, 1, False, True)
    x = _plain_out(lay, x, mods, a, p['w_out'].astype(BF16), jnp.zeros((D,), F32))
    return x, s_fin


MOE_BM = 512
EXPERT_TF = 1792
ROUTER_LANES = 128
DMA_UNROLL = 8


def _router_kernel(x_ref, sh_ref, sc_ref, rw_ref, h_ref, idx_ref, gate_ref, rank_ref, cnt_ref):
    @pl.when(pl.program_id(0) == 0)
    def _():
        cnt_ref[...] = jnp.zeros_like(cnt_ref)

    h = _modulate(x_ref[...], sh_ref[...], sc_ref[...])
    h_ref[...] = h
    logits = jnp.dot(h, rw_ref[...], precision=HIGHEST, preferred_element_type=F32)
    lane = lax.broadcasted_iota(jnp.int32, logits.shape, 1)
    neg = jnp.float32(-jnp.inf)
    logits = jnp.where(lane < N_EXPERTS, logits, neg)
    m1 = jnp.max(logits, axis=-1, keepdims=True)
    i1 = jnp.min(jnp.where(logits == m1, lane, ROUTER_LANES), axis=-1, keepdims=True)
    rest = jnp.where(lane == i1, neg, logits)
    m2 = jnp.max(rest, axis=-1, keepdims=True)
    i2 = jnp.min(jnp.where(rest == m2, lane, ROUTER_LANES), axis=-1, keepdims=True)
    e2 = jnp.exp(m2 - m1)
    g1 = 1.0 / (1.0 + e2)
    idx_ref[:, 0:1] = i1
    idx_ref[:, 1:2] = i2
    gate_ref[:, 0:1] = g1
    gate_ref[:, 1:2] = e2 * g1
    tm = logits.shape[0]
    sel1 = lane == i1
    sel2 = lane == i2
    picked = jnp.where(jnp.logical_or(sel1, sel2), 1.0, 0.0)
    before = (lax.broadcasted_iota(jnp.int32, (tm, tm), 1)
              < lax.broadcasted_iota(jnp.int32, (tm, tm), 0)).astype(BF16)
    prior = jnp.dot(before, picked.astype(BF16), preferred_element_type=F32) + cnt_ref[...]
    rank_ref[:, 0:1] = jnp.sum(jnp.where(sel1, prior, 0.0), axis=-1, keepdims=True).astype(jnp.int32)
    rank_ref[:, 1:2] = jnp.sum(jnp.where(sel2, prior, 0.0), axis=-1, keepdims=True).astype(jnp.int32)
    cnt_ref[...] += jnp.sum(picked, axis=0, keepdims=True)


def _router(lay, x, mods, router_w):
    tm = lay.tile(512)
    rw = jnp.pad(router_w, ((0, 0), (0, ROUTER_LANES - N_EXPERTS)))
    return pl.pallas_call(
        _router_kernel,
        out_shape=(jax.ShapeDtypeStruct((lay.T, D), F32),
                   jax.ShapeDtypeStruct((lay.T, 2), jnp.int32),
                   jax.ShapeDtypeStruct((lay.T, 2), F32),
                   jax.ShapeDtypeStruct((lay.T, 2), jnp.int32),
                   jax.ShapeDtypeStruct((1, ROUTER_LANES), F32)),
        grid=(lay.T // tm,),
        in_specs=[pl.BlockSpec((tm, D), lambda i: (i, 0)),
                  _mod_spec(lay, tm, 3, 1), _mod_spec(lay, tm, 4, 1),
                  pl.BlockSpec((D, ROUTER_LANES), lambda i: (0, 0))],
        out_specs=(pl.BlockSpec((tm, D), lambda i: (i, 0)),
                   pl.BlockSpec((tm, 2), lambda i: (i, 0)),
                   pl.BlockSpec((tm, 2), lambda i: (i, 0)),
                   pl.BlockSpec((tm, 2), lambda i: (i, 0)),
                   pl.BlockSpec((1, ROUTER_LANES), lambda i: (0, 0))),
        compiler_params=_cp("arbitrary"),
        name="router",
    )(x, mods, mods, rw)


def _moe_plan(idx, rank, counts, bm):
    a = idx.size
    counts = counts[0, :N_EXPERTS].astype(jnp.int32)
    padded = (counts + bm - 1) // bm * bm
    pad_end = jnp.cumsum(padded)
    pad_start = pad_end - padded
    hit = idx[..., None] == jnp.arange(N_EXPERTS, dtype=jnp.int32)
    dest = (rank + jnp.sum(jnp.where(hit, pad_start, 0), axis=-1)).reshape(a).astype(jnp.int32)
    nb = -(-(a + N_EXPERTS * (bm - 1)) // bm)
    block_start = jnp.arange(nb, dtype=jnp.int32) * bm
    block_e = jnp.minimum(jnp.searchsorted(pad_end, block_start, side='right'), N_EXPERTS - 1).astype(jnp.int32)
    nvalid = (pad_end[-1] // bm).astype(jnp.int32).reshape(1)
    fill = jnp.concatenate([pad_start + counts, pad_end, nvalid]).astype(jnp.int32)
    return dest, fill, block_e, nvalid, nb


def _dispatch_kernel(tm, bm, nb, dest_ref, fill_ref, h_ref, xs_hbm, zero_sc, sem, zsem):
    i = pl.program_id(0)
    zr = zero_sc.shape[0]

    @pl.when(i == 0)
    def _():
        zero_sc[...] = jnp.zeros_like(zero_sc)

        def zero_row(r):
            return pltpu.make_async_copy(zero_sc.at[pl.ds(0, 1)], xs_hbm.at[pl.ds(r, 1)], zsem)

        def zero_rows(r):
            return pltpu.make_async_copy(zero_sc, xs_hbm.at[pl.ds(pl.multiple_of(r, zr), zr)], zsem)

        for e in range(N_EXPERTS):
            lo, hi = fill_ref[e], fill_ref[N_EXPERTS + e]
            lax.fori_loop(lo, hi, lambda r, c: (zero_row(r).start(), c)[1], 0)
            lax.fori_loop(lo, hi, lambda r, c: (zero_row(r).wait(), c)[1], 0)
        lo, hi = fill_ref[2 * N_EXPERTS] * (bm // zr), nb * (bm // zr)
        lax.fori_loop(lo, hi, lambda q, c: (zero_rows(q * zr).start(), c)[1], 0)
        lax.fori_loop(lo, hi, lambda q, c: (zero_rows(q * zr).wait(), c)[1], 0)

    def row_copy(r, dst):
        return pltpu.make_async_copy(h_ref.at[pl.ds(r, 1)], xs_hbm.at[pl.ds(dst, 1)], sem)

    def issue(r, carry):
        a = 2 * (i * tm + r)
        row_copy(r, dest_ref[a]).start()
        row_copy(r, dest_ref[a + 1]).start()
        return carry

    def drain(r, carry):
        row_copy(r, 0).wait()
        row_copy(r, 0).wait()
        return carry

    lax.fori_loop(0, tm, issue, 0, unroll=DMA_UNROLL)
    lax.fori_loop(0, tm, drain, 0, unroll=DMA_UNROLL)


def _dispatch(lay, h, dest, fill, nb, bm):
    tm = lay.tile(512)
    grid_spec = pltpu.PrefetchScalarGridSpec(
        num_scalar_prefetch=2,
        grid=(lay.T // tm,),
        in_specs=[pl.BlockSpec((tm, D), lambda i, d, f: (i, 0))],
        out_specs=pl.BlockSpec(memory_space=pl.ANY),
        scratch_shapes=[pltpu.VMEM((64, D), F32), pltpu.SemaphoreType.DMA(()), pltpu.SemaphoreType.DMA(())],
    )
    return pl.pallas_call(
        functools.partial(_dispatch_kernel, tm, bm, nb),
        out_shape=jax.ShapeDtypeStruct((nb * bm, D), F32),
        grid_spec=grid_spec,
        compiler_params=_cp("arbitrary"),
        name="dispatch",
    )(dest, fill, h)


def _experts_kernel(nf, be_ref, nv_ref, xs_ref, wa_ref, wb_ref, wo_ref, o_ref, xb_sc, acc_sc):
    i = pl.program_id(0)
    f = pl.program_id(1)
    valid = i < nv_ref[0]

    @pl.when(jnp.logical_and(valid, f == 0))
    def _():
        xb_sc[...] = xs_ref[...].astype(BF16)

    @pl.when(valid)
    def _():
        xb = xb_sc[...]
        a = jnp.dot(xb, wa_ref[...], preferred_element_type=F32)
        b = jnp.dot(xb, wb_ref[...], preferred_element_type=F32)
        h = (_silu(a) * b).astype(BF16)
        y = jnp.dot(h, wo_ref[...], preferred_element_type=F32)

        @pl.when(f == 0)
        def _():
            acc_sc[...] = y

        @pl.when(f > 0)
        def _():
            acc_sc[...] += y

    @pl.when(f == nf - 1)
    def _():
        o_ref[...] = jnp.where(valid, acc_sc[...], 0.0)


def _experts(xs, block_e, nvalid, nb, bm, w_in, w_out):
    tf = EXPERT_TF
    nf = EXPERT_DIM // tf

    def wmap(off):
        def imap(i, f, be, nv):
            fe = jnp.where(i < nv[0], f, nf - 1)
            return (be[i], 0, off + fe)
        return imap

    def womap(i, f, be, nv):
        fe = jnp.where(i < nv[0], f, nf - 1)
        return (be[i], fe, 0)

    grid_spec = pltpu.PrefetchScalarGridSpec(
        num_scalar_prefetch=2,
        grid=(nb, nf),
        in_specs=[pl.BlockSpec((bm, D), lambda i, f, be, nv: (jnp.minimum(i, nv[0] - 1), 0)),
                  pl.BlockSpec((None, D, tf), wmap(0)),
                  pl.BlockSpec((None, D, tf), wmap(nf)),
                  pl.BlockSpec((None, tf, D), womap)],
        out_specs=pl.BlockSpec((bm, D), lambda i, f, be, nv: (i, 0)),
        scratch_shapes=[pltpu.VMEM((bm, D), BF16), pltpu.VMEM((bm, D), F32)],
    )
    return pl.pallas_call(
        functools.partial(_experts_kernel, nf),
        out_shape=jax.ShapeDtypeStruct((nb * bm, D), F32),
        grid_spec=grid_spec,
        compiler_params=_cp("arbitrary", "arbitrary"),
        name="experts",
    )(block_e, nvalid, xs, w_in, w_in, w_out)


def _combine_kernel(tm, nt, final, dest_ref, x_ref, gate_ref, g_ref, fg_ref, ys_hbm, o_ref, y_sc, sem):
    i = pl.program_id(0)
    slot = i % 2

    def row_copy(s, k, r, src):
        return pltpu.make_async_copy(ys_hbm.at[pl.ds(src, 1)], y_sc.at[s, k, pl.ds(r, 1)], sem.at[s])

    def issue_tile(t, s):
        def issue(r, carry):
            a = 2 * (t * tm + r)
            row_copy(s, 0, r, dest_ref[a]).start()
            row_copy(s, 1, r, dest_ref[a + 1]).start()
            return carry
        lax.fori_loop(0, tm, issue, 0, unroll=DMA_UNROLL)

    @pl.when(i == 0)
    def _():
        issue_tile(0, 0)

    @pl.when(i + 1 < nt)
    def _():
        issue_tile(i + 1, 1 - slot)

    def drain(r, carry):
        row_copy(slot, 0, r, 0).wait()
        row_copy(slot, 1, r, 0).wait()
        return carry

    lax.fori_loop(0, tm, drain, 0, unroll=DMA_UNROLL)
    gate = gate_ref[...]
    out = x_ref[...] + g_ref[...] * (gate[:, 0:1] * y_sc[slot, 0] + gate[:, 1:2] * y_sc[slot, 1])
    if final:
        ms = jnp.mean(out * out, axis=-1, keepdims=True)
        out = out * lax.rsqrt(ms + RMS_EPS) * fg_ref[...]
    o_ref[...] = out


def _combine(lay, x, mods, gates, ys, dest, final_g):
    tm = lay.tile(256)
    nt = lay.T // tm
    final = final_g is not None
    fg = (final_g if final else jnp.ones((D,), F32)).reshape(1, D)
    grid_spec = pltpu.PrefetchScalarGridSpec(
        num_scalar_prefetch=1,
        grid=(nt,),
        in_specs=[pl.BlockSpec((tm, D), lambda i, d: (i, 0)),
                  pl.BlockSpec((tm, 2), lambda i, d: (i, 0)),
                  pl.BlockSpec((None, 1, D), lambda i, d: (lay.group(i * tm) * MOD_CHUNKS + 5, 0, 0)),
                  pl.BlockSpec((1, D), lambda i, d: (0, 0)),
                  pl.BlockSpec(memory_space=pl.ANY)],
        out_specs=pl.BlockSpec((tm, D), lambda i, d: (i, 0)),
        scratch_shapes=[pltpu.VMEM((2, 2, tm, D), F32), pltpu.SemaphoreType.DMA((2,))],
    )
    return pl.pallas_call(
        functools.partial(_combine_kernel, tm, nt, final),
        out_shape=jax.ShapeDtypeStruct((lay.T, D), F32),
        grid_spec=grid_spec,
        compiler_params=_cp("arbitrary"),
        name="combine",
    )(dest, x, gates, mods, fg, ys)


def _moe_layer(lay, x, mods, router_w, w_in, w_out, final_g=None, bm=MOE_BM):
    h, idx, gates, rank, counts = _router(lay, x, mods, router_w)
    dest, fill, block_e, nvalid, nb = _moe_plan(idx, rank, counts, bm)
    xs = _dispatch(lay, h, dest, fill, nb, bm)
    ys = _experts(xs, block_e, nvalid, nb, bm, w_in, w_out)
    return _combine(lay, x, mods, gates, ys, dest, final_g)


def kernel(x_prompt, x_sample, c, state_l0_s5_re, state_l0_s5_im, state_l2_gla, state_l3_ret, c_ctx, l0_mod_w, l0_mod_b, l0_s5_a_re, l0_s5_a_im, l0_s5_log_dt, l0_s5_b_re, l0_s5_b_im, l0_s5_c_re, l0_s5_c_im, l0_s5_d, l0_s5_glu_w, l0_ffn_w_in, l0_ffn_w_out, l1_mod_w, l1_mod_b, l1_hy_w_in, l1_hy_b_in, l1_hy_short_w, l1_hy_short_b, l1_hy_f_w1, l1_hy_f_b1, l1_hy_f_w2, l1_hy_f_b2, l1_hy_f_w3, l1_hy_f_freq, l1_hy_skip, l1_hy_w_out, l1_hy_b_out, l1_moe_router, l1_moe_w_in, l1_moe_w_out, l2_mod_w, l2_mod_b, l2_gla_w_in, l2_gla_gate_w1, l2_gla_gate_w2, l2_gla_gate_b, l2_gla_norm_g, l2_gla_w_out, l2_ffn_w_in, l2_ffn_w_out, l3_mod_w, l3_mod_b, l3_ret_w_in, l3_ret_log_decay, l3_ret_norm_g, l3_ret_w_out, l3_moe_router, l3_moe_w_in, l3_moe_w_out, final_norm_g):
    B, L, _ = x_prompt.shape
    NS, LS, _ = x_sample.shape
    lay = Layout(B, L, NS, LS)
    x = jnp.concatenate([x_prompt.reshape(B * L, D), x_sample.reshape(NS * LS, D)], axis=0)
    cond = jnp.concatenate([c_ctx[None], c, jnp.zeros((8 - 1 - NS, D), F32)], axis=0)
    mods0 = _mods(cond, l0_mod_w, l0_mod_b)
    p0 = dict(a_re=l0_s5_a_re, a_im=l0_s5_a_im, log_dt=l0_s5_log_dt, b_re=l0_s5_b_re, b_im=l0_s5_b_im,
              c_re=l0_s5_c_re, c_im=l0_s5_c_im, d=l0_s5_d, glu_w=l0_s5_glu_w.astype(BF16))
    x, s5_re, s5_im = _s5_layer(lay, x, mods0, p0, state_l0_s5_re, state_l0_s5_im)
    x = _ffn(lay, x, mods0, l0_ffn_w_in.astype(BF16), l0_ffn_w_out.astype(BF16))

    mods1 = _mods(cond, l1_mod_w, l1_mod_b)
    p1 = dict(w_in=l1_hy_w_in, b_in=l1_hy_b_in, short_w=l1_hy_short_w, short_b=l1_hy_short_b,
              f_w1=l1_hy_f_w1, f_b1=l1_hy_f_b1, f_w2=l1_hy_f_w2, f_b2=l1_hy_f_b2, f_w3=l1_hy_f_w3,
              f_freq=l1_hy_f_freq, skip=l1_hy_skip, w_out=l1_hy_w_out, b_out=l1_hy_b_out)
    x = _hyena_layer(lay, x, mods1, p1)
    x = _moe_layer(lay, x, mods1, l1_moe_router, l1_moe_w_in.astype(BF16), l1_moe_w_out.astype(BF16))

    mods2 = _mods(cond, l2_mod_w, l2_mod_b)
    p2 = dict(w_in=l2_gla_w_in, gate_w1=l2_gla_gate_w1, gate_w2=l2_gla_gate_w2, gate_b=l2_gla_gate_b,
              norm_g=l2_gla_norm_g, w_out=l2_gla_w_out)
    x, gla_state = _gla_layer(lay, x, mods2, p2, state_l2_gla)
    x = _ffn(lay, x, mods2, l2_ffn_w_in.astype(BF16), l2_ffn_w_out.astype(BF16))

    mods3 = _mods(cond, l3_mod_w, l3_mod_b)
    p3 = dict(w_in=l3_ret_w_in, log_decay=l3_ret_log_decay, norm_g=l3_ret_norm_g, w_out=l3_ret_w_out)
    x, ret_state = _ret_layer(lay, x, mods3, p3, state_l3_ret)
    y = _moe_layer(lay, x, mods3, l3_moe_router, l3_moe_w_in.astype(BF16), l3_moe_w_out.astype(BF16),
                   final_g=final_norm_g)
    return (y[:lay.TP].reshape(B, L, D), y[lay.TP:].reshape(NS, LS, D), s5_re, s5_im, gla_state, ret_state)
```

```python
import functools
import math

import jax
import jax.numpy as jnp
import numpy as np
from jax import lax
from jax.experimental import pallas as pl
from jax.experimental.pallas import tpu as pltpu

F32 = jnp.float32
BF16 = jnp.bfloat16
HIGHEST = lax.Precision.HIGHEST

D = 1024
RMS_EPS = 1e-6
MOD_CHUNKS = 6
GRID_W = 64

S5_Q = 16
S5_G = D // S5_Q
S5_P = 64
S5_T = 16
S5_SCAN_ROWS = 64

HY_BANDS = 16
HY_TARGET = 1e-2
HY_FAST_PCT = 0.3
HY_SLOW_PCT = 1.5

GLA_H, GLA_DK, GLA_DV = 4, 128, 256
GLA_RANK = 16
GLA_TAU = 16.0
RET_H, RET_DK, RET_DV = 4, 256, 512
CHUNK = 64
RET_CHUNK = 256
ROPE_BASE = 10000.0

FFN_DIM = 2816
N_EXPERTS = 8
EXPERT_DIM = 3584

VMEM_LIMIT_V7X = 56 * 1024 * 1024


def _cp(*sem):
    return pltpu.CompilerParams(dimension_semantics=sem, vmem_limit_bytes=VMEM_LIMIT_V7X)


def _silu(x):
    return x * jax.nn.sigmoid(x)


def _modulate(x, shift, scale):
    ms = jnp.mean(x * x, axis=-1, keepdims=True)
    return x * lax.rsqrt(ms + RMS_EPS) * (1.0 + scale) + shift


class Layout:
    def __init__(self, n_prompt, l_prompt, n_sample, l_sample):
        self.B, self.L, self.NS, self.LS = n_prompt, l_prompt, n_sample, l_sample
        self.TP = n_prompt * l_prompt
        self.T = self.TP + n_sample * l_sample

    def tile(self, want):
        t = math.gcd(math.gcd(self.TP, self.LS), want)
        assert t % 8 == 0
        return t

    def group(self, row):
        return jnp.where(row < self.TP, 0, 1 + (row - self.TP) // self.LS)


def _mod_spec(lay, tm, chunk, ngrid):
    def imap(*ids):
        return (lay.group(ids[0] * tm) * MOD_CHUNKS + chunk, 0, 0)
    del ngrid
    return pl.BlockSpec((None, 1, D), imap)


def _mods_kernel(c_ref, w_ref, b_ref, o_ref):
    o_ref[...] = jnp.dot(_silu(c_ref[...]), w_ref[...], precision=HIGHEST,
                         preferred_element_type=F32) + b_ref[...]


def _mods(cond, w, b):
    n = MOD_CHUNKS * D
    tn = 1536
    out = pl.pallas_call(
        _mods_kernel,
        out_shape=jax.ShapeDtypeStruct((8, n), F32),
        grid=(n // tn,),
        in_specs=[pl.BlockSpec((8, D), lambda j: (0, 0)),
                  pl.BlockSpec((D, tn), lambda j: (0, j)),
                  pl.BlockSpec((1, tn), lambda j: (0, j))],
        out_specs=pl.BlockSpec((8, tn), lambda j: (0, j)),
        compiler_params=_cp("arbitrary"),
        name="mods",
    )(cond, w, b.reshape(1, n))
    return out.reshape(8 * MOD_CHUNKS, 1, D)


def _modulate_kernel(x_ref, sh_ref, sc_ref, o_ref):
    o_ref[...] = _modulate(x_ref[...], sh_ref[...], sc_ref[...]).astype(o_ref.dtype)


def _modulate_call(lay, x, mods, c_shift, c_scale, dtype):
    tm = lay.tile(512)
    return pl.pallas_call(
        _modulate_kernel,
        out_shape=jax.ShapeDtypeStruct((lay.T, D), dtype),
        grid=(lay.T // tm,),
        in_specs=[pl.BlockSpec((tm, D), lambda i: (i, 0)),
                  _mod_spec(lay, tm, c_shift, 1), _mod_spec(lay, tm, c_scale, 1)],
        out_specs=pl.BlockSpec((tm, D), lambda i: (i, 0)),
        compiler_params=_cp("arbitrary"),
        name="modulate",
    )(x, mods, mods)


def _proj_kernel(x_ref, sh_ref, sc_ref, w_ref, b_ref, o_ref, u_sc):
    @pl.when(pl.program_id(1) == 0)
    def _():
        u_sc[...] = _modulate(x_ref[...], sh_ref[...], sc_ref[...]).astype(BF16)

    acc = jnp.dot(u_sc[...], w_ref[...], preferred_element_type=F32) + b_ref[...]
    o_ref[...] = acc.astype(o_ref.dtype)


def _proj(lay, x, mods, c_shift, c_scale, w, b, tn, out_dtype=BF16):
    tm = lay.tile(1024)
    n = w.shape[1]
    assert n % tn == 0
    return pl.pallas_call(
        _proj_kernel,
        out_shape=jax.ShapeDtypeStruct((lay.T, n), out_dtype),
        grid=(lay.T // tm, n // tn),
        in_specs=[pl.BlockSpec((tm, D), lambda i, j: (i, 0)),
                  _mod_spec(lay, tm, c_shift, 2), _mod_spec(lay, tm, c_scale, 2),
                  pl.BlockSpec((D, tn), lambda i, j: (0, j)),
                  pl.BlockSpec((1, tn), lambda i, j: (0, j))],
        out_specs=pl.BlockSpec((tm, tn), lambda i, j: (i, j)),
        scratch_shapes=[pltpu.VMEM((tm, D), BF16)],
        compiler_params=_cp("arbitrary", "arbitrary"),
        name="proj",
    )(x, mods, mods, w, b)


def _ffn_kernel(x_ref, sh_ref, sc_ref, g_ref, wa_ref, wb_ref, wo_ref, o_ref):
    x = x_ref[...]
    u = _modulate(x, sh_ref[...], sc_ref[...]).astype(BF16)
    a = jnp.dot(u, wa_ref[...], preferred_element_type=F32)
    b = jnp.dot(u, wb_ref[...], preferred_element_type=F32)
    h = (_silu(a) * b).astype(BF16)
    o_ref[...] = x + g_ref[...] * jnp.dot(h, wo_ref[...], preferred_element_type=F32)


def _ffn(lay, x, mods, w_in, w_out):
    tm = lay.tile(512)
    once = dict(pipeline_mode=pl.Buffered(1))
    return pl.pallas_call(
        _ffn_kernel,
        out_shape=jax.ShapeDtypeStruct((lay.T, D), F32),
        grid=(lay.T // tm,),
        in_specs=[pl.BlockSpec((tm, D), lambda i: (i, 0)),
                  _mod_spec(lay, tm, 3, 1), _mod_spec(lay, tm, 4, 1), _mod_spec(lay, tm, 5, 1),
                  pl.BlockSpec((D, FFN_DIM), lambda i: (0, 0), **once),
                  pl.BlockSpec((D, FFN_DIM), lambda i: (0, 1), **once),
                  pl.BlockSpec((FFN_DIM, D), lambda i: (0, 0), **once)],
        out_specs=pl.BlockSpec((tm, D), lambda i: (i, 0)),
        compiler_params=_cp("arbitrary"),
        name="ffn",
    )(x, mods, mods, mods, w_in, w_in, w_out)


def _s5_tables(a_re, a_im, log_dt, b_re, b_im, c_re, c_im, d_skip):
    T, G, P, Q = S5_T, S5_G, S5_P, S5_Q
    a = lax.complex(a_re, a_im)
    adt = a * jnp.exp(log_dt)[..., None]
    lam = jnp.exp(adt)
    bb = ((lam - 1.0) / a)[..., None] * lax.complex(b_re, b_im)
    cm = lax.complex(c_re, c_im)
    steps = jnp.arange(T + 1, dtype=F32)
    pw = jnp.exp(steps[None, :, None, None] * adt[:, None])
    kern = jnp.real(jnp.einsum('dgqp,djgp,dgpr->djgqr', cm, pw[:, :T], bb))
    lag = jnp.arange(T)[:, None, None]
    s_i = jnp.arange(T)[None, :, None]
    t_i = jnp.arange(T)[None, None, :]
    place = jnp.stack([t_i - s_i == lag, s_i - t_i == lag]).astype(F32)
    m = jnp.einsum('djst,djgqr->gsrtq', place, kern, precision=HIGHEST)
    eye = (jnp.eye(T)[:, None, :, None] * jnp.eye(Q)[None, :, None, :])
    m = m + eye[None] * d_skip.reshape(G, 1, 1, 1, Q)
    m = m.reshape(G, T * Q, T * Q)
    e_f = pw[0][T - 1 - jnp.arange(T)]
    e_b = pw[1][jnp.arange(T)]
    n_f = e_f[..., None] * bb[0][None]
    n_b = e_b[..., None] * bb[1][None]
    n_c = jnp.concatenate([n_f, n_b], axis=2)
    n_c = jnp.transpose(n_c, (1, 0, 3, 2)).reshape(G, T * Q, 2 * P)
    lam_t = jnp.concatenate([pw[0][T], pw[1][T]], axis=-1)
    w_f = cm[0][:, None] * jnp.transpose(pw[0][1:T + 1], (1, 0, 2))[:, :, None, :]
    w_b = cm[1][:, None] * jnp.transpose(pw[1][T - jnp.arange(T)], (1, 0, 2))[:, :, None, :]
    w_f = jnp.transpose(w_f, (0, 3, 1, 2)).reshape(G, P, T * Q)
    w_b = jnp.transpose(w_b, (0, 3, 1, 2)).reshape(G, P, T * Q)
    z = jnp.zeros_like(jnp.real(w_f))
    c_mats = dict(c_f_re=jnp.concatenate([jnp.real(w_f), z], axis=1),
                  c_f_im=jnp.concatenate([-jnp.imag(w_f), z], axis=1),
                  c_b_re=jnp.concatenate([z, jnp.real(w_b)], axis=1),
                  c_b_im=jnp.concatenate([z, -jnp.imag(w_b)], axis=1))
    return dict(m=m.astype(BF16), n_re=jnp.real(n_c).astype(BF16), n_im=jnp.imag(n_c).astype(BF16),
                l_re=jnp.real(lam_t), l_im=jnp.imag(lam_t), **{k: v.astype(BF16) for k, v in c_mats.items()})


def _s5_in_kernel(u_ref, m_ref, nre_ref, nim_ref, yi_ref, sre_ref, sim_ref):
    u = u_ref[...]
    yi_ref[...] = jnp.dot(u, m_ref[...], preferred_element_type=F32)
    sre_ref[...] = jnp.dot(u, nre_ref[...], preferred_element_type=F32).reshape(sre_ref.shape)
    sim_ref[...] = jnp.dot(u, nim_ref[...], preferred_element_type=F32).reshape(sim_ref.shape)


def _s5_in(ug, tabs):
    G, R, W = ug.shape
    P2 = 2 * S5_P
    RB = S5_SCAN_ROWS
    assert R % RB == 0
    gspec = lambda n: pl.BlockSpec((None, W, n), lambda g: (g, 0, 0))
    rspec = pl.BlockSpec((None, R, W), lambda g: (g, 0, 0))
    sspec = pl.BlockSpec((R // RB, RB, P2), lambda g: (0, g, 0))
    sshape = jax.ShapeDtypeStruct((R // RB, G * RB, P2), F32)
    return pl.pallas_call(
        _s5_in_kernel,
        out_shape=(jax.ShapeDtypeStruct((G, R, W), F32), sshape, sshape),
        grid=(G,),
        in_specs=[rspec, gspec(W), gspec(P2), gspec(P2)],
        out_specs=(rspec, sspec, sspec),
        compiler_params=_cp("arbitrary"),
        name="s5_in",
    )(ug, tabs['m'], tabs['n_re'], tabs['n_im'])


def _s5_scan_kernel(nsb, ncb, nblk, sref_ref, simf_ref, sreb_ref, simb_ref, lre_ref, lim_ref,
                    h0re_ref, h0im_ref, *rest):
    hfre_ref, hfim_ref, hbre_ref, hbim_ref, fre_ref, fim_ref, cre_sc, cim_sc = rest[4:]
    P = S5_P
    rows = sref_ref.shape[0] // ncb
    j = pl.program_id(1)
    fwd = lax.broadcasted_iota(jnp.int32, (1, 2 * P), 1) < P
    lre = lre_ref[...]
    lim = lim_ref[...]

    @pl.when(j == 0)
    def _():
        cre_sc[...] = h0re_ref[...]
        cim_sc[...] = h0im_ref[...]

    def at(k):
        return pl.ds(k, rows, stride=ncb)

    def body(k, carry):
        hre, him = carry
        kb = ncb - 1 - k
        hfre_ref[at(k), :] = hre
        hfim_ref[at(k), :] = him
        hbre_ref[at(kb), :] = hre
        hbim_ref[at(kb), :] = him
        sre = jnp.where(fwd, sref_ref[at(k), :], sreb_ref[at(kb), :])
        sim = jnp.where(fwd, simf_ref[at(k), :], simb_ref[at(kb), :])
        return (lre * hre - lim * him + sre, lre * him + lim * hre + sim)

    hre, him = lax.fori_loop(0, ncb, body, (cre_sc[...], cim_sc[...]), unroll=4)
    cre_sc[...] = hre
    cim_sc[...] = him

    @pl.when(j == nblk - 1)
    def _():
        fre_ref[...] = hre
        fim_ref[...] = him


def _s5_scan(sre, sim, tabs, h0re, h0im, hprev, row0, nseq, nc, nsb, nblk):
    _, grb, P2 = sre.shape
    rb = S5_SCAN_ROWS
    G = grb // rb
    assert nblk == 1 or nsb == 1
    ncb = nc // nblk
    assert nsb * ncb == rb and row0 % rb == 0 and nseq % nsb == 0 and nc % nblk == 0
    b0 = row0 // rb
    fspec = pl.BlockSpec((None, grb, P2), lambda i, j: (b0 + i * nblk + j, 0, 0))
    bspec = pl.BlockSpec((None, grb, P2), lambda i, j: (b0 + i * nblk + nblk - 1 - j, 0, 0))
    lspec = pl.BlockSpec((G * nsb, P2), lambda i, j: (0, 0))
    qspec = pl.BlockSpec((None, G * nsb, P2), lambda i, j: (i, 0, 0))
    anyspec = pl.BlockSpec(memory_space=pl.ANY)
    fin = jax.ShapeDtypeStruct((nseq // nsb, G * nsb, P2), F32)
    rep = lambda a: jnp.repeat(a, nsb, axis=0)
    flat = lambda a: a.reshape(nseq // nsb, G * nsb, P2)
    outs = pl.pallas_call(
        functools.partial(_s5_scan_kernel, nsb, ncb, nblk),
        out_shape=tuple(jax.ShapeDtypeStruct(h.shape, h.dtype) for h in hprev) + (fin, fin),
        grid=(nseq // nsb, nblk),
        in_specs=[fspec, fspec, bspec, bspec, lspec, lspec, qspec, qspec] + [anyspec] * 4,
        out_specs=(fspec, fspec, bspec, bspec, qspec, qspec),
        scratch_shapes=[pltpu.VMEM((G * nsb, P2), F32), pltpu.VMEM((G * nsb, P2), F32)],
        input_output_aliases={8: 0, 9: 1, 10: 2, 11: 3},
        compiler_params=_cp("arbitrary", "arbitrary"),
        name="s5_scan",
    )(sre, sim, sre, sim, rep(tabs['l_re']), rep(tabs['l_im']), flat(h0re), flat(h0im), *hprev)
    return outs[:4], outs[4].reshape(nseq // nsb, G, nsb, P2), outs[5].reshape(nseq // nsb, G, nsb, P2)


def _s5_out_kernel(yi_ref, hfre_ref, hfim_ref, hbre_ref, hbim_ref, cfre_ref, cfim_ref, cbre_ref, cbim_ref,
                   y_ref):
    y = yi_ref[...]
    for h_ref, c_ref in ((hfre_ref, cfre_ref), (hfim_ref, cfim_ref), (hbre_ref, cbre_ref), (hbim_ref, cbim_ref)):
        h = h_ref[...].reshape(y.shape[0], h_ref.shape[-1])
        y += jnp.dot(h.astype(BF16), c_ref[...], preferred_element_type=F32)
    y_ref[...] = y.astype(y_ref.dtype)


def _s5_out(yi, hprev, tabs):
    G, R, W = yi.shape
    P2 = 2 * S5_P
    RB = S5_SCAN_ROWS
    gspec = pl.BlockSpec((None, P2, W), lambda g: (g, 0, 0))
    hspec = pl.BlockSpec((R // RB, RB, P2), lambda g: (0, g, 0))
    rspec = pl.BlockSpec((None, R, W), lambda g: (g, 0, 0))
    return pl.pallas_call(
        _s5_out_kernel,
        out_shape=jax.ShapeDtypeStruct((G, R, W), F32),
        grid=(G,),
        in_specs=[rspec] + [hspec] * 4 + [gspec] * 4,
        out_specs=rspec,
        compiler_params=_cp("arbitrary"),
        name="s5_out",
    )(yi, *hprev, tabs['c_f_re'], tabs['c_f_im'], tabs['c_b_re'], tabs['c_b_im'])


LANES = 128
S5_GB = LANES // S5_Q


def _block_transpose(sets):
    blk = lax.broadcasted_iota(jnp.int32, sets[0][0].shape, 1) // S5_Q
    sets = [list(regs) for regs in sets]
    d = S5_GB // 2
    while d:
        keep = (blk & d) == 0
        for regs in sets:
            for i in range(S5_GB):
                if i & d:
                    continue
                a, b = regs[i], regs[i + d]
                regs[i] = jnp.where(keep, a, pltpu.roll(b, d * S5_Q, 1))
                regs[i + d] = jnp.where(keep, pltpu.roll(a, LANES - d * S5_Q, 1), b)
        d //= 2
    return sets


def _s5_pre_kernel(tm, x_ref, sh_ref, sc_ref, ug_ref, u_sc):
    u = _modulate(x_ref[...], sh_ref[...], sc_ref[...])
    for j in range(D // LANES):
        u_sc[j] = u[:, j * LANES:(j + 1) * LANES]
    rows16 = 16
    nh = S5_T // S5_GB
    for c in range(tm // (S5_T * rows16)):
        base = c * S5_T * rows16
        for j in range(D // LANES):
            sets = [[u_sc[j, pl.ds(base + h * S5_GB + s, rows16, stride=S5_T), :] for s in range(S5_GB)]
                    for h in range(nh)]
            for h, regs in enumerate(_block_transpose(sets)):
                for gl, t in enumerate(regs):
                    ug_ref[j * S5_GB + gl, c * rows16:(c + 1) * rows16, h * LANES:(h + 1) * LANES] = t.astype(BF16)


def _s5_pre(lay, x, mods):
    tm = lay.tile(512)
    assert tm % (S5_T * 16) == 0
    return pl.pallas_call(
        functools.partial(_s5_pre_kernel, tm),
        out_shape=jax.ShapeDtypeStruct((S5_G, lay.T // S5_T, S5_T * S5_Q), BF16),
        grid=(lay.T // tm,),
        in_specs=[pl.BlockSpec((tm, D), lambda i: (i, 0)), _mod_spec(lay, tm, 0, 1), _mod_spec(lay, tm, 1, 1)],
        out_specs=pl.BlockSpec((S5_G, tm // S5_T, S5_T * S5_Q), lambda i: (0, i, 0)),
        scratch_shapes=[pltpu.VMEM((D // LANES, tm, LANES), F32)],
        compiler_params=_cp("arbitrary"),
        name="s5_pre",
    )(x, mods, mods)


def _s5_glu_kernel(tm, x_ref, yg_ref, g_ref, wv_ref, wg_ref, o_ref, a_sc, y_sc):
    @pl.when(pl.program_id(1) == 0)
    def _():
        def sub_tile(c, carry):
            crow = pl.ds(pl.multiple_of(c * 8, 8), 8)
            base = c * (S5_T * 8)
            nh = S5_T // S5_GB
            for j2 in range(0, D // LANES, 2):
                keys = [(j, h) for j in (j2, j2 + 1) for h in range(nh)]
                sets = [[yg_ref[j * S5_GB + gl, crow, h * LANES:(h + 1) * LANES] for gl in range(S5_GB)]
                        for j, h in keys]
                for (j, h), regs in zip(keys, _block_transpose(sets)):
                    for s, t in enumerate(regs):
                        y_sc[j, pl.ds(base + h * S5_GB + s, 8, stride=S5_T), :] = t
            return carry

        lax.fori_loop(0, tm // (S5_T * 8), sub_tile, 0)
        for j in range(D // LANES):
            a_sc[:, j * LANES:(j + 1) * LANES] = jax.nn.gelu(y_sc[j]).astype(BF16)

    a = a_sc[...]
    val = jnp.dot(a, wv_ref[...], preferred_element_type=F32)
    gate = jnp.dot(a, wg_ref[...], preferred_element_type=F32)
    o_ref[...] = x_ref[...] + g_ref[...] * (val * jax.nn.sigmoid(gate))


def _s5_glu(lay, x, yg, mods, glu_w):
    tm = lay.tile(1024)
    assert tm % (S5_T * 8) == 0
    tn = 512
    nn = D // tn
    return pl.pallas_call(
        functools.partial(_s5_glu_kernel, tm),
        out_shape=jax.ShapeDtypeStruct((lay.T, D), F32),
        grid=(lay.T // tm, nn),
        in_specs=[pl.BlockSpec((tm, tn), lambda i, j: (i, j)),
                  pl.BlockSpec((S5_G, tm // S5_T, S5_T * S5_Q), lambda i, j: (0, i, 0)),
                  pl.BlockSpec((None, 1, tn), lambda i, j: (lay.group(i * tm) * MOD_CHUNKS + 2, 0, j)),
                  pl.BlockSpec((D, tn), lambda i, j: (0, j)),
                  pl.BlockSpec((D, tn), lambda i, j: (0, nn + j))],
        out_specs=pl.BlockSpec((tm, tn), lambda i, j: (i, j)),
        scratch_shapes=[pltpu.VMEM((tm, D), BF16), pltpu.VMEM((D // LANES, tm, LANES), F32)],
        compiler_params=_cp("arbitrary", "arbitrary"),
        name="s5_glu",
    )(x, yg, mods, glu_w, glu_w)


def _s5_layer(lay, x, mods, p, h0_re, h0_im):
    T, G, P, Q = S5_T, S5_G, S5_P, S5_Q
    tabs = _s5_tables(p['a_re'], p['a_im'], p['log_dt'], p['b_re'], p['b_im'], p['c_re'], p['c_im'], p['d'])
    R = lay.T // T
    ug = _s5_pre(lay, x, mods)
    yi, sre, sim = _s5_in(ug, tabs)
    hprev = tuple(jnp.zeros(sre.shape, F32) for _ in range(4))
    ncp, ncs = lay.L // T, lay.LS // T
    nsb = S5_SCAN_ROWS // ncp
    zero = jnp.zeros((lay.B // nsb, G, nsb, 2 * P), F32)
    hprev, fre, fim = _s5_scan(sre, sim, tabs, zero, zero, hprev, 0, lay.B, ncp, nsb, 1)
    to_lanes = lambda s: jnp.transpose(s, (0, 2, 1, 3)).reshape(lay.NS, G, 1, 2 * P)
    hprev, _, _ = _s5_scan(sre, sim, tabs, to_lanes(h0_re), to_lanes(h0_im), hprev,
                           lay.TP // T, lay.NS, ncs, 1, max(1, ncs // S5_SCAN_ROWS))
    yg = _s5_out(yi, hprev, tabs)
    x = _s5_glu(lay, x, yg, mods, p['glu_w'])
    from_lanes = lambda s: jnp.transpose(s, (0, 2, 1, 3)).reshape(lay.B, G, 2, P).transpose(0, 2, 1, 3)
    return x, from_lanes(fre), from_lanes(fim)


def _hyena_filters(L, p):
    mm = functools.partial(jnp.matmul, precision=HIGHEST)
    f = jnp.linspace(1e-4, HY_BANDS - 1, HY_BANDS, dtype=F32)[None, :]
    max_decay = math.log(HY_TARGET) / HY_FAST_PCT
    min_decay = math.log(HY_TARGET) / HY_SLOW_PCT
    deltas = jnp.abs(jnp.linspace(min_decay, max_decay, D, dtype=F32))
    w3 = p['f_w3'].reshape(-1, 2, 2, D)

    def side(pos, s):
        t = (pos.astype(F32) / (L - 1))[:, None]
        w = 2.0 * math.pi * pos.astype(F32)[:, None] / L
        feats = jnp.concatenate([t, jnp.cos(f * w), -jnp.sin(f * w)], axis=-1)
        z = jnp.sin(p['f_freq'][0] * (mm(feats, p['f_w1']) + p['f_b1']))
        z = jnp.sin(p['f_freq'][1] * (mm(z, p['f_w2']) + p['f_b2']))
        win = jnp.exp(-t * deltas)
        return jnp.stack([mm(z, w3[:, o, s]) * win for o in range(2)])

    j = jnp.arange(L, dtype=jnp.int32)
    k_lo = side(j, 0)
    k_hi = side((L - j) % L, 1) * (j > 0).astype(F32)[None, :, None]
    norm = jnp.sum(jnp.abs(k_lo), axis=1, keepdims=True) + jnp.sum(jnp.abs(k_hi), axis=1, keepdims=True)
    k_lo, k_hi = k_lo / norm, k_hi / norm
    alt = (1.0 - 2.0 * (j % 2).astype(F32))[None, :, None]
    k_ny = jnp.sum(alt * (k_lo + k_hi), axis=1) / (2 * L)
    return k_lo, k_hi, k_ny


def _dft_tables(L):
    r = math.isqrt(L)
    assert r * r == L
    t = jnp.arange(L, dtype=jnp.int32)[None, :]
    a = jnp.arange(r, dtype=jnp.int32)[:, None]

    def unit(idx):
        ang = (idx % (2 * L)).astype(F32) * (math.pi / L)
        return jnp.cos(ang), jnp.sin(ang)

    c1, s1 = unit(a * r * t)
    c2, s2 = unit(a * t)
    c1, s1, c2, s2 = c1[:, None], s1[:, None], c2[None], s2[None]
    cos = (c1 * c2 - s1 * s2).reshape(L, L)
    sin = (s1 * c2 + c1 * s2).reshape(L, L)
    return cos.astype(BF16), sin.astype(BF16)


def _hy_spec_kernel(L, tr, c_ref, s_ref, klo_ref, khi_ref, p_ref, q_ref):
    r = pl.program_id(2)
    f = r * tr + lax.broadcasted_iota(jnp.int32, (tr, 1), 0)
    sgn = (1 - 2 * (f % 2)).astype(F32)
    scale = jnp.where(f == 0, 1.0, 2.0) * (1.0 / (2 * L))
    c, s = c_ref[...], s_ref[...]
    lo, hi = klo_ref[...], khi_ref[...]
    dot = functools.partial(jnp.dot, preferred_element_type=F32)
    p_ref[...] = scale * (dot(c, lo) + sgn * dot(c, hi))
    q_ref[...] = scale * (dot(s, lo) + sgn * dot(s, hi))


def _hy_spectrum(L, cos, sin, k_lo, k_hi):
    tr = min(L, 512)
    tc = 512
    kspec = pl.BlockSpec((None, L, tc), lambda o, j, r: (o, 0, j))
    tspec = pl.BlockSpec((tr, L), lambda o, j, r: (r, 0))
    ospec = pl.BlockSpec((None, tr, tc), lambda o, j, r: (o, r, j))
    return pl.pallas_call(
        functools.partial(_hy_spec_kernel, L, tr),
        out_shape=(jax.ShapeDtypeStruct((2, L, D), F32), jax.ShapeDtypeStruct((2, L, D), F32)),
        grid=(2, D // tc, L // tr),
        in_specs=[tspec, tspec, kspec, kspec],
        out_specs=(ospec, ospec),
        compiler_params=_cp("arbitrary", "arbitrary", "arbitrary"),
        name="hy_spectrum",
    )(cos, sin, k_lo.astype(BF16), k_hi.astype(BF16))


def _hy_core_kernel(L, tr, ngrp, tc, x1_ref, x2_ref, v_ref, sw1_ref, sw2_ref, swv_ref, sb1_ref, sb2_ref,
                    sbv_ref, c_ref, s_ref, p_ref, q_ref, kny_ref, skip_ref, *rest):
    o_ref, z0_sc, z1_sc, x2_sc, a_sc, b_sc, ny0_sc, ny1_sc = rest[-8:]
    fused = tr == L
    ph = pl.program_id(2)
    r = pl.program_id(3)
    W = ngrp * tc

    def phase(k):
        return (lambda f: f()) if fused else pl.when(ph == k)
    tcv = min(L, 512)
    halo = 16

    def alt_sign(start, n):
        t = start + lax.broadcasted_iota(jnp.int32, (n, 1), 0)
        return (1 - 2 * (t % 2)).astype(F32)

    def conv3(src_ref, g, a, w_ref, b_ref):
        x = src_ref[g, pl.ds(a, tcv), :].astype(F32)
        row = lax.broadcasted_iota(jnp.int32, (tcv, 1), 0)
        up_at = pl.multiple_of(jnp.maximum(a - halo, 0), halo)
        dn_at = pl.multiple_of(jnp.minimum(a + tcv, L - halo), halo)
        up = src_ref[g, pl.ds(up_at, halo), :][halo - 1:halo, :].astype(F32)
        dn = src_ref[g, pl.ds(dn_at, halo), :][0:1, :].astype(F32)
        up = jnp.where(a > 0, up, 0.0)
        dn = jnp.where(a + tcv < L, dn, 0.0)
        prev = jnp.where(row == 0, up, pltpu.roll(x, 1, 0))
        nxt = jnp.where(row == tcv - 1, dn, pltpu.roll(x, tcv - 1, 0))
        return prev * w_ref[0:1, :] + x * w_ref[1:2, :] + nxt * w_ref[2:3, :] + b_ref[...]

    @(phase(0) if fused else pl.when(jnp.logical_and(ph == 0, r == 0)))
    def _():
        ny0_sc[...] = jnp.zeros_like(ny0_sc)

        def conv_tile(ti, carry):
            a = pl.multiple_of(ti * tcv, tcv)
            rows_a = pl.ds(a, tcv)
            for g in range(ngrp):
                cols = slice(g * tc, (g + 1) * tc)
                z1_sc[rows_a, cols] = conv3(x1_ref, g, a, sw1_ref, sb1_ref).astype(BF16)
                x2_sc[rows_a, cols] = conv3(x2_ref, g, a, sw2_ref, sb2_ref).astype(BF16)
                z0_sc[rows_a, cols] = conv3(v_ref, g, a, swv_ref, sbv_ref).astype(BF16)
            ny0_sc[...] += jnp.sum(alt_sign(a, tcv) * z0_sc[rows_a, :].astype(F32), axis=0, keepdims=True)
            return carry

        lax.fori_loop(0, L // tcv, conv_tile, 0)

    start = pl.multiple_of(r * tr, tr)
    rows = pl.ds(start, tr)
    dot = functools.partial(jnp.dot, preferred_element_type=F32)

    def forward(order, z_sc):
        z = z_sc[...]
        zre = dot(c_ref[...], z)
        zim = dot(s_ref[...], z)
        pw = jnp.concatenate([p_ref[order] if fused else p_ref[...]] * ngrp, axis=1)
        qw = jnp.concatenate([q_ref[order] if fused else q_ref[...]] * ngrp, axis=1)
        a_sc[rows, :] = (zre * pw - zim * qw).astype(BF16)
        b_sc[rows, :] = (zim * pw + zre * qw).astype(BF16)

    def inverse(order, z_sc, ny_sc):
        y = dot(c_ref[...], a_sc[...]) + dot(s_ref[...], b_sc[...])
        kny = jnp.concatenate([kny_ref[order:order + 1, :]] * ngrp, axis=1)
        skip = jnp.concatenate([skip_ref[order:order + 1, :]] * ngrp, axis=1)
        return y + alt_sign(start, tr) * (ny_sc[...] * kny) + skip * z_sc[rows, :].astype(F32)

    @phase(0)
    def _():
        forward(0, z0_sc)

    @phase(1)
    def _():
        z1 = (z1_sc[rows, :].astype(F32) * inverse(0, z0_sc, ny0_sc)).astype(BF16)
        z1_sc[rows, :] = z1

        @pl.when(r == 0)
        def _():
            ny1_sc[...] = jnp.zeros_like(ny1_sc)

        ny1_sc[...] += jnp.sum(alt_sign(start, tr) * z1.astype(F32), axis=0, keepdims=True)

    @phase(2)
    def _():
        forward(1, z1_sc)

    @phase(3)
    def _():
        out = x2_sc[rows, :].astype(F32) * inverse(1, z1_sc, ny1_sc)
        for g in range(ngrp):
            o_ref[g, rows, :] = out[:, g * tc:(g + 1) * tc].astype(o_ref.dtype)


def _hy_core(proj, short_w, short_b, skip, cos, sin, pq, k_ny, o_prev, row0, nseq, L, ngrp, tc):
    T = proj.shape[0]
    tr = min(L, 256)
    nrt = L // tr
    assert row0 % (L * ngrp) == 0 and nseq % ngrp == 0 and T % L == 0
    sb0 = row0 // (L * ngrp)
    nct = D // tc
    p3 = proj.reshape(T // L, L, 3 * D)
    p_arr, q_arr = pq

    def xspec(part):
        mode = {} if nrt == 1 else dict(pipeline_mode=pl.Buffered(1))
        return pl.BlockSpec((ngrp, L, tc), lambda i, j, ph, r: (sb0 + i, 0, part * nct + j), **mode)

    def wspec(part, rows_):
        return pl.BlockSpec((rows_, tc), lambda i, j, ph, r: (0, part * nct + j))

    fused = nrt == 1
    if fused:
        pq_spec = pl.BlockSpec((2, tr, tc), lambda i, j, ph, r: (0, 0, j))
    else:
        pq_spec = pl.BlockSpec((None, tr, tc),
                               lambda i, j, ph, r: (ph // 2, jnp.where(ph % 2 == 0, r, nrt - 1), j))

    tspec = pl.BlockSpec((tr, L), lambda i, j, ph, r: (r, 0))
    in_specs = [xspec(0), xspec(1), xspec(2), wspec(0, 3), wspec(1, 3), wspec(2, 3),
                wspec(0, 1), wspec(1, 1), wspec(2, 1), tspec, tspec, pq_spec, pq_spec,
                pl.BlockSpec((2, tc), lambda i, j, ph, r: (0, j)),
                pl.BlockSpec((2, tc), lambda i, j, ph, r: (0, j))]
    sb = short_b.reshape(1, 3 * D)
    args = [p3, p3, p3, short_w, short_w, short_w, sb, sb, sb, cos, sin, p_arr, q_arr, k_ny, skip]
    aliases = {}
    if o_prev is not None:
        in_specs.append(pl.BlockSpec(memory_space=pl.ANY))
        args.append(o_prev.reshape(T // L, L, D))
        aliases = {len(args) - 1: 0}
    W = ngrp * tc
    out = pl.pallas_call(
        functools.partial(_hy_core_kernel, L, tr, ngrp, tc),
        out_shape=jax.ShapeDtypeStruct((T // L, L, D), BF16),
        grid=(nseq // ngrp, nct, 1 if fused else 4, nrt),
        in_specs=in_specs,
        out_specs=pl.BlockSpec((ngrp, L, tc), lambda i, j, ph, r: (sb0 + i, 0, j)),
        scratch_shapes=[pltpu.VMEM((L, W), BF16)] * 5 + [pltpu.VMEM((1, W), F32)] * 2,
        input_output_aliases=aliases,
        compiler_params=_cp("arbitrary", "arbitrary", "arbitrary", "arbitrary"),
        name="hy_core",
    )(*args)
    return out.reshape(T, D)


def _plain_out_kernel(x_ref, z_ref, g_ref, w_ref, b_ref, o_ref):
    acc = jnp.dot(z_ref[...], w_ref[...], preferred_element_type=F32) + b_ref[...]
    o_ref[...] = x_ref[...] + g_ref[...] * acc


def _plain_out(lay, x, mods, z, w, b):
    tm = lay.tile(1024)
    tn = 512
    kdim = z.shape[1]
    return pl.pallas_call(
        _plain_out_kernel,
        out_shape=jax.ShapeDtypeStruct((lay.T, D), F32),
        grid=(lay.T // tm, D // tn),
        in_specs=[pl.BlockSpec((tm, tn), lambda i, j: (i, j)),
                  pl.BlockSpec((tm, kdim), lambda i, j: (i, 0)),
                  pl.BlockSpec((None, 1, tn), lambda i, j: (lay.group(i * tm) * MOD_CHUNKS + 2, 0, j)),
                  pl.BlockSpec((kdim, tn), lambda i, j: (0, j)),
                  pl.BlockSpec((1, tn), lambda i, j: (0, j))],
        out_specs=pl.BlockSpec((tm, tn), lambda i, j: (i, j)),
        compiler_params=_cp("arbitrary", "arbitrary"),
        name="plain_out",
    )(x, z, mods, w, b.reshape(1, D))


def _hyena_layer(lay, x, mods, p):
    proj = _proj(lay, x, mods, 0, 1, p['w_in'].astype(BF16), p['b_in'].reshape(1, 3 * D), 768)
    z = jnp.zeros((lay.T, D), BF16)
    for row0, nseq, L, ngrp, tc in ((0, lay.B, lay.L, math.gcd(lay.B, 4), 256),
                                    (lay.TP, lay.NS, lay.LS, lay.NS, 256)):
        k_lo, k_hi, k_ny = _hyena_filters(L, p)
        cos, sin = _dft_tables(L)
        pq = _hy_spectrum(L, cos, sin, k_lo, k_hi)
        z = _hy_core(proj, p['short_w'], p['short_b'], p['skip'], cos, sin, pq, k_ny, z, row0, nseq, L, ngrp, tc)
    return _plain_out(lay, x, mods, z, p['w_out'].astype(BF16), p['b_out'])


_NT = (((1,), (1,)), ((), ()))
_TN = (((0,), (0,)), ((), ()))


def _tri(dr):
    t = lax.broadcasted_iota(jnp.int32, (CHUNK, CHUNK), 0)
    s = lax.broadcasted_iota(jnp.int32, (CHUNK, CHUNK), 1)
    return (s <= t) if dr == 0 else (s >= t)


def _chunk_cumsum(g, dr):
    n = g.shape[0]
    pos = lax.broadcasted_iota(jnp.int32, g.shape, 0) % CHUNK
    sh = 1
    while sh < CHUNK:
        if dr == 0:
            g = g + jnp.where(pos >= sh, pltpu.roll(g, sh, 0), 0.0)
        else:
            g = g + jnp.where(pos < CHUNK - sh, pltpu.roll(g, n - sh, 0), 0.0)
        sh *= 2
    return g


def _head_epilogue(o_sc, gate_ref, ng_ref, a_ref, center):
    rows = o_sc.shape[0]
    tr = math.gcd(rows, 256)

    def tile(i, carry):
        r = pl.ds(pl.multiple_of(i * tr, tr), tr)
        o = o_sc[r, :]
        if center:
            o = o - jnp.mean(o, axis=-1, keepdims=True)
        o = o * lax.rsqrt(jnp.mean(o * o, axis=-1, keepdims=True) + RMS_EPS) * ng_ref[...]
        a_ref[r, :] = (o * _silu(gate_ref[r, :].astype(F32))).astype(a_ref.dtype)
        return carry

    lax.fori_loop(0, rows // tr, tile, 0)


def _gla_kernel(cps, nseg, U, has_s0, want_final, *refs):
    q_ref, k_ref, v_ref, lr_ref, w2f_ref, w2b_ref, gb_ref, gate_ref, ng_ref = refs[:9]
    s0_ref = refs[9] if has_s0 else None
    qin_sc, kin_sc, kout_sc, dec_sc, st_sc, s_sc, s0t_sc, o_ref = refs[-8:]
    outs = refs[-10:-8] if want_final else refs[-9:-8]
    a_ref = outs[0]
    sf_ref = outs[1] if want_final else None
    C = CHUNK
    nsc = cps // U
    nchunks = nseg * cps
    rows_total = nchunks * C
    w2 = (w2f_ref, w2b_ref)

    for dr in range(2):
        pre = jnp.dot(lr_ref[...], w2[dr][...], preferred_element_type=F32) + gb_ref[dr:dr + 1, :]
        g = (jnp.minimum(pre, 0.0) - jnp.log(1.0 + jnp.exp(-jnp.abs(pre)))) * (1.0 / GLA_TAU)
        b = _chunk_cumsum(g, dr)
        b3 = b.reshape(nchunks, C, GLA_DK)
        tot = b3[:, C - 1:C, :] if dr == 0 else b3[:, 0:1, :]
        dec_sc[...] = jnp.exp(tot).reshape(nchunks, GLA_DK)
        k = k_ref[...].astype(F32)
        qin_sc[...] = (q_ref[...].astype(F32) * (GLA_DK ** -0.5) * jnp.exp(b)).astype(BF16)
        kin_sc[...] = (k * jnp.exp(-b)).astype(BF16)
        kout_sc[...] = (k * jnp.exp(tot - b3).reshape(rows_total, GLA_DK)).astype(BF16)
        if has_s0:
            s0t_sc[...] = jnp.transpose(s0_ref[dr], (1, 0))
        tri = _tri(dr)

        def super_chunk(jj, carry, dr=dr, tri=tri):
            j = jj if dr == 0 else nseg * nsc - 1 - jj
            in_seg = j % nsc
            first = (in_seg == 0) if dr == 0 else (in_seg == nsc - 1)
            last = (in_seg == nsc - 1) if dr == 0 else (in_seg == 0)

            @pl.when(first)
            def _():
                s_sc[...] = s0t_sc[...] if has_s0 else jnp.zeros_like(s_sc)

            base = j * (U * C)
            for u in range(U):
                rows = pl.ds(pl.multiple_of(base + u * C, C), C)
                v = v_ref[rows, :]
                sc = lax.dot_general(qin_sc[rows, :], kin_sc[rows, :], _NT, preferred_element_type=F32)
                o = jnp.dot(jnp.where(tri, sc, 0.0).astype(BF16), v, preferred_element_type=F32)
                st_sc[u] = lax.dot_general(v, kout_sc[rows, :], _TN, preferred_element_type=F32)
                if dr == 0:
                    o_ref[rows, :] = o
                else:
                    o_ref[rows, :] += o
            s = s_sc[...]
            for u in (range(U) if dr == 0 else reversed(range(U))):
                kv = st_sc[u]
                st_sc[u] = s
                s = dec_sc[pl.ds(j * U + u, 1), :] * s + kv
            s_sc[...] = s
            for u in range(U):
                rows = pl.ds(pl.multiple_of(base + u * C, C), C)
                o_ref[rows, :] += lax.dot_general(qin_sc[rows, :], st_sc[u].astype(BF16), _NT,
                                                  preferred_element_type=F32)
            if want_final:
                @pl.when(last)
                def _():
                    sf_ref[j // nsc, dr] = jnp.transpose(s, (1, 0))
            return carry

        lax.fori_loop(0, nseg * nsc, super_chunk, 0)

    _head_epilogue(o_ref, gate_ref, ng_ref, a_ref, center=False)


def _gla_core(proj, w2f, w2b, gate_b, norm_g, s0, o_prev, row0, nseq, seqlen, nseg, want_final):
    T = proj.shape[0]
    rows = nseg * seqlen
    cps = seqlen // CHUNK
    U = math.gcd(cps, 8)
    assert row0 % rows == 0 and seqlen % CHUNK == 0 and nseq % nseg == 0
    rb = row0 // rows
    hk = GLA_H * GLA_DK
    has_s0 = s0 is not None
    assert not has_s0 or nseg == 1
    in_specs = [pl.BlockSpec((rows, GLA_DK), lambda b, h: (rb + b, h)),
                pl.BlockSpec((rows, GLA_DK), lambda b, h: (rb + b, GLA_H + h)),
                pl.BlockSpec((rows, GLA_DV), lambda b, h: (rb + b, 2 * hk // GLA_DV + h)),
                pl.BlockSpec((rows, 128), lambda b, h: (rb + b, (2 * hk + 2 * GLA_H * GLA_DV) // 128)),
                pl.BlockSpec((128, GLA_DK), lambda b, h: (0, h)),
                pl.BlockSpec((128, GLA_DK), lambda b, h: (0, h)),
                pl.BlockSpec((2, GLA_DK), lambda b, h: (0, h)),
                pl.BlockSpec((rows, GLA_DV), lambda b, h: (rb + b, (2 * hk) // GLA_DV + GLA_H + h)),
                pl.BlockSpec((1, GLA_DV), lambda b, h: (0, 0))]
    args = [proj, proj, proj, proj, w2f, w2b, gate_b, proj, norm_g.reshape(1, GLA_DV)]
    if has_s0:
        in_specs.append(pl.BlockSpec((None, 2, None, GLA_DK, GLA_DV), lambda b, h: (b, 0, h, 0, 0)))
        args.append(s0)
    in_specs.append(pl.BlockSpec(memory_space=pl.ANY))
    args.append(o_prev)
    aliases = {len(args) - 1: 0}
    out_shape = [jax.ShapeDtypeStruct((T, GLA_H * GLA_DV), BF16)]
    out_specs = [pl.BlockSpec((rows, GLA_DV), lambda b, h: (rb + b, h))]
    if want_final:
        out_shape.append(jax.ShapeDtypeStruct((nseq, 2, GLA_H, GLA_DK, GLA_DV), F32))
        out_specs.append(pl.BlockSpec((nseg, 2, None, GLA_DK, GLA_DV), lambda b, h: (b, 0, h, 0, 0)))
    outs = pl.pallas_call(
        functools.partial(_gla_kernel, cps, nseg, U, has_s0, want_final),
        out_shape=tuple(out_shape),
        grid=(nseq // nseg, GLA_H),
        in_specs=in_specs,
        out_specs=tuple(out_specs),
        scratch_shapes=[pltpu.VMEM((rows, GLA_DK), BF16)] * 3
        + [pltpu.VMEM((nseg * cps, GLA_DK), F32), pltpu.VMEM((U, GLA_DV, GLA_DK), F32),
           pltpu.VMEM((GLA_DV, GLA_DK), F32), pltpu.VMEM((GLA_DV, GLA_DK), F32),
           pltpu.VMEM((rows, GLA_DV), F32)],
        input_output_aliases=aliases,
        compiler_params=_cp("arbitrary", "arbitrary"),
        name="gla",
    )(*args)
    return (outs[0], outs[1]) if want_final else (outs[0], None)


def _ret_kernel(cps, nseg, U, has_s0, want_final, rope, *refs):
    q_ref, k_ref, v_ref, dm_ref, qd_ref, kd_ref, cd_ref, gate_ref, ng_ref = refs[:9]
    nxt = 9
    if rope:
        cos_ref, sin_ref = refs[9:11]
        nxt = 11
    s0_ref = refs[nxt] if has_s0 else None
    qr_sc, kr_sc, qd_sc, kd_sc, st_sc, s_sc, o_ref = refs[-7:]
    outs = refs[-9:-7] if want_final else refs[-8:-7]
    a_ref = outs[0]
    sf_ref = outs[1] if want_final else None
    C = RET_CHUNK
    nsc = cps // U
    R = U * C
    SB = 64

    def rot(x, rows):
        if not rope:
            return x
        half = x.shape[1] // 2
        swapped = jnp.concatenate([pltpu.roll(x[:, :half], half // 2, 1),
                                   pltpu.roll(x[:, half:], half // 2, 1)], axis=1)
        return x * cos_ref[rows, :] + swapped * sin_ref[rows, :]

    for dr in range(2):
        def super_chunk(jj, carry, dr=dr):
            j = jj if dr == 0 else nseg * nsc - 1 - jj
            in_seg = j % nsc
            first = (in_seg == 0) if dr == 0 else (in_seg == nsc - 1)
            last = (in_seg == nsc - 1) if dr == 0 else (in_seg == 0)

            @pl.when(first)
            def _():
                s_sc[...] = s0_ref[dr] if has_s0 else jnp.zeros_like(s_sc)

            base = pl.multiple_of(j * R, R)
            rows_r = pl.ds(base, R)
            q = rot(q_ref[rows_r, :].astype(F32), rows_r)
            k = rot(k_ref[rows_r, :].astype(F32), rows_r) * (RET_DK ** -0.5)
            qr_sc[...] = q.astype(BF16)
            kr_sc[...] = k.astype(BF16)
            qd_sc[...] = (q.reshape(U, C, RET_DK) * qd_ref[dr][None]).reshape(R, RET_DK).astype(BF16)
            kd_sc[...] = (k.reshape(U, C, RET_DK) * kd_ref[dr][None]).reshape(R, RET_DK).astype(BF16)
            for u in range(U):
                loc = pl.ds(u * C, C)
                rows = pl.ds(pl.multiple_of(base + u * C, C), C)
                v = v_ref[rows, :]
                sc = lax.dot_general(qr_sc[loc, :], kr_sc[loc, :], _NT, preferred_element_type=F32)
                o = jnp.dot((sc * dm_ref[dr]).astype(BF16), v, preferred_element_type=F32)
                st_sc[u] = lax.dot_general(kd_sc[loc, :], v, _TN, preferred_element_type=F32)
                if dr == 0:
                    o_ref[rows, :] = o
                else:
                    o_ref[rows, :] += o
            cd = cd_ref[dr]
            for r0 in range(0, RET_DK, SB):
                srows = pl.ds(r0, SB)
                s = s_sc[srows, :]
                for u in (range(U) if dr == 0 else reversed(range(U))):
                    kv = st_sc[u, srows, :]
                    st_sc[u, srows, :] = s
                    s = cd * s + kv
                s_sc[srows, :] = s
            for u in range(U):
                rows = pl.ds(pl.multiple_of(base + u * C, C), C)
                o_ref[rows, :] += jnp.dot(qd_sc[pl.ds(u * C, C), :], st_sc[u].astype(BF16),
                                          preferred_element_type=F32)
            if want_final:
                @pl.when(last)
                def _():
                    sf_ref[j // nsc, dr] = s_sc[...]
            return carry

        lax.fori_loop(0, nseg * nsc, super_chunk, 0)

    _head_epilogue(o_ref, gate_ref, ng_ref, a_ref, center=True)


def _ret_tables(log_decay):
    C = RET_CHUNK
    lg = log_decay.astype(F32)[:, :, None, None]
    t = jnp.arange(C, dtype=F32)[:, None]
    s = jnp.arange(C, dtype=F32)[None, :]
    lag = jnp.stack([t - s, s - t])[:, None]
    dmask = jnp.where(lag >= 0, jnp.exp(jnp.maximum(lag, 0.0) * lg), 0.0)
    tl = jnp.arange(C, dtype=F32)[None, None, :, None]
    qdec = jnp.concatenate([jnp.exp((tl + 1.0) * lg[0:1]), jnp.exp((C - tl) * lg[1:2])], axis=0)
    kdec = jnp.concatenate([jnp.exp((C - 1.0 - tl) * lg[0:1]), jnp.exp(tl * lg[1:2])], axis=0)
    cdec = jnp.exp(C * lg)
    return dmask, qdec, kdec, cdec


def _rope_tables(seqlen, dk):
    half = dk // 2
    nf = half // 2
    pos = jnp.arange(seqlen, dtype=jnp.int32)
    inv = ROPE_BASE ** (-jnp.arange(nf, dtype=F32) / nf)
    ang_r = (pos // GRID_W).astype(F32)[:, None] * inv[None, :]
    ang_c = (pos % GRID_W).astype(F32)[:, None] * inv[None, :]
    cos = jnp.concatenate([jnp.cos(ang_r)] * 2 + [jnp.cos(ang_c)] * 2, axis=1)
    sin = jnp.concatenate([-jnp.sin(ang_r), jnp.sin(ang_r), -jnp.sin(ang_c), jnp.sin(ang_c)], axis=1)
    return cos, sin


def _ret_core(proj, tabs, norm_g, s0, o_prev, row0, nseq, seqlen, nseg, want_final, rope):
    T = proj.shape[0]
    rows = nseg * seqlen
    C = RET_CHUNK
    cps = seqlen // C
    U = math.gcd(cps, 4)
    assert row0 % rows == 0 and seqlen % C == 0 and nseq % nseg == 0
    rb = row0 // rows
    hk, hv = RET_H * RET_DK, RET_H * RET_DV
    has_s0 = s0 is not None
    assert not (has_s0 or rope) or nseg == 1
    tspec = lambda r, c: pl.BlockSpec((2, None, r, c), lambda b, h: (0, h, 0, 0))
    mode = dict(pipeline_mode=pl.Buffered(1)) if rows * RET_DV * 2 >= (4 << 20) else {}
    in_specs = [pl.BlockSpec((rows, RET_DK), lambda b, h: (rb + b, h), **mode),
                pl.BlockSpec((rows, RET_DK), lambda b, h: (rb + b, RET_H + h), **mode),
                pl.BlockSpec((rows, RET_DV), lambda b, h: (rb + b, 2 * hk // RET_DV + h), **mode),
                tspec(C, C), tspec(C, 1), tspec(C, 1), tspec(1, 1),
                pl.BlockSpec((rows, RET_DV), lambda b, h: (rb + b, (2 * hk + hv) // RET_DV + h), **mode),
                pl.BlockSpec((1, RET_DV), lambda b, h: (0, 0))]
    args = [proj, proj, proj, *tabs, proj, norm_g.reshape(1, RET_DV)]
    if rope:
        cos, sin = _rope_tables(seqlen, RET_DK)
        in_specs += [pl.BlockSpec((seqlen, RET_DK), lambda b, h: (0, 0), pipeline_mode=pl.Buffered(1))] * 2
        args += [cos, sin]
    if has_s0:
        in_specs.append(pl.BlockSpec((None, 2, None, RET_DK, RET_DV), lambda b, h: (b, 0, h, 0, 0)))
        args.append(s0)
    in_specs.append(pl.BlockSpec(memory_space=pl.ANY))
    args.append(o_prev)
    aliases = {len(args) - 1: 0}
    out_shape = [jax.ShapeDtypeStruct((T, hv), BF16)]
    out_specs = [pl.BlockSpec((rows, RET_DV), lambda b, h: (rb + b, h))]
    if want_final:
        out_shape.append(jax.ShapeDtypeStruct((nseq, 2, RET_H, RET_DK, RET_DV), F32))
        out_specs.append(pl.BlockSpec((nseg, 2, None, RET_DK, RET_DV), lambda b, h: (b, 0, h, 0, 0)))
    outs = pl.pallas_call(
        functools.partial(_ret_kernel, cps, nseg, U, has_s0, want_final, rope),
        out_shape=tuple(out_shape),
        grid=(nseq // nseg, RET_H),
        in_specs=in_specs,
        out_specs=tuple(out_specs),
        scratch_shapes=[pltpu.VMEM((U * C, RET_DK), BF16)] * 4
        + [pltpu.VMEM((U, RET_DK, RET_DV), F32), pltpu.VMEM((RET_DK, RET_DV), F32),
           pltpu.VMEM((rows, RET_DV), F32)],
        input_output_aliases=aliases,
        compiler_params=_cp("arbitrary", "arbitrary"),
        name="ret",
    )(*args)
    return (outs[0], outs[1]) if want_final else (outs[0], None)


def _gla_layer(lay, x, mods, p, s0):
    hk, hv = GLA_H * GLA_DK, GLA_H * GLA_DV
    w_all = jnp.concatenate([p['w_in'], p['gate_w1'][0], p['gate_w1'][1],
                             jnp.zeros((D, 128 - 2 * GLA_RANK), F32)], axis=1).astype(BF16)
    proj = _proj(lay, x, mods, 0, 1, w_all, jnp.zeros((1, w_all.shape[1]), F32), 640)
    pad = lambda w, lo: jnp.pad(w, ((lo, 128 - GLA_RANK - lo), (0, 0))).astype(BF16)
    w2f, w2b = pad(p['gate_w2'][0], 0), pad(p['gate_w2'][1], GLA_RANK)
    a = jnp.zeros((lay.T, hv), BF16)
    a, s_fin = _gla_core(proj, w2f, w2b, p['gate_b'], p['norm_g'], None, a, 0, lay.B, lay.L,
                         math.gcd(lay.B, 8), True)
    a, _ = _gla_core(proj, w2f, w2b, p['gate_b'], p['norm_g'], s0, a, lay.TP, lay.NS, lay.LS, 1, False)
    x = _plain_out(lay, x, mods, a, p['w_out'].astype(BF16), jnp.zeros((D,), F32))
    return x, s_fin


def _ret_layer(lay, x, mods, p, s0):
    hk, hv = RET_H * RET_DK, RET_H * RET_DV
    proj = _proj(lay, x, mods, 0, 1, p['w_in'].astype(BF16), jnp.zeros((1, 2 * hk + 2 * hv), F32), 1536)
    tabs = _ret_tables(p['log_decay'])
    a = jnp.zeros((lay.T, hv), BF16)
    a, s_fin = _ret_core(proj, tabs, p['norm_g'], None, a, 0, lay.B, lay.L, math.gcd(lay.B, 8), True, False)
    a, _ = _ret_core(proj, tabs, p['norm_g'], s0, a, lay.TP, lay.NS, lay.LS, 1, False, True)
    x = _plain_out(lay, x, mods, a, p['w_out'].astype(BF16), jnp.zeros((D,), F32))
    return x, s_fin


MOE_BM = 512
EXPERT_TF = 1792
ROUTER_LANES = 128
DMA_UNROLL = 8


def _router_kernel(x_ref, sh_ref, sc_ref, rw_ref, h_ref, idx_ref, gate_ref, rank_ref, cnt_ref):
    @pl.when(pl.program_id(0) == 0)
    def _():
        cnt_ref[...] = jnp.zeros_like(cnt_ref)

    h = _modulate(x_ref[...], sh_ref[...], sc_ref[...])
    h_ref[...] = h
    logits = jnp.dot(h, rw_ref[...], precision=HIGHEST, preferred_element_type=F32)
    lane = lax.broadcasted_iota(jnp.int32, logits.shape, 1)
    neg = jnp.float32(-jnp.inf)
    logits = jnp.where(lane < N_EXPERTS, logits, neg)
    m1 = jnp.max(logits, axis=-1, keepdims=True)
    i1 = jnp.min(jnp.where(logits == m1, lane, ROUTER_LANES), axis=-1, keepdims=True)
    rest = jnp.where(lane == i1, neg, logits)
    m2 = jnp.max(rest, axis=-1, keepdims=True)
    i2 = jnp.min(jnp.where(rest == m2, lane, ROUTER_LANES), axis=-1, keepdims=True)
    e2 = jnp.exp(m2 - m1)
    g1 = 1.0 / (1.0 + e2)
    idx_ref[:, 0:1] = i1
    idx_ref[:, 1:2] = i2
    gate_ref[:, 0:1] = g1
    gate_ref[:, 1:2] = e2 * g1
    tm = logits.shape[0]
    sel1 = lane == i1
    sel2 = lane == i2
    picked = jnp.where(jnp.logical_or(sel1, sel2), 1.0, 0.0)
    before = (lax.broadcasted_iota(jnp.int32, (tm, tm), 1)
              < lax.broadcasted_iota(jnp.int32, (tm, tm), 0)).astype(BF16)
    prior = jnp.dot(before, picked.astype(BF16), preferred_element_type=F32) + cnt_ref[...]
    rank_ref[:, 0:1] = jnp.sum(jnp.where(sel1, prior, 0.0), axis=-1, keepdims=True).astype(jnp.int32)
    rank_ref[:, 1:2] = jnp.sum(jnp.where(sel2, prior, 0.0), axis=-1, keepdims=True).astype(jnp.int32)
    cnt_ref[...] += jnp.sum(picked, axis=0, keepdims=True)


def _router(lay, x, mods, router_w):
    tm = lay.tile(512)
    rw = jnp.pad(router_w, ((0, 0), (0, ROUTER_LANES - N_EXPERTS)))
    return pl.pallas_call(
        _router_kernel,
        out_shape=(jax.ShapeDtypeStruct((lay.T, D), F32),
                   jax.ShapeDtypeStruct((lay.T, 2), jnp.int32),
                   jax.ShapeDtypeStruct((lay.T, 2), F32),
                   jax.ShapeDtypeStruct((lay.T, 2), jnp.int32),
                   jax.ShapeDtypeStruct((1, ROUTER_LANES), F32)),
        grid=(lay.T // tm,),
        in_specs=[pl.BlockSpec((tm, D), lambda i: (i, 0)),
                  _mod_spec(lay, tm, 3, 1), _mod_spec(lay, tm, 4, 1),
                  pl.BlockSpec((D, ROUTER_LANES), lambda i: (0, 0))],
        out_specs=(pl.BlockSpec((tm, D), lambda i: (i, 0)),
                   pl.BlockSpec((tm, 2), lambda i: (i, 0)),
                   pl.BlockSpec((tm, 2), lambda i: (i, 0)),
                   pl.BlockSpec((tm, 2), lambda i: (i, 0)),
                   pl.BlockSpec((1, ROUTER_LANES), lambda i: (0, 0))),
        compiler_params=_cp("arbitrary"),
        name="router",
    )(x, mods, mods, rw)


def _moe_plan(idx, rank, counts, bm):
    a = idx.size
    counts = counts[0, :N_EXPERTS].astype(jnp.int32)
    padded = (counts + bm - 1) // bm * bm
    pad_end = jnp.cumsum(padded)
    pad_start = pad_end - padded
    hit = idx[..., None] == jnp.arange(N_EXPERTS, dtype=jnp.int32)
    dest = (rank + jnp.sum(jnp.where(hit, pad_start, 0), axis=-1)).reshape(a).astype(jnp.int32)
    nb = -(-(a + N_EXPERTS * (bm - 1)) // bm)
    block_start = jnp.arange(nb, dtype=jnp.int32) * bm
    block_e = jnp.sum((block_start[:, None] >= pad_end[None, :]).astype(jnp.int32), axis=1)
    block_e = jnp.minimum(block_e, N_EXPERTS - 1)
    nvalid = (pad_end[-1] // bm).astype(jnp.int32).reshape(1)
    fill = jnp.concatenate([pad_start + counts, pad_end, nvalid]).astype(jnp.int32)
    return dest, fill, block_e, nvalid, nb


def _dispatch_kernel(tm, bm, nb, dest_ref, fill_ref, h_ref, xs_hbm, zero_sc, sem, zsem):
    i = pl.program_id(0)
    zr = zero_sc.shape[0]

    @pl.when(i == 0)
    def _():
        zero_sc[...] = jnp.zeros_like(zero_sc)

        def zero_row(r):
            return pltpu.make_async_copy(zero_sc.at[pl.ds(0, 1)], xs_hbm.at[pl.ds(r, 1)], zsem)

        def zero_rows(r):
            return pltpu.make_async_copy(zero_sc, xs_hbm.at[pl.ds(pl.multiple_of(r, zr), zr)], zsem)

        for e in range(N_EXPERTS):
            lo, hi = fill_ref[e], fill_ref[N_EXPERTS + e]
            lax.fori_loop(lo, hi, lambda r, c: (zero_row(r).start(), c)[1], 0)
            lax.fori_loop(lo, hi, lambda r, c: (zero_row(r).wait(), c)[1], 0)
        lo, hi = fill_ref[2 * N_EXPERTS] * (bm // zr), nb * (bm // zr)
        lax.fori_loop(lo, hi, lambda q, c: (zero_rows(q * zr).start(), c)[1], 0)
        lax.fori_loop(lo, hi, lambda q, c: (zero_rows(q * zr).wait(), c)[1], 0)

    def row_copy(r, dst):
        return pltpu.make_async_copy(h_ref.at[pl.ds(r, 1)], xs_hbm.at[pl.ds(dst, 1)], sem)

    def issue(r, carry):
        a = 2 * (i * tm + r)
        row_copy(r, dest_ref[a]).start()
        row_copy(r, dest_ref[a + 1]).start()
        return carry

    def drain(r, carry):
        row_copy(r, 0).wait()
        row_copy(r, 0).wait()
        return carry

    lax.fori_loop(0, tm, issue, 0, unroll=DMA_UNROLL)
    lax.fori_loop(0, tm, drain, 0, unroll=DMA_UNROLL)


def _dispatch(lay, h, dest, fill, nb, bm):
    tm = lay.tile(512)
    grid_spec = pltpu.PrefetchScalarGridSpec(
        num_scalar_prefetch=2,
        grid=(lay.T // tm,),
        in_specs=[pl.BlockSpec((tm, D), lambda i, d, f: (i, 0))],
        out_specs=pl.BlockSpec(memory_space=pl.ANY),
        scratch_shapes=[pltpu.VMEM((64, D), F32), pltpu.SemaphoreType.DMA(()), pltpu.SemaphoreType.DMA(())],
    )
    return pl.pallas_call(
        functools.partial(_dispatch_kernel, tm, bm, nb),
        out_shape=jax.ShapeDtypeStruct((nb * bm, D), F32),
        grid_spec=grid_spec,
        compiler_params=_cp("arbitrary"),
        name="dispatch",
    )(dest, fill, h)


def _experts_kernel(nf, be_ref, nv_ref, xs_ref, wa_ref, wb_ref, wo_ref, o_ref, xb_sc, acc_sc):
    i = pl.program_id(0)
    f = pl.program_id(1)
    valid = i < nv_ref[0]

    @pl.when(jnp.logical_and(valid, f == 0))
    def _():
        xb_sc[...] = xs_ref[...].astype(BF16)

    @pl.when(valid)
    def _():
        xb = xb_sc[...]
        a = jnp.dot(xb, wa_ref[...], preferred_element_type=F32)
        b = jnp.dot(xb, wb_ref[...], preferred_element_type=F32)
        h = (_silu(a) * b).astype(BF16)
        y = jnp.dot(h, wo_ref[...], preferred_element_type=F32)

        @pl.when(f == 0)
        def _():
            acc_sc[...] = y

        @pl.when(f > 0)
        def _():
            acc_sc[...] += y

    @pl.when(f == nf - 1)
    def _():
        o_ref[...] = jnp.where(valid, acc_sc[...], 0.0)


def _experts(xs, block_e, nvalid, nb, bm, w_in, w_out):
    tf = EXPERT_TF
    nf = EXPERT_DIM // tf

    def wmap(off):
        def imap(i, f, be, nv):
            fe = jnp.where(i < nv[0], f, nf - 1)
            return (be[i], 0, off + fe)
        return imap

    def womap(i, f, be, nv):
        fe = jnp.where(i < nv[0], f, nf - 1)
        return (be[i], fe, 0)

    grid_spec = pltpu.PrefetchScalarGridSpec(
        num_scalar_prefetch=2,
        grid=(nb, nf),
        in_specs=[pl.BlockSpec((bm, D), lambda i, f, be, nv: (jnp.minimum(i, nv[0] - 1), 0)),
                  pl.BlockSpec((None, D, tf), wmap(0)),
                  pl.BlockSpec((None, D, tf), wmap(nf)),
                  pl.BlockSpec((None, tf, D), womap)],
        out_specs=pl.BlockSpec((bm, D), lambda i, f, be, nv: (i, 0)),
        scratch_shapes=[pltpu.VMEM((bm, D), BF16), pltpu.VMEM((bm, D), F32)],
    )
    return pl.pallas_call(
        functools.partial(_experts_kernel, nf),
        out_shape=jax.ShapeDtypeStruct((nb * bm, D), F32),
        grid_spec=grid_spec,
        compiler_params=_cp("arbitrary", "arbitrary"),
        name="experts",
    )(block_e, nvalid, xs, w_in, w_in, w_out)


def _combine_kernel(tm, nt, final, dest_ref, x_ref, gate_ref, g_ref, fg_ref, ys_hbm, o_ref, y_sc, sem):
    i = pl.program_id(0)
    slot = i % 2

    def row_copy(s, k, r, src):
        return pltpu.make_async_copy(ys_hbm.at[pl.ds(src, 1)], y_sc.at[s, k, pl.ds(r, 1)], sem.at[s])

    def issue_tile(t, s):
        def issue(r, carry):
            a = 2 * (t * tm + r)
            row_copy(s, 0, r, dest_ref[a]).start()
            row_copy(s, 1, r, dest_ref[a + 1]).start()
            return carry
        lax.fori_loop(0, tm, issue, 0, unroll=DMA_UNROLL)

    @pl.when(i == 0)
    def _():
        issue_tile(0, 0)

    @pl.when(i + 1 < nt)
    def _():
        issue_tile(i + 1, 1 - slot)

    def drain(r, carry):
        row_copy(slot, 0, r, 0).wait()
        row_copy(slot, 1, r, 0).wait()
        return carry

    lax.fori_loop(0, tm, drain, 0, unroll=DMA_UNROLL)
    gate = gate_ref[...]
    out = x_ref[...] + g_ref[...] * (gate[:, 0:1] * y_sc[slot, 0] + gate[:, 1:2] * y_sc[slot, 1])
    if final:
        ms = jnp.mean(out * out, axis=-1, keepdims=True)
        out = out * lax.rsqrt(ms + RMS_EPS) * fg_ref[...]
    o_ref[...] = out


def _combine(lay, x, mods, gates, ys, dest, final_g):
    tm = lay.tile(256)
    nt = lay.T // tm
    final = final_g is not None
    fg = (final_g if final else jnp.ones((D,), F32)).reshape(1, D)
    grid_spec = pltpu.PrefetchScalarGridSpec(
        num_scalar_prefetch=1,
        grid=(nt,),
        in_specs=[pl.BlockSpec((tm, D), lambda i, d: (i, 0)),
                  pl.BlockSpec((tm, 2), lambda i, d: (i, 0)),
                  pl.BlockSpec((None, 1, D), lambda i, d: (lay.group(i * tm) * MOD_CHUNKS + 5, 0, 0)),
                  pl.BlockSpec((1, D), lambda i, d: (0, 0)),
                  pl.BlockSpec(memory_space=pl.ANY)],
        out_specs=pl.BlockSpec((tm, D), lambda i, d: (i, 0)),
        scratch_shapes=[pltpu.VMEM((2, 2, tm, D), F32), pltpu.SemaphoreType.DMA((2,))],
    )
    return pl.pallas_call(
        functools.partial(_combine_kernel, tm, nt, final),
        out_shape=jax.ShapeDtypeStruct((lay.T, D), F32),
        grid_spec=grid_spec,
        compiler_params=_cp("arbitrary"),
        name="combine",
    )(dest, x, gates, mods, fg, ys)


def _moe_layer(lay, x, mods, router_w, w_in, w_out, final_g=None, bm=MOE_BM):
    h, idx, gates, rank, counts = _router(lay, x, mods, router_w)
    dest, fill, block_e, nvalid, nb = _moe_plan(idx, rank, counts, bm)
    xs = _dispatch(lay, h, dest, fill, nb, bm)
    ys = _experts(xs, block_e, nvalid, nb, bm, w_in, w_out)
    return _combine(lay, x, mods, gates, ys, dest, final_g)


def kernel(x_prompt, x_sample, c, state_l0_s5_re, state_l0_s5_im, state_l2_gla, state_l3_ret, c_ctx, l0_mod_w, l0_mod_b, l0_s5_a_re, l0_s5_a_im, l0_s5_log_dt, l0_s5_b_re, l0_s5_b_im, l0_s5_c_re, l0_s5_c_im, l0_s5_d, l0_s5_glu_w, l0_ffn_w_in, l0_ffn_w_out, l1_mod_w, l1_mod_b, l1_hy_w_in, l1_hy_b_in, l1_hy_short_w, l1_hy_short_b, l1_hy_f_w1, l1_hy_f_b1, l1_hy_f_w2, l1_hy_f_b2, l1_hy_f_w3, l1_hy_f_freq, l1_hy_skip, l1_hy_w_out, l1_hy_b_out, l1_moe_router, l1_moe_w_in, l1_moe_w_out, l2_mod_w, l2_mod_b, l2_gla_w_in, l2_gla_gate_w1, l2_gla_gate_w2, l2_gla_gate_b, l2_gla_norm_g, l2_gla_w_out, l2_ffn_w_in, l2_ffn_w_out, l3_mod_w, l3_mod_b, l3_ret_w_in, l3_ret_log_decay, l3_ret_norm_g, l3_ret_w_out, l3_moe_router, l3_moe_w_in, l3_moe_w_out, final_norm_g):
    B, L, _ = x_prompt.shape
    NS, LS, _ = x_sample.shape
    lay = Layout(B, L, NS, LS)
    x = jnp.concatenate([x_prompt.reshape(B * L, D), x_sample.reshape(NS * LS, D)], axis=0)
    cond = jnp.concatenate([c_ctx[None], c, jnp.zeros((8 - 1 - NS, D), F32)], axis=0)
    mods0 = _mods(cond, l0_mod_w, l0_mod_b)
    p0 = dict(a_re=l0_s5_a_re, a_im=l0_s5_a_im, log_dt=l0_s5_log_dt, b_re=l0_s5_b_re, b_im=l0_s5_b_im,
              c_re=l0_s5_c_re, c_im=l0_s5_c_im, d=l0_s5_d, glu_w=l0_s5_glu_w.astype(BF16))
    x, s5_re, s5_im = _s5_layer(lay, x, mods0, p0, state_l0_s5_re, state_l0_s5_im)
    x = _ffn(lay, x, mods0, l0_ffn_w_in.astype(BF16), l0_ffn_w_out.astype(BF16))

    mods1 = _mods(cond, l1_mod_w, l1_mod_b)
    p1 = dict(w_in=l1_hy_w_in, b_in=l1_hy_b_in, short_w=l1_hy_short_w, short_b=l1_hy_short_b,
              f_w1=l1_hy_f_w1, f_b1=l1_hy_f_b1, f_w2=l1_hy_f_w2, f_b2=l1_hy_f_b2, f_w3=l1_hy_f_w3,
              f_freq=l1_hy_f_freq, skip=l1_hy_skip, w_out=l1_hy_w_out, b_out=l1_hy_b_out)
    x = _hyena_layer(lay, x, mods1, p1)
    x = _moe_layer(lay, x, mods1, l1_moe_router, l1_moe_w_in.astype(BF16), l1_moe_w_out.astype(BF16))

    mods2 = _mods(cond, l2_mod_w, l2_mod_b)
    p2 = dict(w_in=l2_gla_w_in, gate_w1=l2_gla_gate_w1, gate_w2=l2_gla_gate_w2, gate_b=l2_gla_gate_b,
              norm_g=l2_gla_norm_g, w_out=l2_gla_w_out)
    x, gla_state = _gla_layer(lay, x, mods2, p2, state_l2_gla)
    x = _ffn(lay, x, mods2, l2_ffn_w_in.astype(BF16), l2_ffn_w_out.astype(BF16))

    mods3 = _mods(cond, l3_mod_w, l3_mod_b)
    p3 = dict(w_in=l3_ret_w_in, log_decay=l3_ret_log_decay, norm_g=l3_ret_norm_g, w_out=l3_ret_w_out)
    x, ret_state = _ret_layer(lay, x, mods3, p3, state_l3_ret)
    y = _moe_layer(lay, x, mods3, l3_moe_router, l3_moe_w_in.astype(BF16), l3_moe_w_out.astype(BF16),
                   final_g=final_norm_g)
    return (y[:lay.TP].reshape(B, L, D), y[lay.TP:].reshape(NS, LS, D), s5_re, s5_im, gla_state, ret_state)
```

```python
import functools
import math

import jax
import jax.numpy as jnp
import numpy as np
from jax import lax
from jax.experimental import pallas as pl
from jax.experimental.pallas import tpu as pltpu

F32 = jnp.float32
BF16 = jnp.bfloat16
HIGHEST = lax.Precision.HIGHEST

D = 1024
RMS_EPS = 1e-6
MOD_CHUNKS = 6
GRID_W = 64

S5_Q = 16
S5_G = D // S5_Q
S5_P = 64
S5_T = 16
S5_SCAN_ROWS = 64

HY_BANDS = 16
HY_TARGET = 1e-2
HY_FAST_PCT = 0.3
HY_SLOW_PCT = 1.5

GLA_H, GLA_DK, GLA_DV = 4, 128, 256
GLA_RANK = 16
GLA_TAU = 16.0
RET_H, RET_DK, RET_DV = 4, 256, 512
CHUNK = 64
RET_CHUNK = 256
ROPE_BASE = 10000.0

FFN_DIM = 2816
N_EXPERTS = 8
EXPERT_DIM = 3584

VMEM_LIMIT_V7X = 56 * 1024 * 1024


def _cp(*sem):
    return pltpu.CompilerParams(dimension_semantics=sem, vmem_limit_bytes=VMEM_LIMIT_V7X)


def _silu(x):
    return x * jax.nn.sigmoid(x)


def _modulate(x, shift, scale):
    ms = jnp.mean(x * x, axis=-1, keepdims=True)
    return x * lax.rsqrt(ms + RMS_EPS) * (1.0 + scale) + shift


class Layout:
    def __init__(self, n_prompt, l_prompt, n_sample, l_sample):
        self.B, self.L, self.NS, self.LS = n_prompt, l_prompt, n_sample, l_sample
        self.TP = n_prompt * l_prompt
        self.T = self.TP + n_sample * l_sample

    def tile(self, want):
        t = math.gcd(math.gcd(self.TP, self.LS), want)
        assert t % 8 == 0
        return t

    def group(self, row):
        return jnp.where(row < self.TP, 0, 1 + (row - self.TP) // self.LS)


def _mod_spec(lay, tm, chunk, ngrid):
    def imap(*ids):
        return (lay.group(ids[0] * tm) * MOD_CHUNKS + chunk, 0, 0)
    del ngrid
    return pl.BlockSpec((None, 1, D), imap)


def _mods_kernel(c_ref, w_ref, b_ref, o_ref):
    o_ref[...] = jnp.dot(_silu(c_ref[...]), w_ref[...], precision=HIGHEST,
                         preferred_element_type=F32) + b_ref[...]


def _mods(cond, w, b):
    n = MOD_CHUNKS * D
    tn = 1536
    out = pl.pallas_call(
        _mods_kernel,
        out_shape=jax.ShapeDtypeStruct((8, n), F32),
        grid=(n // tn,),
        in_specs=[pl.BlockSpec((8, D), lambda j: (0, 0)),
                  pl.BlockSpec((D, tn), lambda j: (0, j)),
                  pl.BlockSpec((1, tn), lambda j: (0, j))],
        out_specs=pl.BlockSpec((8, tn), lambda j: (0, j)),
        compiler_params=_cp("arbitrary"),
        name="mods",
    )(cond, w, b.reshape(1, n))
    return out.reshape(8 * MOD_CHUNKS, 1, D)


def _modulate_kernel(x_ref, sh_ref, sc_ref, o_ref):
    o_ref[...] = _modulate(x_ref[...], sh_ref[...], sc_ref[...]).astype(o_ref.dtype)


def _modulate_call(lay, x, mods, c_shift, c_scale, dtype):
    tm = lay.tile(512)
    return pl.pallas_call(
        _modulate_kernel,
        out_shape=jax.ShapeDtypeStruct((lay.T, D), dtype),
        grid=(lay.T // tm,),
        in_specs=[pl.BlockSpec((tm, D), lambda i: (i, 0)),
                  _mod_spec(lay, tm, c_shift, 1), _mod_spec(lay, tm, c_scale, 1)],
        out_specs=pl.BlockSpec((tm, D), lambda i: (i, 0)),
        compiler_params=_cp("arbitrary"),
        name="modulate",
    )(x, mods, mods)


def _proj_kernel(x_ref, sh_ref, sc_ref, w_ref, b_ref, o_ref, u_sc):
    @pl.when(pl.program_id(1) == 0)
    def _():
        u_sc[...] = _modulate(x_ref[...], sh_ref[...], sc_ref[...]).astype(BF16)

    acc = jnp.dot(u_sc[...], w_ref[...], preferred_element_type=F32) + b_ref[...]
    o_ref[...] = acc.astype(o_ref.dtype)


def _proj(lay, x, mods, c_shift, c_scale, w, b, tn, out_dtype=BF16):
    tm = lay.tile(1024)
    n = w.shape[1]
    assert n % tn == 0
    return pl.pallas_call(
        _proj_kernel,
        out_shape=jax.ShapeDtypeStruct((lay.T, n), out_dtype),
        grid=(lay.T // tm, n // tn),
        in_specs=[pl.BlockSpec((tm, D), lambda i, j: (i, 0)),
                  _mod_spec(lay, tm, c_shift, 2), _mod_spec(lay, tm, c_scale, 2),
                  pl.BlockSpec((D, tn), lambda i, j: (0, j)),
                  pl.BlockSpec((1, tn), lambda i, j: (0, j))],
        out_specs=pl.BlockSpec((tm, tn), lambda i, j: (i, j)),
        scratch_shapes=[pltpu.VMEM((tm, D), BF16)],
        compiler_params=_cp("arbitrary", "arbitrary"),
        name="proj",
    )(x, mods, mods, w, b)


def _ffn_kernel(x_ref, sh_ref, sc_ref, g_ref, wa_ref, wb_ref, wo_ref, o_ref):
    x = x_ref[...]
    u = _modulate(x, sh_ref[...], sc_ref[...]).astype(BF16)
    a = jnp.dot(u, wa_ref[...], preferred_element_type=F32)
    b = jnp.dot(u, wb_ref[...], preferred_element_type=F32)
    h = (_silu(a) * b).astype(BF16)
    o_ref[...] = x + g_ref[...] * jnp.dot(h, wo_ref[...], preferred_element_type=F32)


def _ffn(lay, x, mods, w_in, w_out):
    tm = lay.tile(512)
    once = dict(pipeline_mode=pl.Buffered(1))
    return pl.pallas_call(
        _ffn_kernel,
        out_shape=jax.ShapeDtypeStruct((lay.T, D), F32),
        grid=(lay.T // tm,),
        in_specs=[pl.BlockSpec((tm, D), lambda i: (i, 0)),
                  _mod_spec(lay, tm, 3, 1), _mod_spec(lay, tm, 4, 1), _mod_spec(lay, tm, 5, 1),
                  pl.BlockSpec((D, FFN_DIM), lambda i: (0, 0), **once),
                  pl.BlockSpec((D, FFN_DIM), lambda i: (0, 1), **once),
                  pl.BlockSpec((FFN_DIM, D), lambda i: (0, 0), **once)],
        out_specs=pl.BlockSpec((tm, D), lambda i: (i, 0)),
        compiler_params=_cp("arbitrary"),
        name="ffn",
    )(x, mods, mods, mods, w_in, w_in, w_out)


def _s5_tables(a_re, a_im, log_dt, b_re, b_im, c_re, c_im, d_skip):
    T, G, P, Q = S5_T, S5_G, S5_P, S5_Q
    a = lax.complex(a_re, a_im)
    adt = a * jnp.exp(log_dt)[..., None]
    lam = jnp.exp(adt)
    bb = ((lam - 1.0) / a)[..., None] * lax.complex(b_re, b_im)
    cm = lax.complex(c_re, c_im)
    steps = jnp.arange(T + 1, dtype=F32)
    pw = jnp.exp(steps[None, :, None, None] * adt[:, None])
    kern = jnp.real(jnp.einsum('dgqp,djgp,dgpr->djgqr', cm, pw[:, :T], bb))
    lag = jnp.arange(T)[:, None, None]
    s_i = jnp.arange(T)[None, :, None]
    t_i = jnp.arange(T)[None, None, :]
    place = jnp.stack([t_i - s_i == lag, s_i - t_i == lag]).astype(F32)
    m = jnp.einsum('djst,djgqr->gsrtq', place, kern, precision=HIGHEST)
    eye = (jnp.eye(T)[:, None, :, None] * jnp.eye(Q)[None, :, None, :])
    m = m + eye[None] * d_skip.reshape(G, 1, 1, 1, Q)
    m = m.reshape(G, T * Q, T * Q)
    e_f = pw[0][T - 1 - jnp.arange(T)]
    e_b = pw[1][jnp.arange(T)]
    n_f = e_f[..., None] * bb[0][None]
    n_b = e_b[..., None] * bb[1][None]
    n_c = jnp.concatenate([n_f, n_b], axis=2)
    n_c = jnp.transpose(n_c, (1, 0, 3, 2)).reshape(G, T * Q, 2 * P)
    lam_t = jnp.concatenate([pw[0][T], pw[1][T]], axis=-1)
    w_f = cm[0][:, None] * jnp.transpose(pw[0][1:T + 1], (1, 0, 2))[:, :, None, :]
    w_b = cm[1][:, None] * jnp.transpose(pw[1][T - jnp.arange(T)], (1, 0, 2))[:, :, None, :]
    w_f = jnp.transpose(w_f, (0, 3, 1, 2)).reshape(G, P, T * Q)
    w_b = jnp.transpose(w_b, (0, 3, 1, 2)).reshape(G, P, T * Q)
    z = jnp.zeros_like(jnp.real(w_f))
    c_mats = dict(c_f_re=jnp.concatenate([jnp.real(w_f), z], axis=1),
                  c_f_im=jnp.concatenate([-jnp.imag(w_f), z], axis=1),
                  c_b_re=jnp.concatenate([z, jnp.real(w_b)], axis=1),
                  c_b_im=jnp.concatenate([z, -jnp.imag(w_b)], axis=1))
    return dict(m=m.astype(BF16), n_re=jnp.real(n_c).astype(BF16), n_im=jnp.imag(n_c).astype(BF16),
                l_re=jnp.real(lam_t), l_im=jnp.imag(lam_t), **{k: v.astype(BF16) for k, v in c_mats.items()})


def _s5_in_kernel(u_ref, m_ref, nre_ref, nim_ref, yi_ref, sre_ref, sim_ref):
    u = u_ref[...]
    yi_ref[...] = jnp.dot(u, m_ref[...], preferred_element_type=F32)
    sre_ref[...] = jnp.dot(u, nre_ref[...], preferred_element_type=F32).reshape(sre_ref.shape)
    sim_ref[...] = jnp.dot(u, nim_ref[...], preferred_element_type=F32).reshape(sim_ref.shape)


def _s5_in(ug, tabs):
    G, R, W = ug.shape
    P2 = 2 * S5_P
    RB = S5_SCAN_ROWS
    assert R % RB == 0
    gspec = lambda n: pl.BlockSpec((None, W, n), lambda g: (g, 0, 0))
    rspec = pl.BlockSpec((None, R, W), lambda g: (g, 0, 0))
    sspec = pl.BlockSpec((R // RB, RB, P2), lambda g: (0, g, 0))
    sshape = jax.ShapeDtypeStruct((R // RB, G * RB, P2), F32)
    return pl.pallas_call(
        _s5_in_kernel,
        out_shape=(jax.ShapeDtypeStruct((G, R, W), F32), sshape, sshape),
        grid=(G,),
        in_specs=[rspec, gspec(W), gspec(P2), gspec(P2)],
        out_specs=(rspec, sspec, sspec),
        compiler_params=_cp("arbitrary"),
        name="s5_in",
    )(ug, tabs['m'], tabs['n_re'], tabs['n_im'])


def _s5_scan_kernel(nsb, ncb, nblk, sref_ref, simf_ref, sreb_ref, simb_ref, lre_ref, lim_ref,
                    h0re_ref, h0im_ref, *rest):
    hfre_ref, hfim_ref, hbre_ref, hbim_ref, fre_ref, fim_ref, cre_sc, cim_sc = rest[4:]
    P = S5_P
    rows = sref_ref.shape[0] // ncb
    j = pl.program_id(1)
    fwd = lax.broadcasted_iota(jnp.int32, (1, 2 * P), 1) < P
    lre = lre_ref[...]
    lim = lim_ref[...]

    @pl.when(j == 0)
    def _():
        cre_sc[...] = h0re_ref[...]
        cim_sc[...] = h0im_ref[...]

    def at(k):
        return pl.ds(k, rows, stride=ncb)

    def body(k, carry):
        hre, him = carry
        kb = ncb - 1 - k
        hfre_ref[at(k), :] = hre
        hfim_ref[at(k), :] = him
        hbre_ref[at(kb), :] = hre
        hbim_ref[at(kb), :] = him
        sre = jnp.where(fwd, sref_ref[at(k), :], sreb_ref[at(kb), :])
        sim = jnp.where(fwd, simf_ref[at(k), :], simb_ref[at(kb), :])
        return (lre * hre - lim * him + sre, lre * him + lim * hre + sim)

    hre, him = lax.fori_loop(0, ncb, body, (cre_sc[...], cim_sc[...]), unroll=4)
    cre_sc[...] = hre
    cim_sc[...] = him

    @pl.when(j == nblk - 1)
    def _():
        fre_ref[...] = hre
        fim_ref[...] = him


def _s5_scan(sre, sim, tabs, h0re, h0im, hprev, row0, nseq, nc, nsb, nblk):
    _, grb, P2 = sre.shape
    rb = S5_SCAN_ROWS
    G = grb // rb
    assert nblk == 1 or nsb == 1
    ncb = nc // nblk
    assert nsb * ncb == rb and row0 % rb == 0 and nseq % nsb == 0 and nc % nblk == 0
    b0 = row0 // rb
    fspec = pl.BlockSpec((None, grb, P2), lambda i, j: (b0 + i * nblk + j, 0, 0))
    bspec = pl.BlockSpec((None, grb, P2), lambda i, j: (b0 + i * nblk + nblk - 1 - j, 0, 0))
    lspec = pl.BlockSpec((G * nsb, P2), lambda i, j: (0, 0))
    qspec = pl.BlockSpec((None, G * nsb, P2), lambda i, j: (i, 0, 0))
    anyspec = pl.BlockSpec(memory_space=pl.ANY)
    fin = jax.ShapeDtypeStruct((nseq // nsb, G * nsb, P2), F32)
    rep = lambda a: jnp.repeat(a, nsb, axis=0)
    flat = lambda a: a.reshape(nseq // nsb, G * nsb, P2)
    outs = pl.pallas_call(
        functools.partial(_s5_scan_kernel, nsb, ncb, nblk),
        out_shape=tuple(jax.ShapeDtypeStruct(h.shape, h.dtype) for h in hprev) + (fin, fin),
        grid=(nseq // nsb, nblk),
        in_specs=[fspec, fspec, bspec, bspec, lspec, lspec, qspec, qspec] + [anyspec] * 4,
        out_specs=(fspec, fspec, bspec, bspec, qspec, qspec),
        scratch_shapes=[pltpu.VMEM((G * nsb, P2), F32), pltpu.VMEM((G * nsb, P2), F32)],
        input_output_aliases={8: 0, 9: 1, 10: 2, 11: 3},
        compiler_params=_cp("arbitrary", "arbitrary"),
        name="s5_scan",
    )(sre, sim, sre, sim, rep(tabs['l_re']), rep(tabs['l_im']), flat(h0re), flat(h0im), *hprev)
    return outs[:4], outs[4].reshape(nseq // nsb, G, nsb, P2), outs[5].reshape(nseq // nsb, G, nsb, P2)


def _s5_out_kernel(yi_ref, hfre_ref, hfim_ref, hbre_ref, hbim_ref, cfre_ref, cfim_ref, cbre_ref, cbim_ref,
                   y_ref):
    y = yi_ref[...]
    for h_ref, c_ref in ((hfre_ref, cfre_ref), (hfim_ref, cfim_ref), (hbre_ref, cbre_ref), (hbim_ref, cbim_ref)):
        h = h_ref[...].reshape(y.shape[0], h_ref.shape[-1])
        y += jnp.dot(h.astype(BF16), c_ref[...], preferred_element_type=F32)
    y_ref[...] = y.astype(y_ref.dtype)


def _s5_out(yi, hprev, tabs):
    G, R, W = yi.shape
    P2 = 2 * S5_P
    RB = S5_SCAN_ROWS
    gspec = pl.BlockSpec((None, P2, W), lambda g: (g, 0, 0))
    hspec = pl.BlockSpec((R // RB, RB, P2), lambda g: (0, g, 0))
    rspec = pl.BlockSpec((None, R, W), lambda g: (g, 0, 0))
    return pl.pallas_call(
        _s5_out_kernel,
        out_shape=jax.ShapeDtypeStruct((G, R, W), F32),
        grid=(G,),
        in_specs=[rspec] + [hspec] * 4 + [gspec] * 4,
        out_specs=rspec,
        compiler_params=_cp("arbitrary"),
        name="s5_out",
    )(yi, *hprev, tabs['c_f_re'], tabs['c_f_im'], tabs['c_b_re'], tabs['c_b_im'])


LANES = 128
S5_GB = LANES // S5_Q


def _block_transpose(sets):
    blk = lax.broadcasted_iota(jnp.int32, sets[0][0].shape, 1) // S5_Q
    sets = [list(regs) for regs in sets]
    d = S5_GB // 2
    while d:
        keep = (blk & d) == 0
        for regs in sets:
            for i in range(S5_GB):
                if i & d:
                    continue
                a, b = regs[i], regs[i + d]
                regs[i] = jnp.where(keep, a, pltpu.roll(b, d * S5_Q, 1))
                regs[i + d] = jnp.where(keep, pltpu.roll(a, LANES - d * S5_Q, 1), b)
        d //= 2
    return sets


def _s5_pre_kernel(tm, x_ref, sh_ref, sc_ref, ug_ref, u_sc):
    u = _modulate(x_ref[...], sh_ref[...], sc_ref[...])
    for j in range(D // LANES):
        u_sc[j] = u[:, j * LANES:(j + 1) * LANES]
    rows16 = 16
    nh = S5_T // S5_GB
    for c in range(tm // (S5_T * rows16)):
        base = c * S5_T * rows16
        for j in range(D // LANES):
            sets = [[u_sc[j, pl.ds(base + h * S5_GB + s, rows16, stride=S5_T), :] for s in range(S5_GB)]
                    for h in range(nh)]
            for h, regs in enumerate(_block_transpose(sets)):
                for gl, t in enumerate(regs):
                    ug_ref[j * S5_GB + gl, c * rows16:(c + 1) * rows16, h * LANES:(h + 1) * LANES] = t.astype(BF16)


def _s5_pre(lay, x, mods):
    tm = lay.tile(512)
    assert tm % (S5_T * 16) == 0
    return pl.pallas_call(
        functools.partial(_s5_pre_kernel, tm),
        out_shape=jax.ShapeDtypeStruct((S5_G, lay.T // S5_T, S5_T * S5_Q), BF16),
        grid=(lay.T // tm,),
        in_specs=[pl.BlockSpec((tm, D), lambda i: (i, 0)), _mod_spec(lay, tm, 0, 1), _mod_spec(lay, tm, 1, 1)],
        out_specs=pl.BlockSpec((S5_G, tm // S5_T, S5_T * S5_Q), lambda i: (0, i, 0)),
        scratch_shapes=[pltpu.VMEM((D // LANES, tm, LANES), F32)],
        compiler_params=_cp("arbitrary"),
        name="s5_pre",
    )(x, mods, mods)


def _s5_glu_kernel(tm, x_ref, yg_ref, g_ref, wv_ref, wg_ref, o_ref, a_sc, y_sc):
    @pl.when(pl.program_id(1) == 0)
    def _():
        def sub_tile(c, carry):
            crow = pl.ds(pl.multiple_of(c * 8, 8), 8)
            base = c * (S5_T * 8)
            nh = S5_T // S5_GB
            for j2 in range(0, D // LANES, 2):
                keys = [(j, h) for j in (j2, j2 + 1) for h in range(nh)]
                sets = [[yg_ref[j * S5_GB + gl, crow, h * LANES:(h + 1) * LANES] for gl in range(S5_GB)]
                        for j, h in keys]
                for (j, h), regs in zip(keys, _block_transpose(sets)):
                    for s, t in enumerate(regs):
                        y_sc[j, pl.ds(base + h * S5_GB + s, 8, stride=S5_T), :] = t
            return carry

        lax.fori_loop(0, tm // (S5_T * 8), sub_tile, 0)
        for j in range(D // LANES):
            a_sc[:, j * LANES:(j + 1) * LANES] = jax.nn.gelu(y_sc[j]).astype(BF16)

    a = a_sc[...]
    val = jnp.dot(a, wv_ref[...], preferred_element_type=F32)
    gate = jnp.dot(a, wg_ref[...], preferred_element_type=F32)
    o_ref[...] = x_ref[...] + g_ref[...] * (val * jax.nn.sigmoid(gate))


def _s5_glu(lay, x, yg, mods, glu_w):
    tm = lay.tile(1024)
    assert tm % (S5_T * 8) == 0
    tn = 512
    nn = D // tn
    return pl.pallas_call(
        functools.partial(_s5_glu_kernel, tm),
        out_shape=jax.ShapeDtypeStruct((lay.T, D), F32),
        grid=(lay.T // tm, nn),
        in_specs=[pl.BlockSpec((tm, tn), lambda i, j: (i, j)),
                  pl.BlockSpec((S5_G, tm // S5_T, S5_T * S5_Q), lambda i, j: (0, i, 0)),
                  pl.BlockSpec((None, 1, tn), lambda i, j: (lay.group(i * tm) * MOD_CHUNKS + 2, 0, j)),
                  pl.BlockSpec((D, tn), lambda i, j: (0, j)),
                  pl.BlockSpec((D, tn), lambda i, j: (0, nn + j))],
        out_specs=pl.BlockSpec((tm, tn), lambda i, j: (i, j)),
        scratch_shapes=[pltpu.VMEM((tm, D), BF16), pltpu.VMEM((D // LANES, tm, LANES), F32)],
        compiler_params=_cp("arbitrary", "arbitrary"),
        name="s5_glu",
    )(x, yg, mods, glu_w, glu_w)


def _s5_layer(lay, x, mods, p, h0_re, h0_im):
    T, G, P, Q = S5_T, S5_G, S5_P, S5_Q
    tabs = _s5_tables(p['a_re'], p['a_im'], p['log_dt'], p['b_re'], p['b_im'], p['c_re'], p['c_im'], p['d'])
    R = lay.T // T
    ug = _s5_pre(lay, x, mods)
    yi, sre, sim = _s5_in(ug, tabs)
    hprev = tuple(jnp.zeros(sre.shape, F32) for _ in range(4))
    ncp, ncs = lay.L // T, lay.LS // T
    nsb = S5_SCAN_ROWS // ncp
    zero = jnp.zeros((lay.B // nsb, G, nsb, 2 * P), F32)
    hprev, fre, fim = _s5_scan(sre, sim, tabs, zero, zero, hprev, 0, lay.B, ncp, nsb, 1)
    to_lanes = lambda s: jnp.transpose(s, (0, 2, 1, 3)).reshape(lay.NS, G, 1, 2 * P)
    hprev, _, _ = _s5_scan(sre, sim, tabs, to_lanes(h0_re), to_lanes(h0_im), hprev,
                           lay.TP // T, lay.NS, ncs, 1, max(1, ncs // S5_SCAN_ROWS))
    yg = _s5_out(yi, hprev, tabs)
    x = _s5_glu(lay, x, yg, mods, p['glu_w'])
    from_lanes = lambda s: jnp.transpose(s, (0, 2, 1, 3)).reshape(lay.B, G, 2, P).transpose(0, 2, 1, 3)
    return x, from_lanes(fre), from_lanes(fim)


def _hyena_filters(L, p):
    mm = functools.partial(jnp.matmul, precision=HIGHEST)
    f = jnp.linspace(1e-4, HY_BANDS - 1, HY_BANDS, dtype=F32)[None, :]
    max_decay = math.log(HY_TARGET) / HY_FAST_PCT
    min_decay = math.log(HY_TARGET) / HY_SLOW_PCT
    deltas = jnp.abs(jnp.linspace(min_decay, max_decay, D, dtype=F32))
    w3 = p['f_w3'].reshape(-1, 2, 2, D)

    def side(pos, s):
        t = (pos.astype(F32) / (L - 1))[:, None]
        w = 2.0 * math.pi * pos.astype(F32)[:, None] / L
        feats = jnp.concatenate([t, jnp.cos(f * w), -jnp.sin(f * w)], axis=-1)
        z = jnp.sin(p['f_freq'][0] * (mm(feats, p['f_w1']) + p['f_b1']))
        z = jnp.sin(p['f_freq'][1] * (mm(z, p['f_w2']) + p['f_b2']))
        win = jnp.exp(-t * deltas)
        return jnp.stack([mm(z, w3[:, o, s]) * win for o in range(2)])

    j = jnp.arange(L, dtype=jnp.int32)
    k_lo = side(j, 0)
    k_hi = side((L - j) % L, 1) * (j > 0).astype(F32)[None, :, None]
    norm = jnp.sum(jnp.abs(k_lo), axis=1, keepdims=True) + jnp.sum(jnp.abs(k_hi), axis=1, keepdims=True)
    k_lo, k_hi = k_lo / norm, k_hi / norm
    alt = (1.0 - 2.0 * (j % 2).astype(F32))[None, :, None]
    k_ny = jnp.sum(alt * (k_lo + k_hi), axis=1) / (2 * L)
    return k_lo, k_hi, k_ny


def _dft_tables(L):
    r = math.isqrt(L)
    assert r * r == L
    t = jnp.arange(L, dtype=jnp.int32)[None, :]
    a = jnp.arange(r, dtype=jnp.int32)[:, None]

    def unit(idx):
        ang = (idx % (2 * L)).astype(F32) * (math.pi / L)
        return jnp.cos(ang), jnp.sin(ang)

    c1, s1 = unit(a * r * t)
    c2, s2 = unit(a * t)
    c1, s1, c2, s2 = c1[:, None], s1[:, None], c2[None], s2[None]
    cos = (c1 * c2 - s1 * s2).reshape(L, L)
    sin = (s1 * c2 + c1 * s2).reshape(L, L)
    return cos.astype(BF16), sin.astype(BF16)


def _hy_spec_kernel(L, tr, c_ref, s_ref, klo_ref, khi_ref, p_ref, q_ref):
    r = pl.program_id(2)
    f = r * tr + lax.broadcasted_iota(jnp.int32, (tr, 1), 0)
    sgn = (1 - 2 * (f % 2)).astype(F32)
    scale = jnp.where(f == 0, 1.0, 2.0) * (1.0 / (2 * L))
    c, s = c_ref[...], s_ref[...]
    lo, hi = klo_ref[...], khi_ref[...]
    dot = functools.partial(jnp.dot, preferred_element_type=F32)
    p_ref[...] = scale * (dot(c, lo) + sgn * dot(c, hi))
    q_ref[...] = scale * (dot(s, lo) + sgn * dot(s, hi))


def _hy_spectrum(L, cos, sin, k_lo, k_hi):
    tr = min(L, 512)
    tc = 512
    kspec = pl.BlockSpec((None, L, tc), lambda o, j, r: (o, 0, j))
    tspec = pl.BlockSpec((tr, L), lambda o, j, r: (r, 0))
    ospec = pl.BlockSpec((None, tr, tc), lambda o, j, r: (o, r, j))
    return pl.pallas_call(
        functools.partial(_hy_spec_kernel, L, tr),
        out_shape=(jax.ShapeDtypeStruct((2, L, D), F32), jax.ShapeDtypeStruct((2, L, D), F32)),
        grid=(2, D // tc, L // tr),
        in_specs=[tspec, tspec, kspec, kspec],
        out_specs=(ospec, ospec),
        compiler_params=_cp("arbitrary", "arbitrary", "arbitrary"),
        name="hy_spectrum",
    )(cos, sin, k_lo.astype(BF16), k_hi.astype(BF16))


def _hy_core_kernel(L, tr, ngrp, tc, x1_ref, x2_ref, v_ref, sw1_ref, sw2_ref, swv_ref, sb1_ref, sb2_ref,
                    sbv_ref, c_ref, s_ref, p_ref, q_ref, kny_ref, skip_ref, *rest):
    o_ref, z0_sc, z1_sc, x2_sc, a_sc, b_sc, ny0_sc, ny1_sc = rest[-8:]
    fused = tr == L
    ph = pl.program_id(2)
    r = pl.program_id(3)
    W = ngrp * tc

    def phase(k):
        return (lambda f: f()) if fused else pl.when(ph == k)
    tcv = min(L, 512)
    halo = 16

    def alt_sign(start, n):
        t = start + lax.broadcasted_iota(jnp.int32, (n, 1), 0)
        return (1 - 2 * (t % 2)).astype(F32)

    def conv3(src_ref, g, a, w_ref, b_ref):
        x = src_ref[g, pl.ds(a, tcv), :].astype(F32)
        row = lax.broadcasted_iota(jnp.int32, (tcv, 1), 0)
        up_at = pl.multiple_of(jnp.maximum(a - halo, 0), halo)
        dn_at = pl.multiple_of(jnp.minimum(a + tcv, L - halo), halo)
        up = src_ref[g, pl.ds(up_at, halo), :][halo - 1:halo, :].astype(F32)
        dn = src_ref[g, pl.ds(dn_at, halo), :][0:1, :].astype(F32)
        up = jnp.where(a > 0, up, 0.0)
        dn = jnp.where(a + tcv < L, dn, 0.0)
        prev = jnp.where(row == 0, up, pltpu.roll(x, 1, 0))
        nxt = jnp.where(row == tcv - 1, dn, pltpu.roll(x, tcv - 1, 0))
        return prev * w_ref[0:1, :] + x * w_ref[1:2, :] + nxt * w_ref[2:3, :] + b_ref[...]

    @(phase(0) if fused else pl.when(jnp.logical_and(ph == 0, r == 0)))
    def _():
        ny0_sc[...] = jnp.zeros_like(ny0_sc)

        def conv_tile(ti, carry):
            a = pl.multiple_of(ti * tcv, tcv)
            rows_a = pl.ds(a, tcv)
            for g in range(ngrp):
                cols = slice(g * tc, (g + 1) * tc)
                z1_sc[rows_a, cols] = conv3(x1_ref, g, a, sw1_ref, sb1_ref).astype(BF16)
                x2_sc[rows_a, cols] = conv3(x2_ref, g, a, sw2_ref, sb2_ref).astype(BF16)
                z0_sc[rows_a, cols] = conv3(v_ref, g, a, swv_ref, sbv_ref).astype(BF16)
            ny0_sc[...] += jnp.sum(alt_sign(a, tcv) * z0_sc[rows_a, :].astype(F32), axis=0, keepdims=True)
            return carry

        lax.fori_loop(0, L // tcv, conv_tile, 0)

    start = pl.multiple_of(r * tr, tr)
    rows = pl.ds(start, tr)
    dot = functools.partial(jnp.dot, preferred_element_type=F32)

    def forward(order, z_sc):
        z = z_sc[...]
        zre = dot(c_ref[...], z)
        zim = dot(s_ref[...], z)
        pw = jnp.concatenate([p_ref[order] if fused else p_ref[...]] * ngrp, axis=1)
        qw = jnp.concatenate([q_ref[order] if fused else q_ref[...]] * ngrp, axis=1)
        a_sc[rows, :] = (zre * pw - zim * qw).astype(BF16)
        b_sc[rows, :] = (zim * pw + zre * qw).astype(BF16)

    def inverse(order, z_sc, ny_sc):
        y = dot(c_ref[...], a_sc[...]) + dot(s_ref[...], b_sc[...])
        kny = jnp.concatenate([kny_ref[order:order + 1, :]] * ngrp, axis=1)
        skip = jnp.concatenate([skip_ref[order:order + 1, :]] * ngrp, axis=1)
        return y + alt_sign(start, tr) * (ny_sc[...] * kny) + skip * z_sc[rows, :].astype(F32)

    @phase(0)
    def _():
        forward(0, z0_sc)

    @phase(1)
    def _():
        z1 = (z1_sc[rows, :].astype(F32) * inverse(0, z0_sc, ny0_sc)).astype(BF16)
        z1_sc[rows, :] = z1

        @pl.when(r == 0)
        def _():
            ny1_sc[...] = jnp.zeros_like(ny1_sc)

        ny1_sc[...] += jnp.sum(alt_sign(start, tr) * z1.astype(F32), axis=0, keepdims=True)

    @phase(2)
    def _():
        forward(1, z1_sc)

    @phase(3)
    def _():
        out = x2_sc[rows, :].astype(F32) * inverse(1, z1_sc, ny1_sc)
        for g in range(ngrp):
            o_ref[g, rows, :] = out[:, g * tc:(g + 1) * tc].astype(o_ref.dtype)


def _hy_core(proj, short_w, short_b, skip, cos, sin, pq, k_ny, o_prev, row0, nseq, L, ngrp, tc):
    T = proj.shape[0]
    tr = min(L, 256)
    nrt = L // tr
    assert row0 % (L * ngrp) == 0 and nseq % ngrp == 0 and T % L == 0
    sb0 = row0 // (L * ngrp)
    nct = D // tc
    p3 = proj.reshape(T // L, L, 3 * D)
    p_arr, q_arr = pq

    def xspec(part):
        mode = {} if nrt == 1 else dict(pipeline_mode=pl.Buffered(1))
        return pl.BlockSpec((ngrp, L, tc), lambda i, j, ph, r: (sb0 + i, 0, part * nct + j), **mode)

    def wspec(part, rows_):
        return pl.BlockSpec((rows_, tc), lambda i, j, ph, r: (0, part * nct + j))

    fused = nrt == 1
    if fused:
        pq_spec = pl.BlockSpec((2, tr, tc), lambda i, j, ph, r: (0, 0, j))
    else:
        pq_spec = pl.BlockSpec((None, tr, tc),
                               lambda i, j, ph, r: (ph // 2, jnp.where(ph % 2 == 0, r, nrt - 1), j))

    tspec = pl.BlockSpec((tr, L), lambda i, j, ph, r: (r, 0))
    in_specs = [xspec(0), xspec(1), xspec(2), wspec(0, 3), wspec(1, 3), wspec(2, 3),
                wspec(0, 1), wspec(1, 1), wspec(2, 1), tspec, tspec, pq_spec, pq_spec,
                pl.BlockSpec((2, tc), lambda i, j, ph, r: (0, j)),
                pl.BlockSpec((2, tc), lambda i, j, ph, r: (0, j))]
    sb = short_b.reshape(1, 3 * D)
    args = [p3, p3, p3, short_w, short_w, short_w, sb, sb, sb, cos, sin, p_arr, q_arr, k_ny, skip]
    aliases = {}
    if o_prev is not None:
        in_specs.append(pl.BlockSpec(memory_space=pl.ANY))
        args.append(o_prev.reshape(T // L, L, D))
        aliases = {len(args) - 1: 0}
    W = ngrp * tc
    out = pl.pallas_call(
        functools.partial(_hy_core_kernel, L, tr, ngrp, tc),
        out_shape=jax.ShapeDtypeStruct((T // L, L, D), BF16),
        grid=(nseq // ngrp, nct, 1 if fused else 4, nrt),
        in_specs=in_specs,
        out_specs=pl.BlockSpec((ngrp, L, tc), lambda i, j, ph, r: (sb0 + i, 0, j)),
        scratch_shapes=[pltpu.VMEM((L, W), BF16)] * 5 + [pltpu.VMEM((1, W), F32)] * 2,
        input_output_aliases=aliases,
        compiler_params=_cp("arbitrary", "arbitrary", "arbitrary", "arbitrary"),
        name="hy_core",
    )(*args)
    return out.reshape(T, D)


FOLD_BLK = 256


def _hyena_filters_folded(L, p):
    H = L // 2
    mm = functools.partial(jnp.matmul, precision=HIGHEST)
    f = jnp.linspace(1e-4, HY_BANDS - 1, HY_BANDS, dtype=F32)[None, :]
    max_decay = math.log(HY_TARGET) / HY_FAST_PCT
    min_decay = math.log(HY_TARGET) / HY_SLOW_PCT
    deltas = jnp.abs(jnp.linspace(min_decay, max_decay, D, dtype=F32))
    w3 = p['f_w3'].reshape(-1, 2, 2, D)

    def side(pos, s):
        t = (pos.astype(F32) / (L - 1))[:, None]
        w = 2.0 * math.pi * pos.astype(F32)[:, None] / L
        feats = jnp.concatenate([t, jnp.cos(f * w), -jnp.sin(f * w)], axis=-1)
        z = jnp.sin(p['f_freq'][0] * (mm(feats, p['f_w1']) + p['f_b1']))
        z = jnp.sin(p['f_freq'][1] * (mm(z, p['f_w2']) + p['f_b2']))
        win = jnp.exp(-t * deltas)
        return jnp.stack([mm(z, w3[:, o, s]) * win for o in range(2)])

    t = jnp.arange(H, dtype=jnp.int32)
    pos = jnp.concatenate([t, (L - t) % L, jnp.full((1,), H, jnp.int32)])
    live = (t > 0).astype(F32)[None, :, None]
    s0, s1 = side(pos, 0), side(pos, 1)
    klo_lo, klo_hr, klo_h = s0[:, :H], s0[:, H:2 * H] * live, s0[:, 2 * H]
    khi_lo, khi_hr, khi_h = s1[:, H:2 * H] * live, s1[:, :H] * live, s1[:, 2 * H]
    norm = sum(jnp.sum(jnp.abs(a), axis=1) for a in (klo_lo, klo_hr, khi_lo, khi_hr)) \
        + jnp.abs(klo_h) + jnp.abs(khi_h)
    alt = (1.0 - 2.0 * (t % 2).astype(F32))[None, :, None]
    alt_h = 1.0 - 2.0 * (H % 2)
    k_ny = (jnp.sum(alt * (klo_lo + klo_hr + khi_lo + khi_hr), axis=1) + alt_h * (klo_h + khi_h)) / norm / (2 * L)
    inv = (1.0 / norm)[:, None, :]
    p_lo, p_hr = (klo_lo + khi_lo) * inv, (klo_hr + khi_hr) * inv
    m_lo, m_hr = (klo_lo - khi_lo) * inv, (klo_hr - khi_hr) * inv
    xc = jnp.stack([p_lo + p_hr, m_lo - m_hr])
    xs = jnp.stack([p_lo - p_hr, m_lo + m_hr])
    xh = jnp.stack([(klo_h + khi_h) / norm, (klo_h - khi_h) / norm])
    return xc, xs, xh, k_ny


def _dft_tables_folded(L):
    H = L // 2
    ra = 1 << (int(math.log2(H)) // 2)
    rb = H // ra
    th = math.pi / L
    a = jnp.arange(ra, dtype=jnp.int32)[:, None]
    b = jnp.arange(rb, dtype=jnp.int32)[:, None]
    u = jnp.arange(H, dtype=jnp.int32)[None, :]

    def unit(idx):
        ang = (idx % (2 * L)).astype(F32) * th
        return jnp.cos(ang), jnp.sin(ang)

    def outer(e1, e2):
        (c1, s1), (c2, s2) = e1, e2
        c1, s1, c2, s2 = c1[:, None], s1[:, None], c2[None], s2[None]
        return (c1 * c2 - s1 * s2).reshape(H, H), (s1 * c2 + c1 * s2).reshape(H, H)

    tabs = []
    for c in (0, 1):
        tabs += list(outer(unit(2 * rb * a * u), unit((2 * b + c) * u)))
    tabs += list(outer(unit((2 * u + 1) * rb * a), unit((2 * u + 1) * b)))
    return jnp.stack(tabs).astype(BF16)


def _hy_spec_folded_kernel(L, tr, c_ref, s_ref, xc_ref, xs_ref, xh_ref, p_ref, q_ref):
    half = pl.program_id(2)
    r = pl.program_id(3)
    m = r * tr + lax.broadcasted_iota(jnp.int32, (tr, 1), 0)
    alt = (1 - 2 * (m % 2)).astype(F32)
    scale = jnp.where(jnp.logical_and(half == 0, m == 0), 1.0, 2.0) * (1.0 / (2 * L))
    mid = alt * xh_ref[...]
    kc = jnp.dot(c_ref[...], xc_ref[...], preferred_element_type=F32) + jnp.where(half == 0, mid, 0.0)
    ks = jnp.dot(s_ref[...], xs_ref[...], preferred_element_type=F32) + jnp.where(half == 1, mid, 0.0)
    p_ref[...] = scale * kc
    q_ref[...] = scale * ks


def _hy_spectrum_folded(L, tabs, xc, xs, xh):
    H = L // 2
    tr = min(H, 512)
    tc = 512
    nrt = H // tr
    xspec = pl.BlockSpec((None, None, H, tc), lambda o, j, hf, r: (hf, o, 0, j))
    hspec = pl.BlockSpec((None, None, 1, tc), lambda o, j, hf, r: (hf, o, 0, j))
    cspec = pl.BlockSpec((None, tr, H), lambda o, j, hf, r: (2 * hf, r, 0))
    sspec = pl.BlockSpec((None, tr, H), lambda o, j, hf, r: (2 * hf + 1, r, 0))
    ospec = pl.BlockSpec((None, tr, tc), lambda o, j, hf, r: (o, hf * nrt + r, j))
    return pl.pallas_call(
        functools.partial(_hy_spec_folded_kernel, L, tr),
        out_shape=(jax.ShapeDtypeStruct((2, L, D), F32), jax.ShapeDtypeStruct((2, L, D), F32)),
        grid=(2, D // tc, 2, nrt),
        in_specs=[cspec, sspec, xspec, xspec, hspec],
        out_specs=(ospec, ospec),
        compiler_params=_cp("arbitrary", "arbitrary", "arbitrary", "arbitrary"),
        name="hy_spectrum",
    )(tabs, tabs, xc.astype(BF16), xs.astype(BF16), xh[:, :, None, :])


def _hy_fold_kernel(L, tr, ngrp, tc, x1_ref, x2_ref, v_ref, sw1_ref, sw2_ref, swv_ref, sb1_ref, sb2_ref,
                    sbv_ref, ce_ref, se_ref, co_ref, so_ref, pe_ref, po_ref, qe_ref, qo_ref, kny_ref, skip_ref,
                    *rest):
    o_ref, z0_sc, z1_sc, x2_sc, a_sc, b_sc, mid_sc = rest[-7:]
    H = L // 2
    nrt = H // tr
    nb = H // FOLD_BLK
    W = ngrp * tc
    ph = pl.program_id(2)
    r = pl.program_id(3)
    tcv = min(L, 512)
    halo = 16
    dot = functools.partial(jnp.dot, preferred_element_type=F32)
    ZH, X1H, X2H, YSP, NY, NYACC = range(6)
    alt_h = 1.0 - 2.0 * (H % 2)

    def alt_sign(start, n):
        t = start + lax.broadcasted_iota(jnp.int32, (n, 1), 0)
        return (1 - 2 * (t % 2)).astype(F32)

    def conv3(src_ref, g, a, w_ref, b_ref):
        x = src_ref[g, pl.ds(a, tcv), :].astype(F32)
        row = lax.broadcasted_iota(jnp.int32, (tcv, 1), 0)
        up_at = pl.multiple_of(jnp.maximum(a - halo, 0), halo)
        dn_at = pl.multiple_of(jnp.minimum(a + tcv, L - halo), halo)
        up = src_ref[g, pl.ds(up_at, halo), :][halo - 1:halo, :].astype(F32)
        dn = src_ref[g, pl.ds(dn_at, halo), :][0:1, :].astype(F32)
        up = jnp.where(a > 0, up, 0.0)
        dn = jnp.where(a + tcv < L, dn, 0.0)
        prev = jnp.where(row == 0, up, pltpu.roll(x, 1, 0))
        nxt = jnp.where(row == tcv - 1, dn, pltpu.roll(x, tcv - 1, 0))
        return prev * w_ref[0:1, :] + x * w_ref[1:2, :] + nxt * w_ref[2:3, :] + b_ref[...]

    def flip_mats():
        u = lax.broadcasted_iota(jnp.int32, (FOLD_BLK, FOLD_BLK), 0)
        v = lax.broadcasted_iota(jnp.int32, (FOLD_BLK, FOLD_BLK), 1)
        jshift = jnp.where(jnp.logical_and(u >= 1, v == FOLD_BLK - u), 1.0, 0.0).astype(BF16)
        e0 = jnp.where(jnp.logical_and(u == 0, v == 0), 1.0, 0.0).astype(BF16)
        return jshift, e0

    def flipped_block(src, b, jshift, e0):
        blk = lambda k: src[H + k * FOLD_BLK:H + (k + 1) * FOLD_BLK, :]
        out = dot(jshift, blk(nb - 1 - b))
        if b >= 1:
            out = out + dot(e0, blk(nb - b))
        return out

    def fold_in_place(sc, stage, off, jshift, e0):
        for b in range(nb):
            stage[off + b * FOLD_BLK:off + (b + 1) * FOLD_BLK, :] = flipped_block(sc, b, jshift, e0).astype(BF16)

        def copy_back(b, carry):
            at = pl.multiple_of(b * FOLD_BLK, FOLD_BLK)
            sc[pl.ds(H + at, FOLD_BLK), :] = stage[pl.ds(off + at, FOLD_BLK), :]
            return carry

        lax.fori_loop(0, nb, copy_back, 0)

    def to_operands(sc):
        def tile(i, acc):
            a = pl.multiple_of(i * FOLD_BLK, FOLD_BLK)
            ra, rb = pl.ds(a, FOLD_BLK), pl.ds(pl.multiple_of(H + a, FOLD_BLK), FOLD_BLK)
            lo = sc[ra, :].astype(F32)
            hr = sc[rb, :].astype(F32)
            zs = (lo + hr).astype(BF16)
            sc[ra, :] = zs
            sc[rb, :] = (lo - hr).astype(BF16)
            return acc + jnp.sum(alt_sign(a, FOLD_BLK) * zs.astype(F32), axis=0, keepdims=True)

        return lax.fori_loop(0, nb, tile, jnp.zeros((1, W), F32))

    def row_of(sc, at):
        return sc[at:at + halo, :][0:1, :].astype(F32)

    @pl.when(jnp.logical_and(ph == 0, r == 0))
    def _():
        def conv_tile(ti, carry):
            a = pl.multiple_of(ti * tcv, tcv)
            rows_a = pl.ds(a, tcv)
            for g in range(ngrp):
                cols = slice(g * tc, (g + 1) * tc)
                z1_sc[rows_a, cols] = conv3(x1_ref, g, a, sw1_ref, sb1_ref).astype(BF16)
                x2_sc[rows_a, cols] = conv3(x2_ref, g, a, sw2_ref, sb2_ref).astype(BF16)
                z0_sc[rows_a, cols] = conv3(v_ref, g, a, swv_ref, sbv_ref).astype(BF16)
            return carry

        lax.fori_loop(0, L // tcv, conv_tile, 0)
        mid_sc[ZH:ZH + 1, :] = row_of(z0_sc, H)
        mid_sc[X1H:X1H + 1, :] = row_of(z1_sc, H)
        mid_sc[X2H:X2H + 1, :] = row_of(x2_sc, H)
        jshift, e0 = flip_mats()
        for sc, stage, off in ((z0_sc, a_sc, 0), (z1_sc, a_sc, H), (x2_sc, b_sc, 0)):
            fold_in_place(sc, stage, off, jshift, e0)
        mid_sc[NY:NY + 1, :] = to_operands(z0_sc) + alt_h * mid_sc[ZH:ZH + 1, :]
        mid_sc[YSP:YSP + 1, :] = jnp.zeros((1, W), F32)
        mid_sc[NYACC:NYACC + 1, :] = jnp.zeros((1, W), F32)

    start = pl.multiple_of(r * tr, tr)
    rows = pl.ds(start, tr)
    rows_hi = pl.ds(pl.multiple_of(H + r * tr, tr), tr)
    rep = lambda ref: jnp.concatenate([ref[...]] * ngrp, axis=1)
    rep_row = lambda ref, o: jnp.concatenate([ref[o:o + 1, :]] * ngrp, axis=1)

    def forward(z_sc):
        zs, zd = z_sc[0:H, :], z_sc[H:L, :]
        mid = alt_sign(start, tr) * mid_sc[ZH:ZH + 1, :]
        zre_e = dot(ce_ref[...], zs) + mid
        zim_e = dot(se_ref[...], zd)
        zre_o = dot(co_ref[...], zd)
        zim_o = dot(so_ref[...], zs) + mid
        pe, qe, po, qo = rep(pe_ref), rep(qe_ref), rep(po_ref), rep(qo_ref)
        a_e = zre_e * pe - zim_e * qe
        b_e = zim_e * pe + zre_e * qe
        a_o = zre_o * po - zim_o * qo
        b_o = zim_o * po + zre_o * qo
        a_sc[rows, :] = a_e.astype(BF16)
        b_sc[rows, :] = b_e.astype(BF16)
        a_sc[rows_hi, :] = a_o.astype(BF16)
        b_sc[rows_hi, :] = b_o.astype(BF16)
        mid_sc[YSP:YSP + 1, :] += jnp.sum(alt_sign(start, tr) * (a_e + b_o), axis=0, keepdims=True)

    def inverse(order, zin_sc):
        pe = dot(ce_ref[...], a_sc[0:H, :])
        qe = dot(se_ref[...], b_sc[0:H, :])
        po = dot(co_ref[...], a_sc[H:L, :])
        qo = dot(so_ref[...], b_sc[H:L, :])
        c = alt_sign(start, tr) * (mid_sc[NY:NY + 1, :] * rep_row(kny_ref, order))
        half_skip = 0.5 * rep_row(skip_ref, order)
        zs = zin_sc[rows, :].astype(F32)
        zd = zin_sc[rows_hi, :].astype(F32)
        y_lo = pe + qe + po + qo + c + half_skip * (zs + zd)
        y_hr = pe - qe - po + qo + c + half_skip * (zs - zd)
        return y_lo, y_hr

    def middle(order):
        return (mid_sc[YSP:YSP + 1, :] + alt_h * mid_sc[NY:NY + 1, :] * rep_row(kny_ref, order)
                + rep_row(skip_ref, order) * mid_sc[ZH:ZH + 1, :])

    @pl.when(ph == 0)
    def _():
        forward(z0_sc)

    @pl.when(ph == 1)
    def _():
        y_lo, y_hr = inverse(0, z0_sc)
        z_lo = (z1_sc[rows, :].astype(F32) * y_lo).astype(BF16)
        z_hr = (z1_sc[rows_hi, :].astype(F32) * y_hr).astype(BF16)
        z1_sc[rows, :] = z_lo
        z1_sc[rows_hi, :] = z_hr
        mid_sc[NYACC:NYACC + 1, :] += jnp.sum(alt_sign(start, tr) * (z_lo.astype(F32) + z_hr.astype(F32)),
                                              axis=0, keepdims=True)

        @pl.when(r == nrt - 1)
        def _():
            z1h = (mid_sc[X1H:X1H + 1, :] * middle(0)).astype(BF16).astype(F32)
            mid_sc[ZH:ZH + 1, :] = z1h
            mid_sc[NY:NY + 1, :] = mid_sc[NYACC:NYACC + 1, :] + alt_h * z1h
            mid_sc[YSP:YSP + 1, :] = jnp.zeros((1, W), F32)

    @pl.when(ph == 2)
    def _():
        @pl.when(r == 0)
        def _():
            to_operands(z1_sc)

        forward(z1_sc)

    @pl.when(ph == 3)
    def _():
        y_lo, y_hr = inverse(1, z1_sc)
        z0_sc[rows, :] = (x2_sc[rows, :].astype(F32) * y_lo).astype(BF16)
        z0_sc[rows_hi, :] = (x2_sc[rows_hi, :].astype(F32) * y_hr).astype(BF16)

        @pl.when(r == nrt - 1)
        def _():
            out_h = mid_sc[X2H:X2H + 1, :] * middle(1)
            jshift, e0 = flip_mats()
            first = (lax.broadcasted_iota(jnp.int32, (FOLD_BLK, 1), 0) == 0).astype(F32)
            for b in range(nb):
                up = flipped_block(z0_sc, b, jshift, e0)
                if b == 0:
                    up = up + first * out_h
                for g in range(ngrp):
                    cols = slice(g * tc, (g + 1) * tc)
                    o_ref[g, H + b * FOLD_BLK:H + (b + 1) * FOLD_BLK, :] = up[:, cols].astype(o_ref.dtype)
            for b in range(nb):
                lo_rows = slice(b * FOLD_BLK, (b + 1) * FOLD_BLK)
                for g in range(ngrp):
                    o_ref[g, lo_rows, :] = z0_sc[lo_rows, g * tc:(g + 1) * tc]


def _hy_core_folded(proj, short_w, short_b, skip, tabs, pq, k_ny, o_prev, row0, nseq, L, ngrp, tc):
    T = proj.shape[0]
    H = L // 2
    tr = min(H, 256)
    nrt = H // tr
    assert H % FOLD_BLK == 0 and row0 % (L * ngrp) == 0 and nseq % ngrp == 0 and T % L == 0
    sb0 = row0 // (L * ngrp)
    nct = D // tc
    p3 = proj.reshape(T // L, L, 3 * D)
    p_arr, q_arr = pq
    once = dict(pipeline_mode=pl.Buffered(1))

    def xspec(part):
        return pl.BlockSpec((ngrp, L, tc), lambda i, j, ph, r: (sb0 + i, 0, part * nct + j), **once)

    def wspec(part, rows_):
        return pl.BlockSpec((rows_, tc), lambda i, j, ph, r: (0, part * nct + j))

    def tspec(k_fwd, k_inv):
        return pl.BlockSpec((None, tr, H), lambda i, j, ph, r: (jnp.where(ph % 2 == 0, k_fwd, k_inv), r, 0))

    def pqspec(half):
        def imap(i, j, ph, r):
            return (ph // 2, half * nrt + jnp.where(ph % 2 == 0, r, nrt - 1), j)
        return pl.BlockSpec((None, tr, tc), imap)

    in_specs = [xspec(0), xspec(1), xspec(2), wspec(0, 3), wspec(1, 3), wspec(2, 3),
                wspec(0, 1), wspec(1, 1), wspec(2, 1),
                tspec(0, 0), tspec(1, 1), tspec(2, 4), tspec(3, 5),
                pqspec(0), pqspec(1), pqspec(0), pqspec(1),
                pl.BlockSpec((2, tc), lambda i, j, ph, r: (0, j)),
                pl.BlockSpec((2, tc), lambda i, j, ph, r: (0, j))]
    sb = short_b.reshape(1, 3 * D)
    args = [p3, p3, p3, short_w, short_w, short_w, sb, sb, sb, tabs, tabs, tabs, tabs,
            p_arr, p_arr, q_arr, q_arr, k_ny, skip]
    in_specs.append(pl.BlockSpec(memory_space=pl.ANY))
    args.append(o_prev.reshape(T // L, L, D))
    aliases = {len(args) - 1: 0}
    W = ngrp * tc
    out = pl.pallas_call(
        functools.partial(_hy_fold_kernel, L, tr, ngrp, tc),
        out_shape=jax.ShapeDtypeStruct((T // L, L, D), BF16),
        grid=(nseq // ngrp, nct, 4, nrt),
        in_specs=in_specs,
        out_specs=pl.BlockSpec((ngrp, L, tc), lambda i, j, ph, r: (sb0 + i, 0, j)),
        scratch_shapes=[pltpu.VMEM((L, W), BF16)] * 5 + [pltpu.VMEM((8, W), F32)],
        input_output_aliases=aliases,
        compiler_params=_cp("arbitrary", "arbitrary", "arbitrary", "arbitrary"),
        name="hy_core",
    )(*args)
    return out.reshape(T, D)


def _plain_out_kernel(x_ref, z_ref, g_ref, w_ref, b_ref, o_ref):
    acc = jnp.dot(z_ref[...], w_ref[...], preferred_element_type=F32) + b_ref[...]
    o_ref[...] = x_ref[...] + g_ref[...] * acc


def _plain_out(lay, x, mods, z, w, b):
    tm = lay.tile(1024)
    tn = 512
    kdim = z.shape[1]
    return pl.pallas_call(
        _plain_out_kernel,
        out_shape=jax.ShapeDtypeStruct((lay.T, D), F32),
        grid=(lay.T // tm, D // tn),
        in_specs=[pl.BlockSpec((tm, tn), lambda i, j: (i, j)),
                  pl.BlockSpec((tm, kdim), lambda i, j: (i, 0)),
                  pl.BlockSpec((None, 1, tn), lambda i, j: (lay.group(i * tm) * MOD_CHUNKS + 2, 0, j)),
                  pl.BlockSpec((kdim, tn), lambda i, j: (0, j)),
                  pl.BlockSpec((1, tn), lambda i, j: (0, j))],
        out_specs=pl.BlockSpec((tm, tn), lambda i, j: (i, j)),
        compiler_params=_cp("arbitrary", "arbitrary"),
        name="plain_out",
    )(x, z, mods, w, b.reshape(1, D))


def _hyena_layer(lay, x, mods, p):
    proj = _proj(lay, x, mods, 0, 1, p['w_in'].astype(BF16), p['b_in'].reshape(1, 3 * D), 768)
    z = jnp.zeros((lay.T, D), BF16)
    for row0, nseq, L, ngrp, tc in ((0, lay.B, lay.L, math.gcd(lay.B, 4), 256),
                                    (lay.TP, lay.NS, lay.LS, lay.NS, 256)):
        conv = (proj, p['short_w'], p['short_b'], p['skip'])
        if (L // 2) % FOLD_BLK == 0 and L > 2 * FOLD_BLK:
            xc, xs, xh, k_ny = _hyena_filters_folded(L, p)
            tabs = _dft_tables_folded(L)
            pq = _hy_spectrum_folded(L, tabs, xc, xs, xh)
            z = _hy_core_folded(*conv, tabs, pq, k_ny, z, row0, nseq, L, ngrp, tc)
        else:
            k_lo, k_hi, k_ny = _hyena_filters(L, p)
            cos, sin = _dft_tables(L)
            pq = _hy_spectrum(L, cos, sin, k_lo, k_hi)
            z = _hy_core(*conv, cos, sin, pq, k_ny, z, row0, nseq, L, ngrp, tc)
    return _plain_out(lay, x, mods, z, p['w_out'].astype(BF16), p['b_out'])


_NT = (((1,), (1,)), ((), ()))
_TN = (((0,), (0,)), ((), ()))


def _tri(dr):
    t = lax.broadcasted_iota(jnp.int32, (CHUNK, CHUNK), 0)
    s = lax.broadcasted_iota(jnp.int32, (CHUNK, CHUNK), 1)
    return (s <= t) if dr == 0 else (s >= t)


def _chunk_cumsum(g, dr):
    n = g.shape[0]
    pos = lax.broadcasted_iota(jnp.int32, g.shape, 0) % CHUNK
    sh = 1
    while sh < CHUNK:
        if dr == 0:
            g = g + jnp.where(pos >= sh, pltpu.roll(g, sh, 0), 0.0)
        else:
            g = g + jnp.where(pos < CHUNK - sh, pltpu.roll(g, n - sh, 0), 0.0)
        sh *= 2
    return g


def _head_epilogue(o_sc, gate_ref, ng_ref, a_ref, center):
    rows = o_sc.shape[0]
    tr = math.gcd(rows, 256)

    def tile(i, carry):
        r = pl.ds(pl.multiple_of(i * tr, tr), tr)
        o = o_sc[r, :]
        if center:
            o = o - jnp.mean(o, axis=-1, keepdims=True)
        o = o * lax.rsqrt(jnp.mean(o * o, axis=-1, keepdims=True) + RMS_EPS) * ng_ref[...]
        a_ref[r, :] = (o * _silu(gate_ref[r, :].astype(F32))).astype(a_ref.dtype)
        return carry

    lax.fori_loop(0, rows // tr, tile, 0)


def _gla_kernel(cps, nseg, U, has_s0, want_final, *refs):
    q_ref, k_ref, v_ref, lr_ref, w2f_ref, w2b_ref, gb_ref, gate_ref, ng_ref = refs[:9]
    s0_ref = refs[9] if has_s0 else None
    qin_sc, kin_sc, kout_sc, dec_sc, st_sc, s_sc, s0t_sc, o_ref = refs[-8:]
    outs = refs[-10:-8] if want_final else refs[-9:-8]
    a_ref = outs[0]
    sf_ref = outs[1] if want_final else None
    C = CHUNK
    nsc = cps // U
    nchunks = nseg * cps
    rows_total = nchunks * C
    w2 = (w2f_ref, w2b_ref)

    for dr in range(2):
        pre = jnp.dot(lr_ref[...], w2[dr][...], preferred_element_type=F32) + gb_ref[dr:dr + 1, :]
        g = (jnp.minimum(pre, 0.0) - jnp.log(1.0 + jnp.exp(-jnp.abs(pre)))) * (1.0 / GLA_TAU)
        b = _chunk_cumsum(g, dr)
        b3 = b.reshape(nchunks, C, GLA_DK)
        tot = b3[:, C - 1:C, :] if dr == 0 else b3[:, 0:1, :]
        dec_sc[...] = jnp.exp(tot).reshape(nchunks, GLA_DK)
        k = k_ref[...].astype(F32)
        qin_sc[...] = (q_ref[...].astype(F32) * (GLA_DK ** -0.5) * jnp.exp(b)).astype(BF16)
        kin_sc[...] = (k * jnp.exp(-b)).astype(BF16)
        kout_sc[...] = (k * jnp.exp(tot - b3).reshape(rows_total, GLA_DK)).astype(BF16)
        if has_s0:
            s0t_sc[...] = jnp.transpose(s0_ref[dr], (1, 0))
        tri = _tri(dr)

        def super_chunk(jj, carry, dr=dr, tri=tri):
            j = jj if dr == 0 else nseg * nsc - 1 - jj
            in_seg = j % nsc
            first = (in_seg == 0) if dr == 0 else (in_seg == nsc - 1)
            last = (in_seg == nsc - 1) if dr == 0 else (in_seg == 0)

            @pl.when(first)
            def _():
                s_sc[...] = s0t_sc[...] if has_s0 else jnp.zeros_like(s_sc)

            base = j * (U * C)
            for u in range(U):
                rows = pl.ds(pl.multiple_of(base + u * C, C), C)
                v = v_ref[rows, :]
                sc = lax.dot_general(qin_sc[rows, :], kin_sc[rows, :], _NT, preferred_element_type=F32)
                o = jnp.dot(jnp.where(tri, sc, 0.0).astype(BF16), v, preferred_element_type=F32)
                st_sc[u] = lax.dot_general(v, kout_sc[rows, :], _TN, preferred_element_type=F32)
                if dr == 0:
                    o_ref[rows, :] = o
                else:
                    o_ref[rows, :] += o
            s = s_sc[...]
            for u in (range(U) if dr == 0 else reversed(range(U))):
                kv = st_sc[u]
                st_sc[u] = s
                s = dec_sc[pl.ds(j * U + u, 1), :] * s + kv
            s_sc[...] = s
            for u in range(U):
                rows = pl.ds(pl.multiple_of(base + u * C, C), C)
                o_ref[rows, :] += lax.dot_general(qin_sc[rows, :], st_sc[u].astype(BF16), _NT,
                                                  preferred_element_type=F32)
            if want_final:
                @pl.when(last)
                def _():
                    sf_ref[j // nsc, dr] = jnp.transpose(s, (1, 0))
            return carry

        lax.fori_loop(0, nseg * nsc, super_chunk, 0)

    _head_epilogue(o_ref, gate_ref, ng_ref, a_ref, center=False)


def _gla_core(proj, w2f, w2b, gate_b, norm_g, s0, o_prev, row0, nseq, seqlen, nseg, want_final):
    T = proj.shape[0]
    rows = nseg * seqlen
    cps = seqlen // CHUNK
    U = math.gcd(cps, 8)
    assert row0 % rows == 0 and seqlen % CHUNK == 0 and nseq % nseg == 0
    rb = row0 // rows
    hk = GLA_H * GLA_DK
    has_s0 = s0 is not None
    assert not has_s0 or nseg == 1
    in_specs = [pl.BlockSpec((rows, GLA_DK), lambda b, h: (rb + b, h)),
                pl.BlockSpec((rows, GLA_DK), lambda b, h: (rb + b, GLA_H + h)),
                pl.BlockSpec((rows, GLA_DV), lambda b, h: (rb + b, 2 * hk // GLA_DV + h)),
                pl.BlockSpec((rows, 128), lambda b, h: (rb + b, (2 * hk + 2 * GLA_H * GLA_DV) // 128)),
                pl.BlockSpec((128, GLA_DK), lambda b, h: (0, h)),
                pl.BlockSpec((128, GLA_DK), lambda b, h: (0, h)),
                pl.BlockSpec((2, GLA_DK), lambda b, h: (0, h)),
                pl.BlockSpec((rows, GLA_DV), lambda b, h: (rb + b, (2 * hk) // GLA_DV + GLA_H + h)),
                pl.BlockSpec((1, GLA_DV), lambda b, h: (0, 0))]
    args = [proj, proj, proj, proj, w2f, w2b, gate_b, proj, norm_g.reshape(1, GLA_DV)]
    if has_s0:
        in_specs.append(pl.BlockSpec((None, 2, None, GLA_DK, GLA_DV), lambda b, h: (b, 0, h, 0, 0)))
        args.append(s0)
    in_specs.append(pl.BlockSpec(memory_space=pl.ANY))
    args.append(o_prev)
    aliases = {len(args) - 1: 0}
    out_shape = [jax.ShapeDtypeStruct((T, GLA_H * GLA_DV), BF16)]
    out_specs = [pl.BlockSpec((rows, GLA_DV), lambda b, h: (rb + b, h))]
    if want_final:
        out_shape.append(jax.ShapeDtypeStruct((nseq, 2, GLA_H, GLA_DK, GLA_DV), F32))
        out_specs.append(pl.BlockSpec((nseg, 2, None, GLA_DK, GLA_DV), lambda b, h: (b, 0, h, 0, 0)))
    outs = pl.pallas_call(
        functools.partial(_gla_kernel, cps, nseg, U, has_s0, want_final),
        out_shape=tuple(out_shape),
        grid=(nseq // nseg, GLA_H),
        in_specs=in_specs,
        out_specs=tuple(out_specs),
        scratch_shapes=[pltpu.VMEM((rows, GLA_DK), BF16)] * 3
        + [pltpu.VMEM((nseg * cps, GLA_DK), F32), pltpu.VMEM((U, GLA_DV, GLA_DK), F32),
           pltpu.VMEM((GLA_DV, GLA_DK), F32), pltpu.VMEM((GLA_DV, GLA_DK), F32),
           pltpu.VMEM((rows, GLA_DV), F32)],
        input_output_aliases=aliases,
        compiler_params=_cp("arbitrary", "arbitrary"),
        name="gla",
    )(*args)
    return (outs[0], outs[1]) if want_final else (outs[0], None)


def _ret_kernel(cps, nseg, U, has_s0, want_final, rope, *refs):
    q_ref, k_ref, v_ref, dm_ref, qd_ref, kd_ref, cd_ref, gate_ref, ng_ref = refs[:9]
    nxt = 9
    if rope:
        cos_ref, sin_ref = refs[9:11]
        nxt = 11
    s0_ref = refs[nxt] if has_s0 else None
    qr_sc, kr_sc, qd_sc, kd_sc, st_sc, s_sc, o_ref = refs[-7:]
    outs = refs[-9:-7] if want_final else refs[-8:-7]
    a_ref = outs[0]
    sf_ref = outs[1] if want_final else None
    C = RET_CHUNK
    nsc = cps // U
    R = U * C
    SB = 64

    def rot(x, rows):
        if not rope:
            return x
        half = x.shape[1] // 2
        swapped = jnp.concatenate([pltpu.roll(x[:, :half], half // 2, 1),
                                   pltpu.roll(x[:, half:], half // 2, 1)], axis=1)
        return x * cos_ref[rows, :] + swapped * sin_ref[rows, :]

    for dr in range(2):
        def super_chunk(jj, carry, dr=dr):
            j = jj if dr == 0 else nseg * nsc - 1 - jj
            in_seg = j % nsc
            first = (in_seg == 0) if dr == 0 else (in_seg == nsc - 1)
            last = (in_seg == nsc - 1) if dr == 0 else (in_seg == 0)

            @pl.when(first)
            def _():
                s_sc[...] = s0_ref[dr] if has_s0 else jnp.zeros_like(s_sc)

            base = pl.multiple_of(j * R, R)
            rows_r = pl.ds(base, R)
            q = rot(q_ref[rows_r, :].astype(F32), rows_r)
            k = rot(k_ref[rows_r, :].astype(F32), rows_r) * (RET_DK ** -0.5)
            qr_sc[...] = q.astype(BF16)
            kr_sc[...] = k.astype(BF16)
            qd_sc[...] = (q.reshape(U, C, RET_DK) * qd_ref[dr][None]).reshape(R, RET_DK).astype(BF16)
            kd_sc[...] = (k.reshape(U, C, RET_DK) * kd_ref[dr][None]).reshape(R, RET_DK).astype(BF16)
            for u in range(U):
                loc = pl.ds(u * C, C)
                rows = pl.ds(pl.multiple_of(base + u * C, C), C)
                v = v_ref[rows, :]
                sc = lax.dot_general(qr_sc[loc, :], kr_sc[loc, :], _NT, preferred_element_type=F32)
                o = jnp.dot((sc * dm_ref[dr]).astype(BF16), v, preferred_element_type=F32)
                st_sc[u] = lax.dot_general(kd_sc[loc, :], v, _TN, preferred_element_type=F32)
                if dr == 0:
                    o_ref[rows, :] = o
                else:
                    o_ref[rows, :] += o
            cd = cd_ref[dr]
            for r0 in range(0, RET_DK, SB):
                srows = pl.ds(r0, SB)
                s = s_sc[srows, :]
                for u in (range(U) if dr == 0 else reversed(range(U))):
                    kv = st_sc[u, srows, :]
                    st_sc[u, srows, :] = s
                    s = cd * s + kv
                s_sc[srows, :] = s
            for u in range(U):
                rows = pl.ds(pl.multiple_of(base + u * C, C), C)
                o_ref[rows, :] += jnp.dot(qd_sc[pl.ds(u * C, C), :], st_sc[u].astype(BF16),
                                          preferred_element_type=F32)
            if want_final:
                @pl.when(last)
                def _():
                    sf_ref[j // nsc, dr] = s_sc[...]
            return carry

        lax.fori_loop(0, nseg * nsc, super_chunk, 0)

    _head_epilogue(o_ref, gate_ref, ng_ref, a_ref, center=True)


def _ret_tables(log_decay):
    C = RET_CHUNK
    lg = log_decay.astype(F32)[:, :, None, None]
    t = jnp.arange(C, dtype=F32)[:, None]
    s = jnp.arange(C, dtype=F32)[None, :]
    lag = jnp.stack([t - s, s - t])[:, None]
    dmask = jnp.where(lag >= 0, jnp.exp(jnp.maximum(lag, 0.0) * lg), 0.0)
    tl = jnp.arange(C, dtype=F32)[None, None, :, None]
    qdec = jnp.concatenate([jnp.exp((tl + 1.0) * lg[0:1]), jnp.exp((C - tl) * lg[1:2])], axis=0)
    kdec = jnp.concatenate([jnp.exp((C - 1.0 - tl) * lg[0:1]), jnp.exp(tl * lg[1:2])], axis=0)
    cdec = jnp.exp(C * lg)
    return dmask, qdec, kdec, cdec


def _rope_tables(seqlen, dk):
    half = dk // 2
    nf = half // 2
    pos = jnp.arange(seqlen, dtype=jnp.int32)
    inv = ROPE_BASE ** (-jnp.arange(nf, dtype=F32) / nf)
    ang_r = (pos // GRID_W).astype(F32)[:, None] * inv[None, :]
    ang_c = (pos % GRID_W).astype(F32)[:, None] * inv[None, :]
    cos = jnp.concatenate([jnp.cos(ang_r)] * 2 + [jnp.cos(ang_c)] * 2, axis=1)
    sin = jnp.concatenate([-jnp.sin(ang_r), jnp.sin(ang_r), -jnp.sin(ang_c), jnp.sin(ang_c)], axis=1)
    return cos, sin


def _ret_core(proj, tabs, norm_g, s0, o_prev, row0, nseq, seqlen, nseg, want_final, rope):
    T = proj.shape[0]
    rows = nseg * seqlen
    C = RET_CHUNK
    cps = seqlen // C
    U = math.gcd(cps, 4)
    assert row0 % rows == 0 and seqlen % C == 0 and nseq % nseg == 0
    rb = row0 // rows
    hk, hv = RET_H * RET_DK, RET_H * RET_DV
    has_s0 = s0 is not None
    assert not (has_s0 or rope) or nseg == 1
    tspec = lambda r, c: pl.BlockSpec((2, None, r, c), lambda b, h: (0, h, 0, 0))
    mode = dict(pipeline_mode=pl.Buffered(1)) if rows * RET_DV * 2 >= (4 << 20) else {}
    in_specs = [pl.BlockSpec((rows, RET_DK), lambda b, h: (rb + b, h), **mode),
                pl.BlockSpec((rows, RET_DK), lambda b, h: (rb + b, RET_H + h), **mode),
                pl.BlockSpec((rows, RET_DV), lambda b, h: (rb + b, 2 * hk // RET_DV + h), **mode),
                tspec(C, C), tspec(C, 1), tspec(C, 1), tspec(1, 1),
                pl.BlockSpec((rows, RET_DV), lambda b, h: (rb + b, (2 * hk + hv) // RET_DV + h), **mode),
                pl.BlockSpec((1, RET_DV), lambda b, h: (0, 0))]
    args = [proj, proj, proj, *tabs, proj, norm_g.reshape(1, RET_DV)]
    if rope:
        cos, sin = _rope_tables(seqlen, RET_DK)
        in_specs += [pl.BlockSpec((seqlen, RET_DK), lambda b, h: (0, 0), pipeline_mode=pl.Buffered(1))] * 2
        args += [cos, sin]
    if has_s0:
        in_specs.append(pl.BlockSpec((None, 2, None, RET_DK, RET_DV), lambda b, h: (b, 0, h, 0, 0)))
        args.append(s0)
    in_specs.append(pl.BlockSpec(memory_space=pl.ANY))
    args.append(o_prev)
    aliases = {len(args) - 1: 0}
    out_shape = [jax.ShapeDtypeStruct((T, hv), BF16)]
    out_specs = [pl.BlockSpec((rows, RET_DV), lambda b, h: (rb + b, h))]
    if want_final:
        out_shape.append(jax.ShapeDtypeStruct((nseq, 2, RET_H, RET_DK, RET_DV), F32))
        out_specs.append(pl.BlockSpec((nseg, 2, None, RET_DK, RET_DV), lambda b, h: (b, 0, h, 0, 0)))
    outs = pl.pallas_call(
        functools.partial(_ret_kernel, cps, nseg, U, has_s0, want_final, rope),
        out_shape=tuple(out_shape),
        grid=(nseq // nseg, RET_H),
        in_specs=in_specs,
        out_specs=tuple(out_specs),
        scratch_shapes=[pltpu.VMEM((U * C, RET_DK), BF16)] * 4
        + [pltpu.VMEM((U, RET_DK, RET_DV), F32), pltpu.VMEM((RET_DK, RET_DV), F32),
           pltpu.VMEM((rows, RET_DV), F32)],
        input_output_aliases=aliases,
        compiler_params=_cp("arbitrary", "arbitrary"),
        name="ret",
    )(*args)
    return (outs[0], outs[1]) if want_final else (outs[0], None)


def _gla_layer(lay, x, mods, p, s0):
    hk, hv = GLA_H * GLA_DK, GLA_H * GLA_DV
    w_all = jnp.concatenate([p['w_in'], p['gate_w1'][0], p['gate_w1'][1],
                             jnp.zeros((D, 128 - 2 * GLA_RANK), F32)], axis=1).astype(BF16)
    proj = _proj(lay, x, mods, 0, 1, w_all, jnp.zeros((1, w_all.shape[1]), F32), 640)
    pad = lambda w, lo: jnp.pad(w, ((lo, 128 - GLA_RANK - lo), (0, 0))).astype(BF16)
    w2f, w2b = pad(p['gate_w2'][0], 0), pad(p['gate_w2'][1], GLA_RANK)
    a = jnp.zeros((lay.T, hv), BF16)
    a, s_fin = _gla_core(proj, w2f, w2b, p['gate_b'], p['norm_g'], None, a, 0, lay.B, lay.L,
                         math.gcd(lay.B, 8), True)
    a, _ = _gla_core(proj, w2f, w2b, p['gate_b'], p['norm_g'], s0, a, lay.TP, lay.NS, lay.LS, 1, False)
    x = _plain_out(lay, x, mods, a, p['w_out'].astype(BF16), jnp.zeros((D,), F32))
    return x, s_fin


def _ret_layer(lay, x, mods, p, s0):
    hk, hv = RET_H * RET_DK, RET_H * RET_DV
    proj = _proj(lay, x, mods, 0, 1, p['w_in'].astype(BF16), jnp.zeros((1, 2 * hk + 2 * hv), F32), 1536)
    tabs = _ret_tables(p['log_decay'])
    a = jnp.zeros((lay.T, hv), BF16)
    a, s_fin = _ret_core(proj, tabs, p['norm_g'], None, a, 0, lay.B, lay.L, math.gcd(lay.B, 8), True, False)
    a, _ = _ret_core(proj, tabs, p['norm_g'], s0, a, lay.TP, lay.NS, lay.LS, 1, False, True)
    x = _plain_out(lay, x, mods, a, p['w_out'].astype(BF16), jnp.zeros((D,), F32))
    return x, s_fin


MOE_BM = 512
EXPERT_TF = 1792
ROUTER_LANES = 128
DMA_UNROLL = 8


def _router_kernel(x_ref, sh_ref, sc_ref, rw_ref, h_ref, idx_ref, gate_ref, rank_ref, cnt_ref):
    @pl.when(pl.program_id(0) == 0)
    def _():
        cnt_ref[...] = jnp.zeros_like(cnt_ref)

    h = _modulate(x_ref[...], sh_ref[...], sc_ref[...])
    h_ref[...] = h
    logits = jnp.dot(h, rw_ref[...], precision=HIGHEST, preferred_element_type=F32)
    lane = lax.broadcasted_iota(jnp.int32, logits.shape, 1)
    neg = jnp.float32(-jnp.inf)
    logits = jnp.where(lane < N_EXPERTS, logits, neg)
    m1 = jnp.max(logits, axis=-1, keepdims=True)
    i1 = jnp.min(jnp.where(logits == m1, lane, ROUTER_LANES), axis=-1, keepdims=True)
    rest = jnp.where(lane == i1, neg, logits)
    m2 = jnp.max(rest, axis=-1, keepdims=True)
    i2 = jnp.min(jnp.where(rest == m2, lane, ROUTER_LANES), axis=-1, keepdims=True)
    e2 = jnp.exp(m2 - m1)
    g1 = 1.0 / (1.0 + e2)
    idx_ref[:, 0:1] = i1
    idx_ref[:, 1:2] = i2
    gate_ref[:, 0:1] = g1
    gate_ref[:, 1:2] = e2 * g1
    tm = logits.shape[0]
    sel1 = lane == i1
    sel2 = lane == i2
    picked = jnp.where(jnp.logical_or(sel1, sel2), 1.0, 0.0)
    before = (lax.broadcasted_iota(jnp.int32, (tm, tm), 1)
              < lax.broadcasted_iota(jnp.int32, (tm, tm), 0)).astype(BF16)
    prior = jnp.dot(before, picked.astype(BF16), preferred_element_type=F32) + cnt_ref[...]
    rank_ref[:, 0:1] = jnp.sum(jnp.where(sel1, prior, 0.0), axis=-1, keepdims=True).astype(jnp.int32)
    rank_ref[:, 1:2] = jnp.sum(jnp.where(sel2, prior, 0.0), axis=-1, keepdims=True).astype(jnp.int32)
    cnt_ref[...] += jnp.sum(picked, axis=0, keepdims=True)


def _router(lay, x, mods, router_w):
    tm = lay.tile(512)
    rw = jnp.pad(router_w, ((0, 0), (0, ROUTER_LANES - N_EXPERTS)))
    return pl.pallas_call(
        _router_kernel,
        out_shape=(jax.ShapeDtypeStruct((lay.T, D), F32),
                   jax.ShapeDtypeStruct((lay.T, 2), jnp.int32),
                   jax.ShapeDtypeStruct((lay.T, 2), F32),
                   jax.ShapeDtypeStruct((lay.T, 2), jnp.int32),
                   jax.ShapeDtypeStruct((1, ROUTER_LANES), F32)),
        grid=(lay.T // tm,),
        in_specs=[pl.BlockSpec((tm, D), lambda i: (i, 0)),
                  _mod_spec(lay, tm, 3, 1), _mod_spec(lay, tm, 4, 1),
                  pl.BlockSpec((D, ROUTER_LANES), lambda i: (0, 0))],
        out_specs=(pl.BlockSpec((tm, D), lambda i: (i, 0)),
                   pl.BlockSpec((tm, 2), lambda i: (i, 0)),
                   pl.BlockSpec((tm, 2), lambda i: (i, 0)),
                   pl.BlockSpec((tm, 2), lambda i: (i, 0)),
                   pl.BlockSpec((1, ROUTER_LANES), lambda i: (0, 0))),
        compiler_params=_cp("arbitrary"),
        name="router",
    )(x, mods, mods, rw)


def _moe_plan(idx, rank, counts, bm):
    a = idx.size
    counts = counts[0, :N_EXPERTS].astype(jnp.int32)
    padded = (counts + bm - 1) // bm * bm
    pad_end = jnp.cumsum(padded)
    pad_start = pad_end - padded
    hit = idx[..., None] == jnp.arange(N_EXPERTS, dtype=jnp.int32)
    dest = (rank + jnp.sum(jnp.where(hit, pad_start, 0), axis=-1)).reshape(a).astype(jnp.int32)
    nb = -(-(a + N_EXPERTS * (bm - 1)) // bm)
    block_start = jnp.arange(nb, dtype=jnp.int32) * bm
    block_e = jnp.sum((block_start[:, None] >= pad_end[None, :]).astype(jnp.int32), axis=1)
    block_e = jnp.minimum(block_e, N_EXPERTS - 1)
    nvalid = (pad_end[-1] // bm).astype(jnp.int32).reshape(1)
    fill = jnp.concatenate([pad_start + counts, pad_end, nvalid]).astype(jnp.int32)
    return dest, fill, block_e, nvalid, nb


def _dispatch_kernel(tm, bm, nb, dest_ref, fill_ref, h_ref, xs_hbm, zero_sc, sem, zsem):
    i = pl.program_id(0)
    zr = zero_sc.shape[0]

    @pl.when(i == 0)
    def _():
        zero_sc[...] = jnp.zeros_like(zero_sc)

        def zero_row(r):
            return pltpu.make_async_copy(zero_sc.at[pl.ds(0, 1)], xs_hbm.at[pl.ds(r, 1)], zsem)

        def zero_rows(r):
            return pltpu.make_async_copy(zero_sc, xs_hbm.at[pl.ds(pl.multiple_of(r, zr), zr)], zsem)

        for e in range(N_EXPERTS):
            lo, hi = fill_ref[e], fill_ref[N_EXPERTS + e]
            lax.fori_loop(lo, hi, lambda r, c: (zero_row(r).start(), c)[1], 0)
            lax.fori_loop(lo, hi, lambda r, c: (zero_row(r).wait(), c)[1], 0)
        lo, hi = fill_ref[2 * N_EXPERTS] * (bm // zr), nb * (bm // zr)
        lax.fori_loop(lo, hi, lambda q, c: (zero_rows(q * zr).start(), c)[1], 0)
        lax.fori_loop(lo, hi, lambda q, c: (zero_rows(q * zr).wait(), c)[1], 0)

    def row_copy(r, dst):
        return pltpu.make_async_copy(h_ref.at[pl.ds(r, 1)], xs_hbm.at[pl.ds(dst, 1)], sem)

    def issue(r, carry):
        a = 2 * (i * tm + r)
        row_copy(r, dest_ref[a]).start()
        row_copy(r, dest_ref[a + 1]).start()
        return carry

    def drain(r, carry):
        row_copy(r, 0).wait()
        row_copy(r, 0).wait()
        return carry

    lax.fori_loop(0, tm, issue, 0, unroll=DMA_UNROLL)
    lax.fori_loop(0, tm, drain, 0, unroll=DMA_UNROLL)


def _dispatch(lay, h, dest, fill, nb, bm):
    tm = lay.tile(512)
    grid_spec = pltpu.PrefetchScalarGridSpec(
        num_scalar_prefetch=2,
        grid=(lay.T // tm,),
        in_specs=[pl.BlockSpec((tm, D), lambda i, d, f: (i, 0))],
        out_specs=pl.BlockSpec(memory_space=pl.ANY),
        scratch_shapes=[pltpu.VMEM((64, D), F32), pltpu.SemaphoreType.DMA(()), pltpu.SemaphoreType.DMA(())],
    )
    return pl.pallas_call(
        functools.partial(_dispatch_kernel, tm, bm, nb),
        out_shape=jax.ShapeDtypeStruct((nb * bm, D), F32),
        grid_spec=grid_spec,
        compiler_params=_cp("arbitrary"),
        name="dispatch",
    )(dest, fill, h)


def _experts_kernel(nf, be_ref, nv_ref, xs_ref, wa_ref, wb_ref, wo_ref, o_ref, xb_sc, acc_sc):
    i = pl.program_id(0)
    f = pl.program_id(1)
    valid = i < nv_ref[0]

    @pl.when(jnp.logical_and(valid, f == 0))
    def _():
        xb_sc[...] = xs_ref[...].astype(BF16)

    @pl.when(valid)
    def _():
        xb = xb_sc[...]
        a = jnp.dot(xb, wa_ref[...], preferred_element_type=F32)
        b = jnp.dot(xb, wb_ref[...], preferred_element_type=F32)
        h = (_silu(a) * b).astype(BF16)
        y = jnp.dot(h, wo_ref[...], preferred_element_type=F32)

        @pl.when(f == 0)
        def _():
            acc_sc[...] = y

        @pl.when(f > 0)
        def _():
            acc_sc[...] += y

    @pl.when(f == nf - 1)
    def _():
        o_ref[...] = jnp.where(valid, acc_sc[...], 0.0)


def _experts(xs, block_e, nvalid, nb, bm, w_in, w_out):
    tf = EXPERT_TF
    nf = EXPERT_DIM // tf

    def wmap(off):
        def imap(i, f, be, nv):
            fe = jnp.where(i < nv[0], f, nf - 1)
            return (be[i], 0, off + fe)
        return imap

    def womap(i, f, be, nv):
        fe = jnp.where(i < nv[0], f, nf - 1)
        return (be[i], fe, 0)

    grid_spec = pltpu.PrefetchScalarGridSpec(
        num_scalar_prefetch=2,
        grid=(nb, nf),
        in_specs=[pl.BlockSpec((bm, D), lambda i, f, be, nv: (jnp.minimum(i, nv[0] - 1), 0)),
                  pl.BlockSpec((None, D, tf), wmap(0)),
                  pl.BlockSpec((None, D, tf), wmap(nf)),
                  pl.BlockSpec((None, tf, D), womap)],
        out_specs=pl.BlockSpec((bm, D), lambda i, f, be, nv: (i, 0)),
        scratch_shapes=[pltpu.VMEM((bm, D), BF16), pltpu.VMEM((bm, D), F32)],
    )
    return pl.pallas_call(
        functools.partial(_experts_kernel, nf),
        out_shape=jax.ShapeDtypeStruct((nb * bm, D), F32),
        grid_spec=grid_spec,
        compiler_params=_cp("arbitrary", "arbitrary"),
        name="experts",
    )(block_e, nvalid, xs, w_in, w_in, w_out)


def _combine_kernel(tm, nt, final, dest_ref, x_ref, gate_ref, g_ref, fg_ref, ys_hbm, o_ref, y_sc, sem):
    i = pl.program_id(0)
    slot = i % 2

    def row_copy(s, k, r, src):
        return pltpu.make_async_copy(ys_hbm.at[pl.ds(src, 1)], y_sc.at[s, k, pl.ds(r, 1)], sem.at[s])

    def issue_tile(t, s):
        def issue(r, carry):
            a = 2 * (t * tm + r)
            row_copy(s, 0, r, dest_ref[a]).start()
            row_copy(s, 1, r, dest_ref[a + 1]).start()
            return carry
        lax.fori_loop(0, tm, issue, 0, unroll=DMA_UNROLL)

    @pl.when(i == 0)
    def _():
        issue_tile(0, 0)

    @pl.when(i + 1 < nt)
    def _():
        issue_tile(i + 1, 1 - slot)

    def drain(r, carry):
        row_copy(slot, 0, r, 0).wait()
        row_copy(slot, 1, r, 0).wait()
        return carry

    lax.fori_loop(0, tm, drain, 0, unroll=DMA_UNROLL)
    gate = gate_ref[...]
    out = x_ref[...] + g_ref[...] * (gate[:, 0:1] * y_sc[slot, 0] + gate[:, 1:2] * y_sc[slot, 1])
    if final:
        ms = jnp.mean(out * out, axis=-1, keepdims=True)
        out = out * lax.rsqrt(ms + RMS_EPS) * fg_ref[...]
    o_ref[...] = out


def _combine(lay, x, mods, gates, ys, dest, final_g):
    tm = lay.tile(256)
    nt = lay.T // tm
    final = final_g is not None
    fg = (final_g if final else jnp.ones((D,), F32)).reshape(1, D)
    grid_spec = pltpu.PrefetchScalarGridSpec(
        num_scalar_prefetch=1,
        grid=(nt,),
        in_specs=[pl.BlockSpec((tm, D), lambda i, d: (i, 0)),
                  pl.BlockSpec((tm, 2), lambda i, d: (i, 0)),
                  pl.BlockSpec((None, 1, D), lambda i, d: (lay.group(i * tm) * MOD_CHUNKS + 5, 0, 0)),
                  pl.BlockSpec((1, D), lambda i, d: (0, 0)),
                  pl.BlockSpec(memory_space=pl.ANY)],
        out_specs=pl.BlockSpec((tm, D), lambda i, d: (i, 0)),
        scratch_shapes=[pltpu.VMEM((2, 2, tm, D), F32), pltpu.SemaphoreType.DMA((2,))],
    )
    return pl.pallas_call(
        functools.partial(_combine_kernel, tm, nt, final),
        out_shape=jax.ShapeDtypeStruct((lay.T, D), F32),
        grid_spec=grid_spec,
        compiler_params=_cp("arbitrary"),
        name="combine",
    )(dest, x, gates, mods, fg, ys)


def _moe_layer(lay, x, mods, router_w, w_in, w_out, final_g=None, bm=MOE_BM):
    h, idx, gates, rank, counts = _router(lay, x, mods, router_w)
    dest, fill, block_e, nvalid, nb = _moe_plan(idx, rank, counts, bm)
    xs = _dispatch(lay, h, dest, fill, nb, bm)
    ys = _experts(xs, block_e, nvalid, nb, bm, w_in, w_out)
    return _combine(lay, x, mods, gates, ys, dest, final_g)


def kernel(x_prompt, x_sample, c, state_l0_s5_re, state_l0_s5_im, state_l2_gla, state_l3_ret, c_ctx, l0_mod_w, l0_mod_b, l0_s5_a_re, l0_s5_a_im, l0_s5_log_dt, l0_s5_b_re, l0_s5_b_im, l0_s5_c_re, l0_s5_c_im, l0_s5_d, l0_s5_glu_w, l0_ffn_w_in, l0_ffn_w_out, l1_mod_w, l1_mod_b, l1_hy_w_in, l1_hy_b_in, l1_hy_short_w, l1_hy_short_b, l1_hy_f_w1, l1_hy_f_b1, l1_hy_f_w2, l1_hy_f_b2, l1_hy_f_w3, l1_hy_f_freq, l1_hy_skip, l1_hy_w_out, l1_hy_b_out, l1_moe_router, l1_moe_w_in, l1_moe_w_out, l2_mod_w, l2_mod_b, l2_gla_w_in, l2_gla_gate_w1, l2_gla_gate_w2, l2_gla_gate_b, l2_gla_norm_g, l2_gla_w_out, l2_ffn_w_in, l2_ffn_w_out, l3_mod_w, l3_mod_b, l3_ret_w_in, l3_ret_log_decay, l3_ret_norm_g, l3_ret_w_out, l3_moe_router, l3_moe_w_in, l3_moe_w_out, final_norm_g):
    B, L, _ = x_prompt.shape
    NS, LS, _ = x_sample.shape
    lay = Layout(B, L, NS, LS)
    x = jnp.concatenate([x_prompt.reshape(B * L, D), x_sample.reshape(NS * LS, D)], axis=0)
    cond = jnp.concatenate([c_ctx[None], c, jnp.zeros((8 - 1 - NS, D), F32)], axis=0)
    mods0 = _mods(cond, l0_mod_w, l0_mod_b)
    p0 = dict(a_re=l0_s5_a_re, a_im=l0_s5_a_im, log_dt=l0_s5_log_dt, b_re=l0_s5_b_re, b_im=l0_s5_b_im,
              c_re=l0_s5_c_re, c_im=l0_s5_c_im, d=l0_s5_d, glu_w=l0_s5_glu_w.astype(BF16))
    x, s5_re, s5_im = _s5_layer(lay, x, mods0, p0, state_l0_s5_re, state_l0_s5_im)
    x = _ffn(lay, x, mods0, l0_ffn_w_in.astype(BF16), l0_ffn_w_out.astype(BF16))

    mods1 = _mods(cond, l1_mod_w, l1_mod_b)
    p1 = dict(w_in=l1_hy_w_in, b_in=l1_hy_b_in, short_w=l1_hy_short_w, short_b=l1_hy_short_b,
              f_w1=l1_hy_f_w1, f_b1=l1_hy_f_b1, f_w2=l1_hy_f_w2, f_b2=l1_hy_f_b2, f_w3=l1_hy_f_w3,
              f_freq=l1_hy_f_freq, skip=l1_hy_skip, w_out=l1_hy_w_out, b_out=l1_hy_b_out)
    x = _hyena_layer(lay, x, mods1, p1)
    x = _moe_layer(lay, x, mods1, l1_moe_router, l1_moe_w_in.astype(BF16), l1_moe_w_out.astype(BF16))

    mods2 = _mods(cond, l2_mod_w, l2_mod_b)
    p2 = dict(w_in=l2_gla_w_in, gate_w1=l2_gla_gate_w1, gate_w2=l2_gla_gate_w2, gate_b=l2_gla_gate_b,
              norm_g=l2_gla_norm_g, w_out=l2_gla_w_out)
    x, gla_state = _gla_layer(lay, x, mods2, p2, state_l2_gla)
    x = _ffn(lay, x, mods2, l2_ffn_w_in.astype(BF16), l2_ffn_w_out.astype(BF16))

    mods3 = _mods(cond, l3_mod_w, l3_mod_b)
    p3 = dict(w_in=l3_ret_w_in, log_decay=l3_ret_log_decay, norm_g=l3_ret_norm_g, w_out=l3_ret_w_out)
    x, ret_state = _ret_layer(lay, x, mods3, p3, state_l3_ret)
    y = _moe_layer(lay, x, mods3, l3_moe_router, l3_moe_w_in.astype(BF16), l3_moe_w_out.astype(BF16),
                   final_g=final_norm_g)
    return (y[:lay.TP].reshape(B, L, D), y[lay.TP:].reshape(NS, LS, D), s5_re, s5_im, gla_state, ret_state)
```

```python
import functools
import math

import jax
import jax.numpy as jnp
import numpy as np
from jax import lax
from jax.experimental import pallas as pl
from jax.experimental.pallas import tpu as pltpu

F32 = jnp.float32
BF16 = jnp.bfloat16
HIGHEST = lax.Precision.HIGHEST

D = 1024
RMS_EPS = 1e-6
MOD_CHUNKS = 6
GRID_W = 64

S5_Q = 16
S5_G = D // S5_Q
S5_P = 64
S5_T = 16
S5_SCAN_ROWS = 64

HY_BANDS = 16
HY_TARGET = 1e-2
HY_FAST_PCT = 0.3
HY_SLOW_PCT = 1.5

GLA_H, GLA_DK, GLA_DV = 4, 128, 256
GLA_RANK = 16
GLA_TAU = 16.0
RET_H, RET_DK, RET_DV = 4, 256, 512
CHUNK = 64
RET_CHUNK = 256
ROPE_BASE = 10000.0

FFN_DIM = 2816
N_EXPERTS = 8
EXPERT_DIM = 3584

VMEM_LIMIT_V7X = 56 * 1024 * 1024


def _cp(*sem):
    return pltpu.CompilerParams(dimension_semantics=sem, vmem_limit_bytes=VMEM_LIMIT_V7X)


def _silu(x):
    return x * jax.nn.sigmoid(x)


def _modulate(x, shift, scale):
    ms = jnp.mean(x * x, axis=-1, keepdims=True)
    return x * lax.rsqrt(ms + RMS_EPS) * (1.0 + scale) + shift


class Layout:
    def __init__(self, n_prompt, l_prompt, n_sample, l_sample):
        self.B, self.L, self.NS, self.LS = n_prompt, l_prompt, n_sample, l_sample
        self.TP = n_prompt * l_prompt
        self.T = self.TP + n_sample * l_sample

    def tile(self, want):
        t = math.gcd(math.gcd(self.TP, self.LS), want)
        assert t % 8 == 0
        return t

    def group(self, row):
        return jnp.where(row < self.TP, 0, 1 + (row - self.TP) // self.LS)


def _mod_spec(lay, tm, chunk, ngrid):
    def imap(*ids):
        return (lay.group(ids[0] * tm) * MOD_CHUNKS + chunk, 0, 0)
    del ngrid
    return pl.BlockSpec((None, 1, D), imap)


def _mods_kernel(c_ref, w_ref, b_ref, o_ref):
    o_ref[...] = jnp.dot(_silu(c_ref[...]), w_ref[...], precision=HIGHEST,
                         preferred_element_type=F32) + b_ref[...]


def _mods(cond, w, b):
    n = MOD_CHUNKS * D
    tn = 1536
    out = pl.pallas_call(
        _mods_kernel,
        out_shape=jax.ShapeDtypeStruct((8, n), F32),
        grid=(n // tn,),
        in_specs=[pl.BlockSpec((8, D), lambda j: (0, 0)),
                  pl.BlockSpec((D, tn), lambda j: (0, j)),
                  pl.BlockSpec((1, tn), lambda j: (0, j))],
        out_specs=pl.BlockSpec((8, tn), lambda j: (0, j)),
        compiler_params=_cp("arbitrary"),
        name="mods",
    )(cond, w, b.reshape(1, n))
    return out.reshape(8 * MOD_CHUNKS, 1, D)


def _modulate_kernel(x_ref, sh_ref, sc_ref, o_ref):
    o_ref[...] = _modulate(x_ref[...], sh_ref[...], sc_ref[...]).astype(o_ref.dtype)


def _modulate_call(lay, x, mods, c_shift, c_scale, dtype):
    tm = lay.tile(512)
    return pl.pallas_call(
        _modulate_kernel,
        out_shape=jax.ShapeDtypeStruct((lay.T, D), dtype),
        grid=(lay.T // tm,),
        in_specs=[pl.BlockSpec((tm, D), lambda i: (i, 0)),
                  _mod_spec(lay, tm, c_shift, 1), _mod_spec(lay, tm, c_scale, 1)],
        out_specs=pl.BlockSpec((tm, D), lambda i: (i, 0)),
        compiler_params=_cp("arbitrary"),
        name="modulate",
    )(x, mods, mods)


def _proj_kernel(x_ref, sh_ref, sc_ref, w_ref, b_ref, o_ref, u_sc):
    @pl.when(pl.program_id(1) == 0)
    def _():
        u_sc[...] = _modulate(x_ref[...], sh_ref[...], sc_ref[...]).astype(BF16)

    acc = jnp.dot(u_sc[...], w_ref[...], preferred_element_type=F32) + b_ref[...]
    o_ref[...] = acc.astype(o_ref.dtype)


def _proj(lay, x, mods, c_shift, c_scale, w, b, tn, out_dtype=BF16):
    tm = lay.tile(1024)
    n = w.shape[1]
    assert n % tn == 0
    return pl.pallas_call(
        _proj_kernel,
        out_shape=jax.ShapeDtypeStruct((lay.T, n), out_dtype),
        grid=(lay.T // tm, n // tn),
        in_specs=[pl.BlockSpec((tm, D), lambda i, j: (i, 0)),
                  _mod_spec(lay, tm, c_shift, 2), _mod_spec(lay, tm, c_scale, 2),
                  pl.BlockSpec((D, tn), lambda i, j: (0, j)),
                  pl.BlockSpec((1, tn), lambda i, j: (0, j))],
        out_specs=pl.BlockSpec((tm, tn), lambda i, j: (i, j)),
        scratch_shapes=[pltpu.VMEM((tm, D), BF16)],
        compiler_params=_cp("arbitrary", "arbitrary"),
        name="proj",
    )(x, mods, mods, w, b)


def _ffn_kernel(x_ref, sh_ref, sc_ref, g_ref, wa_ref, wb_ref, wo_ref, o_ref):
    x = x_ref[...]
    u = _modulate(x, sh_ref[...], sc_ref[...]).astype(BF16)
    a = jnp.dot(u, wa_ref[...], preferred_element_type=F32)
    b = jnp.dot(u, wb_ref[...], preferred_element_type=F32)
    h = (_silu(a) * b).astype(BF16)
    o_ref[...] = x + g_ref[...] * jnp.dot(h, wo_ref[...], preferred_element_type=F32)


def _ffn(lay, x, mods, w_in, w_out):
    tm = lay.tile(512)
    once = dict(pipeline_mode=pl.Buffered(1))
    return pl.pallas_call(
        _ffn_kernel,
        out_shape=jax.ShapeDtypeStruct((lay.T, D), F32),
        grid=(lay.T // tm,),
        in_specs=[pl.BlockSpec((tm, D), lambda i: (i, 0)),
                  _mod_spec(lay, tm, 3, 1), _mod_spec(lay, tm, 4, 1), _mod_spec(lay, tm, 5, 1),
                  pl.BlockSpec((D, FFN_DIM), lambda i: (0, 0), **once),
                  pl.BlockSpec((D, FFN_DIM), lambda i: (0, 1), **once),
                  pl.BlockSpec((FFN_DIM, D), lambda i: (0, 0), **once)],
        out_specs=pl.BlockSpec((tm, D), lambda i: (i, 0)),
        compiler_params=_cp("arbitrary"),
        name="ffn",
    )(x, mods, mods, mods, w_in, w_in, w_out)


def _s5_tables(a_re, a_im, log_dt, b_re, b_im, c_re, c_im, d_skip):
    T, G, P, Q = S5_T, S5_G, S5_P, S5_Q
    a = lax.complex(a_re, a_im)
    adt = a * jnp.exp(log_dt)[..., None]
    lam = jnp.exp(adt)
    bb = ((lam - 1.0) / a)[..., None] * lax.complex(b_re, b_im)
    cm = lax.complex(c_re, c_im)
    steps = jnp.arange(T + 1, dtype=F32)
    pw = jnp.exp(steps[None, :, None, None] * adt[:, None])
    kern = jnp.real(jnp.einsum('dgqp,djgp,dgpr->djgqr', cm, pw[:, :T], bb))
    lag = jnp.arange(T)[:, None, None]
    s_i = jnp.arange(T)[None, :, None]
    t_i = jnp.arange(T)[None, None, :]
    place = jnp.stack([t_i - s_i == lag, s_i - t_i == lag]).astype(F32)
    m = jnp.einsum('djst,djgqr->gsrtq', place, kern, precision=HIGHEST)
    eye = (jnp.eye(T)[:, None, :, None] * jnp.eye(Q)[None, :, None, :])
    m = m + eye[None] * d_skip.reshape(G, 1, 1, 1, Q)
    m = m.reshape(G, T * Q, T * Q)
    e_f = pw[0][T - 1 - jnp.arange(T)]
    e_b = pw[1][jnp.arange(T)]
    n_f = e_f[..., None] * bb[0][None]
    n_b = e_b[..., None] * bb[1][None]
    n_c = jnp.concatenate([n_f, n_b], axis=2)
    n_c = jnp.transpose(n_c, (1, 0, 3, 2)).reshape(G, T * Q, 2 * P)
    lam_t = jnp.concatenate([pw[0][T], pw[1][T]], axis=-1)
    w_f = cm[0][:, None] * jnp.transpose(pw[0][1:T + 1], (1, 0, 2))[:, :, None, :]
    w_b = cm[1][:, None] * jnp.transpose(pw[1][T - jnp.arange(T)], (1, 0, 2))[:, :, None, :]
    w_f = jnp.transpose(w_f, (0, 3, 1, 2)).reshape(G, P, T * Q)
    w_b = jnp.transpose(w_b, (0, 3, 1, 2)).reshape(G, P, T * Q)
    z = jnp.zeros_like(jnp.real(w_f))
    c_mats = dict(c_f_re=jnp.concatenate([jnp.real(w_f), z], axis=1),
                  c_f_im=jnp.concatenate([-jnp.imag(w_f), z], axis=1),
                  c_b_re=jnp.concatenate([z, jnp.real(w_b)], axis=1),
                  c_b_im=jnp.concatenate([z, -jnp.imag(w_b)], axis=1))
    return dict(m=m.astype(BF16), n_re=jnp.real(n_c).astype(BF16), n_im=jnp.imag(n_c).astype(BF16),
                l_re=jnp.real(lam_t), l_im=jnp.imag(lam_t), **{k: v.astype(BF16) for k, v in c_mats.items()})


def _s5_in_kernel(u_ref, m_ref, nre_ref, nim_ref, yi_ref, sre_ref, sim_ref):
    u = u_ref[...]
    yi_ref[...] = jnp.dot(u, m_ref[...], preferred_element_type=F32)
    sre_ref[...] = jnp.dot(u, nre_ref[...], preferred_element_type=F32).reshape(sre_ref.shape)
    sim_ref[...] = jnp.dot(u, nim_ref[...], preferred_element_type=F32).reshape(sim_ref.shape)


def _s5_in(ug, tabs):
    G, R, W = ug.shape
    P2 = 2 * S5_P
    RB = S5_SCAN_ROWS
    assert R % RB == 0
    gspec = lambda n: pl.BlockSpec((None, W, n), lambda g: (g, 0, 0))
    rspec = pl.BlockSpec((None, R, W), lambda g: (g, 0, 0))
    sspec = pl.BlockSpec((R // RB, RB, P2), lambda g: (0, g, 0))
    sshape = jax.ShapeDtypeStruct((R // RB, G * RB, P2), F32)
    return pl.pallas_call(
        _s5_in_kernel,
        out_shape=(jax.ShapeDtypeStruct((G, R, W), F32), sshape, sshape),
        grid=(G,),
        in_specs=[rspec, gspec(W), gspec(P2), gspec(P2)],
        out_specs=(rspec, sspec, sspec),
        compiler_params=_cp("arbitrary"),
        name="s5_in",
    )(ug, tabs['m'], tabs['n_re'], tabs['n_im'])


def _s5_scan_kernel(nsb, ncb, nblk, sref_ref, simf_ref, sreb_ref, simb_ref, lre_ref, lim_ref,
                    h0re_ref, h0im_ref, *rest):
    hfre_ref, hfim_ref, hbre_ref, hbim_ref, fre_ref, fim_ref, cre_sc, cim_sc = rest[4:]
    P = S5_P
    rows = sref_ref.shape[0] // ncb
    j = pl.program_id(1)
    fwd = lax.broadcasted_iota(jnp.int32, (1, 2 * P), 1) < P
    lre = lre_ref[...]
    lim = lim_ref[...]

    @pl.when(j == 0)
    def _():
        cre_sc[...] = h0re_ref[...]
        cim_sc[...] = h0im_ref[...]

    def at(k):
        return pl.ds(k, rows, stride=ncb)

    def body(k, carry):
        hre, him = carry
        kb = ncb - 1 - k
        hfre_ref[at(k), :] = hre
        hfim_ref[at(k), :] = him
        hbre_ref[at(kb), :] = hre
        hbim_ref[at(kb), :] = him
        sre = jnp.where(fwd, sref_ref[at(k), :], sreb_ref[at(kb), :])
        sim = jnp.where(fwd, simf_ref[at(k), :], simb_ref[at(kb), :])
        return (lre * hre - lim * him + sre, lre * him + lim * hre + sim)

    hre, him = lax.fori_loop(0, ncb, body, (cre_sc[...], cim_sc[...]), unroll=4)
    cre_sc[...] = hre
    cim_sc[...] = him

    @pl.when(j == nblk - 1)
    def _():
        fre_ref[...] = hre
        fim_ref[...] = him


def _s5_scan(sre, sim, tabs, h0re, h0im, hprev, row0, nseq, nc, nsb, nblk):
    _, grb, P2 = sre.shape
    rb = S5_SCAN_ROWS
    G = grb // rb
    assert nblk == 1 or nsb == 1
    ncb = nc // nblk
    assert nsb * ncb == rb and row0 % rb == 0 and nseq % nsb == 0 and nc % nblk == 0
    b0 = row0 // rb
    fspec = pl.BlockSpec((None, grb, P2), lambda i, j: (b0 + i * nblk + j, 0, 0))
    bspec = pl.BlockSpec((None, grb, P2), lambda i, j: (b0 + i * nblk + nblk - 1 - j, 0, 0))
    lspec = pl.BlockSpec((G * nsb, P2), lambda i, j: (0, 0))
    qspec = pl.BlockSpec((None, G * nsb, P2), lambda i, j: (i, 0, 0))
    anyspec = pl.BlockSpec(memory_space=pl.ANY)
    fin = jax.ShapeDtypeStruct((nseq // nsb, G * nsb, P2), F32)
    rep = lambda a: jnp.repeat(a, nsb, axis=0)
    flat = lambda a: a.reshape(nseq // nsb, G * nsb, P2)
    outs = pl.pallas_call(
        functools.partial(_s5_scan_kernel, nsb, ncb, nblk),
        out_shape=tuple(jax.ShapeDtypeStruct(h.shape, h.dtype) for h in hprev) + (fin, fin),
        grid=(nseq // nsb, nblk),
        in_specs=[fspec, fspec, bspec, bspec, lspec, lspec, qspec, qspec] + [anyspec] * 4,
        out_specs=(fspec, fspec, bspec, bspec, qspec, qspec),
        scratch_shapes=[pltpu.VMEM((G * nsb, P2), F32), pltpu.VMEM((G * nsb, P2), F32)],
        input_output_aliases={8: 0, 9: 1, 10: 2, 11: 3},
        compiler_params=_cp("arbitrary", "arbitrary"),
        name="s5_scan",
    )(sre, sim, sre, sim, rep(tabs['l_re']), rep(tabs['l_im']), flat(h0re), flat(h0im), *hprev)
    return outs[:4], outs[4].reshape(nseq // nsb, G, nsb, P2), outs[5].reshape(nseq // nsb, G, nsb, P2)


def _s5_out_kernel(yi_ref, hfre_ref, hfim_ref, hbre_ref, hbim_ref, cfre_ref, cfim_ref, cbre_ref, cbim_ref,
                   y_ref):
    y = yi_ref[...]
    for h_ref, c_ref in ((hfre_ref, cfre_ref), (hfim_ref, cfim_ref), (hbre_ref, cbre_ref), (hbim_ref, cbim_ref)):
        h = h_ref[...].reshape(y.shape[0], h_ref.shape[-1])
        y += jnp.dot(h.astype(BF16), c_ref[...], preferred_element_type=F32)
    y_ref[...] = y.astype(y_ref.dtype)


def _s5_out(yi, hprev, tabs):
    G, R, W = yi.shape
    P2 = 2 * S5_P
    RB = S5_SCAN_ROWS
    gspec = pl.BlockSpec((None, P2, W), lambda g: (g, 0, 0))
    hspec = pl.BlockSpec((R // RB, RB, P2), lambda g: (0, g, 0))
    rspec = pl.BlockSpec((None, R, W), lambda g: (g, 0, 0))
    return pl.pallas_call(
        _s5_out_kernel,
        out_shape=jax.ShapeDtypeStruct((G, R, W), F32),
        grid=(G,),
        in_specs=[rspec] + [hspec] * 4 + [gspec] * 4,
        out_specs=rspec,
        compiler_params=_cp("arbitrary"),
        name="s5_out",
    )(yi, *hprev, tabs['c_f_re'], tabs['c_f_im'], tabs['c_b_re'], tabs['c_b_im'])


LANES = 128
S5_GB = LANES // S5_Q


def _block_transpose(sets):
    blk = lax.broadcasted_iota(jnp.int32, sets[0][0].shape, 1) // S5_Q
    sets = [list(regs) for regs in sets]
    d = S5_GB // 2
    while d:
        keep = (blk & d) == 0
        for regs in sets:
            for i in range(S5_GB):
                if i & d:
                    continue
                a, b = regs[i], regs[i + d]
                regs[i] = jnp.where(keep, a, pltpu.roll(b, d * S5_Q, 1))
                regs[i + d] = jnp.where(keep, pltpu.roll(a, LANES - d * S5_Q, 1), b)
        d //= 2
    return sets


def _s5_pre_kernel(tm, x_ref, sh_ref, sc_ref, ug_ref, u_sc):
    u = _modulate(x_ref[...], sh_ref[...], sc_ref[...])
    for j in range(D // LANES):
        u_sc[j] = u[:, j * LANES:(j + 1) * LANES]
    rows16 = 16
    nh = S5_T // S5_GB
    for c in range(tm // (S5_T * rows16)):
        base = c * S5_T * rows16
        for j in range(D // LANES):
            sets = [[u_sc[j, pl.ds(base + h * S5_GB + s, rows16, stride=S5_T), :] for s in range(S5_GB)]
                    for h in range(nh)]
            for h, regs in enumerate(_block_transpose(sets)):
                for gl, t in enumerate(regs):
                    ug_ref[j * S5_GB + gl, c * rows16:(c + 1) * rows16, h * LANES:(h + 1) * LANES] = t.astype(BF16)


def _s5_pre(lay, x, mods):
    tm = lay.tile(512)
    assert tm % (S5_T * 16) == 0
    return pl.pallas_call(
        functools.partial(_s5_pre_kernel, tm),
        out_shape=jax.ShapeDtypeStruct((S5_G, lay.T // S5_T, S5_T * S5_Q), BF16),
        grid=(lay.T // tm,),
        in_specs=[pl.BlockSpec((tm, D), lambda i: (i, 0)), _mod_spec(lay, tm, 0, 1), _mod_spec(lay, tm, 1, 1)],
        out_specs=pl.BlockSpec((S5_G, tm // S5_T, S5_T * S5_Q), lambda i: (0, i, 0)),
        scratch_shapes=[pltpu.VMEM((D // LANES, tm, LANES), F32)],
        compiler_params=_cp("arbitrary"),
        name="s5_pre",
    )(x, mods, mods)


def _s5_glu_kernel(tm, x_ref, yg_ref, g_ref, wv_ref, wg_ref, o_ref, a_sc, y_sc):
    @pl.when(pl.program_id(1) == 0)
    def _():
        def sub_tile(c, carry):
            crow = pl.ds(pl.multiple_of(c * 8, 8), 8)
            base = c * (S5_T * 8)
            nh = S5_T // S5_GB
            for j2 in range(0, D // LANES, 2):
                keys = [(j, h) for j in (j2, j2 + 1) for h in range(nh)]
                sets = [[yg_ref[j * S5_GB + gl, crow, h * LANES:(h + 1) * LANES] for gl in range(S5_GB)]
                        for j, h in keys]
                for (j, h), regs in zip(keys, _block_transpose(sets)):
                    for s, t in enumerate(regs):
                        y_sc[j, pl.ds(base + h * S5_GB + s, 8, stride=S5_T), :] = t
            return carry

        lax.fori_loop(0, tm // (S5_T * 8), sub_tile, 0)
        for j in range(D // LANES):
            a_sc[:, j * LANES:(j + 1) * LANES] = jax.nn.gelu(y_sc[j]).astype(BF16)

    a = a_sc[...]
    val = jnp.dot(a, wv_ref[...], preferred_element_type=F32)
    gate = jnp.dot(a, wg_ref[...], preferred_element_type=F32)
    o_ref[...] = x_ref[...] + g_ref[...] * (val * jax.nn.sigmoid(gate))


def _s5_glu(lay, x, yg, mods, glu_w):
    tm = lay.tile(1024)
    assert tm % (S5_T * 8) == 0
    tn = 512
    nn = D // tn
    return pl.pallas_call(
        functools.partial(_s5_glu_kernel, tm),
        out_shape=jax.ShapeDtypeStruct((lay.T, D), F32),
        grid=(lay.T // tm, nn),
        in_specs=[pl.BlockSpec((tm, tn), lambda i, j: (i, j)),
                  pl.BlockSpec((S5_G, tm // S5_T, S5_T * S5_Q), lambda i, j: (0, i, 0)),
                  pl.BlockSpec((None, 1, tn), lambda i, j: (lay.group(i * tm) * MOD_CHUNKS + 2, 0, j)),
                  pl.BlockSpec((D, tn), lambda i, j: (0, j)),
                  pl.BlockSpec((D, tn), lambda i, j: (0, nn + j))],
        out_specs=pl.BlockSpec((tm, tn), lambda i, j: (i, j)),
        scratch_shapes=[pltpu.VMEM((tm, D), BF16), pltpu.VMEM((D // LANES, tm, LANES), F32)],
        compiler_params=_cp("arbitrary", "arbitrary"),
        name="s5_glu",
    )(x, yg, mods, glu_w, glu_w)


def _s5_layer(lay, x, mods, p, h0_re, h0_im):
    T, G, P, Q = S5_T, S5_G, S5_P, S5_Q
    tabs = _s5_tables(p['a_re'], p['a_im'], p['log_dt'], p['b_re'], p['b_im'], p['c_re'], p['c_im'], p['d'])
    R = lay.T // T
    ug = _s5_pre(lay, x, mods)
    yi, sre, sim = _s5_in(ug, tabs)
    hprev = tuple(jnp.zeros(sre.shape, F32) for _ in range(4))
    ncp, ncs = lay.L // T, lay.LS // T
    nsb = S5_SCAN_ROWS // ncp
    zero = jnp.zeros((lay.B // nsb, G, nsb, 2 * P), F32)
    hprev, fre, fim = _s5_scan(sre, sim, tabs, zero, zero, hprev, 0, lay.B, ncp, nsb, 1)
    to_lanes = lambda s: jnp.transpose(s, (0, 2, 1, 3)).reshape(lay.NS, G, 1, 2 * P)
    hprev, _, _ = _s5_scan(sre, sim, tabs, to_lanes(h0_re), to_lanes(h0_im), hprev,
                           lay.TP // T, lay.NS, ncs, 1, max(1, ncs // S5_SCAN_ROWS))
    yg = _s5_out(yi, hprev, tabs)
    x = _s5_glu(lay, x, yg, mods, p['glu_w'])
    from_lanes = lambda s: jnp.transpose(s, (0, 2, 1, 3)).reshape(lay.B, G, 2, P).transpose(0, 2, 1, 3)
    return x, from_lanes(fre), from_lanes(fim)


def _hyena_filters(L, p):
    mm = functools.partial(jnp.matmul, precision=HIGHEST)
    f = jnp.linspace(1e-4, HY_BANDS - 1, HY_BANDS, dtype=F32)[None, :]
    max_decay = math.log(HY_TARGET) / HY_FAST_PCT
    min_decay = math.log(HY_TARGET) / HY_SLOW_PCT
    deltas = jnp.abs(jnp.linspace(min_decay, max_decay, D, dtype=F32))
    w3 = p['f_w3'].reshape(-1, 2, 2, D)

    def side(pos, s):
        t = (pos.astype(F32) / (L - 1))[:, None]
        w = 2.0 * math.pi * pos.astype(F32)[:, None] / L
        feats = jnp.concatenate([t, jnp.cos(f * w), -jnp.sin(f * w)], axis=-1)
        z = jnp.sin(p['f_freq'][0] * (mm(feats, p['f_w1']) + p['f_b1']))
        z = jnp.sin(p['f_freq'][1] * (mm(z, p['f_w2']) + p['f_b2']))
        win = jnp.exp(-t * deltas)
        return jnp.stack([mm(z, w3[:, o, s]) * win for o in range(2)])

    j = jnp.arange(L, dtype=jnp.int32)
    k_lo = side(j, 0)
    k_hi = side((L - j) % L, 1) * (j > 0).astype(F32)[None, :, None]
    norm = jnp.sum(jnp.abs(k_lo), axis=1, keepdims=True) + jnp.sum(jnp.abs(k_hi), axis=1, keepdims=True)
    k_lo, k_hi = k_lo / norm, k_hi / norm
    alt = (1.0 - 2.0 * (j % 2).astype(F32))[None, :, None]
    k_ny = jnp.sum(alt * (k_lo + k_hi), axis=1) / (2 * L)
    return k_lo, k_hi, k_ny


def _dft_tables(L):
    r = math.isqrt(L)
    assert r * r == L
    t = jnp.arange(L, dtype=jnp.int32)[None, :]
    a = jnp.arange(r, dtype=jnp.int32)[:, None]

    def unit(idx):
        ang = (idx % (2 * L)).astype(F32) * (math.pi / L)
        return jnp.cos(ang), jnp.sin(ang)

    c1, s1 = unit(a * r * t)
    c2, s2 = unit(a * t)
    c1, s1, c2, s2 = c1[:, None], s1[:, None], c2[None], s2[None]
    cos = (c1 * c2 - s1 * s2).reshape(L, L)
    sin = (s1 * c2 + c1 * s2).reshape(L, L)
    return cos.astype(BF16), sin.astype(BF16)


def _hy_spec_kernel(L, tr, c_ref, s_ref, klo_ref, khi_ref, p_ref, q_ref):
    r = pl.program_id(2)
    f = r * tr + lax.broadcasted_iota(jnp.int32, (tr, 1), 0)
    sgn = (1 - 2 * (f % 2)).astype(F32)
    scale = jnp.where(f == 0, 1.0, 2.0) * (1.0 / (2 * L))
    c, s = c_ref[...], s_ref[...]
    lo, hi = klo_ref[...], khi_ref[...]
    dot = functools.partial(jnp.dot, preferred_element_type=F32)
    p_ref[...] = scale * (dot(c, lo) + sgn * dot(c, hi))
    q_ref[...] = scale * (dot(s, lo) + sgn * dot(s, hi))


def _hy_spectrum(L, cos, sin, k_lo, k_hi):
    tr = min(L, 512)
    tc = 512
    kspec = pl.BlockSpec((None, L, tc), lambda o, j, r: (o, 0, j))
    tspec = pl.BlockSpec((tr, L), lambda o, j, r: (r, 0))
    ospec = pl.BlockSpec((None, tr, tc), lambda o, j, r: (o, r, j))
    return pl.pallas_call(
        functools.partial(_hy_spec_kernel, L, tr),
        out_shape=(jax.ShapeDtypeStruct((2, L, D), F32), jax.ShapeDtypeStruct((2, L, D), F32)),
        grid=(2, D // tc, L // tr),
        in_specs=[tspec, tspec, kspec, kspec],
        out_specs=(ospec, ospec),
        compiler_params=_cp("arbitrary", "arbitrary", "arbitrary"),
        name="hy_spectrum",
    )(cos, sin, k_lo.astype(BF16), k_hi.astype(BF16))


def _hy_core_kernel(L, tr, ngrp, tc, x1_ref, x2_ref, v_ref, sw1_ref, sw2_ref, swv_ref, sb1_ref, sb2_ref,
                    sbv_ref, c_ref, s_ref, p_ref, q_ref, kny_ref, skip_ref, *rest):
    o_ref, z0_sc, z1_sc, x2_sc, a_sc, b_sc, ny0_sc, ny1_sc = rest[-8:]
    fused = tr == L
    ph = pl.program_id(2)
    r = pl.program_id(3)
    W = ngrp * tc

    def phase(k):
        return (lambda f: f()) if fused else pl.when(ph == k)
    tcv = min(L, 512)
    halo = 16

    def alt_sign(start, n):
        t = start + lax.broadcasted_iota(jnp.int32, (n, 1), 0)
        return (1 - 2 * (t % 2)).astype(F32)

    def conv3(src_ref, g, a, w_ref, b_ref):
        x = src_ref[g, pl.ds(a, tcv), :].astype(F32)
        row = lax.broadcasted_iota(jnp.int32, (tcv, 1), 0)
        up_at = pl.multiple_of(jnp.maximum(a - halo, 0), halo)
        dn_at = pl.multiple_of(jnp.minimum(a + tcv, L - halo), halo)
        up = src_ref[g, pl.ds(up_at, halo), :][halo - 1:halo, :].astype(F32)
        dn = src_ref[g, pl.ds(dn_at, halo), :][0:1, :].astype(F32)
        up = jnp.where(a > 0, up, 0.0)
        dn = jnp.where(a + tcv < L, dn, 0.0)
        prev = jnp.where(row == 0, up, pltpu.roll(x, 1, 0))
        nxt = jnp.where(row == tcv - 1, dn, pltpu.roll(x, tcv - 1, 0))
        return prev * w_ref[0:1, :] + x * w_ref[1:2, :] + nxt * w_ref[2:3, :] + b_ref[...]

    @(phase(0) if fused else pl.when(jnp.logical_and(ph == 0, r == 0)))
    def _():
        ny0_sc[...] = jnp.zeros_like(ny0_sc)

        def conv_tile(ti, carry):
            a = pl.multiple_of(ti * tcv, tcv)
            rows_a = pl.ds(a, tcv)
            for g in range(ngrp):
                cols = slice(g * tc, (g + 1) * tc)
                z1_sc[rows_a, cols] = conv3(x1_ref, g, a, sw1_ref, sb1_ref).astype(BF16)
                x2_sc[rows_a, cols] = conv3(x2_ref, g, a, sw2_ref, sb2_ref).astype(BF16)
                z0_sc[rows_a, cols] = conv3(v_ref, g, a, swv_ref, sbv_ref).astype(BF16)
            ny0_sc[...] += jnp.sum(alt_sign(a, tcv) * z0_sc[rows_a, :].astype(F32), axis=0, keepdims=True)
            return carry

        lax.fori_loop(0, L // tcv, conv_tile, 0)

    start = pl.multiple_of(r * tr, tr)
    rows = pl.ds(start, tr)
    dot = functools.partial(jnp.dot, preferred_element_type=F32)

    def forward(order, z_sc):
        z = z_sc[...]
        zre = dot(c_ref[...], z)
        zim = dot(s_ref[...], z)
        pw = jnp.concatenate([p_ref[order] if fused else p_ref[...]] * ngrp, axis=1)
        qw = jnp.concatenate([q_ref[order] if fused else q_ref[...]] * ngrp, axis=1)
        a_sc[rows, :] = (zre * pw - zim * qw).astype(BF16)
        b_sc[rows, :] = (zim * pw + zre * qw).astype(BF16)

    def inverse(order, z_sc, ny_sc):
        y = dot(c_ref[...], a_sc[...]) + dot(s_ref[...], b_sc[...])
        kny = jnp.concatenate([kny_ref[order:order + 1, :]] * ngrp, axis=1)
        skip = jnp.concatenate([skip_ref[order:order + 1, :]] * ngrp, axis=1)
        return y + alt_sign(start, tr) * (ny_sc[...] * kny) + skip * z_sc[rows, :].astype(F32)

    @phase(0)
    def _():
        forward(0, z0_sc)

    @phase(1)
    def _():
        z1 = (z1_sc[rows, :].astype(F32) * inverse(0, z0_sc, ny0_sc)).astype(BF16)
        z1_sc[rows, :] = z1

        @pl.when(r == 0)
        def _():
            ny1_sc[...] = jnp.zeros_like(ny1_sc)

        ny1_sc[...] += jnp.sum(alt_sign(start, tr) * z1.astype(F32), axis=0, keepdims=True)

    @phase(2)
    def _():
        forward(1, z1_sc)

    @phase(3)
    def _():
        out = x2_sc[rows, :].astype(F32) * inverse(1, z1_sc, ny1_sc)
        for g in range(ngrp):
            o_ref[g, rows, :] = out[:, g * tc:(g + 1) * tc].astype(o_ref.dtype)


def _hy_core(proj, short_w, short_b, skip, cos, sin, pq, k_ny, o_prev, row0, nseq, L, ngrp, tc):
    T = proj.shape[0]
    tr = min(L, 256)
    nrt = L // tr
    assert row0 % (L * ngrp) == 0 and nseq % ngrp == 0 and T % L == 0
    sb0 = row0 // (L * ngrp)
    nct = D // tc
    p3 = proj.reshape(T // L, L, 3 * D)
    p_arr, q_arr = pq

    def xspec(part):
        mode = {} if nrt == 1 else dict(pipeline_mode=pl.Buffered(1))
        return pl.BlockSpec((ngrp, L, tc), lambda i, j, ph, r: (sb0 + i, 0, part * nct + j), **mode)

    def wspec(part, rows_):
        return pl.BlockSpec((rows_, tc), lambda i, j, ph, r: (0, part * nct + j))

    fused = nrt == 1
    if fused:
        pq_spec = pl.BlockSpec((2, tr, tc), lambda i, j, ph, r: (0, 0, j))
    else:
        pq_spec = pl.BlockSpec((None, tr, tc),
                               lambda i, j, ph, r: (ph // 2, jnp.where(ph % 2 == 0, r, nrt - 1), j))

    tspec = pl.BlockSpec((tr, L), lambda i, j, ph, r: (r, 0))
    in_specs = [xspec(0), xspec(1), xspec(2), wspec(0, 3), wspec(1, 3), wspec(2, 3),
                wspec(0, 1), wspec(1, 1), wspec(2, 1), tspec, tspec, pq_spec, pq_spec,
                pl.BlockSpec((2, tc), lambda i, j, ph, r: (0, j)),
                pl.BlockSpec((2, tc), lambda i, j, ph, r: (0, j))]
    sb = short_b.reshape(1, 3 * D)
    args = [p3, p3, p3, short_w, short_w, short_w, sb, sb, sb, cos, sin, p_arr, q_arr, k_ny, skip]
    aliases = {}
    if o_prev is not None:
        in_specs.append(pl.BlockSpec(memory_space=pl.ANY))
        args.append(o_prev.reshape(T // L, L, D))
        aliases = {len(args) - 1: 0}
    W = ngrp * tc
    out = pl.pallas_call(
        functools.partial(_hy_core_kernel, L, tr, ngrp, tc),
        out_shape=jax.ShapeDtypeStruct((T // L, L, D), BF16),
        grid=(nseq // ngrp, nct, 1 if fused else 4, nrt),
        in_specs=in_specs,
        out_specs=pl.BlockSpec((ngrp, L, tc), lambda i, j, ph, r: (sb0 + i, 0, j)),
        scratch_shapes=[pltpu.VMEM((L, W), BF16)] * 5 + [pltpu.VMEM((1, W), F32)] * 2,
        input_output_aliases=aliases,
        compiler_params=_cp("arbitrary", "arbitrary", "arbitrary", "arbitrary"),
        name="hy_core",
    )(*args)
    return out.reshape(T, D)


FOLD_BLK = 256


def _hyena_filters_folded(L, p):
    H = L // 2
    mm = functools.partial(jnp.matmul, precision=HIGHEST)
    f = jnp.linspace(1e-4, HY_BANDS - 1, HY_BANDS, dtype=F32)[None, :]
    max_decay = math.log(HY_TARGET) / HY_FAST_PCT
    min_decay = math.log(HY_TARGET) / HY_SLOW_PCT
    deltas = jnp.abs(jnp.linspace(min_decay, max_decay, D, dtype=F32))
    w3 = p['f_w3'].reshape(-1, 2, 2, D)

    def side(pos, s):
        t = (pos.astype(F32) / (L - 1))[:, None]
        w = 2.0 * math.pi * pos.astype(F32)[:, None] / L
        feats = jnp.concatenate([t, jnp.cos(f * w), -jnp.sin(f * w)], axis=-1)
        z = jnp.sin(p['f_freq'][0] * (mm(feats, p['f_w1']) + p['f_b1']))
        z = jnp.sin(p['f_freq'][1] * (mm(z, p['f_w2']) + p['f_b2']))
        win = jnp.exp(-t * deltas)
        return jnp.stack([mm(z, w3[:, o, s]) * win for o in range(2)])

    t = jnp.arange(H, dtype=jnp.int32)
    pos = jnp.concatenate([t, (L - t) % L])
    mid = jnp.full((1,), H, jnp.int32)
    live = (t > 0).astype(F32)[None, :, None]
    s0 = side(pos, 0).reshape(2, 2, H, D)
    s1 = side(pos, 1).reshape(2, 2, H, D)
    klo_lo, klo_hr, klo_h = s0[:, 0], s0[:, 1] * live, side(mid, 0)[:, 0]
    khi_lo, khi_hr, khi_h = s1[:, 1] * live, s1[:, 0] * live, side(mid, 1)[:, 0]
    norm = sum(jnp.sum(jnp.abs(a), axis=1) for a in (klo_lo, klo_hr, khi_lo, khi_hr)) \
        + jnp.abs(klo_h) + jnp.abs(khi_h)
    alt = (1.0 - 2.0 * (t % 2).astype(F32))[None, :, None]
    alt_h = 1.0 - 2.0 * (H % 2)
    k_ny = (jnp.sum(alt * (klo_lo + klo_hr + khi_lo + khi_hr), axis=1) + alt_h * (klo_h + khi_h)) / norm / (2 * L)
    inv = (1.0 / norm)[:, None, :]
    p_lo, p_hr = (klo_lo + khi_lo) * inv, (klo_hr + khi_hr) * inv
    m_lo, m_hr = (klo_lo - khi_lo) * inv, (klo_hr - khi_hr) * inv
    xc = jnp.stack([p_lo + p_hr, m_lo - m_hr])
    xs = jnp.stack([p_lo - p_hr, m_lo + m_hr])
    xh = jnp.stack([(klo_h + khi_h) / norm, (klo_h - khi_h) / norm])
    return xc, xs, xh, k_ny


def _dft_tables_folded(L):
    H = L // 2
    ra = 1 << (int(math.log2(H)) // 2)
    rb = H // ra
    th = math.pi / L
    a = jnp.arange(ra, dtype=jnp.int32)[:, None]
    b = jnp.arange(rb, dtype=jnp.int32)[:, None]
    u = jnp.arange(H, dtype=jnp.int32)[None, :]

    def unit(idx):
        ang = (idx % (2 * L)).astype(F32) * th
        return jnp.cos(ang), jnp.sin(ang)

    e1 = [unit(2 * rb * a * u), unit(2 * rb * a * u), unit((2 * u + 1) * rb * a)]
    e2 = [unit(2 * b * u), unit((2 * b + 1) * u), unit((2 * u + 1) * b)]
    c1 = jnp.stack([e[0] for e in e1])[:, None, :, None, :]
    s1 = jnp.stack([e[1] for e in e1])[:, None, :, None, :]
    c2 = jnp.stack([e[0] for e in e2])[:, None, None, :, :]
    s2 = jnp.stack([e[1] for e in e2])[:, None, None, :, :]
    lead = jnp.concatenate([c1, s1], axis=1)
    cross = jnp.concatenate([-s1, c1], axis=1)
    return (lead * c2 + cross * s2).reshape(6, H, H).astype(BF16)


def _hy_spec_folded_kernel(L, tr, c_ref, s_ref, xc_ref, xs_ref, xh_ref, p_ref, q_ref):
    half = pl.program_id(2)
    r = pl.program_id(3)
    m = r * tr + lax.broadcasted_iota(jnp.int32, (tr, 1), 0)
    alt = (1 - 2 * (m % 2)).astype(F32)
    scale = jnp.where(jnp.logical_and(half == 0, m == 0), 1.0, 2.0) * (1.0 / (2 * L))
    mid = alt * xh_ref[...]
    kc = jnp.dot(c_ref[...], xc_ref[...], preferred_element_type=F32) + jnp.where(half == 0, mid, 0.0)
    ks = jnp.dot(s_ref[...], xs_ref[...], preferred_element_type=F32) + jnp.where(half == 1, mid, 0.0)
    p_ref[...] = scale * kc
    q_ref[...] = scale * ks


def _hy_spectrum_folded(L, tabs, xc, xs, xh):
    H = L // 2
    tr = min(H, 512)
    tc = 512
    nrt = H // tr
    xspec = pl.BlockSpec((None, None, H, tc), lambda o, j, hf, r: (hf, o, 0, j))
    hspec = pl.BlockSpec((None, None, 1, tc), lambda o, j, hf, r: (hf, o, 0, j))
    cspec = pl.BlockSpec((None, tr, H), lambda o, j, hf, r: (2 * hf, r, 0))
    sspec = pl.BlockSpec((None, tr, H), lambda o, j, hf, r: (2 * hf + 1, r, 0))
    ospec = pl.BlockSpec((None, tr, tc), lambda o, j, hf, r: (o, hf * nrt + r, j))
    return pl.pallas_call(
        functools.partial(_hy_spec_folded_kernel, L, tr),
        out_shape=(jax.ShapeDtypeStruct((2, L, D), F32), jax.ShapeDtypeStruct((2, L, D), F32)),
        grid=(2, D // tc, 2, nrt),
        in_specs=[cspec, sspec, xspec, xspec, hspec],
        out_specs=(ospec, ospec),
        compiler_params=_cp("arbitrary", "arbitrary", "arbitrary", "arbitrary"),
        name="hy_spectrum",
    )(tabs, tabs, xc.astype(BF16), xs.astype(BF16), xh[:, :, None, :])


def _hy_fold_kernel(L, tr, ngrp, tc, x1_ref, x2_ref, v_ref, sw1_ref, sw2_ref, swv_ref, sb1_ref, sb2_ref,
                    sbv_ref, ce_ref, se_ref, co_ref, so_ref, pe_ref, po_ref, qe_ref, qo_ref, kny_ref, skip_ref,
                    *rest):
    o_ref, z0_sc, z1_sc, x2_sc, a_sc, b_sc, mid_sc = rest[-7:]
    H = L // 2
    nrt = H // tr
    nb = H // FOLD_BLK
    W = ngrp * tc
    ph = pl.program_id(2)
    r = pl.program_id(3)
    tcv = min(L, 512)
    halo = 16
    dot = functools.partial(jnp.dot, preferred_element_type=F32)
    ZH, X1H, X2H, YSP, NY, NYACC = range(6)
    alt_h = 1.0 - 2.0 * (H % 2)

    def alt_sign(start, n):
        t = start + lax.broadcasted_iota(jnp.int32, (n, 1), 0)
        return (1 - 2 * (t % 2)).astype(F32)

    def conv3(src_ref, g, a, w_ref, b_ref):
        x = src_ref[g, pl.ds(a, tcv), :].astype(F32)
        row = lax.broadcasted_iota(jnp.int32, (tcv, 1), 0)
        up_at = pl.multiple_of(jnp.maximum(a - halo, 0), halo)
        dn_at = pl.multiple_of(jnp.minimum(a + tcv, L - halo), halo)
        up = src_ref[g, pl.ds(up_at, halo), :][halo - 1:halo, :].astype(F32)
        dn = src_ref[g, pl.ds(dn_at, halo), :][0:1, :].astype(F32)
        up = jnp.where(a > 0, up, 0.0)
        dn = jnp.where(a + tcv < L, dn, 0.0)
        prev = jnp.where(row == 0, up, pltpu.roll(x, 1, 0))
        nxt = jnp.where(row == tcv - 1, dn, pltpu.roll(x, tcv - 1, 0))
        return prev * w_ref[0:1, :] + x * w_ref[1:2, :] + nxt * w_ref[2:3, :] + b_ref[...]

    def flip_mats():
        u = lax.broadcasted_iota(jnp.int32, (FOLD_BLK, FOLD_BLK), 0)
        v = lax.broadcasted_iota(jnp.int32, (FOLD_BLK, FOLD_BLK), 1)
        jshift = jnp.where(jnp.logical_and(u >= 1, v == FOLD_BLK - u), 1.0, 0.0).astype(BF16)
        e0 = jnp.where(jnp.logical_and(u == 0, v == 0), 1.0, 0.0).astype(BF16)
        return jshift, e0

    def flipped_block(src, b, jshift, e0):
        blk = lambda k: src[H + k * FOLD_BLK:H + (k + 1) * FOLD_BLK, :]
        out = dot(jshift, blk(nb - 1 - b))
        if b >= 1:
            out = out + dot(e0, blk(nb - b))
        return out

    def fold_in_place(sc, stage, off, jshift, e0):
        for b in range(nb):
            stage[off + b * FOLD_BLK:off + (b + 1) * FOLD_BLK, :] = flipped_block(sc, b, jshift, e0).astype(BF16)

        def copy_back(b, carry):
            at = pl.multiple_of(b * FOLD_BLK, FOLD_BLK)
            sc[pl.ds(H + at, FOLD_BLK), :] = stage[pl.ds(off + at, FOLD_BLK), :]
            return carry

        lax.fori_loop(0, nb, copy_back, 0)

    def to_operands(sc):
        def tile(i, acc):
            a = pl.multiple_of(i * FOLD_BLK, FOLD_BLK)
            ra, rb = pl.ds(a, FOLD_BLK), pl.ds(pl.multiple_of(H + a, FOLD_BLK), FOLD_BLK)
            lo = sc[ra, :].astype(F32)
            hr = sc[rb, :].astype(F32)
            zs = (lo + hr).astype(BF16)
            sc[ra, :] = zs
            sc[rb, :] = (lo - hr).astype(BF16)
            return acc + jnp.sum(alt_sign(a, FOLD_BLK) * zs.astype(F32), axis=0, keepdims=True)

        return lax.fori_loop(0, nb, tile, jnp.zeros((1, W), F32))

    def row_of(sc, at):
        return sc[at:at + halo, :][0:1, :].astype(F32)

    @pl.when(jnp.logical_and(ph == 0, r == 0))
    def _():
        def conv_tile(ti, carry):
            a = pl.multiple_of(ti * tcv, tcv)
            rows_a = pl.ds(a, tcv)
            for g in range(ngrp):
                cols = slice(g * tc, (g + 1) * tc)
                z1_sc[rows_a, cols] = conv3(x1_ref, g, a, sw1_ref, sb1_ref).astype(BF16)
                x2_sc[rows_a, cols] = conv3(x2_ref, g, a, sw2_ref, sb2_ref).astype(BF16)
                z0_sc[rows_a, cols] = conv3(v_ref, g, a, swv_ref, sbv_ref).astype(BF16)
            return carry

        lax.fori_loop(0, L // tcv, conv_tile, 0)
        mid_sc[ZH:ZH + 1, :] = row_of(z0_sc, H)
        mid_sc[X1H:X1H + 1, :] = row_of(z1_sc, H)
        mid_sc[X2H:X2H + 1, :] = row_of(x2_sc, H)
        jshift, e0 = flip_mats()
        for sc, stage, off in ((z0_sc, a_sc, 0), (z1_sc, a_sc, H), (x2_sc, b_sc, 0)):
            fold_in_place(sc, stage, off, jshift, e0)
        mid_sc[NY:NY + 1, :] = to_operands(z0_sc) + alt_h * mid_sc[ZH:ZH + 1, :]
        mid_sc[YSP:YSP + 1, :] = jnp.zeros((1, W), F32)
        mid_sc[NYACC:NYACC + 1, :] = jnp.zeros((1, W), F32)

    start = pl.multiple_of(r * tr, tr)
    rows = pl.ds(start, tr)
    rows_hi = pl.ds(pl.multiple_of(H + r * tr, tr), tr)
    rep = lambda ref: jnp.concatenate([ref[...]] * ngrp, axis=1)
    rep_row = lambda ref, o: jnp.concatenate([ref[o:o + 1, :]] * ngrp, axis=1)

    def forward(z_sc):
        zs, zd = z_sc[0:H, :], z_sc[H:L, :]
        mid = alt_sign(start, tr) * mid_sc[ZH:ZH + 1, :]
        zre_e = dot(ce_ref[...], zs) + mid
        zim_e = dot(se_ref[...], zd)
        zre_o = dot(co_ref[...], zd)
        zim_o = dot(so_ref[...], zs) + mid
        pe, qe, po, qo = rep(pe_ref), rep(qe_ref), rep(po_ref), rep(qo_ref)
        a_e = zre_e * pe - zim_e * qe
        b_e = zim_e * pe + zre_e * qe
        a_o = zre_o * po - zim_o * qo
        b_o = zim_o * po + zre_o * qo
        a_sc[rows, :] = a_e.astype(BF16)
        b_sc[rows, :] = b_e.astype(BF16)
        a_sc[rows_hi, :] = a_o.astype(BF16)
        b_sc[rows_hi, :] = b_o.astype(BF16)
        mid_sc[YSP:YSP + 1, :] += jnp.sum(alt_sign(start, tr) * (a_e + b_o), axis=0, keepdims=True)

    def inverse(order, zin_sc):
        pe = dot(ce_ref[...], a_sc[0:H, :])
        qe = dot(se_ref[...], b_sc[0:H, :])
        po = dot(co_ref[...], a_sc[H:L, :])
        qo = dot(so_ref[...], b_sc[H:L, :])
        c = alt_sign(start, tr) * (mid_sc[NY:NY + 1, :] * rep_row(kny_ref, order))
        half_skip = 0.5 * rep_row(skip_ref, order)
        zs = zin_sc[rows, :].astype(F32)
        zd = zin_sc[rows_hi, :].astype(F32)
        y_lo = pe + qe + po + qo + c + half_skip * (zs + zd)
        y_hr = pe - qe - po + qo + c + half_skip * (zs - zd)
        return y_lo, y_hr

    def middle(order):
        return (mid_sc[YSP:YSP + 1, :] + alt_h * mid_sc[NY:NY + 1, :] * rep_row(kny_ref, order)
                + rep_row(skip_ref, order) * mid_sc[ZH:ZH + 1, :])

    @pl.when(ph == 0)
    def _():
        forward(z0_sc)

    @pl.when(ph == 1)
    def _():
        y_lo, y_hr = inverse(0, z0_sc)
        z_lo = (z1_sc[rows, :].astype(F32) * y_lo).astype(BF16)
        z_hr = (z1_sc[rows_hi, :].astype(F32) * y_hr).astype(BF16)
        z1_sc[rows, :] = z_lo
        z1_sc[rows_hi, :] = z_hr
        mid_sc[NYACC:NYACC + 1, :] += jnp.sum(alt_sign(start, tr) * (z_lo.astype(F32) + z_hr.astype(F32)),
                                              axis=0, keepdims=True)

        @pl.when(r == nrt - 1)
        def _():
            z1h = (mid_sc[X1H:X1H + 1, :] * middle(0)).astype(BF16).astype(F32)
            mid_sc[ZH:ZH + 1, :] = z1h
            mid_sc[NY:NY + 1, :] = mid_sc[NYACC:NYACC + 1, :] + alt_h * z1h
            mid_sc[YSP:YSP + 1, :] = jnp.zeros((1, W), F32)

    @pl.when(ph == 2)
    def _():
        @pl.when(r == 0)
        def _():
            to_operands(z1_sc)

        forward(z1_sc)

    @pl.when(ph == 3)
    def _():
        y_lo, y_hr = inverse(1, z1_sc)
        z0_sc[rows, :] = (x2_sc[rows, :].astype(F32) * y_lo).astype(BF16)
        z0_sc[rows_hi, :] = (x2_sc[rows_hi, :].astype(F32) * y_hr).astype(BF16)

        @pl.when(r == nrt - 1)
        def _():
            out_h = mid_sc[X2H:X2H + 1, :] * middle(1)
            jshift, e0 = flip_mats()
            first = (lax.broadcasted_iota(jnp.int32, (FOLD_BLK, 1), 0) == 0).astype(F32)
            for b in range(nb):
                up = flipped_block(z0_sc, b, jshift, e0)
                if b == 0:
                    up = up + first * out_h
                for g in range(ngrp):
                    cols = slice(g * tc, (g + 1) * tc)
                    o_ref[g, H + b * FOLD_BLK:H + (b + 1) * FOLD_BLK, :] = up[:, cols].astype(o_ref.dtype)
            for b in range(nb):
                lo_rows = slice(b * FOLD_BLK, (b + 1) * FOLD_BLK)
                for g in range(ngrp):
                    o_ref[g, lo_rows, :] = z0_sc[lo_rows, g * tc:(g + 1) * tc]


def _hy_core_folded(proj, short_w, short_b, skip, tabs, pq, k_ny, o_prev, row0, nseq, L, ngrp, tc):
    T = proj.shape[0]
    H = L // 2
    tr = min(H, 256)
    nrt = H // tr
    assert H % FOLD_BLK == 0 and row0 % (L * ngrp) == 0 and nseq % ngrp == 0 and T % L == 0
    sb0 = row0 // (L * ngrp)
    nct = D // tc
    p3 = proj.reshape(T // L, L, 3 * D)
    p_arr, q_arr = pq
    once = dict(pipeline_mode=pl.Buffered(1))

    def xspec(part):
        return pl.BlockSpec((ngrp, L, tc), lambda i, j, ph, r: (sb0 + i, 0, part * nct + j), **once)

    def wspec(part, rows_):
        return pl.BlockSpec((rows_, tc), lambda i, j, ph, r: (0, part * nct + j))

    def tspec(k_fwd, k_inv):
        return pl.BlockSpec((None, tr, H), lambda i, j, ph, r: (jnp.where(ph % 2 == 0, k_fwd, k_inv), r, 0))

    def pqspec(half):
        def imap(i, j, ph, r):
            return (ph // 2, half * nrt + jnp.where(ph % 2 == 0, r, nrt - 1), j)
        return pl.BlockSpec((None, tr, tc), imap)

    in_specs = [xspec(0), xspec(1), xspec(2), wspec(0, 3), wspec(1, 3), wspec(2, 3),
                wspec(0, 1), wspec(1, 1), wspec(2, 1),
                tspec(0, 0), tspec(1, 1), tspec(2, 4), tspec(3, 5),
                pqspec(0), pqspec(1), pqspec(0), pqspec(1),
                pl.BlockSpec((2, tc), lambda i, j, ph, r: (0, j)),
                pl.BlockSpec((2, tc), lambda i, j, ph, r: (0, j))]
    sb = short_b.reshape(1, 3 * D)
    args = [p3, p3, p3, short_w, short_w, short_w, sb, sb, sb, tabs, tabs, tabs, tabs,
            p_arr, p_arr, q_arr, q_arr, k_ny, skip]
    in_specs.append(pl.BlockSpec(memory_space=pl.ANY))
    args.append(o_prev.reshape(T // L, L, D))
    aliases = {len(args) - 1: 0}
    W = ngrp * tc
    out = pl.pallas_call(
        functools.partial(_hy_fold_kernel, L, tr, ngrp, tc),
        out_shape=jax.ShapeDtypeStruct((T // L, L, D), BF16),
        grid=(nseq // ngrp, nct, 4, nrt),
        in_specs=in_specs,
        out_specs=pl.BlockSpec((ngrp, L, tc), lambda i, j, ph, r: (sb0 + i, 0, j)),
        scratch_shapes=[pltpu.VMEM((L, W), BF16)] * 5 + [pltpu.VMEM((8, W), F32)],
        input_output_aliases=aliases,
        compiler_params=_cp("arbitrary", "arbitrary", "arbitrary", "arbitrary"),
        name="hy_core",
    )(*args)
    return out.reshape(T, D)


def _plain_out_kernel(x_ref, z_ref, g_ref, w_ref, b_ref, o_ref):
    acc = jnp.dot(z_ref[...], w_ref[...], preferred_element_type=F32) + b_ref[...]
    o_ref[...] = x_ref[...] + g_ref[...] * acc


def _plain_out(lay, x, mods, z, w, b):
    tm = lay.tile(1024)
    tn = 512
    kdim = z.shape[1]
    return pl.pallas_call(
        _plain_out_kernel,
        out_shape=jax.ShapeDtypeStruct((lay.T, D), F32),
        grid=(lay.T // tm, D // tn),
        in_specs=[pl.BlockSpec((tm, tn), lambda i, j: (i, j)),
                  pl.BlockSpec((tm, kdim), lambda i, j: (i, 0)),
                  pl.BlockSpec((None, 1, tn), lambda i, j: (lay.group(i * tm) * MOD_CHUNKS + 2, 0, j)),
                  pl.BlockSpec((kdim, tn), lambda i, j: (0, j)),
                  pl.BlockSpec((1, tn), lambda i, j: (0, j))],
        out_specs=pl.BlockSpec((tm, tn), lambda i, j: (i, j)),
        compiler_params=_cp("arbitrary", "arbitrary"),
        name="plain_out",
    )(x, z, mods, w, b.reshape(1, D))


def _hyena_layer(lay, x, mods, p):
    proj = _proj(lay, x, mods, 0, 1, p['w_in'].astype(BF16), p['b_in'].reshape(1, 3 * D), 768)
    z = jnp.zeros((lay.T, D), BF16)
    for row0, nseq, L, ngrp, tc in ((0, lay.B, lay.L, math.gcd(lay.B, 4), 256),
                                    (lay.TP, lay.NS, lay.LS, lay.NS, 256)):
        conv = (proj, p['short_w'], p['short_b'], p['skip'])
        if (L // 2) % FOLD_BLK == 0 and L > 2 * FOLD_BLK:
            xc, xs, xh, k_ny = _hyena_filters_folded(L, p)
            tabs = _dft_tables_folded(L)
            pq = _hy_spectrum_folded(L, tabs, xc, xs, xh)
            z = _hy_core_folded(*conv, tabs, pq, k_ny, z, row0, nseq, L, ngrp, tc)
        else:
            k_lo, k_hi, k_ny = _hyena_filters(L, p)
            cos, sin = _dft_tables(L)
            pq = _hy_spectrum(L, cos, sin, k_lo, k_hi)
            z = _hy_core(*conv, cos, sin, pq, k_ny, z, row0, nseq, L, ngrp, tc)
    return _plain_out(lay, x, mods, z, p['w_out'].astype(BF16), p['b_out'])


_NT = (((1,), (1,)), ((), ()))
_TN = (((0,), (0,)), ((), ()))


def _tri(dr):
    t = lax.broadcasted_iota(jnp.int32, (CHUNK, CHUNK), 0)
    s = lax.broadcasted_iota(jnp.int32, (CHUNK, CHUNK), 1)
    return (s <= t) if dr == 0 else (s >= t)


def _chunk_cumsum(g, dr):
    n = g.shape[0]
    pos = lax.broadcasted_iota(jnp.int32, g.shape, 0) % CHUNK
    sh = 1
    while sh < CHUNK:
        if dr == 0:
            g = g + jnp.where(pos >= sh, pltpu.roll(g, sh, 0), 0.0)
        else:
            g = g + jnp.where(pos < CHUNK - sh, pltpu.roll(g, n - sh, 0), 0.0)
        sh *= 2
    return g


def _head_epilogue(o_sc, gate_ref, ng_ref, a_ref, center):
    rows = o_sc.shape[0]
    tr = math.gcd(rows, 256)

    def tile(i, carry):
        r = pl.ds(pl.multiple_of(i * tr, tr), tr)
        o = o_sc[r, :]
        if center:
            o = o - jnp.mean(o, axis=-1, keepdims=True)
        o = o * lax.rsqrt(jnp.mean(o * o, axis=-1, keepdims=True) + RMS_EPS) * ng_ref[...]
        a_ref[r, :] = (o * _silu(gate_ref[r, :].astype(F32))).astype(a_ref.dtype)
        return carry

    lax.fori_loop(0, rows // tr, tile, 0)


def _gla_kernel(cps, nseg, U, has_s0, want_final, *refs):
    q_ref, k_ref, v_ref, lr_ref, w2f_ref, w2b_ref, gb_ref, gate_ref, ng_ref = refs[:9]
    s0_ref = refs[9] if has_s0 else None
    qin_sc, kin_sc, kout_sc, dec_sc, st_sc, s_sc, s0t_sc, o_ref = refs[-8:]
    outs = refs[-10:-8] if want_final else refs[-9:-8]
    a_ref = outs[0]
    sf_ref = outs[1] if want_final else None
    C = CHUNK
    nsc = cps // U
    nchunks = nseg * cps
    rows_total = nchunks * C
    w2 = (w2f_ref, w2b_ref)

    for dr in range(2):
        pre = jnp.dot(lr_ref[...], w2[dr][...], preferred_element_type=F32) + gb_ref[dr:dr + 1, :]
        g = (jnp.minimum(pre, 0.0) - jnp.log(1.0 + jnp.exp(-jnp.abs(pre)))) * (1.0 / GLA_TAU)
        b = _chunk_cumsum(g, dr)
        b3 = b.reshape(nchunks, C, GLA_DK)
        tot = b3[:, C - 1:C, :] if dr == 0 else b3[:, 0:1, :]
        dec_sc[...] = jnp.exp(tot).reshape(nchunks, GLA_DK)
        k = k_ref[...].astype(F32)
        qin_sc[...] = (q_ref[...].astype(F32) * (GLA_DK ** -0.5) * jnp.exp(b)).astype(BF16)
        kin_sc[...] = (k * jnp.exp(-b)).astype(BF16)
        kout_sc[...] = (k * jnp.exp(tot - b3).reshape(rows_total, GLA_DK)).astype(BF16)
        if has_s0:
            s0t_sc[...] = jnp.transpose(s0_ref[dr], (1, 0))
        tri = _tri(dr)

        def super_chunk(jj, carry, dr=dr, tri=tri):
            j = jj if dr == 0 else nseg * nsc - 1 - jj
            in_seg = j % nsc
            first = (in_seg == 0) if dr == 0 else (in_seg == nsc - 1)
            last = (in_seg == nsc - 1) if dr == 0 else (in_seg == 0)

            @pl.when(first)
            def _():
                s_sc[...] = s0t_sc[...] if has_s0 else jnp.zeros_like(s_sc)

            base = j * (U * C)
            for u in range(U):
                rows = pl.ds(pl.multiple_of(base + u * C, C), C)
                v = v_ref[rows, :]
                sc = lax.dot_general(qin_sc[rows, :], kin_sc[rows, :], _NT, preferred_element_type=F32)
                o = jnp.dot(jnp.where(tri, sc, 0.0).astype(BF16), v, preferred_element_type=F32)
                st_sc[u] = lax.dot_general(v, kout_sc[rows, :], _TN, preferred_element_type=F32)
                if dr == 0:
                    o_ref[rows, :] = o
                else:
                    o_ref[rows, :] += o
            s = s_sc[...]
            for u in (range(U) if dr == 0 else reversed(range(U))):
                kv = st_sc[u]
                st_sc[u] = s
                s = dec_sc[pl.ds(j * U + u, 1), :] * s + kv
            s_sc[...] = s
            for u in range(U):
                rows = pl.ds(pl.multiple_of(base + u * C, C), C)
                o_ref[rows, :] += lax.dot_general(qin_sc[rows, :], st_sc[u].astype(BF16), _NT,
                                                  preferred_element_type=F32)
            if want_final:
                @pl.when(last)
                def _():
                    sf_ref[j // nsc, dr] = jnp.transpose(s, (1, 0))
            return carry

        lax.fori_loop(0, nseg * nsc, super_chunk, 0)

    _head_epilogue(o_ref, gate_ref, ng_ref, a_ref, center=False)


def _gla_core(proj, w2f, w2b, gate_b, norm_g, s0, o_prev, row0, nseq, seqlen, nseg, want_final):
    T = proj.shape[0]
    rows = nseg * seqlen
    cps = seqlen // CHUNK
    U = math.gcd(cps, 8)
    assert row0 % rows == 0 and seqlen % CHUNK == 0 and nseq % nseg == 0
    rb = row0 // rows
    hk = GLA_H * GLA_DK
    has_s0 = s0 is not None
    assert not has_s0 or nseg == 1
    in_specs = [pl.BlockSpec((rows, GLA_DK), lambda b, h: (rb + b, h)),
                pl.BlockSpec((rows, GLA_DK), lambda b, h: (rb + b, GLA_H + h)),
                pl.BlockSpec((rows, GLA_DV), lambda b, h: (rb + b, 2 * hk // GLA_DV + h)),
                pl.BlockSpec((rows, 128), lambda b, h: (rb + b, (2 * hk + 2 * GLA_H * GLA_DV) // 128)),
                pl.BlockSpec((128, GLA_DK), lambda b, h: (0, h)),
                pl.BlockSpec((128, GLA_DK), lambda b, h: (0, h)),
                pl.BlockSpec((2, GLA_DK), lambda b, h: (0, h)),
                pl.BlockSpec((rows, GLA_DV), lambda b, h: (rb + b, (2 * hk) // GLA_DV + GLA_H + h)),
                pl.BlockSpec((1, GLA_DV), lambda b, h: (0, 0))]
    args = [proj, proj, proj, proj, w2f, w2b, gate_b, proj, norm_g.reshape(1, GLA_DV)]
    if has_s0:
        in_specs.append(pl.BlockSpec((None, 2, None, GLA_DK, GLA_DV), lambda b, h: (b, 0, h, 0, 0)))
        args.append(s0)
    in_specs.append(pl.BlockSpec(memory_space=pl.ANY))
    args.append(o_prev)
    aliases = {len(args) - 1: 0}
    out_shape = [jax.ShapeDtypeStruct((T, GLA_H * GLA_DV), BF16)]
    out_specs = [pl.BlockSpec((rows, GLA_DV), lambda b, h: (rb + b, h))]
    if want_final:
        out_shape.append(jax.ShapeDtypeStruct((nseq, 2, GLA_H, GLA_DK, GLA_DV), F32))
        out_specs.append(pl.BlockSpec((nseg, 2, None, GLA_DK, GLA_DV), lambda b, h: (b, 0, h, 0, 0)))
    outs = pl.pallas_call(
        functools.partial(_gla_kernel, cps, nseg, U, has_s0, want_final),
        out_shape=tuple(out_shape),
        grid=(nseq // nseg, GLA_H),
        in_specs=in_specs,
        out_specs=tuple(out_specs),
        scratch_shapes=[pltpu.VMEM((rows, GLA_DK), BF16)] * 3
        + [pltpu.VMEM((nseg * cps, GLA_DK), F32), pltpu.VMEM((U, GLA_DV, GLA_DK), F32),
           pltpu.VMEM((GLA_DV, GLA_DK), F32), pltpu.VMEM((GLA_DV, GLA_DK), F32),
           pltpu.VMEM((rows, GLA_DV), F32)],
        input_output_aliases=aliases,
        compiler_params=_cp("arbitrary", "arbitrary"),
        name="gla",
    )(*args)
    return (outs[0], outs[1]) if want_final else (outs[0], None)


def _ret_kernel(cps, nseg, U, has_s0, want_final, rope, *refs):
    q_ref, k_ref, v_ref, dm_ref, qd_ref, kd_ref, cd_ref, gate_ref, ng_ref = refs[:9]
    nxt = 9
    if rope:
        cos_ref, sin_ref = refs[9:11]
        nxt = 11
    s0_ref = refs[nxt] if has_s0 else None
    qr_sc, kr_sc, qd_sc, kd_sc, st_sc, s_sc, o_ref = refs[-7:]
    outs = refs[-9:-7] if want_final else refs[-8:-7]
    a_ref = outs[0]
    sf_ref = outs[1] if want_final else None
    C = RET_CHUNK
    nsc = cps // U
    R = U * C
    SB = 64

    def rot(x, rows):
        if not rope:
            return x
        half = x.shape[1] // 2
        swapped = jnp.concatenate([pltpu.roll(x[:, :half], half // 2, 1),
                                   pltpu.roll(x[:, half:], half // 2, 1)], axis=1)
        return x * cos_ref[rows, :] + swapped * sin_ref[rows, :]

    for dr in range(2):
        def super_chunk(jj, carry, dr=dr):
            j = jj if dr == 0 else nseg * nsc - 1 - jj
            in_seg = j % nsc
            first = (in_seg == 0) if dr == 0 else (in_seg == nsc - 1)
            last = (in_seg == nsc - 1) if dr == 0 else (in_seg == 0)

            @pl.when(first)
            def _():
                s_sc[...] = s0_ref[dr] if has_s0 else jnp.zeros_like(s_sc)

            base = pl.multiple_of(j * R, R)
            rows_r = pl.ds(base, R)
            q = rot(q_ref[rows_r, :].astype(F32), rows_r)
            k = rot(k_ref[rows_r, :].astype(F32), rows_r) * (RET_DK ** -0.5)
            qr_sc[...] = q.astype(BF16)
            kr_sc[...] = k.astype(BF16)
            qd_sc[...] = (q.reshape(U, C, RET_DK) * qd_ref[dr][None]).reshape(R, RET_DK).astype(BF16)
            kd_sc[...] = (k.reshape(U, C, RET_DK) * kd_ref[dr][None]).reshape(R, RET_DK).astype(BF16)
            for u in range(U):
                loc = pl.ds(u * C, C)
                rows = pl.ds(pl.multiple_of(base + u * C, C), C)
                v = v_ref[rows, :]
                sc = lax.dot_general(qr_sc[loc, :], kr_sc[loc, :], _NT, preferred_element_type=F32)
                o = jnp.dot((sc * dm_ref[dr]).astype(BF16), v, preferred_element_type=F32)
                st_sc[u] = lax.dot_general(kd_sc[loc, :], v, _TN, preferred_element_type=F32)
                if dr == 0:
                    o_ref[rows, :] = o
                else:
                    o_ref[rows, :] += o
            cd = cd_ref[dr]
            for r0 in range(0, RET_DK, SB):
                srows = pl.ds(r0, SB)
                s = s_sc[srows, :]
                for u in (range(U) if dr == 0 else reversed(range(U))):
                    kv = st_sc[u, srows, :]
                    st_sc[u, srows, :] = s
                    s = cd * s + kv
                s_sc[srows, :] = s
            for u in range(U):
                rows = pl.ds(pl.multiple_of(base + u * C, C), C)
                o_ref[rows, :] += jnp.dot(qd_sc[pl.ds(u * C, C), :], st_sc[u].astype(BF16),
                                          preferred_element_type=F32)
            if want_final:
                @pl.when(last)
                def _():
                    sf_ref[j // nsc, dr] = s_sc[...]
            return carry

        lax.fori_loop(0, nseg * nsc, super_chunk, 0)

    _head_epilogue(o_ref, gate_ref, ng_ref, a_ref, center=True)


def _ret_tables(log_decay):
    C = RET_CHUNK
    lg = log_decay.astype(F32)[:, :, None, None]
    t = jnp.arange(C, dtype=F32)[:, None]
    s = jnp.arange(C, dtype=F32)[None, :]
    lag = jnp.stack([t - s, s - t])[:, None]
    dmask = jnp.where(lag >= 0, jnp.exp(jnp.maximum(lag, 0.0) * lg), 0.0)
    tl = jnp.arange(C, dtype=F32)[None, None, :, None]
    qdec = jnp.concatenate([jnp.exp((tl + 1.0) * lg[0:1]), jnp.exp((C - tl) * lg[1:2])], axis=0)
    kdec = jnp.concatenate([jnp.exp((C - 1.0 - tl) * lg[0:1]), jnp.exp(tl * lg[1:2])], axis=0)
    cdec = jnp.exp(C * lg)
    return dmask, qdec, kdec, cdec


def _rope_tables(seqlen, dk):
    half = dk // 2
    nf = half // 2
    pos = jnp.arange(seqlen, dtype=jnp.int32)
    inv = ROPE_BASE ** (-jnp.arange(nf, dtype=F32) / nf)
    ang_r = (pos // GRID_W).astype(F32)[:, None] * inv[None, :]
    ang_c = (pos % GRID_W).astype(F32)[:, None] * inv[None, :]
    cos = jnp.concatenate([jnp.cos(ang_r)] * 2 + [jnp.cos(ang_c)] * 2, axis=1)
    sin = jnp.concatenate([-jnp.sin(ang_r), jnp.sin(ang_r), -jnp.sin(ang_c), jnp.sin(ang_c)], axis=1)
    return cos, sin


def _ret_core(proj, tabs, norm_g, s0, o_prev, row0, nseq, seqlen, nseg, want_final, rope):
    T = proj.shape[0]
    rows = nseg * seqlen
    C = RET_CHUNK
    cps = seqlen // C
    U = math.gcd(cps, 4)
    assert row0 % rows == 0 and seqlen % C == 0 and nseq % nseg == 0
    rb = row0 // rows
    hk, hv = RET_H * RET_DK, RET_H * RET_DV
    has_s0 = s0 is not None
    assert not (has_s0 or rope) or nseg == 1
    tspec = lambda r, c: pl.BlockSpec((2, None, r, c), lambda b, h: (0, h, 0, 0))
    mode = dict(pipeline_mode=pl.Buffered(1)) if rows * RET_DV * 2 >= (4 << 20) else {}
    in_specs = [pl.BlockSpec((rows, RET_DK), lambda b, h: (rb + b, h), **mode),
                pl.BlockSpec((rows, RET_DK), lambda b, h: (rb + b, RET_H + h), **mode),
                pl.BlockSpec((rows, RET_DV), lambda b, h: (rb + b, 2 * hk // RET_DV + h), **mode),
                tspec(C, C), tspec(C, 1), tspec(C, 1), tspec(1, 1),
                pl.BlockSpec((rows, RET_DV), lambda b, h: (rb + b, (2 * hk + hv) // RET_DV + h), **mode),
                pl.BlockSpec((1, RET_DV), lambda b, h: (0, 0))]
    args = [proj, proj, proj, *tabs, proj, norm_g.reshape(1, RET_DV)]
    if rope:
        cos, sin = _rope_tables(seqlen, RET_DK)
        in_specs += [pl.BlockSpec((seqlen, RET_DK), lambda b, h: (0, 0), pipeline_mode=pl.Buffered(1))] * 2
        args += [cos, sin]
    if has_s0:
        in_specs.append(pl.BlockSpec((None, 2, None, RET_DK, RET_DV), lambda b, h: (b, 0, h, 0, 0)))
        args.append(s0)
    in_specs.append(pl.BlockSpec(memory_space=pl.ANY))
    args.append(o_prev)
    aliases = {len(args) - 1: 0}
    out_shape = [jax.ShapeDtypeStruct((T, hv), BF16)]
    out_specs = [pl.BlockSpec((rows, RET_DV), lambda b, h: (rb + b, h))]
    if want_final:
        out_shape.append(jax.ShapeDtypeStruct((nseq, 2, RET_H, RET_DK, RET_DV), F32))
        out_specs.append(pl.BlockSpec((nseg, 2, None, RET_DK, RET_DV), lambda b, h: (b, 0, h, 0, 0)))
    outs = pl.pallas_call(
        functools.partial(_ret_kernel, cps, nseg, U, has_s0, want_final, rope),
        out_shape=tuple(out_shape),
        grid=(nseq // nseg, RET_H),
        in_specs=in_specs,
        out_specs=tuple(out_specs),
        scratch_shapes=[pltpu.VMEM((U * C, RET_DK), BF16)] * 4
        + [pltpu.VMEM((U, RET_DK, RET_DV), F32), pltpu.VMEM((RET_DK, RET_DV), F32),
           pltpu.VMEM((rows, RET_DV), F32)],
        input_output_aliases=aliases,
        compiler_params=_cp("arbitrary", "arbitrary"),
        name="ret",
    )(*args)
    return (outs[0], outs[1]) if want_final else (outs[0], None)


def _gla_layer(lay, x, mods, p, s0):
    hk, hv = GLA_H * GLA_DK, GLA_H * GLA_DV
    w_all = jnp.concatenate([p['w_in'], p['gate_w1'][0], p['gate_w1'][1],
                             jnp.zeros((D, 128 - 2 * GLA_RANK), F32)], axis=1).astype(BF16)
    proj = _proj(lay, x, mods, 0, 1, w_all, jnp.zeros((1, w_all.shape[1]), F32), 640)
    pad = lambda w, lo: jnp.pad(w, ((lo, 128 - GLA_RANK - lo), (0, 0))).astype(BF16)
    w2f, w2b = pad(p['gate_w2'][0], 0), pad(p['gate_w2'][1], GLA_RANK)
    a = jnp.zeros((lay.T, hv), BF16)
    a, s_fin = _gla_core(proj, w2f, w2b, p['gate_b'], p['norm_g'], None, a, 0, lay.B, lay.L,
                         math.gcd(lay.B, 8), True)
    a, _ = _gla_core(proj, w2f, w2b, p['gate_b'], p['norm_g'], s0, a, lay.TP, lay.NS, lay.LS, 1, False)
    x = _plain_out(lay, x, mods, a, p['w_out'].astype(BF16), jnp.zeros((D,), F32))
    return x, s_fin


def _ret_layer(lay, x, mods, p, s0):
    hk, hv = RET_H * RET_DK, RET_H * RET_DV
    proj = _proj(lay, x, mods, 0, 1, p['w_in'].astype(BF16), jnp.zeros((1, 2 * hk + 2 * hv), F32), 1536)
    tabs = _ret_tables(p['log_decay'])
    a = jnp.zeros((lay.T, hv), BF16)
    a, s_fin = _ret_core(proj, tabs, p['norm_g'], None, a, 0, lay.B, lay.L, math.gcd(lay.B, 8), True, False)
    a, _ = _ret_core(proj, tabs, p['norm_g'], s0, a, lay.TP, lay.NS, lay.LS, 1, False, True)
    x = _plain_out(lay, x, mods, a, p['w_out'].astype(BF16), jnp.zeros((D,), F32))
    return x, s_fin


MOE_BM = 512
EXPERT_TF = 1792
ROUTER_LANES = 128
DMA_UNROLL = 8


def _router_kernel(x_ref, sh_ref, sc_ref, rw_ref, h_ref, idx_ref, gate_ref, rank_ref, cnt_ref):
    @pl.when(pl.program_id(0) == 0)
    def _():
        cnt_ref[...] = jnp.zeros_like(cnt_ref)

    h = _modulate(x_ref[...], sh_ref[...], sc_ref[...])
    h_ref[...] = h
    logits = jnp.dot(h, rw_ref[...], precision=HIGHEST, preferred_element_type=F32)
    lane = lax.broadcasted_iota(jnp.int32, logits.shape, 1)
    neg = jnp.float32(-jnp.inf)
    logits = jnp.where(lane < N_EXPERTS, logits, neg)
    m1 = jnp.max(logits, axis=-1, keepdims=True)
    i1 = jnp.min(jnp.where(logits == m1, lane, ROUTER_LANES), axis=-1, keepdims=True)
    rest = jnp.where(lane == i1, neg, logits)
    m2 = jnp.max(rest, axis=-1, keepdims=True)
    i2 = jnp.min(jnp.where(rest == m2, lane, ROUTER_LANES), axis=-1, keepdims=True)
    e2 = jnp.exp(m2 - m1)
    g1 = 1.0 / (1.0 + e2)
    idx_ref[:, 0:1] = i1
    idx_ref[:, 1:2] = i2
    gate_ref[:, 0:1] = g1
    gate_ref[:, 1:2] = e2 * g1
    tm = logits.shape[0]
    sel1 = lane == i1
    sel2 = lane == i2
    picked = jnp.where(jnp.logical_or(sel1, sel2), 1.0, 0.0)
    before = (lax.broadcasted_iota(jnp.int32, (tm, tm), 1)
              < lax.broadcasted_iota(jnp.int32, (tm, tm), 0)).astype(BF16)
    prior = jnp.dot(before, picked.astype(BF16), preferred_element_type=F32) + cnt_ref[...]
    rank_ref[:, 0:1] = jnp.sum(jnp.where(sel1, prior, 0.0), axis=-1, keepdims=True).astype(jnp.int32)
    rank_ref[:, 1:2] = jnp.sum(jnp.where(sel2, prior, 0.0), axis=-1, keepdims=True).astype(jnp.int32)
    cnt_ref[...] += jnp.sum(picked, axis=0, keepdims=True)


def _router(lay, x, mods, router_w):
    tm = lay.tile(512)
    rw = jnp.pad(router_w, ((0, 0), (0, ROUTER_LANES - N_EXPERTS)))
    return pl.pallas_call(
        _router_kernel,
        out_shape=(jax.ShapeDtypeStruct((lay.T, D), F32),
                   jax.ShapeDtypeStruct((lay.T, 2), jnp.int32),
                   jax.ShapeDtypeStruct((lay.T, 2), F32),
                   jax.ShapeDtypeStruct((lay.T, 2), jnp.int32),
                   jax.ShapeDtypeStruct((1, ROUTER_LANES), F32)),
        grid=(lay.T // tm,),
        in_specs=[pl.BlockSpec((tm, D), lambda i: (i, 0)),
                  _mod_spec(lay, tm, 3, 1), _mod_spec(lay, tm, 4, 1),
                  pl.BlockSpec((D, ROUTER_LANES), lambda i: (0, 0))],
        out_specs=(pl.BlockSpec((tm, D), lambda i: (i, 0)),
                   pl.BlockSpec((tm, 2), lambda i: (i, 0)),
                   pl.BlockSpec((tm, 2), lambda i: (i, 0)),
                   pl.BlockSpec((tm, 2), lambda i: (i, 0)),
                   pl.BlockSpec((1, ROUTER_LANES), lambda i: (0, 0))),
        compiler_params=_cp("arbitrary"),
        name="router",
    )(x, mods, mods, rw)


def _moe_plan(idx, rank, counts, bm):
    a = idx.size
    counts = counts[0, :N_EXPERTS].astype(jnp.int32)
    padded = (counts + bm - 1) // bm * bm
    pad_end = jnp.cumsum(padded)
    pad_start = pad_end - padded
    hit = idx[..., None] == jnp.arange(N_EXPERTS, dtype=jnp.int32)
    dest = (rank + jnp.sum(jnp.where(hit, pad_start, 0), axis=-1)).reshape(a).astype(jnp.int32)
    nb = -(-(a + N_EXPERTS * (bm - 1)) // bm)
    block_start = jnp.arange(nb, dtype=jnp.int32) * bm
    block_e = jnp.sum((block_start[:, None] >= pad_end[None, :]).astype(jnp.int32), axis=1)
    block_e = jnp.minimum(block_e, N_EXPERTS - 1)
    nvalid = (pad_end[-1] // bm).astype(jnp.int32).reshape(1)
    fill = jnp.concatenate([pad_start + counts, pad_end, nvalid]).astype(jnp.int32)
    return dest, fill, block_e, nvalid, nb


def _dispatch_kernel(tm, bm, nb, dest_ref, fill_ref, h_ref, xs_hbm, zero_sc, sem, zsem):
    i = pl.program_id(0)
    zr = zero_sc.shape[0]

    @pl.when(i == 0)
    def _():
        zero_sc[...] = jnp.zeros_like(zero_sc)

        def zero_row(r):
            return pltpu.make_async_copy(zero_sc.at[pl.ds(0, 1)], xs_hbm.at[pl.ds(r, 1)], zsem)

        def zero_rows(r):
            return pltpu.make_async_copy(zero_sc, xs_hbm.at[pl.ds(pl.multiple_of(r, zr), zr)], zsem)

        for e in range(N_EXPERTS):
            lo, hi = fill_ref[e], fill_ref[N_EXPERTS + e]
            lax.fori_loop(lo, hi, lambda r, c: (zero_row(r).start(), c)[1], 0)
            lax.fori_loop(lo, hi, lambda r, c: (zero_row(r).wait(), c)[1], 0)
        lo, hi = fill_ref[2 * N_EXPERTS] * (bm // zr), nb * (bm // zr)
        lax.fori_loop(lo, hi, lambda q, c: (zero_rows(q * zr).start(), c)[1], 0)
        lax.fori_loop(lo, hi, lambda q, c: (zero_rows(q * zr).wait(), c)[1], 0)

    def row_copy(r, dst):
        return pltpu.make_async_copy(h_ref.at[pl.ds(r, 1)], xs_hbm.at[pl.ds(dst, 1)], sem)

    def issue(r, carry):
        a = 2 * (i * tm + r)
        row_copy(r, dest_ref[a]).start()
        row_copy(r, dest_ref[a + 1]).start()
        return carry

    def drain(r, carry):
        row_copy(r, 0).wait()
        row_copy(r, 0).wait()
        return carry

    lax.fori_loop(0, tm, issue, 0, unroll=DMA_UNROLL)
    lax.fori_loop(0, tm, drain, 0, unroll=DMA_UNROLL)


def _dispatch(lay, h, dest, fill, nb, bm):
    tm = lay.tile(512)
    grid_spec = pltpu.PrefetchScalarGridSpec(
        num_scalar_prefetch=2,
        grid=(lay.T // tm,),
        in_specs=[pl.BlockSpec((tm, D), lambda i, d, f: (i, 0))],
        out_specs=pl.BlockSpec(memory_space=pl.ANY),
        scratch_shapes=[pltpu.VMEM((64, D), F32), pltpu.SemaphoreType.DMA(()), pltpu.SemaphoreType.DMA(())],
    )
    return pl.pallas_call(
        functools.partial(_dispatch_kernel, tm, bm, nb),
        out_shape=jax.ShapeDtypeStruct((nb * bm, D), F32),
        grid_spec=grid_spec,
        compiler_params=_cp("arbitrary"),
        name="dispatch",
    )(dest, fill, h)


def _experts_kernel(nf, be_ref, nv_ref, xs_ref, wa_ref, wb_ref, wo_ref, o_ref, xb_sc, acc_sc):
    i = pl.program_id(0)
    f = pl.program_id(1)
    valid = i < nv_ref[0]

    @pl.when(jnp.logical_and(valid, f == 0))
    def _():
        xb_sc[...] = xs_ref[...].astype(BF16)

    @pl.when(valid)
    def _():
        xb = xb_sc[...]
        a = jnp.dot(xb, wa_ref[...], preferred_element_type=F32)
        b = jnp.dot(xb, wb_ref[...], preferred_element_type=F32)
        h = (_silu(a) * b).astype(BF16)
        y = jnp.dot(h, wo_ref[...], preferred_element_type=F32)

        @pl.when(f == 0)
        def _():
            acc_sc[...] = y

        @pl.when(f > 0)
        def _():
            acc_sc[...] += y

    @pl.when(f == nf - 1)
    def _():
        o_ref[...] = jnp.where(valid, acc_sc[...], 0.0)


def _experts(xs, block_e, nvalid, nb, bm, w_in, w_out):
    tf = EXPERT_TF
    nf = EXPERT_DIM // tf

    def wmap(off):
        def imap(i, f, be, nv):
            fe = jnp.where(i < nv[0], f, nf - 1)
            return (be[i], 0, off + fe)
        return imap

    def womap(i, f, be, nv):
        fe = jnp.where(i < nv[0], f, nf - 1)
        return (be[i], fe, 0)

    grid_spec = pltpu.PrefetchScalarGridSpec(
        num_scalar_prefetch=2,
        grid=(nb, nf),
        in_specs=[pl.BlockSpec((bm, D), lambda i, f, be, nv: (jnp.minimum(i, nv[0] - 1), 0)),
                  pl.BlockSpec((None, D, tf), wmap(0)),
                  pl.BlockSpec((None, D, tf), wmap(nf)),
                  pl.BlockSpec((None, tf, D), womap)],
        out_specs=pl.BlockSpec((bm, D), lambda i, f, be, nv: (i, 0)),
        scratch_shapes=[pltpu.VMEM((bm, D), BF16), pltpu.VMEM((bm, D), F32)],
    )
    return pl.pallas_call(
        functools.partial(_experts_kernel, nf),
        out_shape=jax.ShapeDtypeStruct((nb * bm, D), F32),
        grid_spec=grid_spec,
        compiler_params=_cp("arbitrary", "arbitrary"),
        name="experts",
    )(block_e, nvalid, xs, w_in, w_in, w_out)


def _combine_kernel(tm, nt, final, dest_ref, x_ref, gate_ref, g_ref, fg_ref, ys_hbm, o_ref, y_sc, sem):
    i = pl.program_id(0)
    slot = i % 2

    def row_copy(s, k, r, src):
        return pltpu.make_async_copy(ys_hbm.at[pl.ds(src, 1)], y_sc.at[s, k, pl.ds(r, 1)], sem.at[s])

    def issue_tile(t, s):
        def issue(r, carry):
            a = 2 * (t * tm + r)
            row_copy(s, 0, r, dest_ref[a]).start()
            row_copy(s, 1, r, dest_ref[a + 1]).start()
            return carry
        lax.fori_loop(0, tm, issue, 0, unroll=DMA_UNROLL)

    @pl.when(i == 0)
    def _():
        issue_tile(0, 0)

    @pl.when(i + 1 < nt)
    def _():
        issue_tile(i + 1, 1 - slot)

    def drain(r, carry):
        row_copy(slot, 0, r, 0).wait()
        row_copy(slot, 1, r, 0).wait()
        return carry

    lax.fori_loop(0, tm, drain, 0, unroll=DMA_UNROLL)
    gate = gate_ref[...]
    out = x_ref[...] + g_ref[...] * (gate[:, 0:1] * y_sc[slot, 0] + gate[:, 1:2] * y_sc[slot, 1])
    if final:
        ms = jnp.mean(out * out, axis=-1, keepdims=True)
        out = out * lax.rsqrt(ms + RMS_EPS) * fg_ref[...]
    o_ref[...] = out


def _combine(lay, x, mods, gates, ys, dest, final_g):
    tm = lay.tile(256)
    nt = lay.T // tm
    final = final_g is not None
    fg = (final_g if final else jnp.ones((D,), F32)).reshape(1, D)
    grid_spec = pltpu.PrefetchScalarGridSpec(
        num_scalar_prefetch=1,
        grid=(nt,),
        in_specs=[pl.BlockSpec((tm, D), lambda i, d: (i, 0)),
                  pl.BlockSpec((tm, 2), lambda i, d: (i, 0)),
                  pl.BlockSpec((None, 1, D), lambda i, d: (lay.group(i * tm) * MOD_CHUNKS + 5, 0, 0)),
                  pl.BlockSpec((1, D), lambda i, d: (0, 0)),
                  pl.BlockSpec(memory_space=pl.ANY)],
        out_specs=pl.BlockSpec((tm, D), lambda i, d: (i, 0)),
        scratch_shapes=[pltpu.VMEM((2, 2, tm, D), F32), pltpu.SemaphoreType.DMA((2,))],
    )
    return pl.pallas_call(
        functools.partial(_combine_kernel, tm, nt, final),
        out_shape=jax.ShapeDtypeStruct((lay.T, D), F32),
        grid_spec=grid_spec,
        compiler_params=_cp("arbitrary"),
        name="combine",
    )(dest, x, gates, mods, fg, ys)


def _moe_layer(lay, x, mods, router_w, w_in, w_out, final_g=None, bm=MOE_BM):
    h, idx, gates, rank, counts = _router(lay, x, mods, router_w)
    dest, fill, block_e, nvalid, nb = _moe_plan(idx, rank, counts, bm)
    xs = _dispatch(lay, h, dest, fill, nb, bm)
    ys = _experts(xs, block_e, nvalid, nb, bm, w_in, w_out)
    return _combine(lay, x, mods, gates, ys, dest, final_g)


def kernel(x_prompt, x_sample, c, state_l0_s5_re, state_l0_s5_im, state_l2_gla, state_l3_ret, c_ctx, l0_mod_w, l0_mod_b, l0_s5_a_re, l0_s5_a_im, l0_s5_log_dt, l0_s5_b_re, l0_s5_b_im, l0_s5_c_re, l0_s5_c_im, l0_s5_d, l0_s5_glu_w, l0_ffn_w_in, l0_ffn_w_out, l1_mod_w, l1_mod_b, l1_hy_w_in, l1_hy_b_in, l1_hy_short_w, l1_hy_short_b, l1_hy_f_w1, l1_hy_f_b1, l1_hy_f_w2, l1_hy_f_b2, l1_hy_f_w3, l1_hy_f_freq, l1_hy_skip, l1_hy_w_out, l1_hy_b_out, l1_moe_router, l1_moe_w_in, l1_moe_w_out, l2_mod_w, l2_mod_b, l2_gla_w_in, l2_gla_gate_w1, l2_gla_gate_w2, l2_gla_gate_b, l2_gla_norm_g, l2_gla_w_out, l2_ffn_w_in, l2_ffn_w_out, l3_mod_w, l3_mod_b, l3_ret_w_in, l3_ret_log_decay, l3_ret_norm_g, l3_ret_w_out, l3_moe_router, l3_moe_w_in, l3_moe_w_out, final_norm_g):
    B, L, _ = x_prompt.shape
    NS, LS, _ = x_sample.shape
    lay = Layout(B, L, NS, LS)
    x = jnp.concatenate([x_prompt.reshape(B * L, D), x_sample.reshape(NS * LS, D)], axis=0)
    cond = jnp.concatenate([c_ctx[None], c, jnp.zeros((8 - 1 - NS, D), F32)], axis=0)
    mods0 = _mods(cond, l0_mod_w, l0_mod_b)
    p0 = dict(a_re=l0_s5_a_re, a_im=l0_s5_a_im, log_dt=l0_s5_log_dt, b_re=l0_s5_b_re, b_im=l0_s5_b_im,
              c_re=l0_s5_c_re, c_im=l0_s5_c_im, d=l0_s5_d, glu_w=l0_s5_glu_w.astype(BF16))
    x, s5_re, s5_im = _s5_layer(lay, x, mods0, p0, state_l0_s5_re, state_l0_s5_im)
    x = _ffn(lay, x, mods0, l0_ffn_w_in.astype(BF16), l0_ffn_w_out.astype(BF16))

    mods1 = _mods(cond, l1_mod_w, l1_mod_b)
    p1 = dict(w_in=l1_hy_w_in, b_in=l1_hy_b_in, short_w=l1_hy_short_w, short_b=l1_hy_short_b,
              f_w1=l1_hy_f_w1, f_b1=l1_hy_f_b1, f_w2=l1_hy_f_w2, f_b2=l1_hy_f_b2, f_w3=l1_hy_f_w3,
              f_freq=l1_hy_f_freq, skip=l1_hy_skip, w_out=l1_hy_w_out, b_out=l1_hy_b_out)
    x = _hyena_layer(lay, x, mods1, p1)
    x = _moe_layer(lay, x, mods1, l1_moe_router, l1_moe_w_in.astype(BF16), l1_moe_w_out.astype(BF16))

    mods2 = _mods(cond, l2_mod_w, l2_mod_b)
    p2 = dict(w_in=l2_gla_w_in, gate_w1=l2_gla_gate_w1, gate_w2=l2_gla_gate_w2, gate_b=l2_gla_gate_b,
              norm_g=l2_gla_norm_g, w_out=l2_gla_w_out)
    x, gla_state = _gla_layer(lay, x, mods2, p2, state_l2_gla)
    x = _ffn(lay, x, mods2, l2_ffn_w_in.astype(BF16), l2_ffn_w_out.astype(BF16))

    mods3 = _mods(cond, l3_mod_w, l3_mod_b)
    p3 = dict(w_in=l3_ret_w_in, log_decay=l3_ret_log_decay, norm_g=l3_ret_norm_g, w_out=l3_ret_w_out)
    x, ret_state = _ret_layer(lay, x, mods3, p3, state_l3_ret)
    y = _moe_layer(lay, x, mods3, l3_moe_router, l3_moe_w_in.astype(BF16), l3_moe_w_out.astype(BF16),
                   final_g=final_norm_g)
    return (y[:lay.TP].reshape(B, L, D), y[lay.TP:].reshape(NS, LS, D), s5_re, s5_im, gla_state, ret_state)
```

```python
import functools
import math

import jax
import jax.numpy as jnp
import numpy as np
from jax import lax
from jax.experimental import pallas as pl
from jax.experimental.pallas import tpu as pltpu

F32 = jnp.float32
BF16 = jnp.bfloat16
HIGHEST = lax.Precision.HIGHEST

D = 1024
RMS_EPS = 1e-6
MOD_CHUNKS = 6
GRID_W = 64

S5_Q = 16
S5_G = D // S5_Q
S5_P = 64
S5_T = 16
S5_SCAN_ROWS = 64

HY_BANDS = 16
HY_TARGET = 1e-2
HY_FAST_PCT = 0.3
HY_SLOW_PCT = 1.5

GLA_H, GLA_DK, GLA_DV = 4, 128, 256
GLA_RANK = 16
GLA_TAU = 16.0
RET_H, RET_DK, RET_DV = 4, 256, 512
CHUNK = 64
RET_CHUNK = 256
ROPE_BASE = 10000.0

FFN_DIM = 2816
N_EXPERTS = 8
EXPERT_DIM = 3584

VMEM_LIMIT_V7X = 56 * 1024 * 1024


def _cp(*sem):
    return pltpu.CompilerParams(dimension_semantics=sem, vmem_limit_bytes=VMEM_LIMIT_V7X)


def _silu(x):
    return x * jax.nn.sigmoid(x)


def _modulate(x, shift, scale):
    ms = jnp.mean(x * x, axis=-1, keepdims=True)
    return x * lax.rsqrt(ms + RMS_EPS) * (1.0 + scale) + shift


class Layout:
    def __init__(self, n_prompt, l_prompt, n_sample, l_sample):
        self.B, self.L, self.NS, self.LS = n_prompt, l_prompt, n_sample, l_sample
        self.TP = n_prompt * l_prompt
        self.T = self.TP + n_sample * l_sample

    def tile(self, want):
        t = math.gcd(math.gcd(self.TP, self.LS), want)
        assert t % 8 == 0
        return t

    def group(self, row):
        return jnp.where(row < self.TP, 0, 1 + (row - self.TP) // self.LS)


def _mod_spec(lay, tm, chunk, ngrid):
    def imap(*ids):
        return (lay.group(ids[0] * tm) * MOD_CHUNKS + chunk, 0, 0)
    del ngrid
    return pl.BlockSpec((None, 1, D), imap)


def _mods_kernel(c_ref, w_ref, b_ref, o_ref):
    o_ref[...] = jnp.dot(_silu(c_ref[...]), w_ref[...], precision=HIGHEST,
                         preferred_element_type=F32) + b_ref[...]


def _mods(cond, w, b):
    n = MOD_CHUNKS * D
    tn = 1536
    out = pl.pallas_call(
        _mods_kernel,
        out_shape=jax.ShapeDtypeStruct((8, n), F32),
        grid=(n // tn,),
        in_specs=[pl.BlockSpec((8, D), lambda j: (0, 0)),
                  pl.BlockSpec((D, tn), lambda j: (0, j)),
                  pl.BlockSpec((1, tn), lambda j: (0, j))],
        out_specs=pl.BlockSpec((8, tn), lambda j: (0, j)),
        compiler_params=_cp("arbitrary"),
        name="mods",
    )(cond, w, b.reshape(1, n))
    return out.reshape(8 * MOD_CHUNKS, 1, D)


def _modulate_kernel(x_ref, sh_ref, sc_ref, o_ref):
    o_ref[...] = _modulate(x_ref[...], sh_ref[...], sc_ref[...]).astype(o_ref.dtype)


def _modulate_call(lay, x, mods, c_shift, c_scale, dtype):
    tm = lay.tile(512)
    return pl.pallas_call(
        _modulate_kernel,
        out_shape=jax.ShapeDtypeStruct((lay.T, D), dtype),
        grid=(lay.T // tm,),
        in_specs=[pl.BlockSpec((tm, D), lambda i: (i, 0)),
                  _mod_spec(lay, tm, c_shift, 1), _mod_spec(lay, tm, c_scale, 1)],
        out_specs=pl.BlockSpec((tm, D), lambda i: (i, 0)),
        compiler_params=_cp("arbitrary"),
        name="modulate",
    )(x, mods, mods)


def _proj_kernel(x_ref, sh_ref, sc_ref, w_ref, b_ref, o_ref, u_sc):
    @pl.when(pl.program_id(1) == 0)
    def _():
        u_sc[...] = _modulate(x_ref[...], sh_ref[...], sc_ref[...]).astype(BF16)

    acc = jnp.dot(u_sc[...], w_ref[...], preferred_element_type=F32) + b_ref[...]
    o_ref[...] = acc.astype(o_ref.dtype)


def _proj(lay, x, mods, c_shift, c_scale, w, b, tn, out_dtype=BF16, tm=1024):
    tm = lay.tile(tm)
    n = w.shape[1]
    assert n % tn == 0
    resident = dict(pipeline_mode=pl.Buffered(1)) if tn == n else {}
    return pl.pallas_call(
        _proj_kernel,
        out_shape=jax.ShapeDtypeStruct((lay.T, n), out_dtype),
        grid=(lay.T // tm, n // tn),
        in_specs=[pl.BlockSpec((tm, D), lambda i, j: (i, 0)),
                  _mod_spec(lay, tm, c_shift, 2), _mod_spec(lay, tm, c_scale, 2),
                  pl.BlockSpec((D, tn), lambda i, j: (0, j), **resident),
                  pl.BlockSpec((1, tn), lambda i, j: (0, j))],
        out_specs=pl.BlockSpec((tm, tn), lambda i, j: (i, j)),
        scratch_shapes=[pltpu.VMEM((tm, D), BF16)],
        compiler_params=_cp("arbitrary", "arbitrary"),
        name="proj",
    )(x, mods, mods, w, b)


def _ffn_kernel(x_ref, sh_ref, sc_ref, g_ref, wa_ref, wb_ref, wo_ref, o_ref):
    x = x_ref[...]
    u = _modulate(x, sh_ref[...], sc_ref[...]).astype(BF16)
    a = jnp.dot(u, wa_ref[...], preferred_element_type=F32)
    b = jnp.dot(u, wb_ref[...], preferred_element_type=F32)
    h = (_silu(a) * b).astype(BF16)
    o_ref[...] = x + g_ref[...] * jnp.dot(h, wo_ref[...], preferred_element_type=F32)


def _ffn(lay, x, mods, w_in, w_out):
    tm = lay.tile(512)
    once = dict(pipeline_mode=pl.Buffered(1))
    return pl.pallas_call(
        _ffn_kernel,
        out_shape=jax.ShapeDtypeStruct((lay.T, D), F32),
        grid=(lay.T // tm,),
        in_specs=[pl.BlockSpec((tm, D), lambda i: (i, 0)),
                  _mod_spec(lay, tm, 3, 1), _mod_spec(lay, tm, 4, 1), _mod_spec(lay, tm, 5, 1),
                  pl.BlockSpec((D, FFN_DIM), lambda i: (0, 0), **once),
                  pl.BlockSpec((D, FFN_DIM), lambda i: (0, 1), **once),
                  pl.BlockSpec((FFN_DIM, D), lambda i: (0, 0), **once)],
        out_specs=pl.BlockSpec((tm, D), lambda i: (i, 0)),
        compiler_params=_cp("arbitrary"),
        name="ffn",
    )(x, mods, mods, mods, w_in, w_in, w_out)


def _s5_tables(a_re, a_im, log_dt, b_re, b_im, c_re, c_im, d_skip):
    T, G, P, Q = S5_T, S5_G, S5_P, S5_Q
    a = lax.complex(a_re, a_im)
    adt = a * jnp.exp(log_dt)[..., None]
    lam = jnp.exp(adt)
    bb = ((lam - 1.0) / a)[..., None] * lax.complex(b_re, b_im)
    cm = lax.complex(c_re, c_im)
    steps = jnp.arange(T + 1, dtype=F32)
    pw = jnp.exp(steps[None, :, None, None] * adt[:, None])
    kern = jnp.real(jnp.einsum('dgqp,djgp,dgpr->djgqr', cm, pw[:, :T], bb))
    lag = jnp.arange(T)[:, None, None]
    s_i = jnp.arange(T)[None, :, None]
    t_i = jnp.arange(T)[None, None, :]
    place = jnp.stack([t_i - s_i == lag, s_i - t_i == lag]).astype(F32)
    m = jnp.einsum('djst,djgqr->gsrtq', place, kern, precision=HIGHEST)
    eye = (jnp.eye(T)[:, None, :, None] * jnp.eye(Q)[None, :, None, :])
    m = m + eye[None] * d_skip.reshape(G, 1, 1, 1, Q)
    m = m.reshape(G, T * Q, T * Q)
    e_f = pw[0][T - 1 - jnp.arange(T)]
    e_b = pw[1][jnp.arange(T)]
    n_f = e_f[..., None] * bb[0][None]
    n_b = e_b[..., None] * bb[1][None]
    n_c = jnp.concatenate([n_f, n_b], axis=2)
    n_c = jnp.transpose(n_c, (1, 0, 3, 2)).reshape(G, T * Q, 2 * P)
    lam_t = jnp.concatenate([pw[0][T], pw[1][T]], axis=-1)
    w_f = cm[0][:, None] * jnp.transpose(pw[0][1:T + 1], (1, 0, 2))[:, :, None, :]
    w_b = cm[1][:, None] * jnp.transpose(pw[1][T - jnp.arange(T)], (1, 0, 2))[:, :, None, :]
    w_f = jnp.transpose(w_f, (0, 3, 1, 2)).reshape(G, P, T * Q)
    w_b = jnp.transpose(w_b, (0, 3, 1, 2)).reshape(G, P, T * Q)
    z = jnp.zeros_like(jnp.real(w_f))
    c_mats = dict(c_f_re=jnp.concatenate([jnp.real(w_f), z], axis=1),
                  c_f_im=jnp.concatenate([-jnp.imag(w_f), z], axis=1),
                  c_b_re=jnp.concatenate([z, jnp.real(w_b)], axis=1),
                  c_b_im=jnp.concatenate([z, -jnp.imag(w_b)], axis=1))
    return dict(m=m.astype(BF16), n_re=jnp.real(n_c).astype(BF16), n_im=jnp.imag(n_c).astype(BF16),
                l_re=jnp.real(lam_t), l_im=jnp.imag(lam_t), **{k: v.astype(BF16) for k, v in c_mats.items()})


def _s5_in_kernel(u_ref, m_ref, nre_ref, nim_ref, yi_ref, sre_ref, sim_ref):
    u = u_ref[...]
    yi_ref[...] = jnp.dot(u, m_ref[...], preferred_element_type=F32)
    sre_ref[...] = jnp.dot(u, nre_ref[...], preferred_element_type=F32).reshape(sre_ref.shape)
    sim_ref[...] = jnp.dot(u, nim_ref[...], preferred_element_type=F32).reshape(sim_ref.shape)


def _s5_in(ug, tabs):
    G, R, W = ug.shape
    P2 = 2 * S5_P
    RB = S5_SCAN_ROWS
    assert R % RB == 0
    gspec = lambda n: pl.BlockSpec((None, W, n), lambda g: (g, 0, 0))
    rspec = pl.BlockSpec((None, R, W), lambda g: (g, 0, 0))
    sspec = pl.BlockSpec((R // RB, RB, P2), lambda g: (0, g, 0))
    sshape = jax.ShapeDtypeStruct((R // RB, G * RB, P2), F32)
    return pl.pallas_call(
        _s5_in_kernel,
        out_shape=(jax.ShapeDtypeStruct((G, R, W), F32), sshape, sshape),
        grid=(G,),
        in_specs=[rspec, gspec(W), gspec(P2), gspec(P2)],
        out_specs=(rspec, sspec, sspec),
        compiler_params=_cp("arbitrary"),
        name="s5_in",
    )(ug, tabs['m'], tabs['n_re'], tabs['n_im'])


def _s5_scan_kernel(nsb, ncb, nblk, sref_ref, simf_ref, sreb_ref, simb_ref, lre_ref, lim_ref,
                    h0re_ref, h0im_ref, *rest):
    hfre_ref, hfim_ref, hbre_ref, hbim_ref, fre_ref, fim_ref, cre_sc, cim_sc = rest[4:]
    P = S5_P
    rows = sref_ref.shape[0] // ncb
    j = pl.program_id(1)
    fwd = lax.broadcasted_iota(jnp.int32, (1, 2 * P), 1) < P
    lre = lre_ref[...]
    lim = lim_ref[...]

    @pl.when(j == 0)
    def _():
        cre_sc[...] = h0re_ref[...]
        cim_sc[...] = h0im_ref[...]

    def at(k):
        return pl.ds(k, rows, stride=ncb)

    def body(k, carry):
        hre, him = carry
        kb = ncb - 1 - k
        hfre_ref[at(k), :] = hre
        hfim_ref[at(k), :] = him
        hbre_ref[at(kb), :] = hre
        hbim_ref[at(kb), :] = him
        sre = jnp.where(fwd, sref_ref[at(k), :], sreb_ref[at(kb), :])
        sim = jnp.where(fwd, simf_ref[at(k), :], simb_ref[at(kb), :])
        return (lre * hre - lim * him + sre, lre * him + lim * hre + sim)

    hre, him = lax.fori_loop(0, ncb, body, (cre_sc[...], cim_sc[...]), unroll=4)
    cre_sc[...] = hre
    cim_sc[...] = him

    @pl.when(j == nblk - 1)
    def _():
        fre_ref[...] = hre
        fim_ref[...] = him


def _s5_scan(sre, sim, tabs, h0re, h0im, hprev, row0, nseq, nc, nsb, nblk):
    _, grb, P2 = sre.shape
    rb = S5_SCAN_ROWS
    G = grb // rb
    assert nblk == 1 or nsb == 1
    ncb = nc // nblk
    assert nsb * ncb == rb and row0 % rb == 0 and nseq % nsb == 0 and nc % nblk == 0
    b0 = row0 // rb
    fspec = pl.BlockSpec((None, grb, P2), lambda i, j: (b0 + i * nblk + j, 0, 0))
    bspec = pl.BlockSpec((None, grb, P2), lambda i, j: (b0 + i * nblk + nblk - 1 - j, 0, 0))
    lspec = pl.BlockSpec((G * nsb, P2), lambda i, j: (0, 0))
    qspec = pl.BlockSpec((None, G * nsb, P2), lambda i, j: (i, 0, 0))
    anyspec = pl.BlockSpec(memory_space=pl.ANY)
    fin = jax.ShapeDtypeStruct((nseq // nsb, G * nsb, P2), F32)
    rep = lambda a: jnp.repeat(a, nsb, axis=0)
    flat = lambda a: a.reshape(nseq // nsb, G * nsb, P2)
    outs = pl.pallas_call(
        functools.partial(_s5_scan_kernel, nsb, ncb, nblk),
        out_shape=tuple(jax.ShapeDtypeStruct(h.shape, h.dtype) for h in hprev) + (fin, fin),
        grid=(nseq // nsb, nblk),
        in_specs=[fspec, fspec, bspec, bspec, lspec, lspec, qspec, qspec] + [anyspec] * 4,
        out_specs=(fspec, fspec, bspec, bspec, qspec, qspec),
        scratch_shapes=[pltpu.VMEM((G * nsb, P2), F32), pltpu.VMEM((G * nsb, P2), F32)],
        input_output_aliases={8: 0, 9: 1, 10: 2, 11: 3},
        compiler_params=_cp("arbitrary", "arbitrary"),
        name="s5_scan",
    )(sre, sim, sre, sim, rep(tabs['l_re']), rep(tabs['l_im']), flat(h0re), flat(h0im), *hprev)
    return outs[:4], outs[4].reshape(nseq // nsb, G, nsb, P2), outs[5].reshape(nseq // nsb, G, nsb, P2)


def _s5_out_kernel(yi_ref, hfre_ref, hfim_ref, hbre_ref, hbim_ref, cfre_ref, cfim_ref, cbre_ref, cbim_ref,
                   y_ref):
    y = yi_ref[...]
    for h_ref, c_ref in ((hfre_ref, cfre_ref), (hfim_ref, cfim_ref), (hbre_ref, cbre_ref), (hbim_ref, cbim_ref)):
        h = h_ref[...].reshape(y.shape[0], h_ref.shape[-1])
        y += jnp.dot(h.astype(BF16), c_ref[...], preferred_element_type=F32)
    y_ref[...] = y.astype(y_ref.dtype)


def _s5_out(yi, hprev, tabs):
    G, R, W = yi.shape
    P2 = 2 * S5_P
    RB = S5_SCAN_ROWS
    gspec = pl.BlockSpec((None, P2, W), lambda g: (g, 0, 0))
    hspec = pl.BlockSpec((R // RB, RB, P2), lambda g: (0, g, 0))
    rspec = pl.BlockSpec((None, R, W), lambda g: (g, 0, 0))
    return pl.pallas_call(
        _s5_out_kernel,
        out_shape=jax.ShapeDtypeStruct((G, R, W), F32),
        grid=(G,),
        in_specs=[rspec] + [hspec] * 4 + [gspec] * 4,
        out_specs=rspec,
        compiler_params=_cp("arbitrary"),
        name="s5_out",
    )(yi, *hprev, tabs['c_f_re'], tabs['c_f_im'], tabs['c_b_re'], tabs['c_b_im'])


LANES = 128
S5_GB = LANES // S5_Q


def _block_transpose(sets):
    blk = lax.broadcasted_iota(jnp.int32, sets[0][0].shape, 1) // S5_Q
    sets = [list(regs) for regs in sets]
    d = S5_GB // 2
    while d:
        keep = (blk & d) == 0
        for regs in sets:
            for i in range(S5_GB):
                if i & d:
                    continue
                a, b = regs[i], regs[i + d]
                regs[i] = jnp.where(keep, a, pltpu.roll(b, d * S5_Q, 1))
                regs[i + d] = jnp.where(keep, pltpu.roll(a, LANES - d * S5_Q, 1), b)
        d //= 2
    return sets


def _s5_pre_kernel(tm, x_ref, sh_ref, sc_ref, ug_ref, u_sc):
    u = _modulate(x_ref[...], sh_ref[...], sc_ref[...])
    for j in range(D // LANES):
        u_sc[j] = u[:, j * LANES:(j + 1) * LANES]
    rows16 = 16
    nh = S5_T // S5_GB
    for c in range(tm // (S5_T * rows16)):
        base = c * S5_T * rows16
        for j in range(D // LANES):
            sets = [[u_sc[j, pl.ds(base + h * S5_GB + s, rows16, stride=S5_T), :] for s in range(S5_GB)]
                    for h in range(nh)]
            for h, regs in enumerate(_block_transpose(sets)):
                for gl, t in enumerate(regs):
                    ug_ref[j * S5_GB + gl, c * rows16:(c + 1) * rows16, h * LANES:(h + 1) * LANES] = t.astype(BF16)


def _s5_pre(lay, x, mods):
    tm = lay.tile(512)
    assert tm % (S5_T * 16) == 0
    return pl.pallas_call(
        functools.partial(_s5_pre_kernel, tm),
        out_shape=jax.ShapeDtypeStruct((S5_G, lay.T // S5_T, S5_T * S5_Q), BF16),
        grid=(lay.T // tm,),
        in_specs=[pl.BlockSpec((tm, D), lambda i: (i, 0)), _mod_spec(lay, tm, 0, 1), _mod_spec(lay, tm, 1, 1)],
        out_specs=pl.BlockSpec((S5_G, tm // S5_T, S5_T * S5_Q), lambda i: (0, i, 0)),
        scratch_shapes=[pltpu.VMEM((D // LANES, tm, LANES), F32)],
        compiler_params=_cp("arbitrary"),
        name="s5_pre",
    )(x, mods, mods)


def _s5_glu_kernel(tm, x_ref, yg_ref, g_ref, wv_ref, wg_ref, o_ref, a_sc, y_sc):
    @pl.when(pl.program_id(1) == 0)
    def _():
        def sub_tile(c, carry):
            crow = pl.ds(pl.multiple_of(c * 8, 8), 8)
            base = c * (S5_T * 8)
            nh = S5_T // S5_GB
            for j2 in range(0, D // LANES, 2):
                keys = [(j, h) for j in (j2, j2 + 1) for h in range(nh)]
                sets = [[yg_ref[j * S5_GB + gl, crow, h * LANES:(h + 1) * LANES] for gl in range(S5_GB)]
                        for j, h in keys]
                for (j, h), regs in zip(keys, _block_transpose(sets)):
                    for s, t in enumerate(regs):
                        y_sc[j, pl.ds(base + h * S5_GB + s, 8, stride=S5_T), :] = t
            return carry

        lax.fori_loop(0, tm // (S5_T * 8), sub_tile, 0)
        for j in range(D // LANES):
            a_sc[:, j * LANES:(j + 1) * LANES] = jax.nn.gelu(y_sc[j]).astype(BF16)

    a = a_sc[...]
    val = jnp.dot(a, wv_ref[...], preferred_element_type=F32)
    gate = jnp.dot(a, wg_ref[...], preferred_element_type=F32)
    o_ref[...] = x_ref[...] + g_ref[...] * (val * jax.nn.sigmoid(gate))


def _s5_glu(lay, x, yg, mods, glu_w):
    tm = lay.tile(1024)
    assert tm % (S5_T * 8) == 0
    tn = 512
    nn = D // tn
    return pl.pallas_call(
        functools.partial(_s5_glu_kernel, tm),
        out_shape=jax.ShapeDtypeStruct((lay.T, D), F32),
        grid=(lay.T // tm, nn),
        in_specs=[pl.BlockSpec((tm, tn), lambda i, j: (i, j)),
                  pl.BlockSpec((S5_G, tm // S5_T, S5_T * S5_Q), lambda i, j: (0, i, 0)),
                  pl.BlockSpec((None, 1, tn), lambda i, j: (lay.group(i * tm) * MOD_CHUNKS + 2, 0, j)),
                  pl.BlockSpec((D, tn), lambda i, j: (0, j)),
                  pl.BlockSpec((D, tn), lambda i, j: (0, nn + j))],
        out_specs=pl.BlockSpec((tm, tn), lambda i, j: (i, j)),
        scratch_shapes=[pltpu.VMEM((tm, D), BF16), pltpu.VMEM((D // LANES, tm, LANES), F32)],
        compiler_params=_cp("arbitrary", "arbitrary"),
        name="s5_glu",
    )(x, yg, mods, glu_w, glu_w)


def _s5_layer(lay, x, mods, p, h0_re, h0_im):
    T, G, P, Q = S5_T, S5_G, S5_P, S5_Q
    tabs = _s5_tables(p['a_re'], p['a_im'], p['log_dt'], p['b_re'], p['b_im'], p['c_re'], p['c_im'], p['d'])
    R = lay.T // T
    ug = _s5_pre(lay, x, mods)
    yi, sre, sim = _s5_in(ug, tabs)
    hprev = tuple(jnp.zeros(sre.shape, F32) for _ in range(4))
    ncp, ncs = lay.L // T, lay.LS // T
    nsb = S5_SCAN_ROWS // ncp
    zero = jnp.zeros((lay.B // nsb, G, nsb, 2 * P), F32)
    hprev, fre, fim = _s5_scan(sre, sim, tabs, zero, zero, hprev, 0, lay.B, ncp, nsb, 1)
    to_lanes = lambda s: jnp.transpose(s, (0, 2, 1, 3)).reshape(lay.NS, G, 1, 2 * P)
    hprev, _, _ = _s5_scan(sre, sim, tabs, to_lanes(h0_re), to_lanes(h0_im), hprev,
                           lay.TP // T, lay.NS, ncs, 1, max(1, ncs // S5_SCAN_ROWS))
    yg = _s5_out(yi, hprev, tabs)
    x = _s5_glu(lay, x, yg, mods, p['glu_w'])
    from_lanes = lambda s: jnp.transpose(s, (0, 2, 1, 3)).reshape(lay.B, G, 2, P).transpose(0, 2, 1, 3)
    return x, from_lanes(fre), from_lanes(fim)


def _hyena_filters(L, p):
    mm = functools.partial(jnp.matmul, precision=HIGHEST)
    f = jnp.linspace(1e-4, HY_BANDS - 1, HY_BANDS, dtype=F32)[None, :]
    max_decay = math.log(HY_TARGET) / HY_FAST_PCT
    min_decay = math.log(HY_TARGET) / HY_SLOW_PCT
    deltas = jnp.abs(jnp.linspace(min_decay, max_decay, D, dtype=F32))
    w3 = p['f_w3'].reshape(-1, 2, 2, D)

    def side(pos, s):
        t = (pos.astype(F32) / (L - 1))[:, None]
        w = 2.0 * math.pi * pos.astype(F32)[:, None] / L
        feats = jnp.concatenate([t, jnp.cos(f * w), -jnp.sin(f * w)], axis=-1)
        z = jnp.sin(p['f_freq'][0] * (mm(feats, p['f_w1']) + p['f_b1']))
        z = jnp.sin(p['f_freq'][1] * (mm(z, p['f_w2']) + p['f_b2']))
        win = jnp.exp(-t * deltas)
        return jnp.stack([mm(z, w3[:, o, s]) * win for o in range(2)])

    j = jnp.arange(L, dtype=jnp.int32)
    k_lo = side(j, 0)
    k_hi = side((L - j) % L, 1) * (j > 0).astype(F32)[None, :, None]
    norm = jnp.sum(jnp.abs(k_lo), axis=1, keepdims=True) + jnp.sum(jnp.abs(k_hi), axis=1, keepdims=True)
    k_lo, k_hi = k_lo / norm, k_hi / norm
    alt = (1.0 - 2.0 * (j % 2).astype(F32))[None, :, None]
    k_ny = jnp.sum(alt * (k_lo + k_hi), axis=1) / (2 * L)
    return k_lo, k_hi, k_ny


def _dft_tables(L):
    r = math.isqrt(L)
    assert r * r == L
    t = jnp.arange(L, dtype=jnp.int32)[None, :]
    a = jnp.arange(r, dtype=jnp.int32)[:, None]

    def unit(idx):
        ang = (idx % (2 * L)).astype(F32) * (math.pi / L)
        return jnp.cos(ang), jnp.sin(ang)

    c1, s1 = unit(a * r * t)
    c2, s2 = unit(a * t)
    c1, s1, c2, s2 = c1[:, None], s1[:, None], c2[None], s2[None]
    cos = (c1 * c2 - s1 * s2).reshape(L, L)
    sin = (s1 * c2 + c1 * s2).reshape(L, L)
    return cos.astype(BF16), sin.astype(BF16)


def _hy_spec_kernel(L, tr, c_ref, s_ref, klo_ref, khi_ref, p_ref, q_ref):
    r = pl.program_id(2)
    f = r * tr + lax.broadcasted_iota(jnp.int32, (tr, 1), 0)
    sgn = (1 - 2 * (f % 2)).astype(F32)
    scale = jnp.where(f == 0, 1.0, 2.0) * (1.0 / (2 * L))
    c, s = c_ref[...], s_ref[...]
    lo, hi = klo_ref[...], khi_ref[...]
    dot = functools.partial(jnp.dot, preferred_element_type=F32)
    p_ref[...] = scale * (dot(c, lo) + sgn * dot(c, hi))
    q_ref[...] = scale * (dot(s, lo) + sgn * dot(s, hi))


def _hy_spectrum(L, cos, sin, k_lo, k_hi):
    tr = min(L, 512)
    tc = 512
    kspec = pl.BlockSpec((None, L, tc), lambda o, j, r: (o, 0, j))
    tspec = pl.BlockSpec((tr, L), lambda o, j, r: (r, 0))
    ospec = pl.BlockSpec((None, tr, tc), lambda o, j, r: (o, r, j))
    return pl.pallas_call(
        functools.partial(_hy_spec_kernel, L, tr),
        out_shape=(jax.ShapeDtypeStruct((2, L, D), F32), jax.ShapeDtypeStruct((2, L, D), F32)),
        grid=(2, D // tc, L // tr),
        in_specs=[tspec, tspec, kspec, kspec],
        out_specs=(ospec, ospec),
        compiler_params=_cp("arbitrary", "arbitrary", "arbitrary"),
        name="hy_spectrum",
    )(cos, sin, k_lo.astype(BF16), k_hi.astype(BF16))


def _hy_core_kernel(L, tr, ngrp, tc, x1_ref, x2_ref, v_ref, sw1_ref, sw2_ref, swv_ref, sb1_ref, sb2_ref,
                    sbv_ref, c_ref, s_ref, p_ref, q_ref, kny_ref, skip_ref, *rest):
    o_ref, z0_sc, z1_sc, x2_sc, a_sc, b_sc, ny0_sc, ny1_sc = rest[-8:]
    fused = tr == L
    ph = pl.program_id(2)
    r = pl.program_id(3)
    W = ngrp * tc

    def phase(k):
        return (lambda f: f()) if fused else pl.when(ph == k)
    tcv = min(L, 512)
    halo = 16

    def alt_sign(start, n):
        t = start + lax.broadcasted_iota(jnp.int32, (n, 1), 0)
        return (1 - 2 * (t % 2)).astype(F32)

    def conv3(src_ref, g, a, w_ref, b_ref):
        x = src_ref[g, pl.ds(a, tcv), :].astype(F32)
        row = lax.broadcasted_iota(jnp.int32, (tcv, 1), 0)
        up_at = pl.multiple_of(jnp.maximum(a - halo, 0), halo)
        dn_at = pl.multiple_of(jnp.minimum(a + tcv, L - halo), halo)
        up = src_ref[g, pl.ds(up_at, halo), :][halo - 1:halo, :].astype(F32)
        dn = src_ref[g, pl.ds(dn_at, halo), :][0:1, :].astype(F32)
        up = jnp.where(a > 0, up, 0.0)
        dn = jnp.where(a + tcv < L, dn, 0.0)
        prev = jnp.where(row == 0, up, pltpu.roll(x, 1, 0))
        nxt = jnp.where(row == tcv - 1, dn, pltpu.roll(x, tcv - 1, 0))
        return prev * w_ref[0:1, :] + x * w_ref[1:2, :] + nxt * w_ref[2:3, :] + b_ref[...]

    @(phase(0) if fused else pl.when(jnp.logical_and(ph == 0, r == 0)))
    def _():
        ny0_sc[...] = jnp.zeros_like(ny0_sc)

        def conv_tile(ti, carry):
            a = pl.multiple_of(ti * tcv, tcv)
            rows_a = pl.ds(a, tcv)
            for g in range(ngrp):
                cols = slice(g * tc, (g + 1) * tc)
                z1_sc[rows_a, cols] = conv3(x1_ref, g, a, sw1_ref, sb1_ref).astype(BF16)
                x2_sc[rows_a, cols] = conv3(x2_ref, g, a, sw2_ref, sb2_ref).astype(BF16)
                z0_sc[rows_a, cols] = conv3(v_ref, g, a, swv_ref, sbv_ref).astype(BF16)
            ny0_sc[...] += jnp.sum(alt_sign(a, tcv) * z0_sc[rows_a, :].astype(F32), axis=0, keepdims=True)
            return carry

        lax.fori_loop(0, L // tcv, conv_tile, 0)

    start = pl.multiple_of(r * tr, tr)
    rows = pl.ds(start, tr)
    dot = functools.partial(jnp.dot, preferred_element_type=F32)

    def forward(order, z_sc):
        z = z_sc[...]
        zre = dot(c_ref[...], z)
        zim = dot(s_ref[...], z)
        pw = jnp.concatenate([p_ref[order] if fused else p_ref[...]] * ngrp, axis=1)
        qw = jnp.concatenate([q_ref[order] if fused else q_ref[...]] * ngrp, axis=1)
        a_sc[rows, :] = (zre * pw - zim * qw).astype(BF16)
        b_sc[rows, :] = (zim * pw + zre * qw).astype(BF16)

    def inverse(order, z_sc, ny_sc):
        y = dot(c_ref[...], a_sc[...]) + dot(s_ref[...], b_sc[...])
        kny = jnp.concatenate([kny_ref[order:order + 1, :]] * ngrp, axis=1)
        skip = jnp.concatenate([skip_ref[order:order + 1, :]] * ngrp, axis=1)
        return y + alt_sign(start, tr) * (ny_sc[...] * kny) + skip * z_sc[rows, :].astype(F32)

    @phase(0)
    def _():
        forward(0, z0_sc)

    @phase(1)
    def _():
        z1 = (z1_sc[rows, :].astype(F32) * inverse(0, z0_sc, ny0_sc)).astype(BF16)
        z1_sc[rows, :] = z1

        @pl.when(r == 0)
        def _():
            ny1_sc[...] = jnp.zeros_like(ny1_sc)

        ny1_sc[...] += jnp.sum(alt_sign(start, tr) * z1.astype(F32), axis=0, keepdims=True)

    @phase(2)
    def _():
        forward(1, z1_sc)

    @phase(3)
    def _():
        out = x2_sc[rows, :].astype(F32) * inverse(1, z1_sc, ny1_sc)
        for g in range(ngrp):
            o_ref[g, rows, :] = out[:, g * tc:(g + 1) * tc].astype(o_ref.dtype)


def _hy_core(proj, short_w, short_b, skip, cos, sin, pq, k_ny, o_prev, row0, nseq, L, ngrp, tc):
    T = proj.shape[0]
    tr = min(L, 256)
    nrt = L // tr
    assert row0 % (L * ngrp) == 0 and nseq % ngrp == 0 and T % L == 0
    sb0 = row0 // (L * ngrp)
    nct = D // tc
    p3 = proj.reshape(T // L, L, 3 * D)
    p_arr, q_arr = pq

    def xspec(part):
        mode = {} if nrt == 1 else dict(pipeline_mode=pl.Buffered(1))
        return pl.BlockSpec((ngrp, L, tc), lambda i, j, ph, r: (sb0 + i, 0, part * nct + j), **mode)

    def wspec(part, rows_):
        return pl.BlockSpec((rows_, tc), lambda i, j, ph, r: (0, part * nct + j))

    fused = nrt == 1
    if fused:
        pq_spec = pl.BlockSpec((2, tr, tc), lambda i, j, ph, r: (0, 0, j))
    else:
        pq_spec = pl.BlockSpec((None, tr, tc),
                               lambda i, j, ph, r: (ph // 2, jnp.where(ph % 2 == 0, r, nrt - 1), j))

    tspec = pl.BlockSpec((tr, L), lambda i, j, ph, r: (r, 0))
    in_specs = [xspec(0), xspec(1), xspec(2), wspec(0, 3), wspec(1, 3), wspec(2, 3),
                wspec(0, 1), wspec(1, 1), wspec(2, 1), tspec, tspec, pq_spec, pq_spec,
                pl.BlockSpec((2, tc), lambda i, j, ph, r: (0, j)),
                pl.BlockSpec((2, tc), lambda i, j, ph, r: (0, j))]
    sb = short_b.reshape(1, 3 * D)
    args = [p3, p3, p3, short_w, short_w, short_w, sb, sb, sb, cos, sin, p_arr, q_arr, k_ny, skip]
    aliases = {}
    if o_prev is not None:
        in_specs.append(pl.BlockSpec(memory_space=pl.ANY))
        args.append(o_prev.reshape(T // L, L, D))
        aliases = {len(args) - 1: 0}
    W = ngrp * tc
    out = pl.pallas_call(
        functools.partial(_hy_core_kernel, L, tr, ngrp, tc),
        out_shape=jax.ShapeDtypeStruct((T // L, L, D), BF16),
        grid=(nseq // ngrp, nct, 1 if fused else 4, nrt),
        in_specs=in_specs,
        out_specs=pl.BlockSpec((ngrp, L, tc), lambda i, j, ph, r: (sb0 + i, 0, j)),
        scratch_shapes=[pltpu.VMEM((L, W), BF16)] * 5 + [pltpu.VMEM((1, W), F32)] * 2,
        input_output_aliases=aliases,
        compiler_params=_cp("arbitrary", "arbitrary", "arbitrary", "arbitrary"),
        name="hy_core",
    )(*args)
    return out.reshape(T, D)


FOLD_BLK = 256


def _hyena_filters_folded(L, p):
    H = L // 2
    mm = functools.partial(jnp.matmul, precision=HIGHEST)
    f = jnp.linspace(1e-4, HY_BANDS - 1, HY_BANDS, dtype=F32)[None, :]
    max_decay = math.log(HY_TARGET) / HY_FAST_PCT
    min_decay = math.log(HY_TARGET) / HY_SLOW_PCT
    deltas = jnp.abs(jnp.linspace(min_decay, max_decay, D, dtype=F32))
    w3 = p['f_w3'].reshape(-1, 2, 2, D)

    def side(pos, s):
        t = (pos.astype(F32) / (L - 1))[:, None]
        w = 2.0 * math.pi * pos.astype(F32)[:, None] / L
        feats = jnp.concatenate([t, jnp.cos(f * w), -jnp.sin(f * w)], axis=-1)
        z = jnp.sin(p['f_freq'][0] * (mm(feats, p['f_w1']) + p['f_b1']))
        z = jnp.sin(p['f_freq'][1] * (mm(z, p['f_w2']) + p['f_b2']))
        win = jnp.exp(-t * deltas)
        return jnp.stack([mm(z, w3[:, o, s]) * win for o in range(2)])

    t = jnp.arange(H, dtype=jnp.int32)
    pos = jnp.concatenate([t, (L - t) % L])
    mid = jnp.full((1,), H, jnp.int32)
    live = (t > 0).astype(F32)[None, :, None]
    s0 = side(pos, 0).reshape(2, 2, H, D)
    s1 = side(pos, 1).reshape(2, 2, H, D)
    klo_lo, klo_hr, klo_h = s0[:, 0], s0[:, 1] * live, side(mid, 0)[:, 0]
    khi_lo, khi_hr, khi_h = s1[:, 1] * live, s1[:, 0] * live, side(mid, 1)[:, 0]
    norm = sum(jnp.sum(jnp.abs(a), axis=1) for a in (klo_lo, klo_hr, khi_lo, khi_hr)) \
        + jnp.abs(klo_h) + jnp.abs(khi_h)
    alt = (1.0 - 2.0 * (t % 2).astype(F32))[None, :, None]
    alt_h = 1.0 - 2.0 * (H % 2)
    k_ny = (jnp.sum(alt * (klo_lo + klo_hr + khi_lo + khi_hr), axis=1) + alt_h * (klo_h + khi_h)) / norm / (2 * L)
    inv = (1.0 / norm)[:, None, :]
    p_lo, p_hr = (klo_lo + khi_lo) * inv, (klo_hr + khi_hr) * inv
    m_lo, m_hr = (klo_lo - khi_lo) * inv, (klo_hr - khi_hr) * inv
    xc = jnp.stack([p_lo + p_hr, m_lo - m_hr])
    xs = jnp.stack([p_lo - p_hr, m_lo + m_hr])
    xh = jnp.stack([(klo_h + khi_h) / norm, (klo_h - khi_h) / norm])
    return xc, xs, xh, k_ny


def _dft_tables_folded(L):
    H = L // 2
    ra = 1 << (int(math.log2(H)) // 2)
    rb = H // ra
    th = math.pi / L
    a = jnp.arange(ra, dtype=jnp.int32)[:, None]
    b = jnp.arange(rb, dtype=jnp.int32)[:, None]
    u = jnp.arange(H, dtype=jnp.int32)[None, :]

    def unit(idx):
        ang = (idx % (2 * L)).astype(F32) * th
        return jnp.cos(ang), jnp.sin(ang)

    e1 = [unit(2 * rb * a * u), unit(2 * rb * a * u), unit((2 * u + 1) * rb * a)]
    e2 = [unit(2 * b * u), unit((2 * b + 1) * u), unit((2 * u + 1) * b)]
    c1 = jnp.stack([e[0] for e in e1])[:, None, :, None, :]
    s1 = jnp.stack([e[1] for e in e1])[:, None, :, None, :]
    c2 = jnp.stack([e[0] for e in e2])[:, None, None, :, :]
    s2 = jnp.stack([e[1] for e in e2])[:, None, None, :, :]
    lead = jnp.concatenate([c1, s1], axis=1)
    cross = jnp.concatenate([-s1, c1], axis=1)
    return (lead * c2 + cross * s2).reshape(6, H, H).astype(BF16)


def _hy_spec_folded_kernel(L, tr, c_ref, s_ref, xc_ref, xs_ref, xh_ref, p_ref, q_ref):
    half = pl.program_id(2)
    r = pl.program_id(3)
    m = r * tr + lax.broadcasted_iota(jnp.int32, (tr, 1), 0)
    alt = (1 - 2 * (m % 2)).astype(F32)
    scale = jnp.where(jnp.logical_and(half == 0, m == 0), 1.0, 2.0) * (1.0 / (2 * L))
    mid = alt * xh_ref[...]
    kc = jnp.dot(c_ref[...], xc_ref[...], preferred_element_type=F32) + jnp.where(half == 0, mid, 0.0)
    ks = jnp.dot(s_ref[...], xs_ref[...], preferred_element_type=F32) + jnp.where(half == 1, mid, 0.0)
    p_ref[...] = scale * kc
    q_ref[...] = scale * ks


def _hy_spectrum_folded(L, tabs, xc, xs, xh):
    H = L // 2
    tr = min(H, 512)
    tc = 512
    nrt = H // tr
    xspec = pl.BlockSpec((None, None, H, tc), lambda o, j, hf, r: (hf, o, 0, j))
    hspec = pl.BlockSpec((None, None, 1, tc), lambda o, j, hf, r: (hf, o, 0, j))
    cspec = pl.BlockSpec((None, tr, H), lambda o, j, hf, r: (2 * hf, r, 0))
    sspec = pl.BlockSpec((None, tr, H), lambda o, j, hf, r: (2 * hf + 1, r, 0))
    ospec = pl.BlockSpec((None, tr, tc), lambda o, j, hf, r: (o, hf * nrt + r, j))
    return pl.pallas_call(
        functools.partial(_hy_spec_folded_kernel, L, tr),
        out_shape=(jax.ShapeDtypeStruct((2, L, D), F32), jax.ShapeDtypeStruct((2, L, D), F32)),
        grid=(2, D // tc, 2, nrt),
        in_specs=[cspec, sspec, xspec, xspec, hspec],
        out_specs=(ospec, ospec),
        compiler_params=_cp("arbitrary", "arbitrary", "arbitrary", "arbitrary"),
        name="hy_spectrum",
    )(tabs, tabs, xc.astype(BF16), xs.astype(BF16), xh[:, :, None, :])


def _hy_fold_kernel(L, tr, ngrp, tc, x1_ref, x2_ref, v_ref, sw1_ref, sw2_ref, swv_ref, sb1_ref, sb2_ref,
                    sbv_ref, ce_ref, se_ref, co_ref, so_ref, pe_ref, po_ref, qe_ref, qo_ref, kny_ref, skip_ref,
                    *rest):
    o_ref, z0_sc, z1_sc, x2_sc, a_sc, b_sc, mid_sc = rest[-7:]
    H = L // 2
    nrt = H // tr
    nb = H // FOLD_BLK
    W = ngrp * tc
    ph = pl.program_id(2)
    r = pl.program_id(3)
    tcv = min(L, 512)
    halo = 16
    dot = functools.partial(jnp.dot, preferred_element_type=F32)
    ZH, X1H, X2H, YSP, NY, NYACC = range(6)
    alt_h = 1.0 - 2.0 * (H % 2)

    def alt_sign(start, n):
        t = start + lax.broadcasted_iota(jnp.int32, (n, 1), 0)
        return (1 - 2 * (t % 2)).astype(F32)

    def conv3(src_ref, g, a, w_ref, b_ref):
        x = src_ref[g, pl.ds(a, tcv), :].astype(F32)
        row = lax.broadcasted_iota(jnp.int32, (tcv, 1), 0)
        up_at = pl.multiple_of(jnp.maximum(a - halo, 0), halo)
        dn_at = pl.multiple_of(jnp.minimum(a + tcv, L - halo), halo)
        up = src_ref[g, pl.ds(up_at, halo), :][halo - 1:halo, :].astype(F32)
        dn = src_ref[g, pl.ds(dn_at, halo), :][0:1, :].astype(F32)
        up = jnp.where(a > 0, up, 0.0)
        dn = jnp.where(a + tcv < L, dn, 0.0)
        prev = jnp.where(row == 0, up, pltpu.roll(x, 1, 0))
        nxt = jnp.where(row == tcv - 1, dn, pltpu.roll(x, tcv - 1, 0))
        return prev * w_ref[0:1, :] + x * w_ref[1:2, :] + nxt * w_ref[2:3, :] + b_ref[...]

    def flip_mats():
        u = lax.broadcasted_iota(jnp.int32, (FOLD_BLK, FOLD_BLK), 0)
        v = lax.broadcasted_iota(jnp.int32, (FOLD_BLK, FOLD_BLK), 1)
        jshift = jnp.where(jnp.logical_and(u >= 1, v == FOLD_BLK - u), 1.0, 0.0).astype(BF16)
        e0 = jnp.where(jnp.logical_and(u == 0, v == 0), 1.0, 0.0).astype(BF16)
        return jshift, e0

    def flipped_block(src, b, jshift, e0):
        blk = lambda k: src[H + k * FOLD_BLK:H + (k + 1) * FOLD_BLK, :]
        out = dot(jshift, blk(nb - 1 - b))
        if b >= 1:
            out = out + dot(e0, blk(nb - b))
        return out

    def fold_in_place(sc, stage, off, jshift, e0):
        for b in range(nb):
            stage[off + b * FOLD_BLK:off + (b + 1) * FOLD_BLK, :] = flipped_block(sc, b, jshift, e0).astype(BF16)

        def copy_back(b, carry):
            at = pl.multiple_of(b * FOLD_BLK, FOLD_BLK)
            sc[pl.ds(H + at, FOLD_BLK), :] = stage[pl.ds(off + at, FOLD_BLK), :]
            return carry

        lax.fori_loop(0, nb, copy_back, 0)

    def to_operands(sc):
        def tile(i, acc):
            a = pl.multiple_of(i * FOLD_BLK, FOLD_BLK)
            ra, rb = pl.ds(a, FOLD_BLK), pl.ds(pl.multiple_of(H + a, FOLD_BLK), FOLD_BLK)
            lo = sc[ra, :].astype(F32)
            hr = sc[rb, :].astype(F32)
            zs = (lo + hr).astype(BF16)
            sc[ra, :] = zs
            sc[rb, :] = (lo - hr).astype(BF16)
            return acc + jnp.sum(alt_sign(a, FOLD_BLK) * zs.astype(F32), axis=0, keepdims=True)

        return lax.fori_loop(0, nb, tile, jnp.zeros((1, W), F32))

    def row_of(sc, at):
        return sc[at:at + halo, :][0:1, :].astype(F32)

    @pl.when(jnp.logical_and(ph == 0, r == 0))
    def _():
        def conv_tile(ti, carry):
            a = pl.multiple_of(ti * tcv, tcv)
            rows_a = pl.ds(a, tcv)
            for g in range(ngrp):
                cols = slice(g * tc, (g + 1) * tc)
                z1_sc[rows_a, cols] = conv3(x1_ref, g, a, sw1_ref, sb1_ref).astype(BF16)
                x2_sc[rows_a, cols] = conv3(x2_ref, g, a, sw2_ref, sb2_ref).astype(BF16)
                z0_sc[rows_a, cols] = conv3(v_ref, g, a, swv_ref, sbv_ref).astype(BF16)
            return carry

        lax.fori_loop(0, L // tcv, conv_tile, 0)
        mid_sc[ZH:ZH + 1, :] = row_of(z0_sc, H)
        mid_sc[X1H:X1H + 1, :] = row_of(z1_sc, H)
        mid_sc[X2H:X2H + 1, :] = row_of(x2_sc, H)
        jshift, e0 = flip_mats()
        for sc, stage, off in ((z0_sc, a_sc, 0), (z1_sc, a_sc, H), (x2_sc, b_sc, 0)):
            fold_in_place(sc, stage, off, jshift, e0)
        mid_sc[NY:NY + 1, :] = to_operands(z0_sc) + alt_h * mid_sc[ZH:ZH + 1, :]
        mid_sc[YSP:YSP + 1, :] = jnp.zeros((1, W), F32)
        mid_sc[NYACC:NYACC + 1, :] = jnp.zeros((1, W), F32)

    start = pl.multiple_of(r * tr, tr)
    rows = pl.ds(start, tr)
    rows_hi = pl.ds(pl.multiple_of(H + r * tr, tr), tr)
    rep = lambda ref: jnp.concatenate([ref[...]] * ngrp, axis=1)
    rep_row = lambda ref, o: jnp.concatenate([ref[o:o + 1, :]] * ngrp, axis=1)

    def forward(z_sc):
        zs, zd = z_sc[0:H, :], z_sc[H:L, :]
        mid = alt_sign(start, tr) * mid_sc[ZH:ZH + 1, :]
        zre_e = dot(ce_ref[...], zs) + mid
        zim_e = dot(se_ref[...], zd)
        zre_o = dot(co_ref[...], zd)
        zim_o = dot(so_ref[...], zs) + mid
        pe, qe, po, qo = rep(pe_ref), rep(qe_ref), rep(po_ref), rep(qo_ref)
        a_e = zre_e * pe - zim_e * qe
        b_e = zim_e * pe + zre_e * qe
        a_o = zre_o * po - zim_o * qo
        b_o = zim_o * po + zre_o * qo
        a_sc[rows, :] = a_e.astype(BF16)
        b_sc[rows, :] = b_e.astype(BF16)
        a_sc[rows_hi, :] = a_o.astype(BF16)
        b_sc[rows_hi, :] = b_o.astype(BF16)
        mid_sc[YSP:YSP + 1, :] += jnp.sum(alt_sign(start, tr) * (a_e + b_o), axis=0, keepdims=True)

    def inverse(order, zin_sc):
        pe = dot(ce_ref[...], a_sc[0:H, :])
        qe = dot(se_ref[...], b_sc[0:H, :])
        po = dot(co_ref[...], a_sc[H:L, :])
        qo = dot(so_ref[...], b_sc[H:L, :])
        c = alt_sign(start, tr) * (mid_sc[NY:NY + 1, :] * rep_row(kny_ref, order))
        half_skip = 0.5 * rep_row(skip_ref, order)
        zs = zin_sc[rows, :].astype(F32)
        zd = zin_sc[rows_hi, :].astype(F32)
        y_lo = pe + qe + po + qo + c + half_skip * (zs + zd)
        y_hr = pe - qe - po + qo + c + half_skip * (zs - zd)
        return y_lo, y_hr

    def middle(order):
        return (mid_sc[YSP:YSP + 1, :] + alt_h * mid_sc[NY:NY + 1, :] * rep_row(kny_ref, order)
                + rep_row(skip_ref, order) * mid_sc[ZH:ZH + 1, :])

    @pl.when(ph == 0)
    def _():
        forward(z0_sc)

    @pl.when(ph == 1)
    def _():
        y_lo, y_hr = inverse(0, z0_sc)
        z_lo = (z1_sc[rows, :].astype(F32) * y_lo).astype(BF16)
        z_hr = (z1_sc[rows_hi, :].astype(F32) * y_hr).astype(BF16)
        z1_sc[rows, :] = z_lo
        z1_sc[rows_hi, :] = z_hr
        mid_sc[NYACC:NYACC + 1, :] += jnp.sum(alt_sign(start, tr) * (z_lo.astype(F32) + z_hr.astype(F32)),
                                              axis=0, keepdims=True)

        @pl.when(r == nrt - 1)
        def _():
            z1h = (mid_sc[X1H:X1H + 1, :] * middle(0)).astype(BF16).astype(F32)
            mid_sc[ZH:ZH + 1, :] = z1h
            mid_sc[NY:NY + 1, :] = mid_sc[NYACC:NYACC + 1, :] + alt_h * z1h
            mid_sc[YSP:YSP + 1, :] = jnp.zeros((1, W), F32)

    @pl.when(ph == 2)
    def _():
        @pl.when(r == 0)
        def _():
            to_operands(z1_sc)

        forward(z1_sc)

    @pl.when(ph == 3)
    def _():
        y_lo, y_hr = inverse(1, z1_sc)
        z0_sc[rows, :] = (x2_sc[rows, :].astype(F32) * y_lo).astype(BF16)
        z0_sc[rows_hi, :] = (x2_sc[rows_hi, :].astype(F32) * y_hr).astype(BF16)

        @pl.when(r == nrt - 1)
        def _():
            out_h = mid_sc[X2H:X2H + 1, :] * middle(1)
            jshift, e0 = flip_mats()
            first = (lax.broadcasted_iota(jnp.int32, (FOLD_BLK, 1), 0) == 0).astype(F32)
            for b in range(nb):
                up = flipped_block(z0_sc, b, jshift, e0)
                if b == 0:
                    up = up + first * out_h
                for g in range(ngrp):
                    cols = slice(g * tc, (g + 1) * tc)
                    o_ref[g, H + b * FOLD_BLK:H + (b + 1) * FOLD_BLK, :] = up[:, cols].astype(o_ref.dtype)
            for b in range(nb):
                lo_rows = slice(b * FOLD_BLK, (b + 1) * FOLD_BLK)
                for g in range(ngrp):
                    o_ref[g, lo_rows, :] = z0_sc[lo_rows, g * tc:(g + 1) * tc]


def _hy_core_folded(proj, short_w, short_b, skip, tabs, pq, k_ny, o_prev, row0, nseq, L, ngrp, tc):
    T = proj.shape[0]
    H = L // 2
    tr = min(H, 256)
    nrt = H // tr
    assert H % FOLD_BLK == 0 and row0 % (L * ngrp) == 0 and nseq % ngrp == 0 and T % L == 0
    sb0 = row0 // (L * ngrp)
    nct = D // tc
    p3 = proj.reshape(T // L, L, 3 * D)
    p_arr, q_arr = pq
    once = dict(pipeline_mode=pl.Buffered(1))

    def xspec(part):
        return pl.BlockSpec((ngrp, L, tc), lambda i, j, ph, r: (sb0 + i, 0, part * nct + j), **once)

    def wspec(part, rows_):
        return pl.BlockSpec((rows_, tc), lambda i, j, ph, r: (0, part * nct + j))

    def tspec(k_fwd, k_inv):
        return pl.BlockSpec((None, tr, H), lambda i, j, ph, r: (jnp.where(ph % 2 == 0, k_fwd, k_inv), r, 0))

    def pqspec(half):
        def imap(i, j, ph, r):
            return (ph // 2, half * nrt + jnp.where(ph % 2 == 0, r, nrt - 1), j)
        return pl.BlockSpec((None, tr, tc), imap)

    in_specs = [xspec(0), xspec(1), xspec(2), wspec(0, 3), wspec(1, 3), wspec(2, 3),
                wspec(0, 1), wspec(1, 1), wspec(2, 1),
                tspec(0, 0), tspec(1, 1), tspec(2, 4), tspec(3, 5),
                pqspec(0), pqspec(1), pqspec(0), pqspec(1),
                pl.BlockSpec((2, tc), lambda i, j, ph, r: (0, j)),
                pl.BlockSpec((2, tc), lambda i, j, ph, r: (0, j))]
    sb = short_b.reshape(1, 3 * D)
    args = [p3, p3, p3, short_w, short_w, short_w, sb, sb, sb, tabs, tabs, tabs, tabs,
            p_arr, p_arr, q_arr, q_arr, k_ny, skip]
    in_specs.append(pl.BlockSpec(memory_space=pl.ANY))
    args.append(o_prev.reshape(T // L, L, D))
    aliases = {len(args) - 1: 0}
    W = ngrp * tc
    out = pl.pallas_call(
        functools.partial(_hy_fold_kernel, L, tr, ngrp, tc),
        out_shape=jax.ShapeDtypeStruct((T // L, L, D), BF16),
        grid=(nseq // ngrp, nct, 4, nrt),
        in_specs=in_specs,
        out_specs=pl.BlockSpec((ngrp, L, tc), lambda i, j, ph, r: (sb0 + i, 0, j)),
        scratch_shapes=[pltpu.VMEM((L, W), BF16)] * 5 + [pltpu.VMEM((8, W), F32)],
        input_output_aliases=aliases,
        compiler_params=_cp("arbitrary", "arbitrary", "arbitrary", "arbitrary"),
        name="hy_core",
    )(*args)
    return out.reshape(T, D)


def _plain_out_kernel(x_ref, z_ref, g_ref, w_ref, b_ref, o_ref):
    acc = jnp.dot(z_ref[...], w_ref[...], preferred_element_type=F32) + b_ref[...]
    o_ref[...] = x_ref[...] + g_ref[...] * acc


def _plain_out(lay, x, mods, z, w, b):
    tm = lay.tile(1024)
    tn = 512
    kdim = z.shape[1]
    return pl.pallas_call(
        _plain_out_kernel,
        out_shape=jax.ShapeDtypeStruct((lay.T, D), F32),
        grid=(lay.T // tm, D // tn),
        in_specs=[pl.BlockSpec((tm, tn), lambda i, j: (i, j)),
                  pl.BlockSpec((tm, kdim), lambda i, j: (i, 0)),
                  pl.BlockSpec((None, 1, tn), lambda i, j: (lay.group(i * tm) * MOD_CHUNKS + 2, 0, j)),
                  pl.BlockSpec((kdim, tn), lambda i, j: (0, j)),
                  pl.BlockSpec((1, tn), lambda i, j: (0, j))],
        out_specs=pl.BlockSpec((tm, tn), lambda i, j: (i, j)),
        compiler_params=_cp("arbitrary", "arbitrary"),
        name="plain_out",
    )(x, z, mods, w, b.reshape(1, D))


def _hyena_layer(lay, x, mods, p):
    proj = _proj(lay, x, mods, 0, 1, p['w_in'].astype(BF16), p['b_in'].reshape(1, 3 * D), 3 * D, tm=512)
    z = jnp.zeros((lay.T, D), BF16)
    for row0, nseq, L, ngrp, tc in ((0, lay.B, lay.L, math.gcd(lay.B, 4), 256),
                                    (lay.TP, lay.NS, lay.LS, lay.NS, 256)):
        conv = (proj, p['short_w'], p['short_b'], p['skip'])
        if (L // 2) % FOLD_BLK == 0 and L > 2 * FOLD_BLK:
            xc, xs, xh, k_ny = _hyena_filters_folded(L, p)
            tabs = _dft_tables_folded(L)
            pq = _hy_spectrum_folded(L, tabs, xc, xs, xh)
            z = _hy_core_folded(*conv, tabs, pq, k_ny, z, row0, nseq, L, ngrp, tc)
        else:
            k_lo, k_hi, k_ny = _hyena_filters(L, p)
            cos, sin = _dft_tables(L)
            pq = _hy_spectrum(L, cos, sin, k_lo, k_hi)
            z = _hy_core(*conv, cos, sin, pq, k_ny, z, row0, nseq, L, ngrp, tc)
    return _plain_out(lay, x, mods, z, p['w_out'].astype(BF16), p['b_out'])


_NT = (((1,), (1,)), ((), ()))
_TN = (((0,), (0,)), ((), ()))


def _tri(dr):
    t = lax.broadcasted_iota(jnp.int32, (CHUNK, CHUNK), 0)
    s = lax.broadcasted_iota(jnp.int32, (CHUNK, CHUNK), 1)
    return (s <= t) if dr == 0 else (s >= t)


def _chunk_cumsum(g, dr):
    n = g.shape[0]
    pos = lax.broadcasted_iota(jnp.int32, g.shape, 0) % CHUNK
    sh = 1
    while sh < CHUNK:
        if dr == 0:
            g = g + jnp.where(pos >= sh, pltpu.roll(g, sh, 0), 0.0)
        else:
            g = g + jnp.where(pos < CHUNK - sh, pltpu.roll(g, n - sh, 0), 0.0)
        sh *= 2
    return g


def _head_epilogue(o_sc, gate_ref, ng_ref, a_ref, center):
    rows = o_sc.shape[0]
    tr = math.gcd(rows, 256)

    def tile(i, carry):
        r = pl.ds(pl.multiple_of(i * tr, tr), tr)
        o = o_sc[r, :]
        if center:
            o = o - jnp.mean(o, axis=-1, keepdims=True)
        o = o * lax.rsqrt(jnp.mean(o * o, axis=-1, keepdims=True) + RMS_EPS) * ng_ref[...]
        a_ref[r, :] = (o * _silu(gate_ref[r, :].astype(F32))).astype(a_ref.dtype)
        return carry

    lax.fori_loop(0, rows // tr, tile, 0)


def _gla_kernel(cps, nseg, U, has_s0, want_final, *refs):
    q_ref, k_ref, v_ref, lr_ref, w2f_ref, w2b_ref, gb_ref, gate_ref, ng_ref = refs[:9]
    s0_ref = refs[9] if has_s0 else None
    qin_sc, kin_sc, kout_sc, dec_sc, st_sc, s_sc, s0t_sc, o_ref = refs[-8:]
    outs = refs[-10:-8] if want_final else refs[-9:-8]
    a_ref = outs[0]
    sf_ref = outs[1] if want_final else None
    C = CHUNK
    nsc = cps // U
    nchunks = nseg * cps
    rows_total = nchunks * C
    w2 = (w2f_ref, w2b_ref)

    for dr in range(2):
        pre = jnp.dot(lr_ref[...], w2[dr][...], preferred_element_type=F32) + gb_ref[dr:dr + 1, :]
        g = (jnp.minimum(pre, 0.0) - jnp.log(1.0 + jnp.exp(-jnp.abs(pre)))) * (1.0 / GLA_TAU)
        b = _chunk_cumsum(g, dr)
        b3 = b.reshape(nchunks, C, GLA_DK)
        tot = b3[:, C - 1:C, :] if dr == 0 else b3[:, 0:1, :]
        dec_sc[...] = jnp.exp(tot).reshape(nchunks, GLA_DK)
        k = k_ref[...].astype(F32)
        qin_sc[...] = (q_ref[...].astype(F32) * (GLA_DK ** -0.5) * jnp.exp(b)).astype(BF16)
        kin_sc[...] = (k * jnp.exp(-b)).astype(BF16)
        kout_sc[...] = (k * jnp.exp(tot - b3).reshape(rows_total, GLA_DK)).astype(BF16)
        if has_s0:
            s0t_sc[...] = jnp.transpose(s0_ref[dr], (1, 0))
        tri = _tri(dr)

        def super_chunk(jj, carry, dr=dr, tri=tri):
            j = jj if dr == 0 else nseg * nsc - 1 - jj
            in_seg = j % nsc
            first = (in_seg == 0) if dr == 0 else (in_seg == nsc - 1)
            last = (in_seg == nsc - 1) if dr == 0 else (in_seg == 0)

            @pl.when(first)
            def _():
                s_sc[...] = s0t_sc[...] if has_s0 else jnp.zeros_like(s_sc)

            base = j * (U * C)
            for u in range(U):
                rows = pl.ds(pl.multiple_of(base + u * C, C), C)
                v = v_ref[rows, :]
                sc = lax.dot_general(qin_sc[rows, :], kin_sc[rows, :], _NT, preferred_element_type=F32)
                o = jnp.dot(jnp.where(tri, sc, 0.0).astype(BF16), v, preferred_element_type=F32)
                st_sc[u] = lax.dot_general(v, kout_sc[rows, :], _TN, preferred_element_type=F32)
                if dr == 0:
                    o_ref[rows, :] = o
                else:
                    o_ref[rows, :] += o
            s = s_sc[...]
            for u in (range(U) if dr == 0 else reversed(range(U))):
                kv = st_sc[u]
                st_sc[u] = s
                s = dec_sc[pl.ds(j * U + u, 1), :] * s + kv
            s_sc[...] = s
            for u in range(U):
                rows = pl.ds(pl.multiple_of(base + u * C, C), C)
                o_ref[rows, :] += lax.dot_general(qin_sc[rows, :], st_sc[u].astype(BF16), _NT,
                                                  preferred_element_type=F32)
            if want_final:
                @pl.when(last)
                def _():
                    sf_ref[j // nsc, dr] = jnp.transpose(s, (1, 0))
            return carry

        lax.fori_loop(0, nseg * nsc, super_chunk, 0)

    _head_epilogue(o_ref, gate_ref, ng_ref, a_ref, center=False)


def _gla_core(proj, w2f, w2b, gate_b, norm_g, s0, o_prev, row0, nseq, seqlen, nseg, want_final):
    T = proj.shape[0]
    rows = nseg * seqlen
    cps = seqlen // CHUNK
    U = math.gcd(cps, 8)
    assert row0 % rows == 0 and seqlen % CHUNK == 0 and nseq % nseg == 0
    rb = row0 // rows
    hk = GLA_H * GLA_DK
    has_s0 = s0 is not None
    assert not has_s0 or nseg == 1
    in_specs = [pl.BlockSpec((rows, GLA_DK), lambda b, h: (rb + b, h)),
                pl.BlockSpec((rows, GLA_DK), lambda b, h: (rb + b, GLA_H + h)),
                pl.BlockSpec((rows, GLA_DV), lambda b, h: (rb + b, 2 * hk // GLA_DV + h)),
                pl.BlockSpec((rows, 128), lambda b, h: (rb + b, (2 * hk + 2 * GLA_H * GLA_DV) // 128)),
                pl.BlockSpec((128, GLA_DK), lambda b, h: (0, h)),
                pl.BlockSpec((128, GLA_DK), lambda b, h: (0, h)),
                pl.BlockSpec((2, GLA_DK), lambda b, h: (0, h)),
                pl.BlockSpec((rows, GLA_DV), lambda b, h: (rb + b, (2 * hk) // GLA_DV + GLA_H + h)),
                pl.BlockSpec((1, GLA_DV), lambda b, h: (0, 0))]
    args = [proj, proj, proj, proj, w2f, w2b, gate_b, proj, norm_g.reshape(1, GLA_DV)]
    if has_s0:
        in_specs.append(pl.BlockSpec((None, 2, None, GLA_DK, GLA_DV), lambda b, h: (b, 0, h, 0, 0)))
        args.append(s0)
    in_specs.append(pl.BlockSpec(memory_space=pl.ANY))
    args.append(o_prev)
    aliases = {len(args) - 1: 0}
    out_shape = [jax.ShapeDtypeStruct((T, GLA_H * GLA_DV), BF16)]
    out_specs = [pl.BlockSpec((rows, GLA_DV), lambda b, h: (rb + b, h))]
    if want_final:
        out_shape.append(jax.ShapeDtypeStruct((nseq, 2, GLA_H, GLA_DK, GLA_DV), F32))
        out_specs.append(pl.BlockSpec((nseg, 2, None, GLA_DK, GLA_DV), lambda b, h: (b, 0, h, 0, 0)))
    outs = pl.pallas_call(
        functools.partial(_gla_kernel, cps, nseg, U, has_s0, want_final),
        out_shape=tuple(out_shape),
        grid=(nseq // nseg, GLA_H),
        in_specs=in_specs,
        out_specs=tuple(out_specs),
        scratch_shapes=[pltpu.VMEM((rows, GLA_DK), BF16)] * 3
        + [pltpu.VMEM((nseg * cps, GLA_DK), F32), pltpu.VMEM((U, GLA_DV, GLA_DK), F32),
           pltpu.VMEM((GLA_DV, GLA_DK), F32), pltpu.VMEM((GLA_DV, GLA_DK), F32),
           pltpu.VMEM((rows, GLA_DV), F32)],
        input_output_aliases=aliases,
        compiler_params=_cp("arbitrary", "arbitrary"),
        name="gla",
    )(*args)
    return (outs[0], outs[1]) if want_final else (outs[0], None)


def _ret_kernel(cps, nseg, U, has_s0, want_final, rope, *refs):
    q_ref, k_ref, v_ref, dm_ref, qd_ref, kd_ref, cd_ref, gate_ref, ng_ref = refs[:9]
    nxt = 9
    if rope:
        cos_ref, sin_ref = refs[9:11]
        nxt = 11
    s0_ref = refs[nxt] if has_s0 else None
    qr_sc, kr_sc, qd_sc, kd_sc, st_sc, s_sc, o_ref = refs[-7:]
    outs = refs[-9:-7] if want_final else refs[-8:-7]
    a_ref = outs[0]
    sf_ref = outs[1] if want_final else None
    C = RET_CHUNK
    nsc = cps // U
    R = U * C
    SB = 64

    def rot(x, rows):
        if not rope:
            return x
        half = x.shape[1] // 2
        swapped = jnp.concatenate([pltpu.roll(x[:, :half], half // 2, 1),
                                   pltpu.roll(x[:, half:], half // 2, 1)], axis=1)
        return x * cos_ref[rows, :] + swapped * sin_ref[rows, :]

    for dr in range(2):
        def super_chunk(jj, carry, dr=dr):
            j = jj if dr == 0 else nseg * nsc - 1 - jj
            in_seg = j % nsc
            first = (in_seg == 0) if dr == 0 else (in_seg == nsc - 1)
            last = (in_seg == nsc - 1) if dr == 0 else (in_seg == 0)

            @pl.when(first)
            def _():
                s_sc[...] = s0_ref[dr] if has_s0 else jnp.zeros_like(s_sc)

            base = pl.multiple_of(j * R, R)
            rows_r = pl.ds(base, R)
            q = rot(q_ref[rows_r, :].astype(F32), rows_r)
            k = rot(k_ref[rows_r, :].astype(F32), rows_r) * (RET_DK ** -0.5)
            qr_sc[...] = q.astype(BF16)
            kr_sc[...] = k.astype(BF16)
            qd_sc[...] = (q.reshape(U, C, RET_DK) * qd_ref[dr][None]).reshape(R, RET_DK).astype(BF16)
            kd_sc[...] = (k.reshape(U, C, RET_DK) * kd_ref[dr][None]).reshape(R, RET_DK).astype(BF16)
            for u in range(U):
                loc = pl.ds(u * C, C)
                rows = pl.ds(pl.multiple_of(base + u * C, C), C)
                v = v_ref[rows, :]
                sc = lax.dot_general(qr_sc[loc, :], kr_sc[loc, :], _NT, preferred_element_type=F32)
                o = jnp.dot((sc * dm_ref[dr]).astype(BF16), v, preferred_element_type=F32)
                st_sc[u] = lax.dot_general(kd_sc[loc, :], v, _TN, preferred_element_type=F32)
                if dr == 0:
                    o_ref[rows, :] = o
                else:
                    o_ref[rows, :] += o
            cd = cd_ref[dr]
            for r0 in range(0, RET_DK, SB):
                srows = pl.ds(r0, SB)
                s = s_sc[srows, :]
                for u in (range(U) if dr == 0 else reversed(range(U))):
                    kv = st_sc[u, srows, :]
                    st_sc[u, srows, :] = s
                    s = cd * s + kv
                s_sc[srows, :] = s
            for u in range(U):
                rows = pl.ds(pl.multiple_of(base + u * C, C), C)
                o_ref[rows, :] += jnp.dot(qd_sc[pl.ds(u * C, C), :], st_sc[u].astype(BF16),
                                          preferred_element_type=F32)
            if want_final:
                @pl.when(last)
                def _():
                    sf_ref[j // nsc, dr] = s_sc[...]
            return carry

        lax.fori_loop(0, nseg * nsc, super_chunk, 0)

    _head_epilogue(o_ref, gate_ref, ng_ref, a_ref, center=True)


def _ret_tables(log_decay):
    C = RET_CHUNK
    lg = log_decay.astype(F32)[:, :, None, None]
    t = jnp.arange(C, dtype=F32)[:, None]
    s = jnp.arange(C, dtype=F32)[None, :]
    lag = jnp.stack([t - s, s - t])[:, None]
    dmask = jnp.where(lag >= 0, jnp.exp(jnp.maximum(lag, 0.0) * lg), 0.0)
    tl = jnp.arange(C, dtype=F32)[None, None, :, None]
    qdec = jnp.concatenate([jnp.exp((tl + 1.0) * lg[0:1]), jnp.exp((C - tl) * lg[1:2])], axis=0)
    kdec = jnp.concatenate([jnp.exp((C - 1.0 - tl) * lg[0:1]), jnp.exp(tl * lg[1:2])], axis=0)
    cdec = jnp.exp(C * lg)
    return dmask, qdec, kdec, cdec


def _rope_tables(seqlen, dk):
    half = dk // 2
    nf = half // 2
    pos = jnp.arange(seqlen, dtype=jnp.int32)
    inv = ROPE_BASE ** (-jnp.arange(nf, dtype=F32) / nf)
    ang_r = (pos // GRID_W).astype(F32)[:, None] * inv[None, :]
    ang_c = (pos % GRID_W).astype(F32)[:, None] * inv[None, :]
    cos = jnp.concatenate([jnp.cos(ang_r)] * 2 + [jnp.cos(ang_c)] * 2, axis=1)
    sin = jnp.concatenate([-jnp.sin(ang_r), jnp.sin(ang_r), -jnp.sin(ang_c), jnp.sin(ang_c)], axis=1)
    return cos, sin


def _ret_core(proj, tabs, norm_g, s0, o_prev, row0, nseq, seqlen, nseg, want_final, rope):
    T = proj.shape[0]
    rows = nseg * seqlen
    C = RET_CHUNK
    cps = seqlen // C
    U = math.gcd(cps, 4)
    assert row0 % rows == 0 and seqlen % C == 0 and nseq % nseg == 0
    rb = row0 // rows
    hk, hv = RET_H * RET_DK, RET_H * RET_DV
    has_s0 = s0 is not None
    assert not (has_s0 or rope) or nseg == 1
    tspec = lambda r, c: pl.BlockSpec((2, None, r, c), lambda b, h: (0, h, 0, 0))
    mode = dict(pipeline_mode=pl.Buffered(1)) if rows * RET_DV * 2 >= (4 << 20) else {}
    in_specs = [pl.BlockSpec((rows, RET_DK), lambda b, h: (rb + b, h), **mode),
                pl.BlockSpec((rows, RET_DK), lambda b, h: (rb + b, RET_H + h), **mode),
                pl.BlockSpec((rows, RET_DV), lambda b, h: (rb + b, 2 * hk // RET_DV + h), **mode),
                tspec(C, C), tspec(C, 1), tspec(C, 1), tspec(1, 1),
                pl.BlockSpec((rows, RET_DV), lambda b, h: (rb + b, (2 * hk + hv) // RET_DV + h), **mode),
                pl.BlockSpec((1, RET_DV), lambda b, h: (0, 0))]
    args = [proj, proj, proj, *tabs, proj, norm_g.reshape(1, RET_DV)]
    if rope:
        cos, sin = _rope_tables(seqlen, RET_DK)
        in_specs += [pl.BlockSpec((seqlen, RET_DK), lambda b, h: (0, 0), pipeline_mode=pl.Buffered(1))] * 2
        args += [cos, sin]
    if has_s0:
        in_specs.append(pl.BlockSpec((None, 2, None, RET_DK, RET_DV), lambda b, h: (b, 0, h, 0, 0)))
        args.append(s0)
    in_specs.append(pl.BlockSpec(memory_space=pl.ANY))
    args.append(o_prev)
    aliases = {len(args) - 1: 0}
    out_shape = [jax.ShapeDtypeStruct((T, hv), BF16)]
    out_specs = [pl.BlockSpec((rows, RET_DV), lambda b, h: (rb + b, h))]
    if want_final:
        out_shape.append(jax.ShapeDtypeStruct((nseq, 2, RET_H, RET_DK, RET_DV), F32))
        out_specs.append(pl.BlockSpec((nseg, 2, None, RET_DK, RET_DV), lambda b, h: (b, 0, h, 0, 0)))
    outs = pl.pallas_call(
        functools.partial(_ret_kernel, cps, nseg, U, has_s0, want_final, rope),
        out_shape=tuple(out_shape),
        grid=(nseq // nseg, RET_H),
        in_specs=in_specs,
        out_specs=tuple(out_specs),
        scratch_shapes=[pltpu.VMEM((U * C, RET_DK), BF16)] * 4
        + [pltpu.VMEM((U, RET_DK, RET_DV), F32), pltpu.VMEM((RET_DK, RET_DV), F32),
           pltpu.VMEM((rows, RET_DV), F32)],
        input_output_aliases=aliases,
        compiler_params=_cp("arbitrary", "arbitrary"),
        name="ret",
    )(*args)
    return (outs[0], outs[1]) if want_final else (outs[0], None)


def _gla_layer(lay, x, mods, p, s0):
    hk, hv = GLA_H * GLA_DK, GLA_H * GLA_DV
    w_all = jnp.concatenate([p['w_in'], p['gate_w1'][0], p['gate_w1'][1],
                             jnp.zeros((D, 128 - 2 * GLA_RANK), F32)], axis=1).astype(BF16)
    proj = _proj(lay, x, mods, 0, 1, w_all, jnp.zeros((1, w_all.shape[1]), F32), w_all.shape[1], tm=512)
    pad = lambda w, lo: jnp.pad(w, ((lo, 128 - GLA_RANK - lo), (0, 0))).astype(BF16)
    w2f, w2b = pad(p['gate_w2'][0], 0), pad(p['gate_w2'][1], GLA_RANK)
    a = jnp.zeros((lay.T, hv), BF16)
    a, s_fin = _gla_core(proj, w2f, w2b, p['gate_b'], p['norm_g'], None, a, 0, lay.B, lay.L,
                         math.gcd(lay.B, 8), True)
    a, _ = _gla_core(proj, w2f, w2b, p['gate_b'], p['norm_g'], s0, a, lay.TP, lay.NS, lay.LS, 1, False)
    x = _plain_out(lay, x, mods, a, p['w_out'].astype(BF16), jnp.zeros((D,), F32))
    return x, s_fin


def _ret_layer(lay, x, mods, p, s0):
    hk, hv = RET_H * RET_DK, RET_H * RET_DV
    proj = _proj(lay, x, mods, 0, 1, p['w_in'].astype(BF16), jnp.zeros((1, 2 * hk + 2 * hv), F32),
                 2 * hk + 2 * hv, tm=512)
    tabs = _ret_tables(p['log_decay'])
    a = jnp.zeros((lay.T, hv), BF16)
    a, s_fin = _ret_core(proj, tabs, p['norm_g'], None, a, 0, lay.B, lay.L, math.gcd(lay.B, 8), True, False)
    a, _ = _ret_core(proj, tabs, p['norm_g'], s0, a, lay.TP, lay.NS, lay.LS, 1, False, True)
    x = _plain_out(lay, x, mods, a, p['w_out'].astype(BF16), jnp.zeros((D,), F32))
    return x, s_fin


MOE_BM = 512
EXPERT_TF = 1792
ROUTER_LANES = 128
DMA_UNROLL = 8


def _router_kernel(x_ref, sh_ref, sc_ref, rw_ref, h_ref, idx_ref, gate_ref, rank_ref, cnt_ref):
    @pl.when(pl.program_id(0) == 0)
    def _():
        cnt_ref[...] = jnp.zeros_like(cnt_ref)

    h = _modulate(x_ref[...], sh_ref[...], sc_ref[...])
    h_ref[...] = h
    logits = jnp.dot(h, rw_ref[...], precision=HIGHEST, preferred_element_type=F32)
    lane = lax.broadcasted_iota(jnp.int32, logits.shape, 1)
    neg = jnp.float32(-jnp.inf)
    logits = jnp.where(lane < N_EXPERTS, logits, neg)
    m1 = jnp.max(logits, axis=-1, keepdims=True)
    i1 = jnp.min(jnp.where(logits == m1, lane, ROUTER_LANES), axis=-1, keepdims=True)
    rest = jnp.where(lane == i1, neg, logits)
    m2 = jnp.max(rest, axis=-1, keepdims=True)
    i2 = jnp.min(jnp.where(rest == m2, lane, ROUTER_LANES), axis=-1, keepdims=True)
    e2 = jnp.exp(m2 - m1)
    g1 = 1.0 / (1.0 + e2)
    idx_ref[:, 0:1] = i1
    idx_ref[:, 1:2] = i2
    gate_ref[:, 0:1] = g1
    gate_ref[:, 1:2] = e2 * g1
    tm = logits.shape[0]
    sel1 = lane == i1
    sel2 = lane == i2
    picked = jnp.where(jnp.logical_or(sel1, sel2), 1.0, 0.0)
    before = (lax.broadcasted_iota(jnp.int32, (tm, tm), 1)
              < lax.broadcasted_iota(jnp.int32, (tm, tm), 0)).astype(BF16)
    prior = jnp.dot(before, picked.astype(BF16), preferred_element_type=F32) + cnt_ref[...]
    rank_ref[:, 0:1] = jnp.sum(jnp.where(sel1, prior, 0.0), axis=-1, keepdims=True).astype(jnp.int32)
    rank_ref[:, 1:2] = jnp.sum(jnp.where(sel2, prior, 0.0), axis=-1, keepdims=True).astype(jnp.int32)
    cnt_ref[...] += jnp.sum(picked, axis=0, keepdims=True)


def _router(lay, x, mods, router_w):
    tm = lay.tile(512)
    rw = jnp.pad(router_w, ((0, 0), (0, ROUTER_LANES - N_EXPERTS)))
    return pl.pallas_call(
        _router_kernel,
        out_shape=(jax.ShapeDtypeStruct((lay.T, D), F32),
                   jax.ShapeDtypeStruct((lay.T, 2), jnp.int32),
                   jax.ShapeDtypeStruct((lay.T, 2), F32),
                   jax.ShapeDtypeStruct((lay.T, 2), jnp.int32),
                   jax.ShapeDtypeStruct((1, ROUTER_LANES), F32)),
        grid=(lay.T // tm,),
        in_specs=[pl.BlockSpec((tm, D), lambda i: (i, 0)),
                  _mod_spec(lay, tm, 3, 1), _mod_spec(lay, tm, 4, 1),
                  pl.BlockSpec((D, ROUTER_LANES), lambda i: (0, 0))],
        out_specs=(pl.BlockSpec((tm, D), lambda i: (i, 0)),
                   pl.BlockSpec((tm, 2), lambda i: (i, 0)),
                   pl.BlockSpec((tm, 2), lambda i: (i, 0)),
                   pl.BlockSpec((tm, 2), lambda i: (i, 0)),
                   pl.BlockSpec((1, ROUTER_LANES), lambda i: (0, 0))),
        compiler_params=_cp("arbitrary"),
        name="router",
    )(x, mods, mods, rw)


def _moe_plan(idx, rank, counts, bm):
    a = idx.size
    counts = counts[0, :N_EXPERTS].astype(jnp.int32)
    padded = (counts + bm - 1) // bm * bm
    pad_end = jnp.cumsum(padded)
    pad_start = pad_end - padded
    hit = idx[..., None] == jnp.arange(N_EXPERTS, dtype=jnp.int32)
    dest = (rank + jnp.sum(jnp.where(hit, pad_start, 0), axis=-1)).reshape(a).astype(jnp.int32)
    nb = -(-(a + N_EXPERTS * (bm - 1)) // bm)
    block_start = jnp.arange(nb, dtype=jnp.int32) * bm
    block_e = jnp.sum((block_start[:, None] >= pad_end[None, :]).astype(jnp.int32), axis=1)
    block_e = jnp.minimum(block_e, N_EXPERTS - 1)
    nvalid = (pad_end[-1] // bm).astype(jnp.int32).reshape(1)
    fill = jnp.concatenate([pad_start + counts, pad_end, nvalid]).astype(jnp.int32)
    return dest, fill, block_e, nvalid, nb


def _dispatch_kernel(tm, bm, nb, dest_ref, fill_ref, h_ref, xs_hbm, zero_sc, sem, zsem):
    i = pl.program_id(0)
    zr = zero_sc.shape[0]

    @pl.when(i == 0)
    def _():
        zero_sc[...] = jnp.zeros_like(zero_sc)

        def zero_row(r):
            return pltpu.make_async_copy(zero_sc.at[pl.ds(0, 1)], xs_hbm.at[pl.ds(r, 1)], zsem)

        def zero_rows(r):
            return pltpu.make_async_copy(zero_sc, xs_hbm.at[pl.ds(pl.multiple_of(r, zr), zr)], zsem)

        for e in range(N_EXPERTS):
            lo, hi = fill_ref[e], fill_ref[N_EXPERTS + e]
            lax.fori_loop(lo, hi, lambda r, c: (zero_row(r).start(), c)[1], 0)
            lax.fori_loop(lo, hi, lambda r, c: (zero_row(r).wait(), c)[1], 0)
        lo, hi = fill_ref[2 * N_EXPERTS] * (bm // zr), nb * (bm // zr)
        lax.fori_loop(lo, hi, lambda q, c: (zero_rows(q * zr).start(), c)[1], 0)
        lax.fori_loop(lo, hi, lambda q, c: (zero_rows(q * zr).wait(), c)[1], 0)

    def row_copy(r, dst):
        return pltpu.make_async_copy(h_ref.at[pl.ds(r, 1)], xs_hbm.at[pl.ds(dst, 1)], sem)

    def issue(r, carry):
        a = 2 * (i * tm + r)
        row_copy(r, dest_ref[a]).start()
        row_copy(r, dest_ref[a + 1]).start()
        return carry

    def drain(r, carry):
        row_copy(r, 0).wait()
        row_copy(r, 0).wait()
        return carry

    lax.fori_loop(0, tm, issue, 0, unroll=DMA_UNROLL)
    lax.fori_loop(0, tm, drain, 0, unroll=DMA_UNROLL)


def _dispatch(lay, h, dest, fill, nb, bm):
    tm = lay.tile(512)
    grid_spec = pltpu.PrefetchScalarGridSpec(
        num_scalar_prefetch=2,
        grid=(lay.T // tm,),
        in_specs=[pl.BlockSpec((tm, D), lambda i, d, f: (i, 0))],
        out_specs=pl.BlockSpec(memory_space=pl.ANY),
        scratch_shapes=[pltpu.VMEM((64, D), F32), pltpu.SemaphoreType.DMA(()), pltpu.SemaphoreType.DMA(())],
    )
    return pl.pallas_call(
        functools.partial(_dispatch_kernel, tm, bm, nb),
        out_shape=jax.ShapeDtypeStruct((nb * bm, D), F32),
        grid_spec=grid_spec,
        compiler_params=_cp("arbitrary"),
        name="dispatch",
    )(dest, fill, h)


def _experts_kernel(nf, be_ref, nv_ref, xs_ref, wa_ref, wb_ref, wo_ref, o_ref, xb_sc, acc_sc):
    i = pl.program_id(0)
    f = pl.program_id(1)
    valid = i < nv_ref[0]

    @pl.when(jnp.logical_and(valid, f == 0))
    def _():
        xb_sc[...] = xs_ref[...].astype(BF16)

    @pl.when(valid)
    def _():
        xb = xb_sc[...]
        a = jnp.dot(xb, wa_ref[...], preferred_element_type=F32)
        b = jnp.dot(xb, wb_ref[...], preferred_element_type=F32)
        h = (_silu(a) * b).astype(BF16)
        y = jnp.dot(h, wo_ref[...], preferred_element_type=F32)

        @pl.when(f == 0)
        def _():
            acc_sc[...] = y

        @pl.when(f > 0)
        def _():
            acc_sc[...] += y

    @pl.when(f == nf - 1)
    def _():
        o_ref[...] = jnp.where(valid, acc_sc[...], 0.0)


def _experts(xs, block_e, nvalid, nb, bm, w_in, w_out):
    tf = EXPERT_TF
    nf = EXPERT_DIM // tf

    def wmap(off):
        def imap(i, f, be, nv):
            fe = jnp.where(i < nv[0], f, nf - 1)
            return (be[i], 0, off + fe)
        return imap

    def womap(i, f, be, nv):
        fe = jnp.where(i < nv[0], f, nf - 1)
        return (be[i], fe, 0)

    grid_spec = pltpu.PrefetchScalarGridSpec(
        num_scalar_prefetch=2,
        grid=(nb, nf),
        in_specs=[pl.BlockSpec((bm, D), lambda i, f, be, nv: (jnp.minimum(i, nv[0] - 1), 0)),
                  pl.BlockSpec((None, D, tf), wmap(0)),
                  pl.BlockSpec((None, D, tf), wmap(nf)),
                  pl.BlockSpec((None, tf, D), womap)],
        out_specs=pl.BlockSpec((bm, D), lambda i, f, be, nv: (i, 0)),
        scratch_shapes=[pltpu.VMEM((bm, D), BF16), pltpu.VMEM((bm, D), F32)],
    )
    return pl.pallas_call(
        functools.partial(_experts_kernel, nf),
        out_shape=jax.ShapeDtypeStruct((nb * bm, D), F32),
        grid_spec=grid_spec,
        compiler_params=_cp("arbitrary", "arbitrary"),
        name="experts",
    )(block_e, nvalid, xs, w_in, w_in, w_out)


def _combine_kernel(tm, nt, final, dest_ref, x_ref, gate_ref, g_ref, fg_ref, ys_hbm, o_ref, y_sc, sem):
    i = pl.program_id(0)
    slot = i % 2

    def row_copy(s, k, r, src):
        return pltpu.make_async_copy(ys_hbm.at[pl.ds(src, 1)], y_sc.at[s, k, pl.ds(r, 1)], sem.at[s])

    def issue_tile(t, s):
        def issue(r, carry):
            a = 2 * (t * tm + r)
            row_copy(s, 0, r, dest_ref[a]).start()
            row_copy(s, 1, r, dest_ref[a + 1]).start()
            return carry
        lax.fori_loop(0, tm, issue, 0, unroll=DMA_UNROLL)

    @pl.when(i == 0)
    def _():
        issue_tile(0, 0)

    @pl.when(i + 1 < nt)
    def _():
        issue_tile(i + 1, 1 - slot)

    def drain(r, carry):
        row_copy(slot, 0, r, 0).wait()
        row_copy(slot, 1, r, 0).wait()
        return carry

    lax.fori_loop(0, tm, drain, 0, unroll=DMA_UNROLL)
    gate = gate_ref[...]
    out = x_ref[...] + g_ref[...] * (gate[:, 0:1] * y_sc[slot, 0] + gate[:, 1:2] * y_sc[slot, 1])
    if final:
        ms = jnp.mean(out * out, axis=-1, keepdims=True)
        out = out * lax.rsqrt(ms + RMS_EPS) * fg_ref[...]
    o_ref[...] = out


def _combine(lay, x, mods, gates, ys, dest, final_g):
    tm = lay.tile(256)
    nt = lay.T // tm
    final = final_g is not None
    fg = (final_g if final else jnp.ones((D,), F32)).reshape(1, D)
    grid_spec = pltpu.PrefetchScalarGridSpec(
        num_scalar_prefetch=1,
        grid=(nt,),
        in_specs=[pl.BlockSpec((tm, D), lambda i, d: (i, 0)),
                  pl.BlockSpec((tm, 2), lambda i, d: (i, 0)),
                  pl.BlockSpec((None, 1, D), lambda i, d: (lay.group(i * tm) * MOD_CHUNKS + 5, 0, 0)),
                  pl.BlockSpec((1, D), lambda i, d: (0, 0)),
                  pl.BlockSpec(memory_space=pl.ANY)],
        out_specs=pl.BlockSpec((tm, D), lambda i, d: (i, 0)),
        scratch_shapes=[pltpu.VMEM((2, 2, tm, D), F32), pltpu.SemaphoreType.DMA((2,))],
    )
    return pl.pallas_call(
        functools.partial(_combine_kernel, tm, nt, final),
        out_shape=jax.ShapeDtypeStruct((lay.T, D), F32),
        grid_spec=grid_spec,
        compiler_params=_cp("arbitrary"),
        name="combine",
    )(dest, x, gates, mods, fg, ys)


def _moe_layer(lay, x, mods, router_w, w_in, w_out, final_g=None, bm=MOE_BM):
    h, idx, gates, rank, counts = _router(lay, x, mods, router_w)
    dest, fill, block_e, nvalid, nb = _moe_plan(idx, rank, counts, bm)
    xs = _dispatch(lay, h, dest, fill, nb, bm)
    ys = _experts(xs, block_e, nvalid, nb, bm, w_in, w_out)
    return _combine(lay, x, mods, gates, ys, dest, final_g)


def kernel(x_prompt, x_sample, c, state_l0_s5_re, state_l0_s5_im, state_l2_gla, state_l3_ret, c_ctx, l0_mod_w, l0_mod_b, l0_s5_a_re, l0_s5_a_im, l0_s5_log_dt, l0_s5_b_re, l0_s5_b_im, l0_s5_c_re, l0_s5_c_im, l0_s5_d, l0_s5_glu_w, l0_ffn_w_in, l0_ffn_w_out, l1_mod_w, l1_mod_b, l1_hy_w_in, l1_hy_b_in, l1_hy_short_w, l1_hy_short_b, l1_hy_f_w1, l1_hy_f_b1, l1_hy_f_w2, l1_hy_f_b2, l1_hy_f_w3, l1_hy_f_freq, l1_hy_skip, l1_hy_w_out, l1_hy_b_out, l1_moe_router, l1_moe_w_in, l1_moe_w_out, l2_mod_w, l2_mod_b, l2_gla_w_in, l2_gla_gate_w1, l2_gla_gate_w2, l2_gla_gate_b, l2_gla_norm_g, l2_gla_w_out, l2_ffn_w_in, l2_ffn_w_out, l3_mod_w, l3_mod_b, l3_ret_w_in, l3_ret_log_decay, l3_ret_norm_g, l3_ret_w_out, l3_moe_router, l3_moe_w_in, l3_moe_w_out, final_norm_g):
    B, L, _ = x_prompt.shape
    NS, LS, _ = x_sample.shape
    lay = Layout(B, L, NS, LS)
    x = jnp.concatenate([x_prompt.reshape(B * L, D), x_sample.reshape(NS * LS, D)], axis=0)
    cond = jnp.concatenate([c_ctx[None], c, jnp.zeros((8 - 1 - NS, D), F32)], axis=0)
    mods0 = _mods(cond, l0_mod_w, l0_mod_b)
    p0 = dict(a_re=l0_s5_a_re, a_im=l0_s5_a_im, log_dt=l0_s5_log_dt, b_re=l0_s5_b_re, b_im=l0_s5_b_im,
              c_re=l0_s5_c_re, c_im=l0_s5_c_im, d=l0_s5_d, glu_w=l0_s5_glu_w.astype(BF16))
    x, s5_re, s5_im = _s5_layer(lay, x, mods0, p0, state_l0_s5_re, state_l0_s5_im)
    x = _ffn(lay, x, mods0, l0_ffn_w_in.astype(BF16), l0_ffn_w_out.astype(BF16))

    mods1 = _mods(cond, l1_mod_w, l1_mod_b)
    p1 = dict(w_in=l1_hy_w_in, b_in=l1_hy_b_in, short_w=l1_hy_short_w, short_b=l1_hy_short_b,
              f_w1=l1_hy_f_w1, f_b1=l1_hy_f_b1, f_w2=l1_hy_f_w2, f_b2=l1_hy_f_b2, f_w3=l1_hy_f_w3,
              f_freq=l1_hy_f_freq, skip=l1_hy_skip, w_out=l1_hy_w_out, b_out=l1_hy_b_out)
    x = _hyena_layer(lay, x, mods1, p1)
    x = _moe_layer(lay, x, mods1, l1_moe_router, l1_moe_w_in.astype(BF16), l1_moe_w_out.astype(BF16))

    mods2 = _mods(cond, l2_mod_w, l2_mod_b)
    p2 = dict(w_in=l2_gla_w_in, gate_w1=l2_gla_gate_w1, gate_w2=l2_gla_gate_w2, gate_b=l2_gla_gate_b,
              norm_g=l2_gla_norm_g, w_out=l2_gla_w_out)
    x, gla_state = _gla_layer(lay, x, mods2, p2, state_l2_gla)
    x = _ffn(lay, x, mods2, l2_ffn_w_in.astype(BF16), l2_ffn_w_out.astype(BF16))

    mods3 = _mods(cond, l3_mod_w, l3_mod_b)
    p3 = dict(w_in=l3_ret_w_in, log_decay=l3_ret_log_decay, norm_g=l3_ret_norm_g, w_out=l3_ret_w_out)
    x, ret_state = _ret_layer(lay, x, mods3, p3, state_l3_ret)
    y = _moe_layer(lay, x, mods3, l3_moe_router, l3_moe_w_in.astype(BF16), l3_moe_w_out.astype(BF16),
                   final_g=final_norm_g)
    return (y[:lay.TP].reshape(B, L, D), y[lay.TP:].reshape(NS, LS, D), s5_re, s5_im, gla_state, ret_state)
```

```python
import functools
import math

import jax
import jax.numpy as jnp
import numpy as np
from jax import lax
from jax.experimental import pallas as pl
from jax.experimental.pallas import tpu as pltpu

F32 = jnp.float32
BF16 = jnp.bfloat16
HIGHEST = lax.Precision.HIGHEST

D = 1024
RMS_EPS = 1e-6
MOD_CHUNKS = 6
GRID_W = 64

S5_Q = 16
S5_G = D // S5_Q
S5_P = 64
S5_T = 16
S5_SCAN_ROWS = 64

HY_BANDS = 16
HY_TARGET = 1e-2
HY_FAST_PCT = 0.3
HY_SLOW_PCT = 1.5

GLA_H, GLA_DK, GLA_DV = 4, 128, 256
GLA_RANK = 16
GLA_TAU = 16.0
RET_H, RET_DK, RET_DV = 4, 256, 512
CHUNK = 64
RET_CHUNK = 256
ROPE_BASE = 10000.0

FFN_DIM = 2816
N_EXPERTS = 8
EXPERT_DIM = 3584

VMEM_LIMIT_V7X = 56 * 1024 * 1024


def _cp(*sem):
    return pltpu.CompilerParams(dimension_semantics=sem, vmem_limit_bytes=VMEM_LIMIT_V7X)


def _silu(x):
    return x * jax.nn.sigmoid(x)


def _modulate(x, shift, scale):
    ms = jnp.mean(x * x, axis=-1, keepdims=True)
    return x * lax.rsqrt(ms + RMS_EPS) * (1.0 + scale) + shift


class Layout:
    def __init__(self, n_prompt, l_prompt, n_sample, l_sample):
        self.B, self.L, self.NS, self.LS = n_prompt, l_prompt, n_sample, l_sample
        self.TP = n_prompt * l_prompt
        self.T = self.TP + n_sample * l_sample

    def tile(self, want):
        t = math.gcd(math.gcd(self.TP, self.LS), want)
        assert t % 8 == 0
        return t

    def group(self, row):
        return jnp.where(row < self.TP, 0, 1 + (row - self.TP) // self.LS)


def _mod_spec(lay, tm, chunk, ngrid):
    def imap(*ids):
        return (lay.group(ids[0] * tm) * MOD_CHUNKS + chunk, 0, 0)
    del ngrid
    return pl.BlockSpec((None, 1, D), imap)


def _mods_kernel(c_ref, w_ref, b_ref, o_ref):
    o_ref[...] = jnp.dot(_silu(c_ref[...]), w_ref[...], precision=HIGHEST,
                         preferred_element_type=F32) + b_ref[...]


def _mods(cond, w, b):
    n = MOD_CHUNKS * D
    tn = 1536
    out = pl.pallas_call(
        _mods_kernel,
        out_shape=jax.ShapeDtypeStruct((8, n), F32),
        grid=(n // tn,),
        in_specs=[pl.BlockSpec((8, D), lambda j: (0, 0)),
                  pl.BlockSpec((D, tn), lambda j: (0, j)),
                  pl.BlockSpec((1, tn), lambda j: (0, j))],
        out_specs=pl.BlockSpec((8, tn), lambda j: (0, j)),
        compiler_params=_cp("arbitrary"),
        name="mods",
    )(cond, w, b.reshape(1, n))
    return out.reshape(8 * MOD_CHUNKS, 1, D)


def _modulate_kernel(x_ref, sh_ref, sc_ref, o_ref):
    o_ref[...] = _modulate(x_ref[...], sh_ref[...], sc_ref[...]).astype(o_ref.dtype)


def _modulate_call(lay, x, mods, c_shift, c_scale, dtype):
    tm = lay.tile(512)
    return pl.pallas_call(
        _modulate_kernel,
        out_shape=jax.ShapeDtypeStruct((lay.T, D), dtype),
        grid=(lay.T // tm,),
        in_specs=[pl.BlockSpec((tm, D), lambda i: (i, 0)),
                  _mod_spec(lay, tm, c_shift, 1), _mod_spec(lay, tm, c_scale, 1)],
        out_specs=pl.BlockSpec((tm, D), lambda i: (i, 0)),
        compiler_params=_cp("arbitrary"),
        name="modulate",
    )(x, mods, mods)


def _proj_kernel(x_ref, sh_ref, sc_ref, w_ref, b_ref, o_ref, u_sc):
    @pl.when(pl.program_id(1) == 0)
    def _():
        u_sc[...] = _modulate(x_ref[...], sh_ref[...], sc_ref[...]).astype(BF16)

    acc = jnp.dot(u_sc[...], w_ref[...], preferred_element_type=F32) + b_ref[...]
    o_ref[...] = acc.astype(o_ref.dtype)


def _proj(lay, x, mods, c_shift, c_scale, w, b, tn, out_dtype=BF16, tm=1024):
    tm = lay.tile(tm)
    n = w.shape[1]
    assert n % tn == 0
    resident = dict(pipeline_mode=pl.Buffered(1)) if tn == n else {}
    return pl.pallas_call(
        _proj_kernel,
        out_shape=jax.ShapeDtypeStruct((lay.T, n), out_dtype),
        grid=(lay.T // tm, n // tn),
        in_specs=[pl.BlockSpec((tm, D), lambda i, j: (i, 0)),
                  _mod_spec(lay, tm, c_shift, 2), _mod_spec(lay, tm, c_scale, 2),
                  pl.BlockSpec((D, tn), lambda i, j: (0, j), **resident),
                  pl.BlockSpec((1, tn), lambda i, j: (0, j))],
        out_specs=pl.BlockSpec((tm, tn), lambda i, j: (i, j)),
        scratch_shapes=[pltpu.VMEM((tm, D), BF16)],
        compiler_params=_cp("arbitrary", "arbitrary"),
        name="proj",
    )(x, mods, mods, w, b)


def _ffn_kernel(x_ref, sh_ref, sc_ref, g_ref, wa_ref, wb_ref, wo_ref, o_ref):
    x = x_ref[...]
    u = _modulate(x, sh_ref[...], sc_ref[...]).astype(BF16)
    a = jnp.dot(u, wa_ref[...], preferred_element_type=F32)
    b = jnp.dot(u, wb_ref[...], preferred_element_type=F32)
    h = (_silu(a) * b).astype(BF16)
    o_ref[...] = x + g_ref[...] * jnp.dot(h, wo_ref[...], preferred_element_type=F32)


def _ffn(lay, x, mods, w_in, w_out):
    tm = lay.tile(512)
    once = dict(pipeline_mode=pl.Buffered(1))
    return pl.pallas_call(
        _ffn_kernel,
        out_shape=jax.ShapeDtypeStruct((lay.T, D), F32),
        grid=(lay.T // tm,),
        in_specs=[pl.BlockSpec((tm, D), lambda i: (i, 0)),
                  _mod_spec(lay, tm, 3, 1), _mod_spec(lay, tm, 4, 1), _mod_spec(lay, tm, 5, 1),
                  pl.BlockSpec((D, FFN_DIM), lambda i: (0, 0), **once),
                  pl.BlockSpec((D, FFN_DIM), lambda i: (0, 1), **once),
                  pl.BlockSpec((FFN_DIM, D), lambda i: (0, 0), **once)],
        out_specs=pl.BlockSpec((tm, D), lambda i: (i, 0)),
        compiler_params=_cp("arbitrary"),
        name="ffn",
    )(x, mods, mods, mods, w_in, w_in, w_out)


def _s5_tables(a_re, a_im, log_dt, b_re, b_im, c_re, c_im, d_skip):
    T, G, P, Q = S5_T, S5_G, S5_P, S5_Q
    a = lax.complex(a_re, a_im)
    adt = a * jnp.exp(log_dt)[..., None]
    lam = jnp.exp(adt)
    bb = ((lam - 1.0) / a)[..., None] * lax.complex(b_re, b_im)
    cm = lax.complex(c_re, c_im)
    steps = jnp.arange(T + 1, dtype=F32)
    pw = jnp.exp(steps[None, :, None, None] * adt[:, None])
    kern = jnp.real(jnp.einsum('dgqp,djgp,dgpr->djgqr', cm, pw[:, :T], bb))
    lag = jnp.arange(T)[:, None, None]
    s_i = jnp.arange(T)[None, :, None]
    t_i = jnp.arange(T)[None, None, :]
    place = jnp.stack([t_i - s_i == lag, s_i - t_i == lag]).astype(F32)
    m = jnp.einsum('djst,djgqr->gsrtq', place, kern, precision=HIGHEST)
    eye = (jnp.eye(T)[:, None, :, None] * jnp.eye(Q)[None, :, None, :])
    m = m + eye[None] * d_skip.reshape(G, 1, 1, 1, Q)
    m = m.reshape(G, T * Q, T * Q)
    e_f = pw[0][T - 1 - jnp.arange(T)]
    e_b = pw[1][jnp.arange(T)]
    n_f = e_f[..., None] * bb[0][None]
    n_b = e_b[..., None] * bb[1][None]
    n_c = jnp.concatenate([n_f, n_b], axis=2)
    n_c = jnp.transpose(n_c, (1, 0, 3, 2)).reshape(G, T * Q, 2 * P)
    lam_t = jnp.concatenate([pw[0][T], pw[1][T]], axis=-1)
    w_f = cm[0][:, None] * jnp.transpose(pw[0][1:T + 1], (1, 0, 2))[:, :, None, :]
    w_b = cm[1][:, None] * jnp.transpose(pw[1][T - jnp.arange(T)], (1, 0, 2))[:, :, None, :]
    w_f = jnp.transpose(w_f, (0, 3, 1, 2)).reshape(G, P, T * Q)
    w_b = jnp.transpose(w_b, (0, 3, 1, 2)).reshape(G, P, T * Q)
    z = jnp.zeros_like(jnp.real(w_f))
    c_mats = dict(c_f_re=jnp.concatenate([jnp.real(w_f), z], axis=1),
                  c_f_im=jnp.concatenate([-jnp.imag(w_f), z], axis=1),
                  c_b_re=jnp.concatenate([z, jnp.real(w_b)], axis=1),
                  c_b_im=jnp.concatenate([z, -jnp.imag(w_b)], axis=1))
    return dict(m=m.astype(BF16), n_re=jnp.real(n_c).astype(BF16), n_im=jnp.imag(n_c).astype(BF16),
                l_re=jnp.real(lam_t), l_im=jnp.imag(lam_t), **{k: v.astype(BF16) for k, v in c_mats.items()})


def _s5_in_kernel(u_ref, m_ref, nre_ref, nim_ref, yi_ref, sre_ref, sim_ref):
    u = u_ref[...]
    yi_ref[...] = jnp.dot(u, m_ref[...], preferred_element_type=F32)
    sre_ref[...] = jnp.dot(u, nre_ref[...], preferred_element_type=F32).reshape(sre_ref.shape)
    sim_ref[...] = jnp.dot(u, nim_ref[...], preferred_element_type=F32).reshape(sim_ref.shape)


def _s5_in(ug, tabs):
    G, R, W = ug.shape
    P2 = 2 * S5_P
    RB = S5_SCAN_ROWS
    assert R % RB == 0
    gspec = lambda n: pl.BlockSpec((None, W, n), lambda g: (g, 0, 0))
    rspec = pl.BlockSpec((None, R, W), lambda g: (g, 0, 0))
    sspec = pl.BlockSpec((R // RB, RB, P2), lambda g: (0, g, 0))
    sshape = jax.ShapeDtypeStruct((R // RB, G * RB, P2), F32)
    return pl.pallas_call(
        _s5_in_kernel,
        out_shape=(jax.ShapeDtypeStruct((G, R, W), F32), sshape, sshape),
        grid=(G,),
        in_specs=[rspec, gspec(W), gspec(P2), gspec(P2)],
        out_specs=(rspec, sspec, sspec),
        compiler_params=_cp("arbitrary"),
        name="s5_in",
    )(ug, tabs['m'], tabs['n_re'], tabs['n_im'])


def _s5_scan_kernel(nsb, ncb, nblk, sref_ref, simf_ref, sreb_ref, simb_ref, lre_ref, lim_ref,
                    h0re_ref, h0im_ref, *rest):
    hfre_ref, hfim_ref, hbre_ref, hbim_ref, fre_ref, fim_ref, cre_sc, cim_sc = rest[4:]
    P = S5_P
    rows = sref_ref.shape[0] // ncb
    j = pl.program_id(1)
    fwd = lax.broadcasted_iota(jnp.int32, (1, 2 * P), 1) < P
    lre = lre_ref[...]
    lim = lim_ref[...]

    @pl.when(j == 0)
    def _():
        cre_sc[...] = h0re_ref[...]
        cim_sc[...] = h0im_ref[...]

    def at(k):
        return pl.ds(k, rows, stride=ncb)

    def body(k, carry):
        hre, him = carry
        kb = ncb - 1 - k
        hfre_ref[at(k), :] = hre
        hfim_ref[at(k), :] = him
        hbre_ref[at(kb), :] = hre
        hbim_ref[at(kb), :] = him
        sre = jnp.where(fwd, sref_ref[at(k), :], sreb_ref[at(kb), :])
        sim = jnp.where(fwd, simf_ref[at(k), :], simb_ref[at(kb), :])
        return (lre * hre - lim * him + sre, lre * him + lim * hre + sim)

    hre, him = lax.fori_loop(0, ncb, body, (cre_sc[...], cim_sc[...]), unroll=4)
    cre_sc[...] = hre
    cim_sc[...] = him

    @pl.when(j == nblk - 1)
    def _():
        fre_ref[...] = hre
        fim_ref[...] = him


def _s5_scan(sre, sim, tabs, h0re, h0im, hprev, row0, nseq, nc, nsb, nblk):
    _, grb, P2 = sre.shape
    rb = S5_SCAN_ROWS
    G = grb // rb
    assert nblk == 1 or nsb == 1
    ncb = nc // nblk
    assert nsb * ncb == rb and row0 % rb == 0 and nseq % nsb == 0 and nc % nblk == 0
    b0 = row0 // rb
    fspec = pl.BlockSpec((None, grb, P2), lambda i, j: (b0 + i * nblk + j, 0, 0))
    bspec = pl.BlockSpec((None, grb, P2), lambda i, j: (b0 + i * nblk + nblk - 1 - j, 0, 0))
    lspec = pl.BlockSpec((G * nsb, P2), lambda i, j: (0, 0))
    qspec = pl.BlockSpec((None, G * nsb, P2), lambda i, j: (i, 0, 0))
    anyspec = pl.BlockSpec(memory_space=pl.ANY)
    fin = jax.ShapeDtypeStruct((nseq // nsb, G * nsb, P2), F32)
    rep = lambda a: jnp.repeat(a, nsb, axis=0)
    flat = lambda a: a.reshape(nseq // nsb, G * nsb, P2)
    outs = pl.pallas_call(
        functools.partial(_s5_scan_kernel, nsb, ncb, nblk),
        out_shape=tuple(jax.ShapeDtypeStruct(h.shape, h.dtype) for h in hprev) + (fin, fin),
        grid=(nseq // nsb, nblk),
        in_specs=[fspec, fspec, bspec, bspec, lspec, lspec, qspec, qspec] + [anyspec] * 4,
        out_specs=(fspec, fspec, bspec, bspec, qspec, qspec),
        scratch_shapes=[pltpu.VMEM((G * nsb, P2), F32), pltpu.VMEM((G * nsb, P2), F32)],
        input_output_aliases={8: 0, 9: 1, 10: 2, 11: 3},
        compiler_params=_cp("arbitrary", "arbitrary"),
        name="s5_scan",
    )(sre, sim, sre, sim, rep(tabs['l_re']), rep(tabs['l_im']), flat(h0re), flat(h0im), *hprev)
    return outs[:4], outs[4].reshape(nseq // nsb, G, nsb, P2), outs[5].reshape(nseq // nsb, G, nsb, P2)


def _s5_out_kernel(yi_ref, hfre_ref, hfim_ref, hbre_ref, hbim_ref, cfre_ref, cfim_ref, cbre_ref, cbim_ref,
                   y_ref):
    y = yi_ref[...]
    for h_ref, c_ref in ((hfre_ref, cfre_ref), (hfim_ref, cfim_ref), (hbre_ref, cbre_ref), (hbim_ref, cbim_ref)):
        h = h_ref[...].reshape(y.shape[0], h_ref.shape[-1])
        y += jnp.dot(h.astype(BF16), c_ref[...], preferred_element_type=F32)
    y_ref[...] = y.astype(y_ref.dtype)


def _s5_out(yi, hprev, tabs):
    G, R, W = yi.shape
    P2 = 2 * S5_P
    RB = S5_SCAN_ROWS
    gspec = pl.BlockSpec((None, P2, W), lambda g: (g, 0, 0))
    hspec = pl.BlockSpec((R // RB, RB, P2), lambda g: (0, g, 0))
    rspec = pl.BlockSpec((None, R, W), lambda g: (g, 0, 0))
    return pl.pallas_call(
        _s5_out_kernel,
        out_shape=jax.ShapeDtypeStruct((G, R, W), F32),
        grid=(G,),
        in_specs=[rspec] + [hspec] * 4 + [gspec] * 4,
        out_specs=rspec,
        compiler_params=_cp("arbitrary"),
        name="s5_out",
    )(yi, *hprev, tabs['c_f_re'], tabs['c_f_im'], tabs['c_b_re'], tabs['c_b_im'])


LANES = 128
S5_GB = LANES // S5_Q


def _block_transpose(sets):
    blk = lax.broadcasted_iota(jnp.int32, sets[0][0].shape, 1) // S5_Q
    sets = [list(regs) for regs in sets]
    d = S5_GB // 2
    while d:
        keep = (blk & d) == 0
        for regs in sets:
            for i in range(S5_GB):
                if i & d:
                    continue
                a, b = regs[i], regs[i + d]
                regs[i] = jnp.where(keep, a, pltpu.roll(b, d * S5_Q, 1))
                regs[i + d] = jnp.where(keep, pltpu.roll(a, LANES - d * S5_Q, 1), b)
        d //= 2
    return sets


def _s5_pre_kernel(tm, x_ref, sh_ref, sc_ref, ug_ref, u_sc):
    u = _modulate(x_ref[...], sh_ref[...], sc_ref[...])
    for j in range(D // LANES):
        u_sc[j] = u[:, j * LANES:(j + 1) * LANES]
    rows16 = 16
    nh = S5_T // S5_GB
    for c in range(tm // (S5_T * rows16)):
        base = c * S5_T * rows16
        for j in range(D // LANES):
            sets = [[u_sc[j, pl.ds(base + h * S5_GB + s, rows16, stride=S5_T), :] for s in range(S5_GB)]
                    for h in range(nh)]
            for h, regs in enumerate(_block_transpose(sets)):
                for gl, t in enumerate(regs):
                    ug_ref[j * S5_GB + gl, c * rows16:(c + 1) * rows16, h * LANES:(h + 1) * LANES] = t.astype(BF16)


def _s5_pre(lay, x, mods):
    tm = lay.tile(512)
    assert tm % (S5_T * 16) == 0
    return pl.pallas_call(
        functools.partial(_s5_pre_kernel, tm),
        out_shape=jax.ShapeDtypeStruct((S5_G, lay.T // S5_T, S5_T * S5_Q), BF16),
        grid=(lay.T // tm,),
        in_specs=[pl.BlockSpec((tm, D), lambda i: (i, 0)), _mod_spec(lay, tm, 0, 1), _mod_spec(lay, tm, 1, 1)],
        out_specs=pl.BlockSpec((S5_G, tm // S5_T, S5_T * S5_Q), lambda i: (0, i, 0)),
        scratch_shapes=[pltpu.VMEM((D // LANES, tm, LANES), F32)],
        compiler_params=_cp("arbitrary"),
        name="s5_pre",
    )(x, mods, mods)


def _s5_glu_kernel(tm, x_ref, yg_ref, g_ref, wv_ref, wg_ref, o_ref, a_sc, y_sc):
    @pl.when(pl.program_id(1) == 0)
    def _():
        def sub_tile(c, carry):
            crow = pl.ds(pl.multiple_of(c * 8, 8), 8)
            base = c * (S5_T * 8)
            nh = S5_T // S5_GB
            for j2 in range(0, D // LANES, 2):
                keys = [(j, h) for j in (j2, j2 + 1) for h in range(nh)]
                sets = [[yg_ref[j * S5_GB + gl, crow, h * LANES:(h + 1) * LANES] for gl in range(S5_GB)]
                        for j, h in keys]
                for (j, h), regs in zip(keys, _block_transpose(sets)):
                    for s, t in enumerate(regs):
                        y_sc[j, pl.ds(base + h * S5_GB + s, 8, stride=S5_T), :] = t
            return carry

        lax.fori_loop(0, tm // (S5_T * 8), sub_tile, 0)
        for j in range(D // LANES):
            a_sc[:, j * LANES:(j + 1) * LANES] = jax.nn.gelu(y_sc[j]).astype(BF16)

    a = a_sc[...]
    val = jnp.dot(a, wv_ref[...], preferred_element_type=F32)
    gate = jnp.dot(a, wg_ref[...], preferred_element_type=F32)
    o_ref[...] = x_ref[...] + g_ref[...] * (val * jax.nn.sigmoid(gate))


def _s5_glu(lay, x, yg, mods, glu_w):
    tm = lay.tile(1024)
    assert tm % (S5_T * 8) == 0
    tn = 512
    nn = D // tn
    return pl.pallas_call(
        functools.partial(_s5_glu_kernel, tm),
        out_shape=jax.ShapeDtypeStruct((lay.T, D), F32),
        grid=(lay.T // tm, nn),
        in_specs=[pl.BlockSpec((tm, tn), lambda i, j: (i, j)),
                  pl.BlockSpec((S5_G, tm // S5_T, S5_T * S5_Q), lambda i, j: (0, i, 0)),
                  pl.BlockSpec((None, 1, tn), lambda i, j: (lay.group(i * tm) * MOD_CHUNKS + 2, 0, j)),
                  pl.BlockSpec((D, tn), lambda i, j: (0, j)),
                  pl.BlockSpec((D, tn), lambda i, j: (0, nn + j))],
        out_specs=pl.BlockSpec((tm, tn), lambda i, j: (i, j)),
        scratch_shapes=[pltpu.VMEM((tm, D), BF16), pltpu.VMEM((D // LANES, tm, LANES), F32)],
        compiler_params=_cp("arbitrary", "arbitrary"),
        name="s5_glu",
    )(x, yg, mods, glu_w, glu_w)


def _s5_layer(lay, x, mods, p, h0_re, h0_im):
    T, G, P, Q = S5_T, S5_G, S5_P, S5_Q
    tabs = _s5_tables(p['a_re'], p['a_im'], p['log_dt'], p['b_re'], p['b_im'], p['c_re'], p['c_im'], p['d'])
    R = lay.T // T
    ug = _s5_pre(lay, x, mods)
    yi, sre, sim = _s5_in(ug, tabs)
    hprev = tuple(jnp.zeros(sre.shape, F32) for _ in range(4))
    ncp, ncs = lay.L // T, lay.LS // T
    nsb = S5_SCAN_ROWS // ncp
    zero = jnp.zeros((lay.B // nsb, G, nsb, 2 * P), F32)
    hprev, fre, fim = _s5_scan(sre, sim, tabs, zero, zero, hprev, 0, lay.B, ncp, nsb, 1)
    to_lanes = lambda s: jnp.transpose(s, (0, 2, 1, 3)).reshape(lay.NS, G, 1, 2 * P)
    hprev, _, _ = _s5_scan(sre, sim, tabs, to_lanes(h0_re), to_lanes(h0_im), hprev,
                           lay.TP // T, lay.NS, ncs, 1, max(1, ncs // S5_SCAN_ROWS))
    yg = _s5_out(yi, hprev, tabs)
    x = _s5_glu(lay, x, yg, mods, p['glu_w'])
    from_lanes = lambda s: jnp.transpose(s, (0, 2, 1, 3)).reshape(lay.B, G, 2, P).transpose(0, 2, 1, 3)
    return x, from_lanes(fre), from_lanes(fim)


def _hyena_filters(L, p):
    mm = functools.partial(jnp.matmul, precision=HIGHEST)
    f = jnp.linspace(1e-4, HY_BANDS - 1, HY_BANDS, dtype=F32)[None, :]
    max_decay = math.log(HY_TARGET) / HY_FAST_PCT
    min_decay = math.log(HY_TARGET) / HY_SLOW_PCT
    deltas = jnp.abs(jnp.linspace(min_decay, max_decay, D, dtype=F32))
    w3 = p['f_w3'].reshape(-1, 2, 2, D)

    def side(pos, s):
        t = (pos.astype(F32) / (L - 1))[:, None]
        w = 2.0 * math.pi * pos.astype(F32)[:, None] / L
        feats = jnp.concatenate([t, jnp.cos(f * w), -jnp.sin(f * w)], axis=-1)
        z = jnp.sin(p['f_freq'][0] * (mm(feats, p['f_w1']) + p['f_b1']))
        z = jnp.sin(p['f_freq'][1] * (mm(z, p['f_w2']) + p['f_b2']))
        win = jnp.exp(-t * deltas)
        return jnp.stack([mm(z, w3[:, o, s]) * win for o in range(2)])

    j = jnp.arange(L, dtype=jnp.int32)
    k_lo = side(j, 0)
    k_hi = side((L - j) % L, 1) * (j > 0).astype(F32)[None, :, None]
    norm = jnp.sum(jnp.abs(k_lo), axis=1, keepdims=True) + jnp.sum(jnp.abs(k_hi), axis=1, keepdims=True)
    k_lo, k_hi = k_lo / norm, k_hi / norm
    alt = (1.0 - 2.0 * (j % 2).astype(F32))[None, :, None]
    k_ny = jnp.sum(alt * (k_lo + k_hi), axis=1) / (2 * L)
    return k_lo, k_hi, k_ny


def _dft_tables(L):
    r = math.isqrt(L)
    assert r * r == L
    t = jnp.arange(L, dtype=jnp.int32)[None, :]
    a = jnp.arange(r, dtype=jnp.int32)[:, None]

    def unit(idx):
        ang = (idx % (2 * L)).astype(F32) * (math.pi / L)
        return jnp.cos(ang), jnp.sin(ang)

    c1, s1 = unit(a * r * t)
    c2, s2 = unit(a * t)
    c1, s1, c2, s2 = c1[:, None], s1[:, None], c2[None], s2[None]
    cos = (c1 * c2 - s1 * s2).reshape(L, L)
    sin = (s1 * c2 + c1 * s2).reshape(L, L)
    return cos.astype(BF16), sin.astype(BF16)


def _hy_spec_kernel(L, tr, c_ref, s_ref, klo_ref, khi_ref, p_ref, q_ref):
    r = pl.program_id(2)
    f = r * tr + lax.broadcasted_iota(jnp.int32, (tr, 1), 0)
    sgn = (1 - 2 * (f % 2)).astype(F32)
    scale = jnp.where(f == 0, 1.0, 2.0) * (1.0 / (2 * L))
    c, s = c_ref[...], s_ref[...]
    lo, hi = klo_ref[...], khi_ref[...]
    dot = functools.partial(jnp.dot, preferred_element_type=F32)
    p_ref[...] = scale * (dot(c, lo) + sgn * dot(c, hi))
    q_ref[...] = scale * (dot(s, lo) + sgn * dot(s, hi))


def _hy_spectrum(L, cos, sin, k_lo, k_hi):
    tr = min(L, 512)
    tc = 512
    kspec = pl.BlockSpec((None, L, tc), lambda o, j, r: (o, 0, j))
    tspec = pl.BlockSpec((tr, L), lambda o, j, r: (r, 0))
    ospec = pl.BlockSpec((None, tr, tc), lambda o, j, r: (o, r, j))
    return pl.pallas_call(
        functools.partial(_hy_spec_kernel, L, tr),
        out_shape=(jax.ShapeDtypeStruct((2, L, D), F32), jax.ShapeDtypeStruct((2, L, D), F32)),
        grid=(2, D // tc, L // tr),
        in_specs=[tspec, tspec, kspec, kspec],
        out_specs=(ospec, ospec),
        compiler_params=_cp("arbitrary", "arbitrary", "arbitrary"),
        name="hy_spectrum",
    )(cos, sin, k_lo.astype(BF16), k_hi.astype(BF16))


def _hy_core_kernel(L, tr, ngrp, tc, x1_ref, x2_ref, v_ref, sw1_ref, sw2_ref, swv_ref, sb1_ref, sb2_ref,
                    sbv_ref, c_ref, s_ref, p_ref, q_ref, kny_ref, skip_ref, *rest):
    o_ref, z0_sc, z1_sc, x2_sc, a_sc, b_sc, ny0_sc, ny1_sc = rest[-8:]
    fused = tr == L
    ph = pl.program_id(2)
    r = pl.program_id(3)
    W = ngrp * tc

    def phase(k):
        return (lambda f: f()) if fused else pl.when(ph == k)
    tcv = min(L, 512)
    halo = 16

    def alt_sign(start, n):
        t = start + lax.broadcasted_iota(jnp.int32, (n, 1), 0)
        return (1 - 2 * (t % 2)).astype(F32)

    def conv3(src_ref, g, a, w_ref, b_ref):
        x = src_ref[g, pl.ds(a, tcv), :].astype(F32)
        row = lax.broadcasted_iota(jnp.int32, (tcv, 1), 0)
        up_at = pl.multiple_of(jnp.maximum(a - halo, 0), halo)
        dn_at = pl.multiple_of(jnp.minimum(a + tcv, L - halo), halo)
        up = src_ref[g, pl.ds(up_at, halo), :][halo - 1:halo, :].astype(F32)
        dn = src_ref[g, pl.ds(dn_at, halo), :][0:1, :].astype(F32)
        up = jnp.where(a > 0, up, 0.0)
        dn = jnp.where(a + tcv < L, dn, 0.0)
        prev = jnp.where(row == 0, up, pltpu.roll(x, 1, 0))
        nxt = jnp.where(row == tcv - 1, dn, pltpu.roll(x, tcv - 1, 0))
        return prev * w_ref[0:1, :] + x * w_ref[1:2, :] + nxt * w_ref[2:3, :] + b_ref[...]

    @(phase(0) if fused else pl.when(jnp.logical_and(ph == 0, r == 0)))
    def _():
        ny0_sc[...] = jnp.zeros_like(ny0_sc)

        def conv_tile(ti, carry):
            a = pl.multiple_of(ti * tcv, tcv)
            rows_a = pl.ds(a, tcv)
            for g in range(ngrp):
                cols = slice(g * tc, (g + 1) * tc)
                z1_sc[rows_a, cols] = conv3(x1_ref, g, a, sw1_ref, sb1_ref).astype(BF16)
                x2_sc[rows_a, cols] = conv3(x2_ref, g, a, sw2_ref, sb2_ref).astype(BF16)
                z0_sc[rows_a, cols] = conv3(v_ref, g, a, swv_ref, sbv_ref).astype(BF16)
            ny0_sc[...] += jnp.sum(alt_sign(a, tcv) * z0_sc[rows_a, :].astype(F32), axis=0, keepdims=True)
            return carry

        lax.fori_loop(0, L // tcv, conv_tile, 0)

    start = pl.multiple_of(r * tr, tr)
    rows = pl.ds(start, tr)
    dot = functools.partial(jnp.dot, preferred_element_type=F32)

    def forward(order, z_sc):
        z = z_sc[...]
        zre = dot(c_ref[...], z)
        zim = dot(s_ref[...], z)
        pw = jnp.concatenate([p_ref[order] if fused else p_ref[...]] * ngrp, axis=1)
        qw = jnp.concatenate([q_ref[order] if fused else q_ref[...]] * ngrp, axis=1)
        a_sc[rows, :] = (zre * pw - zim * qw).astype(BF16)
        b_sc[rows, :] = (zim * pw + zre * qw).astype(BF16)

    def inverse(order, z_sc, ny_sc):
        y = dot(c_ref[...], a_sc[...]) + dot(s_ref[...], b_sc[...])
        kny = jnp.concatenate([kny_ref[order:order + 1, :]] * ngrp, axis=1)
        skip = jnp.concatenate([skip_ref[order:order + 1, :]] * ngrp, axis=1)
        return y + alt_sign(start, tr) * (ny_sc[...] * kny) + skip * z_sc[rows, :].astype(F32)

    @phase(0)
    def _():
        forward(0, z0_sc)

    @phase(1)
    def _():
        z1 = (z1_sc[rows, :].astype(F32) * inverse(0, z0_sc, ny0_sc)).astype(BF16)
        z1_sc[rows, :] = z1

        @pl.when(r == 0)
        def _():
            ny1_sc[...] = jnp.zeros_like(ny1_sc)

        ny1_sc[...] += jnp.sum(alt_sign(start, tr) * z1.astype(F32), axis=0, keepdims=True)

    @phase(2)
    def _():
        forward(1, z1_sc)

    @phase(3)
    def _():
        out = x2_sc[rows, :].astype(F32) * inverse(1, z1_sc, ny1_sc)
        for g in range(ngrp):
            o_ref[g, rows, :] = out[:, g * tc:(g + 1) * tc].astype(o_ref.dtype)


def _hy_core(proj, short_w, short_b, skip, cos, sin, pq, k_ny, o_prev, row0, nseq, L, ngrp, tc):
    T = proj.shape[0]
    tr = min(L, 256)
    nrt = L // tr
    assert row0 % (L * ngrp) == 0 and nseq % ngrp == 0 and T % L == 0
    sb0 = row0 // (L * ngrp)
    nct = D // tc
    p3 = proj.reshape(T // L, L, 3 * D)
    p_arr, q_arr = pq

    def xspec(part):
        mode = {} if nrt == 1 else dict(pipeline_mode=pl.Buffered(1))
        return pl.BlockSpec((ngrp, L, tc), lambda i, j, ph, r: (sb0 + i, 0, part * nct + j), **mode)

    def wspec(part, rows_):
        return pl.BlockSpec((rows_, tc), lambda i, j, ph, r: (0, part * nct + j))

    fused = nrt == 1
    if fused:
        pq_spec = pl.BlockSpec((2, tr, tc), lambda i, j, ph, r: (0, 0, j))
    else:
        pq_spec = pl.BlockSpec((None, tr, tc),
                               lambda i, j, ph, r: (ph // 2, jnp.where(ph % 2 == 0, r, nrt - 1), j))

    tspec = pl.BlockSpec((tr, L), lambda i, j, ph, r: (r, 0))
    in_specs = [xspec(0), xspec(1), xspec(2), wspec(0, 3), wspec(1, 3), wspec(2, 3),
                wspec(0, 1), wspec(1, 1), wspec(2, 1), tspec, tspec, pq_spec, pq_spec,
                pl.BlockSpec((2, tc), lambda i, j, ph, r: (0, j)),
                pl.BlockSpec((2, tc), lambda i, j, ph, r: (0, j))]
    sb = short_b.reshape(1, 3 * D)
    args = [p3, p3, p3, short_w, short_w, short_w, sb, sb, sb, cos, sin, p_arr, q_arr, k_ny, skip]
    aliases = {}
    if o_prev is not None:
        in_specs.append(pl.BlockSpec(memory_space=pl.ANY))
        args.append(o_prev.reshape(T // L, L, D))
        aliases = {len(args) - 1: 0}
    W = ngrp * tc
    out = pl.pallas_call(
        functools.partial(_hy_core_kernel, L, tr, ngrp, tc),
        out_shape=jax.ShapeDtypeStruct((T // L, L, D), BF16),
        grid=(nseq // ngrp, nct, 1 if fused else 4, nrt),
        in_specs=in_specs,
        out_specs=pl.BlockSpec((ngrp, L, tc), lambda i, j, ph, r: (sb0 + i, 0, j)),
        scratch_shapes=[pltpu.VMEM((L, W), BF16)] * 5 + [pltpu.VMEM((1, W), F32)] * 2,
        input_output_aliases=aliases,
        compiler_params=_cp("arbitrary", "arbitrary", "arbitrary", "arbitrary"),
        name="hy_core",
    )(*args)
    return out.reshape(T, D)


FOLD_BLK = 256


def _hyena_filters_folded(L, p):
    H = L // 2
    mm = functools.partial(jnp.matmul, precision=HIGHEST)
    f = jnp.linspace(1e-4, HY_BANDS - 1, HY_BANDS, dtype=F32)[None, :]
    max_decay = math.log(HY_TARGET) / HY_FAST_PCT
    min_decay = math.log(HY_TARGET) / HY_SLOW_PCT
    deltas = jnp.abs(jnp.linspace(min_decay, max_decay, D, dtype=F32))
    w3 = p['f_w3'].reshape(-1, 2, 2, D)

    def side(pos, s):
        t = (pos.astype(F32) / (L - 1))[:, None]
        w = 2.0 * math.pi * pos.astype(F32)[:, None] / L
        feats = jnp.concatenate([t, jnp.cos(f * w), -jnp.sin(f * w)], axis=-1)
        z = jnp.sin(p['f_freq'][0] * (mm(feats, p['f_w1']) + p['f_b1']))
        z = jnp.sin(p['f_freq'][1] * (mm(z, p['f_w2']) + p['f_b2']))
        win = jnp.exp(-t * deltas)
        return jnp.stack([mm(z, w3[:, o, s]) * win for o in range(2)])

    t = jnp.arange(H, dtype=jnp.int32)
    pos = jnp.concatenate([t, (L - t) % L])
    mid = jnp.full((1,), H, jnp.int32)
    live = (t > 0).astype(F32)[None, :, None]
    s0 = side(pos, 0).reshape(2, 2, H, D)
    s1 = side(pos, 1).reshape(2, 2, H, D)
    klo_lo, klo_hr, klo_h = s0[:, 0], s0[:, 1] * live, side(mid, 0)[:, 0]
    khi_lo, khi_hr, khi_h = s1[:, 1] * live, s1[:, 0] * live, side(mid, 1)[:, 0]
    norm = sum(jnp.sum(jnp.abs(a), axis=1) for a in (klo_lo, klo_hr, khi_lo, khi_hr)) \
        + jnp.abs(klo_h) + jnp.abs(khi_h)
    alt = (1.0 - 2.0 * (t % 2).astype(F32))[None, :, None]
    alt_h = 1.0 - 2.0 * (H % 2)
    k_ny = (jnp.sum(alt * (klo_lo + klo_hr + khi_lo + khi_hr), axis=1) + alt_h * (klo_h + khi_h)) / norm / (2 * L)
    inv = (1.0 / norm)[:, None, :]
    p_lo, p_hr = (klo_lo + khi_lo) * inv, (klo_hr + khi_hr) * inv
    m_lo, m_hr = (klo_lo - khi_lo) * inv, (klo_hr - khi_hr) * inv
    xc = jnp.stack([p_lo + p_hr, m_lo - m_hr])
    xs = jnp.stack([p_lo - p_hr, m_lo + m_hr])
    xh = jnp.stack([(klo_h + khi_h) / norm, (klo_h - khi_h) / norm])
    return xc, xs, xh, k_ny


def _dft_tables_folded(L):
    H = L // 2
    ra = 1 << (int(math.log2(H)) // 2)
    rb = H // ra
    th = math.pi / L
    a = jnp.arange(ra, dtype=jnp.int32)[:, None]
    b = jnp.arange(rb, dtype=jnp.int32)[:, None]
    u = jnp.arange(H, dtype=jnp.int32)[None, :]

    def unit(idx):
        ang = (idx % (2 * L)).astype(F32) * th
        return jnp.cos(ang), jnp.sin(ang)

    e1 = [unit(2 * rb * a * u), unit(2 * rb * a * u), unit((2 * u + 1) * rb * a)]
    e2 = [unit(2 * b * u), unit((2 * b + 1) * u), unit((2 * u + 1) * b)]
    c1 = jnp.stack([e[0] for e in e1])[:, None, :, None, :]
    s1 = jnp.stack([e[1] for e in e1])[:, None, :, None, :]
    c2 = jnp.stack([e[0] for e in e2])[:, None, None, :, :]
    s2 = jnp.stack([e[1] for e in e2])[:, None, None, :, :]
    lead = jnp.concatenate([c1, s1], axis=1)
    cross = jnp.concatenate([-s1, c1], axis=1)
    return (lead * c2 + cross * s2).reshape(6, H, H).astype(BF16)


def _hy_spec_folded_kernel(L, tr, c_ref, s_ref, xc_ref, xs_ref, xh_ref, p_ref, q_ref):
    half = pl.program_id(2)
    r = pl.program_id(3)
    m = r * tr + lax.broadcasted_iota(jnp.int32, (tr, 1), 0)
    alt = (1 - 2 * (m % 2)).astype(F32)
    scale = jnp.where(jnp.logical_and(half == 0, m == 0), 1.0, 2.0) * (1.0 / (2 * L))
    mid = alt * xh_ref[...]
    kc = jnp.dot(c_ref[...], xc_ref[...], preferred_element_type=F32) + jnp.where(half == 0, mid, 0.0)
    ks = jnp.dot(s_ref[...], xs_ref[...], preferred_element_type=F32) + jnp.where(half == 1, mid, 0.0)
    p_ref[...] = scale * kc
    q_ref[...] = scale * ks


def _hy_spectrum_folded(L, tabs, xc, xs, xh):
    H = L // 2
    tr = min(H, 512)
    tc = 512
    nrt = H // tr
    xspec = pl.BlockSpec((None, None, H, tc), lambda o, j, hf, r: (hf, o, 0, j))
    hspec = pl.BlockSpec((None, None, 1, tc), lambda o, j, hf, r: (hf, o, 0, j))
    cspec = pl.BlockSpec((None, tr, H), lambda o, j, hf, r: (2 * hf, r, 0))
    sspec = pl.BlockSpec((None, tr, H), lambda o, j, hf, r: (2 * hf + 1, r, 0))
    ospec = pl.BlockSpec((None, tr, tc), lambda o, j, hf, r: (o, hf * nrt + r, j))
    return pl.pallas_call(
        functools.partial(_hy_spec_folded_kernel, L, tr),
        out_shape=(jax.ShapeDtypeStruct((2, L, D), F32), jax.ShapeDtypeStruct((2, L, D), F32)),
        grid=(2, D // tc, 2, nrt),
        in_specs=[cspec, sspec, xspec, xspec, hspec],
        out_specs=(ospec, ospec),
        compiler_params=_cp("arbitrary", "arbitrary", "arbitrary", "arbitrary"),
        name="hy_spectrum",
    )(tabs, tabs, xc.astype(BF16), xs.astype(BF16), xh[:, :, None, :])


def _hy_fold_kernel(L, tr, ngrp, tc, x1_ref, x2_ref, v_ref, sw1_ref, sw2_ref, swv_ref, sb1_ref, sb2_ref,
                    sbv_ref, ce_ref, se_ref, co_ref, so_ref, pe_ref, po_ref, qe_ref, qo_ref, kny_ref, skip_ref,
                    *rest):
    o_ref, z0_sc, z1_sc, x2_sc, a_sc, b_sc, mid_sc = rest[-7:]
    H = L // 2
    nrt = H // tr
    nb = H // FOLD_BLK
    W = ngrp * tc
    ph = pl.program_id(2)
    r = pl.program_id(3)
    tcv = min(L, 512)
    halo = 16
    dot = functools.partial(jnp.dot, preferred_element_type=F32)
    ZH, X1H, X2H, YSP, NY, NYACC = range(6)
    alt_h = 1.0 - 2.0 * (H % 2)

    def alt_sign(start, n):
        t = start + lax.broadcasted_iota(jnp.int32, (n, 1), 0)
        return (1 - 2 * (t % 2)).astype(F32)

    def conv3(src_ref, g, a, w_ref, b_ref):
        x = src_ref[g, pl.ds(a, tcv), :].astype(F32)
        row = lax.broadcasted_iota(jnp.int32, (tcv, 1), 0)
        up_at = pl.multiple_of(jnp.maximum(a - halo, 0), halo)
        dn_at = pl.multiple_of(jnp.minimum(a + tcv, L - halo), halo)
        up = src_ref[g, pl.ds(up_at, halo), :][halo - 1:halo, :].astype(F32)
        dn = src_ref[g, pl.ds(dn_at, halo), :][0:1, :].astype(F32)
        up = jnp.where(a > 0, up, 0.0)
        dn = jnp.where(a + tcv < L, dn, 0.0)
        prev = jnp.where(row == 0, up, pltpu.roll(x, 1, 0))
        nxt = jnp.where(row == tcv - 1, dn, pltpu.roll(x, tcv - 1, 0))
        return prev * w_ref[0:1, :] + x * w_ref[1:2, :] + nxt * w_ref[2:3, :] + b_ref[...]

    def flip_mats():
        u = lax.broadcasted_iota(jnp.int32, (FOLD_BLK, FOLD_BLK), 0)
        v = lax.broadcasted_iota(jnp.int32, (FOLD_BLK, FOLD_BLK), 1)
        jshift = jnp.where(jnp.logical_and(u >= 1, v == FOLD_BLK - u), 1.0, 0.0).astype(BF16)
        e0 = jnp.where(jnp.logical_and(u == 0, v == 0), 1.0, 0.0).astype(BF16)
        return jshift, e0

    def flipped_block(src, b, jshift, e0):
        blk = lambda k: src[H + k * FOLD_BLK:H + (k + 1) * FOLD_BLK, :]
        out = dot(jshift, blk(nb - 1 - b))
        if b >= 1:
            out = out + dot(e0, blk(nb - b))
        return out

    def fold_in_place(sc, stage, off, jshift, e0):
        for b in range(nb):
            stage[off + b * FOLD_BLK:off + (b + 1) * FOLD_BLK, :] = flipped_block(sc, b, jshift, e0).astype(BF16)

        def copy_back(b, carry):
            at = pl.multiple_of(b * FOLD_BLK, FOLD_BLK)
            sc[pl.ds(H + at, FOLD_BLK), :] = stage[pl.ds(off + at, FOLD_BLK), :]
            return carry

        lax.fori_loop(0, nb, copy_back, 0)

    def to_operands(sc):
        def tile(i, acc):
            a = pl.multiple_of(i * FOLD_BLK, FOLD_BLK)
            ra, rb = pl.ds(a, FOLD_BLK), pl.ds(pl.multiple_of(H + a, FOLD_BLK), FOLD_BLK)
            lo = sc[ra, :].astype(F32)
            hr = sc[rb, :].astype(F32)
            zs = (lo + hr).astype(BF16)
            sc[ra, :] = zs
            sc[rb, :] = (lo - hr).astype(BF16)
            return acc + jnp.sum(alt_sign(a, FOLD_BLK) * zs.astype(F32), axis=0, keepdims=True)

        return lax.fori_loop(0, nb, tile, jnp.zeros((1, W), F32))

    def row_of(sc, at):
        return sc[at:at + halo, :][0:1, :].astype(F32)

    @pl.when(jnp.logical_and(ph == 0, r == 0))
    def _():
        def conv_tile(ti, carry):
            a = pl.multiple_of(ti * tcv, tcv)
            rows_a = pl.ds(a, tcv)
            for g in range(ngrp):
                cols = slice(g * tc, (g + 1) * tc)
                z1_sc[rows_a, cols] = conv3(x1_ref, g, a, sw1_ref, sb1_ref).astype(BF16)
                x2_sc[rows_a, cols] = conv3(x2_ref, g, a, sw2_ref, sb2_ref).astype(BF16)
                z0_sc[rows_a, cols] = conv3(v_ref, g, a, swv_ref, sbv_ref).astype(BF16)
            return carry

        lax.fori_loop(0, L // tcv, conv_tile, 0)
        mid_sc[ZH:ZH + 1, :] = row_of(z0_sc, H)
        mid_sc[X1H:X1H + 1, :] = row_of(z1_sc, H)
        mid_sc[X2H:X2H + 1, :] = row_of(x2_sc, H)
        jshift, e0 = flip_mats()
        for sc, stage, off in ((z0_sc, a_sc, 0), (z1_sc, a_sc, H), (x2_sc, b_sc, 0)):
            fold_in_place(sc, stage, off, jshift, e0)
        mid_sc[NY:NY + 1, :] = to_operands(z0_sc) + alt_h * mid_sc[ZH:ZH + 1, :]
        mid_sc[YSP:YSP + 1, :] = jnp.zeros((1, W), F32)
        mid_sc[NYACC:NYACC + 1, :] = jnp.zeros((1, W), F32)

    start = pl.multiple_of(r * tr, tr)
    rows = pl.ds(start, tr)
    rows_hi = pl.ds(pl.multiple_of(H + r * tr, tr), tr)
    rep = lambda ref: jnp.concatenate([ref[...]] * ngrp, axis=1)
    rep_row = lambda ref, o: jnp.concatenate([ref[o:o + 1, :]] * ngrp, axis=1)

    def forward(z_sc):
        zs, zd = z_sc[0:H, :], z_sc[H:L, :]
        mid = alt_sign(start, tr) * mid_sc[ZH:ZH + 1, :]
        zre_e = dot(ce_ref[...], zs) + mid
        zim_e = dot(se_ref[...], zd)
        zre_o = dot(co_ref[...], zd)
        zim_o = dot(so_ref[...], zs) + mid
        pe, qe, po, qo = rep(pe_ref), rep(qe_ref), rep(po_ref), rep(qo_ref)
        a_e = zre_e * pe - zim_e * qe
        b_e = zim_e * pe + zre_e * qe
        a_o = zre_o * po - zim_o * qo
        b_o = zim_o * po + zre_o * qo
        a_sc[rows, :] = a_e.astype(BF16)
        b_sc[rows, :] = b_e.astype(BF16)
        a_sc[rows_hi, :] = a_o.astype(BF16)
        b_sc[rows_hi, :] = b_o.astype(BF16)
        mid_sc[YSP:YSP + 1, :] += jnp.sum(alt_sign(start, tr) * (a_e + b_o), axis=0, keepdims=True)

    def inverse(order, zin_sc):
        pe = dot(ce_ref[...], a_sc[0:H, :])
        qe = dot(se_ref[...], b_sc[0:H, :])
        po = dot(co_ref[...], a_sc[H:L, :])
        qo = dot(so_ref[...], b_sc[H:L, :])
        c = alt_sign(start, tr) * (mid_sc[NY:NY + 1, :] * rep_row(kny_ref, order))
        half_skip = 0.5 * rep_row(skip_ref, order)
        zs = zin_sc[rows, :].astype(F32)
        zd = zin_sc[rows_hi, :].astype(F32)
        y_lo = pe + qe + po + qo + c + half_skip * (zs + zd)
        y_hr = pe - qe - po + qo + c + half_skip * (zs - zd)
        return y_lo, y_hr

    def middle(order):
        return (mid_sc[YSP:YSP + 1, :] + alt_h * mid_sc[NY:NY + 1, :] * rep_row(kny_ref, order)
                + rep_row(skip_ref, order) * mid_sc[ZH:ZH + 1, :])

    @pl.when(ph == 0)
    def _():
        forward(z0_sc)

    @pl.when(ph == 1)
    def _():
        y_lo, y_hr = inverse(0, z0_sc)
        z_lo = (z1_sc[rows, :].astype(F32) * y_lo).astype(BF16)
        z_hr = (z1_sc[rows_hi, :].astype(F32) * y_hr).astype(BF16)
        z1_sc[rows, :] = z_lo
        z1_sc[rows_hi, :] = z_hr
        mid_sc[NYACC:NYACC + 1, :] += jnp.sum(alt_sign(start, tr) * (z_lo.astype(F32) + z_hr.astype(F32)),
                                              axis=0, keepdims=True)

        @pl.when(r == nrt - 1)
        def _():
            z1h = (mid_sc[X1H:X1H + 1, :] * middle(0)).astype(BF16).astype(F32)
            mid_sc[ZH:ZH + 1, :] = z1h
            mid_sc[NY:NY + 1, :] = mid_sc[NYACC:NYACC + 1, :] + alt_h * z1h
            mid_sc[YSP:YSP + 1, :] = jnp.zeros((1, W), F32)

    @pl.when(ph == 2)
    def _():
        @pl.when(r == 0)
        def _():
            to_operands(z1_sc)

        forward(z1_sc)

    @pl.when(ph == 3)
    def _():
        y_lo, y_hr = inverse(1, z1_sc)
        z0_sc[rows, :] = (x2_sc[rows, :].astype(F32) * y_lo).astype(BF16)
        z0_sc[rows_hi, :] = (x2_sc[rows_hi, :].astype(F32) * y_hr).astype(BF16)

        @pl.when(r == nrt - 1)
        def _():
            out_h = mid_sc[X2H:X2H + 1, :] * middle(1)
            jshift, e0 = flip_mats()
            first = (lax.broadcasted_iota(jnp.int32, (FOLD_BLK, 1), 0) == 0).astype(F32)
            for b in range(nb):
                up = flipped_block(z0_sc, b, jshift, e0)
                if b == 0:
                    up = up + first * out_h
                for g in range(ngrp):
                    cols = slice(g * tc, (g + 1) * tc)
                    o_ref[g, H + b * FOLD_BLK:H + (b + 1) * FOLD_BLK, :] = up[:, cols].astype(o_ref.dtype)
            for b in range(nb):
                lo_rows = slice(b * FOLD_BLK, (b + 1) * FOLD_BLK)
                for g in range(ngrp):
                    o_ref[g, lo_rows, :] = z0_sc[lo_rows, g * tc:(g + 1) * tc]


def _hy_core_folded(proj, short_w, short_b, skip, tabs, pq, k_ny, o_prev, row0, nseq, L, ngrp, tc):
    T = proj.shape[0]
    H = L // 2
    tr = min(H, 256)
    nrt = H // tr
    assert H % FOLD_BLK == 0 and row0 % (L * ngrp) == 0 and nseq % ngrp == 0 and T % L == 0
    sb0 = row0 // (L * ngrp)
    nct = D // tc
    p3 = proj.reshape(T // L, L, 3 * D)
    p_arr, q_arr = pq
    once = dict(pipeline_mode=pl.Buffered(1))

    def xspec(part):
        return pl.BlockSpec((ngrp, L, tc), lambda i, j, ph, r: (sb0 + i, 0, part * nct + j), **once)

    def wspec(part, rows_):
        return pl.BlockSpec((rows_, tc), lambda i, j, ph, r: (0, part * nct + j))

    def tspec(k_fwd, k_inv):
        return pl.BlockSpec((None, tr, H), lambda i, j, ph, r: (jnp.where(ph % 2 == 0, k_fwd, k_inv), r, 0))

    def pqspec(half):
        def imap(i, j, ph, r):
            return (ph // 2, half * nrt + jnp.where(ph % 2 == 0, r, nrt - 1), j)
        return pl.BlockSpec((None, tr, tc), imap)

    in_specs = [xspec(0), xspec(1), xspec(2), wspec(0, 3), wspec(1, 3), wspec(2, 3),
                wspec(0, 1), wspec(1, 1), wspec(2, 1),
                tspec(0, 0), tspec(1, 1), tspec(2, 4), tspec(3, 5),
                pqspec(0), pqspec(1), pqspec(0), pqspec(1),
                pl.BlockSpec((2, tc), lambda i, j, ph, r: (0, j)),
                pl.BlockSpec((2, tc), lambda i, j, ph, r: (0, j))]
    sb = short_b.reshape(1, 3 * D)
    args = [p3, p3, p3, short_w, short_w, short_w, sb, sb, sb, tabs, tabs, tabs, tabs,
            p_arr, p_arr, q_arr, q_arr, k_ny, skip]
    in_specs.append(pl.BlockSpec(memory_space=pl.ANY))
    args.append(o_prev.reshape(T // L, L, D))
    aliases = {len(args) - 1: 0}
    W = ngrp * tc
    out = pl.pallas_call(
        functools.partial(_hy_fold_kernel, L, tr, ngrp, tc),
        out_shape=jax.ShapeDtypeStruct((T // L, L, D), BF16),
        grid=(nseq // ngrp, nct, 4, nrt),
        in_specs=in_specs,
        out_specs=pl.BlockSpec((ngrp, L, tc), lambda i, j, ph, r: (sb0 + i, 0, j)),
        scratch_shapes=[pltpu.VMEM((L, W), BF16)] * 5 + [pltpu.VMEM((8, W), F32)],
        input_output_aliases=aliases,
        compiler_params=_cp("arbitrary", "arbitrary", "arbitrary", "arbitrary"),
        name="hy_core",
    )(*args)
    return out.reshape(T, D)


def _plain_out_kernel(x_ref, z_ref, g_ref, w_ref, b_ref, o_ref):
    acc = jnp.dot(z_ref[...], w_ref[...], preferred_element_type=F32) + b_ref[...]
    o_ref[...] = x_ref[...] + g_ref[...] * acc


def _plain_out(lay, x, mods, z, w, b):
    tm = lay.tile(512)
    tn = D
    kdim = z.shape[1]
    return pl.pallas_call(
        _plain_out_kernel,
        out_shape=jax.ShapeDtypeStruct((lay.T, D), F32),
        grid=(lay.T // tm, D // tn),
        in_specs=[pl.BlockSpec((tm, tn), lambda i, j: (i, j)),
                  pl.BlockSpec((tm, kdim), lambda i, j: (i, 0)),
                  pl.BlockSpec((None, 1, tn), lambda i, j: (lay.group(i * tm) * MOD_CHUNKS + 2, 0, j)),
                  pl.BlockSpec((kdim, tn), lambda i, j: (0, j), pipeline_mode=pl.Buffered(1)),
                  pl.BlockSpec((1, tn), lambda i, j: (0, j))],
        out_specs=pl.BlockSpec((tm, tn), lambda i, j: (i, j)),
        compiler_params=_cp("arbitrary", "arbitrary"),
        name="plain_out",
    )(x, z, mods, w, b.reshape(1, D))


def _hyena_layer(lay, x, mods, p):
    proj = _proj(lay, x, mods, 0, 1, p['w_in'].astype(BF16), p['b_in'].reshape(1, 3 * D), 3 * D, tm=512)
    z = jnp.zeros((lay.T, D), BF16)
    for row0, nseq, L, ngrp, tc in ((0, lay.B, lay.L, math.gcd(lay.B, 4), 256),
                                    (lay.TP, lay.NS, lay.LS, lay.NS, 256)):
        conv = (proj, p['short_w'], p['short_b'], p['skip'])
        if (L // 2) % FOLD_BLK == 0 and L > 2 * FOLD_BLK:
            xc, xs, xh, k_ny = _hyena_filters_folded(L, p)
            tabs = _dft_tables_folded(L)
            pq = _hy_spectrum_folded(L, tabs, xc, xs, xh)
            z = _hy_core_folded(*conv, tabs, pq, k_ny, z, row0, nseq, L, ngrp, tc)
        else:
            k_lo, k_hi, k_ny = _hyena_filters(L, p)
            cos, sin = _dft_tables(L)
            pq = _hy_spectrum(L, cos, sin, k_lo, k_hi)
            z = _hy_core(*conv, cos, sin, pq, k_ny, z, row0, nseq, L, ngrp, tc)
    return _plain_out(lay, x, mods, z, p['w_out'].astype(BF16), p['b_out'])


_NT = (((1,), (1,)), ((), ()))
_TN = (((0,), (0,)), ((), ()))


def _tri(dr):
    t = lax.broadcasted_iota(jnp.int32, (CHUNK, CHUNK), 0)
    s = lax.broadcasted_iota(jnp.int32, (CHUNK, CHUNK), 1)
    return (s <= t) if dr == 0 else (s >= t)


def _chunk_cumsum(g, dr):
    n = g.shape[0]
    pos = lax.broadcasted_iota(jnp.int32, g.shape, 0) % CHUNK
    sh = 1
    while sh < CHUNK:
        if dr == 0:
            g = g + jnp.where(pos >= sh, pltpu.roll(g, sh, 0), 0.0)
        else:
            g = g + jnp.where(pos < CHUNK - sh, pltpu.roll(g, n - sh, 0), 0.0)
        sh *= 2
    return g


def _head_epilogue(o_sc, gate_ref, ng_ref, a_ref, center):
    rows = o_sc.shape[0]
    tr = math.gcd(rows, 256)

    def tile(i, carry):
        r = pl.ds(pl.multiple_of(i * tr, tr), tr)
        o = o_sc[r, :]
        if center:
            o = o - jnp.mean(o, axis=-1, keepdims=True)
        o = o * lax.rsqrt(jnp.mean(o * o, axis=-1, keepdims=True) + RMS_EPS) * ng_ref[...]
        a_ref[r, :] = (o * _silu(gate_ref[r, :].astype(F32))).astype(a_ref.dtype)
        return carry

    lax.fori_loop(0, rows // tr, tile, 0)


def _gla_kernel(cps, nseg, U, has_s0, want_final, *refs):
    q_ref, k_ref, v_ref, lr_ref, w2f_ref, w2b_ref, gb_ref, gate_ref, ng_ref = refs[:9]
    s0_ref = refs[9] if has_s0 else None
    qin_sc, kin_sc, kout_sc, dec_sc, st_sc, s_sc, s0t_sc, o_ref = refs[-8:]
    outs = refs[-10:-8] if want_final else refs[-9:-8]
    a_ref = outs[0]
    sf_ref = outs[1] if want_final else None
    C = CHUNK
    nsc = cps // U
    nchunks = nseg * cps
    rows_total = nchunks * C
    w2 = (w2f_ref, w2b_ref)

    for dr in range(2):
        pre = jnp.dot(lr_ref[...], w2[dr][...], preferred_element_type=F32) + gb_ref[dr:dr + 1, :]
        g = (jnp.minimum(pre, 0.0) - jnp.log(1.0 + jnp.exp(-jnp.abs(pre)))) * (1.0 / GLA_TAU)
        b = _chunk_cumsum(g, dr)
        b3 = b.reshape(nchunks, C, GLA_DK)
        tot = b3[:, C - 1:C, :] if dr == 0 else b3[:, 0:1, :]
        dec_sc[...] = jnp.exp(tot).reshape(nchunks, GLA_DK)
        k = k_ref[...].astype(F32)
        qin_sc[...] = (q_ref[...].astype(F32) * (GLA_DK ** -0.5) * jnp.exp(b)).astype(BF16)
        kin_sc[...] = (k * jnp.exp(-b)).astype(BF16)
        kout_sc[...] = (k * jnp.exp(tot - b3).reshape(rows_total, GLA_DK)).astype(BF16)
        if has_s0:
            s0t_sc[...] = jnp.transpose(s0_ref[dr], (1, 0))
        tri = _tri(dr)

        def super_chunk(jj, carry, dr=dr, tri=tri):
            j = jj if dr == 0 else nseg * nsc - 1 - jj
            in_seg = j % nsc
            first = (in_seg == 0) if dr == 0 else (in_seg == nsc - 1)
            last = (in_seg == nsc - 1) if dr == 0 else (in_seg == 0)

            @pl.when(first)
            def _():
                s_sc[...] = s0t_sc[...] if has_s0 else jnp.zeros_like(s_sc)

            base = j * (U * C)
            for u in range(U):
                rows = pl.ds(pl.multiple_of(base + u * C, C), C)
                v = v_ref[rows, :]
                sc = lax.dot_general(qin_sc[rows, :], kin_sc[rows, :], _NT, preferred_element_type=F32)
                o = jnp.dot(jnp.where(tri, sc, 0.0).astype(BF16), v, preferred_element_type=F32)
                st_sc[u] = lax.dot_general(v, kout_sc[rows, :], _TN, preferred_element_type=F32)
                if dr == 0:
                    o_ref[rows, :] = o
                else:
                    o_ref[rows, :] += o
            s = s_sc[...]
            for u in (range(U) if dr == 0 else reversed(range(U))):
                kv = st_sc[u]
                st_sc[u] = s
                s = dec_sc[pl.ds(j * U + u, 1), :] * s + kv
            s_sc[...] = s
            for u in range(U):
                rows = pl.ds(pl.multiple_of(base + u * C, C), C)
                o_ref[rows, :] += lax.dot_general(qin_sc[rows, :], st_sc[u].astype(BF16), _NT,
                                                  preferred_element_type=F32)
            if want_final:
                @pl.when(last)
                def _():
                    sf_ref[j // nsc, dr] = jnp.transpose(s, (1, 0))
            return carry

        lax.fori_loop(0, nseg * nsc, super_chunk, 0)

    _head_epilogue(o_ref, gate_ref, ng_ref, a_ref, center=False)


def _gla_core(proj, w2f, w2b, gate_b, norm_g, s0, o_prev, row0, nseq, seqlen, nseg, want_final):
    T = proj.shape[0]
    rows = nseg * seqlen
    cps = seqlen // CHUNK
    U = math.gcd(cps, 8)
    assert row0 % rows == 0 and seqlen % CHUNK == 0 and nseq % nseg == 0
    rb = row0 // rows
    hk = GLA_H * GLA_DK
    has_s0 = s0 is not None
    assert not has_s0 or nseg == 1
    in_specs = [pl.BlockSpec((rows, GLA_DK), lambda b, h: (rb + b, h)),
                pl.BlockSpec((rows, GLA_DK), lambda b, h: (rb + b, GLA_H + h)),
                pl.BlockSpec((rows, GLA_DV), lambda b, h: (rb + b, 2 * hk // GLA_DV + h)),
                pl.BlockSpec((rows, 128), lambda b, h: (rb + b, (2 * hk + 2 * GLA_H * GLA_DV) // 128)),
                pl.BlockSpec((128, GLA_DK), lambda b, h: (0, h)),
                pl.BlockSpec((128, GLA_DK), lambda b, h: (0, h)),
                pl.BlockSpec((2, GLA_DK), lambda b, h: (0, h)),
                pl.BlockSpec((rows, GLA_DV), lambda b, h: (rb + b, (2 * hk) // GLA_DV + GLA_H + h)),
                pl.BlockSpec((1, GLA_DV), lambda b, h: (0, 0))]
    args = [proj, proj, proj, proj, w2f, w2b, gate_b, proj, norm_g.reshape(1, GLA_DV)]
    if has_s0:
        in_specs.append(pl.BlockSpec((None, 2, None, GLA_DK, GLA_DV), lambda b, h: (b, 0, h, 0, 0)))
        args.append(s0)
    in_specs.append(pl.BlockSpec(memory_space=pl.ANY))
    args.append(o_prev)
    aliases = {len(args) - 1: 0}
    out_shape = [jax.ShapeDtypeStruct((T, GLA_H * GLA_DV), BF16)]
    out_specs = [pl.BlockSpec((rows, GLA_DV), lambda b, h: (rb + b, h))]
    if want_final:
        out_shape.append(jax.ShapeDtypeStruct((nseq, 2, GLA_H, GLA_DK, GLA_DV), F32))
        out_specs.append(pl.BlockSpec((nseg, 2, None, GLA_DK, GLA_DV), lambda b, h: (b, 0, h, 0, 0)))
    outs = pl.pallas_call(
        functools.partial(_gla_kernel, cps, nseg, U, has_s0, want_final),
        out_shape=tuple(out_shape),
        grid=(nseq // nseg, GLA_H),
        in_specs=in_specs,
        out_specs=tuple(out_specs),
        scratch_shapes=[pltpu.VMEM((rows, GLA_DK), BF16)] * 3
        + [pltpu.VMEM((nseg * cps, GLA_DK), F32), pltpu.VMEM((U, GLA_DV, GLA_DK), F32),
           pltpu.VMEM((GLA_DV, GLA_DK), F32), pltpu.VMEM((GLA_DV, GLA_DK), F32),
           pltpu.VMEM((rows, GLA_DV), F32)],
        input_output_aliases=aliases,
        compiler_params=_cp("arbitrary", "arbitrary"),
        name="gla",
    )(*args)
    return (outs[0], outs[1]) if want_final else (outs[0], None)


def _ret_kernel(cps, nseg, U, has_s0, want_final, rope, *refs):
    q_ref, k_ref, v_ref, dm_ref, qd_ref, kd_ref, cd_ref, gate_ref, ng_ref = refs[:9]
    nxt = 9
    if rope:
        cos_ref, sin_ref = refs[9:11]
        nxt = 11
    s0_ref = refs[nxt] if has_s0 else None
    qr_sc, kr_sc, qd_sc, kd_sc, st_sc, s_sc, o_ref = refs[-7:]
    outs = refs[-9:-7] if want_final else refs[-8:-7]
    a_ref = outs[0]
    sf_ref = outs[1] if want_final else None
    C = RET_CHUNK
    nsc = cps // U
    R = U * C
    SB = 64

    def rot(x, rows):
        if not rope:
            return x
        half = x.shape[1] // 2
        swapped = jnp.concatenate([pltpu.roll(x[:, :half], half // 2, 1),
                                   pltpu.roll(x[:, half:], half // 2, 1)], axis=1)
        return x * cos_ref[rows, :] + swapped * sin_ref[rows, :]

    for dr in range(2):
        def super_chunk(jj, carry, dr=dr):
            j = jj if dr == 0 else nseg * nsc - 1 - jj
            in_seg = j % nsc
            first = (in_seg == 0) if dr == 0 else (in_seg == nsc - 1)
            last = (in_seg == nsc - 1) if dr == 0 else (in_seg == 0)

            @pl.when(first)
            def _():
                s_sc[...] = s0_ref[dr] if has_s0 else jnp.zeros_like(s_sc)

            base = pl.multiple_of(j * R, R)
            rows_r = pl.ds(base, R)
            q = rot(q_ref[rows_r, :].astype(F32), rows_r)
            k = rot(k_ref[rows_r, :].astype(F32), rows_r) * (RET_DK ** -0.5)
            qr_sc[...] = q.astype(BF16)
            kr_sc[...] = k.astype(BF16)
            qd_sc[...] = (q.reshape(U, C, RET_DK) * qd_ref[dr][None]).reshape(R, RET_DK).astype(BF16)
            kd_sc[...] = (k.reshape(U, C, RET_DK) * kd_ref[dr][None]).reshape(R, RET_DK).astype(BF16)
            for u in range(U):
                loc = pl.ds(u * C, C)
                rows = pl.ds(pl.multiple_of(base + u * C, C), C)
                v = v_ref[rows, :]
                sc = lax.dot_general(qr_sc[loc, :], kr_sc[loc, :], _NT, preferred_element_type=F32)
                o = jnp.dot((sc * dm_ref[dr]).astype(BF16), v, preferred_element_type=F32)
                st_sc[u] = lax.dot_general(kd_sc[loc, :], v, _TN, preferred_element_type=F32)
                if dr == 0:
                    o_ref[rows, :] = o
                else:
                    o_ref[rows, :] += o
            cd = cd_ref[dr]
            for r0 in range(0, RET_DK, SB):
                srows = pl.ds(r0, SB)
                s = s_sc[srows, :]
                for u in (range(U) if dr == 0 else reversed(range(U))):
                    kv = st_sc[u, srows, :]
                    st_sc[u, srows, :] = s
                    s = cd * s + kv
                s_sc[srows, :] = s
            for u in range(U):
                rows = pl.ds(pl.multiple_of(base + u * C, C), C)
                o_ref[rows, :] += jnp.dot(qd_sc[pl.ds(u * C, C), :], st_sc[u].astype(BF16),
                                          preferred_element_type=F32)
            if want_final:
                @pl.when(last)
                def _():
                    sf_ref[j // nsc, dr] = s_sc[...]
            return carry

        lax.fori_loop(0, nseg * nsc, super_chunk, 0)

    _head_epilogue(o_ref, gate_ref, ng_ref, a_ref, center=True)


def _ret_tables(log_decay):
    C = RET_CHUNK
    lg = log_decay.astype(F32)[:, :, None, None]
    t = jnp.arange(C, dtype=F32)[:, None]
    s = jnp.arange(C, dtype=F32)[None, :]
    lag = jnp.stack([t - s, s - t])[:, None]
    dmask = jnp.where(lag >= 0, jnp.exp(jnp.maximum(lag, 0.0) * lg), 0.0)
    tl = jnp.arange(C, dtype=F32)[None, None, :, None]
    qdec = jnp.concatenate([jnp.exp((tl + 1.0) * lg[0:1]), jnp.exp((C - tl) * lg[1:2])], axis=0)
    kdec = jnp.concatenate([jnp.exp((C - 1.0 - tl) * lg[0:1]), jnp.exp(tl * lg[1:2])], axis=0)
    cdec = jnp.exp(C * lg)
    return dmask, qdec, kdec, cdec


def _rope_tables(seqlen, dk):
    half = dk // 2
    nf = half // 2
    pos = jnp.arange(seqlen, dtype=jnp.int32)
    inv = ROPE_BASE ** (-jnp.arange(nf, dtype=F32) / nf)
    ang_r = (pos // GRID_W).astype(F32)[:, None] * inv[None, :]
    ang_c = (pos % GRID_W).astype(F32)[:, None] * inv[None, :]
    cos = jnp.concatenate([jnp.cos(ang_r)] * 2 + [jnp.cos(ang_c)] * 2, axis=1)
    sin = jnp.concatenate([-jnp.sin(ang_r), jnp.sin(ang_r), -jnp.sin(ang_c), jnp.sin(ang_c)], axis=1)
    return cos, sin


def _ret_core(proj, tabs, norm_g, s0, o_prev, row0, nseq, seqlen, nseg, want_final, rope):
    T = proj.shape[0]
    rows = nseg * seqlen
    C = RET_CHUNK
    cps = seqlen // C
    U = math.gcd(cps, 4)
    assert row0 % rows == 0 and seqlen % C == 0 and nseq % nseg == 0
    rb = row0 // rows
    hk, hv = RET_H * RET_DK, RET_H * RET_DV
    has_s0 = s0 is not None
    assert not (has_s0 or rope) or nseg == 1
    tspec = lambda r, c: pl.BlockSpec((2, None, r, c), lambda b, h: (0, h, 0, 0))
    mode = dict(pipeline_mode=pl.Buffered(1)) if rows * RET_DV * 2 >= (4 << 20) else {}
    in_specs = [pl.BlockSpec((rows, RET_DK), lambda b, h: (rb + b, h), **mode),
                pl.BlockSpec((rows, RET_DK), lambda b, h: (rb + b, RET_H + h), **mode),
                pl.BlockSpec((rows, RET_DV), lambda b, h: (rb + b, 2 * hk // RET_DV + h), **mode),
                tspec(C, C), tspec(C, 1), tspec(C, 1), tspec(1, 1),
                pl.BlockSpec((rows, RET_DV), lambda b, h: (rb + b, (2 * hk + hv) // RET_DV + h), **mode),
                pl.BlockSpec((1, RET_DV), lambda b, h: (0, 0))]
    args = [proj, proj, proj, *tabs, proj, norm_g.reshape(1, RET_DV)]
    if rope:
        cos, sin = _rope_tables(seqlen, RET_DK)
        in_specs += [pl.BlockSpec((seqlen, RET_DK), lambda b, h: (0, 0), pipeline_mode=pl.Buffered(1))] * 2
        args += [cos, sin]
    if has_s0:
        in_specs.append(pl.BlockSpec((None, 2, None, RET_DK, RET_DV), lambda b, h: (b, 0, h, 0, 0)))
        args.append(s0)
    in_specs.append(pl.BlockSpec(memory_space=pl.ANY))
    args.append(o_prev)
    aliases = {len(args) - 1: 0}
    out_shape = [jax.ShapeDtypeStruct((T, hv), BF16)]
    out_specs = [pl.BlockSpec((rows, RET_DV), lambda b, h: (rb + b, h))]
    if want_final:
        out_shape.append(jax.ShapeDtypeStruct((nseq, 2, RET_H, RET_DK, RET_DV), F32))
        out_specs.append(pl.BlockSpec((nseg, 2, None, RET_DK, RET_DV), lambda b, h: (b, 0, h, 0, 0)))
    outs = pl.pallas_call(
        functools.partial(_ret_kernel, cps, nseg, U, has_s0, want_final, rope),
        out_shape=tuple(out_shape),
        grid=(nseq // nseg, RET_H),
        in_specs=in_specs,
        out_specs=tuple(out_specs),
        scratch_shapes=[pltpu.VMEM((U * C, RET_DK), BF16)] * 4
        + [pltpu.VMEM((U, RET_DK, RET_DV), F32), pltpu.VMEM((RET_DK, RET_DV), F32),
           pltpu.VMEM((rows, RET_DV), F32)],
        input_output_aliases=aliases,
        compiler_params=_cp("arbitrary", "arbitrary"),
        name="ret",
    )(*args)
    return (outs[0], outs[1]) if want_final else (outs[0], None)


def _gla_layer(lay, x, mods, p, s0):
    hk, hv = GLA_H * GLA_DK, GLA_H * GLA_DV
    w_all = jnp.concatenate([p['w_in'], p['gate_w1'][0], p['gate_w1'][1],
                             jnp.zeros((D, 128 - 2 * GLA_RANK), F32)], axis=1).astype(BF16)
    proj = _proj(lay, x, mods, 0, 1, w_all, jnp.zeros((1, w_all.shape[1]), F32), w_all.shape[1], tm=512)
    pad = lambda w, lo: jnp.pad(w, ((lo, 128 - GLA_RANK - lo), (0, 0))).astype(BF16)
    w2f, w2b = pad(p['gate_w2'][0], 0), pad(p['gate_w2'][1], GLA_RANK)
    a = jnp.zeros((lay.T, hv), BF16)
    a, s_fin = _gla_core(proj, w2f, w2b, p['gate_b'], p['norm_g'], None, a, 0, lay.B, lay.L,
                         math.gcd(lay.B, 8), True)
    a, _ = _gla_core(proj, w2f, w2b, p['gate_b'], p['norm_g'], s0, a, lay.TP, lay.NS, lay.LS, 1, False)
    x = _plain_out(lay, x, mods, a, p['w_out'].astype(BF16), jnp.zeros((D,), F32))
    return x, s_fin


def _ret_layer(lay, x, mods, p, s0):
    hk, hv = RET_H * RET_DK, RET_H * RET_DV
    proj = _proj(lay, x, mods, 0, 1, p['w_in'].astype(BF16), jnp.zeros((1, 2 * hk + 2 * hv), F32),
                 2 * hk + 2 * hv, tm=512)
    tabs = _ret_tables(p['log_decay'])
    a = jnp.zeros((lay.T, hv), BF16)
    a, s_fin = _ret_core(proj, tabs, p['norm_g'], None, a, 0, lay.B, lay.L, math.gcd(lay.B, 8), True, False)
    a, _ = _ret_core(proj, tabs, p['norm_g'], s0, a, lay.TP, lay.NS, lay.LS, 1, False, True)
    x = _plain_out(lay, x, mods, a, p['w_out'].astype(BF16), jnp.zeros((D,), F32))
    return x, s_fin


MOE_BM = 512
EXPERT_TF = 1792
ROUTER_LANES = 128
DMA_UNROLL = 8


def _router_kernel(x_ref, sh_ref, sc_ref, rw_ref, h_ref, idx_ref, gate_ref, rank_ref, cnt_ref):
    @pl.when(pl.program_id(0) == 0)
    def _():
        cnt_ref[...] = jnp.zeros_like(cnt_ref)

    h = _modulate(x_ref[...], sh_ref[...], sc_ref[...])
    h_ref[...] = h
    logits = jnp.dot(h, rw_ref[...], precision=HIGHEST, preferred_element_type=F32)
    lane = lax.broadcasted_iota(jnp.int32, logits.shape, 1)
    neg = jnp.float32(-jnp.inf)
    logits = jnp.where(lane < N_EXPERTS, logits, neg)
    m1 = jnp.max(logits, axis=-1, keepdims=True)
    i1 = jnp.min(jnp.where(logits == m1, lane, ROUTER_LANES), axis=-1, keepdims=True)
    rest = jnp.where(lane == i1, neg, logits)
    m2 = jnp.max(rest, axis=-1, keepdims=True)
    i2 = jnp.min(jnp.where(rest == m2, lane, ROUTER_LANES), axis=-1, keepdims=True)
    e2 = jnp.exp(m2 - m1)
    g1 = 1.0 / (1.0 + e2)
    idx_ref[:, 0:1] = i1
    idx_ref[:, 1:2] = i2
    gate_ref[:, 0:1] = g1
    gate_ref[:, 1:2] = e2 * g1
    tm = logits.shape[0]
    sel1 = lane == i1
    sel2 = lane == i2
    picked = jnp.where(jnp.logical_or(sel1, sel2), 1.0, 0.0)
    before = (lax.broadcasted_iota(jnp.int32, (tm, tm), 1)
              < lax.broadcasted_iota(jnp.int32, (tm, tm), 0)).astype(BF16)
    prior = jnp.dot(before, picked.astype(BF16), preferred_element_type=F32) + cnt_ref[...]
    rank_ref[:, 0:1] = jnp.sum(jnp.where(sel1, prior, 0.0), axis=-1, keepdims=True).astype(jnp.int32)
    rank_ref[:, 1:2] = jnp.sum(jnp.where(sel2, prior, 0.0), axis=-1, keepdims=True).astype(jnp.int32)
    cnt_ref[...] += jnp.sum(picked, axis=0, keepdims=True)


def _router(lay, x, mods, router_w):
    tm = lay.tile(512)
    rw = jnp.pad(router_w, ((0, 0), (0, ROUTER_LANES - N_EXPERTS)))
    return pl.pallas_call(
        _router_kernel,
        out_shape=(jax.ShapeDtypeStruct((lay.T, D), F32),
                   jax.ShapeDtypeStruct((lay.T, 2), jnp.int32),
                   jax.ShapeDtypeStruct((lay.T, 2), F32),
                   jax.ShapeDtypeStruct((lay.T, 2), jnp.int32),
                   jax.ShapeDtypeStruct((1, ROUTER_LANES), F32)),
        grid=(lay.T // tm,),
        in_specs=[pl.BlockSpec((tm, D), lambda i: (i, 0)),
                  _mod_spec(lay, tm, 3, 1), _mod_spec(lay, tm, 4, 1),
                  pl.BlockSpec((D, ROUTER_LANES), lambda i: (0, 0))],
        out_specs=(pl.BlockSpec((tm, D), lambda i: (i, 0)),
                   pl.BlockSpec((tm, 2), lambda i: (i, 0)),
                   pl.BlockSpec((tm, 2), lambda i: (i, 0)),
                   pl.BlockSpec((tm, 2), lambda i: (i, 0)),
                   pl.BlockSpec((1, ROUTER_LANES), lambda i: (0, 0))),
        compiler_params=_cp("arbitrary"),
        name="router",
    )(x, mods, mods, rw)


def _moe_plan(idx, rank, counts, bm):
    a = idx.size
    counts = counts[0, :N_EXPERTS].astype(jnp.int32)
    padded = (counts + bm - 1) // bm * bm
    pad_end = jnp.cumsum(padded)
    pad_start = pad_end - padded
    hit = idx[..., None] == jnp.arange(N_EXPERTS, dtype=jnp.int32)
    dest = (rank + jnp.sum(jnp.where(hit, pad_start, 0), axis=-1)).reshape(a).astype(jnp.int32)
    nb = -(-(a + N_EXPERTS * (bm - 1)) // bm)
    block_start = jnp.arange(nb, dtype=jnp.int32) * bm
    block_e = jnp.sum((block_start[:, None] >= pad_end[None, :]).astype(jnp.int32), axis=1)
    block_e = jnp.minimum(block_e, N_EXPERTS - 1)
    nvalid = (pad_end[-1] // bm).astype(jnp.int32).reshape(1)
    fill = jnp.concatenate([pad_start + counts, pad_end, nvalid]).astype(jnp.int32)
    return dest, fill, block_e, nvalid, nb


def _dispatch_kernel(tm, bm, nb, dest_ref, fill_ref, h_ref, xs_hbm, zero_sc, sem, zsem):
    i = pl.program_id(0)
    zr = zero_sc.shape[0]

    @pl.when(i == 0)
    def _():
        zero_sc[...] = jnp.zeros_like(zero_sc)

        def zero_row(r):
            return pltpu.make_async_copy(zero_sc.at[pl.ds(0, 1)], xs_hbm.at[pl.ds(r, 1)], zsem)

        def zero_rows(r):
            return pltpu.make_async_copy(zero_sc, xs_hbm.at[pl.ds(pl.multiple_of(r, zr), zr)], zsem)

        for e in range(N_EXPERTS):
            lo, hi = fill_ref[e], fill_ref[N_EXPERTS + e]
            lax.fori_loop(lo, hi, lambda r, c: (zero_row(r).start(), c)[1], 0)
            lax.fori_loop(lo, hi, lambda r, c: (zero_row(r).wait(), c)[1], 0)
        lo, hi = fill_ref[2 * N_EXPERTS] * (bm // zr), nb * (bm // zr)
        lax.fori_loop(lo, hi, lambda q, c: (zero_rows(q * zr).start(), c)[1], 0)
        lax.fori_loop(lo, hi, lambda q, c: (zero_rows(q * zr).wait(), c)[1], 0)

    def row_copy(r, dst):
        return pltpu.make_async_copy(h_ref.at[pl.ds(r, 1)], xs_hbm.at[pl.ds(dst, 1)], sem)

    def issue(r, carry):
        a = 2 * (i * tm + r)
        row_copy(r, dest_ref[a]).start()
        row_copy(r, dest_ref[a + 1]).start()
        return carry

    def drain(r, carry):
        row_copy(r, 0).wait()
        row_copy(r, 0).wait()
        return carry

    lax.fori_loop(0, tm, issue, 0, unroll=DMA_UNROLL)
    lax.fori_loop(0, tm, drain, 0, unroll=DMA_UNROLL)


def _dispatch(lay, h, dest, fill, nb, bm):
    tm = lay.tile(512)
    grid_spec = pltpu.PrefetchScalarGridSpec(
        num_scalar_prefetch=2,
        grid=(lay.T // tm,),
        in_specs=[pl.BlockSpec((tm, D), lambda i, d, f: (i, 0))],
        out_specs=pl.BlockSpec(memory_space=pl.ANY),
        scratch_shapes=[pltpu.VMEM((64, D), F32), pltpu.SemaphoreType.DMA(()), pltpu.SemaphoreType.DMA(())],
    )
    return pl.pallas_call(
        functools.partial(_dispatch_kernel, tm, bm, nb),
        out_shape=jax.ShapeDtypeStruct((nb * bm, D), F32),
        grid_spec=grid_spec,
        compiler_params=_cp("arbitrary"),
        name="dispatch",
    )(dest, fill, h)


def _experts_kernel(nf, be_ref, nv_ref, xs_ref, wa_ref, wb_ref, wo_ref, o_ref, xb_sc, acc_sc):
    i = pl.program_id(0)
    f = pl.program_id(1)
    valid = i < nv_ref[0]

    @pl.when(jnp.logical_and(valid, f == 0))
    def _():
        xb_sc[...] = xs_ref[...].astype(BF16)

    @pl.when(valid)
    def _():
        xb = xb_sc[...]
        a = jnp.dot(xb, wa_ref[...], preferred_element_type=F32)
        b = jnp.dot(xb, wb_ref[...], preferred_element_type=F32)
        h = (_silu(a) * b).astype(BF16)
        y = jnp.dot(h, wo_ref[...], preferred_element_type=F32)

        @pl.when(f == 0)
        def _():
            acc_sc[...] = y

        @pl.when(f > 0)
        def _():
            acc_sc[...] += y

    @pl.when(f == nf - 1)
    def _():
        o_ref[...] = jnp.where(valid, acc_sc[...], 0.0)


def _experts(xs, block_e, nvalid, nb, bm, w_in, w_out):
    tf = EXPERT_TF
    nf = EXPERT_DIM // tf

    def wmap(off):
        def imap(i, f, be, nv):
            fe = jnp.where(i < nv[0], f, nf - 1)
            return (be[i], 0, off + fe)
        return imap

    def womap(i, f, be, nv):
        fe = jnp.where(i < nv[0], f, nf - 1)
        return (be[i], fe, 0)

    grid_spec = pltpu.PrefetchScalarGridSpec(
        num_scalar_prefetch=2,
        grid=(nb, nf),
        in_specs=[pl.BlockSpec((bm, D), lambda i, f, be, nv: (jnp.minimum(i, nv[0] - 1), 0)),
                  pl.BlockSpec((None, D, tf), wmap(0)),
                  pl.BlockSpec((None, D, tf), wmap(nf)),
                  pl.BlockSpec((None, tf, D), womap)],
        out_specs=pl.BlockSpec((bm, D), lambda i, f, be, nv: (i, 0)),
        scratch_shapes=[pltpu.VMEM((bm, D), BF16), pltpu.VMEM((bm, D), F32)],
    )
    return pl.pallas_call(
        functools.partial(_experts_kernel, nf),
        out_shape=jax.ShapeDtypeStruct((nb * bm, D), F32),
        grid_spec=grid_spec,
        compiler_params=_cp("arbitrary", "arbitrary"),
        name="experts",
    )(block_e, nvalid, xs, w_in, w_in, w_out)


def _combine_kernel(tm, nt, final, dest_ref, x_ref, gate_ref, g_ref, fg_ref, ys_hbm, o_ref, y_sc, sem):
    i = pl.program_id(0)
    slot = i % 2

    def row_copy(s, k, r, src):
        return pltpu.make_async_copy(ys_hbm.at[pl.ds(src, 1)], y_sc.at[s, k, pl.ds(r, 1)], sem.at[s])

    def issue_tile(t, s):
        def issue(r, carry):
            a = 2 * (t * tm + r)
            row_copy(s, 0, r, dest_ref[a]).start()
            row_copy(s, 1, r, dest_ref[a + 1]).start()
            return carry
        lax.fori_loop(0, tm, issue, 0, unroll=DMA_UNROLL)

    @pl.when(i == 0)
    def _():
        issue_tile(0, 0)

    @pl.when(i + 1 < nt)
    def _():
        issue_tile(i + 1, 1 - slot)

    def drain(r, carry):
        row_copy(slot, 0, r, 0).wait()
        row_copy(slot, 1, r, 0).wait()
        return carry

    lax.fori_loop(0, tm, drain, 0, unroll=DMA_UNROLL)
    gate = gate_ref[...]
    out = x_ref[...] + g_ref[...] * (gate[:, 0:1] * y_sc[slot, 0] + gate[:, 1:2] * y_sc[slot, 1])
    if final:
        ms = jnp.mean(out * out, axis=-1, keepdims=True)
        out = out * lax.rsqrt(ms + RMS_EPS) * fg_ref[...]
    o_ref[...] = out


def _combine(lay, x, mods, gates, ys, dest, final_g):
    tm = lay.tile(256)
    nt = lay.T // tm
    final = final_g is not None
    fg = (final_g if final else jnp.ones((D,), F32)).reshape(1, D)
    grid_spec = pltpu.PrefetchScalarGridSpec(
        num_scalar_prefetch=1,
        grid=(nt,),
        in_specs=[pl.BlockSpec((tm, D), lambda i, d: (i, 0)),
                  pl.BlockSpec((tm, 2), lambda i, d: (i, 0)),
                  pl.BlockSpec((None, 1, D), lambda i, d: (lay.group(i * tm) * MOD_CHUNKS + 5, 0, 0)),
                  pl.BlockSpec((1, D), lambda i, d: (0, 0)),
                  pl.BlockSpec(memory_space=pl.ANY)],
        out_specs=pl.BlockSpec((tm, D), lambda i, d: (i, 0)),
        scratch_shapes=[pltpu.VMEM((2, 2, tm, D), F32), pltpu.SemaphoreType.DMA((2,))],
    )
    return pl.pallas_call(
        functools.partial(_combine_kernel, tm, nt, final),
        out_shape=jax.ShapeDtypeStruct((lay.T, D), F32),
        grid_spec=grid_spec,
        compiler_params=_cp("arbitrary"),
        name="combine",
    )(dest, x, gates, mods, fg, ys)


def _moe_layer(lay, x, mods, router_w, w_in, w_out, final_g=None, bm=MOE_BM):
    h, idx, gates, rank, counts = _router(lay, x, mods, router_w)
    dest, fill, block_e, nvalid, nb = _moe_plan(idx, rank, counts, bm)
    xs = _dispatch(lay, h, dest, fill, nb, bm)
    ys = _experts(xs, block_e, nvalid, nb, bm, w_in, w_out)
    return _combine(lay, x, mods, gates, ys, dest, final_g)


def kernel(x_prompt, x_sample, c, state_l0_s5_re, state_l0_s5_im, state_l2_gla, state_l3_ret, c_ctx, l0_mod_w, l0_mod_b, l0_s5_a_re, l0_s5_a_im, l0_s5_log_dt, l0_s5_b_re, l0_s5_b_im, l0_s5_c_re, l0_s5_c_im, l0_s5_d, l0_s5_glu_w, l0_ffn_w_in, l0_ffn_w_out, l1_mod_w, l1_mod_b, l1_hy_w_in, l1_hy_b_in, l1_hy_short_w, l1_hy_short_b, l1_hy_f_w1, l1_hy_f_b1, l1_hy_f_w2, l1_hy_f_b2, l1_hy_f_w3, l1_hy_f_freq, l1_hy_skip, l1_hy_w_out, l1_hy_b_out, l1_moe_router, l1_moe_w_in, l1_moe_w_out, l2_mod_w, l2_mod_b, l2_gla_w_in, l2_gla_gate_w1, l2_gla_gate_w2, l2_gla_gate_b, l2_gla_norm_g, l2_gla_w_out, l2_ffn_w_in, l2_ffn_w_out, l3_mod_w, l3_mod_b, l3_ret_w_in, l3_ret_log_decay, l3_ret_norm_g, l3_ret_w_out, l3_moe_router, l3_moe_w_in, l3_moe_w_out, final_norm_g):
    B, L, _ = x_prompt.shape
    NS, LS, _ = x_sample.shape
    lay = Layout(B, L, NS, LS)
    x = jnp.concatenate([x_prompt.reshape(B * L, D), x_sample.reshape(NS * LS, D)], axis=0)
    cond = jnp.concatenate([c_ctx[None], c, jnp.zeros((8 - 1 - NS, D), F32)], axis=0)
    mods0 = _mods(cond, l0_mod_w, l0_mod_b)
    p0 = dict(a_re=l0_s5_a_re, a_im=l0_s5_a_im, log_dt=l0_s5_log_dt, b_re=l0_s5_b_re, b_im=l0_s5_b_im,
              c_re=l0_s5_c_re, c_im=l0_s5_c_im, d=l0_s5_d, glu_w=l0_s5_glu_w.astype(BF16))
    x, s5_re, s5_im = _s5_layer(lay, x, mods0, p0, state_l0_s5_re, state_l0_s5_im)
    x = _ffn(lay, x, mods0, l0_ffn_w_in.astype(BF16), l0_ffn_w_out.astype(BF16))

    mods1 = _mods(cond, l1_mod_w, l1_mod_b)
    p1 = dict(w_in=l1_hy_w_in, b_in=l1_hy_b_in, short_w=l1_hy_short_w, short_b=l1_hy_short_b,
              f_w1=l1_hy_f_w1, f_b1=l1_hy_f_b1, f_w2=l1_hy_f_w2, f_b2=l1_hy_f_b2, f_w3=l1_hy_f_w3,
              f_freq=l1_hy_f_freq, skip=l1_hy_skip, w_out=l1_hy_w_out, b_out=l1_hy_b_out)
    x = _hyena_layer(lay, x, mods1, p1)
    x = _moe_layer(lay, x, mods1, l1_moe_router, l1_moe_w_in.astype(BF16), l1_moe_w_out.astype(BF16))

    mods2 = _mods(cond, l2_mod_w, l2_mod_b)
    p2 = dict(w_in=l2_gla_w_in, gate_w1=l2_gla_gate_w1, gate_w2=l2_gla_gate_w2, gate_b=l2_gla_gate_b,
              norm_g=l2_gla_norm_g, w_out=l2_gla_w_out)
    x, gla_state = _gla_layer(lay, x, mods2, p2, state_l2_gla)
    x = _ffn(lay, x, mods2, l2_ffn_w_in.astype(BF16), l2_ffn_w_out.astype(BF16))

    mods3 = _mods(cond, l3_mod_w, l3_mod_b)
    p3 = dict(w_in=l3_ret_w_in, log_decay=l3_ret_log_decay, norm_g=l3_ret_norm_g, w_out=l3_ret_w_out)
    x, ret_state = _ret_layer(lay, x, mods3, p3, state_l3_ret)
    y = _moe_layer(lay, x, mods3, l3_moe_router, l3_moe_w_in.astype(BF16), l3_moe_w_out.astype(BF16),
                   final_g=final_norm_g)
    return (y[:lay.TP].reshape(B, L, D), y[lay.TP:].reshape(NS, LS, D), s5_re, s5_im, gla_state, ret_state)
```

```python
import functools
import math

import jax
import jax.numpy as jnp
import numpy as np
from jax import lax
from jax.experimental import pallas as pl
from jax.experimental.pallas import tpu as pltpu

F32 = jnp.float32
BF16 = jnp.bfloat16
HIGHEST = lax.Precision.HIGHEST

D = 1024
RMS_EPS = 1e-6
MOD_CHUNKS = 6
GRID_W = 64

S5_Q = 16
S5_G = D // S5_Q
S5_P = 64
S5_T = 16
S5_SCAN_ROWS = 64

HY_BANDS = 16
HY_TARGET = 1e-2
HY_FAST_PCT = 0.3
HY_SLOW_PCT = 1.5

GLA_H, GLA_DK, GLA_DV = 4, 128, 256
GLA_RANK = 16
GLA_TAU = 16.0
RET_H, RET_DK, RET_DV = 4, 256, 512
CHUNK = 64
RET_CHUNK = 256
ROPE_BASE = 10000.0

FFN_DIM = 2816
N_EXPERTS = 8
EXPERT_DIM = 3584

VMEM_LIMIT_V7X = 56 * 1024 * 1024


def _cp(*sem):
    return pltpu.CompilerParams(dimension_semantics=sem, vmem_limit_bytes=VMEM_LIMIT_V7X)


def _silu(x):
    return x * jax.nn.sigmoid(x)


def _modulate(x, shift, scale):
    ms = jnp.mean(x * x, axis=-1, keepdims=True)
    return x * lax.rsqrt(ms + RMS_EPS) * (1.0 + scale) + shift


class Layout:
    def __init__(self, n_prompt, l_prompt, n_sample, l_sample):
        self.B, self.L, self.NS, self.LS = n_prompt, l_prompt, n_sample, l_sample
        self.TP = n_prompt * l_prompt
        self.T = self.TP + n_sample * l_sample

    def tile(self, want):
        t = math.gcd(math.gcd(self.TP, self.LS), want)
        assert t % 8 == 0
        return t

    def group(self, row):
        return jnp.where(row < self.TP, 0, 1 + (row - self.TP) // self.LS)


def _mod_spec(lay, tm, chunk, ngrid):
    def imap(*ids):
        return (lay.group(ids[0] * tm) * MOD_CHUNKS + chunk, 0, 0)
    del ngrid
    return pl.BlockSpec((None, 1, D), imap)


def _mods_kernel(c_ref, w_ref, b_ref, o_ref):
    o_ref[...] = jnp.dot(_silu(c_ref[...]), w_ref[...], precision=HIGHEST,
                         preferred_element_type=F32) + b_ref[...]


def _mods(cond, w, b):
    n = MOD_CHUNKS * D
    tn = 1536
    out = pl.pallas_call(
        _mods_kernel,
        out_shape=jax.ShapeDtypeStruct((8, n), F32),
        grid=(n // tn,),
        in_specs=[pl.BlockSpec((8, D), lambda j: (0, 0)),
                  pl.BlockSpec((D, tn), lambda j: (0, j)),
                  pl.BlockSpec((1, tn), lambda j: (0, j))],
        out_specs=pl.BlockSpec((8, tn), lambda j: (0, j)),
        compiler_params=_cp("arbitrary"),
        name="mods",
    )(cond, w, b.reshape(1, n))
    return out.reshape(8 * MOD_CHUNKS, 1, D)


def _modulate_kernel(x_ref, sh_ref, sc_ref, o_ref):
    o_ref[...] = _modulate(x_ref[...], sh_ref[...], sc_ref[...]).astype(o_ref.dtype)


def _modulate_call(lay, x, mods, c_shift, c_scale, dtype):
    tm = lay.tile(512)
    return pl.pallas_call(
        _modulate_kernel,
        out_shape=jax.ShapeDtypeStruct((lay.T, D), dtype),
        grid=(lay.T // tm,),
        in_specs=[pl.BlockSpec((tm, D), lambda i: (i, 0)),
                  _mod_spec(lay, tm, c_shift, 1), _mod_spec(lay, tm, c_scale, 1)],
        out_specs=pl.BlockSpec((tm, D), lambda i: (i, 0)),
        compiler_params=_cp("arbitrary"),
        name="modulate",
    )(x, mods, mods)


def _proj_kernel(x_ref, sh_ref, sc_ref, w_ref, b_ref, o_ref, u_sc):
    @pl.when(pl.program_id(1) == 0)
    def _():
        u_sc[...] = _modulate(x_ref[...], sh_ref[...], sc_ref[...]).astype(BF16)

    acc = jnp.dot(u_sc[...], w_ref[...], preferred_element_type=F32) + b_ref[...]
    o_ref[...] = acc.astype(o_ref.dtype)


def _proj(lay, x, mods, c_shift, c_scale, w, b, tn, out_dtype=BF16, tm=1024):
    tm = lay.tile(tm)
    n = w.shape[1]
    assert n % tn == 0
    resident = dict(pipeline_mode=pl.Buffered(1)) if tn == n else {}
    return pl.pallas_call(
        _proj_kernel,
        out_shape=jax.ShapeDtypeStruct((lay.T, n), out_dtype),
        grid=(lay.T // tm, n // tn),
        in_specs=[pl.BlockSpec((tm, D), lambda i, j: (i, 0)),
                  _mod_spec(lay, tm, c_shift, 2), _mod_spec(lay, tm, c_scale, 2),
                  pl.BlockSpec((D, tn), lambda i, j: (0, j), **resident),
                  pl.BlockSpec((1, tn), lambda i, j: (0, j))],
        out_specs=pl.BlockSpec((tm, tn), lambda i, j: (i, j)),
        scratch_shapes=[pltpu.VMEM((tm, D), BF16)],
        compiler_params=_cp("arbitrary", "arbitrary"),
        name="proj",
    )(x, mods, mods, w, b)


def _ffn_kernel(x_ref, sh_ref, sc_ref, g_ref, wa_ref, wb_ref, wo_ref, o_ref):
    x = x_ref[...]
    u = _modulate(x, sh_ref[...], sc_ref[...]).astype(BF16)
    a = jnp.dot(u, wa_ref[...], preferred_element_type=F32)
    b = jnp.dot(u, wb_ref[...], preferred_element_type=F32)
    h = (_silu(a) * b).astype(BF16)
    o_ref[...] = x + g_ref[...] * jnp.dot(h, wo_ref[...], preferred_element_type=F32)


def _ffn(lay, x, mods, w_in, w_out):
    tm = lay.tile(512)
    once = dict(pipeline_mode=pl.Buffered(1))
    return pl.pallas_call(
        _ffn_kernel,
        out_shape=jax.ShapeDtypeStruct((lay.T, D), F32),
        grid=(lay.T // tm,),
        in_specs=[pl.BlockSpec((tm, D), lambda i: (i, 0)),
                  _mod_spec(lay, tm, 3, 1), _mod_spec(lay, tm, 4, 1), _mod_spec(lay, tm, 5, 1),
                  pl.BlockSpec((D, FFN_DIM), lambda i: (0, 0), **once),
                  pl.BlockSpec((D, FFN_DIM), lambda i: (0, 1), **once),
                  pl.BlockSpec((FFN_DIM, D), lambda i: (0, 0), **once)],
        out_specs=pl.BlockSpec((tm, D), lambda i: (i, 0)),
        compiler_params=_cp("arbitrary"),
        name="ffn",
    )(x, mods, mods, mods, w_in, w_in, w_out)


def _s5_tables(a_re, a_im, log_dt, b_re, b_im, c_re, c_im, d_skip):
    T, G, P, Q = S5_T, S5_G, S5_P, S5_Q
    a = lax.complex(a_re, a_im)
    adt = a * jnp.exp(log_dt)[..., None]
    lam = jnp.exp(adt)
    bb = ((lam - 1.0) / a)[..., None] * lax.complex(b_re, b_im)
    cm = lax.complex(c_re, c_im)
    steps = jnp.arange(T + 1, dtype=F32)
    pw = jnp.exp(steps[None, :, None, None] * adt[:, None])
    kern = jnp.real(jnp.einsum('dgqp,djgp,dgpr->djgqr', cm, pw[:, :T], bb))
    lag = jnp.arange(T)[:, None, None]
    s_i = jnp.arange(T)[None, :, None]
    t_i = jnp.arange(T)[None, None, :]
    place = jnp.stack([t_i - s_i == lag, s_i - t_i == lag]).astype(F32)
    m = jnp.einsum('djst,djgqr->gsrtq', place, kern, precision=HIGHEST)
    eye = (jnp.eye(T)[:, None, :, None] * jnp.eye(Q)[None, :, None, :])
    m = m + eye[None] * d_skip.reshape(G, 1, 1, 1, Q)
    m = m.reshape(G, T * Q, T * Q)
    e_f = pw[0][T - 1 - jnp.arange(T)]
    e_b = pw[1][jnp.arange(T)]
    n_f = e_f[..., None] * bb[0][None]
    n_b = e_b[..., None] * bb[1][None]
    n_c = jnp.concatenate([n_f, n_b], axis=2)
    n_c = jnp.transpose(n_c, (1, 0, 3, 2)).reshape(G, T * Q, 2 * P)
    lam_t = jnp.concatenate([pw[0][T], pw[1][T]], axis=-1)
    w_f = cm[0][:, None] * jnp.transpose(pw[0][1:T + 1], (1, 0, 2))[:, :, None, :]
    w_b = cm[1][:, None] * jnp.transpose(pw[1][T - jnp.arange(T)], (1, 0, 2))[:, :, None, :]
    w_f = jnp.transpose(w_f, (0, 3, 1, 2)).reshape(G, P, T * Q)
    w_b = jnp.transpose(w_b, (0, 3, 1, 2)).reshape(G, P, T * Q)
    z = jnp.zeros_like(jnp.real(w_f))
    c_mats = dict(c_f_re=jnp.concatenate([jnp.real(w_f), z], axis=1),
                  c_f_im=jnp.concatenate([-jnp.imag(w_f), z], axis=1),
                  c_b_re=jnp.concatenate([z, jnp.real(w_b)], axis=1),
                  c_b_im=jnp.concatenate([z, -jnp.imag(w_b)], axis=1))
    return dict(m=m.astype(BF16), n_re=jnp.real(n_c).astype(BF16), n_im=jnp.imag(n_c).astype(BF16),
                l_re=jnp.real(lam_t), l_im=jnp.imag(lam_t), **{k: v.astype(BF16) for k, v in c_mats.items()})


def _s5_in_kernel(u_ref, m_ref, nre_ref, nim_ref, yi_ref, sre_ref, sim_ref):
    u = u_ref[...]
    yi_ref[...] = jnp.dot(u, m_ref[...], preferred_element_type=F32)
    sre_ref[...] = jnp.dot(u, nre_ref[...], preferred_element_type=F32).reshape(sre_ref.shape)
    sim_ref[...] = jnp.dot(u, nim_ref[...], preferred_element_type=F32).reshape(sim_ref.shape)


def _s5_in(ug, tabs):
    G, R, W = ug.shape
    P2 = 2 * S5_P
    RB = S5_SCAN_ROWS
    assert R % RB == 0
    gspec = lambda n: pl.BlockSpec((None, W, n), lambda g: (g, 0, 0))
    rspec = pl.BlockSpec((None, R, W), lambda g: (g, 0, 0))
    sspec = pl.BlockSpec((R // RB, RB, P2), lambda g: (0, g, 0))
    sshape = jax.ShapeDtypeStruct((R // RB, G * RB, P2), F32)
    return pl.pallas_call(
        _s5_in_kernel,
        out_shape=(jax.ShapeDtypeStruct((G, R, W), F32), sshape, sshape),
        grid=(G,),
        in_specs=[rspec, gspec(W), gspec(P2), gspec(P2)],
        out_specs=(rspec, sspec, sspec),
        compiler_params=_cp("arbitrary"),
        name="s5_in",
    )(ug, tabs['m'], tabs['n_re'], tabs['n_im'])


def _s5_scan_kernel(nsb, ncb, nblk, sref_ref, simf_ref, sreb_ref, simb_ref, lre_ref, lim_ref,
                    h0re_ref, h0im_ref, *rest):
    hfre_ref, hfim_ref, hbre_ref, hbim_ref, fre_ref, fim_ref, cre_sc, cim_sc = rest[4:]
    P = S5_P
    rows = sref_ref.shape[0] // ncb
    j = pl.program_id(1)
    fwd = lax.broadcasted_iota(jnp.int32, (1, 2 * P), 1) < P
    lre = lre_ref[...]
    lim = lim_ref[...]

    @pl.when(j == 0)
    def _():
        cre_sc[...] = h0re_ref[...]
        cim_sc[...] = h0im_ref[...]

    def at(k):
        return pl.ds(k, rows, stride=ncb)

    def body(k, carry):
        hre, him = carry
        kb = ncb - 1 - k
        hfre_ref[at(k), :] = hre
        hfim_ref[at(k), :] = him
        hbre_ref[at(kb), :] = hre
        hbim_ref[at(kb), :] = him
        sre = jnp.where(fwd, sref_ref[at(k), :], sreb_ref[at(kb), :])
        sim = jnp.where(fwd, simf_ref[at(k), :], simb_ref[at(kb), :])
        return (lre * hre - lim * him + sre, lre * him + lim * hre + sim)

    hre, him = lax.fori_loop(0, ncb, body, (cre_sc[...], cim_sc[...]), unroll=4)
    cre_sc[...] = hre
    cim_sc[...] = him

    @pl.when(j == nblk - 1)
    def _():
        fre_ref[...] = hre
        fim_ref[...] = him


def _s5_scan(sre, sim, tabs, h0re, h0im, hprev, row0, nseq, nc, nsb, nblk):
    _, grb, P2 = sre.shape
    rb = S5_SCAN_ROWS
    G = grb // rb
    assert nblk == 1 or nsb == 1
    ncb = nc // nblk
    assert nsb * ncb == rb and row0 % rb == 0 and nseq % nsb == 0 and nc % nblk == 0
    b0 = row0 // rb
    fspec = pl.BlockSpec((None, grb, P2), lambda i, j: (b0 + i * nblk + j, 0, 0))
    bspec = pl.BlockSpec((None, grb, P2), lambda i, j: (b0 + i * nblk + nblk - 1 - j, 0, 0))
    lspec = pl.BlockSpec((G * nsb, P2), lambda i, j: (0, 0))
    qspec = pl.BlockSpec((None, G * nsb, P2), lambda i, j: (i, 0, 0))
    anyspec = pl.BlockSpec(memory_space=pl.ANY)
    fin = jax.ShapeDtypeStruct((nseq // nsb, G * nsb, P2), F32)
    rep = lambda a: jnp.repeat(a, nsb, axis=0)
    flat = lambda a: a.reshape(nseq // nsb, G * nsb, P2)
    outs = pl.pallas_call(
        functools.partial(_s5_scan_kernel, nsb, ncb, nblk),
        out_shape=tuple(jax.ShapeDtypeStruct(h.shape, h.dtype) for h in hprev) + (fin, fin),
        grid=(nseq // nsb, nblk),
        in_specs=[fspec, fspec, bspec, bspec, lspec, lspec, qspec, qspec] + [anyspec] * 4,
        out_specs=(fspec, fspec, bspec, bspec, qspec, qspec),
        scratch_shapes=[pltpu.VMEM((G * nsb, P2), F32), pltpu.VMEM((G * nsb, P2), F32)],
        input_output_aliases={8: 0, 9: 1, 10: 2, 11: 3},
        compiler_params=_cp("arbitrary", "arbitrary"),
        name="s5_scan",
    )(sre, sim, sre, sim, rep(tabs['l_re']), rep(tabs['l_im']), flat(h0re), flat(h0im), *hprev)
    return outs[:4], outs[4].reshape(nseq // nsb, G, nsb, P2), outs[5].reshape(nseq // nsb, G, nsb, P2)


def _s5_out_kernel(yi_ref, hfre_ref, hfim_ref, hbre_ref, hbim_ref, cfre_ref, cfim_ref, cbre_ref, cbim_ref,
                   y_ref):
    y = yi_ref[...]
    for h_ref, c_ref in ((hfre_ref, cfre_ref), (hfim_ref, cfim_ref), (hbre_ref, cbre_ref), (hbim_ref, cbim_ref)):
        h = h_ref[...].reshape(y.shape[0], h_ref.shape[-1])
        y += jnp.dot(h.astype(BF16), c_ref[...], preferred_element_type=F32)
    y_ref[...] = y.astype(y_ref.dtype)


def _s5_out(yi, hprev, tabs):
    G, R, W = yi.shape
    P2 = 2 * S5_P
    RB = S5_SCAN_ROWS
    gspec = pl.BlockSpec((None, P2, W), lambda g: (g, 0, 0))
    hspec = pl.BlockSpec((R // RB, RB, P2), lambda g: (0, g, 0))
    rspec = pl.BlockSpec((None, R, W), lambda g: (g, 0, 0))
    return pl.pallas_call(
        _s5_out_kernel,
        out_shape=jax.ShapeDtypeStruct((G, R, W), F32),
        grid=(G,),
        in_specs=[rspec] + [hspec] * 4 + [gspec] * 4,
        out_specs=rspec,
        compiler_params=_cp("arbitrary"),
        name="s5_out",
    )(yi, *hprev, tabs['c_f_re'], tabs['c_f_im'], tabs['c_b_re'], tabs['c_b_im'])


LANES = 128
S5_GB = LANES // S5_Q


def _block_transpose(sets):
    blk = lax.broadcasted_iota(jnp.int32, sets[0][0].shape, 1) // S5_Q
    sets = [list(regs) for regs in sets]
    d = S5_GB // 2
    while d:
        keep = (blk & d) == 0
        for regs in sets:
            for i in range(S5_GB):
                if i & d:
                    continue
                a, b = regs[i], regs[i + d]
                regs[i] = jnp.where(keep, a, pltpu.roll(b, d * S5_Q, 1))
                regs[i + d] = jnp.where(keep, pltpu.roll(a, LANES - d * S5_Q, 1), b)
        d //= 2
    return sets


def _s5_pre_kernel(tm, x_ref, sh_ref, sc_ref, ug_ref, u_sc):
    u = _modulate(x_ref[...], sh_ref[...], sc_ref[...])
    for j in range(D // LANES):
        u_sc[j] = u[:, j * LANES:(j + 1) * LANES]
    rows16 = 16
    nh = S5_T // S5_GB
    for c in range(tm // (S5_T * rows16)):
        base = c * S5_T * rows16
        for j in range(D // LANES):
            sets = [[u_sc[j, pl.ds(base + h * S5_GB + s, rows16, stride=S5_T), :] for s in range(S5_GB)]
                    for h in range(nh)]
            for h, regs in enumerate(_block_transpose(sets)):
                for gl, t in enumerate(regs):
                    ug_ref[j * S5_GB + gl, c * rows16:(c + 1) * rows16, h * LANES:(h + 1) * LANES] = t.astype(BF16)


def _s5_pre(lay, x, mods):
    tm = lay.tile(512)
    assert tm % (S5_T * 16) == 0
    return pl.pallas_call(
        functools.partial(_s5_pre_kernel, tm),
        out_shape=jax.ShapeDtypeStruct((S5_G, lay.T // S5_T, S5_T * S5_Q), BF16),
        grid=(lay.T // tm,),
        in_specs=[pl.BlockSpec((tm, D), lambda i: (i, 0)), _mod_spec(lay, tm, 0, 1), _mod_spec(lay, tm, 1, 1)],
        out_specs=pl.BlockSpec((S5_G, tm // S5_T, S5_T * S5_Q), lambda i: (0, i, 0)),
        scratch_shapes=[pltpu.VMEM((D // LANES, tm, LANES), F32)],
        compiler_params=_cp("arbitrary"),
        name="s5_pre",
    )(x, mods, mods)


def _s5_glu_kernel(tm, x_ref, yg_ref, g_ref, wv_ref, wg_ref, o_ref, a_sc, y_sc):
    @pl.when(pl.program_id(1) == 0)
    def _():
        def sub_tile(c, carry):
            crow = pl.ds(pl.multiple_of(c * 8, 8), 8)
            base = c * (S5_T * 8)
            nh = S5_T // S5_GB
            for j2 in range(0, D // LANES, 2):
                keys = [(j, h) for j in (j2, j2 + 1) for h in range(nh)]
                sets = [[yg_ref[j * S5_GB + gl, crow, h * LANES:(h + 1) * LANES] for gl in range(S5_GB)]
                        for j, h in keys]
                for (j, h), regs in zip(keys, _block_transpose(sets)):
                    for s, t in enumerate(regs):
                        y_sc[j, pl.ds(base + h * S5_GB + s, 8, stride=S5_T), :] = t
            return carry

        lax.fori_loop(0, tm // (S5_T * 8), sub_tile, 0)
        for j in range(D // LANES):
            a_sc[:, j * LANES:(j + 1) * LANES] = jax.nn.gelu(y_sc[j]).astype(BF16)

    a = a_sc[...]
    val = jnp.dot(a, wv_ref[...], preferred_element_type=F32)
    gate = jnp.dot(a, wg_ref[...], preferred_element_type=F32)
    o_ref[...] = x_ref[...] + g_ref[...] * (val * jax.nn.sigmoid(gate))


def _s5_glu(lay, x, yg, mods, glu_w):
    tm = lay.tile(512)
    assert tm % (S5_T * 8) == 0
    tn = D
    nn = D // tn
    once = dict(pipeline_mode=pl.Buffered(1))
    return pl.pallas_call(
        functools.partial(_s5_glu_kernel, tm),
        out_shape=jax.ShapeDtypeStruct((lay.T, D), F32),
        grid=(lay.T // tm, nn),
        in_specs=[pl.BlockSpec((tm, tn), lambda i, j: (i, j)),
                  pl.BlockSpec((S5_G, tm // S5_T, S5_T * S5_Q), lambda i, j: (0, i, 0)),
                  pl.BlockSpec((None, 1, tn), lambda i, j: (lay.group(i * tm) * MOD_CHUNKS + 2, 0, j)),
                  pl.BlockSpec((D, tn), lambda i, j: (0, j), **once),
                  pl.BlockSpec((D, tn), lambda i, j: (0, nn + j), **once)],
        out_specs=pl.BlockSpec((tm, tn), lambda i, j: (i, j)),
        scratch_shapes=[pltpu.VMEM((tm, D), BF16), pltpu.VMEM((D // LANES, tm, LANES), F32)],
        compiler_params=_cp("arbitrary", "arbitrary"),
        name="s5_glu",
    )(x, yg, mods, glu_w, glu_w)


def _s5_layer(lay, x, mods, p, h0_re, h0_im):
    T, G, P, Q = S5_T, S5_G, S5_P, S5_Q
    tabs = _s5_tables(p['a_re'], p['a_im'], p['log_dt'], p['b_re'], p['b_im'], p['c_re'], p['c_im'], p['d'])
    R = lay.T // T
    ug = _s5_pre(lay, x, mods)
    yi, sre, sim = _s5_in(ug, tabs)
    hprev = tuple(jnp.zeros(sre.shape, F32) for _ in range(4))
    ncp, ncs = lay.L // T, lay.LS // T
    nsb = S5_SCAN_ROWS // ncp
    zero = jnp.zeros((lay.B // nsb, G, nsb, 2 * P), F32)
    hprev, fre, fim = _s5_scan(sre, sim, tabs, zero, zero, hprev, 0, lay.B, ncp, nsb, 1)
    to_lanes = lambda s: jnp.transpose(s, (0, 2, 1, 3)).reshape(lay.NS, G, 1, 2 * P)
    hprev, _, _ = _s5_scan(sre, sim, tabs, to_lanes(h0_re), to_lanes(h0_im), hprev,
                           lay.TP // T, lay.NS, ncs, 1, max(1, ncs // S5_SCAN_ROWS))
    yg = _s5_out(yi, hprev, tabs)
    x = _s5_glu(lay, x, yg, mods, p['glu_w'])
    from_lanes = lambda s: jnp.transpose(s, (0, 2, 1, 3)).reshape(lay.B, G, 2, P).transpose(0, 2, 1, 3)
    return x, from_lanes(fre), from_lanes(fim)


def _hyena_filters(L, p):
    mm = functools.partial(jnp.matmul, precision=HIGHEST)
    f = jnp.linspace(1e-4, HY_BANDS - 1, HY_BANDS, dtype=F32)[None, :]
    max_decay = math.log(HY_TARGET) / HY_FAST_PCT
    min_decay = math.log(HY_TARGET) / HY_SLOW_PCT
    deltas = jnp.abs(jnp.linspace(min_decay, max_decay, D, dtype=F32))
    w3 = p['f_w3'].reshape(-1, 2, 2, D)

    def side(pos, s):
        t = (pos.astype(F32) / (L - 1))[:, None]
        w = 2.0 * math.pi * pos.astype(F32)[:, None] / L
        feats = jnp.concatenate([t, jnp.cos(f * w), -jnp.sin(f * w)], axis=-1)
        z = jnp.sin(p['f_freq'][0] * (mm(feats, p['f_w1']) + p['f_b1']))
        z = jnp.sin(p['f_freq'][1] * (mm(z, p['f_w2']) + p['f_b2']))
        win = jnp.exp(-t * deltas)
        return jnp.stack([mm(z, w3[:, o, s]) * win for o in range(2)])

    j = jnp.arange(L, dtype=jnp.int32)
    k_lo = side(j, 0)
    k_hi = side((L - j) % L, 1) * (j > 0).astype(F32)[None, :, None]
    norm = jnp.sum(jnp.abs(k_lo), axis=1, keepdims=True) + jnp.sum(jnp.abs(k_hi), axis=1, keepdims=True)
    k_lo, k_hi = k_lo / norm, k_hi / norm
    alt = (1.0 - 2.0 * (j % 2).astype(F32))[None, :, None]
    k_ny = jnp.sum(alt * (k_lo + k_hi), axis=1) / (2 * L)
    return k_lo, k_hi, k_ny


def _dft_tables(L):
    r = math.isqrt(L)
    assert r * r == L
    t = jnp.arange(L, dtype=jnp.int32)[None, :]
    a = jnp.arange(r, dtype=jnp.int32)[:, None]

    def unit(idx):
        ang = (idx % (2 * L)).astype(F32) * (math.pi / L)
        return jnp.cos(ang), jnp.sin(ang)

    c1, s1 = unit(a * r * t)
    c2, s2 = unit(a * t)
    c1, s1, c2, s2 = c1[:, None], s1[:, None], c2[None], s2[None]
    cos = (c1 * c2 - s1 * s2).reshape(L, L)
    sin = (s1 * c2 + c1 * s2).reshape(L, L)
    return cos.astype(BF16), sin.astype(BF16)


def _hy_spec_kernel(L, tr, c_ref, s_ref, klo_ref, khi_ref, p_ref, q_ref):
    r = pl.program_id(2)
    f = r * tr + lax.broadcasted_iota(jnp.int32, (tr, 1), 0)
    sgn = (1 - 2 * (f % 2)).astype(F32)
    scale = jnp.where(f == 0, 1.0, 2.0) * (1.0 / (2 * L))
    c, s = c_ref[...], s_ref[...]
    lo, hi = klo_ref[...], khi_ref[...]
    dot = functools.partial(jnp.dot, preferred_element_type=F32)
    p_ref[...] = scale * (dot(c, lo) + sgn * dot(c, hi))
    q_ref[...] = scale * (dot(s, lo) + sgn * dot(s, hi))


def _hy_spectrum(L, cos, sin, k_lo, k_hi):
    tr = min(L, 512)
    tc = 512
    kspec = pl.BlockSpec((None, L, tc), lambda o, j, r: (o, 0, j))
    tspec = pl.BlockSpec((tr, L), lambda o, j, r: (r, 0))
    ospec = pl.BlockSpec((None, tr, tc), lambda o, j, r: (o, r, j))
    return pl.pallas_call(
        functools.partial(_hy_spec_kernel, L, tr),
        out_shape=(jax.ShapeDtypeStruct((2, L, D), F32), jax.ShapeDtypeStruct((2, L, D), F32)),
        grid=(2, D // tc, L // tr),
        in_specs=[tspec, tspec, kspec, kspec],
        out_specs=(ospec, ospec),
        compiler_params=_cp("arbitrary", "arbitrary", "arbitrary"),
        name="hy_spectrum",
    )(cos, sin, k_lo.astype(BF16), k_hi.astype(BF16))


def _hy_core_kernel(L, tr, ngrp, tc, x1_ref, x2_ref, v_ref, sw1_ref, sw2_ref, swv_ref, sb1_ref, sb2_ref,
                    sbv_ref, c_ref, s_ref, p_ref, q_ref, kny_ref, skip_ref, *rest):
    o_ref, z0_sc, z1_sc, x2_sc, a_sc, b_sc, ny0_sc, ny1_sc = rest[-8:]
    fused = tr == L
    ph = pl.program_id(2)
    r = pl.program_id(3)
    W = ngrp * tc

    def phase(k):
        return (lambda f: f()) if fused else pl.when(ph == k)
    tcv = min(L, 512)
    halo = 16

    def alt_sign(start, n):
        t = start + lax.broadcasted_iota(jnp.int32, (n, 1), 0)
        return (1 - 2 * (t % 2)).astype(F32)

    def conv3(src_ref, g, a, w_ref, b_ref):
        x = src_ref[g, pl.ds(a, tcv), :].astype(F32)
        row = lax.broadcasted_iota(jnp.int32, (tcv, 1), 0)
        up_at = pl.multiple_of(jnp.maximum(a - halo, 0), halo)
        dn_at = pl.multiple_of(jnp.minimum(a + tcv, L - halo), halo)
        up = src_ref[g, pl.ds(up_at, halo), :][halo - 1:halo, :].astype(F32)
        dn = src_ref[g, pl.ds(dn_at, halo), :][0:1, :].astype(F32)
        up = jnp.where(a > 0, up, 0.0)
        dn = jnp.where(a + tcv < L, dn, 0.0)
        prev = jnp.where(row == 0, up, pltpu.roll(x, 1, 0))
        nxt = jnp.where(row == tcv - 1, dn, pltpu.roll(x, tcv - 1, 0))
        return prev * w_ref[0:1, :] + x * w_ref[1:2, :] + nxt * w_ref[2:3, :] + b_ref[...]

    @(phase(0) if fused else pl.when(jnp.logical_and(ph == 0, r == 0)))
    def _():
        ny0_sc[...] = jnp.zeros_like(ny0_sc)

        def conv_tile(ti, carry):
            a = pl.multiple_of(ti * tcv, tcv)
            rows_a = pl.ds(a, tcv)
            for g in range(ngrp):
                cols = slice(g * tc, (g + 1) * tc)
                z1_sc[rows_a, cols] = conv3(x1_ref, g, a, sw1_ref, sb1_ref).astype(BF16)
                x2_sc[rows_a, cols] = conv3(x2_ref, g, a, sw2_ref, sb2_ref).astype(BF16)
                z0_sc[rows_a, cols] = conv3(v_ref, g, a, swv_ref, sbv_ref).astype(BF16)
            ny0_sc[...] += jnp.sum(alt_sign(a, tcv) * z0_sc[rows_a, :].astype(F32), axis=0, keepdims=True)
            return carry

        lax.fori_loop(0, L // tcv, conv_tile, 0)

    start = pl.multiple_of(r * tr, tr)
    rows = pl.ds(start, tr)
    dot = functools.partial(jnp.dot, preferred_element_type=F32)

    def forward(order, z_sc):
        z = z_sc[...]
        zre = dot(c_ref[...], z)
        zim = dot(s_ref[...], z)
        pw = jnp.concatenate([p_ref[order] if fused else p_ref[...]] * ngrp, axis=1)
        qw = jnp.concatenate([q_ref[order] if fused else q_ref[...]] * ngrp, axis=1)
        a_sc[rows, :] = (zre * pw - zim * qw).astype(BF16)
        b_sc[rows, :] = (zim * pw + zre * qw).astype(BF16)

    def inverse(order, z_sc, ny_sc):
        y = dot(c_ref[...], a_sc[...]) + dot(s_ref[...], b_sc[...])
        kny = jnp.concatenate([kny_ref[order:order + 1, :]] * ngrp, axis=1)
        skip = jnp.concatenate([skip_ref[order:order + 1, :]] * ngrp, axis=1)
        return y + alt_sign(start, tr) * (ny_sc[...] * kny) + skip * z_sc[rows, :].astype(F32)

    @phase(0)
    def _():
        forward(0, z0_sc)

    @phase(1)
    def _():
        z1 = (z1_sc[rows, :].astype(F32) * inverse(0, z0_sc, ny0_sc)).astype(BF16)
        z1_sc[rows, :] = z1

        @pl.when(r == 0)
        def _():
            ny1_sc[...] = jnp.zeros_like(ny1_sc)

        ny1_sc[...] += jnp.sum(alt_sign(start, tr) * z1.astype(F32), axis=0, keepdims=True)

    @phase(2)
    def _():
        forward(1, z1_sc)

    @phase(3)
    def _():
        out = x2_sc[rows, :].astype(F32) * inverse(1, z1_sc, ny1_sc)
        for g in range(ngrp):
            o_ref[g, rows, :] = out[:, g * tc:(g + 1) * tc].astype(o_ref.dtype)


def _hy_core(proj, short_w, short_b, skip, cos, sin, pq, k_ny, o_prev, row0, nseq, L, ngrp, tc):
    T = proj.shape[0]
    tr = min(L, 256)
    nrt = L // tr
    assert row0 % (L * ngrp) == 0 and nseq % ngrp == 0 and T % L == 0
    sb0 = row0 // (L * ngrp)
    nct = D // tc
    p3 = proj.reshape(T // L, L, 3 * D)
    p_arr, q_arr = pq

    def xspec(part):
        mode = {} if nrt == 1 else dict(pipeline_mode=pl.Buffered(1))
        return pl.BlockSpec((ngrp, L, tc), lambda i, j, ph, r: (sb0 + i, 0, part * nct + j), **mode)

    def wspec(part, rows_):
        return pl.BlockSpec((rows_, tc), lambda i, j, ph, r: (0, part * nct + j))

    fused = nrt == 1
    if fused:
        pq_spec = pl.BlockSpec((2, tr, tc), lambda i, j, ph, r: (0, 0, j))
    else:
        pq_spec = pl.BlockSpec((None, tr, tc),
                               lambda i, j, ph, r: (ph // 2, jnp.where(ph % 2 == 0, r, nrt - 1), j))

    tspec = pl.BlockSpec((tr, L), lambda i, j, ph, r: (r, 0))
    in_specs = [xspec(0), xspec(1), xspec(2), wspec(0, 3), wspec(1, 3), wspec(2, 3),
                wspec(0, 1), wspec(1, 1), wspec(2, 1), tspec, tspec, pq_spec, pq_spec,
                pl.BlockSpec((2, tc), lambda i, j, ph, r: (0, j)),
                pl.BlockSpec((2, tc), lambda i, j, ph, r: (0, j))]
    sb = short_b.reshape(1, 3 * D)
    args = [p3, p3, p3, short_w, short_w, short_w, sb, sb, sb, cos, sin, p_arr, q_arr, k_ny, skip]
    aliases = {}
    if o_prev is not None:
        in_specs.append(pl.BlockSpec(memory_space=pl.ANY))
        args.append(o_prev.reshape(T // L, L, D))
        aliases = {len(args) - 1: 0}
    W = ngrp * tc
    out = pl.pallas_call(
        functools.partial(_hy_core_kernel, L, tr, ngrp, tc),
        out_shape=jax.ShapeDtypeStruct((T // L, L, D), BF16),
        grid=(nseq // ngrp, nct, 1 if fused else 4, nrt),
        in_specs=in_specs,
        out_specs=pl.BlockSpec((ngrp, L, tc), lambda i, j, ph, r: (sb0 + i, 0, j)),
        scratch_shapes=[pltpu.VMEM((L, W), BF16)] * 5 + [pltpu.VMEM((1, W), F32)] * 2,
        input_output_aliases=aliases,
        compiler_params=_cp("arbitrary", "arbitrary", "arbitrary", "arbitrary"),
        name="hy_core",
    )(*args)
    return out.reshape(T, D)


FOLD_BLK = 256


def _hyena_filters_folded(L, p):
    H = L // 2
    mm = functools.partial(jnp.matmul, precision=HIGHEST)
    f = jnp.linspace(1e-4, HY_BANDS - 1, HY_BANDS, dtype=F32)[None, :]
    max_decay = math.log(HY_TARGET) / HY_FAST_PCT
    min_decay = math.log(HY_TARGET) / HY_SLOW_PCT
    deltas = jnp.abs(jnp.linspace(min_decay, max_decay, D, dtype=F32))
    w3 = p['f_w3'].reshape(-1, 2, 2, D)

    def side(pos, s):
        t = (pos.astype(F32) / (L - 1))[:, None]
        w = 2.0 * math.pi * pos.astype(F32)[:, None] / L
        feats = jnp.concatenate([t, jnp.cos(f * w), -jnp.sin(f * w)], axis=-1)
        z = jnp.sin(p['f_freq'][0] * (mm(feats, p['f_w1']) + p['f_b1']))
        z = jnp.sin(p['f_freq'][1] * (mm(z, p['f_w2']) + p['f_b2']))
        win = jnp.exp(-t * deltas)
        return jnp.stack([mm(z, w3[:, o, s]) * win for o in range(2)])

    t = jnp.arange(H, dtype=jnp.int32)
    pos = jnp.concatenate([t, (L - t) % L])
    mid = jnp.full((1,), H, jnp.int32)
    live = (t > 0).astype(F32)[None, :, None]
    s0 = side(pos, 0).reshape(2, 2, H, D)
    s1 = side(pos, 1).reshape(2, 2, H, D)
    klo_lo, klo_hr, klo_h = s0[:, 0], s0[:, 1] * live, side(mid, 0)[:, 0]
    khi_lo, khi_hr, khi_h = s1[:, 1] * live, s1[:, 0] * live, side(mid, 1)[:, 0]
    norm = sum(jnp.sum(jnp.abs(a), axis=1) for a in (klo_lo, klo_hr, khi_lo, khi_hr)) \
        + jnp.abs(klo_h) + jnp.abs(khi_h)
    alt = (1.0 - 2.0 * (t % 2).astype(F32))[None, :, None]
    alt_h = 1.0 - 2.0 * (H % 2)
    k_ny = (jnp.sum(alt * (klo_lo + klo_hr + khi_lo + khi_hr), axis=1) + alt_h * (klo_h + khi_h)) / norm / (2 * L)
    inv = (1.0 / norm)[:, None, :]
    p_lo, p_hr = (klo_lo + khi_lo) * inv, (klo_hr + khi_hr) * inv
    m_lo, m_hr = (klo_lo - khi_lo) * inv, (klo_hr - khi_hr) * inv
    xc = jnp.stack([p_lo + p_hr, m_lo - m_hr])
    xs = jnp.stack([p_lo - p_hr, m_lo + m_hr])
    xh = jnp.stack([(klo_h + khi_h) / norm, (klo_h - khi_h) / norm])
    return xc, xs, xh, k_ny


def _dft_tables_folded(L):
    H = L // 2
    ra = 1 << (int(math.log2(H)) // 2)
    rb = H // ra
    th = math.pi / L
    a = jnp.arange(ra, dtype=jnp.int32)[:, None]
    b = jnp.arange(rb, dtype=jnp.int32)[:, None]
    u = jnp.arange(H, dtype=jnp.int32)[None, :]

    def unit(idx):
        ang = (idx % (2 * L)).astype(F32) * th
        return jnp.cos(ang), jnp.sin(ang)

    e1 = [unit(2 * rb * a * u), unit(2 * rb * a * u), unit((2 * u + 1) * rb * a)]
    e2 = [unit(2 * b * u), unit((2 * b + 1) * u), unit((2 * u + 1) * b)]
    c1 = jnp.stack([e[0] for e in e1])[:, None, :, None, :]
    s1 = jnp.stack([e[1] for e in e1])[:, None, :, None, :]
    c2 = jnp.stack([e[0] for e in e2])[:, None, None, :, :]
    s2 = jnp.stack([e[1] for e in e2])[:, None, None, :, :]
    lead = jnp.concatenate([c1, s1], axis=1)
    cross = jnp.concatenate([-s1, c1], axis=1)
    return (lead * c2 + cross * s2).reshape(6, H, H).astype(BF16)


def _hy_spec_folded_kernel(L, tr, c_ref, s_ref, xc_ref, xs_ref, xh_ref, p_ref, q_ref):
    half = pl.program_id(2)
    r = pl.program_id(3)
    m = r * tr + lax.broadcasted_iota(jnp.int32, (tr, 1), 0)
    alt = (1 - 2 * (m % 2)).astype(F32)
    scale = jnp.where(jnp.logical_and(half == 0, m == 0), 1.0, 2.0) * (1.0 / (2 * L))
    mid = alt * xh_ref[...]
    kc = jnp.dot(c_ref[...], xc_ref[...], preferred_element_type=F32) + jnp.where(half == 0, mid, 0.0)
    ks = jnp.dot(s_ref[...], xs_ref[...], preferred_element_type=F32) + jnp.where(half == 1, mid, 0.0)
    p_ref[...] = scale * kc
    q_ref[...] = scale * ks


def _hy_spectrum_folded(L, tabs, xc, xs, xh):
    H = L // 2
    tr = min(H, 512)
    tc = 512
    nrt = H // tr
    xspec = pl.BlockSpec((None, None, H, tc), lambda o, j, hf, r: (hf, o, 0, j))
    hspec = pl.BlockSpec((None, None, 1, tc), lambda o, j, hf, r: (hf, o, 0, j))
    cspec = pl.BlockSpec((None, tr, H), lambda o, j, hf, r: (2 * hf, r, 0))
    sspec = pl.BlockSpec((None, tr, H), lambda o, j, hf, r: (2 * hf + 1, r, 0))
    ospec = pl.BlockSpec((None, tr, tc), lambda o, j, hf, r: (o, hf * nrt + r, j))
    return pl.pallas_call(
        functools.partial(_hy_spec_folded_kernel, L, tr),
        out_shape=(jax.ShapeDtypeStruct((2, L, D), F32), jax.ShapeDtypeStruct((2, L, D), F32)),
        grid=(2, D // tc, 2, nrt),
        in_specs=[cspec, sspec, xspec, xspec, hspec],
        out_specs=(ospec, ospec),
        compiler_params=_cp("arbitrary", "arbitrary", "arbitrary", "arbitrary"),
        name="hy_spectrum",
    )(tabs, tabs, xc.astype(BF16), xs.astype(BF16), xh[:, :, None, :])


def _hy_fold_kernel(L, tr, ngrp, tc, x1_ref, x2_ref, v_ref, sw1_ref, sw2_ref, swv_ref, sb1_ref, sb2_ref,
                    sbv_ref, ce_ref, se_ref, co_ref, so_ref, pe_ref, po_ref, qe_ref, qo_ref, kny_ref, skip_ref,
                    *rest):
    o_ref, z0_sc, z1_sc, x2_sc, a_sc, b_sc, mid_sc = rest[-7:]
    H = L // 2
    nrt = H // tr
    nb = H // FOLD_BLK
    W = ngrp * tc
    ph = pl.program_id(2)
    r = pl.program_id(3)
    tcv = min(L, 512)
    halo = 16
    dot = functools.partial(jnp.dot, preferred_element_type=F32)
    ZH, X1H, X2H, YSP, NY, NYACC = range(6)
    alt_h = 1.0 - 2.0 * (H % 2)

    def alt_sign(start, n):
        t = start + lax.broadcasted_iota(jnp.int32, (n, 1), 0)
        return (1 - 2 * (t % 2)).astype(F32)

    def conv3(src_ref, g, a, w_ref, b_ref):
        x = src_ref[g, pl.ds(a, tcv), :].astype(F32)
        row = lax.broadcasted_iota(jnp.int32, (tcv, 1), 0)
        up_at = pl.multiple_of(jnp.maximum(a - halo, 0), halo)
        dn_at = pl.multiple_of(jnp.minimum(a + tcv, L - halo), halo)
        up = src_ref[g, pl.ds(up_at, halo), :][halo - 1:halo, :].astype(F32)
        dn = src_ref[g, pl.ds(dn_at, halo), :][0:1, :].astype(F32)
        up = jnp.where(a > 0, up, 0.0)
        dn = jnp.where(a + tcv < L, dn, 0.0)
        prev = jnp.where(row == 0, up, pltpu.roll(x, 1, 0))
        nxt = jnp.where(row == tcv - 1, dn, pltpu.roll(x, tcv - 1, 0))
        return prev * w_ref[0:1, :] + x * w_ref[1:2, :] + nxt * w_ref[2:3, :] + b_ref[...]

    def flip_mats():
        u = lax.broadcasted_iota(jnp.int32, (FOLD_BLK, FOLD_BLK), 0)
        v = lax.broadcasted_iota(jnp.int32, (FOLD_BLK, FOLD_BLK), 1)
        jshift = jnp.where(jnp.logical_and(u >= 1, v == FOLD_BLK - u), 1.0, 0.0).astype(BF16)
        e0 = jnp.where(jnp.logical_and(u == 0, v == 0), 1.0, 0.0).astype(BF16)
        return jshift, e0

    def flipped_block(src, b, jshift, e0):
        blk = lambda k: src[H + k * FOLD_BLK:H + (k + 1) * FOLD_BLK, :]
        out = dot(jshift, blk(nb - 1 - b))
        if b >= 1:
            out = out + dot(e0, blk(nb - b))
        return out

    def fold_in_place(sc, stage, off, jshift, e0):
        for b in range(nb):
            stage[off + b * FOLD_BLK:off + (b + 1) * FOLD_BLK, :] = flipped_block(sc, b, jshift, e0).astype(BF16)

        def copy_back(b, carry):
            at = pl.multiple_of(b * FOLD_BLK, FOLD_BLK)
            sc[pl.ds(H + at, FOLD_BLK), :] = stage[pl.ds(off + at, FOLD_BLK), :]
            return carry

        lax.fori_loop(0, nb, copy_back, 0)

    def to_operands(sc):
        def tile(i, acc):
            a = pl.multiple_of(i * FOLD_BLK, FOLD_BLK)
            ra, rb = pl.ds(a, FOLD_BLK), pl.ds(pl.multiple_of(H + a, FOLD_BLK), FOLD_BLK)
            lo = sc[ra, :].astype(F32)
            hr = sc[rb, :].astype(F32)
            zs = (lo + hr).astype(BF16)
            sc[ra, :] = zs
            sc[rb, :] = (lo - hr).astype(BF16)
            return acc + jnp.sum(alt_sign(a, FOLD_BLK) * zs.astype(F32), axis=0, keepdims=True)

        return lax.fori_loop(0, nb, tile, jnp.zeros((1, W), F32))

    def row_of(sc, at):
        return sc[at:at + halo, :][0:1, :].astype(F32)

    @pl.when(jnp.logical_and(ph == 0, r == 0))
    def _():
        def conv_tile(ti, carry):
            a = pl.multiple_of(ti * tcv, tcv)
            rows_a = pl.ds(a, tcv)
            for g in range(ngrp):
                cols = slice(g * tc, (g + 1) * tc)
                z1_sc[rows_a, cols] = conv3(x1_ref, g, a, sw1_ref, sb1_ref).astype(BF16)
                x2_sc[rows_a, cols] = conv3(x2_ref, g, a, sw2_ref, sb2_ref).astype(BF16)
                z0_sc[rows_a, cols] = conv3(v_ref, g, a, swv_ref, sbv_ref).astype(BF16)
            return carry

        lax.fori_loop(0, L // tcv, conv_tile, 0)
        mid_sc[ZH:ZH + 1, :] = row_of(z0_sc, H)
        mid_sc[X1H:X1H + 1, :] = row_of(z1_sc, H)
        mid_sc[X2H:X2H + 1, :] = row_of(x2_sc, H)
        jshift, e0 = flip_mats()
        for sc, stage, off in ((z0_sc, a_sc, 0), (z1_sc, a_sc, H), (x2_sc, b_sc, 0)):
            fold_in_place(sc, stage, off, jshift, e0)
        mid_sc[NY:NY + 1, :] = to_operands(z0_sc) + alt_h * mid_sc[ZH:ZH + 1, :]
        mid_sc[YSP:YSP + 1, :] = jnp.zeros((1, W), F32)
        mid_sc[NYACC:NYACC + 1, :] = jnp.zeros((1, W), F32)

    start = pl.multiple_of(r * tr, tr)
    rows = pl.ds(start, tr)
    rows_hi = pl.ds(pl.multiple_of(H + r * tr, tr), tr)
    rep = lambda ref: jnp.concatenate([ref[...]] * ngrp, axis=1)
    rep_row = lambda ref, o: jnp.concatenate([ref[o:o + 1, :]] * ngrp, axis=1)

    def forward(z_sc):
        zs, zd = z_sc[0:H, :], z_sc[H:L, :]
        mid = alt_sign(start, tr) * mid_sc[ZH:ZH + 1, :]
        zre_e = dot(ce_ref[...], zs) + mid
        zim_e = dot(se_ref[...], zd)
        zre_o = dot(co_ref[...], zd)
        zim_o = dot(so_ref[...], zs) + mid
        pe, qe, po, qo = rep(pe_ref), rep(qe_ref), rep(po_ref), rep(qo_ref)
        a_e = zre_e * pe - zim_e * qe
        b_e = zim_e * pe + zre_e * qe
        a_o = zre_o * po - zim_o * qo
        b_o = zim_o * po + zre_o * qo
        a_sc[rows, :] = a_e.astype(BF16)
        b_sc[rows, :] = b_e.astype(BF16)
        a_sc[rows_hi, :] = a_o.astype(BF16)
        b_sc[rows_hi, :] = b_o.astype(BF16)
        mid_sc[YSP:YSP + 1, :] += jnp.sum(alt_sign(start, tr) * (a_e + b_o), axis=0, keepdims=True)

    def inverse(order, zin_sc):
        pe = dot(ce_ref[...], a_sc[0:H, :])
        qe = dot(se_ref[...], b_sc[0:H, :])
        po = dot(co_ref[...], a_sc[H:L, :])
        qo = dot(so_ref[...], b_sc[H:L, :])
        c = alt_sign(start, tr) * (mid_sc[NY:NY + 1, :] * rep_row(kny_ref, order))
        half_skip = 0.5 * rep_row(skip_ref, order)
        zs = zin_sc[rows, :].astype(F32)
        zd = zin_sc[rows_hi, :].astype(F32)
        y_lo = pe + qe + po + qo + c + half_skip * (zs + zd)
        y_hr = pe - qe - po + qo + c + half_skip * (zs - zd)
        return y_lo, y_hr

    def middle(order):
        return (mid_sc[YSP:YSP + 1, :] + alt_h * mid_sc[NY:NY + 1, :] * rep_row(kny_ref, order)
                + rep_row(skip_ref, order) * mid_sc[ZH:ZH + 1, :])

    @pl.when(ph == 0)
    def _():
        forward(z0_sc)

    @pl.when(ph == 1)
    def _():
        y_lo, y_hr = inverse(0, z0_sc)
        z_lo = (z1_sc[rows, :].astype(F32) * y_lo).astype(BF16)
        z_hr = (z1_sc[rows_hi, :].astype(F32) * y_hr).astype(BF16)
        z1_sc[rows, :] = z_lo
        z1_sc[rows_hi, :] = z_hr
        mid_sc[NYACC:NYACC + 1, :] += jnp.sum(alt_sign(start, tr) * (z_lo.astype(F32) + z_hr.astype(F32)),
                                              axis=0, keepdims=True)

        @pl.when(r == nrt - 1)
        def _():
            z1h = (mid_sc[X1H:X1H + 1, :] * middle(0)).astype(BF16).astype(F32)
            mid_sc[ZH:ZH + 1, :] = z1h
            mid_sc[NY:NY + 1, :] = mid_sc[NYACC:NYACC + 1, :] + alt_h * z1h
            mid_sc[YSP:YSP + 1, :] = jnp.zeros((1, W), F32)

    @pl.when(ph == 2)
    def _():
        @pl.when(r == 0)
        def _():
            to_operands(z1_sc)

        forward(z1_sc)

    @pl.when(ph == 3)
    def _():
        y_lo, y_hr = inverse(1, z1_sc)
        z0_sc[rows, :] = (x2_sc[rows, :].astype(F32) * y_lo).astype(BF16)
        z0_sc[rows_hi, :] = (x2_sc[rows_hi, :].astype(F32) * y_hr).astype(BF16)

        @pl.when(r == nrt - 1)
        def _():
            out_h = mid_sc[X2H:X2H + 1, :] * middle(1)
            jshift, e0 = flip_mats()
            first = (lax.broadcasted_iota(jnp.int32, (FOLD_BLK, 1), 0) == 0).astype(F32)
            for b in range(nb):
                up = flipped_block(z0_sc, b, jshift, e0)
                if b == 0:
                    up = up + first * out_h
                for g in range(ngrp):
                    cols = slice(g * tc, (g + 1) * tc)
                    o_ref[g, H + b * FOLD_BLK:H + (b + 1) * FOLD_BLK, :] = up[:, cols].astype(o_ref.dtype)
            for b in range(nb):
                lo_rows = slice(b * FOLD_BLK, (b + 1) * FOLD_BLK)
                for g in range(ngrp):
                    o_ref[g, lo_rows, :] = z0_sc[lo_rows, g * tc:(g + 1) * tc]


def _hy_core_folded(proj, short_w, short_b, skip, tabs, pq, k_ny, o_prev, row0, nseq, L, ngrp, tc):
    T = proj.shape[0]
    H = L // 2
    tr = min(H, 256)
    nrt = H // tr
    assert H % FOLD_BLK == 0 and row0 % (L * ngrp) == 0 and nseq % ngrp == 0 and T % L == 0
    sb0 = row0 // (L * ngrp)
    nct = D // tc
    p3 = proj.reshape(T // L, L, 3 * D)
    p_arr, q_arr = pq
    once = dict(pipeline_mode=pl.Buffered(1))

    def xspec(part):
        return pl.BlockSpec((ngrp, L, tc), lambda i, j, ph, r: (sb0 + i, 0, part * nct + j), **once)

    def wspec(part, rows_):
        return pl.BlockSpec((rows_, tc), lambda i, j, ph, r: (0, part * nct + j))

    def tspec(k_fwd, k_inv):
        return pl.BlockSpec((None, tr, H), lambda i, j, ph, r: (jnp.where(ph % 2 == 0, k_fwd, k_inv), r, 0))

    def pqspec(half):
        def imap(i, j, ph, r):
            return (ph // 2, half * nrt + jnp.where(ph % 2 == 0, r, nrt - 1), j)
        return pl.BlockSpec((None, tr, tc), imap)

    in_specs = [xspec(0), xspec(1), xspec(2), wspec(0, 3), wspec(1, 3), wspec(2, 3),
                wspec(0, 1), wspec(1, 1), wspec(2, 1),
                tspec(0, 0), tspec(1, 1), tspec(2, 4), tspec(3, 5),
                pqspec(0), pqspec(1), pqspec(0), pqspec(1),
                pl.BlockSpec((2, tc), lambda i, j, ph, r: (0, j)),
                pl.BlockSpec((2, tc), lambda i, j, ph, r: (0, j))]
    sb = short_b.reshape(1, 3 * D)
    args = [p3, p3, p3, short_w, short_w, short_w, sb, sb, sb, tabs, tabs, tabs, tabs,
            p_arr, p_arr, q_arr, q_arr, k_ny, skip]
    in_specs.append(pl.BlockSpec(memory_space=pl.ANY))
    args.append(o_prev.reshape(T // L, L, D))
    aliases = {len(args) - 1: 0}
    W = ngrp * tc
    out = pl.pallas_call(
        functools.partial(_hy_fold_kernel, L, tr, ngrp, tc),
        out_shape=jax.ShapeDtypeStruct((T // L, L, D), BF16),
        grid=(nseq // ngrp, nct, 4, nrt),
        in_specs=in_specs,
        out_specs=pl.BlockSpec((ngrp, L, tc), lambda i, j, ph, r: (sb0 + i, 0, j)),
        scratch_shapes=[pltpu.VMEM((L, W), BF16)] * 5 + [pltpu.VMEM((8, W), F32)],
        input_output_aliases=aliases,
        compiler_params=_cp("arbitrary", "arbitrary", "arbitrary", "arbitrary"),
        name="hy_core",
    )(*args)
    return out.reshape(T, D)


def _plain_out_kernel(x_ref, z_ref, g_ref, w_ref, b_ref, o_ref):
    acc = jnp.dot(z_ref[...], w_ref[...], preferred_element_type=F32) + b_ref[...]
    o_ref[...] = x_ref[...] + g_ref[...] * acc


def _plain_out(lay, x, mods, z, w, b):
    tm = lay.tile(512)
    tn = D
    kdim = z.shape[1]
    return pl.pallas_call(
        _plain_out_kernel,
        out_shape=jax.ShapeDtypeStruct((lay.T, D), F32),
        grid=(lay.T // tm, D // tn),
        in_specs=[pl.BlockSpec((tm, tn), lambda i, j: (i, j)),
                  pl.BlockSpec((tm, kdim), lambda i, j: (i, 0)),
                  pl.BlockSpec((None, 1, tn), lambda i, j: (lay.group(i * tm) * MOD_CHUNKS + 2, 0, j)),
                  pl.BlockSpec((kdim, tn), lambda i, j: (0, j), pipeline_mode=pl.Buffered(1)),
                  pl.BlockSpec((1, tn), lambda i, j: (0, j))],
        out_specs=pl.BlockSpec((tm, tn), lambda i, j: (i, j)),
        compiler_params=_cp("arbitrary", "arbitrary"),
        name="plain_out",
    )(x, z, mods, w, b.reshape(1, D))


def _hyena_layer(lay, x, mods, p):
    proj = _proj(lay, x, mods, 0, 1, p['w_in'].astype(BF16), p['b_in'].reshape(1, 3 * D), 3 * D, tm=512)
    z = jnp.zeros((lay.T, D), BF16)
    for row0, nseq, L, ngrp, tc in ((0, lay.B, lay.L, math.gcd(lay.B, 4), 256),
                                    (lay.TP, lay.NS, lay.LS, lay.NS, 256)):
        conv = (proj, p['short_w'], p['short_b'], p['skip'])
        if (L // 2) % FOLD_BLK == 0 and L > 2 * FOLD_BLK:
            xc, xs, xh, k_ny = _hyena_filters_folded(L, p)
            tabs = _dft_tables_folded(L)
            pq = _hy_spectrum_folded(L, tabs, xc, xs, xh)
            z = _hy_core_folded(*conv, tabs, pq, k_ny, z, row0, nseq, L, ngrp, tc)
        else:
            k_lo, k_hi, k_ny = _hyena_filters(L, p)
            cos, sin = _dft_tables(L)
            pq = _hy_spectrum(L, cos, sin, k_lo, k_hi)
            z = _hy_core(*conv, cos, sin, pq, k_ny, z, row0, nseq, L, ngrp, tc)
    return _plain_out(lay, x, mods, z, p['w_out'].astype(BF16), p['b_out'])


_NT = (((1,), (1,)), ((), ()))
_TN = (((0,), (0,)), ((), ()))


def _tri(dr):
    t = lax.broadcasted_iota(jnp.int32, (CHUNK, CHUNK), 0)
    s = lax.broadcasted_iota(jnp.int32, (CHUNK, CHUNK), 1)
    return (s <= t) if dr == 0 else (s >= t)


def _chunk_cumsum(g, dr):
    n = g.shape[0]
    pos = lax.broadcasted_iota(jnp.int32, g.shape, 0) % CHUNK
    sh = 1
    while sh < CHUNK:
        if dr == 0:
            g = g + jnp.where(pos >= sh, pltpu.roll(g, sh, 0), 0.0)
        else:
            g = g + jnp.where(pos < CHUNK - sh, pltpu.roll(g, n - sh, 0), 0.0)
        sh *= 2
    return g


def _head_epilogue(o_sc, gate_ref, ng_ref, a_ref, center):
    rows = o_sc.shape[0]
    tr = math.gcd(rows, 256)

    def tile(i, carry):
        r = pl.ds(pl.multiple_of(i * tr, tr), tr)
        o = o_sc[r, :]
        if center:
            o = o - jnp.mean(o, axis=-1, keepdims=True)
        o = o * lax.rsqrt(jnp.mean(o * o, axis=-1, keepdims=True) + RMS_EPS) * ng_ref[...]
        a_ref[r, :] = (o * _silu(gate_ref[r, :].astype(F32))).astype(a_ref.dtype)
        return carry

    lax.fori_loop(0, rows // tr, tile, 0)


def _gla_kernel(cps, nseg, U, has_s0, want_final, *refs):
    q_ref, k_ref, v_ref, lr_ref, w2f_ref, w2b_ref, gb_ref, gate_ref, ng_ref = refs[:9]
    s0_ref = refs[9] if has_s0 else None
    qin_sc, kin_sc, kout_sc, dec_sc, st_sc, s_sc, s0t_sc, o_ref = refs[-8:]
    outs = refs[-10:-8] if want_final else refs[-9:-8]
    a_ref = outs[0]
    sf_ref = outs[1] if want_final else None
    C = CHUNK
    nsc = cps // U
    nchunks = nseg * cps
    rows_total = nchunks * C
    w2 = (w2f_ref, w2b_ref)

    for dr in range(2):
        pre = jnp.dot(lr_ref[...], w2[dr][...], preferred_element_type=F32) + gb_ref[dr:dr + 1, :]
        g = (jnp.minimum(pre, 0.0) - jnp.log(1.0 + jnp.exp(-jnp.abs(pre)))) * (1.0 / GLA_TAU)
        b = _chunk_cumsum(g, dr)
        b3 = b.reshape(nchunks, C, GLA_DK)
        tot = b3[:, C - 1:C, :] if dr == 0 else b3[:, 0:1, :]
        dec_sc[...] = jnp.exp(tot).reshape(nchunks, GLA_DK)
        k = k_ref[...].astype(F32)
        qin_sc[...] = (q_ref[...].astype(F32) * (GLA_DK ** -0.5) * jnp.exp(b)).astype(BF16)
        kin_sc[...] = (k * jnp.exp(-b)).astype(BF16)
        kout_sc[...] = (k * jnp.exp(tot - b3).reshape(rows_total, GLA_DK)).astype(BF16)
        if has_s0:
            s0t_sc[...] = jnp.transpose(s0_ref[dr], (1, 0))
        tri = _tri(dr)

        def super_chunk(jj, carry, dr=dr, tri=tri):
            j = jj if dr == 0 else nseg * nsc - 1 - jj
            in_seg = j % nsc
            first = (in_seg == 0) if dr == 0 else (in_seg == nsc - 1)
            last = (in_seg == nsc - 1) if dr == 0 else (in_seg == 0)

            @pl.when(first)
            def _():
                s_sc[...] = s0t_sc[...] if has_s0 else jnp.zeros_like(s_sc)

            base = j * (U * C)
            for u in range(U):
                rows = pl.ds(pl.multiple_of(base + u * C, C), C)
                v = v_ref[rows, :]
                sc = lax.dot_general(qin_sc[rows, :], kin_sc[rows, :], _NT, preferred_element_type=F32)
                o = jnp.dot(jnp.where(tri, sc, 0.0).astype(BF16), v, preferred_element_type=F32)
                st_sc[u] = lax.dot_general(v, kout_sc[rows, :], _TN, preferred_element_type=F32)
                if dr == 0:
                    o_ref[rows, :] = o
                else:
                    o_ref[rows, :] += o
            s = s_sc[...]
            for u in (range(U) if dr == 0 else reversed(range(U))):
                kv = st_sc[u]
                st_sc[u] = s
                s = dec_sc[pl.ds(j * U + u, 1), :] * s + kv
            s_sc[...] = s
            for u in range(U):
                rows = pl.ds(pl.multiple_of(base + u * C, C), C)
                o_ref[rows, :] += lax.dot_general(qin_sc[rows, :], st_sc[u].astype(BF16), _NT,
                                                  preferred_element_type=F32)
            if want_final:
                @pl.when(last)
                def _():
                    sf_ref[j // nsc, dr] = jnp.transpose(s, (1, 0))
            return carry

        lax.fori_loop(0, nseg * nsc, super_chunk, 0)

    _head_epilogue(o_ref, gate_ref, ng_ref, a_ref, center=False)


def _gla_core(proj, w2f, w2b, gate_b, norm_g, s0, o_prev, row0, nseq, seqlen, nseg, want_final):
    T = proj.shape[0]
    rows = nseg * seqlen
    cps = seqlen // CHUNK
    U = math.gcd(cps, 8)
    assert row0 % rows == 0 and seqlen % CHUNK == 0 and nseq % nseg == 0
    rb = row0 // rows
    hk = GLA_H * GLA_DK
    has_s0 = s0 is not None
    assert not has_s0 or nseg == 1
    in_specs = [pl.BlockSpec((rows, GLA_DK), lambda b, h: (rb + b, h)),
                pl.BlockSpec((rows, GLA_DK), lambda b, h: (rb + b, GLA_H + h)),
                pl.BlockSpec((rows, GLA_DV), lambda b, h: (rb + b, 2 * hk // GLA_DV + h)),
                pl.BlockSpec((rows, 128), lambda b, h: (rb + b, (2 * hk + 2 * GLA_H * GLA_DV) // 128)),
                pl.BlockSpec((128, GLA_DK), lambda b, h: (0, h)),
                pl.BlockSpec((128, GLA_DK), lambda b, h: (0, h)),
                pl.BlockSpec((2, GLA_DK), lambda b, h: (0, h)),
                pl.BlockSpec((rows, GLA_DV), lambda b, h: (rb + b, (2 * hk) // GLA_DV + GLA_H + h)),
                pl.BlockSpec((1, GLA_DV), lambda b, h: (0, 0))]
    args = [proj, proj, proj, proj, w2f, w2b, gate_b, proj, norm_g.reshape(1, GLA_DV)]
    if has_s0:
        in_specs.append(pl.BlockSpec((None, 2, None, GLA_DK, GLA_DV), lambda b, h: (b, 0, h, 0, 0)))
        args.append(s0)
    in_specs.append(pl.BlockSpec(memory_space=pl.ANY))
    args.append(o_prev)
    aliases = {len(args) - 1: 0}
    out_shape = [jax.ShapeDtypeStruct((T, GLA_H * GLA_DV), BF16)]
    out_specs = [pl.BlockSpec((rows, GLA_DV), lambda b, h: (rb + b, h))]
    if want_final:
        out_shape.append(jax.ShapeDtypeStruct((nseq, 2, GLA_H, GLA_DK, GLA_DV), F32))
        out_specs.append(pl.BlockSpec((nseg, 2, None, GLA_DK, GLA_DV), lambda b, h: (b, 0, h, 0, 0)))
    outs = pl.pallas_call(
        functools.partial(_gla_kernel, cps, nseg, U, has_s0, want_final),
        out_shape=tuple(out_shape),
        grid=(nseq // nseg, GLA_H),
        in_specs=in_specs,
        out_specs=tuple(out_specs),
        scratch_shapes=[pltpu.VMEM((rows, GLA_DK), BF16)] * 3
        + [pltpu.VMEM((nseg * cps, GLA_DK), F32), pltpu.VMEM((U, GLA_DV, GLA_DK), F32),
           pltpu.VMEM((GLA_DV, GLA_DK), F32), pltpu.VMEM((GLA_DV, GLA_DK), F32),
           pltpu.VMEM((rows, GLA_DV), F32)],
        input_output_aliases=aliases,
        compiler_params=_cp("arbitrary", "arbitrary"),
        name="gla",
    )(*args)
    return (outs[0], outs[1]) if want_final else (outs[0], None)


def _ret_kernel(cps, nseg, U, has_s0, want_final, rope, *refs):
    q_ref, k_ref, v_ref, dm_ref, qd_ref, kd_ref, cd_ref, gate_ref, ng_ref = refs[:9]
    nxt = 9
    if rope:
        cos_ref, sin_ref = refs[9:11]
        nxt = 11
    s0_ref = refs[nxt] if has_s0 else None
    qr_sc, kr_sc, qd_sc, kd_sc, st_sc, s_sc, o_ref = refs[-7:]
    outs = refs[-9:-7] if want_final else refs[-8:-7]
    a_ref = outs[0]
    sf_ref = outs[1] if want_final else None
    C = RET_CHUNK
    nsc = cps // U
    R = U * C
    SB = 64

    def rot(x, rows):
        if not rope:
            return x
        half = x.shape[1] // 2
        swapped = jnp.concatenate([pltpu.roll(x[:, :half], half // 2, 1),
                                   pltpu.roll(x[:, half:], half // 2, 1)], axis=1)
        return x * cos_ref[rows, :] + swapped * sin_ref[rows, :]

    for dr in range(2):
        def super_chunk(jj, carry, dr=dr):
            j = jj if dr == 0 else nseg * nsc - 1 - jj
            in_seg = j % nsc
            first = (in_seg == 0) if dr == 0 else (in_seg == nsc - 1)
            last = (in_seg == nsc - 1) if dr == 0 else (in_seg == 0)

            @pl.when(first)
            def _():
                s_sc[...] = s0_ref[dr] if has_s0 else jnp.zeros_like(s_sc)

            base = pl.multiple_of(j * R, R)
            rows_r = pl.ds(base, R)
            q = rot(q_ref[rows_r, :].astype(F32), rows_r)
            k = rot(k_ref[rows_r, :].astype(F32), rows_r) * (RET_DK ** -0.5)
            qr_sc[...] = q.astype(BF16)
            kr_sc[...] = k.astype(BF16)
            qd_sc[...] = (q.reshape(U, C, RET_DK) * qd_ref[dr][None]).reshape(R, RET_DK).astype(BF16)
            kd_sc[...] = (k.reshape(U, C, RET_DK) * kd_ref[dr][None]).reshape(R, RET_DK).astype(BF16)
            for u in range(U):
                loc = pl.ds(u * C, C)
                rows = pl.ds(pl.multiple_of(base + u * C, C), C)
                v = v_ref[rows, :]
                sc = lax.dot_general(qr_sc[loc, :], kr_sc[loc, :], _NT, preferred_element_type=F32)
                o = jnp.dot((sc * dm_ref[dr]).astype(BF16), v, preferred_element_type=F32)
                st_sc[u] = lax.dot_general(kd_sc[loc, :], v, _TN, preferred_element_type=F32)
                if dr == 0:
                    o_ref[rows, :] = o
                else:
                    o_ref[rows, :] += o
            cd = cd_ref[dr]
            for r0 in range(0, RET_DK, SB):
                srows = pl.ds(r0, SB)
                s = s_sc[srows, :]
                for u in (range(U) if dr == 0 else reversed(range(U))):
                    kv = st_sc[u, srows, :]
                    st_sc[u, srows, :] = s
                    s = cd * s + kv
                s_sc[srows, :] = s
            for u in range(U):
                rows = pl.ds(pl.multiple_of(base + u * C, C), C)
                o_ref[rows, :] += jnp.dot(qd_sc[pl.ds(u * C, C), :], st_sc[u].astype(BF16),
                                          preferred_element_type=F32)
            if want_final:
                @pl.when(last)
                def _():
                    sf_ref[j // nsc, dr] = s_sc[...]
            return carry

        lax.fori_loop(0, nseg * nsc, super_chunk, 0)

    _head_epilogue(o_ref, gate_ref, ng_ref, a_ref, center=True)


def _ret_tables(log_decay):
    C = RET_CHUNK
    lg = log_decay.astype(F32)[:, :, None, None]
    t = jnp.arange(C, dtype=F32)[:, None]
    s = jnp.arange(C, dtype=F32)[None, :]
    lag = jnp.stack([t - s, s - t])[:, None]
    dmask = jnp.where(lag >= 0, jnp.exp(jnp.maximum(lag, 0.0) * lg), 0.0)
    tl = jnp.arange(C, dtype=F32)[None, None, :, None]
    qdec = jnp.concatenate([jnp.exp((tl + 1.0) * lg[0:1]), jnp.exp((C - tl) * lg[1:2])], axis=0)
    kdec = jnp.concatenate([jnp.exp((C - 1.0 - tl) * lg[0:1]), jnp.exp(tl * lg[1:2])], axis=0)
    cdec = jnp.exp(C * lg)
    return dmask, qdec, kdec, cdec


def _rope_tables(seqlen, dk):
    half = dk // 2
    nf = half // 2
    pos = jnp.arange(seqlen, dtype=jnp.int32)
    inv = ROPE_BASE ** (-jnp.arange(nf, dtype=F32) / nf)
    ang_r = (pos // GRID_W).astype(F32)[:, None] * inv[None, :]
    ang_c = (pos % GRID_W).astype(F32)[:, None] * inv[None, :]
    cos = jnp.concatenate([jnp.cos(ang_r)] * 2 + [jnp.cos(ang_c)] * 2, axis=1)
    sin = jnp.concatenate([-jnp.sin(ang_r), jnp.sin(ang_r), -jnp.sin(ang_c), jnp.sin(ang_c)], axis=1)
    return cos, sin


def _ret_core(proj, tabs, norm_g, s0, o_prev, row0, nseq, seqlen, nseg, want_final, rope):
    T = proj.shape[0]
    rows = nseg * seqlen
    C = RET_CHUNK
    cps = seqlen // C
    U = math.gcd(cps, 4)
    assert row0 % rows == 0 and seqlen % C == 0 and nseq % nseg == 0
    rb = row0 // rows
    hk, hv = RET_H * RET_DK, RET_H * RET_DV
    has_s0 = s0 is not None
    assert not (has_s0 or rope) or nseg == 1
    tspec = lambda r, c: pl.BlockSpec((2, None, r, c), lambda b, h: (0, h, 0, 0))
    mode = dict(pipeline_mode=pl.Buffered(1)) if rows * RET_DV * 2 >= (4 << 20) else {}
    in_specs = [pl.BlockSpec((rows, RET_DK), lambda b, h: (rb + b, h), **mode),
                pl.BlockSpec((rows, RET_DK), lambda b, h: (rb + b, RET_H + h), **mode),
                pl.BlockSpec((rows, RET_DV), lambda b, h: (rb + b, 2 * hk // RET_DV + h), **mode),
                tspec(C, C), tspec(C, 1), tspec(C, 1), tspec(1, 1),
                pl.BlockSpec((rows, RET_DV), lambda b, h: (rb + b, (2 * hk + hv) // RET_DV + h), **mode),
                pl.BlockSpec((1, RET_DV), lambda b, h: (0, 0))]
    args = [proj, proj, proj, *tabs, proj, norm_g.reshape(1, RET_DV)]
    if rope:
        cos, sin = _rope_tables(seqlen, RET_DK)
        in_specs += [pl.BlockSpec((seqlen, RET_DK), lambda b, h: (0, 0), pipeline_mode=pl.Buffered(1))] * 2
        args += [cos, sin]
    if has_s0:
        in_specs.append(pl.BlockSpec((None, 2, None, RET_DK, RET_DV), lambda b, h: (b, 0, h, 0, 0)))
        args.append(s0)
    in_specs.append(pl.BlockSpec(memory_space=pl.ANY))
    args.append(o_prev)
    aliases = {len(args) - 1: 0}
    out_shape = [jax.ShapeDtypeStruct((T, hv), BF16)]
    out_specs = [pl.BlockSpec((rows, RET_DV), lambda b, h: (rb + b, h))]
    if want_final:
        out_shape.append(jax.ShapeDtypeStruct((nseq, 2, RET_H, RET_DK, RET_DV), F32))
        out_specs.append(pl.BlockSpec((nseg, 2, None, RET_DK, RET_DV), lambda b, h: (b, 0, h, 0, 0)))
    outs = pl.pallas_call(
        functools.partial(_ret_kernel, cps, nseg, U, has_s0, want_final, rope),
        out_shape=tuple(out_shape),
        grid=(nseq // nseg, RET_H),
        in_specs=in_specs,
        out_specs=tuple(out_specs),
        scratch_shapes=[pltpu.VMEM((U * C, RET_DK), BF16)] * 4
        + [pltpu.VMEM((U, RET_DK, RET_DV), F32), pltpu.VMEM((RET_DK, RET_DV), F32),
           pltpu.VMEM((rows, RET_DV), F32)],
        input_output_aliases=aliases,
        compiler_params=_cp("arbitrary", "arbitrary"),
        name="ret",
    )(*args)
    return (outs[0], outs[1]) if want_final else (outs[0], None)


def _gla_layer(lay, x, mods, p, s0):
    hk, hv = GLA_H * GLA_DK, GLA_H * GLA_DV
    w_all = jnp.concatenate([p['w_in'], p['gate_w1'][0], p['gate_w1'][1],
                             jnp.zeros((D, 128 - 2 * GLA_RANK), F32)], axis=1).astype(BF16)
    proj = _proj(lay, x, mods, 0, 1, w_all, jnp.zeros((1, w_all.shape[1]), F32), w_all.shape[1], tm=512)
    pad = lambda w, lo: jnp.pad(w, ((lo, 128 - GLA_RANK - lo), (0, 0))).astype(BF16)
    w2f, w2b = pad(p['gate_w2'][0], 0), pad(p['gate_w2'][1], GLA_RANK)
    a = jnp.zeros((lay.T, hv), BF16)
    a, s_fin = _gla_core(proj, w2f, w2b, p['gate_b'], p['norm_g'], None, a, 0, lay.B, lay.L,
                         math.gcd(lay.B, 8), True)
    a, _ = _gla_core(proj, w2f, w2b, p['gate_b'], p['norm_g'], s0, a, lay.TP, lay.NS, lay.LS, 1, False)
    x = _plain_out(lay, x, mods, a, p['w_out'].astype(BF16), jnp.zeros((D,), F32))
    return x, s_fin


def _ret_layer(lay, x, mods, p, s0):
    hk, hv = RET_H * RET_DK, RET_H * RET_DV
    proj = _proj(lay, x, mods, 0, 1, p['w_in'].astype(BF16), jnp.zeros((1, 2 * hk + 2 * hv), F32),
                 2 * hk + 2 * hv, tm=512)
    tabs = _ret_tables(p['log_decay'])
    a = jnp.zeros((lay.T, hv), BF16)
    a, s_fin = _ret_core(proj, tabs, p['norm_g'], None, a, 0, lay.B, lay.L, math.gcd(lay.B, 8), True, False)
    a, _ = _ret_core(proj, tabs, p['norm_g'], s0, a, lay.TP, lay.NS, lay.LS, 1, False, True)
    x = _plain_out(lay, x, mods, a, p['w_out'].astype(BF16), jnp.zeros((D,), F32))
    return x, s_fin


MOE_BM = 512
EXPERT_TF = 1792
ROUTER_LANES = 128
DMA_UNROLL = 8


def _router_kernel(x_ref, sh_ref, sc_ref, rw_ref, h_ref, idx_ref, gate_ref, rank_ref, cnt_ref):
    @pl.when(pl.program_id(0) == 0)
    def _():
        cnt_ref[...] = jnp.zeros_like(cnt_ref)

    h = _modulate(x_ref[...], sh_ref[...], sc_ref[...])
    h_ref[...] = h
    logits = jnp.dot(h, rw_ref[...], precision=HIGHEST, preferred_element_type=F32)
    lane = lax.broadcasted_iota(jnp.int32, logits.shape, 1)
    neg = jnp.float32(-jnp.inf)
    logits = jnp.where(lane < N_EXPERTS, logits, neg)
    m1 = jnp.max(logits, axis=-1, keepdims=True)
    i1 = jnp.min(jnp.where(logits == m1, lane, ROUTER_LANES), axis=-1, keepdims=True)
    rest = jnp.where(lane == i1, neg, logits)
    m2 = jnp.max(rest, axis=-1, keepdims=True)
    i2 = jnp.min(jnp.where(rest == m2, lane, ROUTER_LANES), axis=-1, keepdims=True)
    e2 = jnp.exp(m2 - m1)
    g1 = 1.0 / (1.0 + e2)
    idx_ref[:, 0:1] = i1
    idx_ref[:, 1:2] = i2
    gate_ref[:, 0:1] = g1
    gate_ref[:, 1:2] = e2 * g1
    tm = logits.shape[0]
    sel1 = lane == i1
    sel2 = lane == i2
    picked = jnp.where(jnp.logical_or(sel1, sel2), 1.0, 0.0)
    before = (lax.broadcasted_iota(jnp.int32, (tm, tm), 1)
              < lax.broadcasted_iota(jnp.int32, (tm, tm), 0)).astype(BF16)
    prior = jnp.dot(before, picked.astype(BF16), preferred_element_type=F32) + cnt_ref[...]
    rank_ref[:, 0:1] = jnp.sum(jnp.where(sel1, prior, 0.0), axis=-1, keepdims=True).astype(jnp.int32)
    rank_ref[:, 1:2] = jnp.sum(jnp.where(sel2, prior, 0.0), axis=-1, keepdims=True).astype(jnp.int32)
    cnt_ref[...] += jnp.sum(picked, axis=0, keepdims=True)


def _router(lay, x, mods, router_w):
    tm = lay.tile(512)
    rw = jnp.pad(router_w, ((0, 0), (0, ROUTER_LANES - N_EXPERTS)))
    return pl.pallas_call(
        _router_kernel,
        out_shape=(jax.ShapeDtypeStruct((lay.T, D), F32),
                   jax.ShapeDtypeStruct((lay.T, 2), jnp.int32),
                   jax.ShapeDtypeStruct((lay.T, 2), F32),
                   jax.ShapeDtypeStruct((lay.T, 2), jnp.int32),
                   jax.ShapeDtypeStruct((1, ROUTER_LANES), F32)),
        grid=(lay.T // tm,),
        in_specs=[pl.BlockSpec((tm, D), lambda i: (i, 0)),
                  _mod_spec(lay, tm, 3, 1), _mod_spec(lay, tm, 4, 1),
                  pl.BlockSpec((D, ROUTER_LANES), lambda i: (0, 0))],
        out_specs=(pl.BlockSpec((tm, D), lambda i: (i, 0)),
                   pl.BlockSpec((tm, 2), lambda i: (i, 0)),
                   pl.BlockSpec((tm, 2), lambda i: (i, 0)),
                   pl.BlockSpec((tm, 2), lambda i: (i, 0)),
                   pl.BlockSpec((1, ROUTER_LANES), lambda i: (0, 0))),
        compiler_params=_cp("arbitrary"),
        name="router",
    )(x, mods, mods, rw)


def _moe_plan(idx, rank, counts, bm):
    a = idx.size
    counts = counts[0, :N_EXPERTS].astype(jnp.int32)
    padded = (counts + bm - 1) // bm * bm
    pad_end = jnp.cumsum(padded)
    pad_start = pad_end - padded
    hit = idx[..., None] == jnp.arange(N_EXPERTS, dtype=jnp.int32)
    dest = (rank + jnp.sum(jnp.where(hit, pad_start, 0), axis=-1)).reshape(a).astype(jnp.int32)
    nb = -(-(a + N_EXPERTS * (bm - 1)) // bm)
    block_start = jnp.arange(nb, dtype=jnp.int32) * bm
    block_e = jnp.sum((block_start[:, None] >= pad_end[None, :]).astype(jnp.int32), axis=1)
    block_e = jnp.minimum(block_e, N_EXPERTS - 1)
    nvalid = (pad_end[-1] // bm).astype(jnp.int32).reshape(1)
    fill = jnp.concatenate([pad_start + counts, pad_end, nvalid]).astype(jnp.int32)
    return dest, fill, block_e, nvalid, nb


def _dispatch_kernel(tm, bm, nb, dest_ref, fill_ref, h_ref, xs_hbm, zero_sc, sem, zsem):
    i = pl.program_id(0)
    zr = zero_sc.shape[0]

    @pl.when(i == 0)
    def _():
        zero_sc[...] = jnp.zeros_like(zero_sc)

        def zero_row(r):
            return pltpu.make_async_copy(zero_sc.at[pl.ds(0, 1)], xs_hbm.at[pl.ds(r, 1)], zsem)

        def zero_rows(r):
            return pltpu.make_async_copy(zero_sc, xs_hbm.at[pl.ds(pl.multiple_of(r, zr), zr)], zsem)

        for e in range(N_EXPERTS):
            lo, hi = fill_ref[e], fill_ref[N_EXPERTS + e]
            lax.fori_loop(lo, hi, lambda r, c: (zero_row(r).start(), c)[1], 0)
            lax.fori_loop(lo, hi, lambda r, c: (zero_row(r).wait(), c)[1], 0)
        lo, hi = fill_ref[2 * N_EXPERTS] * (bm // zr), nb * (bm // zr)
        lax.fori_loop(lo, hi, lambda q, c: (zero_rows(q * zr).start(), c)[1], 0)
        lax.fori_loop(lo, hi, lambda q, c: (zero_rows(q * zr).wait(), c)[1], 0)

    def row_copy(r, dst):
        return pltpu.make_async_copy(h_ref.at[pl.ds(r, 1)], xs_hbm.at[pl.ds(dst, 1)], sem)

    def issue(r, carry):
        a = 2 * (i * tm + r)
        row_copy(r, dest_ref[a]).start()
        row_copy(r, dest_ref[a + 1]).start()
        return carry

    def drain(r, carry):
        row_copy(r, 0).wait()
        row_copy(r, 0).wait()
        return carry

    lax.fori_loop(0, tm, issue, 0, unroll=DMA_UNROLL)
    lax.fori_loop(0, tm, drain, 0, unroll=DMA_UNROLL)


def _dispatch(lay, h, dest, fill, nb, bm):
    tm = lay.tile(512)
    grid_spec = pltpu.PrefetchScalarGridSpec(
        num_scalar_prefetch=2,
        grid=(lay.T // tm,),
        in_specs=[pl.BlockSpec((tm, D), lambda i, d, f: (i, 0))],
        out_specs=pl.BlockSpec(memory_space=pl.ANY),
        scratch_shapes=[pltpu.VMEM((64, D), F32), pltpu.SemaphoreType.DMA(()), pltpu.SemaphoreType.DMA(())],
    )
    return pl.pallas_call(
        functools.partial(_dispatch_kernel, tm, bm, nb),
        out_shape=jax.ShapeDtypeStruct((nb * bm, D), F32),
        grid_spec=grid_spec,
        compiler_params=_cp("arbitrary"),
        name="dispatch",
    )(dest, fill, h)


def _experts_kernel(nf, be_ref, nv_ref, xs_ref, wa_ref, wb_ref, wo_ref, o_ref, xb_sc, acc_sc):
    i = pl.program_id(0)
    f = pl.program_id(1)
    valid = i < nv_ref[0]

    @pl.when(jnp.logical_and(valid, f == 0))
    def _():
        xb_sc[...] = xs_ref[...].astype(BF16)

    @pl.when(valid)
    def _():
        xb = xb_sc[...]
        a = jnp.dot(xb, wa_ref[...], preferred_element_type=F32)
        b = jnp.dot(xb, wb_ref[...], preferred_element_type=F32)
        h = (_silu(a) * b).astype(BF16)
        y = jnp.dot(h, wo_ref[...], preferred_element_type=F32)

        @pl.when(f == 0)
        def _():
            acc_sc[...] = y

        @pl.when(f > 0)
        def _():
            acc_sc[...] += y

    @pl.when(f == nf - 1)
    def _():
        o_ref[...] = jnp.where(valid, acc_sc[...], 0.0)


def _experts(xs, block_e, nvalid, nb, bm, w_in, w_out):
    tf = EXPERT_TF
    nf = EXPERT_DIM // tf

    def wmap(off):
        def imap(i, f, be, nv):
            fe = jnp.where(i < nv[0], f, nf - 1)
            return (be[i], 0, off + fe)
        return imap

    def womap(i, f, be, nv):
        fe = jnp.where(i < nv[0], f, nf - 1)
        return (be[i], fe, 0)

    grid_spec = pltpu.PrefetchScalarGridSpec(
        num_scalar_prefetch=2,
        grid=(nb, nf),
        in_specs=[pl.BlockSpec((bm, D), lambda i, f, be, nv: (jnp.minimum(i, nv[0] - 1), 0)),
                  pl.BlockSpec((None, D, tf), wmap(0)),
                  pl.BlockSpec((None, D, tf), wmap(nf)),
                  pl.BlockSpec((None, tf, D), womap)],
        out_specs=pl.BlockSpec((bm, D), lambda i, f, be, nv: (i, 0)),
        scratch_shapes=[pltpu.VMEM((bm, D), BF16), pltpu.VMEM((bm, D), F32)],
    )
    return pl.pallas_call(
        functools.partial(_experts_kernel, nf),
        out_shape=jax.ShapeDtypeStruct((nb * bm, D), F32),
        grid_spec=grid_spec,
        compiler_params=_cp("arbitrary", "arbitrary"),
        name="experts",
    )(block_e, nvalid, xs, w_in, w_in, w_out)


def _combine_kernel(tm, nt, final, dest_ref, x_ref, gate_ref, g_ref, fg_ref, ys_hbm, o_ref, y_sc, sem):
    i = pl.program_id(0)
    slot = i % 2

    def row_copy(s, k, r, src):
        return pltpu.make_async_copy(ys_hbm.at[pl.ds(src, 1)], y_sc.at[s, k, pl.ds(r, 1)], sem.at[s])

    def issue_tile(t, s):
        def issue(r, carry):
            a = 2 * (t * tm + r)
            row_copy(s, 0, r, dest_ref[a]).start()
            row_copy(s, 1, r, dest_ref[a + 1]).start()
            return carry
        lax.fori_loop(0, tm, issue, 0, unroll=DMA_UNROLL)

    @pl.when(i == 0)
    def _():
        issue_tile(0, 0)

    @pl.when(i + 1 < nt)
    def _():
        issue_tile(i + 1, 1 - slot)

    def drain(r, carry):
        row_copy(slot, 0, r, 0).wait()
        row_copy(slot, 1, r, 0).wait()
        return carry

    lax.fori_loop(0, tm, drain, 0, unroll=DMA_UNROLL)
    gate = gate_ref[...]
    out = x_ref[...] + g_ref[...] * (gate[:, 0:1] * y_sc[slot, 0] + gate[:, 1:2] * y_sc[slot, 1])
    if final:
        ms = jnp.mean(out * out, axis=-1, keepdims=True)
        out = out * lax.rsqrt(ms + RMS_EPS) * fg_ref[...]
    o_ref[...] = out


def _combine(lay, x, mods, gates, ys, dest, final_g):
    tm = lay.tile(256)
    nt = lay.T // tm
    final = final_g is not None
    fg = (final_g if final else jnp.ones((D,), F32)).reshape(1, D)
    grid_spec = pltpu.PrefetchScalarGridSpec(
        num_scalar_prefetch=1,
        grid=(nt,),
        in_specs=[pl.BlockSpec((tm, D), lambda i, d: (i, 0)),
                  pl.BlockSpec((tm, 2), lambda i, d: (i, 0)),
                  pl.BlockSpec((None, 1, D), lambda i, d: (lay.group(i * tm) * MOD_CHUNKS + 5, 0, 0)),
                  pl.BlockSpec((1, D), lambda i, d: (0, 0)),
                  pl.BlockSpec(memory_space=pl.ANY)],
        out_specs=pl.BlockSpec((tm, D), lambda i, d: (i, 0)),
        scratch_shapes=[pltpu.VMEM((2, 2, tm, D), F32), pltpu.SemaphoreType.DMA((2,))],
    )
    return pl.pallas_call(
        functools.partial(_combine_kernel, tm, nt, final),
        out_shape=jax.ShapeDtypeStruct((lay.T, D), F32),
        grid_spec=grid_spec,
        compiler_params=_cp("arbitrary"),
        name="combine",
    )(dest, x, gates, mods, fg, ys)


def _moe_layer(lay, x, mods, router_w, w_in, w_out, final_g=None, bm=MOE_BM):
    h, idx, gates, rank, counts = _router(lay, x, mods, router_w)
    dest, fill, block_e, nvalid, nb = _moe_plan(idx, rank, counts, bm)
    xs = _dispatch(lay, h, dest, fill, nb, bm)
    ys = _experts(xs, block_e, nvalid, nb, bm, w_in, w_out)
    return _combine(lay, x, mods, gates, ys, dest, final_g)


def kernel(x_prompt, x_sample, c, state_l0_s5_re, state_l0_s5_im, state_l2_gla, state_l3_ret, c_ctx, l0_mod_w, l0_mod_b, l0_s5_a_re, l0_s5_a_im, l0_s5_log_dt, l0_s5_b_re, l0_s5_b_im, l0_s5_c_re, l0_s5_c_im, l0_s5_d, l0_s5_glu_w, l0_ffn_w_in, l0_ffn_w_out, l1_mod_w, l1_mod_b, l1_hy_w_in, l1_hy_b_in, l1_hy_short_w, l1_hy_short_b, l1_hy_f_w1, l1_hy_f_b1, l1_hy_f_w2, l1_hy_f_b2, l1_hy_f_w3, l1_hy_f_freq, l1_hy_skip, l1_hy_w_out, l1_hy_b_out, l1_moe_router, l1_moe_w_in, l1_moe_w_out, l2_mod_w, l2_mod_b, l2_gla_w_in, l2_gla_gate_w1, l2_gla_gate_w2, l2_gla_gate_b, l2_gla_norm_g, l2_gla_w_out, l2_ffn_w_in, l2_ffn_w_out, l3_mod_w, l3_mod_b, l3_ret_w_in, l3_ret_log_decay, l3_ret_norm_g, l3_ret_w_out, l3_moe_router, l3_moe_w_in, l3_moe_w_out, final_norm_g):
    B, L, _ = x_prompt.shape
    NS, LS, _ = x_sample.shape
    lay = Layout(B, L, NS, LS)
    x = jnp.concatenate([x_prompt.reshape(B * L, D), x_sample.reshape(NS * LS, D)], axis=0)
    cond = jnp.concatenate([c_ctx[None], c, jnp.zeros((8 - 1 - NS, D), F32)], axis=0)
    mods0 = _mods(cond, l0_mod_w, l0_mod_b)
    p0 = dict(a_re=l0_s5_a_re, a_im=l0_s5_a_im, log_dt=l0_s5_log_dt, b_re=l0_s5_b_re, b_im=l0_s5_b_im,
              c_re=l0_s5_c_re, c_im=l0_s5_c_im, d=l0_s5_d, glu_w=l0_s5_glu_w.astype(BF16))
    x, s5_re, s5_im = _s5_layer(lay, x, mods0, p0, state_l0_s5_re, state_l0_s5_im)
    x = _ffn(lay, x, mods0, l0_ffn_w_in.astype(BF16), l0_ffn_w_out.astype(BF16))

    mods1 = _mods(cond, l1_mod_w, l1_mod_b)
    p1 = dict(w_in=l1_hy_w_in, b_in=l1_hy_b_in, short_w=l1_hy_short_w, short_b=l1_hy_short_b,
              f_w1=l1_hy_f_w1, f_b1=l1_hy_f_b1, f_w2=l1_hy_f_w2, f_b2=l1_hy_f_b2, f_w3=l1_hy_f_w3,
              f_freq=l1_hy_f_freq, skip=l1_hy_skip, w_out=l1_hy_w_out, b_out=l1_hy_b_out)
    x = _hyena_layer(lay, x, mods1, p1)
    x = _moe_layer(lay, x, mods1, l1_moe_router, l1_moe_w_in.astype(BF16), l1_moe_w_out.astype(BF16))

    mods2 = _mods(cond, l2_mod_w, l2_mod_b)
    p2 = dict(w_in=l2_gla_w_in, gate_w1=l2_gla_gate_w1, gate_w2=l2_gla_gate_w2, gate_b=l2_gla_gate_b,
              norm_g=l2_gla_norm_g, w_out=l2_gla_w_out)
    x, gla_state = _gla_layer(lay, x, mods2, p2, state_l2_gla)
    x = _ffn(lay, x, mods2, l2_ffn_w_in.astype(BF16), l2_ffn_w_out.astype(BF16))

    mods3 = _mods(cond, l3_mod_w, l3_mod_b)
    p3 = dict(w_in=l3_ret_w_in, log_decay=l3_ret_log_decay, norm_g=l3_ret_norm_g, w_out=l3_ret_w_out)
    x, ret_state = _ret_layer(lay, x, mods3, p3, state_l3_ret)
    y = _moe_layer(lay, x, mods3, l3_moe_router, l3_moe_w_in.astype(BF16), l3_moe_w_out.astype(BF16),
                   final_g=final_norm_g)
    return (y[:lay.TP].reshape(B, L, D), y[lay.TP:].reshape(NS, LS, D), s5_re, s5_im, gla_state, ret_state)
```

```python
import functools
import math

import jax
import jax.numpy as jnp
import numpy as np
from jax import lax
from jax.experimental import pallas as pl
from jax.experimental.pallas import tpu as pltpu

F32 = jnp.float32
BF16 = jnp.bfloat16
HIGHEST = lax.Precision.HIGHEST

D = 1024
RMS_EPS = 1e-6
MOD_CHUNKS = 6
GRID_W = 64

S5_Q = 16
S5_G = D // S5_Q
S5_P = 64
S5_T = 16
S5_SCAN_ROWS = 64

HY_BANDS = 16
HY_TARGET = 1e-2
HY_FAST_PCT = 0.3
HY_SLOW_PCT = 1.5

GLA_H, GLA_DK, GLA_DV = 4, 128, 256
GLA_RANK = 16
GLA_TAU = 16.0
RET_H, RET_DK, RET_DV = 4, 256, 512
CHUNK = 64
RET_CHUNK = 256
ROPE_BASE = 10000.0

FFN_DIM = 2816
N_EXPERTS = 8
EXPERT_DIM = 3584

VMEM_LIMIT_V7X = 56 * 1024 * 1024


def _cp(*sem):
    return pltpu.CompilerParams(dimension_semantics=sem, vmem_limit_bytes=VMEM_LIMIT_V7X)


def _silu(x):
    return x * jax.nn.sigmoid(x)


def _modulate(x, shift, scale):
    ms = jnp.mean(x * x, axis=-1, keepdims=True)
    return x * lax.rsqrt(ms + RMS_EPS) * (1.0 + scale) + shift


class Layout:
    def __init__(self, n_prompt, l_prompt, n_sample, l_sample):
        self.B, self.L, self.NS, self.LS = n_prompt, l_prompt, n_sample, l_sample
        self.TP = n_prompt * l_prompt
        self.T = self.TP + n_sample * l_sample

    def tile(self, want):
        t = math.gcd(math.gcd(self.TP, self.LS), want)
        assert t % 8 == 0
        return t

    def group(self, row):
        return jnp.where(row < self.TP, 0, 1 + (row - self.TP) // self.LS)


def _mod_spec(lay, tm, chunk, ngrid):
    def imap(*ids):
        return (lay.group(ids[0] * tm) * MOD_CHUNKS + chunk, 0, 0)
    del ngrid
    return pl.BlockSpec((None, 1, D), imap)


def _mods_kernel(c_ref, w_ref, b_ref, o_ref):
    o_ref[...] = jnp.dot(_silu(c_ref[...]), w_ref[...], precision=HIGHEST,
                         preferred_element_type=F32) + b_ref[...]


def _mods(cond, w, b):
    n = MOD_CHUNKS * D
    tn = 1536
    out = pl.pallas_call(
        _mods_kernel,
        out_shape=jax.ShapeDtypeStruct((8, n), F32),
        grid=(n // tn,),
        in_specs=[pl.BlockSpec((8, D), lambda j: (0, 0)),
                  pl.BlockSpec((D, tn), lambda j: (0, j)),
                  pl.BlockSpec((1, tn), lambda j: (0, j))],
        out_specs=pl.BlockSpec((8, tn), lambda j: (0, j)),
        compiler_params=_cp("arbitrary"),
        name="mods",
    )(cond, w, b.reshape(1, n))
    return out.reshape(8 * MOD_CHUNKS, 1, D)


def _modulate_kernel(x_ref, sh_ref, sc_ref, o_ref):
    o_ref[...] = _modulate(x_ref[...], sh_ref[...], sc_ref[...]).astype(o_ref.dtype)


def _modulate_call(lay, x, mods, c_shift, c_scale, dtype):
    tm = lay.tile(512)
    return pl.pallas_call(
        _modulate_kernel,
        out_shape=jax.ShapeDtypeStruct((lay.T, D), dtype),
        grid=(lay.T // tm,),
        in_specs=[pl.BlockSpec((tm, D), lambda i: (i, 0)),
                  _mod_spec(lay, tm, c_shift, 1), _mod_spec(lay, tm, c_scale, 1)],
        out_specs=pl.BlockSpec((tm, D), lambda i: (i, 0)),
        compiler_params=_cp("arbitrary"),
        name="modulate",
    )(x, mods, mods)


def _proj_kernel(x_ref, sh_ref, sc_ref, w_ref, b_ref, o_ref, u_sc):
    @pl.when(pl.program_id(1) == 0)
    def _():
        u_sc[...] = _modulate(x_ref[...], sh_ref[...], sc_ref[...]).astype(BF16)

    acc = jnp.dot(u_sc[...], w_ref[...], preferred_element_type=F32) + b_ref[...]
    o_ref[...] = acc.astype(o_ref.dtype)


def _proj(lay, x, mods, c_shift, c_scale, w, b, tn, out_dtype=BF16, tm=1024):
    tm = lay.tile(tm)
    n = w.shape[1]
    assert n % tn == 0
    resident = dict(pipeline_mode=pl.Buffered(1)) if tn == n else {}
    return pl.pallas_call(
        _proj_kernel,
        out_shape=jax.ShapeDtypeStruct((lay.T, n), out_dtype),
        grid=(lay.T // tm, n // tn),
        in_specs=[pl.BlockSpec((tm, D), lambda i, j: (i, 0)),
                  _mod_spec(lay, tm, c_shift, 2), _mod_spec(lay, tm, c_scale, 2),
                  pl.BlockSpec((D, tn), lambda i, j: (0, j), **resident),
                  pl.BlockSpec((1, tn), lambda i, j: (0, j))],
        out_specs=pl.BlockSpec((tm, tn), lambda i, j: (i, j)),
        scratch_shapes=[pltpu.VMEM((tm, D), BF16)],
        compiler_params=_cp("arbitrary", "arbitrary"),
        name="proj",
    )(x, mods, mods, w, b)


def _ffn_kernel(x_ref, sh_ref, sc_ref, g_ref, wa_ref, wb_ref, wo_ref, o_ref):
    x = x_ref[...]
    u = _modulate(x, sh_ref[...], sc_ref[...]).astype(BF16)
    a = jnp.dot(u, wa_ref[...], preferred_element_type=F32)
    b = jnp.dot(u, wb_ref[...], preferred_element_type=F32)
    h = (_silu(a) * b).astype(BF16)
    o_ref[...] = x + g_ref[...] * jnp.dot(h, wo_ref[...], preferred_element_type=F32)


def _ffn(lay, x, mods, w_in, w_out):
    tm = lay.tile(512)
    once = dict(pipeline_mode=pl.Buffered(1))
    return pl.pallas_call(
        _ffn_kernel,
        out_shape=jax.ShapeDtypeStruct((lay.T, D), F32),
        grid=(lay.T // tm,),
        in_specs=[pl.BlockSpec((tm, D), lambda i: (i, 0)),
                  _mod_spec(lay, tm, 3, 1), _mod_spec(lay, tm, 4, 1), _mod_spec(lay, tm, 5, 1),
                  pl.BlockSpec((D, FFN_DIM), lambda i: (0, 0), **once),
                  pl.BlockSpec((D, FFN_DIM), lambda i: (0, 1), **once),
                  pl.BlockSpec((FFN_DIM, D), lambda i: (0, 0), **once)],
        out_specs=pl.BlockSpec((tm, D), lambda i: (i, 0)),
        compiler_params=_cp("arbitrary"),
        name="ffn",
    )(x, mods, mods, mods, w_in, w_in, w_out)


def _s5_tables(a_re, a_im, log_dt, b_re, b_im, c_re, c_im, d_skip):
    T, G, P, Q = S5_T, S5_G, S5_P, S5_Q
    a = lax.complex(a_re, a_im)
    adt = a * jnp.exp(log_dt)[..., None]
    lam = jnp.exp(adt)
    bb = ((lam - 1.0) / a)[..., None] * lax.complex(b_re, b_im)
    cm = lax.complex(c_re, c_im)
    steps = jnp.arange(T + 1, dtype=F32)
    pw = jnp.exp(steps[None, :, None, None] * adt[:, None])
    kern = jnp.real(jnp.einsum('dgqp,djgp,dgpr->djgqr', cm, pw[:, :T], bb))
    lag = jnp.arange(T)[:, None, None]
    s_i = jnp.arange(T)[None, :, None]
    t_i = jnp.arange(T)[None, None, :]
    place = jnp.stack([t_i - s_i == lag, s_i - t_i == lag]).astype(F32)
    m = jnp.einsum('djst,djgqr->gsrtq', place, kern, precision=HIGHEST)
    eye = (jnp.eye(T)[:, None, :, None] * jnp.eye(Q)[None, :, None, :])
    m = m + eye[None] * d_skip.reshape(G, 1, 1, 1, Q)
    m = m.reshape(G, T * Q, T * Q)
    e_f = pw[0][T - 1 - jnp.arange(T)]
    e_b = pw[1][jnp.arange(T)]
    n_f = e_f[..., None] * bb[0][None]
    n_b = e_b[..., None] * bb[1][None]
    n_c = jnp.concatenate([n_f, n_b], axis=2)
    n_c = jnp.transpose(n_c, (1, 0, 3, 2)).reshape(G, T * Q, 2 * P)
    lam_t = jnp.concatenate([pw[0][T], pw[1][T]], axis=-1)
    w_f = cm[0][:, None] * jnp.transpose(pw[0][1:T + 1], (1, 0, 2))[:, :, None, :]
    w_b = cm[1][:, None] * jnp.transpose(pw[1][T - jnp.arange(T)], (1, 0, 2))[:, :, None, :]
    w_f = jnp.transpose(w_f, (0, 3, 1, 2)).reshape(G, P, T * Q)
    w_b = jnp.transpose(w_b, (0, 3, 1, 2)).reshape(G, P, T * Q)
    z = jnp.zeros_like(jnp.real(w_f))
    c_mats = dict(c_f_re=jnp.concatenate([jnp.real(w_f), z], axis=1),
                  c_f_im=jnp.concatenate([-jnp.imag(w_f), z], axis=1),
                  c_b_re=jnp.concatenate([z, jnp.real(w_b)], axis=1),
                  c_b_im=jnp.concatenate([z, -jnp.imag(w_b)], axis=1))
    return dict(m=m.astype(BF16), n_re=jnp.real(n_c).astype(BF16), n_im=jnp.imag(n_c).astype(BF16),
                l_re=jnp.real(lam_t), l_im=jnp.imag(lam_t), **{k: v.astype(BF16) for k, v in c_mats.items()})


def _s5_in_kernel(u_ref, m_ref, nre_ref, nim_ref, yi_ref, sre_ref, sim_ref):
    u = u_ref[...]
    yi_ref[...] = jnp.dot(u, m_ref[...], preferred_element_type=F32)
    sre_ref[...] = jnp.dot(u, nre_ref[...], preferred_element_type=F32).reshape(sre_ref.shape)
    sim_ref[...] = jnp.dot(u, nim_ref[...], preferred_element_type=F32).reshape(sim_ref.shape)


def _s5_in(ug, tabs):
    G, R, W = ug.shape
    P2 = 2 * S5_P
    RB = S5_SCAN_ROWS
    assert R % RB == 0
    gspec = lambda n: pl.BlockSpec((None, W, n), lambda g: (g, 0, 0))
    rspec = pl.BlockSpec((None, R, W), lambda g: (g, 0, 0))
    sspec = pl.BlockSpec((R // RB, RB, P2), lambda g: (0, g, 0))
    sshape = jax.ShapeDtypeStruct((R // RB, G * RB, P2), F32)
    return pl.pallas_call(
        _s5_in_kernel,
        out_shape=(jax.ShapeDtypeStruct((G, R, W), F32), sshape, sshape),
        grid=(G,),
        in_specs=[rspec, gspec(W), gspec(P2), gspec(P2)],
        out_specs=(rspec, sspec, sspec),
        compiler_params=_cp("arbitrary"),
        name="s5_in",
    )(ug, tabs['m'], tabs['n_re'], tabs['n_im'])


def _s5_scan_kernel(nsb, ncb, nblk, sref_ref, simf_ref, sreb_ref, simb_ref, lre_ref, lim_ref,
                    h0re_ref, h0im_ref, *rest):
    hfre_ref, hfim_ref, hbre_ref, hbim_ref, fre_ref, fim_ref, cre_sc, cim_sc = rest[4:]
    P = S5_P
    rows = sref_ref.shape[0] // ncb
    j = pl.program_id(1)
    fwd = lax.broadcasted_iota(jnp.int32, (1, 2 * P), 1) < P
    lre = lre_ref[...]
    lim = lim_ref[...]

    @pl.when(j == 0)
    def _():
        cre_sc[...] = h0re_ref[...]
        cim_sc[...] = h0im_ref[...]

    def at(k):
        return pl.ds(k, rows, stride=ncb)

    def body(k, carry):
        hre, him = carry
        kb = ncb - 1 - k
        hfre_ref[at(k), :] = hre
        hfim_ref[at(k), :] = him
        hbre_ref[at(kb), :] = hre
        hbim_ref[at(kb), :] = him
        sre = jnp.where(fwd, sref_ref[at(k), :], sreb_ref[at(kb), :])
        sim = jnp.where(fwd, simf_ref[at(k), :], simb_ref[at(kb), :])
        return (lre * hre - lim * him + sre, lre * him + lim * hre + sim)

    hre, him = lax.fori_loop(0, ncb, body, (cre_sc[...], cim_sc[...]), unroll=4)
    cre_sc[...] = hre
    cim_sc[...] = him

    @pl.when(j == nblk - 1)
    def _():
        fre_ref[...] = hre
        fim_ref[...] = him


def _s5_scan(sre, sim, tabs, h0re, h0im, hprev, row0, nseq, nc, nsb, nblk):
    _, grb, P2 = sre.shape
    rb = S5_SCAN_ROWS
    G = grb // rb
    assert nblk == 1 or nsb == 1
    ncb = nc // nblk
    assert nsb * ncb == rb and row0 % rb == 0 and nseq % nsb == 0 and nc % nblk == 0
    b0 = row0 // rb
    fspec = pl.BlockSpec((None, grb, P2), lambda i, j: (b0 + i * nblk + j, 0, 0))
    bspec = pl.BlockSpec((None, grb, P2), lambda i, j: (b0 + i * nblk + nblk - 1 - j, 0, 0))
    lspec = pl.BlockSpec((G * nsb, P2), lambda i, j: (0, 0))
    qspec = pl.BlockSpec((None, G * nsb, P2), lambda i, j: (i, 0, 0))
    anyspec = pl.BlockSpec(memory_space=pl.ANY)
    fin = jax.ShapeDtypeStruct((nseq // nsb, G * nsb, P2), F32)
    rep = lambda a: jnp.repeat(a, nsb, axis=0)
    flat = lambda a: a.reshape(nseq // nsb, G * nsb, P2)
    outs = pl.pallas_call(
        functools.partial(_s5_scan_kernel, nsb, ncb, nblk),
        out_shape=tuple(jax.ShapeDtypeStruct(h.shape, h.dtype) for h in hprev) + (fin, fin),
        grid=(nseq // nsb, nblk),
        in_specs=[fspec, fspec, bspec, bspec, lspec, lspec, qspec, qspec] + [anyspec] * 4,
        out_specs=(fspec, fspec, bspec, bspec, qspec, qspec),
        scratch_shapes=[pltpu.VMEM((G * nsb, P2), F32), pltpu.VMEM((G * nsb, P2), F32)],
        input_output_aliases={8: 0, 9: 1, 10: 2, 11: 3},
        compiler_params=_cp("arbitrary", "arbitrary"),
        name="s5_scan",
    )(sre, sim, sre, sim, rep(tabs['l_re']), rep(tabs['l_im']), flat(h0re), flat(h0im), *hprev)
    return outs[:4], outs[4].reshape(nseq // nsb, G, nsb, P2), outs[5].reshape(nseq // nsb, G, nsb, P2)


def _s5_out_kernel(yi_ref, hfre_ref, hfim_ref, hbre_ref, hbim_ref, cfre_ref, cfim_ref, cbre_ref, cbim_ref,
                   y_ref):
    y = yi_ref[...]
    for h_ref, c_ref in ((hfre_ref, cfre_ref), (hfim_ref, cfim_ref), (hbre_ref, cbre_ref), (hbim_ref, cbim_ref)):
        h = h_ref[...].reshape(y.shape[0], h_ref.shape[-1])
        y += jnp.dot(h.astype(BF16), c_ref[...], preferred_element_type=F32)
    y_ref[...] = y.astype(y_ref.dtype)


def _s5_out(yi, hprev, tabs):
    G, R, W = yi.shape
    P2 = 2 * S5_P
    RB = S5_SCAN_ROWS
    gspec = pl.BlockSpec((None, P2, W), lambda g: (g, 0, 0))
    hspec = pl.BlockSpec((R // RB, RB, P2), lambda g: (0, g, 0))
    rspec = pl.BlockSpec((None, R, W), lambda g: (g, 0, 0))
    return pl.pallas_call(
        _s5_out_kernel,
        out_shape=jax.ShapeDtypeStruct((G, R, W), F32),
        grid=(G,),
        in_specs=[rspec] + [hspec] * 4 + [gspec] * 4,
        out_specs=rspec,
        compiler_params=_cp("arbitrary"),
        name="s5_out",
    )(yi, *hprev, tabs['c_f_re'], tabs['c_f_im'], tabs['c_b_re'], tabs['c_b_im'])


LANES = 128
S5_GB = LANES // S5_Q


def _block_transpose(sets):
    blk = lax.broadcasted_iota(jnp.int32, sets[0][0].shape, 1) // S5_Q
    sets = [list(regs) for regs in sets]
    d = S5_GB // 2
    while d:
        keep = (blk & d) == 0
        for regs in sets:
            for i in range(S5_GB):
                if i & d:
                    continue
                a, b = regs[i], regs[i + d]
                regs[i] = jnp.where(keep, a, pltpu.roll(b, d * S5_Q, 1))
                regs[i + d] = jnp.where(keep, pltpu.roll(a, LANES - d * S5_Q, 1), b)
        d //= 2
    return sets


def _s5_pre_kernel(tm, x_ref, sh_ref, sc_ref, ug_ref, u_sc):
    u = _modulate(x_ref[...], sh_ref[...], sc_ref[...])
    for j in range(D // LANES):
        u_sc[j] = u[:, j * LANES:(j + 1) * LANES]
    rows16 = 16
    nh = S5_T // S5_GB
    for c in range(tm // (S5_T * rows16)):
        base = c * S5_T * rows16
        for j in range(D // LANES):
            sets = [[u_sc[j, pl.ds(base + h * S5_GB + s, rows16, stride=S5_T), :] for s in range(S5_GB)]
                    for h in range(nh)]
            for h, regs in enumerate(_block_transpose(sets)):
                for gl, t in enumerate(regs):
                    ug_ref[j * S5_GB + gl, c * rows16:(c + 1) * rows16, h * LANES:(h + 1) * LANES] = t.astype(BF16)


def _s5_pre(lay, x, mods):
    tm = lay.tile(512)
    assert tm % (S5_T * 16) == 0
    return pl.pallas_call(
        functools.partial(_s5_pre_kernel, tm),
        out_shape=jax.ShapeDtypeStruct((S5_G, lay.T // S5_T, S5_T * S5_Q), BF16),
        grid=(lay.T // tm,),
        in_specs=[pl.BlockSpec((tm, D), lambda i: (i, 0)), _mod_spec(lay, tm, 0, 1), _mod_spec(lay, tm, 1, 1)],
        out_specs=pl.BlockSpec((S5_G, tm // S5_T, S5_T * S5_Q), lambda i: (0, i, 0)),
        scratch_shapes=[pltpu.VMEM((D // LANES, tm, LANES), F32)],
        compiler_params=_cp("arbitrary"),
        name="s5_pre",
    )(x, mods, mods)


def _s5_glu_kernel(tm, x_ref, yg_ref, g_ref, wv_ref, wg_ref, o_ref, a_sc, y_sc):
    @pl.when(pl.program_id(1) == 0)
    def _():
        def sub_tile(c, carry):
            crow = pl.ds(pl.multiple_of(c * 8, 8), 8)
            base = c * (S5_T * 8)
            nh = S5_T // S5_GB
            for j2 in range(0, D // LANES, 2):
                keys = [(j, h) for j in (j2, j2 + 1) for h in range(nh)]
                sets = [[yg_ref[j * S5_GB + gl, crow, h * LANES:(h + 1) * LANES] for gl in range(S5_GB)]
                        for j, h in keys]
                for (j, h), regs in zip(keys, _block_transpose(sets)):
                    for s, t in enumerate(regs):
                        y_sc[j, pl.ds(base + h * S5_GB + s, 8, stride=S5_T), :] = t
            return carry

        lax.fori_loop(0, tm // (S5_T * 8), sub_tile, 0)
        for j in range(D // LANES):
            a_sc[:, j * LANES:(j + 1) * LANES] = jax.nn.gelu(y_sc[j]).astype(BF16)

    a = a_sc[...]
    val = jnp.dot(a, wv_ref[...], preferred_element_type=F32)
    gate = jnp.dot(a, wg_ref[...], preferred_element_type=F32)
    o_ref[...] = x_ref[...] + g_ref[...] * (val * jax.nn.sigmoid(gate))


def _s5_glu(lay, x, yg, mods, glu_w):
    tm = lay.tile(512)
    assert tm % (S5_T * 8) == 0
    tn = D
    nn = D // tn
    once = dict(pipeline_mode=pl.Buffered(1))
    return pl.pallas_call(
        functools.partial(_s5_glu_kernel, tm),
        out_shape=jax.ShapeDtypeStruct((lay.T, D), F32),
        grid=(lay.T // tm, nn),
        in_specs=[pl.BlockSpec((tm, tn), lambda i, j: (i, j)),
                  pl.BlockSpec((S5_G, tm // S5_T, S5_T * S5_Q), lambda i, j: (0, i, 0)),
                  pl.BlockSpec((None, 1, tn), lambda i, j: (lay.group(i * tm) * MOD_CHUNKS + 2, 0, j)),
                  pl.BlockSpec((D, tn), lambda i, j: (0, j), **once),
                  pl.BlockSpec((D, tn), lambda i, j: (0, nn + j), **once)],
        out_specs=pl.BlockSpec((tm, tn), lambda i, j: (i, j)),
        scratch_shapes=[pltpu.VMEM((tm, D), BF16), pltpu.VMEM((D // LANES, tm, LANES), F32)],
        compiler_params=_cp("arbitrary", "arbitrary"),
        name="s5_glu",
    )(x, yg, mods, glu_w, glu_w)


def _s5_layer(lay, x, mods, p, h0_re, h0_im):
    T, G, P, Q = S5_T, S5_G, S5_P, S5_Q
    tabs = _s5_tables(p['a_re'], p['a_im'], p['log_dt'], p['b_re'], p['b_im'], p['c_re'], p['c_im'], p['d'])
    R = lay.T // T
    ug = _s5_pre(lay, x, mods)
    yi, sre, sim = _s5_in(ug, tabs)
    hprev = tuple(jnp.zeros(sre.shape, F32) for _ in range(4))
    ncp, ncs = lay.L // T, lay.LS // T
    nsb = S5_SCAN_ROWS // ncp
    zero = jnp.zeros((lay.B // nsb, G, nsb, 2 * P), F32)
    hprev, fre, fim = _s5_scan(sre, sim, tabs, zero, zero, hprev, 0, lay.B, ncp, nsb, 1)
    to_lanes = lambda s: jnp.transpose(s, (0, 2, 1, 3)).reshape(lay.NS, G, 1, 2 * P)
    hprev, _, _ = _s5_scan(sre, sim, tabs, to_lanes(h0_re), to_lanes(h0_im), hprev,
                           lay.TP // T, lay.NS, ncs, 1, max(1, ncs // S5_SCAN_ROWS))
    yg = _s5_out(yi, hprev, tabs)
    x = _s5_glu(lay, x, yg, mods, p['glu_w'])
    from_lanes = lambda s: jnp.transpose(s, (0, 2, 1, 3)).reshape(lay.B, G, 2, P).transpose(0, 2, 1, 3)
    return x, from_lanes(fre), from_lanes(fim)


def _hyena_filters(L, p):
    mm = functools.partial(jnp.matmul, precision=HIGHEST)
    f = jnp.linspace(1e-4, HY_BANDS - 1, HY_BANDS, dtype=F32)[None, :]
    max_decay = math.log(HY_TARGET) / HY_FAST_PCT
    min_decay = math.log(HY_TARGET) / HY_SLOW_PCT
    deltas = jnp.abs(jnp.linspace(min_decay, max_decay, D, dtype=F32))
    w3 = p['f_w3'].reshape(-1, 2, 2, D)

    def side(pos, s):
        t = (pos.astype(F32) / (L - 1))[:, None]
        w = 2.0 * math.pi * pos.astype(F32)[:, None] / L
        feats = jnp.concatenate([t, jnp.cos(f * w), -jnp.sin(f * w)], axis=-1)
        z = jnp.sin(p['f_freq'][0] * (mm(feats, p['f_w1']) + p['f_b1']))
        z = jnp.sin(p['f_freq'][1] * (mm(z, p['f_w2']) + p['f_b2']))
        win = jnp.exp(-t * deltas)
        return jnp.stack([mm(z, w3[:, o, s]) * win for o in range(2)])

    j = jnp.arange(L, dtype=jnp.int32)
    k_lo = side(j, 0)
    k_hi = side((L - j) % L, 1) * (j > 0).astype(F32)[None, :, None]
    norm = jnp.sum(jnp.abs(k_lo), axis=1, keepdims=True) + jnp.sum(jnp.abs(k_hi), axis=1, keepdims=True)
    k_lo, k_hi = k_lo / norm, k_hi / norm
    alt = (1.0 - 2.0 * (j % 2).astype(F32))[None, :, None]
    k_ny = jnp.sum(alt * (k_lo + k_hi), axis=1) / (2 * L)
    return k_lo, k_hi, k_ny


def _dft_tables(L):
    r = math.isqrt(L)
    assert r * r == L
    t = jnp.arange(L, dtype=jnp.int32)[None, :]
    a = jnp.arange(r, dtype=jnp.int32)[:, None]

    def unit(idx):
        ang = (idx % (2 * L)).astype(F32) * (math.pi / L)
        return jnp.cos(ang), jnp.sin(ang)

    c1, s1 = unit(a * r * t)
    c2, s2 = unit(a * t)
    c1, s1, c2, s2 = c1[:, None], s1[:, None], c2[None], s2[None]
    cos = (c1 * c2 - s1 * s2).reshape(L, L)
    sin = (s1 * c2 + c1 * s2).reshape(L, L)
    return cos.astype(BF16), sin.astype(BF16)


def _hy_spec_kernel(L, tr, c_ref, s_ref, klo_ref, khi_ref, p_ref, q_ref):
    r = pl.program_id(2)
    f = r * tr + lax.broadcasted_iota(jnp.int32, (tr, 1), 0)
    sgn = (1 - 2 * (f % 2)).astype(F32)
    scale = jnp.where(f == 0, 1.0, 2.0) * (1.0 / (2 * L))
    c, s = c_ref[...], s_ref[...]
    lo, hi = klo_ref[...], khi_ref[...]
    dot = functools.partial(jnp.dot, preferred_element_type=F32)
    p_ref[...] = scale * (dot(c, lo) + sgn * dot(c, hi))
    q_ref[...] = scale * (dot(s, lo) + sgn * dot(s, hi))


def _hy_spectrum(L, cos, sin, k_lo, k_hi):
    tr = min(L, 512)
    tc = 512
    kspec = pl.BlockSpec((None, L, tc), lambda o, j, r: (o, 0, j))
    tspec = pl.BlockSpec((tr, L), lambda o, j, r: (r, 0))
    ospec = pl.BlockSpec((None, tr, tc), lambda o, j, r: (o, r, j))
    return pl.pallas_call(
        functools.partial(_hy_spec_kernel, L, tr),
        out_shape=(jax.ShapeDtypeStruct((2, L, D), F32), jax.ShapeDtypeStruct((2, L, D), F32)),
        grid=(2, D // tc, L // tr),
        in_specs=[tspec, tspec, kspec, kspec],
        out_specs=(ospec, ospec),
        compiler_params=_cp("arbitrary", "arbitrary", "arbitrary"),
        name="hy_spectrum",
    )(cos, sin, k_lo.astype(BF16), k_hi.astype(BF16))


def _hy_core_kernel(L, tr, ngrp, tc, x1_ref, x2_ref, v_ref, sw1_ref, sw2_ref, swv_ref, sb1_ref, sb2_ref,
                    sbv_ref, c_ref, s_ref, p_ref, q_ref, kny_ref, skip_ref, *rest):
    o_ref, z0_sc, z1_sc, x2_sc, a_sc, b_sc, ny0_sc, ny1_sc = rest[-8:]
    fused = tr == L
    ph = pl.program_id(2)
    r = pl.program_id(3)
    W = ngrp * tc

    def phase(k):
        return (lambda f: f()) if fused else pl.when(ph == k)
    tcv = min(L, 512)
    halo = 16

    def alt_sign(start, n):
        t = start + lax.broadcasted_iota(jnp.int32, (n, 1), 0)
        return (1 - 2 * (t % 2)).astype(F32)

    def conv3(src_ref, g, a, w_ref, b_ref):
        x = src_ref[g, pl.ds(a, tcv), :].astype(F32)
        row = lax.broadcasted_iota(jnp.int32, (tcv, 1), 0)
        up_at = pl.multiple_of(jnp.maximum(a - halo, 0), halo)
        dn_at = pl.multiple_of(jnp.minimum(a + tcv, L - halo), halo)
        up = src_ref[g, pl.ds(up_at, halo), :][halo - 1:halo, :].astype(F32)
        dn = src_ref[g, pl.ds(dn_at, halo), :][0:1, :].astype(F32)
        up = jnp.where(a > 0, up, 0.0)
        dn = jnp.where(a + tcv < L, dn, 0.0)
        prev = jnp.where(row == 0, up, pltpu.roll(x, 1, 0))
        nxt = jnp.where(row == tcv - 1, dn, pltpu.roll(x, tcv - 1, 0))
        return prev * w_ref[0:1, :] + x * w_ref[1:2, :] + nxt * w_ref[2:3, :] + b_ref[...]

    @(phase(0) if fused else pl.when(jnp.logical_and(ph == 0, r == 0)))
    def _():
        ny0_sc[...] = jnp.zeros_like(ny0_sc)

        def conv_tile(ti, carry):
            a = pl.multiple_of(ti * tcv, tcv)
            rows_a = pl.ds(a, tcv)
            for g in range(ngrp):
                cols = slice(g * tc, (g + 1) * tc)
                z1_sc[rows_a, cols] = conv3(x1_ref, g, a, sw1_ref, sb1_ref).astype(BF16)
                x2_sc[rows_a, cols] = conv3(x2_ref, g, a, sw2_ref, sb2_ref).astype(BF16)
                z0_sc[rows_a, cols] = conv3(v_ref, g, a, swv_ref, sbv_ref).astype(BF16)
            ny0_sc[...] += jnp.sum(alt_sign(a, tcv) * z0_sc[rows_a, :].astype(F32), axis=0, keepdims=True)
            return carry

        lax.fori_loop(0, L // tcv, conv_tile, 0)

    start = pl.multiple_of(r * tr, tr)
    rows = pl.ds(start, tr)
    dot = functools.partial(jnp.dot, preferred_element_type=F32)

    def forward(order, z_sc):
        z = z_sc[...]
        zre = dot(c_ref[...], z)
        zim = dot(s_ref[...], z)
        pw = jnp.concatenate([p_ref[order] if fused else p_ref[...]] * ngrp, axis=1)
        qw = jnp.concatenate([q_ref[order] if fused else q_ref[...]] * ngrp, axis=1)
        a_sc[rows, :] = (zre * pw - zim * qw).astype(BF16)
        b_sc[rows, :] = (zim * pw + zre * qw).astype(BF16)

    def inverse(order, z_sc, ny_sc):
        y = dot(c_ref[...], a_sc[...]) + dot(s_ref[...], b_sc[...])
        kny = jnp.concatenate([kny_ref[order:order + 1, :]] * ngrp, axis=1)
        skip = jnp.concatenate([skip_ref[order:order + 1, :]] * ngrp, axis=1)
        return y + alt_sign(start, tr) * (ny_sc[...] * kny) + skip * z_sc[rows, :].astype(F32)

    @phase(0)
    def _():
        forward(0, z0_sc)

    @phase(1)
    def _():
        z1 = (z1_sc[rows, :].astype(F32) * inverse(0, z0_sc, ny0_sc)).astype(BF16)
        z1_sc[rows, :] = z1

        @pl.when(r == 0)
        def _():
            ny1_sc[...] = jnp.zeros_like(ny1_sc)

        ny1_sc[...] += jnp.sum(alt_sign(start, tr) * z1.astype(F32), axis=0, keepdims=True)

    @phase(2)
    def _():
        forward(1, z1_sc)

    @phase(3)
    def _():
        out = x2_sc[rows, :].astype(F32) * inverse(1, z1_sc, ny1_sc)
        for g in range(ngrp):
            o_ref[g, rows, :] = out[:, g * tc:(g + 1) * tc].astype(o_ref.dtype)


def _hy_core(proj, short_w, short_b, skip, cos, sin, pq, k_ny, o_prev, row0, nseq, L, ngrp, tc):
    T = proj.shape[0]
    tr = min(L, 256)
    nrt = L // tr
    assert row0 % (L * ngrp) == 0 and nseq % ngrp == 0 and T % L == 0
    sb0 = row0 // (L * ngrp)
    nct = D // tc
    p3 = proj.reshape(T // L, L, 3 * D)
    p_arr, q_arr = pq

    def xspec(part):
        mode = {} if nrt == 1 else dict(pipeline_mode=pl.Buffered(1))
        return pl.BlockSpec((ngrp, L, tc), lambda i, j, ph, r: (sb0 + i, 0, part * nct + j), **mode)

    def wspec(part, rows_):
        return pl.BlockSpec((rows_, tc), lambda i, j, ph, r: (0, part * nct + j))

    fused = nrt == 1
    if fused:
        pq_spec = pl.BlockSpec((2, tr, tc), lambda i, j, ph, r: (0, 0, j))
    else:
        pq_spec = pl.BlockSpec((None, tr, tc),
                               lambda i, j, ph, r: (ph // 2, jnp.where(ph % 2 == 0, r, nrt - 1), j))

    tspec = pl.BlockSpec((tr, L), lambda i, j, ph, r: (r, 0))
    in_specs = [xspec(0), xspec(1), xspec(2), wspec(0, 3), wspec(1, 3), wspec(2, 3),
                wspec(0, 1), wspec(1, 1), wspec(2, 1), tspec, tspec, pq_spec, pq_spec,
                pl.BlockSpec((2, tc), lambda i, j, ph, r: (0, j)),
                pl.BlockSpec((2, tc), lambda i, j, ph, r: (0, j))]
    sb = short_b.reshape(1, 3 * D)
    args = [p3, p3, p3, short_w, short_w, short_w, sb, sb, sb, cos, sin, p_arr, q_arr, k_ny, skip]
    aliases = {}
    if o_prev is not None:
        in_specs.append(pl.BlockSpec(memory_space=pl.ANY))
        args.append(o_prev.reshape(T // L, L, D))
        aliases = {len(args) - 1: 0}
    W = ngrp * tc
    out = pl.pallas_call(
        functools.partial(_hy_core_kernel, L, tr, ngrp, tc),
        out_shape=jax.ShapeDtypeStruct((T // L, L, D), BF16),
        grid=(nseq // ngrp, nct, 1 if fused else 4, nrt),
        in_specs=in_specs,
        out_specs=pl.BlockSpec((ngrp, L, tc), lambda i, j, ph, r: (sb0 + i, 0, j)),
        scratch_shapes=[pltpu.VMEM((L, W), BF16)] * 5 + [pltpu.VMEM((1, W), F32)] * 2,
        input_output_aliases=aliases,
        compiler_params=_cp("arbitrary", "arbitrary", "arbitrary", "arbitrary"),
        name="hy_core",
    )(*args)
    return out.reshape(T, D)


FOLD_BLK = 256


def _hyena_filters_folded(L, p):
    H = L // 2
    mm = functools.partial(jnp.matmul, precision=HIGHEST)
    f = jnp.linspace(1e-4, HY_BANDS - 1, HY_BANDS, dtype=F32)[None, :]
    max_decay = math.log(HY_TARGET) / HY_FAST_PCT
    min_decay = math.log(HY_TARGET) / HY_SLOW_PCT
    deltas = jnp.abs(jnp.linspace(min_decay, max_decay, D, dtype=F32))
    w3 = p['f_w3'].reshape(-1, 2, 2, D)

    def side(pos, s):
        t = (pos.astype(F32) / (L - 1))[:, None]
        w = 2.0 * math.pi * pos.astype(F32)[:, None] / L
        feats = jnp.concatenate([t, jnp.cos(f * w), -jnp.sin(f * w)], axis=-1)
        z = jnp.sin(p['f_freq'][0] * (mm(feats, p['f_w1']) + p['f_b1']))
        z = jnp.sin(p['f_freq'][1] * (mm(z, p['f_w2']) + p['f_b2']))
        win = jnp.exp(-t * deltas)
        return jnp.stack([mm(z, w3[:, o, s]) * win for o in range(2)])

    t = jnp.arange(H, dtype=jnp.int32)
    pos = jnp.concatenate([t, (L - t) % L])
    mid = jnp.full((1,), H, jnp.int32)
    live = (t > 0).astype(F32)[None, :, None]
    s0 = side(pos, 0).reshape(2, 2, H, D)
    s1 = side(pos, 1).reshape(2, 2, H, D)
    klo_lo, klo_hr, klo_h = s0[:, 0], s0[:, 1] * live, side(mid, 0)[:, 0]
    khi_lo, khi_hr, khi_h = s1[:, 1] * live, s1[:, 0] * live, side(mid, 1)[:, 0]
    norm = sum(jnp.sum(jnp.abs(a), axis=1) for a in (klo_lo, klo_hr, khi_lo, khi_hr)) \
        + jnp.abs(klo_h) + jnp.abs(khi_h)
    alt = (1.0 - 2.0 * (t % 2).astype(F32))[None, :, None]
    alt_h = 1.0 - 2.0 * (H % 2)
    k_ny = (jnp.sum(alt * (klo_lo + klo_hr + khi_lo + khi_hr), axis=1) + alt_h * (klo_h + khi_h)) / norm / (2 * L)
    inv = (1.0 / norm)[:, None, :]
    p_lo, p_hr = (klo_lo + khi_lo) * inv, (klo_hr + khi_hr) * inv
    m_lo, m_hr = (klo_lo - khi_lo) * inv, (klo_hr - khi_hr) * inv
    xc = jnp.stack([p_lo + p_hr, m_lo - m_hr])
    xs = jnp.stack([p_lo - p_hr, m_lo + m_hr])
    xh = jnp.stack([(klo_h + khi_h) / norm, (klo_h - khi_h) / norm])
    return xc, xs, xh, k_ny


def _dft_tables_folded(L):
    H = L // 2
    ra = 1 << (int(math.log2(H)) // 2)
    rb = H // ra
    th = math.pi / L
    a = jnp.arange(ra, dtype=jnp.int32)[:, None]
    b = jnp.arange(rb, dtype=jnp.int32)[:, None]
    u = jnp.arange(H, dtype=jnp.int32)[None, :]

    def unit(idx):
        ang = (idx % (2 * L)).astype(F32) * th
        return jnp.cos(ang), jnp.sin(ang)

    e1 = [unit(2 * rb * a * u), unit(2 * rb * a * u), unit((2 * u + 1) * rb * a)]
    e2 = [unit(2 * b * u), unit((2 * b + 1) * u), unit((2 * u + 1) * b)]
    c1 = jnp.stack([e[0] for e in e1])[:, None, :, None, :]
    s1 = jnp.stack([e[1] for e in e1])[:, None, :, None, :]
    c2 = jnp.stack([e[0] for e in e2])[:, None, None, :, :]
    s2 = jnp.stack([e[1] for e in e2])[:, None, None, :, :]
    lead = jnp.concatenate([c1, s1], axis=1)
    cross = jnp.concatenate([-s1, c1], axis=1)
    return (lead * c2 + cross * s2).reshape(6, H, H).astype(BF16)


def _hy_spec_folded_kernel(L, tr, c_ref, s_ref, xc_ref, xs_ref, xh_ref, p_ref, q_ref):
    half = pl.program_id(2)
    r = pl.program_id(3)
    m = r * tr + lax.broadcasted_iota(jnp.int32, (tr, 1), 0)
    alt = (1 - 2 * (m % 2)).astype(F32)
    scale = jnp.where(jnp.logical_and(half == 0, m == 0), 1.0, 2.0) * (1.0 / (2 * L))
    mid = alt * xh_ref[...]
    kc = jnp.dot(c_ref[...], xc_ref[...], preferred_element_type=F32) + jnp.where(half == 0, mid, 0.0)
    ks = jnp.dot(s_ref[...], xs_ref[...], preferred_element_type=F32) + jnp.where(half == 1, mid, 0.0)
    p_ref[...] = scale * kc
    q_ref[...] = scale * ks


def _hy_spectrum_folded(L, tabs, xc, xs, xh):
    H = L // 2
    tr = min(H, 512)
    tc = 512
    nrt = H // tr
    xspec = pl.BlockSpec((None, None, H, tc), lambda o, j, hf, r: (hf, o, 0, j))
    hspec = pl.BlockSpec((None, None, 1, tc), lambda o, j, hf, r: (hf, o, 0, j))
    cspec = pl.BlockSpec((None, tr, H), lambda o, j, hf, r: (2 * hf, r, 0))
    sspec = pl.BlockSpec((None, tr, H), lambda o, j, hf, r: (2 * hf + 1, r, 0))
    ospec = pl.BlockSpec((None, tr, tc), lambda o, j, hf, r: (o, hf * nrt + r, j))
    return pl.pallas_call(
        functools.partial(_hy_spec_folded_kernel, L, tr),
        out_shape=(jax.ShapeDtypeStruct((2, L, D), F32), jax.ShapeDtypeStruct((2, L, D), F32)),
        grid=(2, D // tc, 2, nrt),
        in_specs=[cspec, sspec, xspec, xspec, hspec],
        out_specs=(ospec, ospec),
        compiler_params=_cp("arbitrary", "arbitrary", "arbitrary", "arbitrary"),
        name="hy_spectrum",
    )(tabs, tabs, xc.astype(BF16), xs.astype(BF16), xh[:, :, None, :])


def _hy_fold_kernel(L, tr, ngrp, tc, x1_ref, x2_ref, v_ref, sw1_ref, sw2_ref, swv_ref, sb1_ref, sb2_ref,
                    sbv_ref, ce_ref, se_ref, co_ref, so_ref, pe_ref, po_ref, qe_ref, qo_ref, kny_ref, skip_ref,
                    *rest):
    o_ref, z0_sc, z1_sc, x2_sc, a_sc, b_sc, mid_sc = rest[-7:]
    H = L // 2
    nrt = H // tr
    nb = H // FOLD_BLK
    W = ngrp * tc
    ph = pl.program_id(2)
    r = pl.program_id(3)
    tcv = min(L, 512)
    halo = 16
    dot = functools.partial(jnp.dot, preferred_element_type=F32)
    ZH, X1H, X2H, YSP, NY, NYACC = range(6)
    alt_h = 1.0 - 2.0 * (H % 2)

    def alt_sign(start, n):
        t = start + lax.broadcasted_iota(jnp.int32, (n, 1), 0)
        return (1 - 2 * (t % 2)).astype(F32)

    def conv3(src_ref, g, a, w_ref, b_ref):
        x = src_ref[g, pl.ds(a, tcv), :].astype(F32)
        row = lax.broadcasted_iota(jnp.int32, (tcv, 1), 0)
        up_at = pl.multiple_of(jnp.maximum(a - halo, 0), halo)
        dn_at = pl.multiple_of(jnp.minimum(a + tcv, L - halo), halo)
        up = src_ref[g, pl.ds(up_at, halo), :][halo - 1:halo, :].astype(F32)
        dn = src_ref[g, pl.ds(dn_at, halo), :][0:1, :].astype(F32)
        up = jnp.where(a > 0, up, 0.0)
        dn = jnp.where(a + tcv < L, dn, 0.0)
        prev = jnp.where(row == 0, up, pltpu.roll(x, 1, 0))
        nxt = jnp.where(row == tcv - 1, dn, pltpu.roll(x, tcv - 1, 0))
        return prev * w_ref[0:1, :] + x * w_ref[1:2, :] + nxt * w_ref[2:3, :] + b_ref[...]

    def flip_mats():
        u = lax.broadcasted_iota(jnp.int32, (FOLD_BLK, FOLD_BLK), 0)
        v = lax.broadcasted_iota(jnp.int32, (FOLD_BLK, FOLD_BLK), 1)
        jshift = jnp.where(jnp.logical_and(u >= 1, v == FOLD_BLK - u), 1.0, 0.0).astype(BF16)
        e0 = jnp.where(jnp.logical_and(u == 0, v == 0), 1.0, 0.0).astype(BF16)
        return jshift, e0

    def flipped_block(src, b, jshift, e0):
        blk = lambda k: src[H + k * FOLD_BLK:H + (k + 1) * FOLD_BLK, :]
        out = dot(jshift, blk(nb - 1 - b))
        if b >= 1:
            out = out + dot(e0, blk(nb - b))
        return out

    def fold_in_place(sc, stage, off, jshift, e0):
        for b in range(nb):
            stage[off + b * FOLD_BLK:off + (b + 1) * FOLD_BLK, :] = flipped_block(sc, b, jshift, e0).astype(BF16)

        def copy_back(b, carry):
            at = pl.multiple_of(b * FOLD_BLK, FOLD_BLK)
            sc[pl.ds(H + at, FOLD_BLK), :] = stage[pl.ds(off + at, FOLD_BLK), :]
            return carry

        lax.fori_loop(0, nb, copy_back, 0)

    def to_operands(sc):
        def tile(i, acc):
            a = pl.multiple_of(i * FOLD_BLK, FOLD_BLK)
            ra, rb = pl.ds(a, FOLD_BLK), pl.ds(pl.multiple_of(H + a, FOLD_BLK), FOLD_BLK)
            lo = sc[ra, :].astype(F32)
            hr = sc[rb, :].astype(F32)
            zs = (lo + hr).astype(BF16)
            sc[ra, :] = zs
            sc[rb, :] = (lo - hr).astype(BF16)
            return acc + jnp.sum(alt_sign(a, FOLD_BLK) * zs.astype(F32), axis=0, keepdims=True)

        return lax.fori_loop(0, nb, tile, jnp.zeros((1, W), F32))

    def row_of(sc, at):
        return sc[at:at + halo, :][0:1, :].astype(F32)

    @pl.when(jnp.logical_and(ph == 0, r == 0))
    def _():
        def conv_tile(ti, carry):
            a = pl.multiple_of(ti * tcv, tcv)
            rows_a = pl.ds(a, tcv)
            for g in range(ngrp):
                cols = slice(g * tc, (g + 1) * tc)
                z1_sc[rows_a, cols] = conv3(x1_ref, g, a, sw1_ref, sb1_ref).astype(BF16)
                x2_sc[rows_a, cols] = conv3(x2_ref, g, a, sw2_ref, sb2_ref).astype(BF16)
                z0_sc[rows_a, cols] = conv3(v_ref, g, a, swv_ref, sbv_ref).astype(BF16)
            return carry

        lax.fori_loop(0, L // tcv, conv_tile, 0)
        mid_sc[ZH:ZH + 1, :] = row_of(z0_sc, H)
        mid_sc[X1H:X1H + 1, :] = row_of(z1_sc, H)
        mid_sc[X2H:X2H + 1, :] = row_of(x2_sc, H)
        jshift, e0 = flip_mats()
        for sc, stage, off in ((z0_sc, a_sc, 0), (z1_sc, a_sc, H), (x2_sc, b_sc, 0)):
            fold_in_place(sc, stage, off, jshift, e0)
        mid_sc[NY:NY + 1, :] = to_operands(z0_sc) + alt_h * mid_sc[ZH:ZH + 1, :]
        mid_sc[YSP:YSP + 1, :] = jnp.zeros((1, W), F32)
        mid_sc[NYACC:NYACC + 1, :] = jnp.zeros((1, W), F32)

    start = pl.multiple_of(r * tr, tr)
    rows = pl.ds(start, tr)
    rows_hi = pl.ds(pl.multiple_of(H + r * tr, tr), tr)
    rep = lambda ref: jnp.concatenate([ref[...]] * ngrp, axis=1)
    rep_row = lambda ref, o: jnp.concatenate([ref[o:o + 1, :]] * ngrp, axis=1)

    def forward(z_sc):
        zs, zd = z_sc[0:H, :], z_sc[H:L, :]
        mid = alt_sign(start, tr) * mid_sc[ZH:ZH + 1, :]
        zre_e = dot(ce_ref[...], zs) + mid
        zim_e = dot(se_ref[...], zd)
        zre_o = dot(co_ref[...], zd)
        zim_o = dot(so_ref[...], zs) + mid
        pe, qe, po, qo = rep(pe_ref), rep(qe_ref), rep(po_ref), rep(qo_ref)
        a_e = zre_e * pe - zim_e * qe
        b_e = zim_e * pe + zre_e * qe
        a_o = zre_o * po - zim_o * qo
        b_o = zim_o * po + zre_o * qo
        a_sc[rows, :] = a_e.astype(BF16)
        b_sc[rows, :] = b_e.astype(BF16)
        a_sc[rows_hi, :] = a_o.astype(BF16)
        b_sc[rows_hi, :] = b_o.astype(BF16)
        mid_sc[YSP:YSP + 1, :] += jnp.sum(alt_sign(start, tr) * (a_e + b_o), axis=0, keepdims=True)

    def inverse(order, zin_sc):
        pe = dot(ce_ref[...], a_sc[0:H, :])
        qe = dot(se_ref[...], b_sc[0:H, :])
        po = dot(co_ref[...], a_sc[H:L, :])
        qo = dot(so_ref[...], b_sc[H:L, :])
        c = alt_sign(start, tr) * (mid_sc[NY:NY + 1, :] * rep_row(kny_ref, order))
        half_skip = 0.5 * rep_row(skip_ref, order)
        zs = zin_sc[rows, :].astype(F32)
        zd = zin_sc[rows_hi, :].astype(F32)
        y_lo = pe + qe + po + qo + c + half_skip * (zs + zd)
        y_hr = pe - qe - po + qo + c + half_skip * (zs - zd)
        return y_lo, y_hr

    def middle(order):
        return (mid_sc[YSP:YSP + 1, :] + alt_h * mid_sc[NY:NY + 1, :] * rep_row(kny_ref, order)
                + rep_row(skip_ref, order) * mid_sc[ZH:ZH + 1, :])

    @pl.when(ph == 0)
    def _():
        forward(z0_sc)

    @pl.when(ph == 1)
    def _():
        y_lo, y_hr = inverse(0, z0_sc)
        z_lo = (z1_sc[rows, :].astype(F32) * y_lo).astype(BF16)
        z_hr = (z1_sc[rows_hi, :].astype(F32) * y_hr).astype(BF16)
        z1_sc[rows, :] = z_lo
        z1_sc[rows_hi, :] = z_hr
        mid_sc[NYACC:NYACC + 1, :] += jnp.sum(alt_sign(start, tr) * (z_lo.astype(F32) + z_hr.astype(F32)),
                                              axis=0, keepdims=True)

        @pl.when(r == nrt - 1)
        def _():
            z1h = (mid_sc[X1H:X1H + 1, :] * middle(0)).astype(BF16).astype(F32)
            mid_sc[ZH:ZH + 1, :] = z1h
            mid_sc[NY:NY + 1, :] = mid_sc[NYACC:NYACC + 1, :] + alt_h * z1h
            mid_sc[YSP:YSP + 1, :] = jnp.zeros((1, W), F32)

    @pl.when(ph == 2)
    def _():
        @pl.when(r == 0)
        def _():
            to_operands(z1_sc)

        forward(z1_sc)

    @pl.when(ph == 3)
    def _():
        y_lo, y_hr = inverse(1, z1_sc)
        z0_sc[rows, :] = (x2_sc[rows, :].astype(F32) * y_lo).astype(BF16)
        z0_sc[rows_hi, :] = (x2_sc[rows_hi, :].astype(F32) * y_hr).astype(BF16)

        @pl.when(r == nrt - 1)
        def _():
            out_h = mid_sc[X2H:X2H + 1, :] * middle(1)
            jshift, e0 = flip_mats()
            first = (lax.broadcasted_iota(jnp.int32, (FOLD_BLK, 1), 0) == 0).astype(F32)
            for b in range(nb):
                up = flipped_block(z0_sc, b, jshift, e0)
                if b == 0:
                    up = up + first * out_h
                for g in range(ngrp):
                    cols = slice(g * tc, (g + 1) * tc)
                    o_ref[g, H + b * FOLD_BLK:H + (b + 1) * FOLD_BLK, :] = up[:, cols].astype(o_ref.dtype)
            for b in range(nb):
                lo_rows = slice(b * FOLD_BLK, (b + 1) * FOLD_BLK)
                for g in range(ngrp):
                    o_ref[g, lo_rows, :] = z0_sc[lo_rows, g * tc:(g + 1) * tc]


def _hy_core_folded(proj, short_w, short_b, skip, tabs, pq, k_ny, o_prev, row0, nseq, L, ngrp, tc):
    T = proj.shape[0]
    H = L // 2
    tr = min(H, 256)
    nrt = H // tr
    assert H % FOLD_BLK == 0 and row0 % (L * ngrp) == 0 and nseq % ngrp == 0 and T % L == 0
    sb0 = row0 // (L * ngrp)
    nct = D // tc
    p3 = proj.reshape(T // L, L, 3 * D)
    p_arr, q_arr = pq
    once = dict(pipeline_mode=pl.Buffered(1))

    def xspec(part):
        return pl.BlockSpec((ngrp, L, tc), lambda i, j, ph, r: (sb0 + i, 0, part * nct + j), **once)

    def wspec(part, rows_):
        return pl.BlockSpec((rows_, tc), lambda i, j, ph, r: (0, part * nct + j))

    def tspec(k_fwd, k_inv):
        return pl.BlockSpec((None, tr, H), lambda i, j, ph, r: (jnp.where(ph % 2 == 0, k_fwd, k_inv), r, 0))

    def pqspec(half):
        def imap(i, j, ph, r):
            return (ph // 2, half * nrt + jnp.where(ph % 2 == 0, r, nrt - 1), j)
        return pl.BlockSpec((None, tr, tc), imap)

    in_specs = [xspec(0), xspec(1), xspec(2), wspec(0, 3), wspec(1, 3), wspec(2, 3),
                wspec(0, 1), wspec(1, 1), wspec(2, 1),
                tspec(0, 0), tspec(1, 1), tspec(2, 4), tspec(3, 5),
                pqspec(0), pqspec(1), pqspec(0), pqspec(1),
                pl.BlockSpec((2, tc), lambda i, j, ph, r: (0, j)),
                pl.BlockSpec((2, tc), lambda i, j, ph, r: (0, j))]
    sb = short_b.reshape(1, 3 * D)
    args = [p3, p3, p3, short_w, short_w, short_w, sb, sb, sb, tabs, tabs, tabs, tabs,
            p_arr, p_arr, q_arr, q_arr, k_ny, skip]
    in_specs.append(pl.BlockSpec(memory_space=pl.ANY))
    args.append(o_prev.reshape(T // L, L, D))
    aliases = {len(args) - 1: 0}
    W = ngrp * tc
    out = pl.pallas_call(
        functools.partial(_hy_fold_kernel, L, tr, ngrp, tc),
        out_shape=jax.ShapeDtypeStruct((T // L, L, D), BF16),
        grid=(nseq // ngrp, nct, 4, nrt),
        in_specs=in_specs,
        out_specs=pl.BlockSpec((ngrp, L, tc), lambda i, j, ph, r: (sb0 + i, 0, j)),
        scratch_shapes=[pltpu.VMEM((L, W), BF16)] * 5 + [pltpu.VMEM((8, W), F32)],
        input_output_aliases=aliases,
        compiler_params=_cp("arbitrary", "arbitrary", "arbitrary", "arbitrary"),
        name="hy_core",
    )(*args)
    return out.reshape(T, D)


def _plain_out_kernel(x_ref, z_ref, g_ref, w_ref, b_ref, o_ref):
    acc = jnp.dot(z_ref[...], w_ref[...], preferred_element_type=F32) + b_ref[...]
    o_ref[...] = x_ref[...] + g_ref[...] * acc


def _plain_out(lay, x, mods, z, w, b):
    tm = lay.tile(512)
    tn = D
    kdim = z.shape[1]
    return pl.pallas_call(
        _plain_out_kernel,
        out_shape=jax.ShapeDtypeStruct((lay.T, D), F32),
        grid=(lay.T // tm, D // tn),
        in_specs=[pl.BlockSpec((tm, tn), lambda i, j: (i, j)),
                  pl.BlockSpec((tm, kdim), lambda i, j: (i, 0)),
                  pl.BlockSpec((None, 1, tn), lambda i, j: (lay.group(i * tm) * MOD_CHUNKS + 2, 0, j)),
                  pl.BlockSpec((kdim, tn), lambda i, j: (0, j), pipeline_mode=pl.Buffered(1)),
                  pl.BlockSpec((1, tn), lambda i, j: (0, j))],
        out_specs=pl.BlockSpec((tm, tn), lambda i, j: (i, j)),
        compiler_params=_cp("arbitrary", "arbitrary"),
        name="plain_out",
    )(x, z, mods, w, b.reshape(1, D))


def _hyena_layer(lay, x, mods, p):
    proj = _proj(lay, x, mods, 0, 1, p['w_in'].astype(BF16), p['b_in'].reshape(1, 3 * D), 3 * D, tm=512)
    z = jnp.zeros((lay.T, D), BF16)
    for row0, nseq, L, ngrp, tc in ((0, lay.B, lay.L, math.gcd(lay.B, 4), 256),
                                    (lay.TP, lay.NS, lay.LS, lay.NS, 256)):
        conv = (proj, p['short_w'], p['short_b'], p['skip'])
        if (L // 2) % FOLD_BLK == 0 and L > 2 * FOLD_BLK:
            xc, xs, xh, k_ny = _hyena_filters_folded(L, p)
            tabs = _dft_tables_folded(L)
            pq = _hy_spectrum_folded(L, tabs, xc, xs, xh)
            z = _hy_core_folded(*conv, tabs, pq, k_ny, z, row0, nseq, L, ngrp, tc)
        else:
            k_lo, k_hi, k_ny = _hyena_filters(L, p)
            cos, sin = _dft_tables(L)
            pq = _hy_spectrum(L, cos, sin, k_lo, k_hi)
            z = _hy_core(*conv, cos, sin, pq, k_ny, z, row0, nseq, L, ngrp, tc)
    return _plain_out(lay, x, mods, z, p['w_out'].astype(BF16), p['b_out'])


_NT = (((1,), (1,)), ((), ()))
_TN = (((0,), (0,)), ((), ()))


def _tri(dr):
    t = lax.broadcasted_iota(jnp.int32, (CHUNK, CHUNK), 0)
    s = lax.broadcasted_iota(jnp.int32, (CHUNK, CHUNK), 1)
    return (s <= t) if dr == 0 else (s >= t)


def _chunk_cumsum(g, dr):
    n = g.shape[0]
    pos = lax.broadcasted_iota(jnp.int32, g.shape, 0) % CHUNK
    sh = 1
    while sh < CHUNK:
        if dr == 0:
            g = g + jnp.where(pos >= sh, pltpu.roll(g, sh, 0), 0.0)
        else:
            g = g + jnp.where(pos < CHUNK - sh, pltpu.roll(g, n - sh, 0), 0.0)
        sh *= 2
    return g


def _head_epilogue(o_sc, gate_ref, ng_ref, a_ref, center):
    rows = o_sc.shape[0]
    tr = math.gcd(rows, 256)

    def tile(i, carry):
        r = pl.ds(pl.multiple_of(i * tr, tr), tr)
        o = o_sc[r, :]
        if center:
            o = o - jnp.mean(o, axis=-1, keepdims=True)
        o = o * lax.rsqrt(jnp.mean(o * o, axis=-1, keepdims=True) + RMS_EPS) * ng_ref[...]
        a_ref[r, :] = (o * _silu(gate_ref[r, :].astype(F32))).astype(a_ref.dtype)
        return carry

    lax.fori_loop(0, rows // tr, tile, 0)


def _gla_kernel(cps, nseg, U, has_s0, want_final, *refs):
    q_ref, k_ref, v_ref, lr_ref, w2f_ref, w2b_ref, gb_ref, gate_ref, ng_ref = refs[:9]
    s0_ref = refs[9] if has_s0 else None
    qin_sc, kin_sc, kout_sc, dec_sc, st_sc, s_sc, s0t_sc, o_ref = refs[-8:]
    outs = refs[-10:-8] if want_final else refs[-9:-8]
    a_ref = outs[0]
    sf_ref = outs[1] if want_final else None
    C = CHUNK
    nsc = cps // U
    nchunks = nseg * cps
    rows_total = nchunks * C
    w2 = (w2f_ref, w2b_ref)

    for dr in range(2):
        pre = jnp.dot(lr_ref[...], w2[dr][...], preferred_element_type=F32) + gb_ref[dr:dr + 1, :]
        g = (jnp.minimum(pre, 0.0) - jnp.log(1.0 + jnp.exp(-jnp.abs(pre)))) * (1.0 / GLA_TAU)
        b = _chunk_cumsum(g, dr)
        b3 = b.reshape(nchunks, C, GLA_DK)
        tot = b3[:, C - 1:C, :] if dr == 0 else b3[:, 0:1, :]
        dec_sc[...] = jnp.exp(tot).reshape(nchunks, GLA_DK)
        k = k_ref[...].astype(F32)
        qin_sc[...] = (q_ref[...].astype(F32) * (GLA_DK ** -0.5) * jnp.exp(b)).astype(BF16)
        kin_sc[...] = (k * jnp.exp(-b)).astype(BF16)
        kout_sc[...] = (k * jnp.exp(tot - b3).reshape(rows_total, GLA_DK)).astype(BF16)
        if has_s0:
            s0t_sc[...] = jnp.transpose(s0_ref[dr], (1, 0))
        tri = _tri(dr)

        def super_chunk(jj, carry, dr=dr, tri=tri):
            j = jj if dr == 0 else nseg * nsc - 1 - jj
            in_seg = j % nsc
            first = (in_seg == 0) if dr == 0 else (in_seg == nsc - 1)
            last = (in_seg == nsc - 1) if dr == 0 else (in_seg == 0)

            @pl.when(first)
            def _():
                s_sc[...] = s0t_sc[...] if has_s0 else jnp.zeros_like(s_sc)

            base = j * (U * C)
            for u in range(U):
                rows = pl.ds(pl.multiple_of(base + u * C, C), C)
                v = v_ref[rows, :]
                sc = lax.dot_general(qin_sc[rows, :], kin_sc[rows, :], _NT, preferred_element_type=F32)
                o = jnp.dot(jnp.where(tri, sc, 0.0).astype(BF16), v, preferred_element_type=F32)
                st_sc[u] = lax.dot_general(v, kout_sc[rows, :], _TN, preferred_element_type=F32)
                if dr == 0:
                    o_ref[rows, :] = o
                else:
                    o_ref[rows, :] += o
            s = s_sc[...]
            for u in (range(U) if dr == 0 else reversed(range(U))):
                kv = st_sc[u]
                st_sc[u] = s
                s = dec_sc[pl.ds(j * U + u, 1), :] * s + kv
            s_sc[...] = s
            for u in range(U):
                rows = pl.ds(pl.multiple_of(base + u * C, C), C)
                o_ref[rows, :] += lax.dot_general(qin_sc[rows, :], st_sc[u].astype(BF16), _NT,
                                                  preferred_element_type=F32)
            if want_final:
                @pl.when(last)
                def _():
                    sf_ref[j // nsc, dr] = jnp.transpose(s, (1, 0))
            return carry

        lax.fori_loop(0, nseg * nsc, super_chunk, 0)

    _head_epilogue(o_ref, gate_ref, ng_ref, a_ref, center=False)


def _gla_core(proj, w2f, w2b, gate_b, norm_g, s0, o_prev, row0, nseq, seqlen, nseg, want_final):
    T = proj.shape[0]
    rows = nseg * seqlen
    cps = seqlen // CHUNK
    U = math.gcd(cps, 8)
    assert row0 % rows == 0 and seqlen % CHUNK == 0 and nseq % nseg == 0
    rb = row0 // rows
    hk = GLA_H * GLA_DK
    has_s0 = s0 is not None
    assert not has_s0 or nseg == 1
    in_specs = [pl.BlockSpec((rows, GLA_DK), lambda b, h: (rb + b, h)),
                pl.BlockSpec((rows, GLA_DK), lambda b, h: (rb + b, GLA_H + h)),
                pl.BlockSpec((rows, GLA_DV), lambda b, h: (rb + b, 2 * hk // GLA_DV + h)),
                pl.BlockSpec((rows, 128), lambda b, h: (rb + b, (2 * hk + 2 * GLA_H * GLA_DV) // 128)),
                pl.BlockSpec((128, GLA_DK), lambda b, h: (0, h)),
                pl.BlockSpec((128, GLA_DK), lambda b, h: (0, h)),
                pl.BlockSpec((2, GLA_DK), lambda b, h: (0, h)),
                pl.BlockSpec((rows, GLA_DV), lambda b, h: (rb + b, (2 * hk) // GLA_DV + GLA_H + h)),
                pl.BlockSpec((1, GLA_DV), lambda b, h: (0, 0))]
    args = [proj, proj, proj, proj, w2f, w2b, gate_b, proj, norm_g.reshape(1, GLA_DV)]
    if has_s0:
        in_specs.append(pl.BlockSpec((None, 2, None, GLA_DK, GLA_DV), lambda b, h: (b, 0, h, 0, 0)))
        args.append(s0)
    in_specs.append(pl.BlockSpec(memory_space=pl.ANY))
    args.append(o_prev)
    aliases = {len(args) - 1: 0}
    out_shape = [jax.ShapeDtypeStruct((T, GLA_H * GLA_DV), BF16)]
    out_specs = [pl.BlockSpec((rows, GLA_DV), lambda b, h: (rb + b, h))]
    if want_final:
        out_shape.append(jax.ShapeDtypeStruct((nseq, 2, GLA_H, GLA_DK, GLA_DV), F32))
        out_specs.append(pl.BlockSpec((nseg, 2, None, GLA_DK, GLA_DV), lambda b, h: (b, 0, h, 0, 0)))
    outs = pl.pallas_call(
        functools.partial(_gla_kernel, cps, nseg, U, has_s0, want_final),
        out_shape=tuple(out_shape),
        grid=(nseq // nseg, GLA_H),
        in_specs=in_specs,
        out_specs=tuple(out_specs),
        scratch_shapes=[pltpu.VMEM((rows, GLA_DK), BF16)] * 3
        + [pltpu.VMEM((nseg * cps, GLA_DK), F32), pltpu.VMEM((U, GLA_DV, GLA_DK), F32),
           pltpu.VMEM((GLA_DV, GLA_DK), F32), pltpu.VMEM((GLA_DV, GLA_DK), F32),
           pltpu.VMEM((rows, GLA_DV), F32)],
        input_output_aliases=aliases,
        compiler_params=_cp("arbitrary", "arbitrary"),
        name="gla",
    )(*args)
    return (outs[0], outs[1]) if want_final else (outs[0], None)


def _ret_kernel(cps, nseg, U, has_s0, want_final, rope, *refs):
    q_ref, k_ref, v_ref, dm_ref, qd_ref, kd_ref, cd_ref, gate_ref, ng_ref = refs[:9]
    nxt = 9
    if rope:
        cos_ref, sin_ref = refs[9:11]
        nxt = 11
    s0_ref = refs[nxt] if has_s0 else None
    qr_sc, kr_sc, qd_sc, kd_sc, st_sc, s_sc, o_ref = refs[-7:]
    outs = refs[-9:-7] if want_final else refs[-8:-7]
    a_ref = outs[0]
    sf_ref = outs[1] if want_final else None
    C = RET_CHUNK
    nsc = cps // U
    R = U * C
    SB = 64

    def rot(x, rows):
        if not rope:
            return x
        half = x.shape[1] // 2
        swapped = jnp.concatenate([pltpu.roll(x[:, :half], half // 2, 1),
                                   pltpu.roll(x[:, half:], half // 2, 1)], axis=1)
        return x * cos_ref[rows, :] + swapped * sin_ref[rows, :]

    for dr in range(2):
        def super_chunk(jj, carry, dr=dr):
            j = jj if dr == 0 else nseg * nsc - 1 - jj
            in_seg = j % nsc
            first = (in_seg == 0) if dr == 0 else (in_seg == nsc - 1)
            last = (in_seg == nsc - 1) if dr == 0 else (in_seg == 0)

            @pl.when(first)
            def _():
                s_sc[...] = s0_ref[dr] if has_s0 else jnp.zeros_like(s_sc)

            base = pl.multiple_of(j * R, R)
            rows_r = pl.ds(base, R)
            q = rot(q_ref[rows_r, :].astype(F32), rows_r)
            k = rot(k_ref[rows_r, :].astype(F32), rows_r) * (RET_DK ** -0.5)
            qr_sc[...] = q.astype(BF16)
            kr_sc[...] = k.astype(BF16)
            qd_sc[...] = (q.reshape(U, C, RET_DK) * qd_ref[dr][None]).reshape(R, RET_DK).astype(BF16)
            kd_sc[...] = (k.reshape(U, C, RET_DK) * kd_ref[dr][None]).reshape(R, RET_DK).astype(BF16)
            for u in range(U):
                loc = pl.ds(u * C, C)
                rows = pl.ds(pl.multiple_of(base + u * C, C), C)
                v = v_ref[rows, :]
                sc = lax.dot_general(qr_sc[loc, :], kr_sc[loc, :], _NT, preferred_element_type=F32)
                o = jnp.dot((sc * dm_ref[dr]).astype(BF16), v, preferred_element_type=F32)
                st_sc[u] = lax.dot_general(kd_sc[loc, :], v, _TN, preferred_element_type=F32)
                if dr == 0:
                    o_ref[rows, :] = o
                else:
                    o_ref[rows, :] += o
            cd = cd_ref[dr]
            for r0 in range(0, RET_DK, SB):
                srows = pl.ds(r0, SB)
                s = s_sc[srows, :]
                for u in (range(U) if dr == 0 else reversed(range(U))):
                    kv = st_sc[u, srows, :]
                    st_sc[u, srows, :] = s
                    s = cd * s + kv
                s_sc[srows, :] = s
            for u in range(U):
                rows = pl.ds(pl.multiple_of(base + u * C, C), C)
                o_ref[rows, :] += jnp.dot(qd_sc[pl.ds(u * C, C), :], st_sc[u].astype(BF16),
                                          preferred_element_type=F32)
            if want_final:
                @pl.when(last)
                def _():
                    sf_ref[j // nsc, dr] = s_sc[...]
            return carry

        lax.fori_loop(0, nseg * nsc, super_chunk, 0)

    _head_epilogue(o_ref, gate_ref, ng_ref, a_ref, center=True)


def _ret_tables(log_decay):
    C = RET_CHUNK
    lg = log_decay.astype(F32)[:, :, None, None]
    t = jnp.arange(C, dtype=F32)[:, None]
    s = jnp.arange(C, dtype=F32)[None, :]
    lag = jnp.stack([t - s, s - t])[:, None]
    dmask = jnp.where(lag >= 0, jnp.exp(jnp.maximum(lag, 0.0) * lg), 0.0)
    tl = jnp.arange(C, dtype=F32)[None, None, :, None]
    qdec = jnp.concatenate([jnp.exp((tl + 1.0) * lg[0:1]), jnp.exp((C - tl) * lg[1:2])], axis=0)
    kdec = jnp.concatenate([jnp.exp((C - 1.0 - tl) * lg[0:1]), jnp.exp(tl * lg[1:2])], axis=0)
    cdec = jnp.exp(C * lg)
    return dmask, qdec, kdec, cdec


def _rope_tables(seqlen, dk):
    half = dk // 2
    nf = half // 2
    pos = jnp.arange(seqlen, dtype=jnp.int32)
    inv = ROPE_BASE ** (-jnp.arange(nf, dtype=F32) / nf)
    ang_r = (pos // GRID_W).astype(F32)[:, None] * inv[None, :]
    ang_c = (pos % GRID_W).astype(F32)[:, None] * inv[None, :]
    cos = jnp.concatenate([jnp.cos(ang_r)] * 2 + [jnp.cos(ang_c)] * 2, axis=1)
    sin = jnp.concatenate([-jnp.sin(ang_r), jnp.sin(ang_r), -jnp.sin(ang_c), jnp.sin(ang_c)], axis=1)
    return cos, sin


def _ret_core(proj, tabs, norm_g, s0, o_prev, row0, nseq, seqlen, nseg, want_final, rope):
    T = proj.shape[0]
    rows = nseg * seqlen
    C = RET_CHUNK
    cps = seqlen // C
    U = math.gcd(cps, 4)
    assert row0 % rows == 0 and seqlen % C == 0 and nseq % nseg == 0
    rb = row0 // rows
    hk, hv = RET_H * RET_DK, RET_H * RET_DV
    has_s0 = s0 is not None
    assert not (has_s0 or rope) or nseg == 1
    tspec = lambda r, c: pl.BlockSpec((2, None, r, c), lambda b, h: (0, h, 0, 0))
    mode = dict(pipeline_mode=pl.Buffered(1)) if rows * RET_DV * 2 >= (4 << 20) else {}
    in_specs = [pl.BlockSpec((rows, RET_DK), lambda b, h: (rb + b, h), **mode),
                pl.BlockSpec((rows, RET_DK), lambda b, h: (rb + b, RET_H + h), **mode),
                pl.BlockSpec((rows, RET_DV), lambda b, h: (rb + b, 2 * hk // RET_DV + h), **mode),
                tspec(C, C), tspec(C, 1), tspec(C, 1), tspec(1, 1),
                pl.BlockSpec((rows, RET_DV), lambda b, h: (rb + b, (2 * hk + hv) // RET_DV + h), **mode),
                pl.BlockSpec((1, RET_DV), lambda b, h: (0, 0))]
    args = [proj, proj, proj, *tabs, proj, norm_g.reshape(1, RET_DV)]
    if rope:
        cos, sin = _rope_tables(seqlen, RET_DK)
        in_specs += [pl.BlockSpec((seqlen, RET_DK), lambda b, h: (0, 0), pipeline_mode=pl.Buffered(1))] * 2
        args += [cos, sin]
    if has_s0:
        in_specs.append(pl.BlockSpec((None, 2, None, RET_DK, RET_DV), lambda b, h: (b, 0, h, 0, 0)))
        args.append(s0)
    in_specs.append(pl.BlockSpec(memory_space=pl.ANY))
    args.append(o_prev)
    aliases = {len(args) - 1: 0}
    out_shape = [jax.ShapeDtypeStruct((T, hv), BF16)]
    out_specs = [pl.BlockSpec((rows, RET_DV), lambda b, h: (rb + b, h))]
    if want_final:
        out_shape.append(jax.ShapeDtypeStruct((nseq, 2, RET_H, RET_DK, RET_DV), F32))
        out_specs.append(pl.BlockSpec((nseg, 2, None, RET_DK, RET_DV), lambda b, h: (b, 0, h, 0, 0)))
    outs = pl.pallas_call(
        functools.partial(_ret_kernel, cps, nseg, U, has_s0, want_final, rope),
        out_shape=tuple(out_shape),
        grid=(nseq // nseg, RET_H),
        in_specs=in_specs,
        out_specs=tuple(out_specs),
        scratch_shapes=[pltpu.VMEM((U * C, RET_DK), BF16)] * 4
        + [pltpu.VMEM((U, RET_DK, RET_DV), F32), pltpu.VMEM((RET_DK, RET_DV), F32),
           pltpu.VMEM((rows, RET_DV), F32)],
        input_output_aliases=aliases,
        compiler_params=_cp("arbitrary", "arbitrary"),
        name="ret",
    )(*args)
    return (outs[0], outs[1]) if want_final else (outs[0], None)


def _gla_layer(lay, x, mods, p, s0):
    hk, hv = GLA_H * GLA_DK, GLA_H * GLA_DV
    w_all = jnp.concatenate([p['w_in'], p['gate_w1'][0], p['gate_w1'][1],
                             jnp.zeros((D, 128 - 2 * GLA_RANK), F32)], axis=1).astype(BF16)
    proj = _proj(lay, x, mods, 0, 1, w_all, jnp.zeros((1, w_all.shape[1]), F32), w_all.shape[1], tm=512)
    pad = lambda w, lo: jnp.pad(w, ((lo, 128 - GLA_RANK - lo), (0, 0))).astype(BF16)
    w2f, w2b = pad(p['gate_w2'][0], 0), pad(p['gate_w2'][1], GLA_RANK)
    a = jnp.zeros((lay.T, hv), BF16)
    a, s_fin = _gla_core(proj, w2f, w2b, p['gate_b'], p['norm_g'], None, a, 0, lay.B, lay.L,
                         math.gcd(lay.B, 8), True)
    a, _ = _gla_core(proj, w2f, w2b, p['gate_b'], p['norm_g'], s0, a, lay.TP, lay.NS, lay.LS, 1, False)
    x = _plain_out(lay, x, mods, a, p['w_out'].astype(BF16), jnp.zeros((D,), F32))
    return x, s_fin


def _ret_layer(lay, x, mods, p, s0):
    hk, hv = RET_H * RET_DK, RET_H * RET_DV
    proj = _proj(lay, x, mods, 0, 1, p['w_in'].astype(BF16), jnp.zeros((1, 2 * hk + 2 * hv), F32),
                 2 * hk + 2 * hv, tm=512)
    tabs = _ret_tables(p['log_decay'])
    a = jnp.zeros((lay.T, hv), BF16)
    a, s_fin = _ret_core(proj, tabs, p['norm_g'], None, a, 0, lay.B, lay.L, math.gcd(lay.B, 8), True, False)
    a, _ = _ret_core(proj, tabs, p['norm_g'], s0, a, lay.TP, lay.NS, lay.LS, 1, False, True)
    x = _plain_out(lay, x, mods, a, p['w_out'].astype(BF16), jnp.zeros((D,), F32))
    return x, s_fin


MOE_BM = 512
EXPERT_TF = 1792
ROUTER_LANES = 128
DMA_UNROLL = 8


def _router_kernel(x_ref, sh_ref, sc_ref, rw_ref, h_ref, idx_ref, gate_ref, rank_ref, cnt_ref):
    @pl.when(pl.program_id(0) == 0)
    def _():
        cnt_ref[...] = jnp.zeros_like(cnt_ref)

    h = _modulate(x_ref[...], sh_ref[...], sc_ref[...])
    h_ref[...] = h
    logits = jnp.dot(h, rw_ref[...], precision=HIGHEST, preferred_element_type=F32)
    lane = lax.broadcasted_iota(jnp.int32, logits.shape, 1)
    neg = jnp.float32(-jnp.inf)
    logits = jnp.where(lane < N_EXPERTS, logits, neg)
    m1 = jnp.max(logits, axis=-1, keepdims=True)
    i1 = jnp.min(jnp.where(logits == m1, lane, ROUTER_LANES), axis=-1, keepdims=True)
    rest = jnp.where(lane == i1, neg, logits)
    m2 = jnp.max(rest, axis=-1, keepdims=True)
    i2 = jnp.min(jnp.where(rest == m2, lane, ROUTER_LANES), axis=-1, keepdims=True)
    e2 = jnp.exp(m2 - m1)
    g1 = 1.0 / (1.0 + e2)
    idx_ref[:, 0:1] = i1
    idx_ref[:, 1:2] = i2
    gate_ref[:, 0:1] = g1
    gate_ref[:, 1:2] = e2 * g1
    tm = logits.shape[0]
    sel1 = lane == i1
    sel2 = lane == i2
    picked = jnp.where(jnp.logical_or(sel1, sel2), 1.0, 0.0)
    before = (lax.broadcasted_iota(jnp.int32, (tm, tm), 1)
              < lax.broadcasted_iota(jnp.int32, (tm, tm), 0)).astype(BF16)
    prior = jnp.dot(before, picked.astype(BF16), preferred_element_type=F32) + cnt_ref[...]
    rank_ref[:, 0:1] = jnp.sum(jnp.where(sel1, prior, 0.0), axis=-1, keepdims=True).astype(jnp.int32)
    rank_ref[:, 1:2] = jnp.sum(jnp.where(sel2, prior, 0.0), axis=-1, keepdims=True).astype(jnp.int32)
    cnt_ref[...] += jnp.sum(picked, axis=0, keepdims=True)


def _router(lay, x, mods, router_w):
    tm = lay.tile(512)
    rw = jnp.pad(router_w, ((0, 0), (0, ROUTER_LANES - N_EXPERTS)))
    return pl.pallas_call(
        _router_kernel,
        out_shape=(jax.ShapeDtypeStruct((lay.T, D), F32),
                   jax.ShapeDtypeStruct((lay.T, 2), jnp.int32),
                   jax.ShapeDtypeStruct((lay.T, 2), F32),
                   jax.ShapeDtypeStruct((lay.T, 2), jnp.int32),
                   jax.ShapeDtypeStruct((1, ROUTER_LANES), F32)),
        grid=(lay.T // tm,),
        in_specs=[pl.BlockSpec((tm, D), lambda i: (i, 0)),
                  _mod_spec(lay, tm, 3, 1), _mod_spec(lay, tm, 4, 1),
                  pl.BlockSpec((D, ROUTER_LANES), lambda i: (0, 0))],
        out_specs=(pl.BlockSpec((tm, D), lambda i: (i, 0)),
                   pl.BlockSpec((tm, 2), lambda i: (i, 0)),
                   pl.BlockSpec((tm, 2), lambda i: (i, 0)),
                   pl.BlockSpec((tm, 2), lambda i: (i, 0)),
                   pl.BlockSpec((1, ROUTER_LANES), lambda i: (0, 0))),
        compiler_params=_cp("arbitrary"),
        name="router",
    )(x, mods, mods, rw)


def _moe_plan(idx, rank, counts, bm):
    a = idx.size
    counts = counts[0, :N_EXPERTS].astype(jnp.int32)
    padded = (counts + bm - 1) // bm * bm
    pad_end = jnp.cumsum(padded)
    pad_start = pad_end - padded
    hit = idx[..., None] == jnp.arange(N_EXPERTS, dtype=jnp.int32)
    dest = (rank + jnp.sum(jnp.where(hit, pad_start, 0), axis=-1)).reshape(a).astype(jnp.int32)
    nb = -(-(a + N_EXPERTS * (bm - 1)) // bm)
    block_start = jnp.arange(nb, dtype=jnp.int32) * bm
    block_e = jnp.sum((block_start[:, None] >= pad_end[None, :]).astype(jnp.int32), axis=1)
    block_e = jnp.minimum(block_e, N_EXPERTS - 1)
    nvalid = (pad_end[-1] // bm).astype(jnp.int32).reshape(1)
    fill = jnp.concatenate([pad_start + counts, pad_end, nvalid]).astype(jnp.int32)
    return dest, fill, block_e, nvalid, nb


def _dispatch_kernel(tm, bm, nb, dest_ref, fill_ref, h_ref, xs_hbm, zero_sc, sem, zsem):
    i = pl.program_id(0)
    zr = zero_sc.shape[0]

    @pl.when(i == 0)
    def _():
        zero_sc[...] = jnp.zeros_like(zero_sc)

        def zero_row(r):
            return pltpu.make_async_copy(zero_sc.at[pl.ds(0, 1)], xs_hbm.at[pl.ds(r, 1)], zsem)

        def zero_rows(r):
            return pltpu.make_async_copy(zero_sc, xs_hbm.at[pl.ds(pl.multiple_of(r, zr), zr)], zsem)

        for e in range(N_EXPERTS):
            lo, hi = fill_ref[e], fill_ref[N_EXPERTS + e]
            lax.fori_loop(lo, hi, lambda r, c: (zero_row(r).start(), c)[1], 0)
            lax.fori_loop(lo, hi, lambda r, c: (zero_row(r).wait(), c)[1], 0)
        lo, hi = fill_ref[2 * N_EXPERTS] * (bm // zr), nb * (bm // zr)
        lax.fori_loop(lo, hi, lambda q, c: (zero_rows(q * zr).start(), c)[1], 0)
        lax.fori_loop(lo, hi, lambda q, c: (zero_rows(q * zr).wait(), c)[1], 0)

    def row_copy(r, dst):
        return pltpu.make_async_copy(h_ref.at[pl.ds(r, 1)], xs_hbm.at[pl.ds(dst, 1)], sem)

    def issue(r, carry):
        a = 2 * (i * tm + r)
        row_copy(r, dest_ref[a]).start(priority=0)
        row_copy(r, dest_ref[a + 1]).start(priority=1)
        return carry

    def drain(r, carry):
        row_copy(r, 0).wait()
        row_copy(r, 0).wait()
        return carry

    lax.fori_loop(0, tm, issue, 0, unroll=DMA_UNROLL)
    lax.fori_loop(0, tm, drain, 0, unroll=DMA_UNROLL)


def _dispatch(lay, h, dest, fill, nb, bm):
    tm = lay.tile(512)
    grid_spec = pltpu.PrefetchScalarGridSpec(
        num_scalar_prefetch=2,
        grid=(lay.T // tm,),
        in_specs=[pl.BlockSpec((tm, D), lambda i, d, f: (i, 0))],
        out_specs=pl.BlockSpec(memory_space=pl.ANY),
        scratch_shapes=[pltpu.VMEM((64, D), F32), pltpu.SemaphoreType.DMA(()), pltpu.SemaphoreType.DMA(())],
    )
    return pl.pallas_call(
        functools.partial(_dispatch_kernel, tm, bm, nb),
        out_shape=jax.ShapeDtypeStruct((nb * bm, D), F32),
        grid_spec=grid_spec,
        compiler_params=_cp("arbitrary"),
        name="dispatch",
    )(dest, fill, h)


def _experts_kernel(nf, be_ref, nv_ref, xs_ref, wa_ref, wb_ref, wo_ref, o_ref, xb_sc, acc_sc):
    i = pl.program_id(0)
    f = pl.program_id(1)
    valid = i < nv_ref[0]

    @pl.when(jnp.logical_and(valid, f == 0))
    def _():
        xb_sc[...] = xs_ref[...].astype(BF16)

    @pl.when(valid)
    def _():
        xb = xb_sc[...]
        a = jnp.dot(xb, wa_ref[...], preferred_element_type=F32)
        b = jnp.dot(xb, wb_ref[...], preferred_element_type=F32)
        h = (_silu(a) * b).astype(BF16)
        y = jnp.dot(h, wo_ref[...], preferred_element_type=F32)

        @pl.when(f == 0)
        def _():
            acc_sc[...] = y

        @pl.when(f > 0)
        def _():
            acc_sc[...] += y

    @pl.when(f == nf - 1)
    def _():
        o_ref[...] = jnp.where(valid, acc_sc[...], 0.0)


def _experts(xs, block_e, nvalid, nb, bm, w_in, w_out):
    tf = EXPERT_TF
    nf = EXPERT_DIM // tf

    def wmap(off):
        def imap(i, f, be, nv):
            fe = jnp.where(i < nv[0], f, nf - 1)
            return (be[i], 0, off + fe)
        return imap

    def womap(i, f, be, nv):
        fe = jnp.where(i < nv[0], f, nf - 1)
        return (be[i], fe, 0)

    grid_spec = pltpu.PrefetchScalarGridSpec(
        num_scalar_prefetch=2,
        grid=(nb, nf),
        in_specs=[pl.BlockSpec((bm, D), lambda i, f, be, nv: (jnp.minimum(i, nv[0] - 1), 0)),
                  pl.BlockSpec((None, D, tf), wmap(0)),
                  pl.BlockSpec((None, D, tf), wmap(nf)),
                  pl.BlockSpec((None, tf, D), womap)],
        out_specs=pl.BlockSpec((bm, D), lambda i, f, be, nv: (i, 0)),
        scratch_shapes=[pltpu.VMEM((bm, D), BF16), pltpu.VMEM((bm, D), F32)],
    )
    return pl.pallas_call(
        functools.partial(_experts_kernel, nf),
        out_shape=jax.ShapeDtypeStruct((nb * bm, D), F32),
        grid_spec=grid_spec,
        compiler_params=_cp("arbitrary", "arbitrary"),
        name="experts",
    )(block_e, nvalid, xs, w_in, w_in, w_out)


def _combine_kernel(tm, nt, final, dest_ref, x_ref, gate_ref, g_ref, fg_ref, ys_hbm, o_ref, y_sc, sem):
    i = pl.program_id(0)
    slot = i % 2

    def row_copy(s, k, r, src):
        return pltpu.make_async_copy(ys_hbm.at[pl.ds(src, 1)], y_sc.at[s, k, pl.ds(r, 1)], sem.at[s])

    def issue_tile(t, s):
        def issue(r, carry):
            a = 2 * (t * tm + r)
            row_copy(s, 0, r, dest_ref[a]).start(priority=0)
            row_copy(s, 1, r, dest_ref[a + 1]).start(priority=1)
            return carry
        lax.fori_loop(0, tm, issue, 0, unroll=DMA_UNROLL)

    @pl.when(i == 0)
    def _():
        issue_tile(0, 0)

    @pl.when(i + 1 < nt)
    def _():
        issue_tile(i + 1, 1 - slot)

    def drain(r, carry):
        row_copy(slot, 0, r, 0).wait()
        row_copy(slot, 1, r, 0).wait()
        return carry

    lax.fori_loop(0, tm, drain, 0, unroll=DMA_UNROLL)
    gate = gate_ref[...]
    out = x_ref[...] + g_ref[...] * (gate[:, 0:1] * y_sc[slot, 0] + gate[:, 1:2] * y_sc[slot, 1])
    if final:
        ms = jnp.mean(out * out, axis=-1, keepdims=True)
        out = out * lax.rsqrt(ms + RMS_EPS) * fg_ref[...]
    o_ref[...] = out


def _combine(lay, x, mods, gates, ys, dest, final_g):
    tm = lay.tile(256)
    nt = lay.T // tm
    final = final_g is not None
    fg = (final_g if final else jnp.ones((D,), F32)).reshape(1, D)
    grid_spec = pltpu.PrefetchScalarGridSpec(
        num_scalar_prefetch=1,
        grid=(nt,),
        in_specs=[pl.BlockSpec((tm, D), lambda i, d: (i, 0)),
                  pl.BlockSpec((tm, 2), lambda i, d: (i, 0)),
                  pl.BlockSpec((None, 1, D), lambda i, d: (lay.group(i * tm) * MOD_CHUNKS + 5, 0, 0)),
                  pl.BlockSpec((1, D), lambda i, d: (0, 0)),
                  pl.BlockSpec(memory_space=pl.ANY)],
        out_specs=pl.BlockSpec((tm, D), lambda i, d: (i, 0)),
        scratch_shapes=[pltpu.VMEM((2, 2, tm, D), F32), pltpu.SemaphoreType.DMA((2,))],
    )
    return pl.pallas_call(
        functools.partial(_combine_kernel, tm, nt, final),
        out_shape=jax.ShapeDtypeStruct((lay.T, D), F32),
        grid_spec=grid_spec,
        compiler_params=_cp("arbitrary"),
        name="combine",
    )(dest, x, gates, mods, fg, ys)


def _moe_layer(lay, x, mods, router_w, w_in, w_out, final_g=None, bm=MOE_BM):
    h, idx, gates, rank, counts = _router(lay, x, mods, router_w)
    dest, fill, block_e, nvalid, nb = _moe_plan(idx, rank, counts, bm)
    xs = _dispatch(lay, h, dest, fill, nb, bm)
    ys = _experts(xs, block_e, nvalid, nb, bm, w_in, w_out)
    return _combine(lay, x, mods, gates, ys, dest, final_g)


def kernel(x_prompt, x_sample, c, state_l0_s5_re, state_l0_s5_im, state_l2_gla, state_l3_ret, c_ctx, l0_mod_w, l0_mod_b, l0_s5_a_re, l0_s5_a_im, l0_s5_log_dt, l0_s5_b_re, l0_s5_b_im, l0_s5_c_re, l0_s5_c_im, l0_s5_d, l0_s5_glu_w, l0_ffn_w_in, l0_ffn_w_out, l1_mod_w, l1_mod_b, l1_hy_w_in, l1_hy_b_in, l1_hy_short_w, l1_hy_short_b, l1_hy_f_w1, l1_hy_f_b1, l1_hy_f_w2, l1_hy_f_b2, l1_hy_f_w3, l1_hy_f_freq, l1_hy_skip, l1_hy_w_out, l1_hy_b_out, l1_moe_router, l1_moe_w_in, l1_moe_w_out, l2_mod_w, l2_mod_b, l2_gla_w_in, l2_gla_gate_w1, l2_gla_gate_w2, l2_gla_gate_b, l2_gla_norm_g, l2_gla_w_out, l2_ffn_w_in, l2_ffn_w_out, l3_mod_w, l3_mod_b, l3_ret_w_in, l3_ret_log_decay, l3_ret_norm_g, l3_ret_w_out, l3_moe_router, l3_moe_w_in, l3_moe_w_out, final_norm_g):
    B, L, _ = x_prompt.shape
    NS, LS, _ = x_sample.shape
    lay = Layout(B, L, NS, LS)
    x = jnp.concatenate([x_prompt.reshape(B * L, D), x_sample.reshape(NS * LS, D)], axis=0)
    cond = jnp.concatenate([c_ctx[None], c, jnp.zeros((8 - 1 - NS, D), F32)], axis=0)
    mods0 = _mods(cond, l0_mod_w, l0_mod_b)
    p0 = dict(a_re=l0_s5_a_re, a_im=l0_s5_a_im, log_dt=l0_s5_log_dt, b_re=l0_s5_b_re, b_im=l0_s5_b_im,
              c_re=l0_s5_c_re, c_im=l0_s5_c_im, d=l0_s5_d, glu_w=l0_s5_glu_w.astype(BF16))
    x, s5_re, s5_im = _s5_layer(lay, x, mods0, p0, state_l0_s5_re, state_l0_s5_im)
    x = _ffn(lay, x, mods0, l0_ffn_w_in.astype(BF16), l0_ffn_w_out.astype(BF16))

    mods1 = _mods(cond, l1_mod_w, l1_mod_b)
    p1 = dict(w_in=l1_hy_w_in, b_in=l1_hy_b_in, short_w=l1_hy_short_w, short_b=l1_hy_short_b,
              f_w1=l1_hy_f_w1, f_b1=l1_hy_f_b1, f_w2=l1_hy_f_w2, f_b2=l1_hy_f_b2, f_w3=l1_hy_f_w3,
              f_freq=l1_hy_f_freq, skip=l1_hy_skip, w_out=l1_hy_w_out, b_out=l1_hy_b_out)
    x = _hyena_layer(lay, x, mods1, p1)
    x = _moe_layer(lay, x, mods1, l1_moe_router, l1_moe_w_in.astype(BF16), l1_moe_w_out.astype(BF16))

    mods2 = _mods(cond, l2_mod_w, l2_mod_b)
    p2 = dict(w_in=l2_gla_w_in, gate_w1=l2_gla_gate_w1, gate_w2=l2_gla_gate_w2, gate_b=l2_gla_gate_b,
              norm_g=l2_gla_norm_g, w_out=l2_gla_w_out)
    x, gla_state = _gla_layer(lay, x, mods2, p2, state_l2_gla)
    x = _ffn(lay, x, mods2, l2_ffn_w_in.astype(BF16), l2_ffn_w_out.astype(BF16))

    mods3 = _mods(cond, l3_mod_w, l3_mod_b)
    p3 = dict(w_in=l3_ret_w_in, log_decay=l3_ret_log_decay, norm_g=l3_ret_norm_g, w_out=l3_ret_w_out)
    x, ret_state = _ret_layer(lay, x, mods3, p3, state_l3_ret)
    y = _moe_layer(lay, x, mods3, l3_moe_router, l3_moe_w_in.astype(BF16), l3_moe_w_out.astype(BF16),
                   final_g=final_norm_g)
    return (y[:lay.TP].reshape(B, L, D), y[lay.TP:].reshape(NS, LS, D), s5_re, s5_im, gla_state, ret_state)
```
